```python
import numpy as np
import jax
import jax.numpy as jnp
from jax import lax

D_MODEL = 1024
BATCH = 4
SEQ = 4096
DEPTH = 2

MIX = D_MODEL // 2
GM_CHUNK = 128
GM_GROUPS = 4
GM_GW = MIX // GM_GROUPS
CONV_WIDTH = 4
LRU_BLOCKS = 8
LRU_BW = MIX // LRU_BLOCKS
LRU_C = 8.0
N_HEADS = 8
HEAD_DIM = MIX // N_HEADS
N_KV = 2
HPG = N_HEADS // N_KV
CMP_LEN = 32
CMP_STRIDE = 16
SLC_LEN = 64
SLC_TOPN = 8
WIN = 512
Q_BLOCK = 128
NSA_Q = N_HEADS * HEAD_DIM
NSA_KV = N_KV * HEAD_DIM
POOL_WINDOWS = (2, 4, 8, 16)
POOL_GROUPS = len(POOL_WINDOWS)
POOL_GW = MIX // POOL_GROUPS
N_BRANCH = 4
N_GROUPS = 4
EXPERTS_PER_GROUP = 4
N_EXPERTS = N_GROUPS * EXPERTS_PER_GROUP
EXPERT_TOPK = 2
D_EXPERT = D_MODEL // 2

EPS = 1e-6
NEG_INF = -1e30
FORCE_SCORE = 1e6

IN_SPLITS = (MIX, MIX, MIX, MIX, NSA_Q, 6 * NSA_KV, 3 * N_HEADS, MIX, N_BRANCH * D_MODEL)
N_IN = sum(IN_SPLITS)

kernel_name = 'hybrid_gated_mixers_hier_moe'


def rms_norm(x, g):
    xf = x.astype(jnp.float32)
    y = xf * lax.rsqrt(jnp.mean(xf * xf, axis=-1, keepdims=True) + EPS)
    return (y * g.astype(jnp.float32)).astype(x.dtype)


def gmlp_mixer(u, v, v_norm_g, ws, bs):
    B, S, _ = u.shape
    nc = S // GM_CHUNK
    u = jax.nn.gelu(u)
    v = rms_norm(jax.nn.gelu(v), v_norm_g).reshape(B, nc, GM_CHUNK, GM_GROUPS, GM_GW)
    causal = jnp.tril(jnp.ones((GM_CHUNK, GM_CHUNK), dtype=bool))
    w = jnp.where(causal[None], ws, 0.0)
    mixed = jnp.einsum('gts,bnsgc->bntgc', w, v) + bs.T[:, :, None]
    return u * mixed.reshape(B, S, MIX)


def rglru_mixer(gate_in, x_in, conv_w, conv_b, wa, ba, wx, bx, lam):
    B, S, C = x_in.shape
    f32 = jnp.float32
    xp = jnp.pad(x_in, ((0, 0), (CONV_WIDTH - 1, 0), (0, 0)))
    xc = conv_b + sum(xp[:, k:k + S] * conv_w[k] for k in range(CONV_WIDTH))
    xb = xc.reshape(B, S, LRU_BLOCKS, LRU_BW)
    r = jax.nn.sigmoid(jnp.einsum('bshi,hij->bshj', xb, wa).reshape(B, S, C) + ba)
    i = jax.nn.sigmoid(jnp.einsum('bshi,hij->bshj', xb, wx).reshape(B, S, C) + bx)
    log_a = -LRU_C * r.astype(f32) * jax.nn.softplus(-lam.astype(f32))
    a = jnp.exp(log_a)
    b = jnp.sqrt(-jnp.expm1(2.0 * log_a)) * (i * xc).astype(f32)

    def combine(c1, c2):
        a1, b1 = c1
        a2, b2 = c2
        return a1 * a2, a2 * b1 + b2

    _, h = lax.associative_scan(combine, (a, b), axis=1)
    return jax.nn.gelu(gate_in) * h.astype(x_in.dtype)


def _gather_blocks(blocks, idx):
    return blocks[idx].reshape(idx.shape[0], -1, blocks.shape[-1])


_gather_bg = jax.vmap(jax.vmap(_gather_blocks))


def nsa_mixer(q, kv, gate_logits, cmp_pe, cmp_w1, cmp_w2):
    B, S, _ = q.shape
    f32 = jnp.float32
    scale = HEAD_DIM ** -0.5
    t_pos = jnp.arange(S)
    qh = q.reshape(B, S, N_KV, HPG, HEAD_DIM).transpose(0, 2, 3, 1, 4)
    kv = kv.reshape(B, S, 6, N_KV, HEAD_DIM).transpose(2, 0, 3, 1, 4)

    n_sub = CMP_LEN // CMP_STRIDE
    n_chunk = S // CMP_STRIDE
    n_cmp = n_chunk - n_sub + 1
    chunks = kv[0:2].reshape(2, B, N_KV, n_chunk, CMP_STRIDE, HEAD_DIM)
    blocks = jnp.concatenate([chunks[:, :, :, j:j + n_cmp] for j in range(n_sub)], axis=4)
    blocks = blocks + cmp_pe[:, None, None, None]
    flat = blocks.reshape(2, B, N_KV, n_cmp, CMP_LEN * HEAD_DIM)
    hid = jax.nn.gelu(jnp.einsum('cbgnf,cgfe->cbgne', flat, cmp_w1))
    kvc = jnp.einsum('cbgne,cged->cbgnd', hid, cmp_w2)
    k_c, v_c = kvc[0], kvc[1]
    cmp_end = jnp.arange(n_cmp) * CMP_STRIDE + (CMP_LEN - 1)
    cmp_mask = cmp_end[None, :] <= t_pos[:, None]
    s_c = jnp.einsum('bgjsd,bgnd->bgjsn', qh, k_c).astype(f32) * scale
    p_c = jax.nn.softmax(jnp.where(cmp_mask, s_c, NEG_INF), axis=-1)
    p_c = jnp.where(jnp.any(cmp_mask, axis=-1)[:, None], p_c, 0.0)
    o_cmp = jnp.einsum('bgjsn,bgnd->bgjsd', p_c.astype(q.dtype), v_c)

    n_slc = S // SLC_LEN
    topn = min(SLC_TOPN, n_slc)
    c_start = np.arange(n_cmp) * CMP_STRIDE
    s_start = np.arange(n_slc) * SLC_LEN
    overlap = ((c_start[:, None] <= s_start[None, :] + SLC_LEN - 1)
               & (c_start[:, None] + CMP_LEN - 1 >= s_start[None, :])).astype(np.float32)
    imp = jnp.einsum('bgjsn,nm->bgsm', p_c, jnp.asarray(overlap))
    blk = jnp.arange(n_slc)
    cur = t_pos // SLC_LEN
    causal_blk = blk[None, :] * SLC_LEN <= t_pos[:, None]
    forced = (blk[None, :] == 0) | (blk[None, :] == cur[:, None]) | (blk[None, :] == cur[:, None] - 1)
    score = jnp.where(forced, FORCE_SCORE, jnp.where(causal_blk, imp, -1.0))
    top_val, top_idx = lax.top_k(score, topn)
    top_ok = top_val >= 0.0

    nq = S // Q_BLOCK
    ks_blk = kv[2].reshape(B, N_KV, n_slc, SLC_LEN, HEAD_DIM)
    vs_blk = kv[3].reshape(B, N_KV, n_slc, SLC_LEN, HEAD_DIM)

    def slc_block(args):
        q_b, idx_b, ok_b, t_b = args
        k_sel = _gather_bg(ks_blk, idx_b)
        v_sel = _gather_bg(vs_blk, idx_b)
        k_pos = idx_b[..., None] * SLC_LEN + jnp.arange(SLC_LEN)
        ok = (ok_b[..., None] & (k_pos <= t_b[:, None, None])).reshape(B, N_KV, Q_BLOCK, -1)
        s = jnp.einsum('bgjqd,bgqkd->bgjqk', q_b, k_sel).astype(f32) * scale
        p = jax.nn.softmax(jnp.where(ok[:, :, None], s, NEG_INF), axis=-1)
        return jnp.einsum('bgjqk,bgqkd->bgjqd', p.astype(q_b.dtype), v_sel)

    q_blocks = qh.reshape(B, N_KV, HPG, nq, Q_BLOCK, HEAD_DIM).transpose(3, 0, 1, 2, 4, 5)
    idx_blocks = top_idx.reshape(B, N_KV, nq, Q_BLOCK, topn).transpose(2, 0, 1, 3, 4)
    ok_blocks = top_ok.reshape(B, N_KV, nq, Q_BLOCK, topn).transpose(2, 0, 1, 3, 4)
    o_slc = lax.map(slc_block, (q_blocks, idx_blocks, ok_blocks, t_pos.reshape(nq, Q_BLOCK)))
    o_slc = o_slc.transpose(1, 2, 3, 0, 4, 5).reshape(B, N_KV, HPG, S, HEAD_DIM)

    n_pad = WIN // Q_BLOCK
    band_len = Q_BLOCK * (n_pad + 1)
    kw = jnp.pad(kv[4], ((0, 0), (0, 0), (WIN, 0), (0, 0))).reshape(B, N_KV, nq + n_pad, Q_BLOCK, HEAD_DIM)
    vw = jnp.pad(kv[5], ((0, 0), (0, 0), (WIN, 0), (0, 0))).reshape(B, N_KV, nq + n_pad, Q_BLOCK, HEAD_DIM)
    band_k = jnp.concatenate([kw[:, :, j:j + nq] for j in range(n_pad + 1)], axis=3)
    band_v = jnp.concatenate([vw[:, :, j:j + nq] for j in range(n_pad + 1)], axis=3)
    qb = qh.reshape(B, N_KV, HPG, nq, Q_BLOCK, HEAD_DIM)
    s_w = jnp.einsum('bgjiqd,bgikd->bgjiqk', qb, band_k).astype(f32) * scale
    q_idx = jnp.arange(Q_BLOCK)
    k_idx = jnp.arange(band_len)
    delta = WIN + q_idx[:, None] - k_idx[None, :]
    k_abs = (jnp.arange(nq)[:, None] - n_pad) * Q_BLOCK + k_idx[None, :]
    win_mask = ((delta >= 0) & (delta < WIN))[None] & (k_abs >= 0)[:, None, :]
    p_w = jax.nn.softmax(jnp.where(win_mask, s_w, NEG_INF), axis=-1)
    o_win = jnp.einsum('bgjiqk,bgikd->bgjiqd', p_w.astype(q.dtype), band_v).reshape(B, N_KV, HPG, S, HEAD_DIM)

    g = jax.nn.sigmoid(gate_logits.reshape(B, S, N_KV, HPG, 3).transpose(4, 0, 2, 3, 1))[..., None]
    o = g[0] * o_cmp + g[1] * o_slc + g[2] * o_win
    return o.transpose(0, 3, 1, 2, 4).reshape(B, S, NSA_Q)


def pool_mixer(xd, pool_w, pool_scale):
    B, S, C = xd.shape
    f32 = jnp.float32
    xf = xd.astype(f32)
    csum = jnp.pad(lax.cumsum(xf, axis=1), ((0, 0), (1, 0), (0, 0)))
    t = jnp.arange(S)
    outs = []
    for gi, w in enumerate(POOL_WINDOWS):
        c = csum[:, :, gi * POOL_GW:(gi + 1) * POOL_GW]
        lower = jnp.pad(c[:, :S + 1 - w], ((0, 0), (w - 1, 0), (0, 0)))
        cnt = jnp.minimum(t + 1, w).astype(f32)[None, :, None]
        outs.append((c[:, 1:] - lower) / cnt - xf[:, :, gi * POOL_GW:(gi + 1) * POOL_GW])
    pooled = jnp.stack(outs, axis=2).astype(xd.dtype)
    mixed = jnp.einsum('bsgi,gij->bsgj', pooled, pool_w).reshape(B, S, C)
    return mixed * pool_scale


def hybrid_mixer(h, w_in, gm_norm_g, gm_ws, gm_b, conv_w, conv_b, lru_wa, lru_ba, lru_wx,
                 lru_bx, lru_lambda, cmp_pe, cmp_w1, cmp_w2, pool_w, pool_scale, w_branch, w_out):
    B, S, _ = h.shape
    proj = h @ w_in
    cuts = np.cumsum(IN_SPLITS)[:-1].tolist()
    u, v, gate_b, rec_b, q, kv, nsa_g, xd, mg = jnp.split(proj, cuts, axis=-1)
    y_a = gmlp_mixer(u, v, gm_norm_g, gm_ws, gm_b)
    y_b = rglru_mixer(gate_b, rec_b, conv_w, conv_b, lru_wa, lru_ba, lru_wx, lru_bx, lru_lambda)
    y_c = nsa_mixer(q, kv, nsa_g, cmp_pe, cmp_w1, cmp_w2)
    y_d = pool_mixer(xd, pool_w, pool_scale)
    ys = jnp.stack([y_a, y_b, y_c, y_d])
    branch = jnp.einsum('nbsc,ncd->bsnd', ys, w_branch)
    gates = jax.nn.sigmoid(mg.reshape(B, S, N_BRANCH, D_MODEL))
    merged = jnp.sum(gates * branch, axis=2)
    return merged @ w_out


def hier_moe(h, wr_g, br_g, wr_e, br_e, w_gate, w_up, w_down):
    B, S, D = h.shape
    f32 = jnp.float32
    t = h.reshape(-1, D)
    T = t.shape[0]
    grp_p = jax.nn.softmax((t @ wr_g + br_g).astype(f32), axis=-1)
    grp_w, grp_idx = lax.top_k(grp_p, 1)
    exp_logits = (t @ wr_e + br_e).astype(f32).reshape(T, N_GROUPS, EXPERTS_PER_GROUP)
    in_grp = jnp.einsum('tge,tg->te', exp_logits, jax.nn.one_hot(grp_idx[:, 0], N_GROUPS, dtype=f32))
    top_l, top_e = lax.top_k(in_grp, EXPERT_TOPK)
    top_w = jax.nn.softmax(top_l, axis=-1) * grp_w
    expert_id = grp_idx * EXPERTS_PER_GROUP + top_e
    combine = jnp.sum(jax.nn.one_hot(expert_id, N_EXPERTS, dtype=f32) * top_w[..., None], axis=1)
    hid = jax.nn.silu(jnp.einsum('td,edf->etf', t, w_gate)) * jnp.einsum('td,edf->etf', t, w_up)
    hid = hid * combine.T[:, :, None].astype(hid.dtype)
    out = jnp.einsum('etf,efd->td', hid, w_down)
    return out.reshape(B, S, D)


def setup_inputs(seed: int = 0) -> dict:
    key = jax.random.key(seed)
    ks = iter(jax.random.split(key, 40))
    f32 = jnp.float32
    L = DEPTH

    def nrm(shape, scale):
        return jax.random.normal(next(ks), shape, f32) * scale

    def gain(shape):
        return 1.0 + 0.02 * jax.random.normal(next(ks), shape, f32)

    a_c = jax.random.uniform(next(ks), (L, MIX), f32, minval=0.9, maxval=0.999)
    a0 = a_c ** (1.0 / LRU_C)
    lru_lambda = jnp.log(a0) - jnp.log1p(-a0)
    return {
        'x': nrm((BATCH, SEQ, D_MODEL), 1.0),
        'norm1_g': gain((L, D_MODEL)),
        'w_in': nrm((L, D_MODEL, N_IN), D_MODEL ** -0.5),
        'gm_norm_g': gain((L, MIX)),
        'gm_ws': nrm((L, GM_GROUPS, GM_CHUNK, GM_CHUNK), GM_CHUNK ** -0.5),
        'gm_b': gain((L, GM_GROUPS, GM_CHUNK)),
        'conv_w': nrm((L, CONV_WIDTH, MIX), CONV_WIDTH ** -0.5),
        'conv_b': nrm((L, MIX), 0.02),
        'lru_wa': nrm((L, LRU_BLOCKS, LRU_BW, LRU_BW), LRU_BW ** -0.5),
        'lru_ba': nrm((L, MIX), 0.02),
        'lru_wx': nrm((L, LRU_BLOCKS, LRU_BW, LRU_BW), LRU_BW ** -0.5),
        'lru_bx': nrm((L, MIX), 0.02),
        'lru_lambda': lru_lambda,
        'cmp_pe': nrm((L, 2, CMP_LEN, HEAD_DIM), 0.02),
        'cmp_w1': nrm((L, 2, N_KV, CMP_LEN * HEAD_DIM, HEAD_DIM), (CMP_LEN * HEAD_DIM) ** -0.5),
        'cmp_w2': nrm((L, 2, N_KV, HEAD_DIM, HEAD_DIM), HEAD_DIM ** -0.5),
        'pool_w': nrm((L, POOL_GROUPS, POOL_GW, POOL_GW), POOL_GW ** -0.5),
        'pool_scale': gain((L, MIX)),
        'w_branch': nrm((L, N_BRANCH, MIX, D_MODEL), MIX ** -0.5),
        'w_out': nrm((L, D_MODEL, D_MODEL), D_MODEL ** -0.5),
        'norm2_g': gain((L, D_MODEL)),
        'router_w_group': nrm((L, D_MODEL, N_GROUPS), D_MODEL ** -0.5),
        'router_b_group': nrm((L, N_GROUPS), 0.01),
        'router_w_expert': nrm((L, D_MODEL, N_EXPERTS), D_MODEL ** -0.5),
        'router_b_expert': nrm((L, N_EXPERTS), 0.01),
        'moe_w_gate': nrm((L, N_EXPERTS, D_MODEL, D_EXPERT), D_MODEL ** -0.5),
        'moe_w_up': nrm((L, N_EXPERTS, D_MODEL, D_EXPERT), D_MODEL ** -0.5),
        'moe_w_down': nrm((L, N_EXPERTS, D_EXPERT, D_MODEL), D_EXPERT ** -0.5),
        'final_norm_g': gain((D_MODEL,)),
    }


def reference(x, norm1_g, w_in, gm_norm_g, gm_ws, gm_b, conv_w, conv_b, lru_wa, lru_ba, lru_wx,
              lru_bx, lru_lambda, cmp_pe, cmp_w1, cmp_w2, pool_w, pool_scale, w_branch, w_out,
              norm2_g, router_w_group, router_b_group, router_w_expert, router_b_expert,
              moe_w_gate, moe_w_up, moe_w_down, final_norm_g):
    h = x
    for l in range(DEPTH):
        h = h + hybrid_mixer(rms_norm(h, norm1_g[l]), w_in[l], gm_norm_g[l], gm_ws[l], gm_b[l],
                             conv_w[l], conv_b[l], lru_wa[l], lru_ba[l], lru_wx[l], lru_bx[l],
                             lru_lambda[l], cmp_pe[l], cmp_w1[l], cmp_w2[l], pool_w[l],
                             pool_scale[l], w_branch[l], w_out[l])
        h = h + hier_moe(rms_norm(h, norm2_g[l]), router_w_group[l], router_b_group[l],
                         router_w_expert[l], router_b_expert[l], moe_w_gate[l], moe_w_up[l],
                         moe_w_down[l])
    return rms_norm(h, final_norm_g)
```

```python
import functools

import numpy as np
import jax
import jax.numpy as jnp
from jax import lax
from jax.experimental import pallas as pl
from jax.experimental.pallas import tpu as pltpu

F32 = jnp.float32
BF16 = jnp.bfloat16

D_MODEL = 1024
BATCH = 4
SEQ = 4096
TOKENS = BATCH * SEQ
DEPTH = 2
MIX = D_MODEL // 2
GM_CHUNK = 128
GM_GROUPS = 4
GM_GW = MIX // GM_GROUPS
CONV_WIDTH = 4
LRU_BLOCKS = 8
LRU_BW = MIX // LRU_BLOCKS
LRU_C = 8.0
N_HEADS = 8
HEAD_DIM = MIX // N_HEADS
N_KV = 2
HPG = N_HEADS // N_KV
CMP_LEN = 32
CMP_STRIDE = 16
SLC_LEN = 64
SLC_TOPN = 8
WIN = 512
NSA_Q = N_HEADS * HEAD_DIM
NSA_KV = N_KV * HEAD_DIM
POOL_WINDOWS = (2, 4, 8, 16)
POOL_GW = MIX // len(POOL_WINDOWS)
N_BRANCH = 4
N_GROUPS = 4
EXPERTS_PER_GROUP = 4
N_EXPERTS = N_GROUPS * EXPERTS_PER_GROUP
D_EXPERT = D_MODEL // 2
EPS = 1e-6
NEG_INF = -1e30
FORCE_SCORE = 1e6
IN_SPLITS = (MIX, MIX, MIX, MIX, NSA_Q, 6 * NSA_KV, 3 * N_HEADS, MIX, N_BRANCH * D_MODEL)

N_CHUNK = SEQ // CMP_STRIDE
N_CMP = N_CHUNK - CMP_LEN // CMP_STRIDE + 1
N_SLC = SEQ // SLC_LEN

LANES = 128
GATE_PAD = LANES
GATE_ROWS = 16
ROUTER_PAD = LANES
VMEM_LIMIT = 56 * 1024 * 1024

PROJ_WIDTHS = (MIX, MIX, MIX, MIX, NSA_Q, 6 * NSA_KV, MIX, GATE_PAD)
PROJ_TM = 512
GMLP_TM = 512
LRU_TS = 512
POOL_TS = 512
NSA_TQ = 128
SLC_KT = 512
WIN_KEYS = WIN + NSA_TQ
MERGE_TM = 256
MOE_TM = 512


def _cparams(sem):
    return pltpu.CompilerParams(dimension_semantics=sem, vmem_limit_bytes=VMEM_LIMIT)


def _rms(x, g):
    return x * lax.rsqrt(jnp.mean(x * x, axis=-1, keepdims=True) + EPS) * g


def _proj_kernel(x_ref, g_ref, w_ref, *out_refs):
    nb = _rms(x_ref[...], g_ref[...]).astype(BF16)
    off = 0
    for ref in out_refs:
        w = ref.shape[-1]
        ref[...] = jnp.dot(nb, w_ref[:, off:off + w], preferred_element_type=F32).astype(ref.dtype)
        off += w


def _proj(x, g, w):
    n_in = sum(PROJ_WIDTHS)
    out_shape = [jax.ShapeDtypeStruct((TOKENS, wd), BF16) for wd in PROJ_WIDTHS[:-1]]
    out_shape.append(jax.ShapeDtypeStruct((TOKENS, GATE_PAD), F32))
    return pl.pallas_call(
        _proj_kernel,
        grid=(TOKENS // PROJ_TM,),
        in_specs=[pl.BlockSpec((PROJ_TM, D_MODEL), lambda i: (i, 0)),
                  pl.BlockSpec((1, D_MODEL), lambda i: (0, 0)),
                  pl.BlockSpec((D_MODEL, n_in), lambda i: (0, 0))],
        out_specs=[pl.BlockSpec((PROJ_TM, wd), lambda i: (i, 0)) for wd in PROJ_WIDTHS],
        out_shape=out_shape,
        compiler_params=_cparams(("arbitrary",)),
        name="proj",
    )(x, g, w)


def _gmlp_kernel(u_ref, v_ref, g_ref, ws_ref, bs_ref, o_ref):
    u = jax.nn.gelu(u_ref[...].astype(F32))
    v = _rms(jax.nn.gelu(v_ref[...].astype(F32)), g_ref[...]).astype(BF16)
    row = lax.broadcasted_iota(jnp.int32, (GM_CHUNK, GM_CHUNK), 0)
    col = lax.broadcasted_iota(jnp.int32, (GM_CHUNK, GM_CHUNK), 1)
    causal = row >= col
    for gi in range(GM_GROUPS):
        w = jnp.where(causal, ws_ref[gi], 0.0).astype(BF16)
        cs = slice(gi * GM_GW, (gi + 1) * GM_GW)
        for c in range(GMLP_TM // GM_CHUNK):
            rs = slice(c * GM_CHUNK, (c + 1) * GM_CHUNK)
            mixed = jnp.dot(w, v[rs, cs], preferred_element_type=F32) + bs_ref[gi]
            o_ref[rs, cs] = (u[rs, cs] * mixed).astype(o_ref.dtype)


def _gmlp(u, v, g, ws, bs):
    tok = lambda i: (i, 0)
    return pl.pallas_call(
        _gmlp_kernel,
        grid=(TOKENS // GMLP_TM,),
        in_specs=[pl.BlockSpec((GMLP_TM, MIX), tok),
                  pl.BlockSpec((GMLP_TM, MIX), tok),
                  pl.BlockSpec((1, MIX), lambda i: (0, 0)),
                  pl.BlockSpec((GM_GROUPS, GM_CHUNK, GM_CHUNK), lambda i: (0, 0, 0)),
                  pl.BlockSpec((GM_GROUPS, GM_CHUNK, GM_GW), lambda i: (0, 0, 0))],
        out_specs=pl.BlockSpec((GMLP_TM, MIX), tok),
        out_shape=jax.ShapeDtypeStruct((TOKENS, MIX), BF16),
        compiler_params=_cparams(("arbitrary",)),
        name="gmlp",
    )(u, v, g, ws, bs)


LRU_TAIL = 8


def _lru_kernel(gb_ref, rb_ref, cw_ref, cb_ref, wa_ref, ba_ref, wx_ref, bx_ref, lam_ref, o_ref,
                tail_ref, h_ref):
    @pl.when(pl.program_id(1) == 0)
    def _():
        tail_ref[...] = jnp.zeros_like(tail_ref)
        h_ref[...] = jnp.zeros_like(h_ref)

    ts = LRU_TS
    x = rb_ref[...].astype(F32)
    ext = jnp.concatenate([tail_ref[...], x], axis=0)
    tail_ref[...] = x[ts - LRU_TAIL:, :]
    xc = cb_ref[...] + x * cw_ref[CONV_WIDTH - 1:CONV_WIDTH, :]
    for d in range(1, CONV_WIDTH):
        xs = pltpu.roll(ext, d, axis=0)[LRU_TAIL:, :]
        xc = xc + xs * cw_ref[CONV_WIDTH - 1 - d:CONV_WIDTH - d, :]
    xcb = xc.astype(BF16)
    r = jax.nn.sigmoid(jnp.dot(xcb, wa_ref[...], preferred_element_type=F32) + ba_ref[...])
    ig = jax.nn.sigmoid(jnp.dot(xcb, wx_ref[...], preferred_element_type=F32) + bx_ref[...])
    z = -lam_ref[...]
    softplus = jnp.maximum(z, 0.0) + jnp.log1p(jnp.exp(-jnp.abs(z)))
    log_a = -LRU_C * r * softplus
    a = jnp.exp(log_a)
    b = jnp.sqrt(1.0 - jnp.exp(2.0 * log_a)) * (ig * xc)
    rows = lax.broadcasted_iota(jnp.int32, (ts, 1), 0)
    d = 1
    while d < ts:
        valid = rows >= d
        a_prev = pltpu.roll(a, d, axis=0)
        b_prev = pltpu.roll(b, d, axis=0)
        b = jnp.where(valid, a * b_prev, 0.0) + b
        a = jnp.where(valid, a * a_prev, a)
        d *= 2
    h = a * h_ref[...] + b
    h_ref[...] = h[ts - 1:ts, :]
    o_ref[...] = (jax.nn.gelu(gb_ref[...].astype(F32)) * h).astype(o_ref.dtype)


def _lru(gb, rb, cw, cb, wa, ba, wx, bx, lam):
    nt = SEQ // LRU_TS
    tok = lambda b, s: (b * nt + s, 0)
    vec = pl.BlockSpec((1, MIX), lambda b, s: (0, 0))
    mat = pl.BlockSpec((MIX, MIX), lambda b, s: (0, 0))
    return pl.pallas_call(
        _lru_kernel,
        grid=(BATCH, nt),
        in_specs=[pl.BlockSpec((LRU_TS, MIX), tok), pl.BlockSpec((LRU_TS, MIX), tok),
                  pl.BlockSpec((CONV_WIDTH, MIX), lambda b, s: (0, 0)), vec,
                  mat, vec, mat, vec, vec],
        out_specs=pl.BlockSpec((LRU_TS, MIX), tok),
        out_shape=jax.ShapeDtypeStruct((TOKENS, MIX), BF16),
        scratch_shapes=[pltpu.VMEM((LRU_TAIL, MIX), F32), pltpu.VMEM((1, MIX), F32)],
        compiler_params=_cparams(("arbitrary", "arbitrary")),
        name="rglru",
    )(gb, rb, cw, cb, wa, ba, wx, bx, lam)


POOL_TAIL = 16


def _pool_kernel(x_ref, w_ref, sc_ref, o_ref, tail_ref):
    s_id = pl.program_id(1)

    @pl.when(s_id == 0)
    def _():
        tail_ref[...] = jnp.zeros_like(tail_ref)

    ts = POOL_TS
    x = x_ref[...].astype(F32)
    ext = jnp.concatenate([tail_ref[...], x], axis=0)
    tail_ref[...] = x[ts - POOL_TAIL:, :]
    pos = s_id * ts + lax.broadcasted_iota(jnp.int32, (ts, 1), 0)
    acc = ext
    width = 1
    for gi, wdw in enumerate(POOL_WINDOWS):
        while width < wdw:
            acc = acc + pltpu.roll(acc, width, axis=0)
            width *= 2
        cs = slice(gi * POOL_GW, (gi + 1) * POOL_GW)
        cnt = jnp.minimum(pos + 1, wdw).astype(F32)
        pooled = acc[POOL_TAIL:, cs] / cnt - x[:, cs]
        mixed = jnp.dot(pooled.astype(BF16), w_ref[gi], preferred_element_type=F32)
        o_ref[:, cs] = (mixed * sc_ref[:, cs]).astype(o_ref.dtype)


def _pool(xd, w, sc):
    nt = SEQ // POOL_TS
    tok = lambda b, s: (b * nt + s, 0)
    return pl.pallas_call(
        _pool_kernel,
        grid=(BATCH, nt),
        in_specs=[pl.BlockSpec((POOL_TS, MIX), tok),
                  pl.BlockSpec((len(POOL_WINDOWS), POOL_GW, POOL_GW), lambda b, s: (0, 0, 0)),
                  pl.BlockSpec((1, MIX), lambda b, s: (0, 0))],
        out_specs=pl.BlockSpec((POOL_TS, MIX), tok),
        out_shape=jax.ShapeDtypeStruct((TOKENS, MIX), BF16),
        scratch_shapes=[pltpu.VMEM((POOL_TAIL, MIX), F32)],
        compiler_params=_cparams(("arbitrary", "arbitrary")),
        name="pool",
    )(xd, w, sc)


HALF_FLAT = CMP_STRIDE * HEAD_DIM


def _cmpkv_kernel(x_ref, pe_ref, w1_ref, w2_ref, o_ref):
    x = x_ref[0, 0, 0].astype(F32)
    pe = pe_ref[0]
    w1 = w1_ref[0, 0]
    first = jnp.dot((x + pe[0:1, :]).astype(BF16), w1[:HALF_FLAT, :], preferred_element_type=F32)
    second = jnp.dot((x + pe[1:2, :]).astype(BF16), w1[HALF_FLAT:, :], preferred_element_type=F32)
    hid = jax.nn.gelu(first + pltpu.roll(second, N_CHUNK - 1, axis=0))
    out = jnp.dot(hid.astype(BF16), w2_ref[0, 0], preferred_element_type=F32)
    row = lax.broadcasted_iota(jnp.int32, (N_CHUNK, 1), 0)
    o_ref[0, 0, 0] = jnp.where(row < N_CMP, out, 0.0).astype(o_ref.dtype)


def _cmpkv(x, pe, w1, w2):
    return pl.pallas_call(
        _cmpkv_kernel,
        grid=(2, BATCH, N_KV),
        in_specs=[pl.BlockSpec((1, 1, 1, N_CHUNK, HALF_FLAT), lambda c, b, g: (c, b, g, 0, 0)),
                  pl.BlockSpec((1, 2, HALF_FLAT), lambda c, b, g: (c, 0, 0)),
                  pl.BlockSpec((1, 1, CMP_LEN * HEAD_DIM, HEAD_DIM), lambda c, b, g: (c, g, 0, 0)),
                  pl.BlockSpec((1, 1, HEAD_DIM, HEAD_DIM), lambda c, b, g: (c, g, 0, 0))],
        out_specs=pl.BlockSpec((1, 1, 1, N_CHUNK, HEAD_DIM), lambda c, b, g: (c, b, g, 0, 0)),
        out_shape=jax.ShapeDtypeStruct((2, BATCH, N_KV, N_CHUNK, HEAD_DIM), BF16),
        compiler_params=_cparams(("arbitrary", "arbitrary", "arbitrary")),
        name="cmpkv",
    )(x, pe, w1, w2)


def _lane_tile(x, n):
    return jnp.concatenate([x] * n, axis=1)


def _nsa_kernel(q_ref, kc_ref, vct_ref, ks_ref, vst_ref, kw_ref, vwt_ref, gt_ref, ovt_ref, y_ref,
                sel_ref):
    g = pl.program_id(1)
    i = pl.program_id(2)
    tq = NSA_TQ
    nl = HPG * tq
    t_row = i * tq + lax.broadcasted_iota(jnp.int32, (1, tq), 1)
    row0 = pl.multiple_of(g * HEAD_DIM, HEAD_DIM)

    lane = lax.broadcasted_iota(jnp.int32, (tq, LANES), 1)
    keep = (lane >= HEAD_DIM) == (g == 1)
    scale = HEAD_DIM ** -0.5
    q4 = jnp.concatenate(
        [jnp.where(keep, q_ref[:, j * LANES:(j + 1) * LANES], 0) * scale for j in range(HPG)],
        axis=0).astype(BF16)

    def scores(k):
        return lax.dot_general(k, q4, (((1,), (1,)), ((), ())), preferred_element_type=F32)

    s = scores(kc_ref[0])
    n_idx = lax.broadcasted_iota(jnp.int32, (N_CHUNK, 1), 0)
    ok = n_idx * CMP_STRIDE + (CMP_LEN - 1) <= _lane_tile(t_row, HPG)
    s = jnp.where(ok, s, NEG_INF)
    m = jnp.max(s, axis=0, keepdims=True)
    e = jnp.where(ok, jnp.exp(s - m), 0.0)
    l = jnp.sum(e, axis=0, keepdims=True)
    p = e / jnp.where(l > 0.0, l, 1.0)
    o_cmp = jnp.dot(vct_ref[0, pl.ds(row0, HEAD_DIM), :], p.astype(BF16),
                    preferred_element_type=F32)
    psum = p[:, 0:tq]
    for j in range(1, HPG):
        psum = psum + p[:, j * tq:(j + 1) * tq]
    p_hi = psum.astype(BF16)
    p_lo = (psum - p_hi.astype(F32)).astype(BF16)
    imp = (jnp.dot(ovt_ref[...], p_hi, preferred_element_type=F32)
           + jnp.dot(ovt_ref[...], p_lo, preferred_element_type=F32))

    blk = lax.broadcasted_iota(jnp.int32, (N_SLC, 1), 0)
    cur = jnp.right_shift(t_row, 6)
    forced = (blk == 0) | (blk == cur) | (blk == cur - 1)
    causal_blk = blk * SLC_LEN <= t_row
    work = jnp.where(forced, FORCE_SCORE, jnp.where(causal_blk, imp, -1.0))
    sel = jnp.zeros((N_SLC, tq), F32)
    for _ in range(SLC_TOPN):
        mx = jnp.max(work, axis=0, keepdims=True)
        idx = jnp.min(jnp.where(work == mx, blk, N_SLC), axis=0, keepdims=True)
        pick = blk == idx
        sel = jnp.where(pick & (mx >= 0.0), 1.0, sel)
        work = jnp.where(pick, -3e38, work)
    sel_ref[...] = sel

    def slc_step(kt, carry):
        m_i, l_i, acc = carry
        k0 = pl.multiple_of(kt * SLC_KT, SLC_KT)
        sT = scores(ks_ref[pl.ds(k0, SLC_KT), :])
        sel8 = sel_ref[pl.ds(pl.multiple_of(kt * (SLC_KT // SLC_LEN), 8), SLC_KT // SLC_LEN), :]
        selx = jnp.concatenate(
            [jnp.broadcast_to(sel8[r:r + 1, :], (SLC_LEN, tq)) for r in range(SLC_KT // SLC_LEN)],
            axis=0)
        kpos = k0 + lax.broadcasted_iota(jnp.int32, (SLC_KT, 1), 0)
        bias = jnp.where((selx > 0.0) & (kpos <= t_row), 0.0, NEG_INF)
        sT = sT + _lane_tile(bias, HPG)
        m_new = jnp.maximum(m_i, jnp.max(sT, axis=0, keepdims=True))
        alpha = jnp.exp(m_i - m_new)
        pT = jnp.exp(sT - m_new)
        l_new = alpha * l_i + jnp.sum(pT, axis=0, keepdims=True)
        vT = vst_ref[pl.ds(row0, HEAD_DIM), pl.ds(k0, SLC_KT)]
        acc = alpha * acc + jnp.dot(vT, pT.astype(BF16), preferred_element_type=F32)
        return m_new, l_new, acc

    n_kt = jnp.right_shift(i, 2) + 1
    init = (jnp.full((1, nl), -3e38, F32), jnp.zeros((1, nl), F32), jnp.zeros((HEAD_DIM, nl), F32))
    _, l_s, acc_s = lax.fori_loop(0, n_kt, slc_step, init)
    o_slc = acc_s / l_s

    start = pl.multiple_of(jnp.maximum(i - WIN // tq, 0) * tq, tq)
    sT = scores(kw_ref[pl.ds(start, WIN_KEYS), :])
    delta = t_row - (start + lax.broadcasted_iota(jnp.int32, (WIN_KEYS, 1), 0))
    bias = jnp.where((delta >= 0) & (delta < WIN), 0.0, NEG_INF)
    sT = sT + _lane_tile(bias, HPG)
    pT = jnp.exp(sT - jnp.max(sT, axis=0, keepdims=True))
    l_w = jnp.sum(pT, axis=0, keepdims=True)
    vT = vwt_ref[pl.ds(row0, HEAD_DIM), pl.ds(start, WIN_KEYS)]
    o_win = jnp.dot(vT, pT.astype(BF16), preferred_element_type=F32) / l_w

    gates = jax.nn.sigmoid(gt_ref[0, 0])

    def gate_row(br):
        return jnp.concatenate([gates[3 * j + br:3 * j + br + 1, :] for j in range(HPG)], axis=1)

    o = gate_row(0) * o_cmp + gate_row(1) * o_slc + gate_row(2) * o_win
    for j in range(HPG):
        y_ref[0, j * HEAD_DIM:(j + 1) * HEAD_DIM, :] = o[:, j * tq:(j + 1) * tq].astype(y_ref.dtype)


def _nsa(q, kc, vct, kv, vst, vwt, gt, ovt):
    nq = SEQ // NSA_TQ
    return pl.pallas_call(
        _nsa_kernel,
        grid=(BATCH, N_KV, nq),
        in_specs=[pl.BlockSpec((NSA_TQ, NSA_Q), lambda b, g, i: (b * nq + i, 0)),
                  pl.BlockSpec((1, N_CHUNK, NSA_KV), lambda b, g, i: (b, 0, 0)),
                  pl.BlockSpec((1, NSA_KV, N_CHUNK), lambda b, g, i: (b, 0, 0)),
                  pl.BlockSpec((SEQ, NSA_KV), lambda b, g, i: (b, 2)),
                  pl.BlockSpec((None, NSA_KV, SEQ), lambda b, g, i: (b, 0, 0)),
                  pl.BlockSpec((SEQ, NSA_KV), lambda b, g, i: (b, 4)),
                  pl.BlockSpec((None, NSA_KV, SEQ), lambda b, g, i: (b, 0, 0)),
                  pl.BlockSpec((1, 1, GATE_ROWS, NSA_TQ), lambda b, g, i: (b, g, 0, i)),
                  pl.BlockSpec((N_SLC, N_CHUNK), lambda b, g, i: (0, 0))],
        out_specs=pl.BlockSpec((1, HPG * HEAD_DIM, NSA_TQ), lambda b, g, i: (b, g, i)),
        out_shape=jax.ShapeDtypeStruct((BATCH, NSA_Q, SEQ), BF16),
        scratch_shapes=[pltpu.VMEM((N_SLC, NSA_TQ), F32)],
        compiler_params=_cparams(("arbitrary", "arbitrary", "arbitrary")),
        name="nsa",
    )(q, kc, vct, kv, vst, kv, vwt, gt, ovt)


def _overlap_t():
    c_start = np.arange(N_CHUNK) * CMP_STRIDE
    s_start = np.arange(N_SLC) * SLC_LEN
    ov = ((c_start[None, :] <= s_start[:, None] + SLC_LEN - 1)
          & (c_start[None, :] + CMP_LEN - 1 >= s_start[:, None])
          & (np.arange(N_CHUNK)[None, :] < N_CMP))
    return jnp.asarray(ov.astype(np.float32), dtype=BF16)


def _merge_kernel(x_ref, g_ref, wmg_ref, ya_ref, yb_ref, yct_ref, yd_ref, wb_ref, wo_ref, o_ref):
    x = x_ref[...]
    nb = _rms(x, g_ref[...]).astype(BF16)
    yc = yct_ref[0].astype(F32).T.astype(BF16)
    ys = (ya_ref[...], yb_ref[...], yc, yd_ref[...])
    merged = jnp.zeros((MERGE_TM, D_MODEL), F32)
    for bi, y in enumerate(ys):
        gate = jax.nn.sigmoid(jnp.dot(nb, wmg_ref[:, bi * D_MODEL:(bi + 1) * D_MODEL],
                                      preferred_element_type=F32))
        merged = merged + gate * jnp.dot(y, wb_ref[bi], preferred_element_type=F32)
    o_ref[...] = x + jnp.dot(merged.astype(BF16), wo_ref[...], preferred_element_type=F32)


def _merge(x, g, wmg, ya, yb, yct, yd, wb, wo):
    nt = SEQ // MERGE_TM
    tok = lambda i: (i, 0)
    const2 = lambda i: (0, 0)
    return pl.pallas_call(
        _merge_kernel,
        grid=(TOKENS // MERGE_TM,),
        in_specs=[pl.BlockSpec((MERGE_TM, D_MODEL), tok),
                  pl.BlockSpec((1, D_MODEL), const2),
                  pl.BlockSpec((D_MODEL, N_BRANCH * D_MODEL), const2),
                  pl.BlockSpec((MERGE_TM, MIX), tok),
                  pl.BlockSpec((MERGE_TM, MIX), tok),
                  pl.BlockSpec((1, MIX, MERGE_TM), lambda i: (i // nt, 0, i % nt)),
                  pl.BlockSpec((MERGE_TM, MIX), tok),
                  pl.BlockSpec((N_BRANCH, MIX, D_MODEL), lambda i: (0, 0, 0)),
                  pl.BlockSpec((D_MODEL, D_MODEL), const2)],
        out_specs=pl.BlockSpec((MERGE_TM, D_MODEL), tok),
        out_shape=jax.ShapeDtypeStruct((TOKENS, D_MODEL), F32),
        compiler_params=_cparams(("arbitrary",)),
        name="merge",
    )(x, g, wmg, ya, yb, yct, yd, wb, wo)


def _route(t, wr_ref, br_ref):
    logits = jnp.dot(t, wr_ref[...], preferred_element_type=F32,
                     precision=lax.Precision.HIGHEST) + br_ref[...]
    lane = lax.broadcasted_iota(jnp.int32, logits.shape, 1)
    is_grp = lane < N_GROUPS
    lg = jnp.where(is_grp, logits, NEG_INF)
    gmax = jnp.max(lg, axis=1, keepdims=True)
    gsum = jnp.sum(jnp.where(is_grp, jnp.exp(lg - gmax), 0.0), axis=1, keepdims=True)
    grp_w = 1.0 / gsum
    grp_idx = jnp.min(jnp.where(is_grp & (lg == gmax), lane, ROUTER_PAD), axis=1, keepdims=True)
    e_idx = lane - N_GROUPS
    in_grp = (e_idx >= grp_idx * EXPERTS_PER_GROUP) & (e_idx < (grp_idx + 1) * EXPERTS_PER_GROUP)
    le = jnp.where(in_grp, logits, NEG_INF)
    l1 = jnp.max(le, axis=1, keepdims=True)
    i1 = jnp.min(jnp.where(in_grp & (le == l1), lane, ROUTER_PAD), axis=1, keepdims=True)
    le2 = jnp.where(lane == i1, NEG_INF, le)
    l2 = jnp.max(le2, axis=1, keepdims=True)
    i2 = jnp.min(jnp.where(in_grp & (lane != i1) & (le2 == l2), lane, ROUTER_PAD), axis=1,
                 keepdims=True)
    e2 = jnp.exp(l2 - l1)
    den = 1.0 + e2
    w1 = grp_w / den
    w2 = grp_w * e2 / den
    return jnp.where(lane == i1, w1, 0.0) + jnp.where(lane == i2, w2, 0.0)


def _moe_kernel(x_ref, g_ref, wr_ref, br_ref, wg_ref, wu_ref, wd_ref, fg_ref, o_ref,
                t_ref, comb_ref, acc_ref, *, final_norm):
    e = pl.program_id(1)

    @pl.when(e == 0)
    def _():
        t = _rms(x_ref[...], g_ref[...])
        t_ref[...] = t.astype(BF16)
        comb_ref[...] = _route(t, wr_ref, br_ref)
        acc_ref[...] = jnp.zeros_like(acc_ref)

    tb = t_ref[...]
    comb = comb_ref[...]
    lane = lax.broadcasted_iota(jnp.int32, comb.shape, 1)
    ce = jnp.sum(jnp.where(lane == e + N_GROUPS, comb, 0.0), axis=1, keepdims=True)
    hid = (jax.nn.silu(jnp.dot(tb, wg_ref[0], preferred_element_type=F32))
           * jnp.dot(tb, wu_ref[0], preferred_element_type=F32)) * ce
    acc_ref[...] += jnp.dot(hid.astype(BF16), wd_ref[0], preferred_element_type=F32)

    @pl.when(e == N_EXPERTS - 1)
    def _():
        h = x_ref[...] + acc_ref[...]
        if final_norm:
            h = _rms(h, fg_ref[...])
        o_ref[...] = h


def _moe(x, g, wr, br, wg, wu, wd, fg, final_norm):
    tok = lambda i, e: (i, 0)
    const2 = lambda i, e: (0, 0)
    return pl.pallas_call(
        functools.partial(_moe_kernel, final_norm=final_norm),
        grid=(TOKENS // MOE_TM, N_EXPERTS),
        in_specs=[pl.BlockSpec((MOE_TM, D_MODEL), tok),
                  pl.BlockSpec((1, D_MODEL), const2),
                  pl.BlockSpec((D_MODEL, ROUTER_PAD), const2),
                  pl.BlockSpec((1, ROUTER_PAD), const2),
                  pl.BlockSpec((1, D_MODEL, D_EXPERT), lambda i, e: (e, 0, 0)),
                  pl.BlockSpec((1, D_MODEL, D_EXPERT), lambda i, e: (e, 0, 0)),
                  pl.BlockSpec((1, D_EXPERT, D_MODEL), lambda i, e: (e, 0, 0)),
                  pl.BlockSpec((1, D_MODEL), const2)],
        out_specs=pl.BlockSpec((MOE_TM, D_MODEL), tok),
        out_shape=jax.ShapeDtypeStruct((TOKENS, D_MODEL), F32),
        scratch_shapes=[pltpu.VMEM((MOE_TM, D_MODEL), BF16),
                        pltpu.VMEM((MOE_TM, ROUTER_PAD), F32),
                        pltpu.VMEM((MOE_TM, D_MODEL), F32)],
        compiler_params=_cparams(("arbitrary", "arbitrary")),
        name="moe",
    )(x, g, wr, br, wg, wu, wd, fg)


def _block_diag(w):
    eye = jnp.eye(LRU_BLOCKS, dtype=w.dtype)
    return jnp.einsum('hij,hk->hikj', w, eye).reshape(MIX, MIX)


def _in_proj_columns():
    cuts = np.cumsum((0,) + IN_SPLITS)
    q0 = int(cuts[4])
    q_cols = np.concatenate([q0 + (g * HPG + j) * HEAD_DIM + np.arange(HEAD_DIM)
                             for j in range(HPG) for g in range(N_KV)])
    main = np.concatenate([np.arange(cuts[0], cuts[4]), q_cols, np.arange(cuts[5], cuts[6]),
                           np.arange(cuts[7], cuts[8])])
    gate = np.arange(cuts[6], cuts[7])
    mg = np.arange(cuts[8], cuts[9])
    return main, gate, mg


def _layer(h, p, final_g, final_norm):
    main, gate, mg = _in_proj_columns()
    w_in = p['w_in']
    w_gate = jnp.pad(w_in[:, gate], ((0, 0), (0, GATE_PAD - gate.size)))
    w_proj = jnp.concatenate([w_in[:, main], w_gate], axis=1).astype(BF16)
    row = lambda a: a.reshape(1, -1)

    u, v, gb, rb, q, kv, xd, ng = _proj(h, row(p['norm1_g']), w_proj)

    bs = jnp.broadcast_to(p['gm_b'][:, :, None], (GM_GROUPS, GM_CHUNK, GM_GW))
    y_a = _gmlp(u, v, row(p['gm_norm_g']), p['gm_ws'], bs)

    y_b = _lru(gb, rb, p['conv_w'], row(p['conv_b']), _block_diag(p['lru_wa']).astype(BF16),
               row(p['lru_ba']), _block_diag(p['lru_wx']).astype(BF16), row(p['lru_bx']),
               row(p['lru_lambda']))

    y_d = _pool(xd, p['pool_w'].astype(BF16), row(p['pool_scale']))

    kvc = kv[:, :2 * NSA_KV].reshape(BATCH, SEQ, 2, N_KV, HEAD_DIM).transpose(2, 0, 3, 1, 4)
    kvc = kvc.reshape(2, BATCH, N_KV, N_CHUNK, HALF_FLAT)
    pe = p['cmp_pe'].reshape(2, 2, HALF_FLAT)
    cmp = _cmpkv(kvc, pe, p['cmp_w1'].astype(BF16), p['cmp_w2'].astype(BF16))
    kc = cmp[0].transpose(0, 2, 1, 3).reshape(BATCH, N_CHUNK, NSA_KV)
    vct = cmp[1].transpose(0, 1, 3, 2).reshape(BATCH, NSA_KV, N_CHUNK)
    kv3 = kv.reshape(BATCH, SEQ, 6 * NSA_KV)
    vst = kv3[:, :, 3 * NSA_KV:4 * NSA_KV].transpose(0, 2, 1)
    vwt = kv3[:, :, 5 * NSA_KV:6 * NSA_KV].transpose(0, 2, 1)
    gt = ng[:, :3 * N_HEADS].reshape(BATCH, SEQ, N_KV, 3 * HPG).transpose(0, 2, 3, 1)
    gt = jnp.pad(gt, ((0, 0), (0, 0), (0, GATE_ROWS - 3 * HPG), (0, 0)))
    y_ct = _nsa(q, kc, vct, kv, vst, vwt, gt, _overlap_t())

    h = _merge(h, row(p['norm1_g']), w_in[:, mg].astype(BF16), y_a, y_b, y_ct, y_d,
               p['w_branch'].astype(BF16), p['w_out'].astype(BF16))

    wr = jnp.concatenate([p['router_w_group'], p['router_w_expert']], axis=1)
    wr = jnp.pad(wr, ((0, 0), (0, ROUTER_PAD - wr.shape[1])))
    br = jnp.concatenate([p['router_b_group'], p['router_b_expert']])
    br = jnp.pad(br, (0, ROUTER_PAD - br.shape[0])).reshape(1, ROUTER_PAD)
    return _moe(h, row(p['norm2_g']), wr, br, p['moe_w_gate'].astype(BF16),
                p['moe_w_up'].astype(BF16), p['moe_w_down'].astype(BF16), row(final_g), final_norm)


_LAYER_PARAMS = ('norm1_g', 'w_in', 'gm_norm_g', 'gm_ws', 'gm_b', 'conv_w', 'conv_b', 'lru_wa',
                 'lru_ba', 'lru_wx', 'lru_bx', 'lru_lambda', 'cmp_pe', 'cmp_w1', 'cmp_w2', 'pool_w',
                 'pool_scale', 'w_branch', 'w_out', 'norm2_g', 'router_w_group', 'router_b_group',
                 'router_w_expert', 'router_b_expert', 'moe_w_gate', 'moe_w_up', 'moe_w_down')


def kernel(x, norm1_g, w_in, gm_norm_g, gm_ws, gm_b, conv_w, conv_b, lru_wa, lru_ba, lru_wx,
           lru_bx, lru_lambda, cmp_pe, cmp_w1, cmp_w2, pool_w, pool_scale, w_branch, w_out,
           norm2_g, router_w_group, router_b_group, router_w_expert, router_b_expert,
           moe_w_gate, moe_w_up, moe_w_down, final_norm_g):
    stacked = dict(zip(_LAYER_PARAMS, (
        norm1_g, w_in, gm_norm_g, gm_ws, gm_b, conv_w, conv_b, lru_wa, lru_ba, lru_wx, lru_bx,
        lru_lambda, cmp_pe, cmp_w1, cmp_w2, pool_w, pool_scale, w_branch, w_out, norm2_g,
        router_w_group, router_b_group, router_w_expert, router_b_expert, moe_w_gate, moe_w_up,
        moe_w_down)))
    h = x.reshape(TOKENS, D_MODEL)
    for layer in range(DEPTH):
        p = {k: a[layer] for k, a in stacked.items()}
        h = _layer(h, p, final_norm_g, final_norm=(layer == DEPTH - 1))
    return h.reshape(BATCH, SEQ, D_MODEL)
```

```python
import functools

import numpy as np
import jax
import jax.numpy as jnp
from jax import lax
from jax.experimental import pallas as pl
from jax.experimental.pallas import tpu as pltpu

F32 = jnp.float32
BF16 = jnp.bfloat16

D_MODEL = 1024
BATCH = 4
SEQ = 4096
TOKENS = BATCH * SEQ
DEPTH = 2
MIX = D_MODEL // 2
GM_CHUNK = 128
GM_GROUPS = 4
GM_GW = MIX // GM_GROUPS
CONV_WIDTH = 4
LRU_BLOCKS = 8
LRU_BW = MIX // LRU_BLOCKS
LRU_C = 8.0
N_HEADS = 8
HEAD_DIM = MIX // N_HEADS
N_KV = 2
HPG = N_HEADS // N_KV
CMP_LEN = 32
CMP_STRIDE = 16
SLC_LEN = 64
SLC_TOPN = 8
WIN = 512
NSA_Q = N_HEADS * HEAD_DIM
NSA_KV = N_KV * HEAD_DIM
POOL_WINDOWS = (2, 4, 8, 16)
POOL_GW = MIX // len(POOL_WINDOWS)
N_BRANCH = 4
N_GROUPS = 4
EXPERTS_PER_GROUP = 4
N_EXPERTS = N_GROUPS * EXPERTS_PER_GROUP
D_EXPERT = D_MODEL // 2
EPS = 1e-6
NEG_INF = -1e30
FORCE_SCORE = 1e6
IN_SPLITS = (MIX, MIX, MIX, MIX, NSA_Q, 6 * NSA_KV, 3 * N_HEADS, MIX, N_BRANCH * D_MODEL)

N_CHUNK = SEQ // CMP_STRIDE
N_CMP = N_CHUNK - CMP_LEN // CMP_STRIDE + 1
N_SLC = SEQ // SLC_LEN

LANES = 128
GATE_PAD = LANES
GATE_ROWS = 16
ROUTER_PAD = LANES
VMEM_LIMIT = 56 * 1024 * 1024

PROJ_WIDTHS = (MIX, MIX, MIX, MIX, NSA_Q, 6 * NSA_KV, MIX, GATE_PAD)
PROJ_TM = 512
GMLP_TM = 512
LRU_TS = 512
POOL_TS = 512
NSA_TQ = 128
SLC_KT = 512
WIN_KEYS = WIN + NSA_TQ
MERGE_TM = 256
ROUTE_TM = 512
MOE_TM = 256
MOE_ROWS = TOKENS + N_GROUPS * MOE_TM
MOE_NT = MOE_ROWS // MOE_TM


def _cparams(sem):
    return pltpu.CompilerParams(dimension_semantics=sem, vmem_limit_bytes=VMEM_LIMIT)


def _rms(x, g):
    return x * lax.rsqrt(jnp.mean(x * x, axis=-1, keepdims=True) + EPS) * g


def _proj_kernel(x_ref, g_ref, w_ref, *out_refs):
    nb = _rms(x_ref[...], g_ref[...]).astype(BF16)
    off = 0
    for ref in out_refs:
        w = ref.shape[-1]
        ref[...] = jnp.dot(nb, w_ref[:, off:off + w], preferred_element_type=F32).astype(ref.dtype)
        off += w


def _proj(x, g, w):
    n_in = sum(PROJ_WIDTHS)
    out_shape = [jax.ShapeDtypeStruct((TOKENS, wd), BF16) for wd in PROJ_WIDTHS[:-1]]
    out_shape.append(jax.ShapeDtypeStruct((TOKENS, GATE_PAD), F32))
    return pl.pallas_call(
        _proj_kernel,
        grid=(TOKENS // PROJ_TM,),
        in_specs=[pl.BlockSpec((PROJ_TM, D_MODEL), lambda i: (i, 0)),
                  pl.BlockSpec((1, D_MODEL), lambda i: (0, 0)),
                  pl.BlockSpec((D_MODEL, n_in), lambda i: (0, 0))],
        out_specs=[pl.BlockSpec((PROJ_TM, wd), lambda i: (i, 0)) for wd in PROJ_WIDTHS],
        out_shape=out_shape,
        compiler_params=_cparams(("arbitrary",)),
        name="proj",
    )(x, g, w)


def _gmlp_kernel(u_ref, v_ref, g_ref, ws_ref, bs_ref, o_ref):
    u = jax.nn.gelu(u_ref[...].astype(F32))
    v = _rms(jax.nn.gelu(v_ref[...].astype(F32)), g_ref[...]).astype(BF16)
    row = lax.broadcasted_iota(jnp.int32, (GM_CHUNK, GM_CHUNK), 0)
    col = lax.broadcasted_iota(jnp.int32, (GM_CHUNK, GM_CHUNK), 1)
    causal = row >= col
    for gi in range(GM_GROUPS):
        w = jnp.where(causal, ws_ref[gi], 0.0).astype(BF16)
        cs = slice(gi * GM_GW, (gi + 1) * GM_GW)
        for c in range(GMLP_TM // GM_CHUNK):
            rs = slice(c * GM_CHUNK, (c + 1) * GM_CHUNK)
            mixed = jnp.dot(w, v[rs, cs], preferred_element_type=F32) + bs_ref[gi]
            o_ref[rs, cs] = (u[rs, cs] * mixed).astype(o_ref.dtype)


def _gmlp(u, v, g, ws, bs):
    tok = lambda i: (i, 0)
    return pl.pallas_call(
        _gmlp_kernel,
        grid=(TOKENS // GMLP_TM,),
        in_specs=[pl.BlockSpec((GMLP_TM, MIX), tok),
                  pl.BlockSpec((GMLP_TM, MIX), tok),
                  pl.BlockSpec((1, MIX), lambda i: (0, 0)),
                  pl.BlockSpec((GM_GROUPS, GM_CHUNK, GM_CHUNK), lambda i: (0, 0, 0)),
                  pl.BlockSpec((GM_GROUPS, GM_CHUNK, GM_GW), lambda i: (0, 0, 0))],
        out_specs=pl.BlockSpec((GMLP_TM, MIX), tok),
        out_shape=jax.ShapeDtypeStruct((TOKENS, MIX), BF16),
        compiler_params=_cparams(("arbitrary",)),
        name="gmlp",
    )(u, v, g, ws, bs)


LRU_TAIL = 8


def _lru_kernel(gb_ref, rb_ref, cw_ref, cb_ref, wa_ref, ba_ref, wx_ref, bx_ref, lam_ref, o_ref,
                tail_ref, h_ref):
    @pl.when(pl.program_id(1) == 0)
    def _():
        tail_ref[...] = jnp.zeros_like(tail_ref)
        h_ref[...] = jnp.zeros_like(h_ref)

    ts = LRU_TS
    x = rb_ref[...].astype(F32)
    ext = jnp.concatenate([tail_ref[...], x], axis=0)
    tail_ref[...] = x[ts - LRU_TAIL:, :]
    xc = cb_ref[...] + x * cw_ref[CONV_WIDTH - 1:CONV_WIDTH, :]
    for d in range(1, CONV_WIDTH):
        xs = pltpu.roll(ext, d, axis=0)[LRU_TAIL:, :]
        xc = xc + xs * cw_ref[CONV_WIDTH - 1 - d:CONV_WIDTH - d, :]
    xcb = xc.astype(BF16)
    r = jax.nn.sigmoid(jnp.dot(xcb, wa_ref[...], preferred_element_type=F32) + ba_ref[...])
    ig = jax.nn.sigmoid(jnp.dot(xcb, wx_ref[...], preferred_element_type=F32) + bx_ref[...])
    z = -lam_ref[...]
    softplus = jnp.maximum(z, 0.0) + jnp.log1p(jnp.exp(-jnp.abs(z)))
    log_a = -LRU_C * r * softplus
    a = jnp.exp(log_a)
    b = jnp.sqrt(1.0 - jnp.exp(2.0 * log_a)) * (ig * xc)
    rows = lax.broadcasted_iota(jnp.int32, (ts, 1), 0)
    d = 1
    while d < ts:
        valid = rows >= d
        a_prev = pltpu.roll(a, d, axis=0)
        b_prev = pltpu.roll(b, d, axis=0)
        b = jnp.where(valid, a * b_prev, 0.0) + b
        a = jnp.where(valid, a * a_prev, a)
        d *= 2
    h = a * h_ref[...] + b
    h_ref[...] = h[ts - 1:ts, :]
    o_ref[...] = (jax.nn.gelu(gb_ref[...].astype(F32)) * h).astype(o_ref.dtype)


def _lru(gb, rb, cw, cb, wa, ba, wx, bx, lam):
    nt = SEQ // LRU_TS
    tok = lambda b, s: (b * nt + s, 0)
    vec = pl.BlockSpec((1, MIX), lambda b, s: (0, 0))
    mat = pl.BlockSpec((MIX, MIX), lambda b, s: (0, 0))
    return pl.pallas_call(
        _lru_kernel,
        grid=(BATCH, nt),
        in_specs=[pl.BlockSpec((LRU_TS, MIX), tok), pl.BlockSpec((LRU_TS, MIX), tok),
                  pl.BlockSpec((CONV_WIDTH, MIX), lambda b, s: (0, 0)), vec,
                  mat, vec, mat, vec, vec],
        out_specs=pl.BlockSpec((LRU_TS, MIX), tok),
        out_shape=jax.ShapeDtypeStruct((TOKENS, MIX), BF16),
        scratch_shapes=[pltpu.VMEM((LRU_TAIL, MIX), F32), pltpu.VMEM((1, MIX), F32)],
        compiler_params=_cparams(("arbitrary", "arbitrary")),
        name="rglru",
    )(gb, rb, cw, cb, wa, ba, wx, bx, lam)


POOL_TAIL = 16


def _pool_kernel(x_ref, w_ref, sc_ref, o_ref, tail_ref):
    s_id = pl.program_id(1)

    @pl.when(s_id == 0)
    def _():
        tail_ref[...] = jnp.zeros_like(tail_ref)

    ts = POOL_TS
    x = x_ref[...].astype(F32)
    ext = jnp.concatenate([tail_ref[...], x], axis=0)
    tail_ref[...] = x[ts - POOL_TAIL:, :]
    pos = s_id * ts + lax.broadcasted_iota(jnp.int32, (ts, 1), 0)
    acc = ext
    width = 1
    for gi, wdw in enumerate(POOL_WINDOWS):
        while width < wdw:
            acc = acc + pltpu.roll(acc, width, axis=0)
            width *= 2
        cs = slice(gi * POOL_GW, (gi + 1) * POOL_GW)
        cnt = jnp.minimum(pos + 1, wdw).astype(F32)
        pooled = acc[POOL_TAIL:, cs] / cnt - x[:, cs]
        mixed = jnp.dot(pooled.astype(BF16), w_ref[gi], preferred_element_type=F32)
        o_ref[:, cs] = (mixed * sc_ref[:, cs]).astype(o_ref.dtype)


def _pool(xd, w, sc):
    nt = SEQ // POOL_TS
    tok = lambda b, s: (b * nt + s, 0)
    return pl.pallas_call(
        _pool_kernel,
        grid=(BATCH, nt),
        in_specs=[pl.BlockSpec((POOL_TS, MIX), tok),
                  pl.BlockSpec((len(POOL_WINDOWS), POOL_GW, POOL_GW), lambda b, s: (0, 0, 0)),
                  pl.BlockSpec((1, MIX), lambda b, s: (0, 0))],
        out_specs=pl.BlockSpec((POOL_TS, MIX), tok),
        out_shape=jax.ShapeDtypeStruct((TOKENS, MIX), BF16),
        scratch_shapes=[pltpu.VMEM((POOL_TAIL, MIX), F32)],
        compiler_params=_cparams(("arbitrary", "arbitrary")),
        name="pool",
    )(xd, w, sc)


HALF_FLAT = CMP_STRIDE * HEAD_DIM


def _cmpkv_kernel(x_ref, pe_ref, w1_ref, w2_ref, o_ref):
    x = x_ref[0, 0, 0].astype(F32)
    pe = pe_ref[0]
    w1 = w1_ref[0, 0]
    first = jnp.dot((x + pe[0:1, :]).astype(BF16), w1[:HALF_FLAT, :], preferred_element_type=F32)
    second = jnp.dot((x + pe[1:2, :]).astype(BF16), w1[HALF_FLAT:, :], preferred_element_type=F32)
    hid = jax.nn.gelu(first + pltpu.roll(second, N_CHUNK - 1, axis=0))
    out = jnp.dot(hid.astype(BF16), w2_ref[0, 0], preferred_element_type=F32)
    row = lax.broadcasted_iota(jnp.int32, (N_CHUNK, 1), 0)
    o_ref[0, 0, 0] = jnp.where(row < N_CMP, out, 0.0).astype(o_ref.dtype)


def _cmpkv(x, pe, w1, w2):
    return pl.pallas_call(
        _cmpkv_kernel,
        grid=(2, BATCH, N_KV),
        in_specs=[pl.BlockSpec((1, 1, 1, N_CHUNK, HALF_FLAT), lambda c, b, g: (c, b, g, 0, 0)),
                  pl.BlockSpec((1, 2, HALF_FLAT), lambda c, b, g: (c, 0, 0)),
                  pl.BlockSpec((1, 1, CMP_LEN * HEAD_DIM, HEAD_DIM), lambda c, b, g: (c, g, 0, 0)),
                  pl.BlockSpec((1, 1, HEAD_DIM, HEAD_DIM), lambda c, b, g: (c, g, 0, 0))],
        out_specs=pl.BlockSpec((1, 1, 1, N_CHUNK, HEAD_DIM), lambda c, b, g: (c, b, g, 0, 0)),
        out_shape=jax.ShapeDtypeStruct((2, BATCH, N_KV, N_CHUNK, HEAD_DIM), BF16),
        compiler_params=_cparams(("arbitrary", "arbitrary", "arbitrary")),
        name="cmpkv",
    )(x, pe, w1, w2)


def _lane_tile(x, n):
    return jnp.concatenate([x] * n, axis=1)


def _nsa_kernel(q_ref, kc_ref, vct_ref, ks_ref, vst_ref, kw_ref, vwt_ref, gt_ref, ovt_ref, y_ref,
                sel_ref):
    g = pl.program_id(1)
    i = pl.program_id(2)
    tq = NSA_TQ
    nl = HPG * tq
    t_row = i * tq + lax.broadcasted_iota(jnp.int32, (1, tq), 1)
    row0 = pl.multiple_of(g * HEAD_DIM, HEAD_DIM)

    lane = lax.broadcasted_iota(jnp.int32, (tq, LANES), 1)
    keep = (lane >= HEAD_DIM) == (g == 1)
    scale = HEAD_DIM ** -0.5
    q4 = jnp.concatenate(
        [jnp.where(keep, q_ref[:, j * LANES:(j + 1) * LANES], 0) * scale for j in range(HPG)],
        axis=0).astype(BF16)

    def scores(k):
        return lax.dot_general(k, q4, (((1,), (1,)), ((), ())), preferred_element_type=F32)

    s = scores(kc_ref[0])
    n_idx = lax.broadcasted_iota(jnp.int32, (N_CHUNK, 1), 0)
    ok = n_idx * CMP_STRIDE + (CMP_LEN - 1) <= _lane_tile(t_row, HPG)
    s = jnp.where(ok, s, NEG_INF)
    m = jnp.max(s, axis=0, keepdims=True)
    e = jnp.where(ok, jnp.exp(s - m), 0.0)
    l = jnp.sum(e, axis=0, keepdims=True)
    p = e / jnp.where(l > 0.0, l, 1.0)
    o_cmp = jnp.dot(vct_ref[0, pl.ds(row0, HEAD_DIM), :], p.astype(BF16),
                    preferred_element_type=F32)
    psum = p[:, 0:tq]
    for j in range(1, HPG):
        psum = psum + p[:, j * tq:(j + 1) * tq]
    p_hi = psum.astype(BF16)
    p_lo = (psum - p_hi.astype(F32)).astype(BF16)
    imp = (jnp.dot(ovt_ref[...], p_hi, preferred_element_type=F32)
           + jnp.dot(ovt_ref[...], p_lo, preferred_element_type=F32))

    blk = lax.broadcasted_iota(jnp.int32, (N_SLC, 1), 0)
    cur = jnp.right_shift(t_row, 6)
    forced = (blk == 0) | (blk == cur) | (blk == cur - 1)
    causal_blk = blk * SLC_LEN <= t_row
    work = jnp.where(forced, FORCE_SCORE, jnp.where(causal_blk, imp, -1.0))
    sel = jnp.zeros((N_SLC, tq), F32)
    for _ in range(SLC_TOPN):
        mx = jnp.max(work, axis=0, keepdims=True)
        idx = jnp.min(jnp.where(work == mx, blk, N_SLC), axis=0, keepdims=True)
        pick = blk == idx
        sel = jnp.where(pick & (mx >= 0.0), 1.0, sel)
        work = jnp.where(pick, -3e38, work)
    sel_ref[...] = sel

    def slc_step(kt, carry):
        m_i, l_i, acc = carry
        k0 = pl.multiple_of(kt * SLC_KT, SLC_KT)
        sT = scores(ks_ref[pl.ds(k0, SLC_KT), :])
        sel8 = sel_ref[pl.ds(pl.multiple_of(kt * (SLC_KT // SLC_LEN), 8), SLC_KT // SLC_LEN), :]
        selx = jnp.concatenate(
            [jnp.broadcast_to(sel8[r:r + 1, :], (SLC_LEN, tq)) for r in range(SLC_KT // SLC_LEN)],
            axis=0)
        kpos = k0 + lax.broadcasted_iota(jnp.int32, (SLC_KT, 1), 0)
        bias = jnp.where((selx > 0.0) & (kpos <= t_row), 0.0, NEG_INF)
        sT = sT + _lane_tile(bias, HPG)
        m_new = jnp.maximum(m_i, jnp.max(sT, axis=0, keepdims=True))
        alpha = jnp.exp(m_i - m_new)
        pT = jnp.exp(sT - m_new)
        l_new = alpha * l_i + jnp.sum(pT, axis=0, keepdims=True)
        vT = vst_ref[pl.ds(row0, HEAD_DIM), pl.ds(k0, SLC_KT)]
        acc = alpha * acc + jnp.dot(vT, pT.astype(BF16), preferred_element_type=F32)
        return m_new, l_new, acc

    n_kt = jnp.right_shift(i, 2) + 1
    init = (jnp.full((1, nl), -3e38, F32), jnp.zeros((1, nl), F32), jnp.zeros((HEAD_DIM, nl), F32))
    _, l_s, acc_s = lax.fori_loop(0, n_kt, slc_step, init)
    o_slc = acc_s / l_s

    start = pl.multiple_of(jnp.maximum(i - WIN // tq, 0) * tq, tq)
    sT = scores(kw_ref[pl.ds(start, WIN_KEYS), :])
    delta = t_row - (start + lax.broadcasted_iota(jnp.int32, (WIN_KEYS, 1), 0))
    bias = jnp.where((delta >= 0) & (delta < WIN), 0.0, NEG_INF)
    sT = sT + _lane_tile(bias, HPG)
    pT = jnp.exp(sT - jnp.max(sT, axis=0, keepdims=True))
    l_w = jnp.sum(pT, axis=0, keepdims=True)
    vT = vwt_ref[pl.ds(row0, HEAD_DIM), pl.ds(start, WIN_KEYS)]
    o_win = jnp.dot(vT, pT.astype(BF16), preferred_element_type=F32) / l_w

    gates = jax.nn.sigmoid(gt_ref[0, 0])

    def gate_row(br):
        return jnp.concatenate([gates[3 * j + br:3 * j + br + 1, :] for j in range(HPG)], axis=1)

    o = gate_row(0) * o_cmp + gate_row(1) * o_slc + gate_row(2) * o_win
    for j in range(HPG):
        y_ref[0, j * HEAD_DIM:(j + 1) * HEAD_DIM, :] = o[:, j * tq:(j + 1) * tq].astype(y_ref.dtype)


def _nsa(q, kc, vct, kv, vst, vwt, gt, ovt):
    nq = SEQ // NSA_TQ
    return pl.pallas_call(
        _nsa_kernel,
        grid=(BATCH, N_KV, nq),
        in_specs=[pl.BlockSpec((NSA_TQ, NSA_Q), lambda b, g, i: (b * nq + i, 0)),
                  pl.BlockSpec((1, N_CHUNK, NSA_KV), lambda b, g, i: (b, 0, 0)),
                  pl.BlockSpec((1, NSA_KV, N_CHUNK), lambda b, g, i: (b, 0, 0)),
                  pl.BlockSpec((SEQ, NSA_KV), lambda b, g, i: (b, 2)),
                  pl.BlockSpec((None, NSA_KV, SEQ), lambda b, g, i: (b, 0, 0)),
                  pl.BlockSpec((SEQ, NSA_KV), lambda b, g, i: (b, 4)),
                  pl.BlockSpec((None, NSA_KV, SEQ), lambda b, g, i: (b, 0, 0)),
                  pl.BlockSpec((1, 1, GATE_ROWS, NSA_TQ), lambda b, g, i: (b, g, 0, i)),
                  pl.BlockSpec((N_SLC, N_CHUNK), lambda b, g, i: (0, 0))],
        out_specs=pl.BlockSpec((1, HPG * HEAD_DIM, NSA_TQ), lambda b, g, i: (b, g, i)),
        out_shape=jax.ShapeDtypeStruct((BATCH, NSA_Q, SEQ), BF16),
        scratch_shapes=[pltpu.VMEM((N_SLC, NSA_TQ), F32)],
        compiler_params=_cparams(("arbitrary", "arbitrary", "arbitrary")),
        name="nsa",
    )(q, kc, vct, kv, vst, kv, vwt, gt, ovt)


def _overlap_t():
    c_start = np.arange(N_CHUNK) * CMP_STRIDE
    s_start = np.arange(N_SLC) * SLC_LEN
    ov = ((c_start[None, :] <= s_start[:, None] + SLC_LEN - 1)
          & (c_start[None, :] + CMP_LEN - 1 >= s_start[:, None])
          & (np.arange(N_CHUNK)[None, :] < N_CMP))
    return jnp.asarray(ov.astype(np.float32), dtype=BF16)


def _merge_kernel(x_ref, g_ref, wmg_ref, ya_ref, yb_ref, yct_ref, yd_ref, wb_ref, wo_ref, o_ref):
    x = x_ref[...]
    nb = _rms(x, g_ref[...]).astype(BF16)
    yc = yct_ref[0].astype(F32).T.astype(BF16)
    ys = (ya_ref[...], yb_ref[...], yc, yd_ref[...])
    merged = jnp.zeros((MERGE_TM, D_MODEL), F32)
    for bi, y in enumerate(ys):
        gate = jax.nn.sigmoid(jnp.dot(nb, wmg_ref[:, bi * D_MODEL:(bi + 1) * D_MODEL],
                                      preferred_element_type=F32))
        merged = merged + gate * jnp.dot(y, wb_ref[bi], preferred_element_type=F32)
    o_ref[...] = x + jnp.dot(merged.astype(BF16), wo_ref[...], preferred_element_type=F32)


def _merge(x, g, wmg, ya, yb, yct, yd, wb, wo):
    nt = SEQ // MERGE_TM
    tok = lambda i: (i, 0)
    const2 = lambda i: (0, 0)
    return pl.pallas_call(
        _merge_kernel,
        grid=(TOKENS // MERGE_TM,),
        in_specs=[pl.BlockSpec((MERGE_TM, D_MODEL), tok),
                  pl.BlockSpec((1, D_MODEL), const2),
                  pl.BlockSpec((D_MODEL, N_BRANCH * D_MODEL), const2),
                  pl.BlockSpec((MERGE_TM, MIX), tok),
                  pl.BlockSpec((MERGE_TM, MIX), tok),
                  pl.BlockSpec((1, MIX, MERGE_TM), lambda i: (i // nt, 0, i % nt)),
                  pl.BlockSpec((MERGE_TM, MIX), tok),
                  pl.BlockSpec((N_BRANCH, MIX, D_MODEL), lambda i: (0, 0, 0)),
                  pl.BlockSpec((D_MODEL, D_MODEL), const2)],
        out_specs=pl.BlockSpec((MERGE_TM, D_MODEL), tok),
        out_shape=jax.ShapeDtypeStruct((TOKENS, D_MODEL), F32),
        compiler_params=_cparams(("arbitrary",)),
        name="merge",
    )(x, g, wmg, ya, yb, yct, yd, wb, wo)


def _router_logits(t, wr_ref, br_ref):
    return jnp.dot(t, wr_ref[...], preferred_element_type=F32,
                   precision=lax.Precision.HIGHEST) + br_ref[...]


def _top_group(logits):
    lane = lax.broadcasted_iota(jnp.int32, logits.shape, 1)
    is_grp = lane < N_GROUPS
    lg = jnp.where(is_grp, logits, NEG_INF)
    gmax = jnp.max(lg, axis=1, keepdims=True)
    grp_idx = jnp.min(jnp.where(is_grp & (lg == gmax), lane, ROUTER_PAD), axis=1, keepdims=True)
    return lg, gmax, grp_idx


def _combine_weights(logits, grp_idx):
    lane = lax.broadcasted_iota(jnp.int32, logits.shape, 1)
    lg, gmax, _ = _top_group(logits)
    ge = jnp.where(lane < N_GROUPS, jnp.exp(lg - gmax), 0.0)
    grp_w = (jnp.sum(jnp.where(lane == grp_idx, ge, 0.0), axis=1, keepdims=True)
             / jnp.sum(ge, axis=1, keepdims=True))
    e_idx = lane - N_GROUPS
    in_grp = (e_idx >= grp_idx * EXPERTS_PER_GROUP) & (e_idx < (grp_idx + 1) * EXPERTS_PER_GROUP)
    le = jnp.where(in_grp, logits, NEG_INF)
    l1 = jnp.max(le, axis=1, keepdims=True)
    i1 = jnp.min(jnp.where(in_grp & (le == l1), lane, ROUTER_PAD), axis=1, keepdims=True)
    le2 = jnp.where(lane == i1, NEG_INF, le)
    l2 = jnp.max(le2, axis=1, keepdims=True)
    i2 = jnp.min(jnp.where(in_grp & (lane != i1) & (le2 == l2), lane, ROUTER_PAD), axis=1,
                 keepdims=True)
    e2 = jnp.exp(l2 - l1)
    den = 1.0 + e2
    w1 = grp_w / den
    w2 = grp_w * e2 / den
    return jnp.where(lane == i1, w1, 0.0) + jnp.where(lane == i2, w2, 0.0)


def _route_kernel(x_ref, g_ref, wr_ref, br_ref, tri_ref, meta_ref, cnt_ref, run_ref):
    @pl.when(pl.program_id(0) == 0)
    def _():
        run_ref[...] = jnp.zeros_like(run_ref)

    logits = _router_logits(_rms(x_ref[...], g_ref[...]), wr_ref, br_ref)
    _, _, grp_idx = _top_group(logits)
    lane = lax.broadcasted_iota(jnp.int32, logits.shape, 1)
    onehot = jnp.where(lane == grp_idx, 1.0, 0.0)
    before = jnp.dot(tri_ref[...], onehot.astype(BF16), preferred_element_type=F32) + run_ref[...]
    rank = jnp.sum(onehot * before, axis=1, keepdims=True).astype(jnp.int32)
    run_ref[...] += jnp.sum(onehot, axis=0, keepdims=True)
    meta_ref[...] = jnp.where(lane == 0, grp_idx, jnp.where(lane == 1, rank, 0))
    cnt_ref[...] = run_ref[...]


def _route(x, g, wr, br):
    tok = lambda i: (i, 0)
    const2 = lambda i: (0, 0)
    tri = jnp.asarray(np.tril(np.ones((ROUTE_TM, ROUTE_TM), np.float32), -1), dtype=BF16)
    return pl.pallas_call(
        _route_kernel,
        grid=(TOKENS // ROUTE_TM,),
        in_specs=[pl.BlockSpec((ROUTE_TM, D_MODEL), tok),
                  pl.BlockSpec((1, D_MODEL), const2),
                  pl.BlockSpec((D_MODEL, ROUTER_PAD), const2),
                  pl.BlockSpec((1, ROUTER_PAD), const2),
                  pl.BlockSpec((ROUTE_TM, ROUTE_TM), const2)],
        out_specs=[pl.BlockSpec((ROUTE_TM, ROUTER_PAD), tok), pl.BlockSpec((1, ROUTER_PAD), const2)],
        out_shape=[jax.ShapeDtypeStruct((TOKENS, ROUTER_PAD), jnp.int32),
                   jax.ShapeDtypeStruct((1, ROUTER_PAD), F32)],
        scratch_shapes=[pltpu.VMEM((1, ROUTER_PAD), F32)],
        compiler_params=_cparams(("arbitrary",)),
        name="route",
    )(x, g, wr, br, tri)


def _experts_kernel(tok_ref, tg_ref, tv_ref, h_hbm, g_ref, wr_ref, br_ref, wg_ref, wu_ref, wd_ref,
                    fg_ref, out_hbm, xbuf, obuf, gsem, ssem, *, final_norm):
    j = pl.program_id(0)
    nt = pl.num_programs(0)
    slot = lax.rem(j, 2)
    tm = MOE_TM

    def gather_copy(tile, sl, r):
        return pltpu.make_async_copy(h_hbm.at[pl.ds(tok_ref[tile * tm + r], 1)],
                                     xbuf.at[sl, pl.ds(r, 1)], gsem.at[sl])

    def scatter_copy(tile, sl, r):
        return pltpu.make_async_copy(obuf.at[sl, pl.ds(r, 1)],
                                     out_hbm.at[pl.ds(tok_ref[tile * tm + r], 1)], ssem.at[sl])

    def for_rows(n, fn):
        def body(r, c):
            fn(r)
            return c
        lax.fori_loop(0, n, body, 0)

    def start_gather(tile, sl):
        for_rows(tv_ref[tile], lambda r: gather_copy(tile, sl, r).start())

    @pl.when(j == 0)
    def _():
        xbuf[...] = jnp.zeros_like(xbuf)
        start_gather(0, 0)

    @pl.when(j + 1 < nt)
    def _():
        start_gather(j + 1, 1 - slot)

    @pl.when(j >= 2)
    def _():
        for_rows(tv_ref[j - 2], lambda r: scatter_copy(j - 2, slot, r).wait())

    nv = tv_ref[j]
    for_rows(nv, lambda r: gather_copy(j, slot, r).wait())

    @pl.when(nv > 0)
    def _():
        x = xbuf[slot]
        t = _rms(x, g_ref[...])
        grp = tg_ref[j]
        comb = _combine_weights(_router_logits(t, wr_ref, br_ref), grp)
        lane = lax.broadcasted_iota(jnp.int32, comb.shape, 1)
        tb = t.astype(BF16)
        acc = jnp.zeros((tm, D_MODEL), F32)
        for e in range(EXPERTS_PER_GROUP):
            ce = jnp.sum(jnp.where(lane == N_GROUPS + grp * EXPERTS_PER_GROUP + e, comb, 0.0),
                         axis=1, keepdims=True)
            hid = (jax.nn.silu(jnp.dot(tb, wg_ref[e], preferred_element_type=F32))
                   * jnp.dot(tb, wu_ref[e], preferred_element_type=F32)) * ce
            acc = acc + jnp.dot(hid.astype(BF16), wd_ref[e], preferred_element_type=F32)
        h = x + acc
        if final_norm:
            h = _rms(h, fg_ref[...])
        obuf[slot] = h
        for_rows(nv, lambda r: scatter_copy(j, slot, r).start())

    @pl.when(j == nt - 1)
    def _():
        for_rows(tv_ref[j - 1], lambda r: scatter_copy(j - 1, 1 - slot, r).wait())
        for_rows(nv, lambda r: scatter_copy(j, slot, r).wait())


def _experts(tok, tile_group, tile_valid, h, g, wr, br, wg, wu, wd, fg, final_norm):
    const2 = lambda j, *_: (0, 0)
    grp_w = lambda j, tok, tg, tv: (tg[j], 0, 0)
    grid_spec = pltpu.PrefetchScalarGridSpec(
        num_scalar_prefetch=3,
        grid=(MOE_NT,),
        in_specs=[pl.BlockSpec(memory_space=pl.ANY),
                  pl.BlockSpec((1, D_MODEL), const2),
                  pl.BlockSpec((D_MODEL, ROUTER_PAD), const2),
                  pl.BlockSpec((1, ROUTER_PAD), const2),
                  pl.BlockSpec((EXPERTS_PER_GROUP, D_MODEL, D_EXPERT), grp_w),
                  pl.BlockSpec((EXPERTS_PER_GROUP, D_MODEL, D_EXPERT), grp_w),
                  pl.BlockSpec((EXPERTS_PER_GROUP, D_EXPERT, D_MODEL), grp_w),
                  pl.BlockSpec((1, D_MODEL), const2)],
        out_specs=pl.BlockSpec(memory_space=pl.ANY),
        scratch_shapes=[pltpu.VMEM((2, MOE_TM, D_MODEL), F32),
                        pltpu.VMEM((2, MOE_TM, D_MODEL), F32),
                        pltpu.SemaphoreType.DMA((2,)),
                        pltpu.SemaphoreType.DMA((2,))])
    return pl.pallas_call(
        functools.partial(_experts_kernel, final_norm=final_norm),
        grid_spec=grid_spec,
        out_shape=jax.ShapeDtypeStruct((TOKENS, D_MODEL), F32),
        compiler_params=_cparams(("arbitrary",)),
        name="experts",
    )(tok, tile_group, tile_valid, h, g, wr, br, wg, wu, wd, fg)


def _moe(x, g, wr, br, wg, wu, wd, fg, final_norm):
    meta, cnt = _route(x, g, wr, br)
    grp, rank = meta[:, 0], meta[:, 1]
    counts = cnt[0, :N_GROUPS].astype(jnp.int32)
    padded = (counts + MOE_TM - 1) // MOE_TM * MOE_TM
    ends = jnp.cumsum(padded)
    starts = ends - padded
    pos = starts[grp] + rank
    tok = jnp.zeros((MOE_ROWS,), jnp.int32).at[pos].set(jnp.arange(TOKENS, dtype=jnp.int32))
    tile_start = jnp.arange(MOE_NT, dtype=jnp.int32) * MOE_TM
    tile_group = jnp.minimum(jnp.sum(tile_start[:, None] >= ends[None, :], axis=1), N_GROUPS - 1)
    tile_group = tile_group.astype(jnp.int32)
    tile_valid = jnp.clip(starts[tile_group] + counts[tile_group] - tile_start, 0, MOE_TM)
    return _experts(tok, tile_group, tile_valid.astype(jnp.int32), x, g, wr, br, wg, wu, wd, fg,
                    final_norm)


def _block_diag(w):
    eye = jnp.eye(LRU_BLOCKS, dtype=w.dtype)
    return jnp.einsum('hij,hk->hikj', w, eye).reshape(MIX, MIX)


def _in_proj_columns():
    cuts = np.cumsum((0,) + IN_SPLITS)
    q0 = int(cuts[4])
    q_cols = np.concatenate([q0 + (g * HPG + j) * HEAD_DIM + np.arange(HEAD_DIM)
                             for j in range(HPG) for g in range(N_KV)])
    main = np.concatenate([np.arange(cuts[0], cuts[4]), q_cols, np.arange(cuts[5], cuts[6]),
                           np.arange(cuts[7], cuts[8])])
    gate = np.arange(cuts[6], cuts[7])
    mg = np.arange(cuts[8], cuts[9])
    return main, gate, mg


def _layer(h, p, final_g, final_norm):
    main, gate, mg = _in_proj_columns()
    w_in = p['w_in']
    w_gate = jnp.pad(w_in[:, gate], ((0, 0), (0, GATE_PAD - gate.size)))
    w_proj = jnp.concatenate([w_in[:, main], w_gate], axis=1).astype(BF16)
    row = lambda a: a.reshape(1, -1)

    u, v, gb, rb, q, kv, xd, ng = _proj(h, row(p['norm1_g']), w_proj)

    bs = jnp.broadcast_to(p['gm_b'][:, :, None], (GM_GROUPS, GM_CHUNK, GM_GW))
    y_a = _gmlp(u, v, row(p['gm_norm_g']), p['gm_ws'], bs)

    y_b = _lru(gb, rb, p['conv_w'], row(p['conv_b']), _block_diag(p['lru_wa']).astype(BF16),
               row(p['lru_ba']), _block_diag(p['lru_wx']).astype(BF16), row(p['lru_bx']),
               row(p['lru_lambda']))

    y_d = _pool(xd, p['pool_w'].astype(BF16), row(p['pool_scale']))

    kvc = kv[:, :2 * NSA_KV].reshape(BATCH, SEQ, 2, N_KV, HEAD_DIM).transpose(2, 0, 3, 1, 4)
    kvc = kvc.reshape(2, BATCH, N_KV, N_CHUNK, HALF_FLAT)
    pe = p['cmp_pe'].reshape(2, 2, HALF_FLAT)
    cmp = _cmpkv(kvc, pe, p['cmp_w1'].astype(BF16), p['cmp_w2'].astype(BF16))
    kc = cmp[0].transpose(0, 2, 1, 3).reshape(BATCH, N_CHUNK, NSA_KV)
    vct = cmp[1].transpose(0, 1, 3, 2).reshape(BATCH, NSA_KV, N_CHUNK)
    kv3 = kv.reshape(BATCH, SEQ, 6 * NSA_KV)
    vst = kv3[:, :, 3 * NSA_KV:4 * NSA_KV].transpose(0, 2, 1)
    vwt = kv3[:, :, 5 * NSA_KV:6 * NSA_KV].transpose(0, 2, 1)
    gt = ng[:, :3 * N_HEADS].reshape(BATCH, SEQ, N_KV, 3 * HPG).transpose(0, 2, 3, 1)
    gt = jnp.pad(gt, ((0, 0), (0, 0), (0, GATE_ROWS - 3 * HPG), (0, 0)))
    y_ct = _nsa(q, kc, vct, kv, vst, vwt, gt, _overlap_t())

    h = _merge(h, row(p['norm1_g']), w_in[:, mg].astype(BF16), y_a, y_b, y_ct, y_d,
               p['w_branch'].astype(BF16), p['w_out'].astype(BF16))

    wr = jnp.concatenate([p['router_w_group'], p['router_w_expert']], axis=1)
    wr = jnp.pad(wr, ((0, 0), (0, ROUTER_PAD - wr.shape[1])))
    br = jnp.concatenate([p['router_b_group'], p['router_b_expert']])
    br = jnp.pad(br, (0, ROUTER_PAD - br.shape[0])).reshape(1, ROUTER_PAD)
    return _moe(h, row(p['norm2_g']), wr, br, p['moe_w_gate'].astype(BF16),
                p['moe_w_up'].astype(BF16), p['moe_w_down'].astype(BF16), row(final_g), final_norm)


_LAYER_PARAMS = ('norm1_g', 'w_in', 'gm_norm_g', 'gm_ws', 'gm_b', 'conv_w', 'conv_b', 'lru_wa',
                 'lru_ba', 'lru_wx', 'lru_bx', 'lru_lambda', 'cmp_pe', 'cmp_w1', 'cmp_w2', 'pool_w',
                 'pool_scale', 'w_branch', 'w_out', 'norm2_g', 'router_w_group', 'router_b_group',
                 'router_w_expert', 'router_b_expert', 'moe_w_gate', 'moe_w_up', 'moe_w_down')


def kernel(x, norm1_g, w_in, gm_norm_g, gm_ws, gm_b, conv_w, conv_b, lru_wa, lru_ba, lru_wx,
           lru_bx, lru_lambda, cmp_pe, cmp_w1, cmp_w2, pool_w, pool_scale, w_branch, w_out,
           norm2_g, router_w_group, router_b_group, router_w_expert, router_b_expert,
           moe_w_gate, moe_w_up, moe_w_down, final_norm_g):
    stacked = dict(zip(_LAYER_PARAMS, (
        norm1_g, w_in, gm_norm_g, gm_ws, gm_b, conv_w, conv_b, lru_wa, lru_ba, lru_wx, lru_bx,
        lru_lambda, cmp_pe, cmp_w1, cmp_w2, pool_w, pool_scale, w_branch, w_out, norm2_g,
        router_w_group, router_b_group, router_w_expert, router_b_expert, moe_w_gate, moe_w_up,
        moe_w_down)))
    h = x.reshape(TOKENS, D_MODEL)
    for layer in range(DEPTH):
        p = {k: a[layer] for k, a in stacked.items()}
        h = _layer(h, p, final_norm_g, final_norm=(layer == DEPTH - 1))
    return h.reshape(BATCH, SEQ, D_MODEL)
```

```python
import functools

import numpy as np
import jax
import jax.numpy as jnp
from jax import lax
from jax.experimental import pallas as pl
from jax.experimental.pallas import tpu as pltpu

F32 = jnp.float32
BF16 = jnp.bfloat16

D_MODEL = 1024
BATCH = 4
SEQ = 4096
TOKENS = BATCH * SEQ
DEPTH = 2
MIX = D_MODEL // 2
GM_CHUNK = 128
GM_GROUPS = 4
GM_GW = MIX // GM_GROUPS
CONV_WIDTH = 4
LRU_BLOCKS = 8
LRU_BW = MIX // LRU_BLOCKS
LRU_C = 8.0
N_HEADS = 8
HEAD_DIM = MIX // N_HEADS
N_KV = 2
HPG = N_HEADS // N_KV
CMP_LEN = 32
CMP_STRIDE = 16
SLC_LEN = 64
SLC_TOPN = 8
WIN = 512
NSA_Q = N_HEADS * HEAD_DIM
NSA_KV = N_KV * HEAD_DIM
POOL_WINDOWS = (2, 4, 8, 16)
POOL_GW = MIX // len(POOL_WINDOWS)
N_BRANCH = 4
N_GROUPS = 4
EXPERTS_PER_GROUP = 4
N_EXPERTS = N_GROUPS * EXPERTS_PER_GROUP
D_EXPERT = D_MODEL // 2
EPS = 1e-6
NEG_INF = -1e30
FORCE_SCORE = 1e6
IN_SPLITS = (MIX, MIX, MIX, MIX, NSA_Q, 6 * NSA_KV, 3 * N_HEADS, MIX, N_BRANCH * D_MODEL)

N_CHUNK = SEQ // CMP_STRIDE
N_CMP = N_CHUNK - CMP_LEN // CMP_STRIDE + 1
N_SLC = SEQ // SLC_LEN

LANES = 128
GATE_PAD = LANES
GATE_ROWS = 16
ROUTER_PAD = LANES
VMEM_LIMIT = 56 * 1024 * 1024

PROJ_WIDTHS = (MIX, MIX, MIX, MIX, NSA_Q, 6 * NSA_KV, MIX, GATE_PAD)
PROJ_TM = 512
GMLP_TM = 512
LRU_TS = 512
POOL_TS = 512
NSA_TQ = 128
SLC_KT = 512
WIN_KEYS = WIN + NSA_TQ
MERGE_TM = 256
ROUTE_TM = 512
MOE_TM = 256
MOE_ROWS = TOKENS + N_GROUPS * MOE_TM
MOE_NT = MOE_ROWS // MOE_TM


def _cparams(sem):
    return pltpu.CompilerParams(dimension_semantics=sem, vmem_limit_bytes=VMEM_LIMIT)


def _rms(x, g):
    return x * lax.rsqrt(jnp.mean(x * x, axis=-1, keepdims=True) + EPS) * g


def _proj_kernel(x_ref, g_ref, w_ref, *out_refs):
    nb = _rms(x_ref[...], g_ref[...]).astype(BF16)
    off = 0
    for ref in out_refs:
        w = ref.shape[-1]
        ref[...] = jnp.dot(nb, w_ref[:, off:off + w], preferred_element_type=F32).astype(ref.dtype)
        off += w


def _proj(x, g, w, layer):
    n_in = sum(PROJ_WIDTHS)
    out_shape = [jax.ShapeDtypeStruct((TOKENS, wd), BF16) for wd in PROJ_WIDTHS[:-1]]
    out_shape.append(jax.ShapeDtypeStruct((TOKENS, GATE_PAD), F32))
    return pl.pallas_call(
        _proj_kernel,
        grid=(TOKENS // PROJ_TM,),
        in_specs=[pl.BlockSpec((PROJ_TM, D_MODEL), lambda i: (i, 0)),
                  pl.BlockSpec((1, D_MODEL), lambda i: (0, 0)),
                  pl.BlockSpec((None, D_MODEL, n_in), lambda i: (layer, 0, 0))],
        out_specs=[pl.BlockSpec((PROJ_TM, wd), lambda i: (i, 0)) for wd in PROJ_WIDTHS],
        out_shape=out_shape,
        compiler_params=_cparams(("arbitrary",)),
        name="proj",
    )(x, g, w)


def _gmlp_kernel(u_ref, v_ref, g_ref, ws_ref, bs_ref, o_ref):
    u = jax.nn.gelu(u_ref[...].astype(F32))
    v = _rms(jax.nn.gelu(v_ref[...].astype(F32)), g_ref[...]).astype(BF16)
    row = lax.broadcasted_iota(jnp.int32, (GM_CHUNK, GM_CHUNK), 0)
    col = lax.broadcasted_iota(jnp.int32, (GM_CHUNK, GM_CHUNK), 1)
    causal = row >= col
    for gi in range(GM_GROUPS):
        w = jnp.where(causal, ws_ref[gi], 0.0).astype(BF16)
        cs = slice(gi * GM_GW, (gi + 1) * GM_GW)
        for c in range(GMLP_TM // GM_CHUNK):
            rs = slice(c * GM_CHUNK, (c + 1) * GM_CHUNK)
            mixed = jnp.dot(w, v[rs, cs], preferred_element_type=F32) + bs_ref[gi]
            o_ref[rs, cs] = (u[rs, cs] * mixed).astype(o_ref.dtype)


def _gmlp(u, v, g, ws, bs):
    tok = lambda i: (i, 0)
    return pl.pallas_call(
        _gmlp_kernel,
        grid=(TOKENS // GMLP_TM,),
        in_specs=[pl.BlockSpec((GMLP_TM, MIX), tok),
                  pl.BlockSpec((GMLP_TM, MIX), tok),
                  pl.BlockSpec((1, MIX), lambda i: (0, 0)),
                  pl.BlockSpec((GM_GROUPS, GM_CHUNK, GM_CHUNK), lambda i: (0, 0, 0)),
                  pl.BlockSpec((GM_GROUPS, GM_CHUNK, GM_GW), lambda i: (0, 0, 0))],
        out_specs=pl.BlockSpec((GMLP_TM, MIX), tok),
        out_shape=jax.ShapeDtypeStruct((TOKENS, MIX), BF16),
        compiler_params=_cparams(("arbitrary",)),
        name="gmlp",
    )(u, v, g, ws, bs)


LRU_TAIL = 8


def _lru_kernel(gb_ref, rb_ref, cw_ref, cb_ref, wa_ref, ba_ref, wx_ref, bx_ref, lam_ref, o_ref,
                tail_ref, h_ref):
    @pl.when(pl.program_id(1) == 0)
    def _():
        tail_ref[...] = jnp.zeros_like(tail_ref)
        h_ref[...] = jnp.zeros_like(h_ref)

    ts = LRU_TS
    x = rb_ref[...].astype(F32)
    ext = jnp.concatenate([tail_ref[...], x], axis=0)
    tail_ref[...] = x[ts - LRU_TAIL:, :]
    xc = cb_ref[...] + x * cw_ref[CONV_WIDTH - 1:CONV_WIDTH, :]
    for d in range(1, CONV_WIDTH):
        xs = pltpu.roll(ext, d, axis=0)[LRU_TAIL:, :]
        xc = xc + xs * cw_ref[CONV_WIDTH - 1 - d:CONV_WIDTH - d, :]
    xcb = xc.astype(BF16)
    r = jax.nn.sigmoid(jnp.dot(xcb, wa_ref[...], preferred_element_type=F32) + ba_ref[...])
    ig = jax.nn.sigmoid(jnp.dot(xcb, wx_ref[...], preferred_element_type=F32) + bx_ref[...])
    z = -lam_ref[...]
    softplus = jnp.maximum(z, 0.0) + jnp.log1p(jnp.exp(-jnp.abs(z)))
    log_a = -LRU_C * r * softplus
    a = jnp.exp(log_a)
    b = jnp.sqrt(1.0 - jnp.exp(2.0 * log_a)) * (ig * xc)
    rows = lax.broadcasted_iota(jnp.int32, (ts, 1), 0)
    d = 1
    while d < ts:
        valid = rows >= d
        a_prev = pltpu.roll(a, d, axis=0)
        b_prev = pltpu.roll(b, d, axis=0)
        b = jnp.where(valid, a * b_prev, 0.0) + b
        a = jnp.where(valid, a * a_prev, a)
        d *= 2
    h = a * h_ref[...] + b
    h_ref[...] = h[ts - 1:ts, :]
    o_ref[...] = (jax.nn.gelu(gb_ref[...].astype(F32)) * h).astype(o_ref.dtype)


def _lru(gb, rb, cw, cb, wa, ba, wx, bx, lam):
    nt = SEQ // LRU_TS
    tok = lambda b, s: (b * nt + s, 0)
    vec = pl.BlockSpec((1, MIX), lambda b, s: (0, 0))
    mat = pl.BlockSpec((MIX, MIX), lambda b, s: (0, 0))
    return pl.pallas_call(
        _lru_kernel,
        grid=(BATCH, nt),
        in_specs=[pl.BlockSpec((LRU_TS, MIX), tok), pl.BlockSpec((LRU_TS, MIX), tok),
                  pl.BlockSpec((CONV_WIDTH, MIX), lambda b, s: (0, 0)), vec,
                  mat, vec, mat, vec, vec],
        out_specs=pl.BlockSpec((LRU_TS, MIX), tok),
        out_shape=jax.ShapeDtypeStruct((TOKENS, MIX), BF16),
        scratch_shapes=[pltpu.VMEM((LRU_TAIL, MIX), F32), pltpu.VMEM((1, MIX), F32)],
        compiler_params=_cparams(("arbitrary", "arbitrary")),
        name="rglru",
    )(gb, rb, cw, cb, wa, ba, wx, bx, lam)


POOL_TAIL = 16


def _pool_kernel(x_ref, w_ref, sc_ref, o_ref, tail_ref):
    s_id = pl.program_id(1)

    @pl.when(s_id == 0)
    def _():
        tail_ref[...] = jnp.zeros_like(tail_ref)

    ts = POOL_TS
    x = x_ref[...].astype(F32)
    ext = jnp.concatenate([tail_ref[...], x], axis=0)
    tail_ref[...] = x[ts - POOL_TAIL:, :]
    pos = s_id * ts + lax.broadcasted_iota(jnp.int32, (ts, 1), 0)
    acc = ext
    width = 1
    for gi, wdw in enumerate(POOL_WINDOWS):
        while width < wdw:
            acc = acc + pltpu.roll(acc, width, axis=0)
            width *= 2
        cs = slice(gi * POOL_GW, (gi + 1) * POOL_GW)
        cnt = jnp.minimum(pos + 1, wdw).astype(F32)
        pooled = acc[POOL_TAIL:, cs] / cnt - x[:, cs]
        mixed = jnp.dot(pooled.astype(BF16), w_ref[gi], preferred_element_type=F32)
        o_ref[:, cs] = (mixed * sc_ref[:, cs]).astype(o_ref.dtype)


def _pool(xd, w, sc):
    nt = SEQ // POOL_TS
    tok = lambda b, s: (b * nt + s, 0)
    return pl.pallas_call(
        _pool_kernel,
        grid=(BATCH, nt),
        in_specs=[pl.BlockSpec((POOL_TS, MIX), tok),
                  pl.BlockSpec((len(POOL_WINDOWS), POOL_GW, POOL_GW), lambda b, s: (0, 0, 0)),
                  pl.BlockSpec((1, MIX), lambda b, s: (0, 0))],
        out_specs=pl.BlockSpec((POOL_TS, MIX), tok),
        out_shape=jax.ShapeDtypeStruct((TOKENS, MIX), BF16),
        scratch_shapes=[pltpu.VMEM((POOL_TAIL, MIX), F32)],
        compiler_params=_cparams(("arbitrary", "arbitrary")),
        name="pool",
    )(xd, w, sc)


HALF_FLAT = CMP_STRIDE * HEAD_DIM


def _cmpkv_kernel(x_ref, pe_ref, w1_ref, w2_ref, o_ref):
    x = x_ref[0, 0, 0].astype(F32)
    pe = pe_ref[0]
    w1 = w1_ref[0, 0]
    first = jnp.dot((x + pe[0:1, :]).astype(BF16), w1[:HALF_FLAT, :], preferred_element_type=F32)
    second = jnp.dot((x + pe[1:2, :]).astype(BF16), w1[HALF_FLAT:, :], preferred_element_type=F32)
    hid = jax.nn.gelu(first + pltpu.roll(second, N_CHUNK - 1, axis=0))
    out = jnp.dot(hid.astype(BF16), w2_ref[0, 0], preferred_element_type=F32)
    row = lax.broadcasted_iota(jnp.int32, (N_CHUNK, 1), 0)
    o_ref[0, 0, 0] = jnp.where(row < N_CMP, out, 0.0).astype(o_ref.dtype)


def _cmpkv(x, pe, w1, w2):
    return pl.pallas_call(
        _cmpkv_kernel,
        grid=(2, BATCH, N_KV),
        in_specs=[pl.BlockSpec((1, 1, 1, N_CHUNK, HALF_FLAT), lambda c, b, g: (c, b, g, 0, 0)),
                  pl.BlockSpec((1, 2, HALF_FLAT), lambda c, b, g: (c, 0, 0)),
                  pl.BlockSpec((1, 1, CMP_LEN * HEAD_DIM, HEAD_DIM), lambda c, b, g: (c, g, 0, 0)),
                  pl.BlockSpec((1, 1, HEAD_DIM, HEAD_DIM), lambda c, b, g: (c, g, 0, 0))],
        out_specs=pl.BlockSpec((1, 1, 1, N_CHUNK, HEAD_DIM), lambda c, b, g: (c, b, g, 0, 0)),
        out_shape=jax.ShapeDtypeStruct((2, BATCH, N_KV, N_CHUNK, HEAD_DIM), BF16),
        compiler_params=_cparams(("arbitrary", "arbitrary", "arbitrary")),
        name="cmpkv",
    )(x, pe, w1, w2)


def _lane_tile(x, n):
    return jnp.concatenate([x] * n, axis=1)


ONES_ROWS = 16


def _with_ones(v_t):
    return jnp.concatenate([v_t, jnp.ones((ONES_ROWS, v_t.shape[1]), BF16)], axis=0)


def _nsa_kernel(q_ref, kc_ref, vct_ref, ks_ref, vst_ref, kw_ref, vwt_ref, gt_ref, ovt_ref, blk_ref,
                y_ref):
    i = pl.program_id(1)
    tq = NSA_TQ
    nl = HPG * tq
    t_row = i * tq + lax.broadcasted_iota(jnp.int32, (1, tq), 1)
    groups = range(N_KV)
    vrows = [slice(g * HEAD_DIM, (g + 1) * HEAD_DIM) for g in groups]

    def nt_dot(k, qm):
        return lax.dot_general(k, qm, (((1,), (1,)), ((), ())), preferred_element_type=F32)

    def normalized(acc):
        return acc[:HEAD_DIM, :] / acc[HEAD_DIM:HEAD_DIM + 1, :]

    lane = lax.broadcasted_iota(jnp.int32, (tq, LANES), 1)
    scale = HEAD_DIM ** -0.5
    q4 = [jnp.concatenate(
        [jnp.where((lane >= HEAD_DIM) == (g == 1), q_ref[:, j * LANES:(j + 1) * LANES], 0) * scale
         for j in range(HPG)], axis=0).astype(BF16) for g in groups]

    n_idx = lax.broadcasted_iota(jnp.int32, (N_CHUNK, 1), 0)
    ok = n_idx * CMP_STRIDE + (CMP_LEN - 1) <= _lane_tile(t_row, HPG)
    blk = lax.broadcasted_iota(jnp.int32, (N_SLC, 1), 0)
    cur = jnp.right_shift(t_row, 6)
    forced = (blk == 0) | (blk == cur) | (blk == cur - 1)
    causal_blk = blk * SLC_LEN <= t_row
    o_cmp, q_aug = [], []
    for g in groups:
        s = jnp.where(ok, nt_dot(kc_ref[0], q4[g]), NEG_INF)
        m = jnp.max(s, axis=0, keepdims=True)
        e = jnp.where(ok, jnp.exp(s - m), 0.0)
        l = jnp.sum(e, axis=0, keepdims=True)
        p = e / jnp.where(l > 0.0, l, 1.0)
        o_cmp.append(jnp.dot(vct_ref[0, vrows[g], :], p.astype(BF16), preferred_element_type=F32))
        psum = p[:, 0:tq]
        for j in range(1, HPG):
            psum = psum + p[:, j * tq:(j + 1) * tq]
        p_hi = psum.astype(BF16)
        p_lo = (psum - p_hi.astype(F32)).astype(BF16)
        imp = (jnp.dot(ovt_ref[...], p_hi, preferred_element_type=F32)
               + jnp.dot(ovt_ref[...], p_lo, preferred_element_type=F32))
        work = jnp.where(forced, FORCE_SCORE, jnp.where(causal_blk, imp, -1.0))
        sel = jnp.zeros((N_SLC, tq), F32)
        for _ in range(SLC_TOPN):
            mx = jnp.max(work, axis=0, keepdims=True)
            idx = jnp.min(jnp.where(work == mx, blk, N_SLC), axis=0, keepdims=True)
            pick = blk == idx
            sel = jnp.where(pick & (mx >= 0.0), 1.0, sel)
            work = jnp.where(pick, -3e38, work)
        sel_bias = jnp.where(sel > 0.0, 0.0, NEG_INF)
        sel_bias = jnp.concatenate([sel_bias, jnp.zeros((LANES - N_SLC, tq), F32)], axis=0).T
        sel_bias = sel_bias.astype(BF16)
        q_aug.append(jnp.concatenate([q4[g], jnp.concatenate([sel_bias] * HPG, axis=0)], axis=1))

    def slc_scores(kt):
        k0 = pl.multiple_of(kt * SLC_KT, SLC_KT)
        k_aug = jnp.concatenate([ks_ref[pl.ds(k0, SLC_KT), :], blk_ref[pl.ds(k0, SLC_KT), :]],
                                axis=1)
        return [nt_dot(k_aug, q_aug[g]) for g in groups]

    def slc_update(kt, state, s_t):
        k0 = pl.multiple_of(kt * SLC_KT, SLC_KT)
        out = []
        for g in groups:
            m_i, acc = state[2 * g], state[2 * g + 1]
            m_new = jnp.maximum(m_i, jnp.max(s_t[g], axis=0, keepdims=True))
            alpha = jnp.exp(m_i - m_new)
            pT = jnp.exp((s_t[g] - m_new).astype(BF16))
            vT = _with_ones(vst_ref[vrows[g], pl.ds(k0, SLC_KT)])
            out += [m_new, alpha * acc + jnp.dot(vT, pT, preferred_element_type=F32)]
        return out

    def slc_step(kt, carry):
        s_next = slc_scores(kt + 1)
        return tuple(slc_update(kt, carry[:2 * N_KV], carry[2 * N_KV:]) + s_next)

    last_kt = (i * tq) // SLC_KT
    init = ((jnp.full((1, nl), -3e38, F32), jnp.zeros((HEAD_DIM + ONES_ROWS, nl), F32)) * N_KV
            + tuple(slc_scores(0)))
    carry = lax.fori_loop(0, last_kt, slc_step, init)
    kpos = last_kt * SLC_KT + lax.broadcasted_iota(jnp.int32, (SLC_KT, 1), 0)
    causal_bias = _lane_tile(jnp.where(kpos <= t_row, 0.0, NEG_INF), HPG)
    state = slc_update(last_kt, carry[:2 * N_KV], [s + causal_bias for s in carry[2 * N_KV:]])
    o_slc = [normalized(state[2 * g + 1]) for g in groups]

    start = pl.multiple_of(jnp.maximum(i - WIN // tq, 0) * tq, tq)
    k_win = kw_ref[pl.ds(start, WIN_KEYS), :]
    delta = t_row - (start + lax.broadcasted_iota(jnp.int32, (WIN_KEYS, 1), 0))
    win_bias = _lane_tile(jnp.where((delta >= 0) & (delta < WIN), 0.0, NEG_INF), HPG)
    o_win = []
    for g in groups:
        sT = nt_dot(k_win, q4[g]) + win_bias
        pT = jnp.exp((sT - jnp.max(sT, axis=0, keepdims=True)).astype(BF16))
        vT = _with_ones(vwt_ref[vrows[g], pl.ds(start, WIN_KEYS)])
        o_win.append(normalized(jnp.dot(vT, pT, preferred_element_type=F32)))

    for g in groups:
        gates = jax.nn.sigmoid(gt_ref[0, g])

        def gate_row(br):
            return jnp.concatenate([gates[3 * j + br:3 * j + br + 1, :] for j in range(HPG)],
                                   axis=1)

        o = gate_row(0) * o_cmp[g] + gate_row(1) * o_slc[g] + gate_row(2) * o_win[g]
        for j in range(HPG):
            h0 = (g * HPG + j) * HEAD_DIM
            y_ref[0, h0:h0 + HEAD_DIM, :] = o[:, j * tq:(j + 1) * tq].astype(y_ref.dtype)


def _block_onehot():
    oh = (np.arange(SEQ)[:, None] // SLC_LEN) == np.arange(LANES)[None, :]
    return jnp.asarray(oh.astype(np.float32), dtype=BF16)


def _nsa(q, kc, vct, kv, vst, vwt, gt, ovt):
    nq = SEQ // NSA_TQ
    return pl.pallas_call(
        _nsa_kernel,
        grid=(BATCH, nq),
        in_specs=[pl.BlockSpec((NSA_TQ, NSA_Q), lambda b, i: (b * nq + i, 0)),
                  pl.BlockSpec((1, N_CHUNK, NSA_KV), lambda b, i: (b, 0, 0)),
                  pl.BlockSpec((1, NSA_KV, N_CHUNK), lambda b, i: (b, 0, 0)),
                  pl.BlockSpec((SEQ, NSA_KV), lambda b, i: (b, 2)),
                  pl.BlockSpec((None, NSA_KV, SEQ), lambda b, i: (b, 0, 0)),
                  pl.BlockSpec((SEQ, NSA_KV), lambda b, i: (b, 4)),
                  pl.BlockSpec((None, NSA_KV, SEQ), lambda b, i: (b, 0, 0)),
                  pl.BlockSpec((1, N_KV, GATE_ROWS, NSA_TQ), lambda b, i: (b, 0, 0, i)),
                  pl.BlockSpec((N_SLC, N_CHUNK), lambda b, i: (0, 0)),
                  pl.BlockSpec((SEQ, LANES), lambda b, i: (0, 0))],
        out_specs=pl.BlockSpec((1, NSA_Q, NSA_TQ), lambda b, i: (b, 0, i)),
        out_shape=jax.ShapeDtypeStruct((BATCH, NSA_Q, SEQ), BF16),
        compiler_params=_cparams(("arbitrary", "arbitrary")),
        name="nsa",
    )(q, kc, vct, kv, vst, kv, vwt, gt, ovt, _block_onehot())


def _overlap_t():
    c_start = np.arange(N_CHUNK) * CMP_STRIDE
    s_start = np.arange(N_SLC) * SLC_LEN
    ov = ((c_start[None, :] <= s_start[:, None] + SLC_LEN - 1)
          & (c_start[None, :] + CMP_LEN - 1 >= s_start[:, None])
          & (np.arange(N_CHUNK)[None, :] < N_CMP))
    return jnp.asarray(ov.astype(np.float32), dtype=BF16)


def _merge_kernel(x_ref, g_ref, wmg_ref, ya_ref, yb_ref, yct_ref, yd_ref, wb_ref, wo_ref, o_ref):
    x = x_ref[...]
    nb = _rms(x, g_ref[...]).astype(BF16)
    yc = yct_ref[0].astype(F32).T.astype(BF16)
    ys = (ya_ref[...], yb_ref[...], yc, yd_ref[...])
    merged = jnp.zeros((MERGE_TM, D_MODEL), F32)
    for bi, y in enumerate(ys):
        gate = jax.nn.sigmoid(jnp.dot(nb, wmg_ref[:, bi * D_MODEL:(bi + 1) * D_MODEL],
                                      preferred_element_type=F32))
        merged = merged + gate * jnp.dot(y, wb_ref[bi], preferred_element_type=F32)
    o_ref[...] = x + jnp.dot(merged.astype(BF16), wo_ref[...], preferred_element_type=F32)


def _merge(x, g, wmg, ya, yb, yct, yd, wb, wo, layer):
    nt = SEQ // MERGE_TM
    tok = lambda i: (i, 0)
    const2 = lambda i: (0, 0)
    return pl.pallas_call(
        _merge_kernel,
        grid=(TOKENS // MERGE_TM,),
        in_specs=[pl.BlockSpec((MERGE_TM, D_MODEL), tok),
                  pl.BlockSpec((1, D_MODEL), const2),
                  pl.BlockSpec((None, D_MODEL, N_BRANCH * D_MODEL), lambda i: (layer, 0, 0)),
                  pl.BlockSpec((MERGE_TM, MIX), tok),
                  pl.BlockSpec((MERGE_TM, MIX), tok),
                  pl.BlockSpec((1, MIX, MERGE_TM), lambda i: (i // nt, 0, i % nt)),
                  pl.BlockSpec((MERGE_TM, MIX), tok),
                  pl.BlockSpec((None, N_BRANCH, MIX, D_MODEL), lambda i: (layer, 0, 0, 0)),
                  pl.BlockSpec((None, D_MODEL, D_MODEL), lambda i: (layer, 0, 0))],
        out_specs=pl.BlockSpec((MERGE_TM, D_MODEL), tok),
        out_shape=jax.ShapeDtypeStruct((TOKENS, D_MODEL), F32),
        compiler_params=_cparams(("arbitrary",)),
        name="merge",
    )(x, g, wmg, ya, yb, yct, yd, wb, wo)


def _router_logits(t, wr_ref, br_ref):
    return jnp.dot(t, wr_ref[...], preferred_element_type=F32,
                   precision=lax.Precision.HIGHEST) + br_ref[...]


def _top_group(logits):
    lane = lax.broadcasted_iota(jnp.int32, logits.shape, 1)
    is_grp = lane < N_GROUPS
    lg = jnp.where(is_grp, logits, NEG_INF)
    gmax = jnp.max(lg, axis=1, keepdims=True)
    grp_idx = jnp.min(jnp.where(is_grp & (lg == gmax), lane, ROUTER_PAD), axis=1, keepdims=True)
    return lg, gmax, grp_idx


def _combine_weights(logits, grp_idx):
    lane = lax.broadcasted_iota(jnp.int32, logits.shape, 1)
    lg, gmax, _ = _top_group(logits)
    ge = jnp.where(lane < N_GROUPS, jnp.exp(lg - gmax), 0.0)
    grp_w = (jnp.sum(jnp.where(lane == grp_idx, ge, 0.0), axis=1, keepdims=True)
             / jnp.sum(ge, axis=1, keepdims=True))
    e_idx = lane - N_GROUPS
    in_grp = (e_idx >= grp_idx * EXPERTS_PER_GROUP) & (e_idx < (grp_idx + 1) * EXPERTS_PER_GROUP)
    le = jnp.where(in_grp, logits, NEG_INF)
    l1 = jnp.max(le, axis=1, keepdims=True)
    i1 = jnp.min(jnp.where(in_grp & (le == l1), lane, ROUTER_PAD), axis=1, keepdims=True)
    le2 = jnp.where(lane == i1, NEG_INF, le)
    l2 = jnp.max(le2, axis=1, keepdims=True)
    i2 = jnp.min(jnp.where(in_grp & (lane != i1) & (le2 == l2), lane, ROUTER_PAD), axis=1,
                 keepdims=True)
    e2 = jnp.exp(l2 - l1)
    den = 1.0 + e2
    w1 = grp_w / den
    w2 = grp_w * e2 / den
    return jnp.where(lane == i1, w1, 0.0) + jnp.where(lane == i2, w2, 0.0)


def _route_kernel(x_ref, g_ref, wr_ref, br_ref, tri_ref, meta_ref, cnt_ref, run_ref):
    @pl.when(pl.program_id(0) == 0)
    def _():
        run_ref[...] = jnp.zeros_like(run_ref)

    logits = _router_logits(_rms(x_ref[...], g_ref[...]), wr_ref, br_ref)
    _, _, grp_idx = _top_group(logits)
    lane = lax.broadcasted_iota(jnp.int32, logits.shape, 1)
    onehot = jnp.where(lane == grp_idx, 1.0, 0.0)
    before = jnp.dot(tri_ref[...], onehot.astype(BF16), preferred_element_type=F32) + run_ref[...]
    rank = jnp.sum(onehot * before, axis=1, keepdims=True).astype(jnp.int32)
    run_ref[...] += jnp.sum(onehot, axis=0, keepdims=True)
    meta_ref[...] = jnp.where(lane == 0, grp_idx, jnp.where(lane == 1, rank, 0))
    cnt_ref[...] = run_ref[...]


def _route(x, g, wr, br):
    tok = lambda i: (i, 0)
    const2 = lambda i: (0, 0)
    tri = jnp.asarray(np.tril(np.ones((ROUTE_TM, ROUTE_TM), np.float32), -1), dtype=BF16)
    return pl.pallas_call(
        _route_kernel,
        grid=(TOKENS // ROUTE_TM,),
        in_specs=[pl.BlockSpec((ROUTE_TM, D_MODEL), tok),
                  pl.BlockSpec((1, D_MODEL), const2),
                  pl.BlockSpec((D_MODEL, ROUTER_PAD), const2),
                  pl.BlockSpec((1, ROUTER_PAD), const2),
                  pl.BlockSpec((ROUTE_TM, ROUTE_TM), const2)],
        out_specs=[pl.BlockSpec((ROUTE_TM, ROUTER_PAD), tok), pl.BlockSpec((1, ROUTER_PAD), const2)],
        out_shape=[jax.ShapeDtypeStruct((TOKENS, ROUTER_PAD), jnp.int32),
                   jax.ShapeDtypeStruct((1, ROUTER_PAD), F32)],
        scratch_shapes=[pltpu.VMEM((1, ROUTER_PAD), F32)],
        compiler_params=_cparams(("arbitrary",)),
        name="route",
    )(x, g, wr, br, tri)


def _experts_kernel(tok_ref, tg_ref, tv_ref, h_hbm, g_ref, wr_ref, br_ref, wg_ref, wu_ref, wd_ref,
                    fg_ref, out_hbm, xbuf, obuf, gsem, ssem, *, final_norm):
    j = pl.program_id(0)
    nt = pl.num_programs(0)
    slot = lax.rem(j, 2)
    tm = MOE_TM

    def gather_copy(tile, sl, r):
        return pltpu.make_async_copy(h_hbm.at[pl.ds(tok_ref[tile * tm + r], 1)],
                                     xbuf.at[sl, pl.ds(r, 1)], gsem.at[sl])

    def scatter_copy(tile, sl, r):
        return pltpu.make_async_copy(obuf.at[sl, pl.ds(r, 1)],
                                     out_hbm.at[pl.ds(tok_ref[tile * tm + r], 1)], ssem.at[sl])

    def gather_all(sl):
        return pltpu.make_async_copy(h_hbm.at[pl.ds(0, tm)], xbuf.at[sl], gsem.at[sl])

    def scatter_all(sl):
        return pltpu.make_async_copy(obuf.at[sl], out_hbm.at[pl.ds(0, tm)], ssem.at[sl])

    def for_rows(n, fn):
        def body(r, c):
            fn(r)
            return c
        lax.fori_loop(0, n, body, 0)

    def start_gather(tile, sl):
        for r in range(tm):
            gather_copy(tile, sl, r).start()

    def wait_scatter(tile, sl):
        n = tv_ref[tile]

        @pl.when(n == tm)
        def _():
            scatter_all(sl).wait()

        @pl.when(n < tm)
        def _():
            for_rows(n, lambda r: scatter_copy(tile, sl, r).wait())

    @pl.when(j == 0)
    def _():
        start_gather(0, 0)

    @pl.when(j >= 2)
    def _():
        wait_scatter(j - 2, slot)

    gather_all(slot).wait()
    nv = tv_ref[j]

    def compute():
        x = xbuf[slot]
        t = _rms(x, g_ref[...])
        grp = tg_ref[j]
        comb = _combine_weights(_router_logits(t, wr_ref, br_ref), grp)
        lane = lax.broadcasted_iota(jnp.int32, comb.shape, 1)
        tb = t.astype(BF16)
        acc = jnp.zeros((tm, D_MODEL), F32)
        for e in range(EXPERTS_PER_GROUP):
            ce = jnp.sum(jnp.where(lane == N_GROUPS + grp * EXPERTS_PER_GROUP + e, comb, 0.0),
                         axis=1, keepdims=True)
            hid = (jax.nn.silu(jnp.dot(tb, wg_ref[e], preferred_element_type=F32))
                   * jnp.dot(tb, wu_ref[e], preferred_element_type=F32)) * ce
            acc = acc + jnp.dot(hid.astype(BF16), wd_ref[e], preferred_element_type=F32)
        h = x + acc
        if final_norm:
            h = _rms(h, fg_ref[...])
        obuf[slot] = h

    @pl.when(nv == tm)
    def _():
        start_gather(j + 1, 1 - slot)
        compute()
        for r in range(tm):
            scatter_copy(j, slot, r).start()

    @pl.when((nv > 0) & (nv < tm))
    def _():
        start_gather(j + 1, 1 - slot)
        compute()
        for_rows(nv, lambda r: scatter_copy(j, slot, r).start())

    @pl.when(nv == 0)
    def _():
        start_gather(j + 1, 1 - slot)

    @pl.when(j == nt - 1)
    def _():
        wait_scatter(j - 1, 1 - slot)
        wait_scatter(j, slot)
        gather_all(1 - slot).wait()


def _experts(tok, tile_group, tile_valid, h, g, wr, br, wg, wu, wd, fg, layer, final_norm):
    const2 = lambda j, *_: (0, 0)
    grp_w = lambda j, tok, tg, tv: (layer * N_GROUPS + tg[j], 0, 0)
    grid_spec = pltpu.PrefetchScalarGridSpec(
        num_scalar_prefetch=3,
        grid=(MOE_NT,),
        in_specs=[pl.BlockSpec(memory_space=pl.ANY),
                  pl.BlockSpec((1, D_MODEL), const2),
                  pl.BlockSpec((D_MODEL, ROUTER_PAD), const2),
                  pl.BlockSpec((1, ROUTER_PAD), const2),
                  pl.BlockSpec((EXPERTS_PER_GROUP, D_MODEL, D_EXPERT), grp_w),
                  pl.BlockSpec((EXPERTS_PER_GROUP, D_MODEL, D_EXPERT), grp_w),
                  pl.BlockSpec((EXPERTS_PER_GROUP, D_EXPERT, D_MODEL), grp_w),
                  pl.BlockSpec((1, D_MODEL), const2)],
        out_specs=pl.BlockSpec(memory_space=pl.ANY),
        scratch_shapes=[pltpu.VMEM((2, MOE_TM, D_MODEL), F32),
                        pltpu.VMEM((2, MOE_TM, D_MODEL), F32),
                        pltpu.SemaphoreType.DMA((2,)),
                        pltpu.SemaphoreType.DMA((2,))])
    return pl.pallas_call(
        functools.partial(_experts_kernel, final_norm=final_norm),
        grid_spec=grid_spec,
        out_shape=jax.ShapeDtypeStruct((TOKENS, D_MODEL), F32),
        compiler_params=_cparams(("arbitrary",)),
        name="experts",
    )(tok, tile_group, tile_valid, h, g, wr, br, wg, wu, wd, fg)


def _moe(x, g, wr, br, wg, wu, wd, fg, layer, final_norm):
    meta, cnt = _route(x, g, wr, br)
    grp, rank = meta[:, 0], meta[:, 1]
    counts = cnt[0, :N_GROUPS].astype(jnp.int32)
    padded = (counts + MOE_TM - 1) // MOE_TM * MOE_TM
    ends = jnp.cumsum(padded)
    starts = ends - padded
    pos = starts[grp] + rank
    tok = jnp.zeros((MOE_ROWS + MOE_TM,), jnp.int32).at[pos].set(
        jnp.arange(TOKENS, dtype=jnp.int32))
    tile_start = jnp.arange(MOE_NT, dtype=jnp.int32) * MOE_TM
    tile_group = jnp.minimum(jnp.sum(tile_start[:, None] >= ends[None, :], axis=1), N_GROUPS - 1)
    tile_group = tile_group.astype(jnp.int32)
    tile_valid = jnp.clip(starts[tile_group] + counts[tile_group] - tile_start, 0, MOE_TM)
    return _experts(tok, tile_group, tile_valid.astype(jnp.int32), x, g, wr, br, wg, wu, wd, fg,
                    layer, final_norm)


def _block_diag(w):
    eye = jnp.eye(LRU_BLOCKS, dtype=w.dtype)
    return jnp.einsum('hij,hk->hikj', w, eye).reshape(MIX, MIX)


def _split_w_in(w_in):
    cuts = [int(c) for c in np.cumsum((0,) + IN_SPLITS)]
    cols = lambda a, b: w_in[:, :, a:b]
    q_parts = [cols(cuts[4] + (g * HPG + j) * HEAD_DIM, cuts[4] + (g * HPG + j + 1) * HEAD_DIM)
               for j in range(HPG) for g in range(N_KV)]
    gate_pad = jnp.zeros((DEPTH, D_MODEL, GATE_PAD - IN_SPLITS[6]), w_in.dtype)
    w_proj = jnp.concatenate([cols(cuts[0], cuts[4])] + q_parts
                             + [cols(cuts[5], cuts[6]), cols(cuts[7], cuts[8]),
                                cols(cuts[6], cuts[7]), gate_pad], axis=2)
    return w_proj.astype(BF16), cols(cuts[8], cuts[9]).astype(BF16)


def _layer(h, p, big, layer, final_g, final_norm):
    row = lambda a: a.reshape(1, -1)

    u, v, gb, rb, q, kv, xd, ng = _proj(h, row(p['norm1_g']), big['w_proj'], layer)

    bs = jnp.broadcast_to(p['gm_b'][:, :, None], (GM_GROUPS, GM_CHUNK, GM_GW))
    y_a = _gmlp(u, v, row(p['gm_norm_g']), p['gm_ws'], bs)

    y_b = _lru(gb, rb, p['conv_w'], row(p['conv_b']), _block_diag(p['lru_wa']).astype(BF16),
               row(p['lru_ba']), _block_diag(p['lru_wx']).astype(BF16), row(p['lru_bx']),
               row(p['lru_lambda']))

    y_d = _pool(xd, p['pool_w'].astype(BF16), row(p['pool_scale']))

    kvc = kv[:, :2 * NSA_KV].reshape(BATCH, SEQ, 2, N_KV, HEAD_DIM).transpose(2, 0, 3, 1, 4)
    kvc = kvc.reshape(2, BATCH, N_KV, N_CHUNK, HALF_FLAT)
    pe = p['cmp_pe'].reshape(2, 2, HALF_FLAT)
    cmp = _cmpkv(kvc, pe, p['cmp_w1'].astype(BF16), p['cmp_w2'].astype(BF16))
    kc = cmp[0].transpose(0, 2, 1, 3).reshape(BATCH, N_CHUNK, NSA_KV)
    vct = cmp[1].transpose(0, 1, 3, 2).reshape(BATCH, NSA_KV, N_CHUNK)
    kv3 = kv.reshape(BATCH, SEQ, 6 * NSA_KV)
    vst = kv3[:, :, 3 * NSA_KV:4 * NSA_KV].transpose(0, 2, 1)
    vwt = kv3[:, :, 5 * NSA_KV:6 * NSA_KV].transpose(0, 2, 1)
    gt = ng[:, :3 * N_HEADS].reshape(BATCH, SEQ, N_KV, 3 * HPG).transpose(0, 2, 3, 1)
    gt = jnp.pad(gt, ((0, 0), (0, 0), (0, GATE_ROWS - 3 * HPG), (0, 0)))
    y_ct = _nsa(q, kc, vct, kv, vst, vwt, gt, _overlap_t())

    h = _merge(h, row(p['norm1_g']), big['w_mg'], y_a, y_b, y_ct, y_d, big['w_branch'],
               big['w_out'], layer)

    wr = jnp.concatenate([p['router_w_group'], p['router_w_expert']], axis=1)
    wr = jnp.pad(wr, ((0, 0), (0, ROUTER_PAD - wr.shape[1])))
    br = jnp.concatenate([p['router_b_group'], p['router_b_expert']])
    br = jnp.pad(br, (0, ROUTER_PAD - br.shape[0])).reshape(1, ROUTER_PAD)
    return _moe(h, row(p['norm2_g']), wr, br, big['moe_w_gate'], big['moe_w_up'],
                big['moe_w_down'], row(final_g), layer, final_norm)


_LAYER_PARAMS = ('norm1_g', 'gm_norm_g', 'gm_ws', 'gm_b', 'conv_w', 'conv_b', 'lru_wa',
                 'lru_ba', 'lru_wx', 'lru_bx', 'lru_lambda', 'cmp_pe', 'cmp_w1', 'cmp_w2', 'pool_w',
                 'pool_scale', 'norm2_g', 'router_w_group', 'router_b_group',
                 'router_w_expert', 'router_b_expert')


def kernel(x, norm1_g, w_in, gm_norm_g, gm_ws, gm_b, conv_w, conv_b, lru_wa, lru_ba, lru_wx,
           lru_bx, lru_lambda, cmp_pe, cmp_w1, cmp_w2, pool_w, pool_scale, w_branch, w_out,
           norm2_g, router_w_group, router_b_group, router_w_expert, router_b_expert,
           moe_w_gate, moe_w_up, moe_w_down, final_norm_g):
    stacked = dict(zip(_LAYER_PARAMS, (
        norm1_g, gm_norm_g, gm_ws, gm_b, conv_w, conv_b, lru_wa, lru_ba, lru_wx, lru_bx,
        lru_lambda, cmp_pe, cmp_w1, cmp_w2, pool_w, pool_scale, norm2_g,
        router_w_group, router_b_group, router_w_expert, router_b_expert)))
    w_proj, w_mg = _split_w_in(w_in)
    experts = lambda w: w.astype(BF16).reshape((DEPTH * N_EXPERTS,) + w.shape[2:])
    big = dict(w_proj=w_proj, w_mg=w_mg, w_branch=w_branch.astype(BF16), w_out=w_out.astype(BF16),
               moe_w_gate=experts(moe_w_gate), moe_w_up=experts(moe_w_up),
               moe_w_down=experts(moe_w_down))
    h = x.reshape(TOKENS, D_MODEL)
    for layer in range(DEPTH):
        p = {k: a[layer] for k, a in stacked.items()}
        h = _layer(h, p, big, layer, final_norm_g, final_norm=(layer == DEPTH - 1))
    return h.reshape(BATCH, SEQ, D_MODEL)
```

```python
import functools

import numpy as np
import jax
import jax.numpy as jnp
from jax import lax
from jax.experimental import pallas as pl
from jax.experimental.pallas import tpu as pltpu

F32 = jnp.float32
BF16 = jnp.bfloat16

D_MODEL = 1024
BATCH = 4
SEQ = 4096
TOKENS = BATCH * SEQ
DEPTH = 2
MIX = D_MODEL // 2
GM_CHUNK = 128
GM_GROUPS = 4
GM_GW = MIX // GM_GROUPS
CONV_WIDTH = 4
LRU_BLOCKS = 8
LRU_BW = MIX // LRU_BLOCKS
LRU_C = 8.0
N_HEADS = 8
HEAD_DIM = MIX // N_HEADS
N_KV = 2
HPG = N_HEADS // N_KV
CMP_LEN = 32
CMP_STRIDE = 16
SLC_LEN = 64
SLC_TOPN = 8
MAX_FORCED = 3
WIN = 512
NSA_Q = N_HEADS * HEAD_DIM
NSA_KV = N_KV * HEAD_DIM
POOL_WINDOWS = (2, 4, 8, 16)
POOL_GW = MIX // len(POOL_WINDOWS)
N_BRANCH = 4
N_GROUPS = 4
EXPERTS_PER_GROUP = 4
N_EXPERTS = N_GROUPS * EXPERTS_PER_GROUP
D_EXPERT = D_MODEL // 2
EPS = 1e-6
NEG_INF = -1e30
FORCE_SCORE = 1e6
IN_SPLITS = (MIX, MIX, MIX, MIX, NSA_Q, 6 * NSA_KV, 3 * N_HEADS, MIX, N_BRANCH * D_MODEL)

N_CHUNK = SEQ // CMP_STRIDE
N_CMP = N_CHUNK - CMP_LEN // CMP_STRIDE + 1
N_SLC = SEQ // SLC_LEN

LANES = 128
GATE_PAD = LANES
GATE_ROWS = 16
ROUTER_PAD = LANES
VMEM_LIMIT = 56 * 1024 * 1024

PROJ_WIDTHS = (MIX, MIX, MIX, MIX, NSA_Q, 6 * NSA_KV, MIX, GATE_PAD)
PROJ_TM = 512
GMLP_TM = 512
LRU_TS = 512
POOL_TS = 512
NSA_TQ = 128
SLC_KT = 512
WIN_KEYS = WIN + NSA_TQ
MERGE_TM = 256
ROUTE_TM = 512
MOE_TM = 256
MOE_ROWS = TOKENS + N_GROUPS * MOE_TM
MOE_NT = MOE_ROWS // MOE_TM
ROW_CHUNK = 256


def _cparams(sem):
    return pltpu.CompilerParams(dimension_semantics=sem, vmem_limit_bytes=VMEM_LIMIT)


def _rms(x, g):
    return x * lax.rsqrt(jnp.mean(x * x, axis=-1, keepdims=True) + EPS) * g


def _proj_kernel(x_ref, g_ref, w_ref, *out_refs):
    nb = _rms(x_ref[...], g_ref[...]).astype(BF16)
    off = 0
    for ref in out_refs:
        w = ref.shape[-1]
        ref[...] = jnp.dot(nb, w_ref[:, off:off + w], preferred_element_type=F32).astype(ref.dtype)
        off += w


def _proj(x, g, w, layer):
    n_in = sum(PROJ_WIDTHS)
    out_shape = [jax.ShapeDtypeStruct((TOKENS, wd), BF16) for wd in PROJ_WIDTHS[:-1]]
    out_shape.append(jax.ShapeDtypeStruct((TOKENS, GATE_PAD), F32))
    return pl.pallas_call(
        _proj_kernel,
        grid=(TOKENS // PROJ_TM,),
        in_specs=[pl.BlockSpec((PROJ_TM, D_MODEL), lambda i: (i, 0)),
                  pl.BlockSpec((1, D_MODEL), lambda i: (0, 0)),
                  pl.BlockSpec((None, D_MODEL, n_in), lambda i: (layer, 0, 0))],
        out_specs=[pl.BlockSpec((PROJ_TM, wd), lambda i: (i, 0)) for wd in PROJ_WIDTHS],
        out_shape=out_shape,
        compiler_params=_cparams(("arbitrary",)),
        name="proj",
    )(x, g, w)


def _gmlp_kernel(u_ref, v_ref, g_ref, ws_ref, bs_ref, o_ref):
    u = jax.nn.gelu(u_ref[...].astype(F32))
    v = _rms(jax.nn.gelu(v_ref[...].astype(F32)), g_ref[...]).astype(BF16)
    row = lax.broadcasted_iota(jnp.int32, (GM_CHUNK, GM_CHUNK), 0)
    col = lax.broadcasted_iota(jnp.int32, (GM_CHUNK, GM_CHUNK), 1)
    causal = row >= col
    for gi in range(GM_GROUPS):
        w = jnp.where(causal, ws_ref[gi], 0.0).astype(BF16)
        cs = slice(gi * GM_GW, (gi + 1) * GM_GW)
        for c in range(GMLP_TM // GM_CHUNK):
            rs = slice(c * GM_CHUNK, (c + 1) * GM_CHUNK)
            mixed = jnp.dot(w, v[rs, cs], preferred_element_type=F32) + bs_ref[gi]
            o_ref[rs, cs] = (u[rs, cs] * mixed).astype(o_ref.dtype)


def _gmlp(u, v, g, ws, bs):
    tok = lambda i: (i, 0)
    return pl.pallas_call(
        _gmlp_kernel,
        grid=(TOKENS // GMLP_TM,),
        in_specs=[pl.BlockSpec((GMLP_TM, MIX), tok),
                  pl.BlockSpec((GMLP_TM, MIX), tok),
                  pl.BlockSpec((1, MIX), lambda i: (0, 0)),
                  pl.BlockSpec((GM_GROUPS, GM_CHUNK, GM_CHUNK), lambda i: (0, 0, 0)),
                  pl.BlockSpec((GM_GROUPS, GM_CHUNK, GM_GW), lambda i: (0, 0, 0))],
        out_specs=pl.BlockSpec((GMLP_TM, MIX), tok),
        out_shape=jax.ShapeDtypeStruct((TOKENS, MIX), BF16),
        compiler_params=_cparams(("arbitrary",)),
        name="gmlp",
    )(u, v, g, ws, bs)


LRU_TAIL = 8


def _lru_kernel(gb_ref, rb_ref, cw_ref, cb_ref, wa_ref, ba_ref, wx_ref, bx_ref, lam_ref, o_ref,
                tail_ref, h_ref):
    @pl.when(pl.program_id(1) == 0)
    def _():
        tail_ref[...] = jnp.zeros_like(tail_ref)
        h_ref[...] = jnp.zeros_like(h_ref)

    ts = LRU_TS
    x = rb_ref[...].astype(F32)
    ext = jnp.concatenate([tail_ref[...], x], axis=0)
    tail_ref[...] = x[ts - LRU_TAIL:, :]
    xc = cb_ref[...] + x * cw_ref[CONV_WIDTH - 1:CONV_WIDTH, :]
    for d in range(1, CONV_WIDTH):
        xs = pltpu.roll(ext, d, axis=0)[LRU_TAIL:, :]
        xc = xc + xs * cw_ref[CONV_WIDTH - 1 - d:CONV_WIDTH - d, :]
    xcb = xc.astype(BF16)
    r = jax.nn.sigmoid(jnp.dot(xcb, wa_ref[...], preferred_element_type=F32) + ba_ref[...])
    ig = jax.nn.sigmoid(jnp.dot(xcb, wx_ref[...], preferred_element_type=F32) + bx_ref[...])
    z = -lam_ref[...]
    softplus = jnp.maximum(z, 0.0) + jnp.log1p(jnp.exp(-jnp.abs(z)))
    log_a = -LRU_C * r * softplus
    a = jnp.exp(log_a)
    b = jnp.sqrt(1.0 - jnp.exp(2.0 * log_a)) * (ig * xc)
    rows = lax.broadcasted_iota(jnp.int32, (ts, 1), 0)
    d = 1
    while d < ts:
        valid = rows >= d
        a_prev = pltpu.roll(a, d, axis=0)
        b_prev = pltpu.roll(b, d, axis=0)
        b = jnp.where(valid, a * b_prev, 0.0) + b
        a = jnp.where(valid, a * a_prev, a)
        d *= 2
    h = a * h_ref[...] + b
    h_ref[...] = h[ts - 1:ts, :]
    o_ref[...] = (jax.nn.gelu(gb_ref[...].astype(F32)) * h).astype(o_ref.dtype)


def _lru(gb, rb, cw, cb, wa, ba, wx, bx, lam):
    nt = SEQ // LRU_TS
    tok = lambda b, s: (b * nt + s, 0)
    vec = pl.BlockSpec((1, MIX), lambda b, s: (0, 0))
    mat = pl.BlockSpec((MIX, MIX), lambda b, s: (0, 0))
    return pl.pallas_call(
        _lru_kernel,
        grid=(BATCH, nt),
        in_specs=[pl.BlockSpec((LRU_TS, MIX), tok), pl.BlockSpec((LRU_TS, MIX), tok),
                  pl.BlockSpec((CONV_WIDTH, MIX), lambda b, s: (0, 0)), vec,
                  mat, vec, mat, vec, vec],
        out_specs=pl.BlockSpec((LRU_TS, MIX), tok),
        out_shape=jax.ShapeDtypeStruct((TOKENS, MIX), BF16),
        scratch_shapes=[pltpu.VMEM((LRU_TAIL, MIX), F32), pltpu.VMEM((1, MIX), F32)],
        compiler_params=_cparams(("arbitrary", "arbitrary")),
        name="rglru",
    )(gb, rb, cw, cb, wa, ba, wx, bx, lam)


POOL_TAIL = 16


def _pool_kernel(x_ref, w_ref, sc_ref, o_ref, tail_ref):
    s_id = pl.program_id(1)

    @pl.when(s_id == 0)
    def _():
        tail_ref[...] = jnp.zeros_like(tail_ref)

    ts = POOL_TS
    x = x_ref[...].astype(F32)
    ext = jnp.concatenate([tail_ref[...], x], axis=0)
    tail_ref[...] = x[ts - POOL_TAIL:, :]
    pos = s_id * ts + lax.broadcasted_iota(jnp.int32, (ts, 1), 0)
    acc = ext
    width = 1
    for gi, wdw in enumerate(POOL_WINDOWS):
        while width < wdw:
            acc = acc + pltpu.roll(acc, width, axis=0)
            width *= 2
        cs = slice(gi * POOL_GW, (gi + 1) * POOL_GW)
        cnt = jnp.minimum(pos + 1, wdw).astype(F32)
        pooled = acc[POOL_TAIL:, cs] / cnt - x[:, cs]
        mixed = jnp.dot(pooled.astype(BF16), w_ref[gi], preferred_element_type=F32)
        o_ref[:, cs] = (mixed * sc_ref[:, cs]).astype(o_ref.dtype)


def _pool(xd, w, sc):
    nt = SEQ // POOL_TS
    tok = lambda b, s: (b * nt + s, 0)
    return pl.pallas_call(
        _pool_kernel,
        grid=(BATCH, nt),
        in_specs=[pl.BlockSpec((POOL_TS, MIX), tok),
                  pl.BlockSpec((len(POOL_WINDOWS), POOL_GW, POOL_GW), lambda b, s: (0, 0, 0)),
                  pl.BlockSpec((1, MIX), lambda b, s: (0, 0))],
        out_specs=pl.BlockSpec((POOL_TS, MIX), tok),
        out_shape=jax.ShapeDtypeStruct((TOKENS, MIX), BF16),
        scratch_shapes=[pltpu.VMEM((POOL_TAIL, MIX), F32)],
        compiler_params=_cparams(("arbitrary", "arbitrary")),
        name="pool",
    )(xd, w, sc)


HALF_FLAT = CMP_STRIDE * HEAD_DIM


def _cmpkv_kernel(x_ref, pe_ref, w1_ref, w2_ref, o_ref):
    x = x_ref[0, 0, 0].astype(F32)
    pe = pe_ref[0]
    w1 = w1_ref[0, 0]
    first = jnp.dot((x + pe[0:1, :]).astype(BF16), w1[:HALF_FLAT, :], preferred_element_type=F32)
    second = jnp.dot((x + pe[1:2, :]).astype(BF16), w1[HALF_FLAT:, :], preferred_element_type=F32)
    hid = jax.nn.gelu(first + pltpu.roll(second, N_CHUNK - 1, axis=0))
    out = jnp.dot(hid.astype(BF16), w2_ref[0, 0], preferred_element_type=F32)
    row = lax.broadcasted_iota(jnp.int32, (N_CHUNK, 1), 0)
    o_ref[0, 0, 0] = jnp.where(row < N_CMP, out, 0.0).astype(o_ref.dtype)


def _cmpkv(x, pe, w1, w2):
    return pl.pallas_call(
        _cmpkv_kernel,
        grid=(2, BATCH, N_KV),
        in_specs=[pl.BlockSpec((1, 1, 1, N_CHUNK, HALF_FLAT), lambda c, b, g: (c, b, g, 0, 0)),
                  pl.BlockSpec((1, 2, HALF_FLAT), lambda c, b, g: (c, 0, 0)),
                  pl.BlockSpec((1, 1, CMP_LEN * HEAD_DIM, HEAD_DIM), lambda c, b, g: (c, g, 0, 0)),
                  pl.BlockSpec((1, 1, HEAD_DIM, HEAD_DIM), lambda c, b, g: (c, g, 0, 0))],
        out_specs=pl.BlockSpec((1, 1, 1, N_CHUNK, HEAD_DIM), lambda c, b, g: (c, b, g, 0, 0)),
        out_shape=jax.ShapeDtypeStruct((2, BATCH, N_KV, N_CHUNK, HEAD_DIM), BF16),
        compiler_params=_cparams(("arbitrary", "arbitrary", "arbitrary")),
        name="cmpkv",
    )(x, pe, w1, w2)


def _lane_tile(x, n):
    return jnp.concatenate([x] * n, axis=1)


ONES_ROWS = 16


def _with_ones(v_t):
    return jnp.concatenate([v_t, jnp.ones((ONES_ROWS, v_t.shape[1]), BF16)], axis=0)


def _nsa_kernel(q_ref, kc_ref, vct_ref, ks_ref, vst_ref, kw_ref, vwt_ref, gt_ref, ovt_ref, blk_ref,
                y_ref):
    i = pl.program_id(1)
    tq = NSA_TQ
    nl = HPG * tq
    t_row = i * tq + lax.broadcasted_iota(jnp.int32, (1, tq), 1)
    groups = range(N_KV)
    vrows = [slice(g * HEAD_DIM, (g + 1) * HEAD_DIM) for g in groups]

    def scores(k, q_t):
        return jnp.dot(k, q_t, preferred_element_type=F32)

    def normalized(acc):
        return acc[:HEAD_DIM, :] / acc[HEAD_DIM:HEAD_DIM + 1, :]

    lane = lax.broadcasted_iota(jnp.int32, (tq, LANES), 1)
    scale = HEAD_DIM ** -0.5
    q4 = [jnp.concatenate(
        [(jnp.where((lane >= HEAD_DIM) == (g == 1),
                    q_ref[:, j * LANES:(j + 1) * LANES].astype(F32), 0.0) * scale).T
         for j in range(HPG)], axis=1).astype(BF16) for g in groups]

    n_idx = lax.broadcasted_iota(jnp.int32, (N_CHUNK, 1), 0)
    ok = n_idx * CMP_STRIDE + (CMP_LEN - 1) <= _lane_tile(t_row, HPG)
    blk = lax.broadcasted_iota(jnp.int32, (N_SLC, 1), 0)
    cur = jnp.right_shift(t_row, 6)
    forced = (blk == 0) | (blk == cur) | (blk == cur - 1)
    causal_blk = blk * SLC_LEN <= t_row
    o_cmp, q_aug = [], []
    for g in groups:
        s = jnp.where(ok, scores(kc_ref[0], q4[g]), NEG_INF)
        m = jnp.max(s, axis=0, keepdims=True)
        e = jnp.where(ok, jnp.exp(s - m), 0.0)
        l = jnp.sum(e, axis=0, keepdims=True)
        p = e / jnp.where(l > 0.0, l, 1.0)
        o_cmp.append(jnp.dot(vct_ref[0, vrows[g], :], p.astype(BF16), preferred_element_type=F32))
        psum = p[:, 0:tq]
        for j in range(1, HPG):
            psum = psum + p[:, j * tq:(j + 1) * tq]
        p_hi = psum.astype(BF16)
        p_lo = (psum - p_hi.astype(F32)).astype(BF16)
        imp = (jnp.dot(ovt_ref[...], p_hi, preferred_element_type=F32)
               + jnp.dot(ovt_ref[...], p_lo, preferred_element_type=F32))
        work = jnp.where(forced, -3e38, jnp.where(causal_blk, imp, -1.0))
        sel = jnp.where(forced, 1.0, 0.0)
        for _ in range(SLC_TOPN - MAX_FORCED):
            mx = jnp.max(work, axis=0, keepdims=True)
            idx = jnp.min(jnp.where(work == mx, blk, N_SLC), axis=0, keepdims=True)
            pick = blk == idx
            sel = jnp.where(pick & (mx >= 0.0), 1.0, sel)
            work = jnp.where(pick, -3e38, work)
        sel_bias = jnp.where(sel > 0.0, 0.0, NEG_INF)
        sel_bias = jnp.concatenate([sel_bias, jnp.zeros((LANES - N_SLC, tq), F32)], axis=0)
        sel_bias = _lane_tile(sel_bias.astype(BF16), HPG)
        q_aug.append(jnp.concatenate([q4[g], sel_bias], axis=0))

    def slc_scores(kt):
        k0 = pl.multiple_of(kt * SLC_KT, SLC_KT)
        k_aug = jnp.concatenate([ks_ref[pl.ds(k0, SLC_KT), :], blk_ref[pl.ds(k0, SLC_KT), :]],
                                axis=1)
        return [scores(k_aug, q_aug[g]) for g in groups]

    def slc_update(kt, state, s_t):
        k0 = pl.multiple_of(kt * SLC_KT, SLC_KT)
        out = []
        for g in groups:
            m_i, acc = state[2 * g], state[2 * g + 1]
            m_new = jnp.maximum(m_i, jnp.max(s_t[g], axis=0, keepdims=True))
            alpha = jnp.exp(m_i - m_new)
            pT = jnp.exp((s_t[g] - m_new).astype(BF16))
            vT = _with_ones(vst_ref[vrows[g], pl.ds(k0, SLC_KT)])
            out += [m_new, alpha * acc + jnp.dot(vT, pT, preferred_element_type=F32)]
        return out

    def slc_step(kt, carry):
        s_next = slc_scores(kt + 1)
        return tuple(slc_update(kt, carry[:2 * N_KV], carry[2 * N_KV:]) + s_next)

    last_kt = (i * tq) // SLC_KT
    init = ((jnp.full((1, nl), -3e38, F32), jnp.zeros((HEAD_DIM + ONES_ROWS, nl), F32)) * N_KV
            + tuple(slc_scores(0)))
    carry = lax.fori_loop(0, last_kt, slc_step, init)
    kpos = last_kt * SLC_KT + lax.broadcasted_iota(jnp.int32, (SLC_KT, 1), 0)
    causal_bias = _lane_tile(jnp.where(kpos <= t_row, 0.0, NEG_INF), HPG)
    state = slc_update(last_kt, carry[:2 * N_KV], [s + causal_bias for s in carry[2 * N_KV:]])
    o_slc = [normalized(state[2 * g + 1]) for g in groups]

    start = pl.multiple_of(jnp.maximum(i - WIN // tq, 0) * tq, tq)
    k_win = kw_ref[pl.ds(start, WIN_KEYS), :]
    delta = t_row - (start + lax.broadcasted_iota(jnp.int32, (WIN_KEYS, 1), 0))
    win_bias = _lane_tile(jnp.where((delta >= 0) & (delta < WIN), 0.0, NEG_INF), HPG)
    o_win = []
    for g in groups:
        sT = scores(k_win, q4[g]) + win_bias
        pT = jnp.exp((sT - jnp.max(sT, axis=0, keepdims=True)).astype(BF16))
        vT = _with_ones(vwt_ref[vrows[g], pl.ds(start, WIN_KEYS)])
        o_win.append(normalized(jnp.dot(vT, pT, preferred_element_type=F32)))

    for g in groups:
        gates = jax.nn.sigmoid(gt_ref[0, g])

        def gate_row(br):
            return jnp.concatenate([gates[3 * j + br:3 * j + br + 1, :] for j in range(HPG)],
                                   axis=1)

        o = gate_row(0) * o_cmp[g] + gate_row(1) * o_slc[g] + gate_row(2) * o_win[g]
        for j in range(HPG):
            h0 = (g * HPG + j) * HEAD_DIM
            y_ref[0, h0:h0 + HEAD_DIM, :] = o[:, j * tq:(j + 1) * tq].astype(y_ref.dtype)


def _block_onehot():
    oh = (np.arange(SEQ)[:, None] // SLC_LEN) == np.arange(LANES)[None, :]
    return jnp.asarray(oh.astype(np.float32), dtype=BF16)


def _nsa(q, kc, vct, kv, vst, vwt, gt, ovt):
    nq = SEQ // NSA_TQ
    return pl.pallas_call(
        _nsa_kernel,
        grid=(BATCH, nq),
        in_specs=[pl.BlockSpec((NSA_TQ, NSA_Q), lambda b, i: (b * nq + i, 0)),
                  pl.BlockSpec((1, N_CHUNK, NSA_KV), lambda b, i: (b, 0, 0)),
                  pl.BlockSpec((1, NSA_KV, N_CHUNK), lambda b, i: (b, 0, 0)),
                  pl.BlockSpec((SEQ, NSA_KV), lambda b, i: (b, 2)),
                  pl.BlockSpec((None, NSA_KV, SEQ), lambda b, i: (b, 0, 0)),
                  pl.BlockSpec((SEQ, NSA_KV), lambda b, i: (b, 4)),
                  pl.BlockSpec((None, NSA_KV, SEQ), lambda b, i: (b, 0, 0)),
                  pl.BlockSpec((1, N_KV, GATE_ROWS, NSA_TQ), lambda b, i: (b, 0, 0, i)),
                  pl.BlockSpec((N_SLC, N_CHUNK), lambda b, i: (0, 0)),
                  pl.BlockSpec((SEQ, LANES), lambda b, i: (0, 0))],
        out_specs=pl.BlockSpec((1, NSA_Q, NSA_TQ), lambda b, i: (b, 0, i)),
        out_shape=jax.ShapeDtypeStruct((BATCH, NSA_Q, SEQ), BF16),
        compiler_params=_cparams(("arbitrary", "arbitrary")),
        name="nsa",
    )(q, kc, vct, kv, vst, kv, vwt, gt, ovt, _block_onehot())


def _overlap_t():
    c_start = np.arange(N_CHUNK) * CMP_STRIDE
    s_start = np.arange(N_SLC) * SLC_LEN
    ov = ((c_start[None, :] <= s_start[:, None] + SLC_LEN - 1)
          & (c_start[None, :] + CMP_LEN - 1 >= s_start[:, None])
          & (np.arange(N_CHUNK)[None, :] < N_CMP))
    return jnp.asarray(ov.astype(np.float32), dtype=BF16)


def _merge_kernel(x_ref, g_ref, wmg_ref, ya_ref, yb_ref, yct_ref, yd_ref, wb_ref, wo_ref, o_ref):
    x = x_ref[...]
    nb = _rms(x, g_ref[...]).astype(BF16)
    yc = yct_ref[0].astype(F32).T.astype(BF16)
    ys = (ya_ref[...], yb_ref[...], yc, yd_ref[...])
    merged = jnp.zeros((MERGE_TM, D_MODEL), F32)
    for bi, y in enumerate(ys):
        gate = jax.nn.sigmoid(jnp.dot(nb, wmg_ref[:, bi * D_MODEL:(bi + 1) * D_MODEL],
                                      preferred_element_type=F32))
        merged = merged + gate * jnp.dot(y, wb_ref[bi], preferred_element_type=F32)
    o_ref[...] = x + jnp.dot(merged.astype(BF16), wo_ref[...], preferred_element_type=F32)


def _merge(x, g, wmg, ya, yb, yct, yd, wb, wo, layer):
    nt = SEQ // MERGE_TM
    tok = lambda i: (i, 0)
    const2 = lambda i: (0, 0)
    return pl.pallas_call(
        _merge_kernel,
        grid=(TOKENS // MERGE_TM,),
        in_specs=[pl.BlockSpec((MERGE_TM, D_MODEL), tok),
                  pl.BlockSpec((1, D_MODEL), const2),
                  pl.BlockSpec((None, D_MODEL, N_BRANCH * D_MODEL), lambda i: (layer, 0, 0)),
                  pl.BlockSpec((MERGE_TM, MIX), tok),
                  pl.BlockSpec((MERGE_TM, MIX), tok),
                  pl.BlockSpec((1, MIX, MERGE_TM), lambda i: (i // nt, 0, i % nt)),
                  pl.BlockSpec((MERGE_TM, MIX), tok),
                  pl.BlockSpec((None, N_BRANCH, MIX, D_MODEL), lambda i: (layer, 0, 0, 0)),
                  pl.BlockSpec((None, D_MODEL, D_MODEL), lambda i: (layer, 0, 0))],
        out_specs=pl.BlockSpec((MERGE_TM, D_MODEL), tok),
        out_shape=jax.ShapeDtypeStruct((TOKENS, D_MODEL), F32),
        compiler_params=_cparams(("arbitrary",)),
        name="merge",
    )(x, g, wmg, ya, yb, yct, yd, wb, wo)


def _router_logits(t, wr_ref, br_ref):
    return jnp.dot(t, wr_ref[...], preferred_element_type=F32,
                   precision=lax.Precision.HIGHEST) + br_ref[...]


def _top_group(logits):
    lane = lax.broadcasted_iota(jnp.int32, logits.shape, 1)
    is_grp = lane < N_GROUPS
    lg = jnp.where(is_grp, logits, NEG_INF)
    gmax = jnp.max(lg, axis=1, keepdims=True)
    grp_idx = jnp.min(jnp.where(is_grp & (lg == gmax), lane, ROUTER_PAD), axis=1, keepdims=True)
    return lg, gmax, grp_idx


def _combine_weights(logits, grp_idx):
    lane = lax.broadcasted_iota(jnp.int32, logits.shape, 1)
    lg, gmax, _ = _top_group(logits)
    ge = jnp.where(lane < N_GROUPS, jnp.exp(lg - gmax), 0.0)
    grp_w = (jnp.sum(jnp.where(lane == grp_idx, ge, 0.0), axis=1, keepdims=True)
             / jnp.sum(ge, axis=1, keepdims=True))
    e_idx = lane - N_GROUPS
    in_grp = (e_idx >= grp_idx * EXPERTS_PER_GROUP) & (e_idx < (grp_idx + 1) * EXPERTS_PER_GROUP)
    le = jnp.where(in_grp, logits, NEG_INF)
    l1 = jnp.max(le, axis=1, keepdims=True)
    i1 = jnp.min(jnp.where(in_grp & (le == l1), lane, ROUTER_PAD), axis=1, keepdims=True)
    le2 = jnp.where(lane == i1, NEG_INF, le)
    l2 = jnp.max(le2, axis=1, keepdims=True)
    i2 = jnp.min(jnp.where(in_grp & (lane != i1) & (le2 == l2), lane, ROUTER_PAD), axis=1,
                 keepdims=True)
    e2 = jnp.exp(l2 - l1)
    den = 1.0 + e2
    w1 = grp_w / den
    w2 = grp_w * e2 / den
    return jnp.where(lane == i1, w1, 0.0) + jnp.where(lane == i2, w2, 0.0)


def _route_kernel(x_ref, g_ref, wr_ref, br_ref, tri_ref, meta_ref, cnt_ref, run_ref):
    @pl.when(pl.program_id(0) == 0)
    def _():
        run_ref[...] = jnp.zeros_like(run_ref)

    logits = _router_logits(_rms(x_ref[...], g_ref[...]), wr_ref, br_ref)
    _, _, grp_idx = _top_group(logits)
    lane = lax.broadcasted_iota(jnp.int32, logits.shape, 1)
    onehot = jnp.where(lane == grp_idx, 1.0, 0.0)
    before = jnp.dot(tri_ref[...], onehot.astype(BF16), preferred_element_type=F32) + run_ref[...]
    rank = jnp.sum(onehot * before, axis=1, keepdims=True).astype(jnp.int32)
    run_ref[...] += jnp.sum(onehot, axis=0, keepdims=True)
    meta_ref[...] = jnp.where(lane == 0, grp_idx, jnp.where(lane == 1, rank, 0))
    cnt_ref[...] = run_ref[...]


def _route(x, g, wr, br):
    tok = lambda i: (i, 0)
    const2 = lambda i: (0, 0)
    tri = jnp.asarray(np.tril(np.ones((ROUTE_TM, ROUTE_TM), np.float32), -1), dtype=BF16)
    return pl.pallas_call(
        _route_kernel,
        grid=(TOKENS // ROUTE_TM,),
        in_specs=[pl.BlockSpec((ROUTE_TM, D_MODEL), tok),
                  pl.BlockSpec((1, D_MODEL), const2),
                  pl.BlockSpec((D_MODEL, ROUTER_PAD), const2),
                  pl.BlockSpec((1, ROUTER_PAD), const2),
                  pl.BlockSpec((ROUTE_TM, ROUTE_TM), const2)],
        out_specs=[pl.BlockSpec((ROUTE_TM, ROUTER_PAD), tok), pl.BlockSpec((1, ROUTER_PAD), const2)],
        out_shape=[jax.ShapeDtypeStruct((TOKENS, ROUTER_PAD), jnp.int32),
                   jax.ShapeDtypeStruct((1, ROUTER_PAD), F32)],
        scratch_shapes=[pltpu.VMEM((1, ROUTER_PAD), F32)],
        compiler_params=_cparams(("arbitrary",)),
        name="route",
    )(x, g, wr, br, tri)


def _move_rows_kernel(idx_ref, pad_ref, src_hbm, fill_hbm, dst_hbm, sem, *, scatter):
    def chunk_done():
        return pltpu.make_async_copy(src_hbm.at[pl.ds(0, ROW_CHUNK)],
                                     dst_hbm.at[pl.ds(0, ROW_CHUNK)], sem)

    def run(n_rows, copy):
        def body(c, carry):
            for r in range(ROW_CHUNK):
                copy(c * ROW_CHUNK + r, r).start()

            @pl.when(c > 0)
            def _():
                chunk_done().wait()
            return carry
        lax.fori_loop(0, n_rows // ROW_CHUNK, body, 0)
        chunk_done().wait()

    def token_copy(r, _):
        s, d = (r, idx_ref[r]) if scatter else (idx_ref[r], r)
        return pltpu.make_async_copy(src_hbm.at[pl.ds(s, 1)], dst_hbm.at[pl.ds(d, 1)], sem)

    run(TOKENS, token_copy)
    if scatter:
        run(MOE_ROWS - TOKENS, lambda k, r: pltpu.make_async_copy(
            fill_hbm.at[pl.ds(r, 1)], dst_hbm.at[pl.ds(pad_ref[k], 1)], sem))


def _move_rows(idx, pad, src, fill, n_out, scatter):
    any_spec = pl.BlockSpec(memory_space=pl.ANY)
    grid_spec = pltpu.PrefetchScalarGridSpec(
        num_scalar_prefetch=2, grid=(1,), in_specs=[any_spec, any_spec], out_specs=any_spec,
        scratch_shapes=[pltpu.SemaphoreType.DMA(())])
    return pl.pallas_call(
        functools.partial(_move_rows_kernel, scatter=scatter),
        grid_spec=grid_spec,
        out_shape=jax.ShapeDtypeStruct((n_out, D_MODEL), F32),
        compiler_params=_cparams(("arbitrary",)),
        name="dispatch" if scatter else "collect",
    )(idx, pad, src, fill)


def _experts_kernel(tg_ref, tv_ref, x_ref, g_ref, wr_ref, br_ref, wg_ref, wu_ref, wd_ref, fg_ref,
                    o_ref, *, final_norm):
    j = pl.program_id(0)
    nv = tv_ref[j]

    @pl.when(nv > 0)
    def _():
        x = x_ref[...]
        t = _rms(x, g_ref[...])
        grp = tg_ref[j]
        comb = _combine_weights(_router_logits(t, wr_ref, br_ref), grp)
        lane = lax.broadcasted_iota(jnp.int32, comb.shape, 1)
        tb = t.astype(BF16)
        acc = jnp.zeros((MOE_TM, D_MODEL), F32)
        for e in range(EXPERTS_PER_GROUP):
            ce = jnp.sum(jnp.where(lane == N_GROUPS + grp * EXPERTS_PER_GROUP + e, comb, 0.0),
                         axis=1, keepdims=True)
            hid = (jax.nn.silu(jnp.dot(tb, wg_ref[e], preferred_element_type=F32))
                   * jnp.dot(tb, wu_ref[e], preferred_element_type=F32)) * ce
            acc = acc + jnp.dot(hid.astype(BF16), wd_ref[e], preferred_element_type=F32)
        h = x + acc
        if final_norm:
            h = _rms(h, fg_ref[...])
        o_ref[...] = h

    @pl.when(nv == 0)
    def _():
        o_ref[...] = jnp.zeros_like(o_ref)


def _experts(tile_group, tile_valid, hs, g, wr, br, wg, wu, wd, fg, layer, final_norm):
    const2 = lambda j, *_: (0, 0)
    rows = lambda j, *_: (j, 0)
    grp_w = lambda j, tg, tv: (layer * N_GROUPS + tg[j], 0, 0)
    grid_spec = pltpu.PrefetchScalarGridSpec(
        num_scalar_prefetch=2,
        grid=(MOE_NT,),
        in_specs=[pl.BlockSpec((MOE_TM, D_MODEL), rows),
                  pl.BlockSpec((1, D_MODEL), const2),
                  pl.BlockSpec((D_MODEL, ROUTER_PAD), const2),
                  pl.BlockSpec((1, ROUTER_PAD), const2),
                  pl.BlockSpec((EXPERTS_PER_GROUP, D_MODEL, D_EXPERT), grp_w),
                  pl.BlockSpec((EXPERTS_PER_GROUP, D_MODEL, D_EXPERT), grp_w),
                  pl.BlockSpec((EXPERTS_PER_GROUP, D_EXPERT, D_MODEL), grp_w),
                  pl.BlockSpec((1, D_MODEL), const2)],
        out_specs=pl.BlockSpec((MOE_TM, D_MODEL), rows))
    return pl.pallas_call(
        functools.partial(_experts_kernel, final_norm=final_norm),
        grid_spec=grid_spec,
        out_shape=jax.ShapeDtypeStruct((MOE_ROWS, D_MODEL), F32),
        compiler_params=_cparams(("arbitrary",)),
        name="experts",
    )(tile_group, tile_valid, hs, g, wr, br, wg, wu, wd, fg)


def _moe(x, g, wr, br, wg, wu, wd, fg, layer, final_norm):
    meta, cnt = _route(x, g, wr, br)
    grp, rank = meta[:, 0], meta[:, 1]
    counts = cnt[0, :N_GROUPS].astype(jnp.int32)
    padded = (counts + MOE_TM - 1) // MOE_TM * MOE_TM
    ends = jnp.cumsum(padded)
    starts = ends - padded
    pos = starts[grp] + rank
    seg_start = jnp.concatenate([starts + counts, ends[-1:]])
    seg_len = jnp.concatenate([padded - counts, MOE_ROWS - ends[-1:]])
    seg_first = jnp.cumsum(seg_len) - seg_len
    k = jnp.arange(MOE_ROWS - TOKENS, dtype=jnp.int32)
    seg = jnp.sum(k[:, None] >= jnp.cumsum(seg_len)[None, :], axis=1)
    pad = (seg_start[seg] + k - seg_first[seg]).astype(jnp.int32)
    tile_start = jnp.arange(MOE_NT, dtype=jnp.int32) * MOE_TM
    tile_group = jnp.minimum(jnp.sum(tile_start[:, None] >= ends[None, :], axis=1), N_GROUPS - 1)
    tile_group = tile_group.astype(jnp.int32)
    tile_valid = jnp.clip(starts[tile_group] + counts[tile_group] - tile_start, 0, MOE_TM)
    hs = _move_rows(pos, pad, x, jnp.zeros((ROW_CHUNK, D_MODEL), F32), MOE_ROWS, scatter=True)
    ys = _experts(tile_group, tile_valid.astype(jnp.int32), hs, g, wr, br, wg, wu, wd, fg, layer,
                  final_norm)
    return _move_rows(pos, pad, ys, ys, TOKENS, scatter=False)


def _block_diag(w):
    eye = jnp.eye(LRU_BLOCKS, dtype=w.dtype)
    return jnp.einsum('hij,hk->hikj', w, eye).reshape(MIX, MIX)


def _split_w_in(w_in):
    cuts = [int(c) for c in np.cumsum((0,) + IN_SPLITS)]
    cols = lambda a, b: w_in[:, :, a:b]
    q_parts = [cols(cuts[4] + (g * HPG + j) * HEAD_DIM, cuts[4] + (g * HPG + j + 1) * HEAD_DIM)
               for j in range(HPG) for g in range(N_KV)]
    gate_pad = jnp.zeros((DEPTH, D_MODEL, GATE_PAD - IN_SPLITS[6]), w_in.dtype)
    w_proj = jnp.concatenate([cols(cuts[0], cuts[4])] + q_parts
                             + [cols(cuts[5], cuts[6]), cols(cuts[7], cuts[8]),
                                cols(cuts[6], cuts[7]), gate_pad], axis=2)
    return w_proj.astype(BF16), cols(cuts[8], cuts[9]).astype(BF16)


def _layer(h, p, big, layer, final_g, final_norm):
    row = lambda a: a.reshape(1, -1)

    u, v, gb, rb, q, kv, xd, ng = _proj(h, row(p['norm1_g']), big['w_proj'], layer)

    bs = jnp.broadcast_to(p['gm_b'][:, :, None], (GM_GROUPS, GM_CHUNK, GM_GW))
    y_a = _gmlp(u, v, row(p['gm_norm_g']), p['gm_ws'], bs)

    y_b = _lru(gb, rb, p['conv_w'], row(p['conv_b']), _block_diag(p['lru_wa']).astype(BF16),
               row(p['lru_ba']), _block_diag(p['lru_wx']).astype(BF16), row(p['lru_bx']),
               row(p['lru_lambda']))

    y_d = _pool(xd, p['pool_w'].astype(BF16), row(p['pool_scale']))

    kvc = kv[:, :2 * NSA_KV].reshape(BATCH, SEQ, 2, N_KV, HEAD_DIM).transpose(2, 0, 3, 1, 4)
    kvc = kvc.reshape(2, BATCH, N_KV, N_CHUNK, HALF_FLAT)
    pe = p['cmp_pe'].reshape(2, 2, HALF_FLAT)
    cmp = _cmpkv(kvc, pe, p['cmp_w1'].astype(BF16), p['cmp_w2'].astype(BF16))
    kc = cmp[0].transpose(0, 2, 1, 3).reshape(BATCH, N_CHUNK, NSA_KV)
    vct = cmp[1].transpose(0, 1, 3, 2).reshape(BATCH, NSA_KV, N_CHUNK)
    kv3 = kv.reshape(BATCH, SEQ, 6 * NSA_KV)
    vst = kv3[:, :, 3 * NSA_KV:4 * NSA_KV].transpose(0, 2, 1)
    vwt = kv3[:, :, 5 * NSA_KV:6 * NSA_KV].transpose(0, 2, 1)
    gt = ng[:, :3 * N_HEADS].reshape(BATCH, SEQ, N_KV, 3 * HPG).transpose(0, 2, 3, 1)
    gt = jnp.pad(gt, ((0, 0), (0, 0), (0, GATE_ROWS - 3 * HPG), (0, 0)))
    y_ct = _nsa(q, kc, vct, kv, vst, vwt, gt, _overlap_t())

    h = _merge(h, row(p['norm1_g']), big['w_mg'], y_a, y_b, y_ct, y_d, big['w_branch'],
               big['w_out'], layer)

    wr = jnp.concatenate([p['router_w_group'], p['router_w_expert']], axis=1)
    wr = jnp.pad(wr, ((0, 0), (0, ROUTER_PAD - wr.shape[1])))
    br = jnp.concatenate([p['router_b_group'], p['router_b_expert']])
    br = jnp.pad(br, (0, ROUTER_PAD - br.shape[0])).reshape(1, ROUTER_PAD)
    return _moe(h, row(p['norm2_g']), wr, br, big['moe_w_gate'], big['moe_w_up'],
                big['moe_w_down'], row(final_g), layer, final_norm)


_LAYER_PARAMS = ('norm1_g', 'gm_norm_g', 'gm_ws', 'gm_b', 'conv_w', 'conv_b', 'lru_wa',
                 'lru_ba', 'lru_wx', 'lru_bx', 'lru_lambda', 'cmp_pe', 'cmp_w1', 'cmp_w2', 'pool_w',
                 'pool_scale', 'norm2_g', 'router_w_group', 'router_b_group',
                 'router_w_expert', 'router_b_expert')


def kernel(x, norm1_g, w_in, gm_norm_g, gm_ws, gm_b, conv_w, conv_b, lru_wa, lru_ba, lru_wx,
           lru_bx, lru_lambda, cmp_pe, cmp_w1, cmp_w2, pool_w, pool_scale, w_branch, w_out,
           norm2_g, router_w_group, router_b_group, router_w_expert, router_b_expert,
           moe_w_gate, moe_w_up, moe_w_down, final_norm_g):
    stacked = dict(zip(_LAYER_PARAMS, (
        norm1_g, gm_norm_g, gm_ws, gm_b, conv_w, conv_b, lru_wa, lru_ba, lru_wx, lru_bx,
        lru_lambda, cmp_pe, cmp_w1, cmp_w2, pool_w, pool_scale, norm2_g,
        router_w_group, router_b_group, router_w_expert, router_b_expert)))
    w_proj, w_mg = _split_w_in(w_in)
    experts = lambda w: w.astype(BF16).reshape((DEPTH * N_EXPERTS,) + w.shape[2:])
    big = dict(w_proj=w_proj, w_mg=w_mg, w_branch=w_branch.astype(BF16), w_out=w_out.astype(BF16),
               moe_w_gate=experts(moe_w_gate), moe_w_up=experts(moe_w_up),
               moe_w_down=experts(moe_w_down))
    h = x.reshape(TOKENS, D_MODEL)
    for layer in range(DEPTH):
        p = {k: a[layer] for k, a in stacked.items()}
        h = _layer(h, p, big, layer, final_norm_g, final_norm=(layer == DEPTH - 1))
    return h.reshape(BATCH, SEQ, D_MODEL)
```

```python
import functools

import numpy as np
import jax
import jax.numpy as jnp
from jax import lax
from jax.experimental import pallas as pl
from jax.experimental.pallas import tpu as pltpu

F32 = jnp.float32
BF16 = jnp.bfloat16

D_MODEL = 1024
BATCH = 4
SEQ = 4096
TOKENS = BATCH * SEQ
DEPTH = 2
MIX = D_MODEL // 2
GM_CHUNK = 128
GM_GROUPS = 4
GM_GW = MIX // GM_GROUPS
CONV_WIDTH = 4
LRU_BLOCKS = 8
LRU_BW = MIX // LRU_BLOCKS
LRU_C = 8.0
N_HEADS = 8
HEAD_DIM = MIX // N_HEADS
N_KV = 2
HPG = N_HEADS // N_KV
CMP_LEN = 32
CMP_STRIDE = 16
SLC_LEN = 64
SLC_TOPN = 8
MAX_FORCED = 3
WIN = 512
NSA_Q = N_HEADS * HEAD_DIM
NSA_KV = N_KV * HEAD_DIM
POOL_WINDOWS = (2, 4, 8, 16)
POOL_GW = MIX // len(POOL_WINDOWS)
N_BRANCH = 4
N_GROUPS = 4
EXPERTS_PER_GROUP = 4
N_EXPERTS = N_GROUPS * EXPERTS_PER_GROUP
D_EXPERT = D_MODEL // 2
EPS = 1e-6
NEG_INF = -1e30
FORCE_SCORE = 1e6
IN_SPLITS = (MIX, MIX, MIX, MIX, NSA_Q, 6 * NSA_KV, 3 * N_HEADS, MIX, N_BRANCH * D_MODEL)

N_CHUNK = SEQ // CMP_STRIDE
N_CMP = N_CHUNK - CMP_LEN // CMP_STRIDE + 1
N_SLC = SEQ // SLC_LEN

LANES = 128
GATE_PAD = LANES
GATE_ROWS = 16
ROUTER_PAD = LANES
VMEM_LIMIT = 56 * 1024 * 1024

PROJ_WIDTHS = (MIX, MIX, MIX, MIX, NSA_Q, 6 * NSA_KV, MIX, GATE_PAD)
PROJ_TM = 512
GMLP_TM = 512
LRU_TS = 512
POOL_TS = 512
NSA_TQ = 128
SLC_KT = 512
WIN_KEYS = WIN + NSA_TQ
MERGE_TM = 256
ROUTE_TM = 512
MOE_TM = 256
MOE_ROWS = TOKENS + N_GROUPS * MOE_TM
MOE_NT = MOE_ROWS // MOE_TM
ROW_CHUNK = 256


def _cparams(sem):
    return pltpu.CompilerParams(dimension_semantics=sem, vmem_limit_bytes=VMEM_LIMIT)


def _rms(x, g):
    return x * lax.rsqrt(jnp.mean(x * x, axis=-1, keepdims=True) + EPS) * g


def _proj_kernel(x_ref, g_ref, w_ref, *out_refs):
    nb = _rms(x_ref[...], g_ref[...]).astype(BF16)
    off = 0
    for ref in out_refs:
        w = ref.shape[-1]
        ref[...] = jnp.dot(nb, w_ref[:, off:off + w], preferred_element_type=F32).astype(ref.dtype)
        off += w


def _proj(x, g, w, layer):
    n_in = sum(PROJ_WIDTHS)
    out_shape = [jax.ShapeDtypeStruct((TOKENS, wd), BF16) for wd in PROJ_WIDTHS[:-1]]
    out_shape.append(jax.ShapeDtypeStruct((TOKENS, GATE_PAD), F32))
    return pl.pallas_call(
        _proj_kernel,
        grid=(TOKENS // PROJ_TM,),
        in_specs=[pl.BlockSpec((PROJ_TM, D_MODEL), lambda i: (i, 0)),
                  pl.BlockSpec((1, D_MODEL), lambda i: (0, 0)),
                  pl.BlockSpec((None, D_MODEL, n_in), lambda i: (layer, 0, 0))],
        out_specs=[pl.BlockSpec((PROJ_TM, wd), lambda i: (i, 0)) for wd in PROJ_WIDTHS],
        out_shape=out_shape,
        compiler_params=_cparams(("arbitrary",)),
        name="proj",
    )(x, g, w)


def _gmlp_kernel(u_ref, v_ref, g_ref, ws_ref, bs_ref, o_ref):
    u = jax.nn.gelu(u_ref[...].astype(F32))
    v = _rms(jax.nn.gelu(v_ref[...].astype(F32)), g_ref[...]).astype(BF16)
    row = lax.broadcasted_iota(jnp.int32, (GM_CHUNK, GM_CHUNK), 0)
    col = lax.broadcasted_iota(jnp.int32, (GM_CHUNK, GM_CHUNK), 1)
    causal = row >= col
    for gi in range(GM_GROUPS):
        w = jnp.where(causal, ws_ref[gi], 0.0).astype(BF16)
        cs = slice(gi * GM_GW, (gi + 1) * GM_GW)
        for c in range(GMLP_TM // GM_CHUNK):
            rs = slice(c * GM_CHUNK, (c + 1) * GM_CHUNK)
            mixed = jnp.dot(w, v[rs, cs], preferred_element_type=F32) + bs_ref[gi]
            o_ref[rs, cs] = (u[rs, cs] * mixed).astype(o_ref.dtype)


def _gmlp(u, v, g, ws, bs):
    tok = lambda i: (i, 0)
    return pl.pallas_call(
        _gmlp_kernel,
        grid=(TOKENS // GMLP_TM,),
        in_specs=[pl.BlockSpec((GMLP_TM, MIX), tok),
                  pl.BlockSpec((GMLP_TM, MIX), tok),
                  pl.BlockSpec((1, MIX), lambda i: (0, 0)),
                  pl.BlockSpec((GM_GROUPS, GM_CHUNK, GM_CHUNK), lambda i: (0, 0, 0)),
                  pl.BlockSpec((GM_GROUPS, GM_CHUNK, GM_GW), lambda i: (0, 0, 0))],
        out_specs=pl.BlockSpec((GMLP_TM, MIX), tok),
        out_shape=jax.ShapeDtypeStruct((TOKENS, MIX), BF16),
        compiler_params=_cparams(("arbitrary",)),
        name="gmlp",
    )(u, v, g, ws, bs)


LRU_TAIL = 8


def _lru_kernel(gb_ref, rb_ref, cw_ref, cb_ref, wa_ref, ba_ref, wx_ref, bx_ref, lam_ref, o_ref,
                tail_ref, h_ref):
    @pl.when(pl.program_id(1) == 0)
    def _():
        tail_ref[...] = jnp.zeros_like(tail_ref)
        h_ref[...] = jnp.zeros_like(h_ref)

    ts = LRU_TS
    x = rb_ref[...].astype(F32)
    ext = jnp.concatenate([tail_ref[...], x], axis=0)
    tail_ref[...] = x[ts - LRU_TAIL:, :]
    xc = cb_ref[...] + x * cw_ref[CONV_WIDTH - 1:CONV_WIDTH, :]
    for d in range(1, CONV_WIDTH):
        xs = pltpu.roll(ext, d, axis=0)[LRU_TAIL:, :]
        xc = xc + xs * cw_ref[CONV_WIDTH - 1 - d:CONV_WIDTH - d, :]
    xcb = xc.astype(BF16)
    r = jax.nn.sigmoid(jnp.dot(xcb, wa_ref[...], preferred_element_type=F32) + ba_ref[...])
    ig = jax.nn.sigmoid(jnp.dot(xcb, wx_ref[...], preferred_element_type=F32) + bx_ref[...])
    z = -lam_ref[...]
    softplus = jnp.maximum(z, 0.0) + jnp.log1p(jnp.exp(-jnp.abs(z)))
    log_a = -LRU_C * r * softplus
    a = jnp.exp(log_a)
    b = jnp.sqrt(1.0 - jnp.exp(2.0 * log_a)) * (ig * xc)
    rows = lax.broadcasted_iota(jnp.int32, (ts, 1), 0)
    d = 1
    while d < ts:
        valid = rows >= d
        a_prev = pltpu.roll(a, d, axis=0)
        b_prev = pltpu.roll(b, d, axis=0)
        b = jnp.where(valid, a * b_prev, 0.0) + b
        a = jnp.where(valid, a * a_prev, a)
        d *= 2
    h = a * h_ref[...] + b
    h_ref[...] = h[ts - 1:ts, :]
    o_ref[...] = (jax.nn.gelu(gb_ref[...].astype(F32)) * h).astype(o_ref.dtype)


def _lru(gb, rb, cw, cb, wa, ba, wx, bx, lam):
    nt = SEQ // LRU_TS
    tok = lambda b, s: (b * nt + s, 0)
    vec = pl.BlockSpec((1, MIX), lambda b, s: (0, 0))
    mat = pl.BlockSpec((MIX, MIX), lambda b, s: (0, 0))
    return pl.pallas_call(
        _lru_kernel,
        grid=(BATCH, nt),
        in_specs=[pl.BlockSpec((LRU_TS, MIX), tok), pl.BlockSpec((LRU_TS, MIX), tok),
                  pl.BlockSpec((CONV_WIDTH, MIX), lambda b, s: (0, 0)), vec,
                  mat, vec, mat, vec, vec],
        out_specs=pl.BlockSpec((LRU_TS, MIX), tok),
        out_shape=jax.ShapeDtypeStruct((TOKENS, MIX), BF16),
        scratch_shapes=[pltpu.VMEM((LRU_TAIL, MIX), F32), pltpu.VMEM((1, MIX), F32)],
        compiler_params=_cparams(("arbitrary", "arbitrary")),
        name="rglru",
    )(gb, rb, cw, cb, wa, ba, wx, bx, lam)


POOL_TAIL = 16


def _pool_kernel(x_ref, w_ref, sc_ref, o_ref, tail_ref):
    s_id = pl.program_id(1)

    @pl.when(s_id == 0)
    def _():
        tail_ref[...] = jnp.zeros_like(tail_ref)

    ts = POOL_TS
    x = x_ref[...].astype(F32)
    ext = jnp.concatenate([tail_ref[...], x], axis=0)
    tail_ref[...] = x[ts - POOL_TAIL:, :]
    pos = s_id * ts + lax.broadcasted_iota(jnp.int32, (ts, 1), 0)
    acc = ext
    width = 1
    for gi, wdw in enumerate(POOL_WINDOWS):
        while width < wdw:
            acc = acc + pltpu.roll(acc, width, axis=0)
            width *= 2
        cs = slice(gi * POOL_GW, (gi + 1) * POOL_GW)
        cnt = jnp.minimum(pos + 1, wdw).astype(F32)
        pooled = acc[POOL_TAIL:, cs] / cnt - x[:, cs]
        mixed = jnp.dot(pooled.astype(BF16), w_ref[gi], preferred_element_type=F32)
        o_ref[:, cs] = (mixed * sc_ref[:, cs]).astype(o_ref.dtype)


def _pool(xd, w, sc):
    nt = SEQ // POOL_TS
    tok = lambda b, s: (b * nt + s, 0)
    return pl.pallas_call(
        _pool_kernel,
        grid=(BATCH, nt),
        in_specs=[pl.BlockSpec((POOL_TS, MIX), tok),
                  pl.BlockSpec((len(POOL_WINDOWS), POOL_GW, POOL_GW), lambda b, s: (0, 0, 0)),
                  pl.BlockSpec((1, MIX), lambda b, s: (0, 0))],
        out_specs=pl.BlockSpec((POOL_TS, MIX), tok),
        out_shape=jax.ShapeDtypeStruct((TOKENS, MIX), BF16),
        scratch_shapes=[pltpu.VMEM((POOL_TAIL, MIX), F32)],
        compiler_params=_cparams(("arbitrary", "arbitrary")),
        name="pool",
    )(xd, w, sc)


def _kvprep_kernel(kc_in, vc_in, vs_in, vw_in, pe_ref, w1_ref, w2_ref, kc_ref, vct_ref, vst_ref,
                   vwt_ref, xf_ref):
    row = lax.broadcasted_iota(jnp.int32, (N_CHUNK, 1), 0)
    for c, x_in in enumerate((kc_in, vc_in)):
        xf_ref[...] = x_in[...].astype(F32)
        first = jnp.zeros((N_CHUNK, NSA_KV), F32)
        second = jnp.zeros((N_CHUNK, NSA_KV), F32)
        for l in range(CMP_STRIDE):
            xl = xf_ref[pl.ds(l, N_CHUNK, stride=CMP_STRIDE), :]
            first += jnp.dot((xl + pe_ref[c, l:l + 1, :]).astype(BF16), w1_ref[c, l],
                             preferred_element_type=F32)
            l2 = l + CMP_STRIDE
            second += jnp.dot((xl + pe_ref[c, l2:l2 + 1, :]).astype(BF16), w1_ref[c, l2],
                              preferred_element_type=F32)
        hid = jax.nn.gelu(first + pltpu.roll(second, N_CHUNK - 1, axis=0))
        out = jnp.dot(hid.astype(BF16), w2_ref[c], preferred_element_type=F32)
        out = jnp.where(row < N_CMP, out, 0.0)
        if c == 0:
            kc_ref[0] = out.astype(kc_ref.dtype)
        else:
            vct_ref[0] = out.T.astype(vct_ref.dtype)
    vst_ref[0] = vs_in[...].astype(F32).T.astype(vst_ref.dtype)
    vwt_ref[0] = vw_in[...].astype(F32).T.astype(vwt_ref.dtype)


def _kvprep(kv, pe, w1, w2):
    col = lambda j: pl.BlockSpec((SEQ, NSA_KV), lambda b: (b, j))
    whole = lambda a: pl.BlockSpec(a.shape, lambda b: (0,) * a.ndim)
    return pl.pallas_call(
        _kvprep_kernel,
        grid=(BATCH,),
        in_specs=[col(0), col(1), col(3), col(5), whole(pe), whole(w1), whole(w2)],
        out_specs=[pl.BlockSpec((1, N_CHUNK, NSA_KV), lambda b: (b, 0, 0)),
                   pl.BlockSpec((1, NSA_KV, N_CHUNK), lambda b: (b, 0, 0)),
                   pl.BlockSpec((1, NSA_KV, SEQ), lambda b: (b, 0, 0)),
                   pl.BlockSpec((1, NSA_KV, SEQ), lambda b: (b, 0, 0))],
        out_shape=[jax.ShapeDtypeStruct((BATCH, N_CHUNK, NSA_KV), BF16),
                   jax.ShapeDtypeStruct((BATCH, NSA_KV, N_CHUNK), BF16),
                   jax.ShapeDtypeStruct((BATCH, NSA_KV, SEQ), BF16),
                   jax.ShapeDtypeStruct((BATCH, NSA_KV, SEQ), BF16)],
        scratch_shapes=[pltpu.VMEM((SEQ, NSA_KV), F32)],
        compiler_params=_cparams(("arbitrary",)),
        name="kvprep",
    )(kv, kv, kv, kv, pe, w1, w2)


def _group_diag(w):
    eye = jnp.eye(N_KV, dtype=w.dtype)
    out = jnp.einsum('...gde,gh->...gdhe', w, eye)
    return out.reshape(w.shape[:-3] + (N_KV * w.shape[-2], N_KV * w.shape[-1]))


def _lane_tile(x, n):
    return jnp.concatenate([x] * n, axis=1)


ONES_ROWS = 16


def _with_ones(v_t):
    return jnp.concatenate([v_t, jnp.ones((ONES_ROWS, v_t.shape[1]), BF16)], axis=0)


def _nsa_kernel(q_ref, kc_ref, vct_ref, ks_ref, vst_ref, kw_ref, vwt_ref, gt_ref, ovt_ref, blk_ref,
                y_ref):
    i = pl.program_id(1)
    tq = NSA_TQ
    nl = HPG * tq
    t_row = i * tq + lax.broadcasted_iota(jnp.int32, (1, tq), 1)
    groups = range(N_KV)
    vrows = [slice(g * HEAD_DIM, (g + 1) * HEAD_DIM) for g in groups]

    def scores(k, q_t):
        return jnp.dot(k, q_t, preferred_element_type=F32)

    def normalized(acc):
        return acc[:HEAD_DIM, :] / acc[HEAD_DIM:HEAD_DIM + 1, :]

    lane = lax.broadcasted_iota(jnp.int32, (tq, LANES), 1)
    scale = HEAD_DIM ** -0.5
    q4 = [jnp.concatenate(
        [(jnp.where((lane >= HEAD_DIM) == (g == 1),
                    q_ref[:, j * LANES:(j + 1) * LANES].astype(F32), 0.0) * scale).T
         for j in range(HPG)], axis=1).astype(BF16) for g in groups]

    n_idx = lax.broadcasted_iota(jnp.int32, (N_CHUNK, 1), 0)
    ok = n_idx * CMP_STRIDE + (CMP_LEN - 1) <= _lane_tile(t_row, HPG)
    blk = lax.broadcasted_iota(jnp.int32, (N_SLC, 1), 0)
    cur = jnp.right_shift(t_row, 6)
    forced = (blk == 0) | (blk == cur) | (blk == cur - 1)
    causal_blk = blk * SLC_LEN <= t_row
    o_cmp, q_aug = [], []
    for g in groups:
        s = jnp.where(ok, scores(kc_ref[0], q4[g]), NEG_INF)
        m = jnp.max(s, axis=0, keepdims=True)
        e = jnp.where(ok, jnp.exp(s - m), 0.0)
        l = jnp.sum(e, axis=0, keepdims=True)
        p = e / jnp.where(l > 0.0, l, 1.0)
        o_cmp.append(jnp.dot(vct_ref[0, vrows[g], :], p.astype(BF16), preferred_element_type=F32))
        psum = p[:, 0:tq]
        for j in range(1, HPG):
            psum = psum + p[:, j * tq:(j + 1) * tq]
        p_hi = psum.astype(BF16)
        p_lo = (psum - p_hi.astype(F32)).astype(BF16)
        imp = (jnp.dot(ovt_ref[...], p_hi, preferred_element_type=F32)
               + jnp.dot(ovt_ref[...], p_lo, preferred_element_type=F32))
        work = jnp.where(forced, -3e38, jnp.where(causal_blk, imp, -1.0))
        sel = jnp.where(forced, 1.0, 0.0)
        for _ in range(SLC_TOPN - MAX_FORCED):
            mx = jnp.max(work, axis=0, keepdims=True)
            idx = jnp.min(jnp.where(work == mx, blk, N_SLC), axis=0, keepdims=True)
            pick = blk == idx
            sel = jnp.where(pick & (mx >= 0.0), 1.0, sel)
            work = jnp.where(pick, -3e38, work)
        sel_bias = jnp.where(sel > 0.0, 0.0, NEG_INF)
        sel_bias = jnp.concatenate([sel_bias, jnp.zeros((LANES - N_SLC, tq), F32)], axis=0)
        sel_bias = _lane_tile(sel_bias.astype(BF16), HPG)
        q_aug.append(jnp.concatenate([q4[g], sel_bias], axis=0))

    def slc_scores(kt):
        k0 = pl.multiple_of(kt * SLC_KT, SLC_KT)
        k_aug = jnp.concatenate([ks_ref[pl.ds(k0, SLC_KT), :], blk_ref[pl.ds(k0, SLC_KT), :]],
                                axis=1)
        return [scores(k_aug, q_aug[g]) for g in groups]

    def slc_update(kt, state, s_t):
        k0 = pl.multiple_of(kt * SLC_KT, SLC_KT)
        out = []
        for g in groups:
            m_i, acc = state[2 * g], state[2 * g + 1]
            m_new = jnp.maximum(m_i, jnp.max(s_t[g], axis=0, keepdims=True))
            alpha = jnp.exp(m_i - m_new)
            pT = jnp.exp((s_t[g] - m_new).astype(BF16))
            vT = _with_ones(vst_ref[vrows[g], pl.ds(k0, SLC_KT)])
            out += [m_new, alpha * acc + jnp.dot(vT, pT, preferred_element_type=F32)]
        return out

    def slc_step(kt, carry):
        s_next = slc_scores(kt + 1)
        return tuple(slc_update(kt, carry[:2 * N_KV], carry[2 * N_KV:]) + s_next)

    last_kt = (i * tq) // SLC_KT
    init = ((jnp.full((1, nl), -3e38, F32), jnp.zeros((HEAD_DIM + ONES_ROWS, nl), F32)) * N_KV
            + tuple(slc_scores(0)))
    carry = lax.fori_loop(0, last_kt, slc_step, init)
    kpos = last_kt * SLC_KT + lax.broadcasted_iota(jnp.int32, (SLC_KT, 1), 0)
    causal_bias = _lane_tile(jnp.where(kpos <= t_row, 0.0, NEG_INF), HPG)
    state = slc_update(last_kt, carry[:2 * N_KV], [s + causal_bias for s in carry[2 * N_KV:]])
    o_slc = [normalized(state[2 * g + 1]) for g in groups]

    start = pl.multiple_of(jnp.maximum(i - WIN // tq, 0) * tq, tq)
    k_win = kw_ref[pl.ds(start, WIN_KEYS), :]
    delta = t_row - (start + lax.broadcasted_iota(jnp.int32, (WIN_KEYS, 1), 0))
    win_bias = _lane_tile(jnp.where((delta >= 0) & (delta < WIN), 0.0, NEG_INF), HPG)
    o_win = []
    for g in groups:
        sT = scores(k_win, q4[g]) + win_bias
        pT = jnp.exp((sT - jnp.max(sT, axis=0, keepdims=True)).astype(BF16))
        vT = _with_ones(vwt_ref[vrows[g], pl.ds(start, WIN_KEYS)])
        o_win.append(normalized(jnp.dot(vT, pT, preferred_element_type=F32)))

    for g in groups:
        gates = jax.nn.sigmoid(gt_ref[0, g])

        def gate_row(br):
            return jnp.concatenate([gates[3 * j + br:3 * j + br + 1, :] for j in range(HPG)],
                                   axis=1)

        o = gate_row(0) * o_cmp[g] + gate_row(1) * o_slc[g] + gate_row(2) * o_win[g]
        for j in range(HPG):
            h0 = (g * HPG + j) * HEAD_DIM
            y_ref[0, h0:h0 + HEAD_DIM, :] = o[:, j * tq:(j + 1) * tq].astype(y_ref.dtype)


def _block_onehot():
    oh = (np.arange(SEQ)[:, None] // SLC_LEN) == np.arange(LANES)[None, :]
    return jnp.asarray(oh.astype(np.float32), dtype=BF16)


def _nsa(q, kc, vct, kv, vst, vwt, gt, ovt):
    nq = SEQ // NSA_TQ
    return pl.pallas_call(
        _nsa_kernel,
        grid=(BATCH, nq),
        in_specs=[pl.BlockSpec((NSA_TQ, NSA_Q), lambda b, i: (b * nq + i, 0)),
                  pl.BlockSpec((1, N_CHUNK, NSA_KV), lambda b, i: (b, 0, 0)),
                  pl.BlockSpec((1, NSA_KV, N_CHUNK), lambda b, i: (b, 0, 0)),
                  pl.BlockSpec((SEQ, NSA_KV), lambda b, i: (b, 2)),
                  pl.BlockSpec((None, NSA_KV, SEQ), lambda b, i: (b, 0, 0)),
                  pl.BlockSpec((SEQ, NSA_KV), lambda b, i: (b, 4)),
                  pl.BlockSpec((None, NSA_KV, SEQ), lambda b, i: (b, 0, 0)),
                  pl.BlockSpec((1, N_KV, GATE_ROWS, NSA_TQ), lambda b, i: (b, 0, 0, i)),
                  pl.BlockSpec((N_SLC, N_CHUNK), lambda b, i: (0, 0)),
                  pl.BlockSpec((SEQ, LANES), lambda b, i: (0, 0))],
        out_specs=pl.BlockSpec((1, NSA_Q, NSA_TQ), lambda b, i: (b, 0, i)),
        out_shape=jax.ShapeDtypeStruct((BATCH, NSA_Q, SEQ), BF16),
        compiler_params=_cparams(("arbitrary", "arbitrary")),
        name="nsa",
    )(q, kc, vct, kv, vst, kv, vwt, gt, ovt, _block_onehot())


def _overlap_t():
    c_start = np.arange(N_CHUNK) * CMP_STRIDE
    s_start = np.arange(N_SLC) * SLC_LEN
    ov = ((c_start[None, :] <= s_start[:, None] + SLC_LEN - 1)
          & (c_start[None, :] + CMP_LEN - 1 >= s_start[:, None])
          & (np.arange(N_CHUNK)[None, :] < N_CMP))
    return jnp.asarray(ov.astype(np.float32), dtype=BF16)


def _merge_kernel(x_ref, g_ref, wmg_ref, ya_ref, yb_ref, yct_ref, yd_ref, wb_ref, wo_ref, o_ref):
    x = x_ref[...]
    nb = _rms(x, g_ref[...]).astype(BF16)
    yc = yct_ref[0].astype(F32).T.astype(BF16)
    ys = (ya_ref[...], yb_ref[...], yc, yd_ref[...])
    merged = jnp.zeros((MERGE_TM, D_MODEL), F32)
    for bi, y in enumerate(ys):
        gate = jax.nn.sigmoid(jnp.dot(nb, wmg_ref[:, bi * D_MODEL:(bi + 1) * D_MODEL],
                                      preferred_element_type=F32))
        merged = merged + gate * jnp.dot(y, wb_ref[bi], preferred_element_type=F32)
    o_ref[...] = x + jnp.dot(merged.astype(BF16), wo_ref[...], preferred_element_type=F32)


def _merge(x, g, wmg, ya, yb, yct, yd, wb, wo, layer):
    nt = SEQ // MERGE_TM
    tok = lambda i: (i, 0)
    const2 = lambda i: (0, 0)
    return pl.pallas_call(
        _merge_kernel,
        grid=(TOKENS // MERGE_TM,),
        in_specs=[pl.BlockSpec((MERGE_TM, D_MODEL), tok),
                  pl.BlockSpec((1, D_MODEL), const2),
                  pl.BlockSpec((None, D_MODEL, N_BRANCH * D_MODEL), lambda i: (layer, 0, 0)),
                  pl.BlockSpec((MERGE_TM, MIX), tok),
                  pl.BlockSpec((MERGE_TM, MIX), tok),
                  pl.BlockSpec((1, MIX, MERGE_TM), lambda i: (i // nt, 0, i % nt)),
                  pl.BlockSpec((MERGE_TM, MIX), tok),
                  pl.BlockSpec((None, N_BRANCH, MIX, D_MODEL), lambda i: (layer, 0, 0, 0)),
                  pl.BlockSpec((None, D_MODEL, D_MODEL), lambda i: (layer, 0, 0))],
        out_specs=pl.BlockSpec((MERGE_TM, D_MODEL), tok),
        out_shape=jax.ShapeDtypeStruct((TOKENS, D_MODEL), F32),
        compiler_params=_cparams(("arbitrary",)),
        name="merge",
    )(x, g, wmg, ya, yb, yct, yd, wb, wo)


def _router_logits(t, wr_ref, br_ref):
    w = wr_ref[...]
    t_hi, w_hi = t.astype(BF16), w.astype(BF16)
    t_lo = (t - t_hi.astype(F32)).astype(BF16)
    w_lo = (w - w_hi.astype(F32)).astype(BF16)
    dot = functools.partial(jnp.dot, preferred_element_type=F32)
    return dot(t_hi, w_hi) + dot(t_hi, w_lo) + dot(t_lo, w_hi) + br_ref[...]


def _top_group(logits):
    lane = lax.broadcasted_iota(jnp.int32, logits.shape, 1)
    is_grp = lane < N_GROUPS
    lg = jnp.where(is_grp, logits, NEG_INF)
    gmax = jnp.max(lg, axis=1, keepdims=True)
    grp_idx = jnp.min(jnp.where(is_grp & (lg == gmax), lane, ROUTER_PAD), axis=1, keepdims=True)
    return lg, gmax, grp_idx


def _combine_weights(logits, grp_idx):
    lane = lax.broadcasted_iota(jnp.int32, logits.shape, 1)
    lg, gmax, _ = _top_group(logits)
    ge = jnp.where(lane < N_GROUPS, jnp.exp(lg - gmax), 0.0)
    grp_w = (jnp.sum(jnp.where(lane == grp_idx, ge, 0.0), axis=1, keepdims=True)
             / jnp.sum(ge, axis=1, keepdims=True))
    e_idx = lane - N_GROUPS
    in_grp = (e_idx >= grp_idx * EXPERTS_PER_GROUP) & (e_idx < (grp_idx + 1) * EXPERTS_PER_GROUP)
    le = jnp.where(in_grp, logits, NEG_INF)
    l1 = jnp.max(le, axis=1, keepdims=True)
    i1 = jnp.min(jnp.where(in_grp & (le == l1), lane, ROUTER_PAD), axis=1, keepdims=True)
    le2 = jnp.where(lane == i1, NEG_INF, le)
    l2 = jnp.max(le2, axis=1, keepdims=True)
    i2 = jnp.min(jnp.where(in_grp & (lane != i1) & (le2 == l2), lane, ROUTER_PAD), axis=1,
                 keepdims=True)
    e2 = jnp.exp(l2 - l1)
    den = 1.0 + e2
    w1 = grp_w / den
    w2 = grp_w * e2 / den
    return jnp.where(lane == i1, w1, 0.0) + jnp.where(lane == i2, w2, 0.0)


def _route_kernel(x_ref, g_ref, wr_ref, br_ref, tri_ref, meta_ref, cnt_ref, run_ref):
    @pl.when(pl.program_id(0) == 0)
    def _():
        run_ref[...] = jnp.zeros_like(run_ref)

    logits = _router_logits(_rms(x_ref[...], g_ref[...]), wr_ref, br_ref)
    _, _, grp_idx = _top_group(logits)
    lane = lax.broadcasted_iota(jnp.int32, logits.shape, 1)
    onehot = jnp.where(lane == grp_idx, 1.0, 0.0)
    before = jnp.dot(tri_ref[...], onehot.astype(BF16), preferred_element_type=F32) + run_ref[...]
    rank = jnp.sum(onehot * before, axis=1, keepdims=True).astype(jnp.int32)
    run_ref[...] += jnp.sum(onehot, axis=0, keepdims=True)
    meta_ref[...] = jnp.where(lane == 0, grp_idx, jnp.where(lane == 1, rank, 0))
    cnt_ref[...] = run_ref[...]


def _route(x, g, wr, br):
    tok = lambda i: (i, 0)
    const2 = lambda i: (0, 0)
    tri = jnp.asarray(np.tril(np.ones((ROUTE_TM, ROUTE_TM), np.float32), -1), dtype=BF16)
    return pl.pallas_call(
        _route_kernel,
        grid=(TOKENS // ROUTE_TM,),
        in_specs=[pl.BlockSpec((ROUTE_TM, D_MODEL), tok),
                  pl.BlockSpec((1, D_MODEL), const2),
                  pl.BlockSpec((D_MODEL, ROUTER_PAD), const2),
                  pl.BlockSpec((1, ROUTER_PAD), const2),
                  pl.BlockSpec((ROUTE_TM, ROUTE_TM), const2)],
        out_specs=[pl.BlockSpec((ROUTE_TM, ROUTER_PAD), tok), pl.BlockSpec((1, ROUTER_PAD), const2)],
        out_shape=[jax.ShapeDtypeStruct((TOKENS, ROUTER_PAD), jnp.int32),
                   jax.ShapeDtypeStruct((1, ROUTER_PAD), F32)],
        scratch_shapes=[pltpu.VMEM((1, ROUTER_PAD), F32)],
        compiler_params=_cparams(("arbitrary",)),
        name="route",
    )(x, g, wr, br, tri)


MOVE_SLOTS = 3


def _move_rows_kernel(idx_ref, pad_ref, src_hbm, dst_hbm, buf, zbuf, in_sem, out_sem, pad_sem, *,
                      scatter):
    c = pl.program_id(0)
    n = pl.num_programs(0)
    slot = lax.rem(c, MOVE_SLOTS)
    nxt = lax.rem(c + 1, MOVE_SLOTS)
    rc = ROW_CHUNK
    chunk = lambda k: pl.ds(k * rc, rc)

    if scatter:
        def read(k, sl):
            return [pltpu.make_async_copy(src_hbm.at[chunk(k)], buf.at[sl], in_sem.at[sl])]

        def write(k, sl):
            return [pltpu.make_async_copy(buf.at[sl, pl.ds(r, 1)],
                                          dst_hbm.at[pl.ds(idx_ref[k * rc + r], 1)], out_sem.at[sl])
                    for r in range(rc)]

        def write_done(sl):
            return pltpu.make_async_copy(buf.at[sl], dst_hbm.at[chunk(0)], out_sem.at[sl])

        def read_done(sl):
            return read(0, sl)[0]
    else:
        def read(k, sl):
            return [pltpu.make_async_copy(src_hbm.at[pl.ds(idx_ref[k * rc + r], 1)],
                                          buf.at[sl, pl.ds(r, 1)], in_sem.at[sl])
                    for r in range(rc)]

        def write(k, sl):
            return [pltpu.make_async_copy(buf.at[sl], dst_hbm.at[chunk(k)], out_sem.at[sl])]

        def write_done(sl):
            return write(0, sl)[0]

        def read_done(sl):
            return pltpu.make_async_copy(src_hbm.at[chunk(0)], buf.at[sl], in_sem.at[sl])

    def pad_copies(k0):
        return [pltpu.make_async_copy(zbuf.at[pl.ds(r, 1)], dst_hbm.at[pl.ds(pad_ref[k0 + r], 1)],
                                      pad_sem) for r in range(rc)]

    n_pad_chunks = (MOE_ROWS - TOKENS) // rc

    @pl.when(c == 0)
    def _():
        for cp in read(0, 0):
            cp.start()
        if scatter:
            zbuf[...] = jnp.zeros_like(zbuf)
            for k in range(n_pad_chunks):
                for cp in pad_copies(k * rc):
                    cp.start()

    @pl.when(c + 1 < n)
    def _():
        @pl.when(c >= 2)
        def _():
            write_done(nxt).wait()
        for cp in read(c + 1, nxt):
            cp.start()

    read_done(slot).wait()
    for cp in write(c, slot):
        cp.start()

    @pl.when(c == n - 1)
    def _():
        write_done(nxt).wait()
        write_done(lax.rem(c + 2, MOVE_SLOTS)).wait()
        write_done(slot).wait()
        if scatter:
            for k in range(n_pad_chunks):
                pltpu.make_async_copy(zbuf, dst_hbm.at[chunk(0)], pad_sem).wait()


def _move_rows(idx, pad, src, n_out, scatter):
    any_spec = pl.BlockSpec(memory_space=pl.ANY)
    grid_spec = pltpu.PrefetchScalarGridSpec(
        num_scalar_prefetch=2, grid=(TOKENS // ROW_CHUNK,), in_specs=[any_spec], out_specs=any_spec,
        scratch_shapes=[pltpu.VMEM((MOVE_SLOTS, ROW_CHUNK, D_MODEL), F32),
                        pltpu.VMEM((ROW_CHUNK, D_MODEL), F32),
                        pltpu.SemaphoreType.DMA((MOVE_SLOTS,)),
                        pltpu.SemaphoreType.DMA((MOVE_SLOTS,)),
                        pltpu.SemaphoreType.DMA(())])
    return pl.pallas_call(
        functools.partial(_move_rows_kernel, scatter=scatter),
        grid_spec=grid_spec,
        out_shape=jax.ShapeDtypeStruct((n_out, D_MODEL), F32),
        compiler_params=_cparams(("arbitrary",)),
        name="dispatch" if scatter else "collect",
    )(idx, pad, src)


def _experts_kernel(tg_ref, tv_ref, x_ref, g_ref, wr_ref, br_ref, wg_ref, wu_ref, wd_ref, fg_ref,
                    o_ref, *, final_norm):
    j = pl.program_id(0)
    nv = tv_ref[j]

    @pl.when(nv > 0)
    def _():
        x = x_ref[...]
        t = _rms(x, g_ref[...])
        grp = tg_ref[j]
        comb = _combine_weights(_router_logits(t, wr_ref, br_ref), grp)
        lane = lax.broadcasted_iota(jnp.int32, comb.shape, 1)
        tb = t.astype(BF16)
        acc = jnp.zeros((MOE_TM, D_MODEL), F32)
        for e in range(EXPERTS_PER_GROUP):
            ce = jnp.sum(jnp.where(lane == N_GROUPS + grp * EXPERTS_PER_GROUP + e, comb, 0.0),
                         axis=1, keepdims=True)
            hid = (jax.nn.silu(jnp.dot(tb, wg_ref[e], preferred_element_type=F32))
                   * jnp.dot(tb, wu_ref[e], preferred_element_type=F32)) * ce
            acc = acc + jnp.dot(hid.astype(BF16), wd_ref[e], preferred_element_type=F32)
        h = x + acc
        if final_norm:
            h = _rms(h, fg_ref[...])
        o_ref[...] = h

    @pl.when(nv == 0)
    def _():
        o_ref[...] = jnp.zeros_like(o_ref)


def _experts(tile_group, tile_valid, hs, g, wr, br, wg, wu, wd, fg, layer, final_norm):
    const2 = lambda j, *_: (0, 0)
    rows = lambda j, *_: (j, 0)
    grp_w = lambda j, tg, tv: (layer * N_GROUPS + tg[j], 0, 0)
    grid_spec = pltpu.PrefetchScalarGridSpec(
        num_scalar_prefetch=2,
        grid=(MOE_NT,),
        in_specs=[pl.BlockSpec((MOE_TM, D_MODEL), rows),
                  pl.BlockSpec((1, D_MODEL), const2),
                  pl.BlockSpec((D_MODEL, ROUTER_PAD), const2),
                  pl.BlockSpec((1, ROUTER_PAD), const2),
                  pl.BlockSpec((EXPERTS_PER_GROUP, D_MODEL, D_EXPERT), grp_w),
                  pl.BlockSpec((EXPERTS_PER_GROUP, D_MODEL, D_EXPERT), grp_w),
                  pl.BlockSpec((EXPERTS_PER_GROUP, D_EXPERT, D_MODEL), grp_w),
                  pl.BlockSpec((1, D_MODEL), const2)],
        out_specs=pl.BlockSpec((MOE_TM, D_MODEL), rows))
    return pl.pallas_call(
        functools.partial(_experts_kernel, final_norm=final_norm),
        grid_spec=grid_spec,
        out_shape=jax.ShapeDtypeStruct((MOE_ROWS, D_MODEL), F32),
        compiler_params=_cparams(("arbitrary",)),
        name="experts",
    )(tile_group, tile_valid, hs, g, wr, br, wg, wu, wd, fg)


def _moe(x, g, wr, br, wg, wu, wd, fg, layer, final_norm):
    meta, cnt = _route(x, g, wr, br)
    grp, rank = meta[:, 0], meta[:, 1]
    counts = cnt[0, :N_GROUPS].astype(jnp.int32)
    padded = (counts + MOE_TM - 1) // MOE_TM * MOE_TM
    ends = jnp.cumsum(padded)
    starts = ends - padded
    pos = starts[grp] + rank
    seg_start = jnp.concatenate([starts + counts, ends[-1:]])
    seg_len = jnp.concatenate([padded - counts, MOE_ROWS - ends[-1:]])
    seg_first = jnp.cumsum(seg_len) - seg_len
    k = jnp.arange(MOE_ROWS - TOKENS, dtype=jnp.int32)
    seg = jnp.sum(k[:, None] >= jnp.cumsum(seg_len)[None, :], axis=1)
    pad = (seg_start[seg] + k - seg_first[seg]).astype(jnp.int32)
    tile_start = jnp.arange(MOE_NT, dtype=jnp.int32) * MOE_TM
    tile_group = jnp.minimum(jnp.sum(tile_start[:, None] >= ends[None, :], axis=1), N_GROUPS - 1)
    tile_group = tile_group.astype(jnp.int32)
    tile_valid = jnp.clip(starts[tile_group] + counts[tile_group] - tile_start, 0, MOE_TM)
    hs = _move_rows(pos, pad, x, MOE_ROWS, scatter=True)
    ys = _experts(tile_group, tile_valid.astype(jnp.int32), hs, g, wr, br, wg, wu, wd, fg, layer,
                  final_norm)
    return _move_rows(pos, pad, ys, TOKENS, scatter=False)


def _block_diag(w):
    eye = jnp.eye(LRU_BLOCKS, dtype=w.dtype)
    return jnp.einsum('hij,hk->hikj', w, eye).reshape(MIX, MIX)


def _split_w_in(w_in):
    cuts = [int(c) for c in np.cumsum((0,) + IN_SPLITS)]
    cols = lambda a, b: w_in[:, :, a:b]
    q_parts = [cols(cuts[4] + (g * HPG + j) * HEAD_DIM, cuts[4] + (g * HPG + j + 1) * HEAD_DIM)
               for j in range(HPG) for g in range(N_KV)]
    gate_pad = jnp.zeros((DEPTH, D_MODEL, GATE_PAD - IN_SPLITS[6]), w_in.dtype)
    w_proj = jnp.concatenate([cols(cuts[0], cuts[4])] + q_parts
                             + [cols(cuts[5], cuts[6]), cols(cuts[7], cuts[8]),
                                cols(cuts[6], cuts[7]), gate_pad], axis=2)
    return w_proj.astype(BF16), cols(cuts[8], cuts[9]).astype(BF16)


def _nsa_mixer(q, kv, ng, p):
    w1 = p['cmp_w1'].reshape(2, N_KV, CMP_LEN, HEAD_DIM, HEAD_DIM).transpose(0, 2, 1, 3, 4)
    pe = jnp.concatenate([p['cmp_pe']] * N_KV, axis=-1)
    kc, vct, vst, vwt = _kvprep(kv, pe, _group_diag(w1).astype(BF16),
                                _group_diag(p['cmp_w2']).astype(BF16))
    gt = ng[:, :3 * N_HEADS].reshape(BATCH, SEQ, N_KV, 3 * HPG).transpose(0, 2, 3, 1)
    gt = jnp.pad(gt, ((0, 0), (0, 0), (0, GATE_ROWS - 3 * HPG), (0, 0)))
    return _nsa(q, kc, vct, kv, vst, vwt, gt, _overlap_t())


def _layer(h, p, big, layer, final_g, final_norm):
    row = lambda a: a.reshape(1, -1)

    u, v, gb, rb, q, kv, xd, ng = _proj(h, row(p['norm1_g']), big['w_proj'], layer)

    bs = jnp.broadcast_to(p['gm_b'][:, :, None], (GM_GROUPS, GM_CHUNK, GM_GW))
    y_a = _gmlp(u, v, row(p['gm_norm_g']), p['gm_ws'], bs)

    y_b = _lru(gb, rb, p['conv_w'], row(p['conv_b']), _block_diag(p['lru_wa']).astype(BF16),
               row(p['lru_ba']), _block_diag(p['lru_wx']).astype(BF16), row(p['lru_bx']),
               row(p['lru_lambda']))

    y_d = _pool(xd, p['pool_w'].astype(BF16), row(p['pool_scale']))

    y_ct = _nsa_mixer(q, kv, ng, p)

    h = _merge(h, row(p['norm1_g']), big['w_mg'], y_a, y_b, y_ct, y_d, big['w_branch'],
               big['w_out'], layer)

    wr = jnp.concatenate([p['router_w_group'], p['router_w_expert']], axis=1)
    wr = jnp.pad(wr, ((0, 0), (0, ROUTER_PAD - wr.shape[1])))
    br = jnp.concatenate([p['router_b_group'], p['router_b_expert']])
    br = jnp.pad(br, (0, ROUTER_PAD - br.shape[0])).reshape(1, ROUTER_PAD)
    return _moe(h, row(p['norm2_g']), wr, br, big['moe_w_gate'], big['moe_w_up'],
                big['moe_w_down'], row(final_g), layer, final_norm)


_LAYER_PARAMS = ('norm1_g', 'gm_norm_g', 'gm_ws', 'gm_b', 'conv_w', 'conv_b', 'lru_wa',
                 'lru_ba', 'lru_wx', 'lru_bx', 'lru_lambda', 'cmp_pe', 'cmp_w1', 'cmp_w2', 'pool_w',
                 'pool_scale', 'norm2_g', 'router_w_group', 'router_b_group',
                 'router_w_expert', 'router_b_expert')


def kernel(x, norm1_g, w_in, gm_norm_g, gm_ws, gm_b, conv_w, conv_b, lru_wa, lru_ba, lru_wx,
           lru_bx, lru_lambda, cmp_pe, cmp_w1, cmp_w2, pool_w, pool_scale, w_branch, w_out,
           norm2_g, router_w_group, router_b_group, router_w_expert, router_b_expert,
           moe_w_gate, moe_w_up, moe_w_down, final_norm_g):
    stacked = dict(zip(_LAYER_PARAMS, (
        norm1_g, gm_norm_g, gm_ws, gm_b, conv_w, conv_b, lru_wa, lru_ba, lru_wx, lru_bx,
        lru_lambda, cmp_pe, cmp_w1, cmp_w2, pool_w, pool_scale, norm2_g,
        router_w_group, router_b_group, router_w_expert, router_b_expert)))
    w_proj, w_mg = _split_w_in(w_in)
    experts = lambda w: w.astype(BF16).reshape((DEPTH * N_EXPERTS,) + w.shape[2:])
    big = dict(w_proj=w_proj, w_mg=w_mg, w_branch=w_branch.astype(BF16), w_out=w_out.astype(BF16),
               moe_w_gate=experts(moe_w_gate), moe_w_up=experts(moe_w_up),
               moe_w_down=experts(moe_w_down))
    h = x.reshape(TOKENS, D_MODEL)
    for layer in range(DEPTH):
        p = {k: a[layer] for k, a in stacked.items()}
        h = _layer(h, p, big, layer, final_norm_g, final_norm=(layer == DEPTH - 1))
    return h.reshape(BATCH, SEQ, D_MODEL)
```

```python
import functools

import numpy as np
import jax
import jax.numpy as jnp
from jax import lax
from jax.experimental import pallas as pl
from jax.experimental.pallas import tpu as pltpu

F32 = jnp.float32
BF16 = jnp.bfloat16

D_MODEL = 1024
BATCH = 4
SEQ = 4096
TOKENS = BATCH * SEQ
DEPTH = 2
MIX = D_MODEL // 2
GM_CHUNK = 128
GM_GROUPS = 4
GM_GW = MIX // GM_GROUPS
CONV_WIDTH = 4
LRU_BLOCKS = 8
LRU_BW = MIX // LRU_BLOCKS
LRU_C = 8.0
N_HEADS = 8
HEAD_DIM = MIX // N_HEADS
N_KV = 2
HPG = N_HEADS // N_KV
CMP_LEN = 32
CMP_STRIDE = 16
SLC_LEN = 64
SLC_TOPN = 8
MAX_FORCED = 3
WIN = 512
NSA_Q = N_HEADS * HEAD_DIM
NSA_KV = N_KV * HEAD_DIM
POOL_WINDOWS = (2, 4, 8, 16)
POOL_GW = MIX // len(POOL_WINDOWS)
N_BRANCH = 4
N_GROUPS = 4
EXPERTS_PER_GROUP = 4
N_EXPERTS = N_GROUPS * EXPERTS_PER_GROUP
D_EXPERT = D_MODEL // 2
EPS = 1e-6
NEG_INF = -1e30
FORCE_SCORE = 1e6
IN_SPLITS = (MIX, MIX, MIX, MIX, NSA_Q, 6 * NSA_KV, 3 * N_HEADS, MIX, N_BRANCH * D_MODEL)

N_CHUNK = SEQ // CMP_STRIDE
N_CMP = N_CHUNK - CMP_LEN // CMP_STRIDE + 1
N_SLC = SEQ // SLC_LEN

LANES = 128
GATE_PAD = LANES
GATE_ROWS = 16
ROUTER_PAD = LANES
VMEM_LIMIT = 56 * 1024 * 1024

PROJ_WIDTHS = (MIX, MIX, MIX, MIX, NSA_Q, 6 * NSA_KV, MIX, GATE_PAD)
PROJ_TM = 512
GMLP_TM = 512
LRU_TS = 512
POOL_TS = 512
NSA_TQ = 256
SLC_KT = 256
WIN_KEYS = WIN + NSA_TQ
MERGE_TM = 256
ROUTE_TM = 512
MOE_TM = 256
MOE_ROWS = TOKENS + N_GROUPS * MOE_TM
MOE_NT = MOE_ROWS // MOE_TM
ROW_CHUNK = 256


def _cparams(sem):
    return pltpu.CompilerParams(dimension_semantics=sem, vmem_limit_bytes=VMEM_LIMIT)


def _rms(x, g):
    return x * lax.rsqrt(jnp.mean(x * x, axis=-1, keepdims=True) + EPS) * g


def _proj_kernel(x_ref, g_ref, w_ref, *out_refs):
    nb = _rms(x_ref[...], g_ref[...]).astype(BF16)
    off = 0
    for ref in out_refs:
        w = ref.shape[-1]
        ref[...] = jnp.dot(nb, w_ref[:, off:off + w], preferred_element_type=F32).astype(ref.dtype)
        off += w


def _proj(x, g, w, layer):
    n_in = sum(PROJ_WIDTHS)
    out_shape = [jax.ShapeDtypeStruct((TOKENS, wd), BF16) for wd in PROJ_WIDTHS[:-1]]
    out_shape.append(jax.ShapeDtypeStruct((TOKENS, GATE_PAD), F32))
    return pl.pallas_call(
        _proj_kernel,
        grid=(TOKENS // PROJ_TM,),
        in_specs=[pl.BlockSpec((PROJ_TM, D_MODEL), lambda i: (i, 0)),
                  pl.BlockSpec((1, D_MODEL), lambda i: (0, 0)),
                  pl.BlockSpec((None, D_MODEL, n_in), lambda i: (layer, 0, 0))],
        out_specs=[pl.BlockSpec((PROJ_TM, wd), lambda i: (i, 0)) for wd in PROJ_WIDTHS],
        out_shape=out_shape,
        compiler_params=_cparams(("arbitrary",)),
        name="proj",
    )(x, g, w)


def _gmlp_kernel(u_ref, v_ref, g_ref, ws_ref, bs_ref, o_ref):
    u = jax.nn.gelu(u_ref[...].astype(F32))
    v = _rms(jax.nn.gelu(v_ref[...].astype(F32)), g_ref[...]).astype(BF16)
    row = lax.broadcasted_iota(jnp.int32, (GM_CHUNK, GM_CHUNK), 0)
    col = lax.broadcasted_iota(jnp.int32, (GM_CHUNK, GM_CHUNK), 1)
    causal = row >= col
    for gi in range(GM_GROUPS):
        w = jnp.where(causal, ws_ref[gi], 0.0).astype(BF16)
        cs = slice(gi * GM_GW, (gi + 1) * GM_GW)
        for c in range(GMLP_TM // GM_CHUNK):
            rs = slice(c * GM_CHUNK, (c + 1) * GM_CHUNK)
            mixed = jnp.dot(w, v[rs, cs], preferred_element_type=F32) + bs_ref[gi]
            o_ref[rs, cs] = (u[rs, cs] * mixed).astype(o_ref.dtype)


def _gmlp(u, v, g, ws, bs):
    tok = lambda i: (i, 0)
    return pl.pallas_call(
        _gmlp_kernel,
        grid=(TOKENS // GMLP_TM,),
        in_specs=[pl.BlockSpec((GMLP_TM, MIX), tok),
                  pl.BlockSpec((GMLP_TM, MIX), tok),
                  pl.BlockSpec((1, MIX), lambda i: (0, 0)),
                  pl.BlockSpec((GM_GROUPS, GM_CHUNK, GM_CHUNK), lambda i: (0, 0, 0)),
                  pl.BlockSpec((GM_GROUPS, GM_CHUNK, GM_GW), lambda i: (0, 0, 0))],
        out_specs=pl.BlockSpec((GMLP_TM, MIX), tok),
        out_shape=jax.ShapeDtypeStruct((TOKENS, MIX), BF16),
        compiler_params=_cparams(("arbitrary",)),
        name="gmlp",
    )(u, v, g, ws, bs)


LRU_TAIL = 8


def _lru_kernel(gb_ref, rb_ref, cw_ref, cb_ref, wa_ref, ba_ref, wx_ref, bx_ref, lam_ref, o_ref,
                tail_ref, h_ref):
    @pl.when(pl.program_id(1) == 0)
    def _():
        tail_ref[...] = jnp.zeros_like(tail_ref)
        h_ref[...] = jnp.zeros_like(h_ref)

    ts = LRU_TS
    x = rb_ref[...].astype(F32)
    ext = jnp.concatenate([tail_ref[...], x], axis=0)
    tail_ref[...] = x[ts - LRU_TAIL:, :]
    xc = cb_ref[...] + x * cw_ref[CONV_WIDTH - 1:CONV_WIDTH, :]
    for d in range(1, CONV_WIDTH):
        xs = pltpu.roll(ext, d, axis=0)[LRU_TAIL:, :]
        xc = xc + xs * cw_ref[CONV_WIDTH - 1 - d:CONV_WIDTH - d, :]
    xcb = xc.astype(BF16)
    r = jax.nn.sigmoid(jnp.dot(xcb, wa_ref[...], preferred_element_type=F32) + ba_ref[...])
    ig = jax.nn.sigmoid(jnp.dot(xcb, wx_ref[...], preferred_element_type=F32) + bx_ref[...])
    z = -lam_ref[...]
    softplus = jnp.maximum(z, 0.0) + jnp.log1p(jnp.exp(-jnp.abs(z)))
    log_a = -LRU_C * r * softplus
    a = jnp.exp(log_a)
    b = jnp.sqrt(1.0 - jnp.exp(2.0 * log_a)) * (ig * xc)
    rows = lax.broadcasted_iota(jnp.int32, (ts, 1), 0)
    d = 1
    while d < ts:
        valid = rows >= d
        a_prev = pltpu.roll(a, d, axis=0)
        b_prev = pltpu.roll(b, d, axis=0)
        b = jnp.where(valid, a * b_prev, 0.0) + b
        a = jnp.where(valid, a * a_prev, a)
        d *= 2
    h = a * h_ref[...] + b
    h_ref[...] = h[ts - 1:ts, :]
    o_ref[...] = (jax.nn.gelu(gb_ref[...].astype(F32)) * h).astype(o_ref.dtype)


def _lru(gb, rb, cw, cb, wa, ba, wx, bx, lam):
    nt = SEQ // LRU_TS
    tok = lambda b, s: (b * nt + s, 0)
    vec = pl.BlockSpec((1, MIX), lambda b, s: (0, 0))
    mat = pl.BlockSpec((MIX, MIX), lambda b, s: (0, 0))
    return pl.pallas_call(
        _lru_kernel,
        grid=(BATCH, nt),
        in_specs=[pl.BlockSpec((LRU_TS, MIX), tok), pl.BlockSpec((LRU_TS, MIX), tok),
                  pl.BlockSpec((CONV_WIDTH, MIX), lambda b, s: (0, 0)), vec,
                  mat, vec, mat, vec, vec],
        out_specs=pl.BlockSpec((LRU_TS, MIX), tok),
        out_shape=jax.ShapeDtypeStruct((TOKENS, MIX), BF16),
        scratch_shapes=[pltpu.VMEM((LRU_TAIL, MIX), F32), pltpu.VMEM((1, MIX), F32)],
        compiler_params=_cparams(("arbitrary", "arbitrary")),
        name="rglru",
    )(gb, rb, cw, cb, wa, ba, wx, bx, lam)


POOL_TAIL = 16


def _pool_kernel(x_ref, w_ref, sc_ref, o_ref, tail_ref):
    s_id = pl.program_id(1)

    @pl.when(s_id == 0)
    def _():
        tail_ref[...] = jnp.zeros_like(tail_ref)

    ts = POOL_TS
    x = x_ref[...].astype(F32)
    ext = jnp.concatenate([tail_ref[...], x], axis=0)
    tail_ref[...] = x[ts - POOL_TAIL:, :]
    pos = s_id * ts + lax.broadcasted_iota(jnp.int32, (ts, 1), 0)
    acc = ext
    width = 1
    for gi, wdw in enumerate(POOL_WINDOWS):
        while width < wdw:
            acc = acc + pltpu.roll(acc, width, axis=0)
            width *= 2
        cs = slice(gi * POOL_GW, (gi + 1) * POOL_GW)
        cnt = jnp.minimum(pos + 1, wdw).astype(F32)
        pooled = acc[POOL_TAIL:, cs] / cnt - x[:, cs]
        mixed = jnp.dot(pooled.astype(BF16), w_ref[gi], preferred_element_type=F32)
        o_ref[:, cs] = (mixed * sc_ref[:, cs]).astype(o_ref.dtype)


def _pool(xd, w, sc):
    nt = SEQ // POOL_TS
    tok = lambda b, s: (b * nt + s, 0)
    return pl.pallas_call(
        _pool_kernel,
        grid=(BATCH, nt),
        in_specs=[pl.BlockSpec((POOL_TS, MIX), tok),
                  pl.BlockSpec((len(POOL_WINDOWS), POOL_GW, POOL_GW), lambda b, s: (0, 0, 0)),
                  pl.BlockSpec((1, MIX), lambda b, s: (0, 0))],
        out_specs=pl.BlockSpec((POOL_TS, MIX), tok),
        out_shape=jax.ShapeDtypeStruct((TOKENS, MIX), BF16),
        scratch_shapes=[pltpu.VMEM((POOL_TAIL, MIX), F32)],
        compiler_params=_cparams(("arbitrary", "arbitrary")),
        name="pool",
    )(xd, w, sc)


def _kvprep_kernel(kc_in, vc_in, vs_in, vw_in, pe_ref, w1_ref, w2_ref, kc_ref, vct_ref, vst_ref,
                   vwt_ref, xf_ref):
    row = lax.broadcasted_iota(jnp.int32, (N_CHUNK, 1), 0)
    for c, x_in in enumerate((kc_in, vc_in)):
        xf_ref[...] = x_in[...].astype(F32)
        first = jnp.zeros((N_CHUNK, NSA_KV), F32)
        second = jnp.zeros((N_CHUNK, NSA_KV), F32)
        for l in range(CMP_STRIDE):
            xl = xf_ref[pl.ds(l, N_CHUNK, stride=CMP_STRIDE), :]
            first += jnp.dot((xl + pe_ref[c, l:l + 1, :]).astype(BF16), w1_ref[c, l],
                             preferred_element_type=F32)
            l2 = l + CMP_STRIDE
            second += jnp.dot((xl + pe_ref[c, l2:l2 + 1, :]).astype(BF16), w1_ref[c, l2],
                              preferred_element_type=F32)
        hid = jax.nn.gelu(first + pltpu.roll(second, N_CHUNK - 1, axis=0))
        out = jnp.dot(hid.astype(BF16), w2_ref[c], preferred_element_type=F32)
        out = jnp.where(row < N_CMP, out, 0.0)
        if c == 0:
            kc_ref[0] = out.astype(kc_ref.dtype)
        else:
            vct_ref[0] = out.T.astype(vct_ref.dtype)
    vst_ref[0] = vs_in[...].astype(F32).T.astype(vst_ref.dtype)
    vwt_ref[0] = vw_in[...].astype(F32).T.astype(vwt_ref.dtype)


def _kvprep(kv, pe, w1, w2):
    col = lambda j: pl.BlockSpec((SEQ, NSA_KV), lambda b: (b, j))
    whole = lambda a: pl.BlockSpec(a.shape, lambda b: (0,) * a.ndim)
    return pl.pallas_call(
        _kvprep_kernel,
        grid=(BATCH,),
        in_specs=[col(0), col(1), col(3), col(5), whole(pe), whole(w1), whole(w2)],
        out_specs=[pl.BlockSpec((1, N_CHUNK, NSA_KV), lambda b: (b, 0, 0)),
                   pl.BlockSpec((1, NSA_KV, N_CHUNK), lambda b: (b, 0, 0)),
                   pl.BlockSpec((1, NSA_KV, SEQ), lambda b: (b, 0, 0)),
                   pl.BlockSpec((1, NSA_KV, SEQ), lambda b: (b, 0, 0))],
        out_shape=[jax.ShapeDtypeStruct((BATCH, N_CHUNK, NSA_KV), BF16),
                   jax.ShapeDtypeStruct((BATCH, NSA_KV, N_CHUNK), BF16),
                   jax.ShapeDtypeStruct((BATCH, NSA_KV, SEQ), BF16),
                   jax.ShapeDtypeStruct((BATCH, NSA_KV, SEQ), BF16)],
        scratch_shapes=[pltpu.VMEM((SEQ, NSA_KV), F32)],
        compiler_params=_cparams(("arbitrary",)),
        name="kvprep",
    )(kv, kv, kv, kv, pe, w1, w2)


def _group_diag(w):
    eye = jnp.eye(N_KV, dtype=w.dtype)
    out = jnp.einsum('...gde,gh->...gdhe', w, eye)
    return out.reshape(w.shape[:-3] + (N_KV * w.shape[-2], N_KV * w.shape[-1]))


def _lane_tile(x, n):
    return jnp.concatenate([x] * n, axis=1)


ONES_ROWS = 16


def _with_ones(v_t):
    return jnp.concatenate([v_t, jnp.ones((ONES_ROWS, v_t.shape[1]), BF16)], axis=0)


def _nsa_kernel(q_ref, kc_ref, vct_ref, ks_ref, vst_ref, kw_ref, vwt_ref, gt_ref, ovt_ref, blk_ref,
                y_ref, sa_ref, sb_ref, oslc_ref):
    i = pl.program_id(1)
    tq = NSA_TQ
    nl = HPG * tq
    t_row = i * tq + lax.broadcasted_iota(jnp.int32, (1, tq), 1)
    groups = range(N_KV)
    vrows = [slice(g * HEAD_DIM, (g + 1) * HEAD_DIM) for g in groups]

    def scores(k, q_t):
        return jnp.dot(k, q_t, preferred_element_type=F32)

    def normalized(acc):
        return acc[:HEAD_DIM, :] / acc[HEAD_DIM:HEAD_DIM + 1, :]

    lane = lax.broadcasted_iota(jnp.int32, (tq, LANES), 1)
    scale = HEAD_DIM ** -0.5
    q4 = [jnp.concatenate(
        [(jnp.where((lane >= HEAD_DIM) == (g == 1),
                    q_ref[:, j * LANES:(j + 1) * LANES].astype(F32), 0.0) * scale).T
         for j in range(HPG)], axis=1).astype(BF16) for g in groups]

    n_idx = lax.broadcasted_iota(jnp.int32, (N_CHUNK, 1), 0)
    ok = n_idx * CMP_STRIDE + (CMP_LEN - 1) <= _lane_tile(t_row, HPG)
    blk = lax.broadcasted_iota(jnp.int32, (N_SLC, 1), 0)
    cur = jnp.right_shift(t_row, 6)
    forced = (blk == 0) | (blk == cur) | (blk == cur - 1)
    causal_blk = blk * SLC_LEN <= t_row
    o_cmp, q_aug = [], []
    for g in groups:
        s = jnp.where(ok, scores(kc_ref[0], q4[g]), NEG_INF)
        m = jnp.max(s, axis=0, keepdims=True)
        e = jnp.where(ok, jnp.exp(s - m), 0.0)
        l = jnp.sum(e, axis=0, keepdims=True)
        p = e / jnp.where(l > 0.0, l, 1.0)
        o_cmp.append(jnp.dot(vct_ref[0, vrows[g], :], p.astype(BF16), preferred_element_type=F32))
        psum = p[:, 0:tq]
        for j in range(1, HPG):
            psum = psum + p[:, j * tq:(j + 1) * tq]
        p_hi = psum.astype(BF16)
        p_lo = (psum - p_hi.astype(F32)).astype(BF16)
        imp = (jnp.dot(ovt_ref[...], p_hi, preferred_element_type=F32)
               + jnp.dot(ovt_ref[...], p_lo, preferred_element_type=F32))
        work = jnp.where(forced, -3e38, jnp.where(causal_blk, imp, -1.0))
        sel = jnp.where(forced, 1.0, 0.0)
        for _ in range(SLC_TOPN - MAX_FORCED):
            mx = jnp.max(work, axis=0, keepdims=True)
            idx = jnp.min(jnp.where(work == mx, blk, N_SLC), axis=0, keepdims=True)
            pick = blk == idx
            sel = jnp.where(pick & (mx >= 0.0), 1.0, sel)
            work = jnp.where(pick, -3e38, work)
        sel_bias = jnp.where(sel > 0.0, 0.0, NEG_INF)
        sel_bias = jnp.concatenate([sel_bias, jnp.zeros((LANES - N_SLC, tq), F32)], axis=0)
        sel_bias = _lane_tile(sel_bias.astype(BF16), HPG)
        q_aug.append(jnp.concatenate([q4[g], sel_bias], axis=0))

    start = pl.multiple_of(jnp.maximum(i - WIN // tq, 0) * tq, tq)
    k_win = kw_ref[pl.ds(start, WIN_KEYS), :]
    delta = t_row - (start + lax.broadcasted_iota(jnp.int32, (WIN_KEYS, 1), 0))
    win_bias = _lane_tile(jnp.where((delta >= 0) & (delta < WIN), 0.0, NEG_INF), HPG)
    o_win = []
    for g in groups:
        sT = scores(k_win, q4[g]) + win_bias
        pT = jnp.exp((sT - jnp.max(sT, axis=0, keepdims=True)).astype(BF16))
        vT = _with_ones(vwt_ref[vrows[g], pl.ds(start, WIN_KEYS)])
        o_win.append(normalized(jnp.dot(vT, pT, preferred_element_type=F32)))

    def slc_scores(kt, dst_ref):
        k0 = pl.multiple_of(kt * SLC_KT, SLC_KT)
        k_aug = jnp.concatenate([ks_ref[pl.ds(k0, SLC_KT), :], blk_ref[pl.ds(k0, SLC_KT), :]],
                                axis=1)
        maxima = []
        for g in groups:
            s_t = scores(k_aug, q_aug[g])
            dst_ref[g] = s_t
            maxima.append(jnp.max(s_t, axis=0, keepdims=True))
        return maxima

    def slc_update(kt, state, s_t, tile_max):
        k0 = pl.multiple_of(kt * SLC_KT, SLC_KT)
        out = []
        for g in groups:
            m_i, acc = state[2 * g], state[2 * g + 1]
            m_new = jnp.maximum(m_i, tile_max[g])
            alpha = jnp.exp(m_i - m_new)
            pT = jnp.exp((s_t[g] - m_new).astype(BF16))
            vT = _with_ones(vst_ref[vrows[g], pl.ds(k0, SLC_KT)])
            out += [m_new, alpha * acc + jnp.dot(vT, pT, preferred_element_type=F32)]
        return out

    def from_ref(src_ref):
        return [src_ref[g] for g in groups]

    def slc_pair(p, carry):
        state, max_a = list(carry[:2 * N_KV]), list(carry[2 * N_KV:])
        max_b = slc_scores(2 * p + 1, sb_ref)
        state = slc_update(2 * p, state, from_ref(sa_ref), max_a)
        max_a = slc_scores(2 * p + 2, sa_ref)
        state = slc_update(2 * p + 1, state, from_ref(sb_ref), max_b)
        return tuple(state + max_a)

    def slc_finish(state, src_ref):
        kpos = last_kt * SLC_KT + lax.broadcasted_iota(jnp.int32, (SLC_KT, 1), 0)
        causal_bias = _lane_tile(jnp.where(kpos <= t_row, 0.0, NEG_INF), HPG)
        s_last = [s + causal_bias for s in from_ref(src_ref)]
        state = slc_update(last_kt, state, s_last,
                           [jnp.max(s, axis=0, keepdims=True) for s in s_last])
        for g in groups:
            oslc_ref[g] = normalized(state[2 * g + 1])

    last_kt = (i * tq) // SLC_KT
    init = ((jnp.full((1, nl), -3e38, F32), jnp.zeros((HEAD_DIM + ONES_ROWS, nl), F32)) * N_KV
            + tuple(slc_scores(0, sa_ref)))
    carry = lax.fori_loop(0, last_kt // 2, slc_pair, init)
    state, max_a = list(carry[:2 * N_KV]), list(carry[2 * N_KV:])
    odd = lax.rem(last_kt, 2) == 1

    @pl.when(odd)
    def _():
        slc_scores(last_kt, sb_ref)
        slc_finish(slc_update(last_kt - 1, state, from_ref(sa_ref), max_a), sb_ref)

    @pl.when(jnp.logical_not(odd))
    def _():
        slc_finish(state, sa_ref)

    o_slc = [oslc_ref[g] for g in groups]

    for g in groups:
        gates = jax.nn.sigmoid(gt_ref[0, g])

        def gate_row(br):
            return jnp.concatenate([gates[3 * j + br:3 * j + br + 1, :] for j in range(HPG)],
                                   axis=1)

        o = gate_row(0) * o_cmp[g] + gate_row(1) * o_slc[g] + gate_row(2) * o_win[g]
        for j in range(HPG):
            h0 = (g * HPG + j) * HEAD_DIM
            y_ref[0, h0:h0 + HEAD_DIM, :] = o[:, j * tq:(j + 1) * tq].astype(y_ref.dtype)


def _block_onehot():
    oh = (np.arange(SEQ)[:, None] // SLC_LEN) == np.arange(LANES)[None, :]
    return jnp.asarray(oh.astype(np.float32), dtype=BF16)


def _nsa(q, kc, vct, kv, vst, vwt, gt, ovt):
    nq = SEQ // NSA_TQ
    return pl.pallas_call(
        _nsa_kernel,
        grid=(BATCH, nq),
        in_specs=[pl.BlockSpec((NSA_TQ, NSA_Q), lambda b, i: (b * nq + i, 0)),
                  pl.BlockSpec((1, N_CHUNK, NSA_KV), lambda b, i: (b, 0, 0)),
                  pl.BlockSpec((1, NSA_KV, N_CHUNK), lambda b, i: (b, 0, 0)),
                  pl.BlockSpec((SEQ, NSA_KV), lambda b, i: (b, 2)),
                  pl.BlockSpec((None, NSA_KV, SEQ), lambda b, i: (b, 0, 0)),
                  pl.BlockSpec((SEQ, NSA_KV), lambda b, i: (b, 4)),
                  pl.BlockSpec((None, NSA_KV, SEQ), lambda b, i: (b, 0, 0)),
                  pl.BlockSpec((1, N_KV, GATE_ROWS, NSA_TQ), lambda b, i: (b, 0, 0, i)),
                  pl.BlockSpec((N_SLC, N_CHUNK), lambda b, i: (0, 0)),
                  pl.BlockSpec((SEQ, LANES), lambda b, i: (0, 0))],
        out_specs=pl.BlockSpec((1, NSA_Q, NSA_TQ), lambda b, i: (b, 0, i)),
        out_shape=jax.ShapeDtypeStruct((BATCH, NSA_Q, SEQ), BF16),
        scratch_shapes=[pltpu.VMEM((N_KV, SLC_KT, HPG * NSA_TQ), F32),
                        pltpu.VMEM((N_KV, SLC_KT, HPG * NSA_TQ), F32),
                        pltpu.VMEM((N_KV, HEAD_DIM, HPG * NSA_TQ), F32)],
        compiler_params=_cparams(("arbitrary", "arbitrary")),
        name="nsa",
    )(q, kc, vct, kv, vst, kv, vwt, gt, ovt, _block_onehot())


def _overlap_t():
    c_start = np.arange(N_CHUNK) * CMP_STRIDE
    s_start = np.arange(N_SLC) * SLC_LEN
    ov = ((c_start[None, :] <= s_start[:, None] + SLC_LEN - 1)
          & (c_start[None, :] + CMP_LEN - 1 >= s_start[:, None])
          & (np.arange(N_CHUNK)[None, :] < N_CMP))
    return jnp.asarray(ov.astype(np.float32), dtype=BF16)


def _merge_kernel(x_ref, g_ref, wmg_ref, ya_ref, yb_ref, yct_ref, yd_ref, wb_ref, wo_ref, o_ref):
    x = x_ref[...]
    nb = _rms(x, g_ref[...]).astype(BF16)
    yc = yct_ref[0].astype(F32).T.astype(BF16)
    ys = (ya_ref[...], yb_ref[...], yc, yd_ref[...])
    merged = jnp.zeros((MERGE_TM, D_MODEL), F32)
    for bi, y in enumerate(ys):
        gate = jax.nn.sigmoid(jnp.dot(nb, wmg_ref[:, bi * D_MODEL:(bi + 1) * D_MODEL],
                                      preferred_element_type=F32))
        merged = merged + gate * jnp.dot(y, wb_ref[bi], preferred_element_type=F32)
    o_ref[...] = x + jnp.dot(merged.astype(BF16), wo_ref[...], preferred_element_type=F32)


def _merge(x, g, wmg, ya, yb, yct, yd, wb, wo, layer):
    nt = SEQ // MERGE_TM
    tok = lambda i: (i, 0)
    const2 = lambda i: (0, 0)
    return pl.pallas_call(
        _merge_kernel,
        grid=(TOKENS // MERGE_TM,),
        in_specs=[pl.BlockSpec((MERGE_TM, D_MODEL), tok),
                  pl.BlockSpec((1, D_MODEL), const2),
                  pl.BlockSpec((None, D_MODEL, N_BRANCH * D_MODEL), lambda i: (layer, 0, 0)),
                  pl.BlockSpec((MERGE_TM, MIX), tok),
                  pl.BlockSpec((MERGE_TM, MIX), tok),
                  pl.BlockSpec((1, MIX, MERGE_TM), lambda i: (i // nt, 0, i % nt)),
                  pl.BlockSpec((MERGE_TM, MIX), tok),
                  pl.BlockSpec((None, N_BRANCH, MIX, D_MODEL), lambda i: (layer, 0, 0, 0)),
                  pl.BlockSpec((None, D_MODEL, D_MODEL), lambda i: (layer, 0, 0))],
        out_specs=pl.BlockSpec((MERGE_TM, D_MODEL), tok),
        out_shape=jax.ShapeDtypeStruct((TOKENS, D_MODEL), F32),
        compiler_params=_cparams(("arbitrary",)),
        name="merge",
    )(x, g, wmg, ya, yb, yct, yd, wb, wo)


def _router_logits(t, wr_ref, br_ref):
    w = wr_ref[...]
    t_hi, w_hi = t.astype(BF16), w.astype(BF16)
    t_lo = (t - t_hi.astype(F32)).astype(BF16)
    w_lo = (w - w_hi.astype(F32)).astype(BF16)
    dot = functools.partial(jnp.dot, preferred_element_type=F32)
    return dot(t_hi, w_hi) + dot(t_hi, w_lo) + dot(t_lo, w_hi) + br_ref[...]


def _top_group(logits):
    lane = lax.broadcasted_iota(jnp.int32, logits.shape, 1)
    is_grp = lane < N_GROUPS
    lg = jnp.where(is_grp, logits, NEG_INF)
    gmax = jnp.max(lg, axis=1, keepdims=True)
    grp_idx = jnp.min(jnp.where(is_grp & (lg == gmax), lane, ROUTER_PAD), axis=1, keepdims=True)
    return lg, gmax, grp_idx


def _combine_weights(logits, grp_idx):
    lane = lax.broadcasted_iota(jnp.int32, logits.shape, 1)
    lg, gmax, _ = _top_group(logits)
    ge = jnp.where(lane < N_GROUPS, jnp.exp(lg - gmax), 0.0)
    grp_w = (jnp.sum(jnp.where(lane == grp_idx, ge, 0.0), axis=1, keepdims=True)
             / jnp.sum(ge, axis=1, keepdims=True))
    e_idx = lane - N_GROUPS
    in_grp = (e_idx >= grp_idx * EXPERTS_PER_GROUP) & (e_idx < (grp_idx + 1) * EXPERTS_PER_GROUP)
    le = jnp.where(in_grp, logits, NEG_INF)
    l1 = jnp.max(le, axis=1, keepdims=True)
    i1 = jnp.min(jnp.where(in_grp & (le == l1), lane, ROUTER_PAD), axis=1, keepdims=True)
    le2 = jnp.where(lane == i1, NEG_INF, le)
    l2 = jnp.max(le2, axis=1, keepdims=True)
    i2 = jnp.min(jnp.where(in_grp & (lane != i1) & (le2 == l2), lane, ROUTER_PAD), axis=1,
                 keepdims=True)
    e2 = jnp.exp(l2 - l1)
    den = 1.0 + e2
    w1 = grp_w / den
    w2 = grp_w * e2 / den
    return jnp.where(lane == i1, w1, 0.0) + jnp.where(lane == i2, w2, 0.0)


def _route_kernel(x_ref, g_ref, wr_ref, br_ref, tri_ref, meta_ref, cnt_ref, run_ref):
    @pl.when(pl.program_id(0) == 0)
    def _():
        run_ref[...] = jnp.zeros_like(run_ref)

    logits = _router_logits(_rms(x_ref[...], g_ref[...]), wr_ref, br_ref)
    _, _, grp_idx = _top_group(logits)
    lane = lax.broadcasted_iota(jnp.int32, logits.shape, 1)
    onehot = jnp.where(lane == grp_idx, 1.0, 0.0)
    before = jnp.dot(tri_ref[...], onehot.astype(BF16), preferred_element_type=F32) + run_ref[...]
    rank = jnp.sum(onehot * before, axis=1, keepdims=True).astype(jnp.int32)
    run_ref[...] += jnp.sum(onehot, axis=0, keepdims=True)
    meta_ref[...] = jnp.where(lane == 0, grp_idx, jnp.where(lane == 1, rank, 0))
    cnt_ref[...] = run_ref[...]


def _route(x, g, wr, br):
    tok = lambda i: (i, 0)
    const2 = lambda i: (0, 0)
    tri = jnp.asarray(np.tril(np.ones((ROUTE_TM, ROUTE_TM), np.float32), -1), dtype=BF16)
    return pl.pallas_call(
        _route_kernel,
        grid=(TOKENS // ROUTE_TM,),
        in_specs=[pl.BlockSpec((ROUTE_TM, D_MODEL), tok),
                  pl.BlockSpec((1, D_MODEL), const2),
                  pl.BlockSpec((D_MODEL, ROUTER_PAD), const2),
                  pl.BlockSpec((1, ROUTER_PAD), const2),
                  pl.BlockSpec((ROUTE_TM, ROUTE_TM), const2)],
        out_specs=[pl.BlockSpec((ROUTE_TM, ROUTER_PAD), tok), pl.BlockSpec((1, ROUTER_PAD), const2)],
        out_shape=[jax.ShapeDtypeStruct((TOKENS, ROUTER_PAD), jnp.int32),
                   jax.ShapeDtypeStruct((1, ROUTER_PAD), F32)],
        scratch_shapes=[pltpu.VMEM((1, ROUTER_PAD), F32)],
        compiler_params=_cparams(("arbitrary",)),
        name="route",
    )(x, g, wr, br, tri)


MOVE_SLOTS = 3


def _move_rows_kernel(idx_ref, pad_ref, src_hbm, dst_hbm, buf, zbuf, in_sem, out_sem, pad_sem, *,
                      scatter):
    c = pl.program_id(0)
    n = pl.num_programs(0)
    slot = lax.rem(c, MOVE_SLOTS)
    nxt = lax.rem(c + 1, MOVE_SLOTS)
    rc = ROW_CHUNK
    chunk = lambda k: pl.ds(k * rc, rc)

    if scatter:
        def read(k, sl):
            return [pltpu.make_async_copy(src_hbm.at[chunk(k)], buf.at[sl], in_sem.at[sl])]

        def write(k, sl):
            return [pltpu.make_async_copy(buf.at[sl, pl.ds(r, 1)],
                                          dst_hbm.at[pl.ds(idx_ref[k * rc + r], 1)], out_sem.at[sl])
                    for r in range(rc)]

        def write_done(sl):
            return pltpu.make_async_copy(buf.at[sl], dst_hbm.at[chunk(0)], out_sem.at[sl])

        def read_done(sl):
            return read(0, sl)[0]
    else:
        def read(k, sl):
            return [pltpu.make_async_copy(src_hbm.at[pl.ds(idx_ref[k * rc + r], 1)],
                                          buf.at[sl, pl.ds(r, 1)], in_sem.at[sl])
                    for r in range(rc)]

        def write(k, sl):
            return [pltpu.make_async_copy(buf.at[sl], dst_hbm.at[chunk(k)], out_sem.at[sl])]

        def write_done(sl):
            return write(0, sl)[0]

        def read_done(sl):
            return pltpu.make_async_copy(src_hbm.at[chunk(0)], buf.at[sl], in_sem.at[sl])

    def pad_copies(k0):
        return [pltpu.make_async_copy(zbuf.at[pl.ds(r, 1)], dst_hbm.at[pl.ds(pad_ref[k0 + r], 1)],
                                      pad_sem) for r in range(rc)]

    n_pad_chunks = (MOE_ROWS - TOKENS) // rc

    @pl.when(c == 0)
    def _():
        for cp in read(0, 0):
            cp.start()
        if scatter:
            zbuf[...] = jnp.zeros_like(zbuf)
            for k in range(n_pad_chunks):
                for cp in pad_copies(k * rc):
                    cp.start()

    @pl.when(c + 1 < n)
    def _():
        @pl.when(c >= 2)
        def _():
            write_done(nxt).wait()
        for cp in read(c + 1, nxt):
            cp.start()

    read_done(slot).wait()
    for cp in write(c, slot):
        cp.start()

    @pl.when(c == n - 1)
    def _():
        write_done(nxt).wait()
        write_done(lax.rem(c + 2, MOVE_SLOTS)).wait()
        write_done(slot).wait()
        if scatter:
            for k in range(n_pad_chunks):
                pltpu.make_async_copy(zbuf, dst_hbm.at[chunk(0)], pad_sem).wait()


def _move_rows(idx, pad, src, n_out, scatter):
    any_spec = pl.BlockSpec(memory_space=pl.ANY)
    grid_spec = pltpu.PrefetchScalarGridSpec(
        num_scalar_prefetch=2, grid=(TOKENS // ROW_CHUNK,), in_specs=[any_spec], out_specs=any_spec,
        scratch_shapes=[pltpu.VMEM((MOVE_SLOTS, ROW_CHUNK, D_MODEL), F32),
                        pltpu.VMEM((ROW_CHUNK, D_MODEL), F32),
                        pltpu.SemaphoreType.DMA((MOVE_SLOTS,)),
                        pltpu.SemaphoreType.DMA((MOVE_SLOTS,)),
                        pltpu.SemaphoreType.DMA(())])
    return pl.pallas_call(
        functools.partial(_move_rows_kernel, scatter=scatter),
        grid_spec=grid_spec,
        out_shape=jax.ShapeDtypeStruct((n_out, D_MODEL), F32),
        compiler_params=_cparams(("arbitrary",)),
        name="dispatch" if scatter else "collect",
    )(idx, pad, src)


def _experts_kernel(tg_ref, tv_ref, x_ref, g_ref, wr_ref, br_ref, wg_ref, wu_ref, wd_ref, fg_ref,
                    o_ref, *, final_norm):
    j = pl.program_id(0)
    nv = tv_ref[j]

    @pl.when(nv > 0)
    def _():
        x = x_ref[...]
        t = _rms(x, g_ref[...])
        grp = tg_ref[j]
        comb = _combine_weights(_router_logits(t, wr_ref, br_ref), grp)
        lane = lax.broadcasted_iota(jnp.int32, comb.shape, 1)
        tb = t.astype(BF16)
        acc = jnp.zeros((MOE_TM, D_MODEL), F32)
        for e in range(EXPERTS_PER_GROUP):
            ce = jnp.sum(jnp.where(lane == N_GROUPS + grp * EXPERTS_PER_GROUP + e, comb, 0.0),
                         axis=1, keepdims=True)
            hid = (jax.nn.silu(jnp.dot(tb, wg_ref[e], preferred_element_type=F32))
                   * jnp.dot(tb, wu_ref[e], preferred_element_type=F32)) * ce
            acc = acc + jnp.dot(hid.astype(BF16), wd_ref[e], preferred_element_type=F32)
        h = x + acc
        if final_norm:
            h = _rms(h, fg_ref[...])
        o_ref[...] = h

    @pl.when(nv == 0)
    def _():
        o_ref[...] = jnp.zeros_like(o_ref)


def _experts(tile_group, tile_valid, hs, g, wr, br, wg, wu, wd, fg, layer, final_norm):
    const2 = lambda j, *_: (0, 0)
    rows = lambda j, *_: (j, 0)
    grp_w = lambda j, tg, tv: (layer * N_GROUPS + tg[j], 0, 0)
    grid_spec = pltpu.PrefetchScalarGridSpec(
        num_scalar_prefetch=2,
        grid=(MOE_NT,),
        in_specs=[pl.BlockSpec((MOE_TM, D_MODEL), rows),
                  pl.BlockSpec((1, D_MODEL), const2),
                  pl.BlockSpec((D_MODEL, ROUTER_PAD), const2),
                  pl.BlockSpec((1, ROUTER_PAD), const2),
                  pl.BlockSpec((EXPERTS_PER_GROUP, D_MODEL, D_EXPERT), grp_w),
                  pl.BlockSpec((EXPERTS_PER_GROUP, D_MODEL, D_EXPERT), grp_w),
                  pl.BlockSpec((EXPERTS_PER_GROUP, D_EXPERT, D_MODEL), grp_w),
                  pl.BlockSpec((1, D_MODEL), const2)],
        out_specs=pl.BlockSpec((MOE_TM, D_MODEL), rows))
    return pl.pallas_call(
        functools.partial(_experts_kernel, final_norm=final_norm),
        grid_spec=grid_spec,
        out_shape=jax.ShapeDtypeStruct((MOE_ROWS, D_MODEL), F32),
        compiler_params=_cparams(("arbitrary",)),
        name="experts",
    )(tile_group, tile_valid, hs, g, wr, br, wg, wu, wd, fg)


def _moe(x, g, wr, br, wg, wu, wd, fg, layer, final_norm):
    meta, cnt = _route(x, g, wr, br)
    grp, rank = meta[:, 0], meta[:, 1]
    counts = cnt[0, :N_GROUPS].astype(jnp.int32)
    padded = (counts + MOE_TM - 1) // MOE_TM * MOE_TM
    ends = jnp.cumsum(padded)
    starts = ends - padded
    pos = starts[grp] + rank
    seg_start = jnp.concatenate([starts + counts, ends[-1:]])
    seg_len = jnp.concatenate([padded - counts, MOE_ROWS - ends[-1:]])
    seg_first = jnp.cumsum(seg_len) - seg_len
    k = jnp.arange(MOE_ROWS - TOKENS, dtype=jnp.int32)
    seg = jnp.sum(k[:, None] >= jnp.cumsum(seg_len)[None, :], axis=1)
    pad = (seg_start[seg] + k - seg_first[seg]).astype(jnp.int32)
    tile_start = jnp.arange(MOE_NT, dtype=jnp.int32) * MOE_TM
    tile_group = jnp.minimum(jnp.sum(tile_start[:, None] >= ends[None, :], axis=1), N_GROUPS - 1)
    tile_group = tile_group.astype(jnp.int32)
    tile_valid = jnp.clip(starts[tile_group] + counts[tile_group] - tile_start, 0, MOE_TM)
    hs = _move_rows(pos, pad, x, MOE_ROWS, scatter=True)
    ys = _experts(tile_group, tile_valid.astype(jnp.int32), hs, g, wr, br, wg, wu, wd, fg, layer,
                  final_norm)
    return _move_rows(pos, pad, ys, TOKENS, scatter=False)


def _block_diag(w):
    eye = jnp.eye(LRU_BLOCKS, dtype=w.dtype)
    return jnp.einsum('hij,hk->hikj', w, eye).reshape(MIX, MIX)


def _split_w_in(w_in):
    cuts = [int(c) for c in np.cumsum((0,) + IN_SPLITS)]
    cols = lambda a, b: w_in[:, :, a:b]
    q_parts = [cols(cuts[4] + (g * HPG + j) * HEAD_DIM, cuts[4] + (g * HPG + j + 1) * HEAD_DIM)
               for j in range(HPG) for g in range(N_KV)]
    gate_pad = jnp.zeros((DEPTH, D_MODEL, GATE_PAD - IN_SPLITS[6]), w_in.dtype)
    w_proj = jnp.concatenate([cols(cuts[0], cuts[4])] + q_parts
                             + [cols(cuts[5], cuts[6]), cols(cuts[7], cuts[8]),
                                cols(cuts[6], cuts[7]), gate_pad], axis=2)
    return w_proj.astype(BF16), cols(cuts[8], cuts[9]).astype(BF16)


def _nsa_mixer(q, kv, ng, p):
    w1 = p['cmp_w1'].reshape(2, N_KV, CMP_LEN, HEAD_DIM, HEAD_DIM).transpose(0, 2, 1, 3, 4)
    pe = jnp.concatenate([p['cmp_pe']] * N_KV, axis=-1)
    kc, vct, vst, vwt = _kvprep(kv, pe, _group_diag(w1).astype(BF16),
                                _group_diag(p['cmp_w2']).astype(BF16))
    gt = ng[:, :3 * N_HEADS].reshape(BATCH, SEQ, N_KV, 3 * HPG).transpose(0, 2, 3, 1)
    gt = jnp.pad(gt, ((0, 0), (0, 0), (0, GATE_ROWS - 3 * HPG), (0, 0)))
    return _nsa(q, kc, vct, kv, vst, vwt, gt, _overlap_t())


def _layer(h, p, big, layer, final_g, final_norm):
    row = lambda a: a.reshape(1, -1)

    u, v, gb, rb, q, kv, xd, ng = _proj(h, row(p['norm1_g']), big['w_proj'], layer)

    bs = jnp.broadcast_to(p['gm_b'][:, :, None], (GM_GROUPS, GM_CHUNK, GM_GW))
    y_a = _gmlp(u, v, row(p['gm_norm_g']), p['gm_ws'], bs)

    y_b = _lru(gb, rb, p['conv_w'], row(p['conv_b']), _block_diag(p['lru_wa']).astype(BF16),
               row(p['lru_ba']), _block_diag(p['lru_wx']).astype(BF16), row(p['lru_bx']),
               row(p['lru_lambda']))

    y_d = _pool(xd, p['pool_w'].astype(BF16), row(p['pool_scale']))

    y_ct = _nsa_mixer(q, kv, ng, p)

    h = _merge(h, row(p['norm1_g']), big['w_mg'], y_a, y_b, y_ct, y_d, big['w_branch'],
               big['w_out'], layer)

    wr = jnp.concatenate([p['router_w_group'], p['router_w_expert']], axis=1)
    wr = jnp.pad(wr, ((0, 0), (0, ROUTER_PAD - wr.shape[1])))
    br = jnp.concatenate([p['router_b_group'], p['router_b_expert']])
    br = jnp.pad(br, (0, ROUTER_PAD - br.shape[0])).reshape(1, ROUTER_PAD)
    return _moe(h, row(p['norm2_g']), wr, br, big['moe_w_gate'], big['moe_w_up'],
                big['moe_w_down'], row(final_g), layer, final_norm)


_LAYER_PARAMS = ('norm1_g', 'gm_norm_g', 'gm_ws', 'gm_b', 'conv_w', 'conv_b', 'lru_wa',
                 'lru_ba', 'lru_wx', 'lru_bx', 'lru_lambda', 'cmp_pe', 'cmp_w1', 'cmp_w2', 'pool_w',
                 'pool_scale', 'norm2_g', 'router_w_group', 'router_b_group',
                 'router_w_expert', 'router_b_expert')


def kernel(x, norm1_g, w_in, gm_norm_g, gm_ws, gm_b, conv_w, conv_b, lru_wa, lru_ba, lru_wx,
           lru_bx, lru_lambda, cmp_pe, cmp_w1, cmp_w2, pool_w, pool_scale, w_branch, w_out,
           norm2_g, router_w_group, router_b_group, router_w_expert, router_b_expert,
           moe_w_gate, moe_w_up, moe_w_down, final_norm_g):
    stacked = dict(zip(_LAYER_PARAMS, (
        norm1_g, gm_norm_g, gm_ws, gm_b, conv_w, conv_b, lru_wa, lru_ba, lru_wx, lru_bx,
        lru_lambda, cmp_pe, cmp_w1, cmp_w2, pool_w, pool_scale, norm2_g,
        router_w_group, router_b_group, router_w_expert, router_b_expert)))
    w_proj, w_mg = _split_w_in(w_in)
    experts = lambda w: w.astype(BF16).reshape((DEPTH * N_EXPERTS,) + w.shape[2:])
    big = dict(w_proj=w_proj, w_mg=w_mg, w_branch=w_branch.astype(BF16), w_out=w_out.astype(BF16),
               moe_w_gate=experts(moe_w_gate), moe_w_up=experts(moe_w_up),
               moe_w_down=experts(moe_w_down))
    h = x.reshape(TOKENS, D_MODEL)
    for layer in range(DEPTH):
        p = {k: a[layer] for k, a in stacked.items()}
        h = _layer(h, p, big, layer, final_norm_g, final_norm=(layer == DEPTH - 1))
    return h.reshape(BATCH, SEQ, D_MODEL)
```

```python
import functools

import numpy as np
import jax
import jax.numpy as jnp
from jax import lax
from jax.experimental import pallas as pl
from jax.experimental.pallas import tpu as pltpu

F32 = jnp.float32
BF16 = jnp.bfloat16

D_MODEL = 1024
BATCH = 4
SEQ = 4096
TOKENS = BATCH * SEQ
DEPTH = 2
MIX = D_MODEL // 2
GM_CHUNK = 128
GM_GROUPS = 4
GM_GW = MIX // GM_GROUPS
CONV_WIDTH = 4
LRU_BLOCKS = 8
LRU_BW = MIX // LRU_BLOCKS
LRU_C = 8.0
N_HEADS = 8
HEAD_DIM = MIX // N_HEADS
N_KV = 2
HPG = N_HEADS // N_KV
CMP_LEN = 32
CMP_STRIDE = 16
SLC_LEN = 64
SLC_TOPN = 8
MAX_FORCED = 3
WIN = 512
NSA_Q = N_HEADS * HEAD_DIM
NSA_KV = N_KV * HEAD_DIM
POOL_WINDOWS = (2, 4, 8, 16)
POOL_GW = MIX // len(POOL_WINDOWS)
N_BRANCH = 4
N_GROUPS = 4
EXPERTS_PER_GROUP = 4
N_EXPERTS = N_GROUPS * EXPERTS_PER_GROUP
D_EXPERT = D_MODEL // 2
EPS = 1e-6
NEG_INF = -1e30
FORCE_SCORE = 1e6
IN_SPLITS = (MIX, MIX, MIX, MIX, NSA_Q, 6 * NSA_KV, 3 * N_HEADS, MIX, N_BRANCH * D_MODEL)

N_CHUNK = SEQ // CMP_STRIDE
N_CMP = N_CHUNK - CMP_LEN // CMP_STRIDE + 1
N_SLC = SEQ // SLC_LEN

LANES = 128
SUBLANES = 8
GATE_PAD = LANES
GATE_ROWS = 16
ROUTER_PAD = LANES
VMEM_LIMIT = 56 * 1024 * 1024

PROJ_WIDTHS = (MIX, MIX, MIX, MIX, NSA_Q, 6 * NSA_KV, MIX, GATE_PAD)
PROJ_TM = 512
GMLP_TM = 512
LRU_TS = 512
POOL_TS = 512
NSA_TQ = 256
SLC_KT = 256
WIN_KEYS = WIN + NSA_TQ
MERGE_TM = 256
ROUTE_TM = 512
MOE_TM = 256
N_PAIRS = EXPERTS_PER_GROUP * (EXPERTS_PER_GROUP - 1) // 2
N_CLASSES = N_GROUPS * N_PAIRS
MOE_ROWS = TOKENS + N_CLASSES * MOE_TM
MOE_NT = MOE_ROWS // MOE_TM
ROW_CHUNK = 256


def _cparams(sem):
    return pltpu.CompilerParams(dimension_semantics=sem, vmem_limit_bytes=VMEM_LIMIT)


def _rms(x, g):
    return x * lax.rsqrt(jnp.mean(x * x, axis=-1, keepdims=True) + EPS) * g


def _proj_kernel(x_ref, g_ref, w_ref, *out_refs):
    nb = _rms(x_ref[...], g_ref[...]).astype(BF16)
    off = 0
    for ref in out_refs:
        w = ref.shape[-1]
        ref[...] = jnp.dot(nb, w_ref[:, off:off + w], preferred_element_type=F32).astype(ref.dtype)
        off += w


def _proj(x, g, w, layer):
    n_in = sum(PROJ_WIDTHS)
    out_shape = [jax.ShapeDtypeStruct((TOKENS, wd), BF16) for wd in PROJ_WIDTHS[:-1]]
    out_shape.append(jax.ShapeDtypeStruct((TOKENS, GATE_PAD), F32))
    return pl.pallas_call(
        _proj_kernel,
        grid=(TOKENS // PROJ_TM,),
        in_specs=[pl.BlockSpec((PROJ_TM, D_MODEL), lambda i: (i, 0)),
                  pl.BlockSpec((1, D_MODEL), lambda i: (0, 0)),
                  pl.BlockSpec((None, D_MODEL, n_in), lambda i: (layer, 0, 0))],
        out_specs=[pl.BlockSpec((PROJ_TM, wd), lambda i: (i, 0)) for wd in PROJ_WIDTHS],
        out_shape=out_shape,
        compiler_params=_cparams(("arbitrary",)),
        name="proj",
    )(x, g, w)


def _gmlp_kernel(u_ref, v_ref, g_ref, ws_ref, bs_ref, o_ref):
    u = jax.nn.gelu(u_ref[...].astype(F32))
    v = _rms(jax.nn.gelu(v_ref[...].astype(F32)), g_ref[...]).astype(BF16)
    row = lax.broadcasted_iota(jnp.int32, (GM_CHUNK, GM_CHUNK), 0)
    col = lax.broadcasted_iota(jnp.int32, (GM_CHUNK, GM_CHUNK), 1)
    causal = row >= col
    for gi in range(GM_GROUPS):
        w = jnp.where(causal, ws_ref[gi], 0.0).astype(BF16)
        cs = slice(gi * GM_GW, (gi + 1) * GM_GW)
        for c in range(GMLP_TM // GM_CHUNK):
            rs = slice(c * GM_CHUNK, (c + 1) * GM_CHUNK)
            mixed = jnp.dot(w, v[rs, cs], preferred_element_type=F32) + bs_ref[gi]
            o_ref[rs, cs] = (u[rs, cs] * mixed).astype(o_ref.dtype)


def _gmlp(u, v, g, ws, bs):
    tok = lambda i: (i, 0)
    return pl.pallas_call(
        _gmlp_kernel,
        grid=(TOKENS // GMLP_TM,),
        in_specs=[pl.BlockSpec((GMLP_TM, MIX), tok),
                  pl.BlockSpec((GMLP_TM, MIX), tok),
                  pl.BlockSpec((1, MIX), lambda i: (0, 0)),
                  pl.BlockSpec((GM_GROUPS, GM_CHUNK, GM_CHUNK), lambda i: (0, 0, 0)),
                  pl.BlockSpec((GM_GROUPS, GM_CHUNK, GM_GW), lambda i: (0, 0, 0))],
        out_specs=pl.BlockSpec((GMLP_TM, MIX), tok),
        out_shape=jax.ShapeDtypeStruct((TOKENS, MIX), BF16),
        compiler_params=_cparams(("arbitrary",)),
        name="gmlp",
    )(u, v, g, ws, bs)


LRU_TAIL = 8


def _lru_kernel(gb_ref, rb_ref, cw_ref, cb_ref, wa_ref, ba_ref, wx_ref, bx_ref, lam_ref, o_ref,
                tail_ref, h_ref):
    @pl.when(pl.program_id(1) == 0)
    def _():
        tail_ref[...] = jnp.zeros_like(tail_ref)
        h_ref[...] = jnp.zeros_like(h_ref)

    ts = LRU_TS
    x = rb_ref[...].astype(F32)
    ext = jnp.concatenate([tail_ref[...], x], axis=0)
    tail_ref[...] = x[ts - LRU_TAIL:, :]
    xc = cb_ref[...] + x * cw_ref[CONV_WIDTH - 1:CONV_WIDTH, :]
    for d in range(1, CONV_WIDTH):
        xs = pltpu.roll(ext, d, axis=0)[LRU_TAIL:, :]
        xc = xc + xs * cw_ref[CONV_WIDTH - 1 - d:CONV_WIDTH - d, :]
    xcb = xc.astype(BF16)
    r = jax.nn.sigmoid(jnp.dot(xcb, wa_ref[...], preferred_element_type=F32) + ba_ref[...])
    ig = jax.nn.sigmoid(jnp.dot(xcb, wx_ref[...], preferred_element_type=F32) + bx_ref[...])
    z = -lam_ref[...]
    softplus = jnp.maximum(z, 0.0) + jnp.log1p(jnp.exp(-jnp.abs(z)))
    log_a = -LRU_C * r * softplus
    a = jnp.exp(log_a)
    b = jnp.sqrt(1.0 - jnp.exp(2.0 * log_a)) * (ig * xc)
    rows = lax.broadcasted_iota(jnp.int32, (ts, 1), 0)
    d = 1
    while d < ts:
        valid = rows >= d
        a_prev = pltpu.roll(a, d, axis=0)
        b_prev = pltpu.roll(b, d, axis=0)
        b = jnp.where(valid, a * b_prev, 0.0) + b
        a = jnp.where(valid, a * a_prev, a)
        d *= 2
    h = a * h_ref[...] + b
    h_ref[...] = h[ts - 1:ts, :]
    o_ref[...] = (jax.nn.gelu(gb_ref[...].astype(F32)) * h).astype(o_ref.dtype)


def _lru(gb, rb, cw, cb, wa, ba, wx, bx, lam):
    nt = SEQ // LRU_TS
    tok = lambda b, s: (b * nt + s, 0)
    vec = pl.BlockSpec((1, MIX), lambda b, s: (0, 0))
    mat = pl.BlockSpec((MIX, MIX), lambda b, s: (0, 0))
    return pl.pallas_call(
        _lru_kernel,
        grid=(BATCH, nt),
        in_specs=[pl.BlockSpec((LRU_TS, MIX), tok), pl.BlockSpec((LRU_TS, MIX), tok),
                  pl.BlockSpec((CONV_WIDTH, MIX), lambda b, s: (0, 0)), vec,
                  mat, vec, mat, vec, vec],
        out_specs=pl.BlockSpec((LRU_TS, MIX), tok),
        out_shape=jax.ShapeDtypeStruct((TOKENS, MIX), BF16),
        scratch_shapes=[pltpu.VMEM((LRU_TAIL, MIX), F32), pltpu.VMEM((1, MIX), F32)],
        compiler_params=_cparams(("arbitrary", "arbitrary")),
        name="rglru",
    )(gb, rb, cw, cb, wa, ba, wx, bx, lam)


POOL_TAIL = 16


def _pool_kernel(x_ref, w_ref, sc_ref, o_ref, tail_ref):
    s_id = pl.program_id(1)

    @pl.when(s_id == 0)
    def _():
        tail_ref[...] = jnp.zeros_like(tail_ref)

    ts = POOL_TS
    x = x_ref[...].astype(F32)
    ext = jnp.concatenate([tail_ref[...], x], axis=0)
    tail_ref[...] = x[ts - POOL_TAIL:, :]
    pos = s_id * ts + lax.broadcasted_iota(jnp.int32, (ts, 1), 0)
    acc = ext
    width = 1
    for gi, wdw in enumerate(POOL_WINDOWS):
        while width < wdw:
            acc = acc + pltpu.roll(acc, width, axis=0)
            width *= 2
        cs = slice(gi * POOL_GW, (gi + 1) * POOL_GW)
        cnt = jnp.minimum(pos + 1, wdw).astype(F32)
        pooled = acc[POOL_TAIL:, cs] / cnt - x[:, cs]
        mixed = jnp.dot(pooled.astype(BF16), w_ref[gi], preferred_element_type=F32)
        o_ref[:, cs] = (mixed * sc_ref[:, cs]).astype(o_ref.dtype)


def _pool(xd, w, sc):
    nt = SEQ // POOL_TS
    tok = lambda b, s: (b * nt + s, 0)
    return pl.pallas_call(
        _pool_kernel,
        grid=(BATCH, nt),
        in_specs=[pl.BlockSpec((POOL_TS, MIX), tok),
                  pl.BlockSpec((len(POOL_WINDOWS), POOL_GW, POOL_GW), lambda b, s: (0, 0, 0)),
                  pl.BlockSpec((1, MIX), lambda b, s: (0, 0))],
        out_specs=pl.BlockSpec((POOL_TS, MIX), tok),
        out_shape=jax.ShapeDtypeStruct((TOKENS, MIX), BF16),
        scratch_shapes=[pltpu.VMEM((POOL_TAIL, MIX), F32)],
        compiler_params=_cparams(("arbitrary", "arbitrary")),
        name="pool",
    )(xd, w, sc)


def _kvprep_kernel(kc_in, vc_in, vs_in, vw_in, pe_ref, w1_ref, w2_ref, kc_ref, vct_ref, vst_ref,
                   vwt_ref, xf_ref):
    row = lax.broadcasted_iota(jnp.int32, (N_CHUNK, 1), 0)
    for c, x_in in enumerate((kc_in, vc_in)):
        xf_ref[...] = x_in[...].astype(F32)
        first = jnp.zeros((N_CHUNK, NSA_KV), F32)
        second = jnp.zeros((N_CHUNK, NSA_KV), F32)
        for l in range(CMP_STRIDE):
            xl = xf_ref[pl.ds(l, N_CHUNK, stride=CMP_STRIDE), :]
            first += jnp.dot((xl + pe_ref[c, l:l + 1, :]).astype(BF16), w1_ref[c, l],
                             preferred_element_type=F32)
            l2 = l + CMP_STRIDE
            second += jnp.dot((xl + pe_ref[c, l2:l2 + 1, :]).astype(BF16), w1_ref[c, l2],
                              preferred_element_type=F32)
        hid = jax.nn.gelu(first + pltpu.roll(second, N_CHUNK - 1, axis=0))
        out = jnp.dot(hid.astype(BF16), w2_ref[c], preferred_element_type=F32)
        out = jnp.where(row < N_CMP, out, 0.0)
        if c == 0:
            kc_ref[0] = out.astype(kc_ref.dtype)
        else:
            vct_ref[0] = out.T.astype(vct_ref.dtype)
    vst_ref[0] = vs_in[...].astype(F32).T.astype(vst_ref.dtype)
    vwt_ref[0] = vw_in[...].astype(F32).T.astype(vwt_ref.dtype)


def _kvprep(kv, pe, w1, w2):
    col = lambda j: pl.BlockSpec((SEQ, NSA_KV), lambda b: (b, j))
    whole = lambda a: pl.BlockSpec(a.shape, lambda b: (0,) * a.ndim)
    return pl.pallas_call(
        _kvprep_kernel,
        grid=(BATCH,),
        in_specs=[col(0), col(1), col(3), col(5), whole(pe), whole(w1), whole(w2)],
        out_specs=[pl.BlockSpec((1, N_CHUNK, NSA_KV), lambda b: (b, 0, 0)),
                   pl.BlockSpec((1, NSA_KV, N_CHUNK), lambda b: (b, 0, 0)),
                   pl.BlockSpec((1, NSA_KV, SEQ), lambda b: (b, 0, 0)),
                   pl.BlockSpec((1, NSA_KV, SEQ), lambda b: (b, 0, 0))],
        out_shape=[jax.ShapeDtypeStruct((BATCH, N_CHUNK, NSA_KV), BF16),
                   jax.ShapeDtypeStruct((BATCH, NSA_KV, N_CHUNK), BF16),
                   jax.ShapeDtypeStruct((BATCH, NSA_KV, SEQ), BF16),
                   jax.ShapeDtypeStruct((BATCH, NSA_KV, SEQ), BF16)],
        scratch_shapes=[pltpu.VMEM((SEQ, NSA_KV), F32)],
        compiler_params=_cparams(("arbitrary",)),
        name="kvprep",
    )(kv, kv, kv, kv, pe, w1, w2)


def _group_diag(w):
    eye = jnp.eye(N_KV, dtype=w.dtype)
    out = jnp.einsum('...gde,gh->...gdhe', w, eye)
    return out.reshape(w.shape[:-3] + (N_KV * w.shape[-2], N_KV * w.shape[-1]))


def _lane_tile(x, n):
    return jnp.concatenate([x] * n, axis=1)


ONES_ROWS = 16


def _with_ones(v_t):
    return jnp.concatenate([v_t, jnp.ones((ONES_ROWS, v_t.shape[1]), BF16)], axis=0)


def _nsa_kernel(q_ref, kc_ref, vct_ref, ks_ref, vst_ref, kw_ref, vwt_ref, gt_ref, ovt_ref, blk_ref,
                y_ref, sa_ref, sb_ref, oslc_ref):
    i = pl.program_id(1)
    tq = NSA_TQ
    nl = HPG * tq
    t_row = i * tq + lax.broadcasted_iota(jnp.int32, (1, tq), 1)
    groups = range(N_KV)
    vrows = [slice(g * HEAD_DIM, (g + 1) * HEAD_DIM) for g in groups]

    def scores(k, q_t):
        return jnp.dot(k, q_t, preferred_element_type=F32)

    def normalized(acc):
        return acc[:HEAD_DIM, :] / acc[HEAD_DIM:HEAD_DIM + 1, :]

    lane = lax.broadcasted_iota(jnp.int32, (tq, LANES), 1)
    scale = HEAD_DIM ** -0.5
    q4 = [jnp.concatenate(
        [(jnp.where((lane >= HEAD_DIM) == (g == 1),
                    q_ref[:, j * LANES:(j + 1) * LANES].astype(F32), 0.0) * scale).T
         for j in range(HPG)], axis=1).astype(BF16) for g in groups]

    n_idx = lax.broadcasted_iota(jnp.int32, (N_CHUNK, 1), 0)
    ok = n_idx * CMP_STRIDE + (CMP_LEN - 1) <= _lane_tile(t_row, HPG)
    blk = lax.broadcasted_iota(jnp.int32, (N_SLC, 1), 0)
    cur = jnp.right_shift(t_row, 6)
    forced = (blk == 0) | (blk == cur) | (blk == cur - 1)
    causal_blk = blk * SLC_LEN <= t_row
    o_cmp, q_aug = [], []
    for g in groups:
        s = jnp.where(ok, scores(kc_ref[0], q4[g]), NEG_INF)
        m = jnp.max(s, axis=0, keepdims=True)
        e = jnp.where(ok, jnp.exp(s - m), 0.0)
        l = jnp.sum(e, axis=0, keepdims=True)
        p = e / jnp.where(l > 0.0, l, 1.0)
        o_cmp.append(jnp.dot(vct_ref[0, vrows[g], :], p.astype(BF16), preferred_element_type=F32))
        psum = p[:, 0:tq]
        for j in range(1, HPG):
            psum = psum + p[:, j * tq:(j + 1) * tq]
        p_hi = psum.astype(BF16)
        p_lo = (psum - p_hi.astype(F32)).astype(BF16)
        imp = (jnp.dot(ovt_ref[...], p_hi, preferred_element_type=F32)
               + jnp.dot(ovt_ref[...], p_lo, preferred_element_type=F32))
        work = jnp.where(forced, -3e38, jnp.where(causal_blk, imp, -1.0))
        sel = jnp.where(forced, 1.0, 0.0)
        for _ in range(SLC_TOPN - MAX_FORCED):
            mx = jnp.max(work, axis=0, keepdims=True)
            idx = jnp.min(jnp.where(work == mx, blk, N_SLC), axis=0, keepdims=True)
            pick = blk == idx
            sel = jnp.where(pick & (mx >= 0.0), 1.0, sel)
            work = jnp.where(pick, -3e38, work)
        sel_bias = jnp.where(sel > 0.0, 0.0, NEG_INF)
        sel_bias = jnp.concatenate([sel_bias, jnp.zeros((LANES - N_SLC, tq), F32)], axis=0)
        sel_bias = _lane_tile(sel_bias.astype(BF16), HPG)
        q_aug.append(jnp.concatenate([q4[g], sel_bias], axis=0))

    start = pl.multiple_of(jnp.maximum(i - WIN // tq, 0) * tq, tq)
    k_win = kw_ref[pl.ds(start, WIN_KEYS), :]
    delta = t_row - (start + lax.broadcasted_iota(jnp.int32, (WIN_KEYS, 1), 0))
    win_bias = _lane_tile(jnp.where((delta >= 0) & (delta < WIN), 0.0, NEG_INF), HPG)
    o_win = []
    for g in groups:
        sT = scores(k_win, q4[g]) + win_bias
        pT = jnp.exp((sT - jnp.max(sT, axis=0, keepdims=True)).astype(BF16))
        vT = _with_ones(vwt_ref[vrows[g], pl.ds(start, WIN_KEYS)])
        o_win.append(normalized(jnp.dot(vT, pT, preferred_element_type=F32)))

    def slc_scores(kt, dst_ref):
        k0 = pl.multiple_of(kt * SLC_KT, SLC_KT)
        k_aug = jnp.concatenate([ks_ref[pl.ds(k0, SLC_KT), :], blk_ref[pl.ds(k0, SLC_KT), :]],
                                axis=1)
        maxima = []
        for g in groups:
            s_t = scores(k_aug, q_aug[g])
            dst_ref[g] = s_t
            maxima.append(jnp.max(s_t, axis=0, keepdims=True))
        return maxima

    def slc_update(kt, state, s_t, tile_max):
        k0 = pl.multiple_of(kt * SLC_KT, SLC_KT)
        out = []
        for g in groups:
            m_i, acc = state[2 * g], state[2 * g + 1]
            m_new = jnp.maximum(m_i, tile_max[g])
            alpha = jnp.exp(m_i - m_new)
            pT = jnp.exp((s_t[g] - m_new).astype(BF16))
            vT = _with_ones(vst_ref[vrows[g], pl.ds(k0, SLC_KT)])
            out += [m_new, alpha * acc + jnp.dot(vT, pT, preferred_element_type=F32)]
        return out

    def from_ref(src_ref):
        return [src_ref[g] for g in groups]

    def slc_pair(p, carry):
        state, max_a = list(carry[:2 * N_KV]), list(carry[2 * N_KV:])
        max_b = slc_scores(2 * p + 1, sb_ref)
        state = slc_update(2 * p, state, from_ref(sa_ref), max_a)
        max_a = slc_scores(2 * p + 2, sa_ref)
        state = slc_update(2 * p + 1, state, from_ref(sb_ref), max_b)
        return tuple(state + max_a)

    def slc_finish(state, src_ref):
        kpos = last_kt * SLC_KT + lax.broadcasted_iota(jnp.int32, (SLC_KT, 1), 0)
        causal_bias = _lane_tile(jnp.where(kpos <= t_row, 0.0, NEG_INF), HPG)
        s_last = [s + causal_bias for s in from_ref(src_ref)]
        state = slc_update(last_kt, state, s_last,
                           [jnp.max(s, axis=0, keepdims=True) for s in s_last])
        for g in groups:
            oslc_ref[g] = normalized(state[2 * g + 1])

    last_kt = (i * tq) // SLC_KT
    init = ((jnp.full((1, nl), -3e38, F32), jnp.zeros((HEAD_DIM + ONES_ROWS, nl), F32)) * N_KV
            + tuple(slc_scores(0, sa_ref)))
    carry = lax.fori_loop(0, last_kt // 2, slc_pair, init)
    state, max_a = list(carry[:2 * N_KV]), list(carry[2 * N_KV:])
    odd = lax.rem(last_kt, 2) == 1

    @pl.when(odd)
    def _():
        slc_scores(last_kt, sb_ref)
        slc_finish(slc_update(last_kt - 1, state, from_ref(sa_ref), max_a), sb_ref)

    @pl.when(jnp.logical_not(odd))
    def _():
        slc_finish(state, sa_ref)

    o_slc = [oslc_ref[g] for g in groups]

    for g in groups:
        gates = jax.nn.sigmoid(gt_ref[0, g])

        def gate_row(br):
            return jnp.concatenate([gates[3 * j + br:3 * j + br + 1, :] for j in range(HPG)],
                                   axis=1)

        o = gate_row(0) * o_cmp[g] + gate_row(1) * o_slc[g] + gate_row(2) * o_win[g]
        for j in range(HPG):
            h0 = (g * HPG + j) * HEAD_DIM
            y_ref[0, h0:h0 + HEAD_DIM, :] = o[:, j * tq:(j + 1) * tq].astype(y_ref.dtype)


def _block_onehot():
    oh = (np.arange(SEQ)[:, None] // SLC_LEN) == np.arange(LANES)[None, :]
    return jnp.asarray(oh.astype(np.float32), dtype=BF16)


def _nsa(q, kc, vct, kv, vst, vwt, gt, ovt):
    nq = SEQ // NSA_TQ
    return pl.pallas_call(
        _nsa_kernel,
        grid=(BATCH, nq),
        in_specs=[pl.BlockSpec((NSA_TQ, NSA_Q), lambda b, i: (b * nq + i, 0)),
                  pl.BlockSpec((1, N_CHUNK, NSA_KV), lambda b, i: (b, 0, 0)),
                  pl.BlockSpec((1, NSA_KV, N_CHUNK), lambda b, i: (b, 0, 0)),
                  pl.BlockSpec((SEQ, NSA_KV), lambda b, i: (b, 2)),
                  pl.BlockSpec((None, NSA_KV, SEQ), lambda b, i: (b, 0, 0)),
                  pl.BlockSpec((SEQ, NSA_KV), lambda b, i: (b, 4)),
                  pl.BlockSpec((None, NSA_KV, SEQ), lambda b, i: (b, 0, 0)),
                  pl.BlockSpec((1, N_KV, GATE_ROWS, NSA_TQ), lambda b, i: (b, 0, 0, i)),
                  pl.BlockSpec((N_SLC, N_CHUNK), lambda b, i: (0, 0)),
                  pl.BlockSpec((SEQ, LANES), lambda b, i: (0, 0))],
        out_specs=pl.BlockSpec((1, NSA_Q, NSA_TQ), lambda b, i: (b, 0, i)),
        out_shape=jax.ShapeDtypeStruct((BATCH, NSA_Q, SEQ), BF16),
        scratch_shapes=[pltpu.VMEM((N_KV, SLC_KT, HPG * NSA_TQ), F32),
                        pltpu.VMEM((N_KV, SLC_KT, HPG * NSA_TQ), F32),
                        pltpu.VMEM((N_KV, HEAD_DIM, HPG * NSA_TQ), F32)],
        compiler_params=_cparams(("arbitrary", "arbitrary")),
        name="nsa",
    )(q, kc, vct, kv, vst, kv, vwt, gt, ovt, _block_onehot())


def _overlap_t():
    c_start = np.arange(N_CHUNK) * CMP_STRIDE
    s_start = np.arange(N_SLC) * SLC_LEN
    ov = ((c_start[None, :] <= s_start[:, None] + SLC_LEN - 1)
          & (c_start[None, :] + CMP_LEN - 1 >= s_start[:, None])
          & (np.arange(N_CHUNK)[None, :] < N_CMP))
    return jnp.asarray(ov.astype(np.float32), dtype=BF16)


def _merge_kernel(x_ref, g_ref, wmg_ref, ya_ref, yb_ref, yct_ref, yd_ref, wb_ref, wo_ref, o_ref):
    x = x_ref[...]
    nb = _rms(x, g_ref[...]).astype(BF16)
    yc = yct_ref[0].astype(F32).T.astype(BF16)
    ys = (ya_ref[...], yb_ref[...], yc, yd_ref[...])
    merged = jnp.zeros((MERGE_TM, D_MODEL), F32)
    for bi, y in enumerate(ys):
        gate = jax.nn.sigmoid(jnp.dot(nb, wmg_ref[:, bi * D_MODEL:(bi + 1) * D_MODEL],
                                      preferred_element_type=F32))
        merged = merged + gate * jnp.dot(y, wb_ref[bi], preferred_element_type=F32)
    o_ref[...] = x + jnp.dot(merged.astype(BF16), wo_ref[...], preferred_element_type=F32)


def _merge(x, g, wmg, ya, yb, yct, yd, wb, wo, layer):
    nt = SEQ // MERGE_TM
    tok = lambda i: (i, 0)
    const2 = lambda i: (0, 0)
    return pl.pallas_call(
        _merge_kernel,
        grid=(TOKENS // MERGE_TM,),
        in_specs=[pl.BlockSpec((MERGE_TM, D_MODEL), tok),
                  pl.BlockSpec((1, D_MODEL), const2),
                  pl.BlockSpec((None, D_MODEL, N_BRANCH * D_MODEL), lambda i: (layer, 0, 0)),
                  pl.BlockSpec((MERGE_TM, MIX), tok),
                  pl.BlockSpec((MERGE_TM, MIX), tok),
                  pl.BlockSpec((1, MIX, MERGE_TM), lambda i: (i // nt, 0, i % nt)),
                  pl.BlockSpec((MERGE_TM, MIX), tok),
                  pl.BlockSpec((None, N_BRANCH, MIX, D_MODEL), lambda i: (layer, 0, 0, 0)),
                  pl.BlockSpec((None, D_MODEL, D_MODEL), lambda i: (layer, 0, 0))],
        out_specs=pl.BlockSpec((MERGE_TM, D_MODEL), tok),
        out_shape=jax.ShapeDtypeStruct((TOKENS, D_MODEL), F32),
        compiler_params=_cparams(("arbitrary",)),
        name="merge",
    )(x, g, wmg, ya, yb, yct, yd, wb, wo)


def _router_logits(t, wr_ref, br_ref):
    w = wr_ref[...]
    t_hi, w_hi = t.astype(BF16), w.astype(BF16)
    t_lo = (t - t_hi.astype(F32)).astype(BF16)
    w_lo = (w - w_hi.astype(F32)).astype(BF16)
    dot = functools.partial(jnp.dot, preferred_element_type=F32)
    return dot(t_hi, w_hi) + dot(t_hi, w_lo) + dot(t_lo, w_hi) + br_ref[...]


def _top_group(logits):
    lane = lax.broadcasted_iota(jnp.int32, logits.shape, 1)
    is_grp = lane < N_GROUPS
    lg = jnp.where(is_grp, logits, NEG_INF)
    gmax = jnp.max(lg, axis=1, keepdims=True)
    grp_idx = jnp.min(jnp.where(is_grp & (lg == gmax), lane, ROUTER_PAD), axis=1, keepdims=True)
    return lg, gmax, grp_idx


def _group_weight(logits, grp_idx):
    lane = lax.broadcasted_iota(jnp.int32, logits.shape, 1)
    lg, gmax, _ = _top_group(logits)
    ge = jnp.where(lane < N_GROUPS, jnp.exp(lg - gmax), 0.0)
    return (jnp.sum(jnp.where(lane == grp_idx, ge, 0.0), axis=1, keepdims=True)
            / jnp.sum(ge, axis=1, keepdims=True))


def _route_kernel(x_ref, g_ref, wr_ref, br_ref, tri_ref, meta_ref, cnt_ref, run_ref):
    @pl.when(pl.program_id(0) == 0)
    def _():
        run_ref[...] = jnp.zeros_like(run_ref)

    logits = _router_logits(_rms(x_ref[...], g_ref[...]), wr_ref, br_ref)
    _, _, grp_idx = _top_group(logits)
    lane = lax.broadcasted_iota(jnp.int32, logits.shape, 1)
    first = N_GROUPS + grp_idx * EXPERTS_PER_GROUP
    in_grp = (lane >= first) & (lane < first + EXPERTS_PER_GROUP)
    le = jnp.where(in_grp, logits, NEG_INF)
    l1 = jnp.max(le, axis=1, keepdims=True)
    i1 = jnp.min(jnp.where(in_grp & (le == l1), lane, ROUTER_PAD), axis=1, keepdims=True)
    le2 = jnp.where(lane == i1, NEG_INF, le)
    l2 = jnp.max(le2, axis=1, keepdims=True)
    i2 = jnp.min(jnp.where(in_grp & (lane != i1) & (le2 == l2), lane, ROUTER_PAD), axis=1,
                 keepdims=True)
    lo = jnp.minimum(i1, i2) - first
    hi = jnp.maximum(i1, i2) - first
    pair = jnp.right_shift(lo * (2 * EXPERTS_PER_GROUP - 1 - lo), 1) + hi - lo - 1
    cls = grp_idx * N_PAIRS + pair
    onehot = jnp.where(lane == cls, 1.0, 0.0)
    before = jnp.dot(tri_ref[...], onehot.astype(BF16), preferred_element_type=F32) + run_ref[...]
    rank = jnp.sum(onehot * before, axis=1, keepdims=True).astype(jnp.int32)
    run_ref[...] += jnp.sum(onehot, axis=0, keepdims=True)
    meta_ref[...] = jnp.where(lane == 0, cls, jnp.where(lane == 1, rank, 0))
    cnt_ref[...] = run_ref[...]


def _route(x, g, wr, br):
    tok = lambda i: (i, 0)
    const2 = lambda i: (0, 0)
    tri = jnp.asarray(np.tril(np.ones((ROUTE_TM, ROUTE_TM), np.float32), -1), dtype=BF16)
    return pl.pallas_call(
        _route_kernel,
        grid=(TOKENS // ROUTE_TM,),
        in_specs=[pl.BlockSpec((ROUTE_TM, D_MODEL), tok),
                  pl.BlockSpec((1, D_MODEL), const2),
                  pl.BlockSpec((D_MODEL, ROUTER_PAD), const2),
                  pl.BlockSpec((1, ROUTER_PAD), const2),
                  pl.BlockSpec((ROUTE_TM, ROUTE_TM), const2)],
        out_specs=[pl.BlockSpec((ROUTE_TM, ROUTER_PAD), tok), pl.BlockSpec((1, ROUTER_PAD), const2)],
        out_shape=[jax.ShapeDtypeStruct((TOKENS, ROUTER_PAD), jnp.int32),
                   jax.ShapeDtypeStruct((1, ROUTER_PAD), F32)],
        scratch_shapes=[pltpu.VMEM((1, ROUTER_PAD), F32)],
        compiler_params=_cparams(("arbitrary",)),
        name="route",
    )(x, g, wr, br, tri)


MOVE_SLOTS = 3


def _move_rows_kernel(idx_ref, seg_ref, len_ref, src_hbm, dst_hbm, buf, zbuf, in_sem, out_sem,
                      pad_sem, *, scatter):
    c = pl.program_id(0)
    n = pl.num_programs(0)
    slot = lax.rem(c, MOVE_SLOTS)
    nxt = lax.rem(c + 1, MOVE_SLOTS)
    rc = ROW_CHUNK
    chunk = lambda k: pl.ds(k * rc, rc)

    if scatter:
        def read(k, sl):
            return [pltpu.make_async_copy(src_hbm.at[chunk(k)], buf.at[sl], in_sem.at[sl])]

        def write(k, sl):
            return [pltpu.make_async_copy(buf.at[sl, pl.ds(r, 1)],
                                          dst_hbm.at[pl.ds(idx_ref[k * rc + r], 1)], out_sem.at[sl])
                    for r in range(rc)]

        def write_done(sl):
            return pltpu.make_async_copy(buf.at[sl], dst_hbm.at[chunk(0)], out_sem.at[sl])

        def read_done(sl):
            return read(0, sl)[0]
    else:
        def read(k, sl):
            return [pltpu.make_async_copy(src_hbm.at[pl.ds(idx_ref[k * rc + r], 1)],
                                          buf.at[sl, pl.ds(r, 1)], in_sem.at[sl])
                    for r in range(rc)]

        def write(k, sl):
            return [pltpu.make_async_copy(buf.at[sl], dst_hbm.at[chunk(k)], out_sem.at[sl])]

        def write_done(sl):
            return write(0, sl)[0]

        def read_done(sl):
            return pltpu.make_async_copy(src_hbm.at[chunk(0)], buf.at[sl], in_sem.at[sl])

    def pad_fills(fn):
        for sg in range(N_CLASSES):
            start, length = seg_ref[sg], len_ref[sg]
            head = jnp.bitwise_and(-start, SUBLANES - 1)
            for r in range(SUBLANES - 1):
                @pl.when(r < head)
                def _():
                    fn(pltpu.make_async_copy(zbuf.at[pl.ds(0, 1)], dst_hbm.at[pl.ds(start + r, 1)],
                                             pad_sem))
            body = length - head
            bit = rc // 2
            while bit >= SUBLANES:
                done = body - jnp.bitwise_and(body, 2 * bit - 1)
                first = pl.multiple_of(start + head + done, SUBLANES)

                @pl.when(jnp.bitwise_and(body, bit) != 0)
                def _():
                    fn(pltpu.make_async_copy(zbuf.at[pl.ds(0, bit)], dst_hbm.at[pl.ds(first, bit)],
                                             pad_sem))
                bit //= 2
        tail = pl.multiple_of(seg_ref[N_CLASSES], rc)

        def tail_tile(k, carry):
            fn(pltpu.make_async_copy(zbuf, dst_hbm.at[pl.ds(tail + k * rc, rc)], pad_sem))
            return carry
        lax.fori_loop(0, len_ref[N_CLASSES] // rc, tail_tile, 0)

    @pl.when(c == 0)
    def _():
        for cp in read(0, 0):
            cp.start()
        if scatter:
            zbuf[...] = jnp.zeros_like(zbuf)
            pad_fills(lambda cp: cp.start())

    @pl.when(c + 1 < n)
    def _():
        @pl.when(c >= 2)
        def _():
            write_done(nxt).wait()
        for cp in read(c + 1, nxt):
            cp.start()

    read_done(slot).wait()
    for cp in write(c, slot):
        cp.start()

    @pl.when(c == n - 1)
    def _():
        write_done(nxt).wait()
        write_done(lax.rem(c + 2, MOVE_SLOTS)).wait()
        write_done(slot).wait()
        if scatter:
            pad_fills(lambda cp: cp.wait())


def _move_rows(idx, seg_start, seg_len, src, n_out, scatter):
    any_spec = pl.BlockSpec(memory_space=pl.ANY)
    grid_spec = pltpu.PrefetchScalarGridSpec(
        num_scalar_prefetch=3, grid=(TOKENS // ROW_CHUNK,), in_specs=[any_spec], out_specs=any_spec,
        scratch_shapes=[pltpu.VMEM((MOVE_SLOTS, ROW_CHUNK, D_MODEL), F32),
                        pltpu.VMEM((ROW_CHUNK, D_MODEL), F32),
                        pltpu.SemaphoreType.DMA((MOVE_SLOTS,)),
                        pltpu.SemaphoreType.DMA((MOVE_SLOTS,)),
                        pltpu.SemaphoreType.DMA(())])
    return pl.pallas_call(
        functools.partial(_move_rows_kernel, scatter=scatter),
        grid_spec=grid_spec,
        out_shape=jax.ShapeDtypeStruct((n_out, D_MODEL), F32),
        compiler_params=_cparams(("arbitrary",)),
        name="dispatch" if scatter else "collect",
    )(idx, seg_start, seg_len, src)


def _experts_kernel(ea_ref, eb_ref, tv_ref, tb_ref, x_ref, g_ref, wr_ref, br_ref, wga_ref, wua_ref,
                    wda_ref, wgb_ref, wub_ref, wdb_ref, fg_ref, o_ref, *, layer, final_norm):
    j = pl.program_id(0)
    nv = tv_ref[j]

    @pl.when(nv > 0)
    def _():
        x = x_ref[...]
        t = _rms(x, g_ref[...])
        logits = _router_logits(t, wr_ref, br_ref)
        lane = lax.broadcasted_iota(jnp.int32, logits.shape, 1)
        ea = ea_ref[j] - layer * N_EXPERTS
        eb = eb_ref[j] - layer * N_EXPERTS
        la = jnp.sum(jnp.where(lane == N_GROUPS + ea, logits, 0.0), axis=1, keepdims=True)
        lb = jnp.sum(jnp.where(lane == N_GROUPS + eb, logits, 0.0), axis=1, keepdims=True)
        top = jnp.maximum(la, lb)
        pa, pb = jnp.exp(la - top), jnp.exp(lb - top)
        scale = _group_weight(logits, jnp.right_shift(ea, 2)) / (pa + pb)
        tb = t.astype(BF16)
        acc = jnp.zeros((MOE_TM, D_MODEL), F32)
        for wg, wu, wd, w in ((wga_ref, wua_ref, wda_ref, pa * scale),
                              (wgb_ref, wub_ref, wdb_ref, pb * scale)):
            hid = (jax.nn.silu(jnp.dot(tb, wg[0], preferred_element_type=F32))
                   * jnp.dot(tb, wu[0], preferred_element_type=F32)) * w
            acc = acc + jnp.dot(hid.astype(BF16), wd[0], preferred_element_type=F32)
        h = x + acc
        if final_norm:
            h = _rms(h, fg_ref[...])
        o_ref[...] = h

    @pl.when(nv == 0)
    def _():
        o_ref[...] = jnp.zeros_like(o_ref)


def _experts(tile_ea, tile_eb, tile_valid, tile_block, hs, g, wr, br, wg, wu, wd, fg, layer,
             final_norm):
    const2 = lambda j, *_: (0, 0)
    of_a = lambda j, ea, eb, tv, tb: (ea[j], 0, 0)
    of_b = lambda j, ea, eb, tv, tb: (eb[j], 0, 0)
    up = lambda idx: pl.BlockSpec((1, D_MODEL, D_EXPERT), idx)
    down = lambda idx: pl.BlockSpec((1, D_EXPERT, D_MODEL), idx)
    grid_spec = pltpu.PrefetchScalarGridSpec(
        num_scalar_prefetch=4,
        grid=(MOE_NT,),
        in_specs=[pl.BlockSpec((MOE_TM, D_MODEL), lambda j, ea, eb, tv, tb: (tb[j], 0)),
                  pl.BlockSpec((1, D_MODEL), const2),
                  pl.BlockSpec((D_MODEL, ROUTER_PAD), const2),
                  pl.BlockSpec((1, ROUTER_PAD), const2),
                  up(of_a), up(of_a), down(of_a), up(of_b), up(of_b), down(of_b),
                  pl.BlockSpec((1, D_MODEL), const2)],
        out_specs=pl.BlockSpec((MOE_TM, D_MODEL), lambda j, *_: (j, 0)))
    return pl.pallas_call(
        functools.partial(_experts_kernel, layer=layer, final_norm=final_norm),
        grid_spec=grid_spec,
        out_shape=jax.ShapeDtypeStruct((MOE_ROWS, D_MODEL), F32),
        compiler_params=_cparams(("arbitrary",)),
        name="experts",
    )(tile_ea, tile_eb, tile_valid, tile_block, hs, g, wr, br, wg, wu, wd, wg, wu, wd, fg)


_PAIR_LO = np.array([a for a in range(EXPERTS_PER_GROUP) for b in range(a + 1, EXPERTS_PER_GROUP)])
_PAIR_HI = np.array([b for a in range(EXPERTS_PER_GROUP) for b in range(a + 1, EXPERTS_PER_GROUP)])


def _moe(x, g, wr, br, wg, wu, wd, fg, layer, final_norm):
    meta, cnt = _route(x, g, wr, br)
    cls, rank = meta[:, 0], meta[:, 1]
    counts = cnt[0, :N_CLASSES].astype(jnp.int32)
    padded = (counts + MOE_TM - 1) // MOE_TM * MOE_TM
    ends = jnp.cumsum(padded)
    starts = ends - padded
    pos = starts[cls] + rank
    tile_start = jnp.arange(MOE_NT, dtype=jnp.int32) * MOE_TM
    tile_cls = jnp.minimum(jnp.sum(tile_start[:, None] >= ends[None, :], axis=1), N_CLASSES - 1)
    tile_valid = jnp.clip(starts[tile_cls] + counts[tile_cls] - tile_start, 0, MOE_TM)
    tile_valid = tile_valid.astype(jnp.int32)
    tile_block = jnp.where(tile_valid > 0, jnp.arange(MOE_NT, dtype=jnp.int32), 0)
    first = layer * N_EXPERTS + (tile_cls // N_PAIRS) * EXPERTS_PER_GROUP
    tile_ea = (first + jnp.asarray(_PAIR_LO)[tile_cls % N_PAIRS]).astype(jnp.int32)
    tile_eb = (first + jnp.asarray(_PAIR_HI)[tile_cls % N_PAIRS]).astype(jnp.int32)
    seg_start = jnp.concatenate([starts + counts, ends[-1:]])
    seg_len = jnp.concatenate([padded - counts, MOE_ROWS - ends[-1:]])
    hs = _move_rows(pos, seg_start, seg_len, x, MOE_ROWS, scatter=True)
    ys = _experts(tile_ea, tile_eb, tile_valid, tile_block, hs, g, wr, br, wg, wu, wd, fg, layer,
                  final_norm)
    return _move_rows(pos, seg_start, seg_len, ys, TOKENS, scatter=False)


def _block_diag(w):
    eye = jnp.eye(LRU_BLOCKS, dtype=w.dtype)
    return jnp.einsum('hij,hk->hikj', w, eye).reshape(MIX, MIX)


def _split_w_in(w_in):
    cuts = [int(c) for c in np.cumsum((0,) + IN_SPLITS)]
    cols = lambda a, b: w_in[:, :, a:b]
    q_parts = [cols(cuts[4] + (g * HPG + j) * HEAD_DIM, cuts[4] + (g * HPG + j + 1) * HEAD_DIM)
               for j in range(HPG) for g in range(N_KV)]
    gate_pad = jnp.zeros((DEPTH, D_MODEL, GATE_PAD - IN_SPLITS[6]), w_in.dtype)
    w_proj = jnp.concatenate([cols(cuts[0], cuts[4])] + q_parts
                             + [cols(cuts[5], cuts[6]), cols(cuts[7], cuts[8]),
                                cols(cuts[6], cuts[7]), gate_pad], axis=2)
    return w_proj.astype(BF16), cols(cuts[8], cuts[9]).astype(BF16)


def _nsa_mixer(q, kv, ng, p):
    w1 = p['cmp_w1'].reshape(2, N_KV, CMP_LEN, HEAD_DIM, HEAD_DIM).transpose(0, 2, 1, 3, 4)
    pe = jnp.concatenate([p['cmp_pe']] * N_KV, axis=-1)
    kc, vct, vst, vwt = _kvprep(kv, pe, _group_diag(w1).astype(BF16),
                                _group_diag(p['cmp_w2']).astype(BF16))
    gt = ng[:, :3 * N_HEADS].reshape(BATCH, SEQ, N_KV, 3 * HPG).transpose(0, 2, 3, 1)
    gt = jnp.pad(gt, ((0, 0), (0, 0), (0, GATE_ROWS - 3 * HPG), (0, 0)))
    return _nsa(q, kc, vct, kv, vst, vwt, gt, _overlap_t())


def _layer(h, p, big, layer, final_g, final_norm):
    row = lambda a: a.reshape(1, -1)

    u, v, gb, rb, q, kv, xd, ng = _proj(h, row(p['norm1_g']), big['w_proj'], layer)

    bs = jnp.broadcast_to(p['gm_b'][:, :, None], (GM_GROUPS, GM_CHUNK, GM_GW))
    y_a = _gmlp(u, v, row(p['gm_norm_g']), p['gm_ws'], bs)

    y_b = _lru(gb, rb, p['conv_w'], row(p['conv_b']), _block_diag(p['lru_wa']).astype(BF16),
               row(p['lru_ba']), _block_diag(p['lru_wx']).astype(BF16), row(p['lru_bx']),
               row(p['lru_lambda']))

    y_d = _pool(xd, p['pool_w'].astype(BF16), row(p['pool_scale']))

    y_ct = _nsa_mixer(q, kv, ng, p)

    h = _merge(h, row(p['norm1_g']), big['w_mg'], y_a, y_b, y_ct, y_d, big['w_branch'],
               big['w_out'], layer)

    wr = jnp.concatenate([p['router_w_group'], p['router_w_expert']], axis=1)
    wr = jnp.pad(wr, ((0, 0), (0, ROUTER_PAD - wr.shape[1])))
    br = jnp.concatenate([p['router_b_group'], p['router_b_expert']])
    br = jnp.pad(br, (0, ROUTER_PAD - br.shape[0])).reshape(1, ROUTER_PAD)
    return _moe(h, row(p['norm2_g']), wr, br, big['moe_w_gate'], big['moe_w_up'],
                big['moe_w_down'], row(final_g), layer, final_norm)


_LAYER_PARAMS = ('norm1_g', 'gm_norm_g', 'gm_ws', 'gm_b', 'conv_w', 'conv_b', 'lru_wa',
                 'lru_ba', 'lru_wx', 'lru_bx', 'lru_lambda', 'cmp_pe', 'cmp_w1', 'cmp_w2', 'pool_w',
                 'pool_scale', 'norm2_g', 'router_w_group', 'router_b_group',
                 'router_w_expert', 'router_b_expert')


def kernel(x, norm1_g, w_in, gm_norm_g, gm_ws, gm_b, conv_w, conv_b, lru_wa, lru_ba, lru_wx,
           lru_bx, lru_lambda, cmp_pe, cmp_w1, cmp_w2, pool_w, pool_scale, w_branch, w_out,
           norm2_g, router_w_group, router_b_group, router_w_expert, router_b_expert,
           moe_w_gate, moe_w_up, moe_w_down, final_norm_g):
    stacked = dict(zip(_LAYER_PARAMS, (
        norm1_g, gm_norm_g, gm_ws, gm_b, conv_w, conv_b, lru_wa, lru_ba, lru_wx, lru_bx,
        lru_lambda, cmp_pe, cmp_w1, cmp_w2, pool_w, pool_scale, norm2_g,
        router_w_group, router_b_group, router_w_expert, router_b_expert)))
    w_proj, w_mg = _split_w_in(w_in)
    experts = lambda w: w.astype(BF16).reshape((DEPTH * N_EXPERTS,) + w.shape[2:])
    big = dict(w_proj=w_proj, w_mg=w_mg, w_branch=w_branch.astype(BF16), w_out=w_out.astype(BF16),
               moe_w_gate=experts(moe_w_gate), moe_w_up=experts(moe_w_up),
               moe_w_down=experts(moe_w_down))
    h = x.reshape(TOKENS, D_MODEL)
    for layer in range(DEPTH):
        p = {k: a[layer] for k, a in stacked.items()}
        h = _layer(h, p, big, layer, final_norm_g, final_norm=(layer == DEPTH - 1))
    return h.reshape(BATCH, SEQ, D_MODEL)
```

```python
import functools

import numpy as np
import jax
import jax.numpy as jnp
from jax import lax
from jax.experimental import pallas as pl
from jax.experimental.pallas import tpu as pltpu

F32 = jnp.float32
BF16 = jnp.bfloat16

D_MODEL = 1024
BATCH = 4
SEQ = 4096
TOKENS = BATCH * SEQ
DEPTH = 2
MIX = D_MODEL // 2
GM_CHUNK = 128
GM_GROUPS = 4
GM_GW = MIX // GM_GROUPS
CONV_WIDTH = 4
LRU_BLOCKS = 8
LRU_BW = MIX // LRU_BLOCKS
LRU_C = 8.0
N_HEADS = 8
HEAD_DIM = MIX // N_HEADS
N_KV = 2
HPG = N_HEADS // N_KV
CMP_LEN = 32
CMP_STRIDE = 16
SLC_LEN = 64
SLC_TOPN = 8
MAX_FORCED = 3
WIN = 512
NSA_Q = N_HEADS * HEAD_DIM
NSA_KV = N_KV * HEAD_DIM
POOL_WINDOWS = (2, 4, 8, 16)
POOL_GW = MIX // len(POOL_WINDOWS)
N_BRANCH = 4
N_GROUPS = 4
EXPERTS_PER_GROUP = 4
N_EXPERTS = N_GROUPS * EXPERTS_PER_GROUP
D_EXPERT = D_MODEL // 2
EPS = 1e-6
NEG_INF = -1e30
FORCE_SCORE = 1e6
IN_SPLITS = (MIX, MIX, MIX, MIX, NSA_Q, 6 * NSA_KV, 3 * N_HEADS, MIX, N_BRANCH * D_MODEL)

N_CHUNK = SEQ // CMP_STRIDE
N_CMP = N_CHUNK - CMP_LEN // CMP_STRIDE + 1
N_SLC = SEQ // SLC_LEN

LANES = 128
SUBLANES = 8
GATE_PAD = LANES
GATE_ROWS = 16
ROUTER_PAD = LANES
VMEM_LIMIT = 56 * 1024 * 1024

PROJ_WIDTHS = (MIX, MIX, MIX, MIX, NSA_Q, 6 * NSA_KV, MIX, GATE_PAD)
W_IN_HALF = N_BRANCH * D_MODEL
PROJ_TM = 512
GMLP_TM = 512
LRU_TS = 512
POOL_TS = 512
NSA_TQ = 256
SLC_KT = 256
WIN_KEYS = WIN + NSA_TQ
MERGE_TM = 256
ROUTE_TM = 512
MOE_TM = 256
N_PAIRS = EXPERTS_PER_GROUP * (EXPERTS_PER_GROUP - 1) // 2
N_CLASSES = N_GROUPS * N_PAIRS
MOE_ROWS = TOKENS + N_CLASSES * MOE_TM
MOE_NT = MOE_ROWS // MOE_TM
ROW_CHUNK = 256


def _cparams(sem):
    return pltpu.CompilerParams(dimension_semantics=sem, vmem_limit_bytes=VMEM_LIMIT)


def _rms(x, g):
    return x * lax.rsqrt(jnp.mean(x * x, axis=-1, keepdims=True) + EPS) * g


def _proj_kernel(x_ref, g_ref, w_ref, *out_refs):
    nb = _rms(x_ref[...], g_ref[...]).astype(BF16)
    off = 0
    for ref in out_refs:
        w = ref.shape[-1]
        ref[...] = jnp.dot(nb, w_ref[:, off:off + w], preferred_element_type=F32).astype(ref.dtype)
        off += w


def _proj(x, g, w, layer):
    out_shape = [jax.ShapeDtypeStruct((TOKENS, wd), BF16) for wd in PROJ_WIDTHS[:-1]]
    out_shape.append(jax.ShapeDtypeStruct((TOKENS, GATE_PAD), F32))
    return pl.pallas_call(
        _proj_kernel,
        grid=(TOKENS // PROJ_TM,),
        in_specs=[pl.BlockSpec((PROJ_TM, D_MODEL), lambda i: (i, 0)),
                  pl.BlockSpec((1, D_MODEL), lambda i: (0, 0)),
                  pl.BlockSpec((None, D_MODEL, W_IN_HALF), lambda i: (layer, 0, 1))],
        out_specs=[pl.BlockSpec((PROJ_TM, wd), lambda i: (i, 0)) for wd in PROJ_WIDTHS],
        out_shape=out_shape,
        compiler_params=_cparams(("arbitrary",)),
        name="proj",
    )(x, g, w)


def _gmlp_kernel(u_ref, v_ref, g_ref, ws_ref, bs_ref, o_ref):
    u = jax.nn.gelu(u_ref[...].astype(F32))
    v = _rms(jax.nn.gelu(v_ref[...].astype(F32)), g_ref[...]).astype(BF16)
    row = lax.broadcasted_iota(jnp.int32, (GM_CHUNK, GM_CHUNK), 0)
    col = lax.broadcasted_iota(jnp.int32, (GM_CHUNK, GM_CHUNK), 1)
    causal = row >= col
    for gi in range(GM_GROUPS):
        w = jnp.where(causal, ws_ref[gi], 0.0).astype(BF16)
        cs = slice(gi * GM_GW, (gi + 1) * GM_GW)
        for c in range(GMLP_TM // GM_CHUNK):
            rs = slice(c * GM_CHUNK, (c + 1) * GM_CHUNK)
            mixed = jnp.dot(w, v[rs, cs], preferred_element_type=F32) + bs_ref[gi]
            o_ref[rs, cs] = (u[rs, cs] * mixed).astype(o_ref.dtype)


def _gmlp(u, v, g, ws, bs):
    tok = lambda i: (i, 0)
    return pl.pallas_call(
        _gmlp_kernel,
        grid=(TOKENS // GMLP_TM,),
        in_specs=[pl.BlockSpec((GMLP_TM, MIX), tok),
                  pl.BlockSpec((GMLP_TM, MIX), tok),
                  pl.BlockSpec((1, MIX), lambda i: (0, 0)),
                  pl.BlockSpec((GM_GROUPS, GM_CHUNK, GM_CHUNK), lambda i: (0, 0, 0)),
                  pl.BlockSpec((GM_GROUPS, GM_CHUNK, GM_GW), lambda i: (0, 0, 0))],
        out_specs=pl.BlockSpec((GMLP_TM, MIX), tok),
        out_shape=jax.ShapeDtypeStruct((TOKENS, MIX), BF16),
        compiler_params=_cparams(("arbitrary",)),
        name="gmlp",
    )(u, v, g, ws, bs)


LRU_TAIL = 8


def _lru_kernel(gb_ref, rb_ref, cw_ref, cb_ref, wa_ref, ba_ref, wx_ref, bx_ref, lam_ref, o_ref,
                tail_ref, h_ref):
    @pl.when(pl.program_id(1) == 0)
    def _():
        tail_ref[...] = jnp.zeros_like(tail_ref)
        h_ref[...] = jnp.zeros_like(h_ref)

    ts = LRU_TS
    x = rb_ref[...].astype(F32)
    ext = jnp.concatenate([tail_ref[...], x], axis=0)
    tail_ref[...] = x[ts - LRU_TAIL:, :]
    xc = cb_ref[...] + x * cw_ref[CONV_WIDTH - 1:CONV_WIDTH, :]
    for d in range(1, CONV_WIDTH):
        xs = pltpu.roll(ext, d, axis=0)[LRU_TAIL:, :]
        xc = xc + xs * cw_ref[CONV_WIDTH - 1 - d:CONV_WIDTH - d, :]
    xcb = xc.astype(BF16)
    r = jax.nn.sigmoid(jnp.dot(xcb, wa_ref[...], preferred_element_type=F32) + ba_ref[...])
    ig = jax.nn.sigmoid(jnp.dot(xcb, wx_ref[...], preferred_element_type=F32) + bx_ref[...])
    z = -lam_ref[...]
    softplus = jnp.maximum(z, 0.0) + jnp.log1p(jnp.exp(-jnp.abs(z)))
    log_a = -LRU_C * r * softplus
    a = jnp.exp(log_a)
    b = jnp.sqrt(1.0 - jnp.exp(2.0 * log_a)) * (ig * xc)
    rows = lax.broadcasted_iota(jnp.int32, (ts, 1), 0)
    d = 1
    while d < ts:
        valid = rows >= d
        a_prev = pltpu.roll(a, d, axis=0)
        b_prev = pltpu.roll(b, d, axis=0)
        b = jnp.where(valid, a * b_prev, 0.0) + b
        a = jnp.where(valid, a * a_prev, a)
        d *= 2
    h = a * h_ref[...] + b
    h_ref[...] = h[ts - 1:ts, :]
    o_ref[...] = (jax.nn.gelu(gb_ref[...].astype(F32)) * h).astype(o_ref.dtype)


def _lru(gb, rb, cw, cb, wa, ba, wx, bx, lam):
    nt = SEQ // LRU_TS
    tok = lambda b, s: (b * nt + s, 0)
    vec = pl.BlockSpec((1, MIX), lambda b, s: (0, 0))
    mat = pl.BlockSpec((MIX, MIX), lambda b, s: (0, 0))
    return pl.pallas_call(
        _lru_kernel,
        grid=(BATCH, nt),
        in_specs=[pl.BlockSpec((LRU_TS, MIX), tok), pl.BlockSpec((LRU_TS, MIX), tok),
                  pl.BlockSpec((CONV_WIDTH, MIX), lambda b, s: (0, 0)), vec,
                  mat, vec, mat, vec, vec],
        out_specs=pl.BlockSpec((LRU_TS, MIX), tok),
        out_shape=jax.ShapeDtypeStruct((TOKENS, MIX), BF16),
        scratch_shapes=[pltpu.VMEM((LRU_TAIL, MIX), F32), pltpu.VMEM((1, MIX), F32)],
        compiler_params=_cparams(("arbitrary", "arbitrary")),
        name="rglru",
    )(gb, rb, cw, cb, wa, ba, wx, bx, lam)


POOL_TAIL = 16


def _pool_kernel(x_ref, w_ref, sc_ref, o_ref, tail_ref):
    s_id = pl.program_id(1)

    @pl.when(s_id == 0)
    def _():
        tail_ref[...] = jnp.zeros_like(tail_ref)

    ts = POOL_TS
    x = x_ref[...].astype(F32)
    ext = jnp.concatenate([tail_ref[...], x], axis=0)
    tail_ref[...] = x[ts - POOL_TAIL:, :]
    pos = s_id * ts + lax.broadcasted_iota(jnp.int32, (ts, 1), 0)
    acc = ext
    width = 1
    for gi, wdw in enumerate(POOL_WINDOWS):
        while width < wdw:
            acc = acc + pltpu.roll(acc, width, axis=0)
            width *= 2
        cs = slice(gi * POOL_GW, (gi + 1) * POOL_GW)
        cnt = jnp.minimum(pos + 1, wdw).astype(F32)
        pooled = acc[POOL_TAIL:, cs] / cnt - x[:, cs]
        mixed = jnp.dot(pooled.astype(BF16), w_ref[gi], preferred_element_type=F32)
        o_ref[:, cs] = (mixed * sc_ref[:, cs]).astype(o_ref.dtype)


def _pool(xd, w, sc):
    nt = SEQ // POOL_TS
    tok = lambda b, s: (b * nt + s, 0)
    return pl.pallas_call(
        _pool_kernel,
        grid=(BATCH, nt),
        in_specs=[pl.BlockSpec((POOL_TS, MIX), tok),
                  pl.BlockSpec((len(POOL_WINDOWS), POOL_GW, POOL_GW), lambda b, s: (0, 0, 0)),
                  pl.BlockSpec((1, MIX), lambda b, s: (0, 0))],
        out_specs=pl.BlockSpec((POOL_TS, MIX), tok),
        out_shape=jax.ShapeDtypeStruct((TOKENS, MIX), BF16),
        scratch_shapes=[pltpu.VMEM((POOL_TAIL, MIX), F32)],
        compiler_params=_cparams(("arbitrary", "arbitrary")),
        name="pool",
    )(xd, w, sc)


def _kvprep_kernel(kc_in, vc_in, vs_in, vw_in, ng_in, pe_ref, w1_ref, w2_ref, kc_ref, vct_ref,
                   vst_ref, vwt_ref, gt_ref, xf_ref):
    row = lax.broadcasted_iota(jnp.int32, (N_CHUNK, 1), 0)
    for c, x_in in enumerate((kc_in, vc_in)):
        xf_ref[...] = x_in[...].astype(F32)
        first = jnp.zeros((N_CHUNK, NSA_KV), F32)
        second = jnp.zeros((N_CHUNK, NSA_KV), F32)
        for l in range(CMP_STRIDE):
            xl = xf_ref[pl.ds(l, N_CHUNK, stride=CMP_STRIDE), :]
            first += jnp.dot((xl + pe_ref[c, l:l + 1, :]).astype(BF16), w1_ref[c, l],
                             preferred_element_type=F32)
            l2 = l + CMP_STRIDE
            second += jnp.dot((xl + pe_ref[c, l2:l2 + 1, :]).astype(BF16), w1_ref[c, l2],
                              preferred_element_type=F32)
        hid = jax.nn.gelu(first + pltpu.roll(second, N_CHUNK - 1, axis=0))
        out = jnp.dot(hid.astype(BF16), w2_ref[c], preferred_element_type=F32)
        out = jnp.where(row < N_CMP, out, 0.0)
        if c == 0:
            kc_ref[0] = out.astype(kc_ref.dtype)
        else:
            vct_ref[0] = out.T.astype(vct_ref.dtype)
    vst_ref[0] = vs_in[...].astype(F32).T.astype(vst_ref.dtype)
    vwt_ref[0] = vw_in[...].astype(F32).T.astype(vwt_ref.dtype)
    gates_t = ng_in[...].T
    gt_ref[...] = jnp.zeros_like(gt_ref)
    per_group = 3 * HPG
    for g in range(N_KV):
        gt_ref[0, g, :per_group, :] = gates_t[g * per_group:(g + 1) * per_group, :]


def _kvprep(kv, ng, pe, w1, w2):
    col = lambda j: pl.BlockSpec((SEQ, NSA_KV), lambda b: (b, j))
    whole = lambda a: pl.BlockSpec(a.shape, lambda b: (0,) * a.ndim)
    return pl.pallas_call(
        _kvprep_kernel,
        grid=(BATCH,),
        in_specs=[col(0), col(1), col(3), col(5), pl.BlockSpec((SEQ, GATE_PAD), lambda b: (b, 0)),
                  whole(pe), whole(w1), whole(w2)],
        out_specs=[pl.BlockSpec((1, N_CHUNK, NSA_KV), lambda b: (b, 0, 0)),
                   pl.BlockSpec((1, NSA_KV, N_CHUNK), lambda b: (b, 0, 0)),
                   pl.BlockSpec((1, NSA_KV, SEQ), lambda b: (b, 0, 0)),
                   pl.BlockSpec((1, NSA_KV, SEQ), lambda b: (b, 0, 0)),
                   pl.BlockSpec((1, N_KV, GATE_ROWS, SEQ), lambda b: (b, 0, 0, 0))],
        out_shape=[jax.ShapeDtypeStruct((BATCH, N_CHUNK, NSA_KV), BF16),
                   jax.ShapeDtypeStruct((BATCH, NSA_KV, N_CHUNK), BF16),
                   jax.ShapeDtypeStruct((BATCH, NSA_KV, SEQ), BF16),
                   jax.ShapeDtypeStruct((BATCH, NSA_KV, SEQ), BF16),
                   jax.ShapeDtypeStruct((BATCH, N_KV, GATE_ROWS, SEQ), F32)],
        scratch_shapes=[pltpu.VMEM((SEQ, NSA_KV), F32)],
        compiler_params=_cparams(("arbitrary",)),
        name="kvprep",
    )(kv, kv, kv, kv, ng, pe, w1, w2)


def _group_diag(w):
    eye = jnp.eye(N_KV, dtype=w.dtype)
    out = jnp.einsum('...gde,gh->...gdhe', w, eye)
    return out.reshape(w.shape[:-3] + (N_KV * w.shape[-2], N_KV * w.shape[-1]))


def _lane_tile(x, n):
    return jnp.concatenate([x] * n, axis=1)


ONES_ROWS = 16


def _with_ones(v_t):
    return jnp.concatenate([v_t, jnp.ones((ONES_ROWS, v_t.shape[1]), BF16)], axis=0)


def _nsa_kernel(q_ref, kc_ref, vct_ref, ks_ref, vst_ref, kw_ref, vwt_ref, gt_ref, ovt_ref, blk_ref,
                y_ref, sa_ref, sb_ref, oslc_ref):
    i = pl.program_id(1)
    tq = NSA_TQ
    nl = HPG * tq
    t_row = i * tq + lax.broadcasted_iota(jnp.int32, (1, tq), 1)
    groups = range(N_KV)
    vrows = [slice(g * HEAD_DIM, (g + 1) * HEAD_DIM) for g in groups]

    def scores(k, q_t):
        return jnp.dot(k, q_t, preferred_element_type=F32)

    def normalized(acc):
        return acc[:HEAD_DIM, :] / acc[HEAD_DIM:HEAD_DIM + 1, :]

    lane = lax.broadcasted_iota(jnp.int32, (tq, LANES), 1)
    scale = HEAD_DIM ** -0.5
    q4 = [jnp.concatenate(
        [(jnp.where((lane >= HEAD_DIM) == (g == 1),
                    q_ref[:, j * LANES:(j + 1) * LANES].astype(F32), 0.0) * scale).T
         for j in range(HPG)], axis=1).astype(BF16) for g in groups]

    n_idx = lax.broadcasted_iota(jnp.int32, (N_CHUNK, 1), 0)
    ok = n_idx * CMP_STRIDE + (CMP_LEN - 1) <= _lane_tile(t_row, HPG)
    blk = lax.broadcasted_iota(jnp.int32, (N_SLC, 1), 0)
    cur = jnp.right_shift(t_row, 6)
    forced = (blk == 0) | (blk == cur) | (blk == cur - 1)
    causal_blk = blk * SLC_LEN <= t_row
    o_cmp, q_aug = [], []
    for g in groups:
        s = jnp.where(ok, scores(kc_ref[0], q4[g]), NEG_INF)
        m = jnp.max(s, axis=0, keepdims=True)
        e = jnp.where(ok, jnp.exp(s - m), 0.0)
        l = jnp.sum(e, axis=0, keepdims=True)
        p = e / jnp.where(l > 0.0, l, 1.0)
        o_cmp.append(jnp.dot(vct_ref[0, vrows[g], :], p.astype(BF16), preferred_element_type=F32))
        psum = p[:, 0:tq]
        for j in range(1, HPG):
            psum = psum + p[:, j * tq:(j + 1) * tq]
        p_hi = psum.astype(BF16)
        p_lo = (psum - p_hi.astype(F32)).astype(BF16)
        imp = (jnp.dot(ovt_ref[...], p_hi, preferred_element_type=F32)
               + jnp.dot(ovt_ref[...], p_lo, preferred_element_type=F32))
        work = jnp.where(forced, -3e38, jnp.where(causal_blk, imp, -1.0))
        sel = jnp.where(forced, 1.0, 0.0)
        for _ in range(SLC_TOPN - MAX_FORCED):
            mx = jnp.max(work, axis=0, keepdims=True)
            idx = jnp.min(jnp.where(work == mx, blk, N_SLC), axis=0, keepdims=True)
            pick = blk == idx
            sel = jnp.where(pick & (mx >= 0.0), 1.0, sel)
            work = jnp.where(pick, -3e38, work)
        sel_bias = jnp.where(sel > 0.0, 0.0, NEG_INF)
        sel_bias = jnp.concatenate([sel_bias, jnp.zeros((LANES - N_SLC, tq), F32)], axis=0)
        sel_bias = _lane_tile(sel_bias.astype(BF16), HPG)
        q_aug.append(jnp.concatenate([q4[g], sel_bias], axis=0))

    start = pl.multiple_of(jnp.maximum(i - WIN // tq, 0) * tq, tq)
    k_win = kw_ref[pl.ds(start, WIN_KEYS), :]
    delta = t_row - (start + lax.broadcasted_iota(jnp.int32, (WIN_KEYS, 1), 0))
    win_bias = _lane_tile(jnp.where((delta >= 0) & (delta < WIN), 0.0, NEG_INF), HPG)
    o_win = []
    for g in groups:
        sT = scores(k_win, q4[g]) + win_bias
        pT = jnp.exp((sT - jnp.max(sT, axis=0, keepdims=True)).astype(BF16))
        vT = _with_ones(vwt_ref[vrows[g], pl.ds(start, WIN_KEYS)])
        o_win.append(normalized(jnp.dot(vT, pT, preferred_element_type=F32)))

    def slc_scores(kt, dst_ref):
        k0 = pl.multiple_of(kt * SLC_KT, SLC_KT)
        k_aug = jnp.concatenate([ks_ref[pl.ds(k0, SLC_KT), :], blk_ref[pl.ds(k0, SLC_KT), :]],
                                axis=1)
        maxima = []
        for g in groups:
            s_t = scores(k_aug, q_aug[g])
            dst_ref[g] = s_t
            maxima.append(jnp.max(s_t, axis=0, keepdims=True))
        return maxima

    def slc_update(kt, state, s_t, tile_max):
        k0 = pl.multiple_of(kt * SLC_KT, SLC_KT)
        out = []
        for g in groups:
            m_i, acc = state[2 * g], state[2 * g + 1]
            m_new = jnp.maximum(m_i, tile_max[g])
            alpha = jnp.exp(m_i - m_new)
            pT = jnp.exp((s_t[g] - m_new).astype(BF16))
            vT = _with_ones(vst_ref[vrows[g], pl.ds(k0, SLC_KT)])
            out += [m_new, alpha * acc + jnp.dot(vT, pT, preferred_element_type=F32)]
        return out

    def from_ref(src_ref):
        return [src_ref[g] for g in groups]

    def slc_pair(p, carry):
        state, max_a = list(carry[:2 * N_KV]), list(carry[2 * N_KV:])
        max_b = slc_scores(2 * p + 1, sb_ref)
        state = slc_update(2 * p, state, from_ref(sa_ref), max_a)
        max_a = slc_scores(2 * p + 2, sa_ref)
        state = slc_update(2 * p + 1, state, from_ref(sb_ref), max_b)
        return tuple(state + max_a)

    def slc_finish(state, src_ref):
        kpos = last_kt * SLC_KT + lax.broadcasted_iota(jnp.int32, (SLC_KT, 1), 0)
        causal_bias = _lane_tile(jnp.where(kpos <= t_row, 0.0, NEG_INF), HPG)
        s_last = [s + causal_bias for s in from_ref(src_ref)]
        state = slc_update(last_kt, state, s_last,
                           [jnp.max(s, axis=0, keepdims=True) for s in s_last])
        for g in groups:
            oslc_ref[g] = normalized(state[2 * g + 1])

    last_kt = (i * tq) // SLC_KT
    init = ((jnp.full((1, nl), -3e38, F32), jnp.zeros((HEAD_DIM + ONES_ROWS, nl), F32)) * N_KV
            + tuple(slc_scores(0, sa_ref)))
    carry = lax.fori_loop(0, last_kt // 2, slc_pair, init)
    state, max_a = list(carry[:2 * N_KV]), list(carry[2 * N_KV:])
    odd = lax.rem(last_kt, 2) == 1

    @pl.when(odd)
    def _():
        slc_scores(last_kt, sb_ref)
        slc_finish(slc_update(last_kt - 1, state, from_ref(sa_ref), max_a), sb_ref)

    @pl.when(jnp.logical_not(odd))
    def _():
        slc_finish(state, sa_ref)

    o_slc = [oslc_ref[g] for g in groups]

    for g in groups:
        gates = jax.nn.sigmoid(gt_ref[0, g])

        def gate_row(br):
            return jnp.concatenate([gates[3 * j + br:3 * j + br + 1, :] for j in range(HPG)],
                                   axis=1)

        o = gate_row(0) * o_cmp[g] + gate_row(1) * o_slc[g] + gate_row(2) * o_win[g]
        for j in range(HPG):
            h0 = (g * HPG + j) * HEAD_DIM
            y_ref[0, h0:h0 + HEAD_DIM, :] = o[:, j * tq:(j + 1) * tq].astype(y_ref.dtype)


def _block_onehot():
    oh = (np.arange(SEQ)[:, None] // SLC_LEN) == np.arange(LANES)[None, :]
    return jnp.asarray(oh.astype(np.float32), dtype=BF16)


def _nsa(q, kc, vct, kv, vst, vwt, gt, ovt):
    nq = SEQ // NSA_TQ
    return pl.pallas_call(
        _nsa_kernel,
        grid=(BATCH, nq),
        in_specs=[pl.BlockSpec((NSA_TQ, NSA_Q), lambda b, i: (b * nq + i, 0)),
                  pl.BlockSpec((1, N_CHUNK, NSA_KV), lambda b, i: (b, 0, 0)),
                  pl.BlockSpec((1, NSA_KV, N_CHUNK), lambda b, i: (b, 0, 0)),
                  pl.BlockSpec((SEQ, NSA_KV), lambda b, i: (b, 2)),
                  pl.BlockSpec((None, NSA_KV, SEQ), lambda b, i: (b, 0, 0)),
                  pl.BlockSpec((SEQ, NSA_KV), lambda b, i: (b, 4)),
                  pl.BlockSpec((None, NSA_KV, SEQ), lambda b, i: (b, 0, 0)),
                  pl.BlockSpec((1, N_KV, GATE_ROWS, NSA_TQ), lambda b, i: (b, 0, 0, i)),
                  pl.BlockSpec((N_SLC, N_CHUNK), lambda b, i: (0, 0)),
                  pl.BlockSpec((SEQ, LANES), lambda b, i: (0, 0))],
        out_specs=pl.BlockSpec((1, NSA_Q, NSA_TQ), lambda b, i: (b, 0, i)),
        out_shape=jax.ShapeDtypeStruct((BATCH, NSA_Q, SEQ), BF16),
        scratch_shapes=[pltpu.VMEM((N_KV, SLC_KT, HPG * NSA_TQ), F32),
                        pltpu.VMEM((N_KV, SLC_KT, HPG * NSA_TQ), F32),
                        pltpu.VMEM((N_KV, HEAD_DIM, HPG * NSA_TQ), F32)],
        compiler_params=_cparams(("arbitrary", "arbitrary")),
        name="nsa",
    )(q, kc, vct, kv, vst, kv, vwt, gt, ovt, _block_onehot())


def _overlap_t():
    c_start = np.arange(N_CHUNK) * CMP_STRIDE
    s_start = np.arange(N_SLC) * SLC_LEN
    ov = ((c_start[None, :] <= s_start[:, None] + SLC_LEN - 1)
          & (c_start[None, :] + CMP_LEN - 1 >= s_start[:, None])
          & (np.arange(N_CHUNK)[None, :] < N_CMP))
    return jnp.asarray(ov.astype(np.float32), dtype=BF16)


def _merge_kernel(x_ref, g_ref, wmg_ref, ya_ref, yb_ref, yct_ref, yd_ref, wb_ref, wo_ref, o_ref):
    x = x_ref[...]
    nb = _rms(x, g_ref[...]).astype(BF16)
    yc = yct_ref[0].astype(F32).T.astype(BF16)
    ys = (ya_ref[...], yb_ref[...], yc, yd_ref[...])
    merged = jnp.zeros((MERGE_TM, D_MODEL), F32)
    for bi, y in enumerate(ys):
        gate = jax.nn.sigmoid(jnp.dot(nb, wmg_ref[:, bi * D_MODEL:(bi + 1) * D_MODEL],
                                      preferred_element_type=F32))
        merged = merged + gate * jnp.dot(y, wb_ref[bi], preferred_element_type=F32)
    o_ref[...] = x + jnp.dot(merged.astype(BF16), wo_ref[...], preferred_element_type=F32)


def _merge(x, g, wmg, ya, yb, yct, yd, wb, wo, layer):
    nt = SEQ // MERGE_TM
    tok = lambda i: (i, 0)
    const2 = lambda i: (0, 0)
    return pl.pallas_call(
        _merge_kernel,
        grid=(TOKENS // MERGE_TM,),
        in_specs=[pl.BlockSpec((MERGE_TM, D_MODEL), tok),
                  pl.BlockSpec((1, D_MODEL), const2),
                  pl.BlockSpec((None, D_MODEL, N_BRANCH * D_MODEL), lambda i: (layer, 0, 0)),
                  pl.BlockSpec((MERGE_TM, MIX), tok),
                  pl.BlockSpec((MERGE_TM, MIX), tok),
                  pl.BlockSpec((1, MIX, MERGE_TM), lambda i: (i // nt, 0, i % nt)),
                  pl.BlockSpec((MERGE_TM, MIX), tok),
                  pl.BlockSpec((None, N_BRANCH, MIX, D_MODEL), lambda i: (layer, 0, 0, 0)),
                  pl.BlockSpec((None, D_MODEL, D_MODEL), lambda i: (layer, 0, 0))],
        out_specs=pl.BlockSpec((MERGE_TM, D_MODEL), tok),
        out_shape=jax.ShapeDtypeStruct((TOKENS, D_MODEL), F32),
        compiler_params=_cparams(("arbitrary",)),
        name="merge",
    )(x, g, wmg, ya, yb, yct, yd, wb, wo)


def _router_logits(t, wr_ref, br_ref):
    w = wr_ref[...]
    t_hi, w_hi = t.astype(BF16), w.astype(BF16)
    t_lo = (t - t_hi.astype(F32)).astype(BF16)
    w_lo = (w - w_hi.astype(F32)).astype(BF16)
    dot = functools.partial(jnp.dot, preferred_element_type=F32)
    return dot(t_hi, w_hi) + dot(t_hi, w_lo) + dot(t_lo, w_hi) + br_ref[...]


def _top_group(logits):
    lane = lax.broadcasted_iota(jnp.int32, logits.shape, 1)
    is_grp = lane < N_GROUPS
    lg = jnp.where(is_grp, logits, NEG_INF)
    gmax = jnp.max(lg, axis=1, keepdims=True)
    grp_idx = jnp.min(jnp.where(is_grp & (lg == gmax), lane, ROUTER_PAD), axis=1, keepdims=True)
    return lg, gmax, grp_idx


def _group_weight(logits, grp_idx):
    lane = lax.broadcasted_iota(jnp.int32, logits.shape, 1)
    lg, gmax, _ = _top_group(logits)
    ge = jnp.where(lane < N_GROUPS, jnp.exp(lg - gmax), 0.0)
    return (jnp.sum(jnp.where(lane == grp_idx, ge, 0.0), axis=1, keepdims=True)
            / jnp.sum(ge, axis=1, keepdims=True))


def _route_kernel(x_ref, g_ref, wr_ref, br_ref, tri_ref, meta_ref, cnt_ref, run_ref):
    @pl.when(pl.program_id(0) == 0)
    def _():
        run_ref[...] = jnp.zeros_like(run_ref)

    logits = _router_logits(_rms(x_ref[...], g_ref[...]), wr_ref, br_ref)
    _, _, grp_idx = _top_group(logits)
    lane = lax.broadcasted_iota(jnp.int32, logits.shape, 1)
    first = N_GROUPS + grp_idx * EXPERTS_PER_GROUP
    in_grp = (lane >= first) & (lane < first + EXPERTS_PER_GROUP)
    le = jnp.where(in_grp, logits, NEG_INF)
    l1 = jnp.max(le, axis=1, keepdims=True)
    i1 = jnp.min(jnp.where(in_grp & (le == l1), lane, ROUTER_PAD), axis=1, keepdims=True)
    le2 = jnp.where(lane == i1, NEG_INF, le)
    l2 = jnp.max(le2, axis=1, keepdims=True)
    i2 = jnp.min(jnp.where(in_grp & (lane != i1) & (le2 == l2), lane, ROUTER_PAD), axis=1,
                 keepdims=True)
    lo = jnp.minimum(i1, i2) - first
    hi = jnp.maximum(i1, i2) - first
    pair = jnp.right_shift(lo * (2 * EXPERTS_PER_GROUP - 1 - lo), 1) + hi - lo - 1
    cls = grp_idx * N_PAIRS + pair
    onehot = jnp.where(lane == cls, 1.0, 0.0)
    before = jnp.dot(tri_ref[...], onehot.astype(BF16), preferred_element_type=F32) + run_ref[...]
    rank = jnp.sum(onehot * before, axis=1, keepdims=True).astype(jnp.int32)
    run_ref[...] += jnp.sum(onehot, axis=0, keepdims=True)
    meta = jnp.where(lane == 0, cls, jnp.where(lane == 1, rank, 0)).astype(F32)
    meta_ref[...] = meta.T[:SUBLANES, :]
    cnt_ref[...] = run_ref[...]


def _route(x, g, wr, br):
    tok = lambda i: (i, 0)
    const2 = lambda i: (0, 0)
    tri = jnp.asarray(np.tril(np.ones((ROUTE_TM, ROUTE_TM), np.float32), -1), dtype=BF16)
    return pl.pallas_call(
        _route_kernel,
        grid=(TOKENS // ROUTE_TM,),
        in_specs=[pl.BlockSpec((ROUTE_TM, D_MODEL), tok),
                  pl.BlockSpec((1, D_MODEL), const2),
                  pl.BlockSpec((D_MODEL, ROUTER_PAD), const2),
                  pl.BlockSpec((1, ROUTER_PAD), const2),
                  pl.BlockSpec((ROUTE_TM, ROUTE_TM), const2)],
        out_specs=[pl.BlockSpec((SUBLANES, ROUTE_TM), lambda i: (0, i)),
                   pl.BlockSpec((1, ROUTER_PAD), const2)],
        out_shape=[jax.ShapeDtypeStruct((SUBLANES, TOKENS), F32),
                   jax.ShapeDtypeStruct((1, ROUTER_PAD), F32)],
        scratch_shapes=[pltpu.VMEM((1, ROUTER_PAD), F32)],
        compiler_params=_cparams(("arbitrary",)),
        name="route",
    )(x, g, wr, br, tri)


MOVE_SLOTS = 3


def _move_rows_kernel(idx_ref, seg_ref, len_ref, src_hbm, dst_hbm, buf, zbuf, in_sem, out_sem,
                      pad_sem, *, scatter):
    c = pl.program_id(0)
    n = pl.num_programs(0)
    slot = lax.rem(c, MOVE_SLOTS)
    nxt = lax.rem(c + 1, MOVE_SLOTS)
    rc = ROW_CHUNK
    chunk = lambda k: pl.ds(k * rc, rc)

    if scatter:
        def read(k, sl):
            return [pltpu.make_async_copy(src_hbm.at[chunk(k)], buf.at[sl], in_sem.at[sl])]

        def write(k, sl):
            return [pltpu.make_async_copy(buf.at[sl, pl.ds(r, 1)],
                                          dst_hbm.at[pl.ds(idx_ref[k * rc + r], 1)], out_sem.at[sl])
                    for r in range(rc)]

        def write_done(sl):
            return pltpu.make_async_copy(buf.at[sl], dst_hbm.at[chunk(0)], out_sem.at[sl])

        def read_done(sl):
            return read(0, sl)[0]
    else:
        def read(k, sl):
            return [pltpu.make_async_copy(src_hbm.at[pl.ds(idx_ref[k * rc + r], 1)],
                                          buf.at[sl, pl.ds(r, 1)], in_sem.at[sl])
                    for r in range(rc)]

        def write(k, sl):
            return [pltpu.make_async_copy(buf.at[sl], dst_hbm.at[chunk(k)], out_sem.at[sl])]

        def write_done(sl):
            return write(0, sl)[0]

        def read_done(sl):
            return pltpu.make_async_copy(src_hbm.at[chunk(0)], buf.at[sl], in_sem.at[sl])

    def pad_fills(fn):
        for sg in range(N_CLASSES):
            start, length = seg_ref[sg], len_ref[sg]
            head = jnp.bitwise_and(-start, SUBLANES - 1)
            for r in range(SUBLANES - 1):
                @pl.when(r < head)
                def _():
                    fn(pltpu.make_async_copy(zbuf.at[pl.ds(0, 1)], dst_hbm.at[pl.ds(start + r, 1)],
                                             pad_sem))
            body = length - head
            bit = rc // 2
            while bit >= SUBLANES:
                done = body - jnp.bitwise_and(body, 2 * bit - 1)
                first = pl.multiple_of(start + head + done, SUBLANES)

                @pl.when(jnp.bitwise_and(body, bit) != 0)
                def _():
                    fn(pltpu.make_async_copy(zbuf.at[pl.ds(0, bit)], dst_hbm.at[pl.ds(first, bit)],
                                             pad_sem))
                bit //= 2
        tail = pl.multiple_of(seg_ref[N_CLASSES], rc)

        def tail_tile(k, carry):
            fn(pltpu.make_async_copy(zbuf, dst_hbm.at[pl.ds(tail + k * rc, rc)], pad_sem))
            return carry
        lax.fori_loop(0, len_ref[N_CLASSES] // rc, tail_tile, 0)

    @pl.when(c == 0)
    def _():
        for cp in read(0, 0):
            cp.start()
        if scatter:
            zbuf[...] = jnp.zeros_like(zbuf)
            pad_fills(lambda cp: cp.start())

    @pl.when(c + 1 < n)
    def _():
        @pl.when(c >= 2)
        def _():
            write_done(nxt).wait()
        for cp in read(c + 1, nxt):
            cp.start()

    read_done(slot).wait()
    for cp in write(c, slot):
        cp.start()

    @pl.when(c == n - 1)
    def _():
        write_done(nxt).wait()
        write_done(lax.rem(c + 2, MOVE_SLOTS)).wait()
        write_done(slot).wait()
        if scatter:
            pad_fills(lambda cp: cp.wait())


def _move_rows(idx, seg_start, seg_len, src, n_out, scatter):
    any_spec = pl.BlockSpec(memory_space=pl.ANY)
    grid_spec = pltpu.PrefetchScalarGridSpec(
        num_scalar_prefetch=3, grid=(TOKENS // ROW_CHUNK,), in_specs=[any_spec], out_specs=any_spec,
        scratch_shapes=[pltpu.VMEM((MOVE_SLOTS, ROW_CHUNK, D_MODEL), F32),
                        pltpu.VMEM((ROW_CHUNK, D_MODEL), F32),
                        pltpu.SemaphoreType.DMA((MOVE_SLOTS,)),
                        pltpu.SemaphoreType.DMA((MOVE_SLOTS,)),
                        pltpu.SemaphoreType.DMA(())])
    return pl.pallas_call(
        functools.partial(_move_rows_kernel, scatter=scatter),
        grid_spec=grid_spec,
        out_shape=jax.ShapeDtypeStruct((n_out, D_MODEL), F32),
        compiler_params=_cparams(("arbitrary",)),
        name="dispatch" if scatter else "collect",
    )(idx, seg_start, seg_len, src)


def _experts_kernel(ea_ref, eb_ref, tv_ref, tb_ref, x_ref, g_ref, wr_ref, br_ref, wga_ref, wua_ref,
                    wda_ref, wgb_ref, wub_ref, wdb_ref, fg_ref, o_ref, *, layer, final_norm):
    j = pl.program_id(0)
    nv = tv_ref[j]

    @pl.when(nv > 0)
    def _():
        x = x_ref[...]
        t = _rms(x, g_ref[...])
        logits = _router_logits(t, wr_ref, br_ref)
        lane = lax.broadcasted_iota(jnp.int32, logits.shape, 1)
        ea = ea_ref[j] - layer * N_EXPERTS
        eb = eb_ref[j] - layer * N_EXPERTS
        la = jnp.sum(jnp.where(lane == N_GROUPS + ea, logits, 0.0), axis=1, keepdims=True)
        lb = jnp.sum(jnp.where(lane == N_GROUPS + eb, logits, 0.0), axis=1, keepdims=True)
        top = jnp.maximum(la, lb)
        pa, pb = jnp.exp(la - top), jnp.exp(lb - top)
        scale = _group_weight(logits, jnp.right_shift(ea, 2)) / (pa + pb)
        tb = t.astype(BF16)
        acc = jnp.zeros((MOE_TM, D_MODEL), F32)
        for wg, wu, wd, w in ((wga_ref, wua_ref, wda_ref, pa * scale),
                              (wgb_ref, wub_ref, wdb_ref, pb * scale)):
            hid = (jax.nn.silu(jnp.dot(tb, wg[0], preferred_element_type=F32))
                   * jnp.dot(tb, wu[0], preferred_element_type=F32)) * w
            acc = acc + jnp.dot(hid.astype(BF16), wd[0], preferred_element_type=F32)
        h = x + acc
        if final_norm:
            h = _rms(h, fg_ref[...])
        o_ref[...] = h

    @pl.when(nv == 0)
    def _():
        o_ref[...] = jnp.zeros_like(o_ref)


def _experts(tile_ea, tile_eb, tile_valid, tile_block, hs, g, wr, br, wg, wu, wd, fg, layer,
             final_norm):
    const2 = lambda j, *_: (0, 0)
    of_a = lambda j, ea, eb, tv, tb: (ea[j], 0, 0)
    of_b = lambda j, ea, eb, tv, tb: (eb[j], 0, 0)
    up = lambda idx: pl.BlockSpec((1, D_MODEL, D_EXPERT), idx)
    down = lambda idx: pl.BlockSpec((1, D_EXPERT, D_MODEL), idx)
    grid_spec = pltpu.PrefetchScalarGridSpec(
        num_scalar_prefetch=4,
        grid=(MOE_NT,),
        in_specs=[pl.BlockSpec((MOE_TM, D_MODEL), lambda j, ea, eb, tv, tb: (tb[j], 0)),
                  pl.BlockSpec((1, D_MODEL), const2),
                  pl.BlockSpec((D_MODEL, ROUTER_PAD), const2),
                  pl.BlockSpec((1, ROUTER_PAD), const2),
                  up(of_a), up(of_a), down(of_a), up(of_b), up(of_b), down(of_b),
                  pl.BlockSpec((1, D_MODEL), const2)],
        out_specs=pl.BlockSpec((MOE_TM, D_MODEL), lambda j, *_: (j, 0)))
    return pl.pallas_call(
        functools.partial(_experts_kernel, layer=layer, final_norm=final_norm),
        grid_spec=grid_spec,
        out_shape=jax.ShapeDtypeStruct((MOE_ROWS, D_MODEL), F32),
        compiler_params=_cparams(("arbitrary",)),
        name="experts",
    )(tile_ea, tile_eb, tile_valid, tile_block, hs, g, wr, br, wg, wu, wd, wg, wu, wd, fg)


_PAIR_LO = np.array([a for a in range(EXPERTS_PER_GROUP) for b in range(a + 1, EXPERTS_PER_GROUP)])
_PAIR_HI = np.array([b for a in range(EXPERTS_PER_GROUP) for b in range(a + 1, EXPERTS_PER_GROUP)])


def _moe(x, g, wr, br, wg, wu, wd, fg, layer, final_norm):
    meta, cnt = _route(x, g, wr, br)
    cls, rank = meta[0].astype(jnp.int32), meta[1].astype(jnp.int32)
    counts = cnt[0, :N_CLASSES].astype(jnp.int32)
    padded = (counts + MOE_TM - 1) // MOE_TM * MOE_TM
    ends = jnp.cumsum(padded)
    starts = ends - padded
    pos = starts[cls] + rank
    tile_start = jnp.arange(MOE_NT, dtype=jnp.int32) * MOE_TM
    tile_cls = jnp.minimum(jnp.sum(tile_start[:, None] >= ends[None, :], axis=1), N_CLASSES - 1)
    tile_valid = jnp.clip(starts[tile_cls] + counts[tile_cls] - tile_start, 0, MOE_TM)
    tile_valid = tile_valid.astype(jnp.int32)
    tile_block = jnp.where(tile_valid > 0, jnp.arange(MOE_NT, dtype=jnp.int32), 0)
    first = layer * N_EXPERTS + (tile_cls // N_PAIRS) * EXPERTS_PER_GROUP
    tile_ea = (first + jnp.asarray(_PAIR_LO)[tile_cls % N_PAIRS]).astype(jnp.int32)
    tile_eb = (first + jnp.asarray(_PAIR_HI)[tile_cls % N_PAIRS]).astype(jnp.int32)
    seg_start = jnp.concatenate([starts + counts, ends[-1:]])
    seg_len = jnp.concatenate([padded - counts, MOE_ROWS - ends[-1:]])
    hs = _move_rows(pos, seg_start, seg_len, x, MOE_ROWS, scatter=True)
    ys = _experts(tile_ea, tile_eb, tile_valid, tile_block, hs, g, wr, br, wg, wu, wd, fg, layer,
                  final_norm)
    return _move_rows(pos, seg_start, seg_len, ys, TOKENS, scatter=False)


def _block_diag(w):
    eye = jnp.eye(LRU_BLOCKS, dtype=w.dtype)
    return jnp.einsum('hij,hk->hikj', w, eye).reshape(MIX, MIX)


def _prep_w_in(w_in):
    cuts = [int(c) for c in np.cumsum((0,) + IN_SPLITS)]
    cols = lambda a, b: w_in[:, :, a:b]
    q_parts = [cols(cuts[4] + (g * HPG + j) * HEAD_DIM, cuts[4] + (g * HPG + j + 1) * HEAD_DIM)
               for j in range(HPG) for g in range(N_KV)]
    n_pad = W_IN_HALF - sum(PROJ_WIDTHS) + GATE_PAD - IN_SPLITS[6]
    parts = ([cols(cuts[8], cuts[9]), cols(cuts[0], cuts[4])] + q_parts
             + [cols(cuts[5], cuts[6]), cols(cuts[7], cuts[8]), cols(cuts[6], cuts[7]),
                jnp.zeros((DEPTH, D_MODEL, n_pad), w_in.dtype)])
    return jnp.concatenate(parts, axis=2).astype(BF16)


def _nsa_mixer(q, kv, ng, p):
    w1 = p['cmp_w1'].reshape(2, N_KV, CMP_LEN, HEAD_DIM, HEAD_DIM).transpose(0, 2, 1, 3, 4)
    pe = jnp.concatenate([p['cmp_pe']] * N_KV, axis=-1)
    kc, vct, vst, vwt, gt = _kvprep(kv, ng, pe, _group_diag(w1).astype(BF16),
                                    _group_diag(p['cmp_w2']).astype(BF16))
    return _nsa(q, kc, vct, kv, vst, vwt, gt, _overlap_t())


def _layer(h, p, big, layer, final_g, final_norm):
    row = lambda a: a.reshape(1, -1)

    u, v, gb, rb, q, kv, xd, ng = _proj(h, row(p['norm1_g']), big['w_in'], layer)

    bs = jnp.broadcast_to(p['gm_b'][:, :, None], (GM_GROUPS, GM_CHUNK, GM_GW))
    y_a = _gmlp(u, v, row(p['gm_norm_g']), p['gm_ws'], bs)

    y_b = _lru(gb, rb, p['conv_w'], row(p['conv_b']), _block_diag(p['lru_wa']).astype(BF16),
               row(p['lru_ba']), _block_diag(p['lru_wx']).astype(BF16), row(p['lru_bx']),
               row(p['lru_lambda']))

    y_d = _pool(xd, p['pool_w'].astype(BF16), row(p['pool_scale']))

    y_ct = _nsa_mixer(q, kv, ng, p)

    h = _merge(h, row(p['norm1_g']), big['w_in'], y_a, y_b, y_ct, y_d, big['w_branch'],
               big['w_out'], layer)

    wr = jnp.concatenate([p['router_w_group'], p['router_w_expert']], axis=1)
    wr = jnp.pad(wr, ((0, 0), (0, ROUTER_PAD - wr.shape[1])))
    br = jnp.concatenate([p['router_b_group'], p['router_b_expert']])
    br = jnp.pad(br, (0, ROUTER_PAD - br.shape[0])).reshape(1, ROUTER_PAD)
    return _moe(h, row(p['norm2_g']), wr, br, big['moe_w_gate'], big['moe_w_up'],
                big['moe_w_down'], row(final_g), layer, final_norm)


_LAYER_PARAMS = ('norm1_g', 'gm_norm_g', 'gm_ws', 'gm_b', 'conv_w', 'conv_b', 'lru_wa',
                 'lru_ba', 'lru_wx', 'lru_bx', 'lru_lambda', 'cmp_pe', 'cmp_w1', 'cmp_w2', 'pool_w',
                 'pool_scale', 'norm2_g', 'router_w_group', 'router_b_group',
                 'router_w_expert', 'router_b_expert')


def kernel(x, norm1_g, w_in, gm_norm_g, gm_ws, gm_b, conv_w, conv_b, lru_wa, lru_ba, lru_wx,
           lru_bx, lru_lambda, cmp_pe, cmp_w1, cmp_w2, pool_w, pool_scale, w_branch, w_out,
           norm2_g, router_w_group, router_b_group, router_w_expert, router_b_expert,
           moe_w_gate, moe_w_up, moe_w_down, final_norm_g):
    stacked = dict(zip(_LAYER_PARAMS, (
        norm1_g, gm_norm_g, gm_ws, gm_b, conv_w, conv_b, lru_wa, lru_ba, lru_wx, lru_bx,
        lru_lambda, cmp_pe, cmp_w1, cmp_w2, pool_w, pool_scale, norm2_g,
        router_w_group, router_b_group, router_w_expert, router_b_expert)))
    w_all = _prep_w_in(w_in)
    experts = lambda w: w.astype(BF16).reshape((DEPTH * N_EXPERTS,) + w.shape[2:])
    big = dict(w_in=w_all, w_branch=w_branch.astype(BF16), w_out=w_out.astype(BF16),
               moe_w_gate=experts(moe_w_gate), moe_w_up=experts(moe_w_up),
               moe_w_down=experts(moe_w_down))
    h = x.reshape(TOKENS, D_MODEL)
    for layer in range(DEPTH):
        p = {k: a[layer] for k, a in stacked.items()}
        h = _layer(h, p, big, layer, final_norm_g, final_norm=(layer == DEPTH - 1))
    return h.reshape(BATCH, SEQ, D_MODEL)
```

```python
import functools

import numpy as np
import jax
import jax.numpy as jnp
from jax import lax
from jax.experimental import pallas as pl
from jax.experimental.pallas import tpu as pltpu

F32 = jnp.float32
BF16 = jnp.bfloat16

D_MODEL = 1024
BATCH = 4
SEQ = 4096
TOKENS = BATCH * SEQ
DEPTH = 2
MIX = D_MODEL // 2
GM_CHUNK = 128
GM_GROUPS = 4
GM_GW = MIX // GM_GROUPS
CONV_WIDTH = 4
LRU_BLOCKS = 8
LRU_BW = MIX // LRU_BLOCKS
LRU_C = 8.0
N_HEADS = 8
HEAD_DIM = MIX // N_HEADS
N_KV = 2
HPG = N_HEADS // N_KV
CMP_LEN = 32
CMP_STRIDE = 16
SLC_LEN = 64
SLC_TOPN = 8
MAX_FORCED = 3
WIN = 512
NSA_Q = N_HEADS * HEAD_DIM
NSA_KV = N_KV * HEAD_DIM
POOL_WINDOWS = (2, 4, 8, 16)
POOL_GW = MIX // len(POOL_WINDOWS)
N_BRANCH = 4
N_GROUPS = 4
EXPERTS_PER_GROUP = 4
N_EXPERTS = N_GROUPS * EXPERTS_PER_GROUP
D_EXPERT = D_MODEL // 2
EPS = 1e-6
NEG_INF = -1e30
FORCE_SCORE = 1e6
IN_SPLITS = (MIX, MIX, MIX, MIX, NSA_Q, 6 * NSA_KV, 3 * N_HEADS, MIX, N_BRANCH * D_MODEL)

N_CHUNK = SEQ // CMP_STRIDE
N_CMP = N_CHUNK - CMP_LEN // CMP_STRIDE + 1
N_SLC = SEQ // SLC_LEN

LANES = 128
SUBLANES = 8
GATE_PAD = LANES
GATE_ROWS = 16
ROUTER_PAD = LANES
VMEM_LIMIT = 56 * 1024 * 1024

PROJ_WIDTHS = (MIX, MIX, MIX, MIX, NSA_Q, 6 * NSA_KV, MIX, GATE_PAD)
W_IN_HALF = N_BRANCH * D_MODEL
PROJ_TM = 512
PREP_ROWS = 256
GMLP_TM = 512
LRU_TS = 512
POOL_TS = 512
NSA_TQ = 256
SLC_KT = 256
WIN_KEYS = WIN + NSA_TQ
MERGE_TM = 256
ROUTE_TM = 512
MOE_TM = 256
N_PAIRS = EXPERTS_PER_GROUP * (EXPERTS_PER_GROUP - 1) // 2
N_CLASSES = N_GROUPS * N_PAIRS
MOE_ROWS = TOKENS + N_CLASSES * MOE_TM
MOE_NT = MOE_ROWS // MOE_TM
ROW_CHUNK = 256


def _cparams(sem):
    return pltpu.CompilerParams(dimension_semantics=sem, vmem_limit_bytes=VMEM_LIMIT)


def _rms(x, g):
    return x * lax.rsqrt(jnp.mean(x * x, axis=-1, keepdims=True) + EPS) * g


def _proj_kernel(x_ref, g_ref, w_ref, *out_refs):
    nb = _rms(x_ref[...], g_ref[...]).astype(BF16)
    off = 0
    for ref in out_refs:
        w = ref.shape[-1]
        ref[...] = jnp.dot(nb, w_ref[:, off:off + w], preferred_element_type=F32).astype(ref.dtype)
        off += w


def _proj(x, g, w, layer):
    out_shape = [jax.ShapeDtypeStruct((TOKENS, wd), BF16) for wd in PROJ_WIDTHS[:-1]]
    out_shape.append(jax.ShapeDtypeStruct((TOKENS, GATE_PAD), F32))
    return pl.pallas_call(
        _proj_kernel,
        grid=(TOKENS // PROJ_TM,),
        in_specs=[pl.BlockSpec((PROJ_TM, D_MODEL), lambda i: (i, 0)),
                  pl.BlockSpec((1, D_MODEL), lambda i: (0, 0)),
                  pl.BlockSpec((None, D_MODEL, W_IN_HALF), lambda i: (layer, 0, 1))],
        out_specs=[pl.BlockSpec((PROJ_TM, wd), lambda i: (i, 0)) for wd in PROJ_WIDTHS],
        out_shape=out_shape,
        compiler_params=_cparams(("arbitrary",)),
        name="proj",
    )(x, g, w)


def _gmlp_kernel(u_ref, v_ref, g_ref, ws_ref, bs_ref, o_ref):
    u = jax.nn.gelu(u_ref[...].astype(F32))
    v = _rms(jax.nn.gelu(v_ref[...].astype(F32)), g_ref[...]).astype(BF16)
    row = lax.broadcasted_iota(jnp.int32, (GM_CHUNK, GM_CHUNK), 0)
    col = lax.broadcasted_iota(jnp.int32, (GM_CHUNK, GM_CHUNK), 1)
    causal = row >= col
    for gi in range(GM_GROUPS):
        w = jnp.where(causal, ws_ref[gi], 0.0).astype(BF16)
        cs = slice(gi * GM_GW, (gi + 1) * GM_GW)
        for c in range(GMLP_TM // GM_CHUNK):
            rs = slice(c * GM_CHUNK, (c + 1) * GM_CHUNK)
            mixed = jnp.dot(w, v[rs, cs], preferred_element_type=F32) + bs_ref[gi]
            o_ref[rs, cs] = (u[rs, cs] * mixed).astype(o_ref.dtype)


def _gmlp(u, v, g, ws, bs):
    tok = lambda i: (i, 0)
    return pl.pallas_call(
        _gmlp_kernel,
        grid=(TOKENS // GMLP_TM,),
        in_specs=[pl.BlockSpec((GMLP_TM, MIX), tok),
                  pl.BlockSpec((GMLP_TM, MIX), tok),
                  pl.BlockSpec((1, MIX), lambda i: (0, 0)),
                  pl.BlockSpec((GM_GROUPS, GM_CHUNK, GM_CHUNK), lambda i: (0, 0, 0)),
                  pl.BlockSpec((GM_GROUPS, GM_CHUNK, GM_GW), lambda i: (0, 0, 0))],
        out_specs=pl.BlockSpec((GMLP_TM, MIX), tok),
        out_shape=jax.ShapeDtypeStruct((TOKENS, MIX), BF16),
        compiler_params=_cparams(("arbitrary",)),
        name="gmlp",
    )(u, v, g, ws, bs)


LRU_TAIL = 8


def _lru_kernel(gb_ref, rb_ref, cw_ref, cb_ref, wa_ref, ba_ref, wx_ref, bx_ref, lam_ref, o_ref,
                tail_ref, h_ref):
    @pl.when(pl.program_id(1) == 0)
    def _():
        tail_ref[...] = jnp.zeros_like(tail_ref)
        h_ref[...] = jnp.zeros_like(h_ref)

    ts = LRU_TS
    x = rb_ref[...].astype(F32)
    ext = jnp.concatenate([tail_ref[...], x], axis=0)
    tail_ref[...] = x[ts - LRU_TAIL:, :]
    xc = cb_ref[...] + x * cw_ref[CONV_WIDTH - 1:CONV_WIDTH, :]
    for d in range(1, CONV_WIDTH):
        xs = pltpu.roll(ext, d, axis=0)[LRU_TAIL:, :]
        xc = xc + xs * cw_ref[CONV_WIDTH - 1 - d:CONV_WIDTH - d, :]
    xcb = xc.astype(BF16)
    r = jax.nn.sigmoid(jnp.dot(xcb, wa_ref[...], preferred_element_type=F32) + ba_ref[...])
    ig = jax.nn.sigmoid(jnp.dot(xcb, wx_ref[...], preferred_element_type=F32) + bx_ref[...])
    z = -lam_ref[...]
    softplus = jnp.maximum(z, 0.0) + jnp.log1p(jnp.exp(-jnp.abs(z)))
    log_a = -LRU_C * r * softplus
    a = jnp.exp(log_a)
    b = jnp.sqrt(1.0 - jnp.exp(2.0 * log_a)) * (ig * xc)
    rows = lax.broadcasted_iota(jnp.int32, (ts, 1), 0)
    d = 1
    while d < ts:
        valid = rows >= d
        a_prev = pltpu.roll(a, d, axis=0)
        b_prev = pltpu.roll(b, d, axis=0)
        b = jnp.where(valid, a * b_prev, 0.0) + b
        a = jnp.where(valid, a * a_prev, a)
        d *= 2
    h = a * h_ref[...] + b
    h_ref[...] = h[ts - 1:ts, :]
    o_ref[...] = (jax.nn.gelu(gb_ref[...].astype(F32)) * h).astype(o_ref.dtype)


def _lru(gb, rb, cw, cb, wa, ba, wx, bx, lam):
    nt = SEQ // LRU_TS
    tok = lambda b, s: (b * nt + s, 0)
    vec = pl.BlockSpec((1, MIX), lambda b, s: (0, 0))
    mat = pl.BlockSpec((MIX, MIX), lambda b, s: (0, 0))
    return pl.pallas_call(
        _lru_kernel,
        grid=(BATCH, nt),
        in_specs=[pl.BlockSpec((LRU_TS, MIX), tok), pl.BlockSpec((LRU_TS, MIX), tok),
                  pl.BlockSpec((CONV_WIDTH, MIX), lambda b, s: (0, 0)), vec,
                  mat, vec, mat, vec, vec],
        out_specs=pl.BlockSpec((LRU_TS, MIX), tok),
        out_shape=jax.ShapeDtypeStruct((TOKENS, MIX), BF16),
        scratch_shapes=[pltpu.VMEM((LRU_TAIL, MIX), F32), pltpu.VMEM((1, MIX), F32)],
        compiler_params=_cparams(("arbitrary", "arbitrary")),
        name="rglru",
    )(gb, rb, cw, cb, wa, ba, wx, bx, lam)


POOL_TAIL = 16


def _pool_kernel(x_ref, w_ref, sc_ref, o_ref, tail_ref):
    s_id = pl.program_id(1)

    @pl.when(s_id == 0)
    def _():
        tail_ref[...] = jnp.zeros_like(tail_ref)

    ts = POOL_TS
    x = x_ref[...].astype(F32)
    ext = jnp.concatenate([tail_ref[...], x], axis=0)
    tail_ref[...] = x[ts - POOL_TAIL:, :]
    pos = s_id * ts + lax.broadcasted_iota(jnp.int32, (ts, 1), 0)
    acc = ext
    width = 1
    for gi, wdw in enumerate(POOL_WINDOWS):
        while width < wdw:
            acc = acc + pltpu.roll(acc, width, axis=0)
            width *= 2
        cs = slice(gi * POOL_GW, (gi + 1) * POOL_GW)
        cnt = jnp.minimum(pos + 1, wdw).astype(F32)
        pooled = acc[POOL_TAIL:, cs] / cnt - x[:, cs]
        mixed = jnp.dot(pooled.astype(BF16), w_ref[gi], preferred_element_type=F32)
        o_ref[:, cs] = (mixed * sc_ref[:, cs]).astype(o_ref.dtype)


def _pool(xd, w, sc):
    nt = SEQ // POOL_TS
    tok = lambda b, s: (b * nt + s, 0)
    return pl.pallas_call(
        _pool_kernel,
        grid=(BATCH, nt),
        in_specs=[pl.BlockSpec((POOL_TS, MIX), tok),
                  pl.BlockSpec((len(POOL_WINDOWS), POOL_GW, POOL_GW), lambda b, s: (0, 0, 0)),
                  pl.BlockSpec((1, MIX), lambda b, s: (0, 0))],
        out_specs=pl.BlockSpec((POOL_TS, MIX), tok),
        out_shape=jax.ShapeDtypeStruct((TOKENS, MIX), BF16),
        scratch_shapes=[pltpu.VMEM((POOL_TAIL, MIX), F32)],
        compiler_params=_cparams(("arbitrary", "arbitrary")),
        name="pool",
    )(xd, w, sc)


def _kvprep_kernel(kc_in, vc_in, vs_in, vw_in, ng_in, pe_ref, w1_ref, w2_ref, kc_ref, vct_ref,
                   vst_ref, vwt_ref, gt_ref, xf_ref):
    row = lax.broadcasted_iota(jnp.int32, (N_CHUNK, 1), 0)
    for c, x_in in enumerate((kc_in, vc_in)):
        xf_ref[...] = x_in[...].astype(F32)
        first = jnp.zeros((N_CHUNK, NSA_KV), F32)
        second = jnp.zeros((N_CHUNK, NSA_KV), F32)
        for l in range(CMP_STRIDE):
            xl = xf_ref[pl.ds(l, N_CHUNK, stride=CMP_STRIDE), :]
            first += jnp.dot((xl + pe_ref[c, l:l + 1, :]).astype(BF16), w1_ref[c, l],
                             preferred_element_type=F32)
            l2 = l + CMP_STRIDE
            second += jnp.dot((xl + pe_ref[c, l2:l2 + 1, :]).astype(BF16), w1_ref[c, l2],
                              preferred_element_type=F32)
        hid = jax.nn.gelu(first + pltpu.roll(second, N_CHUNK - 1, axis=0))
        out = jnp.dot(hid.astype(BF16), w2_ref[c], preferred_element_type=F32)
        out = jnp.where(row < N_CMP, out, 0.0)
        if c == 0:
            kc_ref[0] = out.astype(kc_ref.dtype)
        else:
            vct_ref[0] = out.T.astype(vct_ref.dtype)
    vst_ref[0] = vs_in[...].astype(F32).T.astype(vst_ref.dtype)
    vwt_ref[0] = vw_in[...].astype(F32).T.astype(vwt_ref.dtype)
    gates_t = ng_in[...].T
    gt_ref[...] = jnp.zeros_like(gt_ref)
    per_group = 3 * HPG
    for g in range(N_KV):
        gt_ref[0, g, :per_group, :] = gates_t[g * per_group:(g + 1) * per_group, :]


def _kvprep(kv, ng, pe, w1, w2):
    col = lambda j: pl.BlockSpec((SEQ, NSA_KV), lambda b: (b, j))
    whole = lambda a: pl.BlockSpec(a.shape, lambda b: (0,) * a.ndim)
    return pl.pallas_call(
        _kvprep_kernel,
        grid=(BATCH,),
        in_specs=[col(0), col(1), col(3), col(5), pl.BlockSpec((SEQ, GATE_PAD), lambda b: (b, 0)),
                  whole(pe), whole(w1), whole(w2)],
        out_specs=[pl.BlockSpec((1, N_CHUNK, NSA_KV), lambda b: (b, 0, 0)),
                   pl.BlockSpec((1, NSA_KV, N_CHUNK), lambda b: (b, 0, 0)),
                   pl.BlockSpec((1, NSA_KV, SEQ), lambda b: (b, 0, 0)),
                   pl.BlockSpec((1, NSA_KV, SEQ), lambda b: (b, 0, 0)),
                   pl.BlockSpec((1, N_KV, GATE_ROWS, SEQ), lambda b: (b, 0, 0, 0))],
        out_shape=[jax.ShapeDtypeStruct((BATCH, N_CHUNK, NSA_KV), BF16),
                   jax.ShapeDtypeStruct((BATCH, NSA_KV, N_CHUNK), BF16),
                   jax.ShapeDtypeStruct((BATCH, NSA_KV, SEQ), BF16),
                   jax.ShapeDtypeStruct((BATCH, NSA_KV, SEQ), BF16),
                   jax.ShapeDtypeStruct((BATCH, N_KV, GATE_ROWS, SEQ), F32)],
        scratch_shapes=[pltpu.VMEM((SEQ, NSA_KV), F32)],
        compiler_params=_cparams(("arbitrary",)),
        name="kvprep",
    )(kv, kv, kv, kv, ng, pe, w1, w2)


def _group_diag(w):
    eye = jnp.eye(N_KV, dtype=w.dtype)
    out = jnp.einsum('...gde,gh->...gdhe', w, eye)
    return out.reshape(w.shape[:-3] + (N_KV * w.shape[-2], N_KV * w.shape[-1]))


def _lane_tile(x, n):
    return jnp.concatenate([x] * n, axis=1)


ONES_ROWS = 16


def _with_ones(v_t):
    return jnp.concatenate([v_t, jnp.ones((ONES_ROWS, v_t.shape[1]), BF16)], axis=0)


def _nsa_kernel(q_ref, kc_ref, vct_ref, ks_ref, vst_ref, kw_ref, vwt_ref, gt_ref, ovt_ref, blk_ref,
                y_ref, sa_ref, sb_ref, oslc_ref):
    i = pl.program_id(1)
    tq = NSA_TQ
    nl = HPG * tq
    t_row = i * tq + lax.broadcasted_iota(jnp.int32, (1, tq), 1)
    groups = range(N_KV)
    vrows = [slice(g * HEAD_DIM, (g + 1) * HEAD_DIM) for g in groups]

    def scores(k, q_t):
        return jnp.dot(k, q_t, preferred_element_type=F32)

    def normalized(acc):
        return acc[:HEAD_DIM, :] / acc[HEAD_DIM:HEAD_DIM + 1, :]

    lane = lax.broadcasted_iota(jnp.int32, (tq, LANES), 1)
    scale = HEAD_DIM ** -0.5
    q4 = [jnp.concatenate(
        [(jnp.where((lane >= HEAD_DIM) == (g == 1),
                    q_ref[:, j * LANES:(j + 1) * LANES].astype(F32), 0.0) * scale).T
         for j in range(HPG)], axis=1).astype(BF16) for g in groups]

    n_idx = lax.broadcasted_iota(jnp.int32, (N_CHUNK, 1), 0)
    ok = n_idx * CMP_STRIDE + (CMP_LEN - 1) <= _lane_tile(t_row, HPG)
    blk = lax.broadcasted_iota(jnp.int32, (N_SLC, 1), 0)
    cur = jnp.right_shift(t_row, 6)
    forced = (blk == 0) | (blk == cur) | (blk == cur - 1)
    causal_blk = blk * SLC_LEN <= t_row
    o_cmp, q_aug = [], []
    for g in groups:
        s = jnp.where(ok, scores(kc_ref[0], q4[g]), NEG_INF)
        m = jnp.max(s, axis=0, keepdims=True)
        e = jnp.where(ok, jnp.exp(s - m), 0.0)
        l = jnp.sum(e, axis=0, keepdims=True)
        p = e / jnp.where(l > 0.0, l, 1.0)
        o_cmp.append(jnp.dot(vct_ref[0, vrows[g], :], p.astype(BF16), preferred_element_type=F32))
        psum = p[:, 0:tq]
        for j in range(1, HPG):
            psum = psum + p[:, j * tq:(j + 1) * tq]
        p_hi = psum.astype(BF16)
        p_lo = (psum - p_hi.astype(F32)).astype(BF16)
        imp = (jnp.dot(ovt_ref[...], p_hi, preferred_element_type=F32)
               + jnp.dot(ovt_ref[...], p_lo, preferred_element_type=F32))
        work = jnp.where(forced, -3e38, jnp.where(causal_blk, imp, -1.0))
        sel = jnp.where(forced, 1.0, 0.0)
        for _ in range(SLC_TOPN - MAX_FORCED):
            mx = jnp.max(work, axis=0, keepdims=True)
            idx = jnp.min(jnp.where(work == mx, blk, N_SLC), axis=0, keepdims=True)
            pick = blk == idx
            sel = jnp.where(pick & (mx >= 0.0), 1.0, sel)
            work = jnp.where(pick, -3e38, work)
        sel_bias = jnp.where(sel > 0.0, 0.0, NEG_INF)
        sel_bias = jnp.concatenate([sel_bias, jnp.zeros((LANES - N_SLC, tq), F32)], axis=0)
        sel_bias = _lane_tile(sel_bias.astype(BF16), HPG)
        q_aug.append(jnp.concatenate([q4[g], sel_bias], axis=0))

    start = pl.multiple_of(jnp.maximum(i - WIN // tq, 0) * tq, tq)
    k_win = kw_ref[pl.ds(start, WIN_KEYS), :]
    delta = t_row - (start + lax.broadcasted_iota(jnp.int32, (WIN_KEYS, 1), 0))
    win_bias = _lane_tile(jnp.where((delta >= 0) & (delta < WIN), 0.0, NEG_INF), HPG)
    o_win = []
    for g in groups:
        sT = scores(k_win, q4[g]) + win_bias
        pT = jnp.exp((sT - jnp.max(sT, axis=0, keepdims=True)).astype(BF16))
        vT = _with_ones(vwt_ref[vrows[g], pl.ds(start, WIN_KEYS)])
        o_win.append(normalized(jnp.dot(vT, pT, preferred_element_type=F32)))

    def slc_scores(kt, dst_ref):
        k0 = pl.multiple_of(kt * SLC_KT, SLC_KT)
        k_aug = jnp.concatenate([ks_ref[pl.ds(k0, SLC_KT), :], blk_ref[pl.ds(k0, SLC_KT), :]],
                                axis=1)
        maxima = []
        for g in groups:
            s_t = scores(k_aug, q_aug[g])
            dst_ref[g] = s_t
            maxima.append(jnp.max(s_t, axis=0, keepdims=True))
        return maxima

    def slc_update(kt, state, s_t, tile_max):
        k0 = pl.multiple_of(kt * SLC_KT, SLC_KT)
        out = []
        for g in groups:
            m_i, acc = state[2 * g], state[2 * g + 1]
            m_new = jnp.maximum(m_i, tile_max[g])
            alpha = jnp.exp(m_i - m_new)
            pT = jnp.exp((s_t[g] - m_new).astype(BF16))
            vT = _with_ones(vst_ref[vrows[g], pl.ds(k0, SLC_KT)])
            out += [m_new, alpha * acc + jnp.dot(vT, pT, preferred_element_type=F32)]
        return out

    def from_ref(src_ref):
        return [src_ref[g] for g in groups]

    def slc_pair(p, carry):
        state, max_a = list(carry[:2 * N_KV]), list(carry[2 * N_KV:])
        max_b = slc_scores(2 * p + 1, sb_ref)
        state = slc_update(2 * p, state, from_ref(sa_ref), max_a)
        max_a = slc_scores(2 * p + 2, sa_ref)
        state = slc_update(2 * p + 1, state, from_ref(sb_ref), max_b)
        return tuple(state + max_a)

    def slc_finish(state, src_ref):
        kpos = last_kt * SLC_KT + lax.broadcasted_iota(jnp.int32, (SLC_KT, 1), 0)
        causal_bias = _lane_tile(jnp.where(kpos <= t_row, 0.0, NEG_INF), HPG)
        s_last = [s + causal_bias for s in from_ref(src_ref)]
        state = slc_update(last_kt, state, s_last,
                           [jnp.max(s, axis=0, keepdims=True) for s in s_last])
        for g in groups:
            oslc_ref[g] = normalized(state[2 * g + 1])

    last_kt = (i * tq) // SLC_KT
    init = ((jnp.full((1, nl), -3e38, F32), jnp.zeros((HEAD_DIM + ONES_ROWS, nl), F32)) * N_KV
            + tuple(slc_scores(0, sa_ref)))
    carry = lax.fori_loop(0, last_kt // 2, slc_pair, init)
    state, max_a = list(carry[:2 * N_KV]), list(carry[2 * N_KV:])
    odd = lax.rem(last_kt, 2) == 1

    @pl.when(odd)
    def _():
        slc_scores(last_kt, sb_ref)
        slc_finish(slc_update(last_kt - 1, state, from_ref(sa_ref), max_a), sb_ref)

    @pl.when(jnp.logical_not(odd))
    def _():
        slc_finish(state, sa_ref)

    o_slc = [oslc_ref[g] for g in groups]

    for g in groups:
        gates = jax.nn.sigmoid(gt_ref[0, g])

        def gate_row(br):
            return jnp.concatenate([gates[3 * j + br:3 * j + br + 1, :] for j in range(HPG)],
                                   axis=1)

        o = gate_row(0) * o_cmp[g] + gate_row(1) * o_slc[g] + gate_row(2) * o_win[g]
        for j in range(HPG):
            h0 = (g * HPG + j) * HEAD_DIM
            y_ref[0, h0:h0 + HEAD_DIM, :] = o[:, j * tq:(j + 1) * tq].astype(y_ref.dtype)


def _block_onehot():
    oh = (np.arange(SEQ)[:, None] // SLC_LEN) == np.arange(LANES)[None, :]
    return jnp.asarray(oh.astype(np.float32), dtype=BF16)


def _nsa(q, kc, vct, kv, vst, vwt, gt, ovt):
    nq = SEQ // NSA_TQ
    return pl.pallas_call(
        _nsa_kernel,
        grid=(BATCH, nq),
        in_specs=[pl.BlockSpec((NSA_TQ, NSA_Q), lambda b, i: (b * nq + i, 0)),
                  pl.BlockSpec((1, N_CHUNK, NSA_KV), lambda b, i: (b, 0, 0)),
                  pl.BlockSpec((1, NSA_KV, N_CHUNK), lambda b, i: (b, 0, 0)),
                  pl.BlockSpec((SEQ, NSA_KV), lambda b, i: (b, 2)),
                  pl.BlockSpec((None, NSA_KV, SEQ), lambda b, i: (b, 0, 0)),
                  pl.BlockSpec((SEQ, NSA_KV), lambda b, i: (b, 4)),
                  pl.BlockSpec((None, NSA_KV, SEQ), lambda b, i: (b, 0, 0)),
                  pl.BlockSpec((1, N_KV, GATE_ROWS, NSA_TQ), lambda b, i: (b, 0, 0, i)),
                  pl.BlockSpec((N_SLC, N_CHUNK), lambda b, i: (0, 0)),
                  pl.BlockSpec((SEQ, LANES), lambda b, i: (0, 0))],
        out_specs=pl.BlockSpec((1, NSA_Q, NSA_TQ), lambda b, i: (b, 0, i)),
        out_shape=jax.ShapeDtypeStruct((BATCH, NSA_Q, SEQ), BF16),
        scratch_shapes=[pltpu.VMEM((N_KV, SLC_KT, HPG * NSA_TQ), F32),
                        pltpu.VMEM((N_KV, SLC_KT, HPG * NSA_TQ), F32),
                        pltpu.VMEM((N_KV, HEAD_DIM, HPG * NSA_TQ), F32)],
        compiler_params=_cparams(("arbitrary", "arbitrary")),
        name="nsa",
    )(q, kc, vct, kv, vst, kv, vwt, gt, ovt, _block_onehot())


def _overlap_t():
    c_start = np.arange(N_CHUNK) * CMP_STRIDE
    s_start = np.arange(N_SLC) * SLC_LEN
    ov = ((c_start[None, :] <= s_start[:, None] + SLC_LEN - 1)
          & (c_start[None, :] + CMP_LEN - 1 >= s_start[:, None])
          & (np.arange(N_CHUNK)[None, :] < N_CMP))
    return jnp.asarray(ov.astype(np.float32), dtype=BF16)


def _merge_kernel(x_ref, g_ref, wmg_ref, ya_ref, yb_ref, yct_ref, yd_ref, wb_ref, wo_ref, o_ref):
    x = x_ref[...]
    nb = _rms(x, g_ref[...]).astype(BF16)
    yc = yct_ref[0].astype(F32).T.astype(BF16)
    ys = (ya_ref[...], yb_ref[...], yc, yd_ref[...])
    merged = jnp.zeros((MERGE_TM, D_MODEL), F32)
    for bi, y in enumerate(ys):
        gate = jax.nn.sigmoid(jnp.dot(nb, wmg_ref[:, bi * D_MODEL:(bi + 1) * D_MODEL],
                                      preferred_element_type=F32))
        merged = merged + gate * jnp.dot(y, wb_ref[bi], preferred_element_type=F32)
    o_ref[...] = x + jnp.dot(merged.astype(BF16), wo_ref[...], preferred_element_type=F32)


def _merge(x, g, wmg, ya, yb, yct, yd, wb, wo, layer):
    nt = SEQ // MERGE_TM
    tok = lambda i: (i, 0)
    const2 = lambda i: (0, 0)
    return pl.pallas_call(
        _merge_kernel,
        grid=(TOKENS // MERGE_TM,),
        in_specs=[pl.BlockSpec((MERGE_TM, D_MODEL), tok),
                  pl.BlockSpec((1, D_MODEL), const2),
                  pl.BlockSpec((None, D_MODEL, N_BRANCH * D_MODEL), lambda i: (layer, 0, 0)),
                  pl.BlockSpec((MERGE_TM, MIX), tok),
                  pl.BlockSpec((MERGE_TM, MIX), tok),
                  pl.BlockSpec((1, MIX, MERGE_TM), lambda i: (i // nt, 0, i % nt)),
                  pl.BlockSpec((MERGE_TM, MIX), tok),
                  pl.BlockSpec((None, N_BRANCH, MIX, D_MODEL), lambda i: (layer, 0, 0, 0)),
                  pl.BlockSpec((None, D_MODEL, D_MODEL), lambda i: (layer, 0, 0))],
        out_specs=pl.BlockSpec((MERGE_TM, D_MODEL), tok),
        out_shape=jax.ShapeDtypeStruct((TOKENS, D_MODEL), F32),
        compiler_params=_cparams(("arbitrary",)),
        name="merge",
    )(x, g, wmg, ya, yb, yct, yd, wb, wo)


def _router_logits(t, wr_ref, br_ref):
    w = wr_ref[...]
    t_hi, w_hi = t.astype(BF16), w.astype(BF16)
    t_lo = (t - t_hi.astype(F32)).astype(BF16)
    w_lo = (w - w_hi.astype(F32)).astype(BF16)
    dot = functools.partial(jnp.dot, preferred_element_type=F32)
    return dot(t_hi, w_hi) + dot(t_hi, w_lo) + dot(t_lo, w_hi) + br_ref[...]


def _top_group(logits):
    lane = lax.broadcasted_iota(jnp.int32, logits.shape, 1)
    is_grp = lane < N_GROUPS
    lg = jnp.where(is_grp, logits, NEG_INF)
    gmax = jnp.max(lg, axis=1, keepdims=True)
    grp_idx = jnp.min(jnp.where(is_grp & (lg == gmax), lane, ROUTER_PAD), axis=1, keepdims=True)
    return lg, gmax, grp_idx


def _group_weight(logits, grp_idx):
    lane = lax.broadcasted_iota(jnp.int32, logits.shape, 1)
    lg, gmax, _ = _top_group(logits)
    ge = jnp.where(lane < N_GROUPS, jnp.exp(lg - gmax), 0.0)
    return (jnp.sum(jnp.where(lane == grp_idx, ge, 0.0), axis=1, keepdims=True)
            / jnp.sum(ge, axis=1, keepdims=True))


ROUTE_ROWS = 32


def _route_kernel(x_ref, g_ref, wr_ref, br_ref, tri_ref, meta_ref, cnt_ref, run_ref):
    @pl.when(pl.program_id(0) == 0)
    def _():
        run_ref[...] = jnp.zeros_like(run_ref)

    logits = _router_logits(_rms(x_ref[...], g_ref[...]), wr_ref, br_ref)
    lt = logits.T[:ROUTE_ROWS, :]
    row = lax.broadcasted_iota(jnp.int32, (ROUTE_ROWS, 1), 0)

    def top(mask, vals):
        v = jnp.where(mask, vals, NEG_INF)
        best = jnp.max(v, axis=0, keepdims=True)
        return v, jnp.min(jnp.where(mask & (v == best), row, ROUTE_ROWS), axis=0, keepdims=True)

    _, grp_idx = top(row < N_GROUPS, lt)
    first = N_GROUPS + grp_idx * EXPERTS_PER_GROUP
    in_grp = (row >= first) & (row < first + EXPERTS_PER_GROUP)
    le, i1 = top(in_grp, lt)
    _, i2 = top(in_grp & (row != i1), le)
    lo = jnp.minimum(i1, i2) - first
    hi = jnp.maximum(i1, i2) - first
    pair = jnp.right_shift(lo * (2 * EXPERTS_PER_GROUP - 1 - lo), 1) + hi - lo - 1
    cls = grp_idx * N_PAIRS + pair
    onehot = jnp.where(row == cls, 1.0, 0.0)
    before = jnp.dot(onehot.astype(BF16), tri_ref[...], preferred_element_type=F32) + run_ref[...]
    rank = jnp.sum(onehot * before, axis=0, keepdims=True).astype(jnp.int32)
    run_ref[...] += jnp.sum(onehot, axis=1, keepdims=True)
    sub = lax.broadcasted_iota(jnp.int32, (SUBLANES, 1), 0)
    meta_ref[...] = jnp.where(sub == 0, cls, jnp.where(sub == 1, rank, 0))
    cnt_ref[...] = jnp.broadcast_to(run_ref[...], cnt_ref.shape)


def _route(x, g, wr, br):
    tok = lambda i: (i, 0)
    const2 = lambda i: (0, 0)
    tri = jnp.asarray(np.triu(np.ones((ROUTE_TM, ROUTE_TM), np.float32), 1), dtype=BF16)
    return pl.pallas_call(
        _route_kernel,
        grid=(TOKENS // ROUTE_TM,),
        in_specs=[pl.BlockSpec((ROUTE_TM, D_MODEL), tok),
                  pl.BlockSpec((1, D_MODEL), const2),
                  pl.BlockSpec((D_MODEL, ROUTER_PAD), const2),
                  pl.BlockSpec((1, ROUTER_PAD), const2),
                  pl.BlockSpec((ROUTE_TM, ROUTE_TM), const2)],
        out_specs=[pl.BlockSpec((SUBLANES, ROUTE_TM), lambda i: (0, i)),
                   pl.BlockSpec((ROUTE_ROWS, LANES), const2)],
        out_shape=[jax.ShapeDtypeStruct((SUBLANES, TOKENS), jnp.int32),
                   jax.ShapeDtypeStruct((ROUTE_ROWS, LANES), F32)],
        scratch_shapes=[pltpu.VMEM((ROUTE_ROWS, 1), F32)],
        compiler_params=_cparams(("arbitrary",)),
        name="route",
    )(x, g, wr, br, tri)


MOVE_SLOTS = 3


def _move_rows_kernel(idx_ref, seg_ref, len_ref, src_hbm, dst_hbm, buf, zbuf, in_sem, out_sem,
                      pad_sem, *, scatter):
    c = pl.program_id(0)
    n = pl.num_programs(0)
    slot = lax.rem(c, MOVE_SLOTS)
    nxt = lax.rem(c + 1, MOVE_SLOTS)
    rc = ROW_CHUNK
    chunk = lambda k: pl.ds(k * rc, rc)

    if scatter:
        def read(k, sl):
            return [pltpu.make_async_copy(src_hbm.at[chunk(k)], buf.at[sl], in_sem.at[sl])]

        def write(k, sl):
            return [pltpu.make_async_copy(buf.at[sl, pl.ds(r, 1)],
                                          dst_hbm.at[pl.ds(idx_ref[k * rc + r], 1)], out_sem.at[sl])
                    for r in range(rc)]

        def write_done(sl):
            return pltpu.make_async_copy(buf.at[sl], dst_hbm.at[chunk(0)], out_sem.at[sl])

        def read_done(sl):
            return read(0, sl)[0]
    else:
        def read(k, sl):
            return [pltpu.make_async_copy(src_hbm.at[pl.ds(idx_ref[k * rc + r], 1)],
                                          buf.at[sl, pl.ds(r, 1)], in_sem.at[sl])
                    for r in range(rc)]

        def write(k, sl):
            return [pltpu.make_async_copy(buf.at[sl], dst_hbm.at[chunk(k)], out_sem.at[sl])]

        def write_done(sl):
            return write(0, sl)[0]

        def read_done(sl):
            return pltpu.make_async_copy(src_hbm.at[chunk(0)], buf.at[sl], in_sem.at[sl])

    def pad_fills(fn):
        for sg in range(N_CLASSES):
            start, length = seg_ref[sg], len_ref[sg]
            head = jnp.bitwise_and(-start, SUBLANES - 1)
            for r in range(SUBLANES - 1):
                @pl.when(r < head)
                def _():
                    fn(pltpu.make_async_copy(zbuf.at[pl.ds(0, 1)], dst_hbm.at[pl.ds(start + r, 1)],
                                             pad_sem))
            body = length - head
            bit = rc // 2
            while bit >= SUBLANES:
                done = body - jnp.bitwise_and(body, 2 * bit - 1)
                first = pl.multiple_of(start + head + done, SUBLANES)

                @pl.when(jnp.bitwise_and(body, bit) != 0)
                def _():
                    fn(pltpu.make_async_copy(zbuf.at[pl.ds(0, bit)], dst_hbm.at[pl.ds(first, bit)],
                                             pad_sem))
                bit //= 2
        tail = pl.multiple_of(seg_ref[N_CLASSES], rc)

        def tail_tile(k, carry):
            fn(pltpu.make_async_copy(zbuf, dst_hbm.at[pl.ds(tail + k * rc, rc)], pad_sem))
            return carry
        lax.fori_loop(0, len_ref[N_CLASSES] // rc, tail_tile, 0)

    @pl.when(c == 0)
    def _():
        for cp in read(0, 0):
            cp.start()
        if scatter:
            zbuf[...] = jnp.zeros_like(zbuf)
            pad_fills(lambda cp: cp.start())

    @pl.when(c + 1 < n)
    def _():
        @pl.when(c >= 2)
        def _():
            write_done(nxt).wait()
        for cp in read(c + 1, nxt):
            cp.start()

    read_done(slot).wait()
    for cp in write(c, slot):
        cp.start()

    @pl.when(c == n - 1)
    def _():
        write_done(nxt).wait()
        write_done(lax.rem(c + 2, MOVE_SLOTS)).wait()
        write_done(slot).wait()
        if scatter:
            pad_fills(lambda cp: cp.wait())


def _move_rows(idx, seg_start, seg_len, src, n_out, scatter):
    any_spec = pl.BlockSpec(memory_space=pl.ANY)
    grid_spec = pltpu.PrefetchScalarGridSpec(
        num_scalar_prefetch=3, grid=(TOKENS // ROW_CHUNK,), in_specs=[any_spec], out_specs=any_spec,
        scratch_shapes=[pltpu.VMEM((MOVE_SLOTS, ROW_CHUNK, D_MODEL), F32),
                        pltpu.VMEM((ROW_CHUNK, D_MODEL), F32),
                        pltpu.SemaphoreType.DMA((MOVE_SLOTS,)),
                        pltpu.SemaphoreType.DMA((MOVE_SLOTS,)),
                        pltpu.SemaphoreType.DMA(())])
    return pl.pallas_call(
        functools.partial(_move_rows_kernel, scatter=scatter),
        grid_spec=grid_spec,
        out_shape=jax.ShapeDtypeStruct((n_out, D_MODEL), F32),
        compiler_params=_cparams(("arbitrary",)),
        name="dispatch" if scatter else "collect",
    )(idx, seg_start, seg_len, src)


def _experts_kernel(ea_ref, eb_ref, tv_ref, tb_ref, x_ref, g_ref, wr_ref, br_ref, wga_ref, wua_ref,
                    wda_ref, wgb_ref, wub_ref, wdb_ref, fg_ref, o_ref, *, layer, final_norm):
    j = pl.program_id(0)
    nv = tv_ref[j]

    @pl.when(nv > 0)
    def _():
        x = x_ref[...]
        t = _rms(x, g_ref[...])
        logits = _router_logits(t, wr_ref, br_ref)
        lane = lax.broadcasted_iota(jnp.int32, logits.shape, 1)
        ea = ea_ref[j] - layer * N_EXPERTS
        eb = eb_ref[j] - layer * N_EXPERTS
        la = jnp.sum(jnp.where(lane == N_GROUPS + ea, logits, 0.0), axis=1, keepdims=True)
        lb = jnp.sum(jnp.where(lane == N_GROUPS + eb, logits, 0.0), axis=1, keepdims=True)
        top = jnp.maximum(la, lb)
        pa, pb = jnp.exp(la - top), jnp.exp(lb - top)
        scale = _group_weight(logits, jnp.right_shift(ea, 2)) / (pa + pb)
        tb = t.astype(BF16)
        acc = jnp.zeros((MOE_TM, D_MODEL), F32)
        for wg, wu, wd, w in ((wga_ref, wua_ref, wda_ref, pa * scale),
                              (wgb_ref, wub_ref, wdb_ref, pb * scale)):
            hid = (jax.nn.silu(jnp.dot(tb, wg[0], preferred_element_type=F32))
                   * jnp.dot(tb, wu[0], preferred_element_type=F32)) * w
            acc = acc + jnp.dot(hid.astype(BF16), wd[0], preferred_element_type=F32)
        h = x + acc
        if final_norm:
            h = _rms(h, fg_ref[...])
        o_ref[...] = h

    @pl.when(nv == 0)
    def _():
        o_ref[...] = jnp.zeros_like(o_ref)


def _experts(tile_ea, tile_eb, tile_valid, tile_block, hs, g, wr, br, wg, wu, wd, fg, layer,
             final_norm):
    const2 = lambda j, *_: (0, 0)
    of_a = lambda j, ea, eb, tv, tb: (ea[j], 0, 0)
    of_b = lambda j, ea, eb, tv, tb: (eb[j], 0, 0)
    up = lambda idx: pl.BlockSpec((1, D_MODEL, D_EXPERT), idx)
    down = lambda idx: pl.BlockSpec((1, D_EXPERT, D_MODEL), idx)
    grid_spec = pltpu.PrefetchScalarGridSpec(
        num_scalar_prefetch=4,
        grid=(MOE_NT,),
        in_specs=[pl.BlockSpec((MOE_TM, D_MODEL), lambda j, ea, eb, tv, tb: (tb[j], 0)),
                  pl.BlockSpec((1, D_MODEL), const2),
                  pl.BlockSpec((D_MODEL, ROUTER_PAD), const2),
                  pl.BlockSpec((1, ROUTER_PAD), const2),
                  up(of_a), up(of_a), down(of_a), up(of_b), up(of_b), down(of_b),
                  pl.BlockSpec((1, D_MODEL), const2)],
        out_specs=pl.BlockSpec((MOE_TM, D_MODEL), lambda j, *_: (j, 0)))
    return pl.pallas_call(
        functools.partial(_experts_kernel, layer=layer, final_norm=final_norm),
        grid_spec=grid_spec,
        out_shape=jax.ShapeDtypeStruct((MOE_ROWS, D_MODEL), F32),
        compiler_params=_cparams(("arbitrary",)),
        name="experts",
    )(tile_ea, tile_eb, tile_valid, tile_block, hs, g, wr, br, wg, wu, wd, wg, wu, wd, fg)


_PAIR_LO = np.array([a for a in range(EXPERTS_PER_GROUP) for b in range(a + 1, EXPERTS_PER_GROUP)])
_PAIR_HI = np.array([b for a in range(EXPERTS_PER_GROUP) for b in range(a + 1, EXPERTS_PER_GROUP)])


def _moe(x, g, wr, br, wg, wu, wd, fg, layer, final_norm):
    meta, cnt = _route(x, g, wr, br)
    cls, rank = meta[0], meta[1]
    counts = cnt[:N_CLASSES, 0].astype(jnp.int32)
    padded = (counts + MOE_TM - 1) // MOE_TM * MOE_TM
    ends = jnp.cumsum(padded)
    starts = ends - padded
    pos = starts[cls] + rank
    tile_start = jnp.arange(MOE_NT, dtype=jnp.int32) * MOE_TM
    tile_cls = jnp.minimum(jnp.sum(tile_start[:, None] >= ends[None, :], axis=1), N_CLASSES - 1)
    tile_valid = jnp.clip(starts[tile_cls] + counts[tile_cls] - tile_start, 0, MOE_TM)
    tile_valid = tile_valid.astype(jnp.int32)
    tile_block = jnp.where(tile_valid > 0, jnp.arange(MOE_NT, dtype=jnp.int32), 0)
    first = layer * N_EXPERTS + (tile_cls // N_PAIRS) * EXPERTS_PER_GROUP
    tile_ea = (first + jnp.asarray(_PAIR_LO)[tile_cls % N_PAIRS]).astype(jnp.int32)
    tile_eb = (first + jnp.asarray(_PAIR_HI)[tile_cls % N_PAIRS]).astype(jnp.int32)
    seg_start = jnp.concatenate([starts + counts, ends[-1:]])
    seg_len = jnp.concatenate([padded - counts, MOE_ROWS - ends[-1:]])
    hs = _move_rows(pos, seg_start, seg_len, x, MOE_ROWS, scatter=True)
    ys = _experts(tile_ea, tile_eb, tile_valid, tile_block, hs, g, wr, br, wg, wu, wd, fg, layer,
                  final_norm)
    return _move_rows(pos, seg_start, seg_len, ys, TOKENS, scatter=False)


def _block_diag(w):
    eye = jnp.eye(LRU_BLOCKS, dtype=w.dtype)
    return jnp.einsum('hij,hk->hikj', w, eye).reshape(MIX, MIX)


def _w_in_column_ranges():
    cuts = [int(c) for c in np.cumsum((0,) + IN_SPLITS)]
    q_parts = [(cuts[4] + (g * HPG + j) * HEAD_DIM, cuts[4] + (g * HPG + j + 1) * HEAD_DIM)
               for j in range(HPG) for g in range(N_KV)]
    return ([(cuts[8], cuts[9]), (cuts[0], cuts[4])] + q_parts
            + [(cuts[5], cuts[6]), (cuts[7], cuts[8]), (cuts[6], cuts[7])])


def _prep_w_in_kernel(w_ref, o_ref):
    off = 0
    for a, b in _w_in_column_ranges():
        o_ref[:, off:off + b - a] = w_ref[:, a:b].astype(o_ref.dtype)
        off += b - a
    o_ref[:, off:] = jnp.zeros((o_ref.shape[0], o_ref.shape[1] - off), o_ref.dtype)


def _prep_w_in(w_in):
    n_in = w_in.shape[-1]
    return pl.pallas_call(
        _prep_w_in_kernel,
        grid=(DEPTH, D_MODEL // PREP_ROWS),
        in_specs=[pl.BlockSpec((None, PREP_ROWS, n_in), lambda l, r: (l, r, 0))],
        out_specs=pl.BlockSpec((None, PREP_ROWS, 2 * W_IN_HALF), lambda l, r: (l, r, 0)),
        out_shape=jax.ShapeDtypeStruct((DEPTH, D_MODEL, 2 * W_IN_HALF), BF16),
        compiler_params=_cparams(("arbitrary", "arbitrary")),
        name="prep_w_in",
    )(w_in)


def _nsa_mixer(q, kv, ng, p):
    w1 = p['cmp_w1'].reshape(2, N_KV, CMP_LEN, HEAD_DIM, HEAD_DIM).transpose(0, 2, 1, 3, 4)
    pe = jnp.concatenate([p['cmp_pe']] * N_KV, axis=-1)
    kc, vct, vst, vwt, gt = _kvprep(kv, ng, pe, _group_diag(w1).astype(BF16),
                                    _group_diag(p['cmp_w2']).astype(BF16))
    return _nsa(q, kc, vct, kv, vst, vwt, gt, _overlap_t())


def _layer(h, p, big, layer, final_g, final_norm):
    row = lambda a: a.reshape(1, -1)

    u, v, gb, rb, q, kv, xd, ng = _proj(h, row(p['norm1_g']), big['w_in'], layer)

    bs = jnp.broadcast_to(p['gm_b'][:, :, None], (GM_GROUPS, GM_CHUNK, GM_GW))
    y_a = _gmlp(u, v, row(p['gm_norm_g']), p['gm_ws'], bs)

    y_b = _lru(gb, rb, p['conv_w'], row(p['conv_b']), _block_diag(p['lru_wa']).astype(BF16),
               row(p['lru_ba']), _block_diag(p['lru_wx']).astype(BF16), row(p['lru_bx']),
               row(p['lru_lambda']))

    y_d = _pool(xd, p['pool_w'].astype(BF16), row(p['pool_scale']))

    y_ct = _nsa_mixer(q, kv, ng, p)

    h = _merge(h, row(p['norm1_g']), big['w_in'], y_a, y_b, y_ct, y_d, big['w_branch'],
               big['w_out'], layer)

    wr = jnp.concatenate([p['router_w_group'], p['router_w_expert']], axis=1)
    wr = jnp.pad(wr, ((0, 0), (0, ROUTER_PAD - wr.shape[1])))
    br = jnp.concatenate([p['router_b_group'], p['router_b_expert']])
    br = jnp.pad(br, (0, ROUTER_PAD - br.shape[0])).reshape(1, ROUTER_PAD)
    return _moe(h, row(p['norm2_g']), wr, br, big['moe_w_gate'], big['moe_w_up'],
                big['moe_w_down'], row(final_g), layer, final_norm)


_LAYER_PARAMS = ('norm1_g', 'gm_norm_g', 'gm_ws', 'gm_b', 'conv_w', 'conv_b', 'lru_wa',
                 'lru_ba', 'lru_wx', 'lru_bx', 'lru_lambda', 'cmp_pe', 'cmp_w1', 'cmp_w2', 'pool_w',
                 'pool_scale', 'norm2_g', 'router_w_group', 'router_b_group',
                 'router_w_expert', 'router_b_expert')


def kernel(x, norm1_g, w_in, gm_norm_g, gm_ws, gm_b, conv_w, conv_b, lru_wa, lru_ba, lru_wx,
           lru_bx, lru_lambda, cmp_pe, cmp_w1, cmp_w2, pool_w, pool_scale, w_branch, w_out,
           norm2_g, router_w_group, router_b_group, router_w_expert, router_b_expert,
           moe_w_gate, moe_w_up, moe_w_down, final_norm_g):
    stacked = dict(zip(_LAYER_PARAMS, (
        norm1_g, gm_norm_g, gm_ws, gm_b, conv_w, conv_b, lru_wa, lru_ba, lru_wx, lru_bx,
        lru_lambda, cmp_pe, cmp_w1, cmp_w2, pool_w, pool_scale, norm2_g,
        router_w_group, router_b_group, router_w_expert, router_b_expert)))
    w_all = _prep_w_in(w_in)
    experts = lambda w: w.astype(BF16).reshape((DEPTH * N_EXPERTS,) + w.shape[2:])
    big = dict(w_in=w_all, w_branch=w_branch.astype(BF16), w_out=w_out.astype(BF16),
               moe_w_gate=experts(moe_w_gate), moe_w_up=experts(moe_w_up),
               moe_w_down=experts(moe_w_down))
    h = x.reshape(TOKENS, D_MODEL)
    for layer in range(DEPTH):
        p = {k: a[layer] for k, a in stacked.items()}
        h = _layer(h, p, big, layer, final_norm_g, final_norm=(layer == DEPTH - 1))
    return h.reshape(BATCH, SEQ, D_MODEL)
```

```python
import functools

import numpy as np
import jax
import jax.numpy as jnp
from jax import lax
from jax.experimental import pallas as pl
from jax.experimental.pallas import tpu as pltpu

F32 = jnp.float32
BF16 = jnp.bfloat16

D_MODEL = 1024
BATCH = 4
SEQ = 4096
TOKENS = BATCH * SEQ
DEPTH = 2
MIX = D_MODEL // 2
GM_CHUNK = 128
GM_GROUPS = 4
GM_GW = MIX // GM_GROUPS
CONV_WIDTH = 4
LRU_BLOCKS = 8
LRU_BW = MIX // LRU_BLOCKS
LRU_C = 8.0
N_HEADS = 8
HEAD_DIM = MIX // N_HEADS
N_KV = 2
HPG = N_HEADS // N_KV
CMP_LEN = 32
CMP_STRIDE = 16
SLC_LEN = 64
SLC_TOPN = 8
MAX_FORCED = 3
WIN = 512
NSA_Q = N_HEADS * HEAD_DIM
NSA_KV = N_KV * HEAD_DIM
POOL_WINDOWS = (2, 4, 8, 16)
POOL_GW = MIX // len(POOL_WINDOWS)
N_BRANCH = 4
N_GROUPS = 4
EXPERTS_PER_GROUP = 4
N_EXPERTS = N_GROUPS * EXPERTS_PER_GROUP
D_EXPERT = D_MODEL // 2
EPS = 1e-6
NEG_INF = -1e30
FORCE_SCORE = 1e6
IN_SPLITS = (MIX, MIX, MIX, MIX, NSA_Q, 6 * NSA_KV, 3 * N_HEADS, MIX, N_BRANCH * D_MODEL)

N_CHUNK = SEQ // CMP_STRIDE
N_CMP = N_CHUNK - CMP_LEN // CMP_STRIDE + 1
N_SLC = SEQ // SLC_LEN

LANES = 128
SUBLANES = 8
GATE_PAD = LANES
GATE_ROWS = 16
ROUTER_PAD = LANES
VMEM_LIMIT = 56 * 1024 * 1024

PROJ_WIDTHS = (MIX, MIX, MIX, MIX, NSA_Q, 6 * NSA_KV, MIX, GATE_PAD)
W_IN_HALF = N_BRANCH * D_MODEL
PROJ_TM = 512
PREP_ROWS = 256
GMLP_TM = 512
LRU_TS = 512
POOL_TS = 512
NSA_TQ = 256
SLC_KT = 256
WIN_KEYS = WIN + NSA_TQ
MERGE_TM = 256
ROUTE_TM = 512
MOE_TM = 256
N_PAIRS = EXPERTS_PER_GROUP * (EXPERTS_PER_GROUP - 1) // 2
N_CLASSES = N_GROUPS * N_PAIRS
MOE_ROWS = TOKENS + N_CLASSES * MOE_TM
MOE_NT = MOE_ROWS // MOE_TM
ROW_CHUNK = 256


def _cparams(sem):
    return pltpu.CompilerParams(dimension_semantics=sem, vmem_limit_bytes=VMEM_LIMIT)


def _rms(x, g):
    return x * lax.rsqrt(jnp.mean(x * x, axis=-1, keepdims=True) + EPS) * g


def _proj_kernel(x_ref, g_ref, w_ref, *out_refs):
    nb = _rms(x_ref[...], g_ref[...]).astype(BF16)
    off = 0
    for ref in out_refs:
        w = ref.shape[-1]
        ref[...] = jnp.dot(nb, w_ref[:, off:off + w], preferred_element_type=F32).astype(ref.dtype)
        off += w


def _proj(x, g, w, layer):
    out_shape = [jax.ShapeDtypeStruct((TOKENS, wd), BF16) for wd in PROJ_WIDTHS[:-1]]
    out_shape.append(jax.ShapeDtypeStruct((TOKENS, GATE_PAD), F32))
    return pl.pallas_call(
        _proj_kernel,
        grid=(TOKENS // PROJ_TM,),
        in_specs=[pl.BlockSpec((PROJ_TM, D_MODEL), lambda i: (i, 0)),
                  pl.BlockSpec((1, D_MODEL), lambda i: (0, 0)),
                  pl.BlockSpec((None, D_MODEL, W_IN_HALF), lambda i: (layer, 0, 1))],
        out_specs=[pl.BlockSpec((PROJ_TM, wd), lambda i: (i, 0)) for wd in PROJ_WIDTHS],
        out_shape=out_shape,
        compiler_params=_cparams(("arbitrary",)),
        name="proj",
    )(x, g, w)


def _gmlp_kernel(u_ref, v_ref, g_ref, ws_ref, bs_ref, o_ref):
    u = jax.nn.gelu(u_ref[...].astype(F32))
    v = _rms(jax.nn.gelu(v_ref[...].astype(F32)), g_ref[...]).astype(BF16)
    row = lax.broadcasted_iota(jnp.int32, (GM_CHUNK, GM_CHUNK), 0)
    col = lax.broadcasted_iota(jnp.int32, (GM_CHUNK, GM_CHUNK), 1)
    causal = row >= col
    for gi in range(GM_GROUPS):
        w = jnp.where(causal, ws_ref[gi], 0.0).astype(BF16)
        cs = slice(gi * GM_GW, (gi + 1) * GM_GW)
        for c in range(GMLP_TM // GM_CHUNK):
            rs = slice(c * GM_CHUNK, (c + 1) * GM_CHUNK)
            mixed = jnp.dot(w, v[rs, cs], preferred_element_type=F32) + bs_ref[gi]
            o_ref[rs, cs] = (u[rs, cs] * mixed).astype(o_ref.dtype)


def _gmlp(u, v, g, ws, bs):
    tok = lambda i: (i, 0)
    return pl.pallas_call(
        _gmlp_kernel,
        grid=(TOKENS // GMLP_TM,),
        in_specs=[pl.BlockSpec((GMLP_TM, MIX), tok),
                  pl.BlockSpec((GMLP_TM, MIX), tok),
                  pl.BlockSpec((1, MIX), lambda i: (0, 0)),
                  pl.BlockSpec((GM_GROUPS, GM_CHUNK, GM_CHUNK), lambda i: (0, 0, 0)),
                  pl.BlockSpec((GM_GROUPS, GM_CHUNK, GM_GW), lambda i: (0, 0, 0))],
        out_specs=pl.BlockSpec((GMLP_TM, MIX), tok),
        out_shape=jax.ShapeDtypeStruct((TOKENS, MIX), BF16),
        compiler_params=_cparams(("arbitrary",)),
        name="gmlp",
    )(u, v, g, ws, bs)


LRU_TAIL = 8


def _lru_kernel(gb_ref, rb_ref, cw_ref, cb_ref, wa_ref, ba_ref, wx_ref, bx_ref, lam_ref, o_ref,
                tail_ref, h_ref):
    @pl.when(pl.program_id(1) == 0)
    def _():
        tail_ref[...] = jnp.zeros_like(tail_ref)
        h_ref[...] = jnp.zeros_like(h_ref)

    ts = LRU_TS
    x = rb_ref[...].astype(F32)
    ext = jnp.concatenate([tail_ref[...], x], axis=0)
    tail_ref[...] = x[ts - LRU_TAIL:, :]
    xc = cb_ref[...] + x * cw_ref[CONV_WIDTH - 1:CONV_WIDTH, :]
    for d in range(1, CONV_WIDTH):
        xs = pltpu.roll(ext, d, axis=0)[LRU_TAIL:, :]
        xc = xc + xs * cw_ref[CONV_WIDTH - 1 - d:CONV_WIDTH - d, :]
    xcb = xc.astype(BF16)
    r = jax.nn.sigmoid(jnp.dot(xcb, wa_ref[...], preferred_element_type=F32) + ba_ref[...])
    ig = jax.nn.sigmoid(jnp.dot(xcb, wx_ref[...], preferred_element_type=F32) + bx_ref[...])
    z = -lam_ref[...]
    softplus = jnp.maximum(z, 0.0) + jnp.log1p(jnp.exp(-jnp.abs(z)))
    log_a = -LRU_C * r * softplus
    a = jnp.exp(log_a)
    b = jnp.sqrt(1.0 - jnp.exp(2.0 * log_a)) * (ig * xc)
    rows = lax.broadcasted_iota(jnp.int32, (ts, 1), 0)
    d = 1
    while d < ts:
        valid = rows >= d
        a_prev = pltpu.roll(a, d, axis=0)
        b_prev = pltpu.roll(b, d, axis=0)
        b = jnp.where(valid, a * b_prev, 0.0) + b
        a = jnp.where(valid, a * a_prev, a)
        d *= 2
    h = a * h_ref[...] + b
    h_ref[...] = h[ts - 1:ts, :]
    o_ref[...] = (jax.nn.gelu(gb_ref[...].astype(F32)) * h).astype(o_ref.dtype)


def _lru(gb, rb, cw, cb, wa, ba, wx, bx, lam):
    nt = SEQ // LRU_TS
    tok = lambda b, s: (b * nt + s, 0)
    vec = pl.BlockSpec((1, MIX), lambda b, s: (0, 0))
    mat = pl.BlockSpec((MIX, MIX), lambda b, s: (0, 0))
    return pl.pallas_call(
        _lru_kernel,
        grid=(BATCH, nt),
        in_specs=[pl.BlockSpec((LRU_TS, MIX), tok), pl.BlockSpec((LRU_TS, MIX), tok),
                  pl.BlockSpec((CONV_WIDTH, MIX), lambda b, s: (0, 0)), vec,
                  mat, vec, mat, vec, vec],
        out_specs=pl.BlockSpec((LRU_TS, MIX), tok),
        out_shape=jax.ShapeDtypeStruct((TOKENS, MIX), BF16),
        scratch_shapes=[pltpu.VMEM((LRU_TAIL, MIX), F32), pltpu.VMEM((1, MIX), F32)],
        compiler_params=_cparams(("arbitrary", "arbitrary")),
        name="rglru",
    )(gb, rb, cw, cb, wa, ba, wx, bx, lam)


POOL_TAIL = 16


def _pool_kernel(x_ref, w_ref, sc_ref, o_ref, tail_ref):
    s_id = pl.program_id(1)

    @pl.when(s_id == 0)
    def _():
        tail_ref[...] = jnp.zeros_like(tail_ref)

    ts = POOL_TS
    x = x_ref[...].astype(F32)
    ext = jnp.concatenate([tail_ref[...], x], axis=0)
    tail_ref[...] = x[ts - POOL_TAIL:, :]
    pos = s_id * ts + lax.broadcasted_iota(jnp.int32, (ts, 1), 0)
    acc = ext
    width = 1
    for gi, wdw in enumerate(POOL_WINDOWS):
        while width < wdw:
            acc = acc + pltpu.roll(acc, width, axis=0)
            width *= 2
        cs = slice(gi * POOL_GW, (gi + 1) * POOL_GW)
        cnt = jnp.minimum(pos + 1, wdw).astype(F32)
        pooled = acc[POOL_TAIL:, cs] / cnt - x[:, cs]
        mixed = jnp.dot(pooled.astype(BF16), w_ref[gi], preferred_element_type=F32)
        o_ref[:, cs] = (mixed * sc_ref[:, cs]).astype(o_ref.dtype)


def _pool(xd, w, sc):
    nt = SEQ // POOL_TS
    tok = lambda b, s: (b * nt + s, 0)
    return pl.pallas_call(
        _pool_kernel,
        grid=(BATCH, nt),
        in_specs=[pl.BlockSpec((POOL_TS, MIX), tok),
                  pl.BlockSpec((len(POOL_WINDOWS), POOL_GW, POOL_GW), lambda b, s: (0, 0, 0)),
                  pl.BlockSpec((1, MIX), lambda b, s: (0, 0))],
        out_specs=pl.BlockSpec((POOL_TS, MIX), tok),
        out_shape=jax.ShapeDtypeStruct((TOKENS, MIX), BF16),
        scratch_shapes=[pltpu.VMEM((POOL_TAIL, MIX), F32)],
        compiler_params=_cparams(("arbitrary", "arbitrary")),
        name="pool",
    )(xd, w, sc)


def _kvprep_kernel(kc_in, vc_in, vs_in, vw_in, ng_in, pe_ref, w1_ref, w2_ref, kc_ref, vct_ref,
                   vst_ref, vwt_ref, gt_ref, xf_ref):
    row = lax.broadcasted_iota(jnp.int32, (N_CHUNK, 1), 0)
    for c, x_in in enumerate((kc_in, vc_in)):
        xf_ref[...] = x_in[...].astype(F32)
        first = jnp.zeros((N_CHUNK, NSA_KV), F32)
        second = jnp.zeros((N_CHUNK, NSA_KV), F32)
        for l in range(CMP_STRIDE):
            xl = xf_ref[pl.ds(l, N_CHUNK, stride=CMP_STRIDE), :]
            first += jnp.dot((xl + pe_ref[c, l:l + 1, :]).astype(BF16), w1_ref[c, l],
                             preferred_element_type=F32)
            l2 = l + CMP_STRIDE
            second += jnp.dot((xl + pe_ref[c, l2:l2 + 1, :]).astype(BF16), w1_ref[c, l2],
                              preferred_element_type=F32)
        hid = jax.nn.gelu(first + pltpu.roll(second, N_CHUNK - 1, axis=0))
        out = jnp.dot(hid.astype(BF16), w2_ref[c], preferred_element_type=F32)
        out = jnp.where(row < N_CMP, out, 0.0)
        if c == 0:
            kc_ref[0] = out.astype(kc_ref.dtype)
        else:
            vct_ref[0] = out.T.astype(vct_ref.dtype)
    vst_ref[0] = vs_in[...].astype(F32).T.astype(vst_ref.dtype)
    vwt_ref[0] = vw_in[...].astype(F32).T.astype(vwt_ref.dtype)
    gates_t = ng_in[...].T
    gt_ref[...] = jnp.zeros_like(gt_ref)
    per_group = 3 * HPG
    for g in range(N_KV):
        gt_ref[0, g, :per_group, :] = gates_t[g * per_group:(g + 1) * per_group, :]


def _kvprep(kv, ng, pe, w1, w2):
    col = lambda j: pl.BlockSpec((SEQ, NSA_KV), lambda b: (b, j))
    whole = lambda a: pl.BlockSpec(a.shape, lambda b: (0,) * a.ndim)
    return pl.pallas_call(
        _kvprep_kernel,
        grid=(BATCH,),
        in_specs=[col(0), col(1), col(3), col(5), pl.BlockSpec((SEQ, GATE_PAD), lambda b: (b, 0)),
                  whole(pe), whole(w1), whole(w2)],
        out_specs=[pl.BlockSpec((1, N_CHUNK, NSA_KV), lambda b: (b, 0, 0)),
                   pl.BlockSpec((1, NSA_KV, N_CHUNK), lambda b: (b, 0, 0)),
                   pl.BlockSpec((1, NSA_KV, SEQ), lambda b: (b, 0, 0)),
                   pl.BlockSpec((1, NSA_KV, SEQ), lambda b: (b, 0, 0)),
                   pl.BlockSpec((1, N_KV, GATE_ROWS, SEQ), lambda b: (b, 0, 0, 0))],
        out_shape=[jax.ShapeDtypeStruct((BATCH, N_CHUNK, NSA_KV), BF16),
                   jax.ShapeDtypeStruct((BATCH, NSA_KV, N_CHUNK), BF16),
                   jax.ShapeDtypeStruct((BATCH, NSA_KV, SEQ), BF16),
                   jax.ShapeDtypeStruct((BATCH, NSA_KV, SEQ), BF16),
                   jax.ShapeDtypeStruct((BATCH, N_KV, GATE_ROWS, SEQ), F32)],
        scratch_shapes=[pltpu.VMEM((SEQ, NSA_KV), F32)],
        compiler_params=_cparams(("arbitrary",)),
        name="kvprep",
    )(kv, kv, kv, kv, ng, pe, w1, w2)


def _group_diag(w):
    eye = jnp.eye(N_KV, dtype=w.dtype)
    out = jnp.einsum('...gde,gh->...gdhe', w, eye)
    return out.reshape(w.shape[:-3] + (N_KV * w.shape[-2], N_KV * w.shape[-1]))


def _lane_tile(x, n):
    return jnp.concatenate([x] * n, axis=1)


ONES_ROWS = 16


def _with_ones(v_t):
    return jnp.concatenate([v_t, jnp.ones((ONES_ROWS, v_t.shape[1]), BF16)], axis=0)


def _nsa_kernel(q_ref, kc_ref, vct_ref, ks_ref, vst_ref, kw_ref, vwt_ref, gt_ref, ovt_ref, blk_ref,
                y_ref, sa_ref, sb_ref, oslc_ref):
    i = pl.program_id(1)
    tq = NSA_TQ
    nl = HPG * tq
    t_row = i * tq + lax.broadcasted_iota(jnp.int32, (1, tq), 1)
    groups = range(N_KV)
    vrows = [slice(g * HEAD_DIM, (g + 1) * HEAD_DIM) for g in groups]

    def scores(k, q_t):
        return jnp.dot(k, q_t, preferred_element_type=F32)

    def normalized(acc):
        return acc[:HEAD_DIM, :] / acc[HEAD_DIM:HEAD_DIM + 1, :]

    lane = lax.broadcasted_iota(jnp.int32, (tq, LANES), 1)
    scale = HEAD_DIM ** -0.5
    q4 = [jnp.concatenate(
        [(jnp.where((lane >= HEAD_DIM) == (g == 1),
                    q_ref[:, j * LANES:(j + 1) * LANES].astype(F32), 0.0) * scale).T
         for j in range(HPG)], axis=1).astype(BF16) for g in groups]

    n_idx = lax.broadcasted_iota(jnp.int32, (N_CHUNK, 1), 0)
    ok = n_idx * CMP_STRIDE + (CMP_LEN - 1) <= _lane_tile(t_row, HPG)
    blk = lax.broadcasted_iota(jnp.int32, (N_SLC, 1), 0)
    cur = jnp.right_shift(t_row, 6)
    forced = (blk == 0) | (blk == cur) | (blk == cur - 1)
    causal_blk = blk * SLC_LEN <= t_row
    o_cmp, q_aug = [], []
    for g in groups:
        s = jnp.where(ok, scores(kc_ref[0], q4[g]), NEG_INF)
        m = jnp.max(s, axis=0, keepdims=True)
        e = jnp.where(ok, jnp.exp(s - m), 0.0)
        l = jnp.sum(e, axis=0, keepdims=True)
        p = e / jnp.where(l > 0.0, l, 1.0)
        o_cmp.append(jnp.dot(vct_ref[0, vrows[g], :], p.astype(BF16), preferred_element_type=F32))
        psum = p[:, 0:tq]
        for j in range(1, HPG):
            psum = psum + p[:, j * tq:(j + 1) * tq]
        p_hi = psum.astype(BF16)
        p_lo = (psum - p_hi.astype(F32)).astype(BF16)
        imp = (jnp.dot(ovt_ref[...], p_hi, preferred_element_type=F32)
               + jnp.dot(ovt_ref[...], p_lo, preferred_element_type=F32))
        work = jnp.where(forced, -3e38, jnp.where(causal_blk, imp, -1.0))
        sel = jnp.where(forced, 1.0, 0.0)
        for _ in range(SLC_TOPN - MAX_FORCED):
            mx = jnp.max(work, axis=0, keepdims=True)
            idx = jnp.min(jnp.where(work == mx, blk, N_SLC), axis=0, keepdims=True)
            pick = blk == idx
            sel = jnp.where(pick & (mx >= 0.0), 1.0, sel)
            work = jnp.where(pick, -3e38, work)
        sel_bias = jnp.where(sel > 0.0, 0.0, NEG_INF)
        sel_bias = jnp.concatenate([sel_bias, jnp.zeros((LANES - N_SLC, tq), F32)], axis=0)
        sel_bias = _lane_tile(sel_bias.astype(BF16), HPG)
        q_aug.append(jnp.concatenate([q4[g], sel_bias], axis=0))

    start = pl.multiple_of(jnp.maximum(i - WIN // tq, 0) * tq, tq)
    k_win = kw_ref[pl.ds(start, WIN_KEYS), :]
    delta = t_row - (start + lax.broadcasted_iota(jnp.int32, (WIN_KEYS, 1), 0))
    win_bias = _lane_tile(jnp.where((delta >= 0) & (delta < WIN), 0.0, NEG_INF), HPG)
    o_win = []
    for g in groups:
        sT = scores(k_win, q4[g]) + win_bias
        pT = jnp.exp((sT - jnp.max(sT, axis=0, keepdims=True)).astype(BF16))
        vT = _with_ones(vwt_ref[vrows[g], pl.ds(start, WIN_KEYS)])
        o_win.append(normalized(jnp.dot(vT, pT, preferred_element_type=F32)))

    def slc_scores(kt, dst_ref):
        k0 = pl.multiple_of(kt * SLC_KT, SLC_KT)
        k_aug = jnp.concatenate([ks_ref[pl.ds(k0, SLC_KT), :], blk_ref[pl.ds(k0, SLC_KT), :]],
                                axis=1)
        maxima = []
        for g in groups:
            s_t = scores(k_aug, q_aug[g])
            dst_ref[g] = s_t
            maxima.append(jnp.max(s_t, axis=0, keepdims=True))
        return maxima

    def slc_update(kt, state, s_t, tile_max):
        k0 = pl.multiple_of(kt * SLC_KT, SLC_KT)
        out = []
        for g in groups:
            m_i, acc = state[2 * g], state[2 * g + 1]
            m_new = jnp.maximum(m_i, tile_max[g])
            alpha = jnp.exp(m_i - m_new)
            pT = jnp.exp((s_t[g] - m_new).astype(BF16))
            vT = _with_ones(vst_ref[vrows[g], pl.ds(k0, SLC_KT)])
            out += [m_new, alpha * acc + jnp.dot(vT, pT, preferred_element_type=F32)]
        return out

    def from_ref(src_ref):
        return [src_ref[g] for g in groups]

    def slc_pair(p, carry):
        state, max_a = list(carry[:2 * N_KV]), list(carry[2 * N_KV:])
        max_b = slc_scores(2 * p + 1, sb_ref)
        state = slc_update(2 * p, state, from_ref(sa_ref), max_a)
        max_a = slc_scores(2 * p + 2, sa_ref)
        state = slc_update(2 * p + 1, state, from_ref(sb_ref), max_b)
        return tuple(state + max_a)

    def slc_finish(state, src_ref):
        kpos = last_kt * SLC_KT + lax.broadcasted_iota(jnp.int32, (SLC_KT, 1), 0)
        causal_bias = _lane_tile(jnp.where(kpos <= t_row, 0.0, NEG_INF), HPG)
        s_last = [s + causal_bias for s in from_ref(src_ref)]
        state = slc_update(last_kt, state, s_last,
                           [jnp.max(s, axis=0, keepdims=True) for s in s_last])
        for g in groups:
            oslc_ref[g] = normalized(state[2 * g + 1])

    last_kt = (i * tq) // SLC_KT
    init = ((jnp.full((1, nl), -3e38, F32), jnp.zeros((HEAD_DIM + ONES_ROWS, nl), F32)) * N_KV
            + tuple(slc_scores(0, sa_ref)))
    carry = lax.fori_loop(0, last_kt // 2, slc_pair, init)
    state, max_a = list(carry[:2 * N_KV]), list(carry[2 * N_KV:])
    odd = lax.rem(last_kt, 2) == 1

    @pl.when(odd)
    def _():
        slc_scores(last_kt, sb_ref)
        slc_finish(slc_update(last_kt - 1, state, from_ref(sa_ref), max_a), sb_ref)

    @pl.when(jnp.logical_not(odd))
    def _():
        slc_finish(state, sa_ref)

    o_slc = [oslc_ref[g] for g in groups]

    for g in groups:
        gates = jax.nn.sigmoid(gt_ref[0, g])

        def gate_row(br):
            return jnp.concatenate([gates[3 * j + br:3 * j + br + 1, :] for j in range(HPG)],
                                   axis=1)

        o = gate_row(0) * o_cmp[g] + gate_row(1) * o_slc[g] + gate_row(2) * o_win[g]
        for j in range(HPG):
            h0 = (g * HPG + j) * HEAD_DIM
            y_ref[0, h0:h0 + HEAD_DIM, :] = o[:, j * tq:(j + 1) * tq].astype(y_ref.dtype)


def _block_onehot():
    oh = (np.arange(SEQ)[:, None] // SLC_LEN) == np.arange(LANES)[None, :]
    return jnp.asarray(oh.astype(np.float32), dtype=BF16)


def _nsa(q, kc, vct, kv, vst, vwt, gt, ovt):
    nq = SEQ // NSA_TQ
    return pl.pallas_call(
        _nsa_kernel,
        grid=(BATCH, nq),
        in_specs=[pl.BlockSpec((NSA_TQ, NSA_Q), lambda b, i: (b * nq + i, 0)),
                  pl.BlockSpec((1, N_CHUNK, NSA_KV), lambda b, i: (b, 0, 0)),
                  pl.BlockSpec((1, NSA_KV, N_CHUNK), lambda b, i: (b, 0, 0)),
                  pl.BlockSpec((SEQ, NSA_KV), lambda b, i: (b, 2)),
                  pl.BlockSpec((None, NSA_KV, SEQ), lambda b, i: (b, 0, 0)),
                  pl.BlockSpec((SEQ, NSA_KV), lambda b, i: (b, 4)),
                  pl.BlockSpec((None, NSA_KV, SEQ), lambda b, i: (b, 0, 0)),
                  pl.BlockSpec((1, N_KV, GATE_ROWS, NSA_TQ), lambda b, i: (b, 0, 0, i)),
                  pl.BlockSpec((N_SLC, N_CHUNK), lambda b, i: (0, 0)),
                  pl.BlockSpec((SEQ, LANES), lambda b, i: (0, 0))],
        out_specs=pl.BlockSpec((1, NSA_Q, NSA_TQ), lambda b, i: (b, 0, i)),
        out_shape=jax.ShapeDtypeStruct((BATCH, NSA_Q, SEQ), BF16),
        scratch_shapes=[pltpu.VMEM((N_KV, SLC_KT, HPG * NSA_TQ), F32),
                        pltpu.VMEM((N_KV, SLC_KT, HPG * NSA_TQ), F32),
                        pltpu.VMEM((N_KV, HEAD_DIM, HPG * NSA_TQ), F32)],
        compiler_params=_cparams(("arbitrary", "arbitrary")),
        name="nsa",
    )(q, kc, vct, kv, vst, kv, vwt, gt, ovt, _block_onehot())


def _overlap_t():
    c_start = np.arange(N_CHUNK) * CMP_STRIDE
    s_start = np.arange(N_SLC) * SLC_LEN
    ov = ((c_start[None, :] <= s_start[:, None] + SLC_LEN - 1)
          & (c_start[None, :] + CMP_LEN - 1 >= s_start[:, None])
          & (np.arange(N_CHUNK)[None, :] < N_CMP))
    return jnp.asarray(ov.astype(np.float32), dtype=BF16)


def _merge_kernel(x_ref, g_ref, wmg_ref, ya_ref, yb_ref, yct_ref, yd_ref, wb_ref, wo_ref, o_ref):
    x = x_ref[...]
    nb = _rms(x, g_ref[...]).astype(BF16)
    yc = yct_ref[0].astype(F32).T.astype(BF16)
    ys = (ya_ref[...], yb_ref[...], yc, yd_ref[...])
    merged = jnp.zeros((MERGE_TM, D_MODEL), F32)
    for bi, y in enumerate(ys):
        gate = jax.nn.sigmoid(jnp.dot(nb, wmg_ref[:, bi * D_MODEL:(bi + 1) * D_MODEL],
                                      preferred_element_type=F32))
        merged = merged + gate * jnp.dot(y, wb_ref[bi], preferred_element_type=F32)
    o_ref[...] = x + jnp.dot(merged.astype(BF16), wo_ref[...], preferred_element_type=F32)


def _merge(x, g, wmg, ya, yb, yct, yd, wb, wo, layer):
    nt = SEQ // MERGE_TM
    tok = lambda i: (i, 0)
    const2 = lambda i: (0, 0)
    return pl.pallas_call(
        _merge_kernel,
        grid=(TOKENS // MERGE_TM,),
        in_specs=[pl.BlockSpec((MERGE_TM, D_MODEL), tok),
                  pl.BlockSpec((1, D_MODEL), const2),
                  pl.BlockSpec((None, D_MODEL, N_BRANCH * D_MODEL), lambda i: (layer, 0, 0)),
                  pl.BlockSpec((MERGE_TM, MIX), tok),
                  pl.BlockSpec((MERGE_TM, MIX), tok),
                  pl.BlockSpec((1, MIX, MERGE_TM), lambda i: (i // nt, 0, i % nt)),
                  pl.BlockSpec((MERGE_TM, MIX), tok),
                  pl.BlockSpec((None, N_BRANCH, MIX, D_MODEL), lambda i: (layer, 0, 0, 0)),
                  pl.BlockSpec((None, D_MODEL, D_MODEL), lambda i: (layer, 0, 0))],
        out_specs=pl.BlockSpec((MERGE_TM, D_MODEL), tok),
        out_shape=jax.ShapeDtypeStruct((TOKENS, D_MODEL), F32),
        compiler_params=_cparams(("arbitrary",)),
        name="merge",
    )(x, g, wmg, ya, yb, yct, yd, wb, wo)


def _router_logits(t, wr_ref, br_ref):
    w = wr_ref[...]
    t_hi, w_hi = t.astype(BF16), w.astype(BF16)
    t_lo = (t - t_hi.astype(F32)).astype(BF16)
    w_lo = (w - w_hi.astype(F32)).astype(BF16)
    dot = functools.partial(jnp.dot, preferred_element_type=F32)
    return dot(t_hi, w_hi) + dot(t_hi, w_lo) + dot(t_lo, w_hi) + br_ref[...]


def _top_group(logits):
    lane = lax.broadcasted_iota(jnp.int32, logits.shape, 1)
    is_grp = lane < N_GROUPS
    lg = jnp.where(is_grp, logits, NEG_INF)
    gmax = jnp.max(lg, axis=1, keepdims=True)
    grp_idx = jnp.min(jnp.where(is_grp & (lg == gmax), lane, ROUTER_PAD), axis=1, keepdims=True)
    return lg, gmax, grp_idx


def _group_weight(logits, grp_idx):
    lane = lax.broadcasted_iota(jnp.int32, logits.shape, 1)
    lg, gmax, _ = _top_group(logits)
    ge = jnp.where(lane < N_GROUPS, jnp.exp(lg - gmax), 0.0)
    return (jnp.sum(jnp.where(lane == grp_idx, ge, 0.0), axis=1, keepdims=True)
            / jnp.sum(ge, axis=1, keepdims=True))


ROUTE_ROWS = 32


def _route_kernel(x_ref, g_ref, wr_ref, br_ref, tri_ref, meta_ref, cnt_ref, run_ref):
    @pl.when(pl.program_id(0) == 0)
    def _():
        run_ref[...] = jnp.zeros_like(run_ref)

    logits = _router_logits(_rms(x_ref[...], g_ref[...]), wr_ref, br_ref)
    lt = logits.T[:ROUTE_ROWS, :]
    row = lax.broadcasted_iota(jnp.int32, (ROUTE_ROWS, 1), 0)

    def top(mask, vals):
        v = jnp.where(mask, vals, NEG_INF)
        best = jnp.max(v, axis=0, keepdims=True)
        return v, jnp.min(jnp.where(mask & (v == best), row, ROUTE_ROWS), axis=0, keepdims=True)

    _, grp_idx = top(row < N_GROUPS, lt)
    first = N_GROUPS + grp_idx * EXPERTS_PER_GROUP
    in_grp = (row >= first) & (row < first + EXPERTS_PER_GROUP)
    le, i1 = top(in_grp, lt)
    _, i2 = top(in_grp & (row != i1), le)
    lo = jnp.minimum(i1, i2) - first
    hi = jnp.maximum(i1, i2) - first
    pair = jnp.right_shift(lo * (2 * EXPERTS_PER_GROUP - 1 - lo), 1) + hi - lo - 1
    cls = grp_idx * N_PAIRS + pair
    onehot = jnp.where(row == cls, 1.0, 0.0)
    before = jnp.dot(onehot.astype(BF16), tri_ref[...], preferred_element_type=F32) + run_ref[...]
    rank = jnp.sum(onehot * before, axis=0, keepdims=True).astype(jnp.int32)
    run_ref[...] += jnp.sum(onehot, axis=1, keepdims=True)
    sub = lax.broadcasted_iota(jnp.int32, (SUBLANES, 1), 0)
    meta_ref[...] = jnp.where(sub == 0, cls, jnp.where(sub == 1, rank, 0))
    cnt_ref[...] = jnp.broadcast_to(run_ref[...], cnt_ref.shape)


def _route(x, g, wr, br):
    tok = lambda i: (i, 0)
    const2 = lambda i: (0, 0)
    tri = jnp.asarray(np.triu(np.ones((ROUTE_TM, ROUTE_TM), np.float32), 1), dtype=BF16)
    return pl.pallas_call(
        _route_kernel,
        grid=(TOKENS // ROUTE_TM,),
        in_specs=[pl.BlockSpec((ROUTE_TM, D_MODEL), tok),
                  pl.BlockSpec((1, D_MODEL), const2),
                  pl.BlockSpec((D_MODEL, ROUTER_PAD), const2),
                  pl.BlockSpec((1, ROUTER_PAD), const2),
                  pl.BlockSpec((ROUTE_TM, ROUTE_TM), const2)],
        out_specs=[pl.BlockSpec((SUBLANES, ROUTE_TM), lambda i: (0, i)),
                   pl.BlockSpec((ROUTE_ROWS, LANES), const2)],
        out_shape=[jax.ShapeDtypeStruct((SUBLANES, TOKENS), jnp.int32),
                   jax.ShapeDtypeStruct((ROUTE_ROWS, LANES), F32)],
        scratch_shapes=[pltpu.VMEM((ROUTE_ROWS, 1), F32)],
        compiler_params=_cparams(("arbitrary",)),
        name="route",
    )(x, g, wr, br, tri)


MOVE_SLOTS = 2


def _move_rows_kernel(idx_ref, seg_ref, len_ref, src_hbm, dst_hbm, buf, zbuf, in_sem, out_sem,
                      pad_sem, *, scatter):
    c = pl.program_id(0)
    n = pl.num_programs(0)
    rc = ROW_CHUNK
    chunk = lambda k: pl.ds(k * rc, rc)

    if scatter:
        def read(k, sl):
            return [pltpu.make_async_copy(src_hbm.at[chunk(k)], buf.at[sl], in_sem.at[sl])]

        def write(k, sl):
            return [pltpu.make_async_copy(buf.at[sl, pl.ds(r, 1)],
                                          dst_hbm.at[pl.ds(idx_ref[k * rc + r], 1)], out_sem.at[sl])
                    for r in range(rc)]

        def write_done(sl):
            return pltpu.make_async_copy(buf.at[sl], dst_hbm.at[chunk(0)], out_sem.at[sl])

        def read_done(sl):
            return read(0, sl)[0]
    else:
        def read(k, sl):
            return [pltpu.make_async_copy(src_hbm.at[pl.ds(idx_ref[k * rc + r], 1)],
                                          buf.at[sl, pl.ds(r, 1)], in_sem.at[sl])
                    for r in range(rc)]

        def write(k, sl):
            return [pltpu.make_async_copy(buf.at[sl], dst_hbm.at[chunk(k)], out_sem.at[sl])]

        def write_done(sl):
            return write(0, sl)[0]

        def read_done(sl):
            return pltpu.make_async_copy(src_hbm.at[chunk(0)], buf.at[sl], in_sem.at[sl])

    def pad_fills(fn):
        for sg in range(N_CLASSES):
            start, length = seg_ref[sg], len_ref[sg]
            head = jnp.bitwise_and(-start, SUBLANES - 1)
            for r in range(SUBLANES - 1):
                @pl.when(r < head)
                def _():
                    fn(pltpu.make_async_copy(zbuf.at[pl.ds(0, 1)], dst_hbm.at[pl.ds(start + r, 1)],
                                             pad_sem))
            body = length - head
            bit = rc // 2
            while bit >= SUBLANES:
                done = body - jnp.bitwise_and(body, 2 * bit - 1)
                first = pl.multiple_of(start + head + done, SUBLANES)

                @pl.when(jnp.bitwise_and(body, bit) != 0)
                def _():
                    fn(pltpu.make_async_copy(zbuf.at[pl.ds(0, bit)], dst_hbm.at[pl.ds(first, bit)],
                                             pad_sem))
                bit //= 2
        tail = pl.multiple_of(seg_ref[N_CLASSES], rc)

        def tail_tile(k, carry):
            fn(pltpu.make_async_copy(zbuf, dst_hbm.at[pl.ds(tail + k * rc, rc)], pad_sem))
            return carry
        lax.fori_loop(0, len_ref[N_CLASSES] // rc, tail_tile, 0)

    @pl.when(c == 0)
    def _():
        for sl in range(MOVE_SLOTS):
            for cp in read(sl, sl):
                cp.start()
        if scatter:
            zbuf[...] = jnp.zeros_like(zbuf)
            pad_fills(lambda cp: cp.start())

    for sl in range(MOVE_SLOTS):
        read_done(sl).wait()
        for cp in write(MOVE_SLOTS * c + sl, sl):
            cp.start()

    @pl.when(c + 1 < n)
    def _():
        for sl in range(MOVE_SLOTS):
            write_done(sl).wait()
            for cp in read(MOVE_SLOTS * (c + 1) + sl, sl):
                cp.start()

    @pl.when(c == n - 1)
    def _():
        for sl in range(MOVE_SLOTS):
            write_done(sl).wait()
        if scatter:
            pad_fills(lambda cp: cp.wait())


def _move_rows(idx, seg_start, seg_len, src, n_out, scatter):
    any_spec = pl.BlockSpec(memory_space=pl.ANY)
    grid_spec = pltpu.PrefetchScalarGridSpec(
        num_scalar_prefetch=3, grid=(TOKENS // (MOVE_SLOTS * ROW_CHUNK),), in_specs=[any_spec],
        out_specs=any_spec,
        scratch_shapes=[pltpu.VMEM((MOVE_SLOTS, ROW_CHUNK, D_MODEL), F32),
                        pltpu.VMEM((ROW_CHUNK, D_MODEL), F32),
                        pltpu.SemaphoreType.DMA((MOVE_SLOTS,)),
                        pltpu.SemaphoreType.DMA((MOVE_SLOTS,)),
                        pltpu.SemaphoreType.DMA(())])
    return pl.pallas_call(
        functools.partial(_move_rows_kernel, scatter=scatter),
        grid_spec=grid_spec,
        out_shape=jax.ShapeDtypeStruct((n_out, D_MODEL), F32),
        compiler_params=_cparams(("arbitrary",)),
        name="dispatch" if scatter else "collect",
    )(idx, seg_start, seg_len, src)


def _experts_kernel(ea_ref, eb_ref, tv_ref, tb_ref, x_ref, g_ref, wr_ref, br_ref, wga_ref, wua_ref,
                    wda_ref, wgb_ref, wub_ref, wdb_ref, fg_ref, o_ref, *, layer, final_norm):
    j = pl.program_id(0)
    nv = tv_ref[j]

    @pl.when(nv > 0)
    def _():
        x = x_ref[...]
        t = _rms(x, g_ref[...])
        logits = _router_logits(t, wr_ref, br_ref)
        lane = lax.broadcasted_iota(jnp.int32, logits.shape, 1)
        ea = ea_ref[j] - layer * N_EXPERTS
        eb = eb_ref[j] - layer * N_EXPERTS
        la = jnp.sum(jnp.where(lane == N_GROUPS + ea, logits, 0.0), axis=1, keepdims=True)
        lb = jnp.sum(jnp.where(lane == N_GROUPS + eb, logits, 0.0), axis=1, keepdims=True)
        top = jnp.maximum(la, lb)
        pa, pb = jnp.exp(la - top), jnp.exp(lb - top)
        scale = _group_weight(logits, jnp.right_shift(ea, 2)) / (pa + pb)
        tb = t.astype(BF16)
        acc = jnp.zeros((MOE_TM, D_MODEL), F32)
        for wg, wu, wd, w in ((wga_ref, wua_ref, wda_ref, pa * scale),
                              (wgb_ref, wub_ref, wdb_ref, pb * scale)):
            hid = (jax.nn.silu(jnp.dot(tb, wg[0], preferred_element_type=F32))
                   * jnp.dot(tb, wu[0], preferred_element_type=F32)) * w
            acc = acc + jnp.dot(hid.astype(BF16), wd[0], preferred_element_type=F32)
        h = x + acc
        if final_norm:
            h = _rms(h, fg_ref[...])
        o_ref[...] = h

    @pl.when(nv == 0)
    def _():
        o_ref[...] = jnp.zeros_like(o_ref)


def _experts(tile_ea, tile_eb, tile_valid, tile_block, hs, g, wr, br, wg, wu, wd, fg, layer,
             final_norm):
    const2 = lambda j, *_: (0, 0)
    of_a = lambda j, ea, eb, tv, tb: (ea[j], 0, 0)
    of_b = lambda j, ea, eb, tv, tb: (eb[j], 0, 0)
    up = lambda idx: pl.BlockSpec((1, D_MODEL, D_EXPERT), idx)
    down = lambda idx: pl.BlockSpec((1, D_EXPERT, D_MODEL), idx)
    grid_spec = pltpu.PrefetchScalarGridSpec(
        num_scalar_prefetch=4,
        grid=(MOE_NT,),
        in_specs=[pl.BlockSpec((MOE_TM, D_MODEL), lambda j, ea, eb, tv, tb: (tb[j], 0)),
                  pl.BlockSpec((1, D_MODEL), const2),
                  pl.BlockSpec((D_MODEL, ROUTER_PAD), const2),
                  pl.BlockSpec((1, ROUTER_PAD), const2),
                  up(of_a), up(of_a), down(of_a), up(of_b), up(of_b), down(of_b),
                  pl.BlockSpec((1, D_MODEL), const2)],
        out_specs=pl.BlockSpec((MOE_TM, D_MODEL), lambda j, *_: (j, 0)))
    return pl.pallas_call(
        functools.partial(_experts_kernel, layer=layer, final_norm=final_norm),
        grid_spec=grid_spec,
        out_shape=jax.ShapeDtypeStruct((MOE_ROWS, D_MODEL), F32),
        compiler_params=_cparams(("arbitrary",)),
        name="experts",
    )(tile_ea, tile_eb, tile_valid, tile_block, hs, g, wr, br, wg, wu, wd, wg, wu, wd, fg)


_PAIR_LO = np.array([a for a in range(EXPERTS_PER_GROUP) for b in range(a + 1, EXPERTS_PER_GROUP)])
_PAIR_HI = np.array([b for a in range(EXPERTS_PER_GROUP) for b in range(a + 1, EXPERTS_PER_GROUP)])


def _moe(x, g, wr, br, wg, wu, wd, fg, layer, final_norm):
    meta, cnt = _route(x, g, wr, br)
    cls, rank = meta[0], meta[1]
    counts = cnt[:N_CLASSES, 0].astype(jnp.int32)
    padded = (counts + MOE_TM - 1) // MOE_TM * MOE_TM
    ends = jnp.cumsum(padded)
    starts = ends - padded
    pos = starts[cls] + rank
    tile_start = jnp.arange(MOE_NT, dtype=jnp.int32) * MOE_TM
    tile_cls = jnp.minimum(jnp.sum(tile_start[:, None] >= ends[None, :], axis=1), N_CLASSES - 1)
    tile_valid = jnp.clip(starts[tile_cls] + counts[tile_cls] - tile_start, 0, MOE_TM)
    tile_valid = tile_valid.astype(jnp.int32)
    tile_block = jnp.where(tile_valid > 0, jnp.arange(MOE_NT, dtype=jnp.int32), 0)
    first = layer * N_EXPERTS + (tile_cls // N_PAIRS) * EXPERTS_PER_GROUP
    tile_ea = (first + jnp.asarray(_PAIR_LO)[tile_cls % N_PAIRS]).astype(jnp.int32)
    tile_eb = (first + jnp.asarray(_PAIR_HI)[tile_cls % N_PAIRS]).astype(jnp.int32)
    seg_start = jnp.concatenate([starts + counts, ends[-1:]])
    seg_len = jnp.concatenate([padded - counts, MOE_ROWS - ends[-1:]])
    hs = _move_rows(pos, seg_start, seg_len, x, MOE_ROWS, scatter=True)
    ys = _experts(tile_ea, tile_eb, tile_valid, tile_block, hs, g, wr, br, wg, wu, wd, fg, layer,
                  final_norm)
    return _move_rows(pos, seg_start, seg_len, ys, TOKENS, scatter=False)


def _block_diag(w):
    eye = jnp.eye(LRU_BLOCKS, dtype=w.dtype)
    return jnp.einsum('hij,hk->hikj', w, eye).reshape(MIX, MIX)


def _w_in_column_ranges():
    cuts = [int(c) for c in np.cumsum((0,) + IN_SPLITS)]
    q_parts = [(cuts[4] + (g * HPG + j) * HEAD_DIM, cuts[4] + (g * HPG + j + 1) * HEAD_DIM)
               for j in range(HPG) for g in range(N_KV)]
    return ([(cuts[8], cuts[9]), (cuts[0], cuts[4])] + q_parts
            + [(cuts[5], cuts[6]), (cuts[7], cuts[8]), (cuts[6], cuts[7])])


def _prep_w_in_kernel(w_ref, o_ref):
    off = 0
    for a, b in _w_in_column_ranges():
        o_ref[:, off:off + b - a] = w_ref[:, a:b].astype(o_ref.dtype)
        off += b - a
    o_ref[:, off:] = jnp.zeros((o_ref.shape[0], o_ref.shape[1] - off), o_ref.dtype)


def _prep_w_in(w_in):
    n_in = w_in.shape[-1]
    return pl.pallas_call(
        _prep_w_in_kernel,
        grid=(DEPTH, D_MODEL // PREP_ROWS),
        in_specs=[pl.BlockSpec((None, PREP_ROWS, n_in), lambda l, r: (l, r, 0))],
        out_specs=pl.BlockSpec((None, PREP_ROWS, 2 * W_IN_HALF), lambda l, r: (l, r, 0)),
        out_shape=jax.ShapeDtypeStruct((DEPTH, D_MODEL, 2 * W_IN_HALF), BF16),
        compiler_params=_cparams(("arbitrary", "arbitrary")),
        name="prep_w_in",
    )(w_in)


def _nsa_mixer(q, kv, ng, p):
    w1 = p['cmp_w1'].reshape(2, N_KV, CMP_LEN, HEAD_DIM, HEAD_DIM).transpose(0, 2, 1, 3, 4)
    pe = jnp.concatenate([p['cmp_pe']] * N_KV, axis=-1)
    kc, vct, vst, vwt, gt = _kvprep(kv, ng, pe, _group_diag(w1).astype(BF16),
                                    _group_diag(p['cmp_w2']).astype(BF16))
    return _nsa(q, kc, vct, kv, vst, vwt, gt, _overlap_t())


def _layer(h, p, big, layer, final_g, final_norm):
    row = lambda a: a.reshape(1, -1)

    u, v, gb, rb, q, kv, xd, ng = _proj(h, row(p['norm1_g']), big['w_in'], layer)

    bs = jnp.broadcast_to(p['gm_b'][:, :, None], (GM_GROUPS, GM_CHUNK, GM_GW))
    y_a = _gmlp(u, v, row(p['gm_norm_g']), p['gm_ws'], bs)

    y_b = _lru(gb, rb, p['conv_w'], row(p['conv_b']), _block_diag(p['lru_wa']).astype(BF16),
               row(p['lru_ba']), _block_diag(p['lru_wx']).astype(BF16), row(p['lru_bx']),
               row(p['lru_lambda']))

    y_d = _pool(xd, p['pool_w'].astype(BF16), row(p['pool_scale']))

    y_ct = _nsa_mixer(q, kv, ng, p)

    h = _merge(h, row(p['norm1_g']), big['w_in'], y_a, y_b, y_ct, y_d, big['w_branch'],
               big['w_out'], layer)

    wr = jnp.concatenate([p['router_w_group'], p['router_w_expert']], axis=1)
    wr = jnp.pad(wr, ((0, 0), (0, ROUTER_PAD - wr.shape[1])))
    br = jnp.concatenate([p['router_b_group'], p['router_b_expert']])
    br = jnp.pad(br, (0, ROUTER_PAD - br.shape[0])).reshape(1, ROUTER_PAD)
    return _moe(h, row(p['norm2_g']), wr, br, big['moe_w_gate'], big['moe_w_up'],
                big['moe_w_down'], row(final_g), layer, final_norm)


_LAYER_PARAMS = ('norm1_g', 'gm_norm_g', 'gm_ws', 'gm_b', 'conv_w', 'conv_b', 'lru_wa',
                 'lru_ba', 'lru_wx', 'lru_bx', 'lru_lambda', 'cmp_pe', 'cmp_w1', 'cmp_w2', 'pool_w',
                 'pool_scale', 'norm2_g', 'router_w_group', 'router_b_group',
                 'router_w_expert', 'router_b_expert')


def kernel(x, norm1_g, w_in, gm_norm_g, gm_ws, gm_b, conv_w, conv_b, lru_wa, lru_ba, lru_wx,
           lru_bx, lru_lambda, cmp_pe, cmp_w1, cmp_w2, pool_w, pool_scale, w_branch, w_out,
           norm2_g, router_w_group, router_b_group, router_w_expert, router_b_expert,
           moe_w_gate, moe_w_up, moe_w_down, final_norm_g):
    stacked = dict(zip(_LAYER_PARAMS, (
        norm1_g, gm_norm_g, gm_ws, gm_b, conv_w, conv_b, lru_wa, lru_ba, lru_wx, lru_bx,
        lru_lambda, cmp_pe, cmp_w1, cmp_w2, pool_w, pool_scale, norm2_g,
        router_w_group, router_b_group, router_w_expert, router_b_expert)))
    w_all = _prep_w_in(w_in)
    experts = lambda w: w.astype(BF16).reshape((DEPTH * N_EXPERTS,) + w.shape[2:])
    big = dict(w_in=w_all, w_branch=w_branch.astype(BF16), w_out=w_out.astype(BF16),
               moe_w_gate=experts(moe_w_gate), moe_w_up=experts(moe_w_up),
               moe_w_down=experts(moe_w_down))
    h = x.reshape(TOKENS, D_MODEL)
    for layer in range(DEPTH):
        p = {k: a[layer] for k, a in stacked.items()}
        h = _layer(h, p, big, layer, final_norm_g, final_norm=(layer == DEPTH - 1))
    return h.reshape(BATCH, SEQ, D_MODEL)
```

```python
import functools

import numpy as np
import jax
import jax.numpy as jnp
from jax import lax
from jax.experimental import pallas as pl
from jax.experimental.pallas import tpu as pltpu

F32 = jnp.float32
BF16 = jnp.bfloat16

D_MODEL = 1024
BATCH = 4
SEQ = 4096
TOKENS = BATCH * SEQ
DEPTH = 2
MIX = D_MODEL // 2
GM_CHUNK = 128
GM_GROUPS = 4
GM_GW = MIX // GM_GROUPS
CONV_WIDTH = 4
LRU_BLOCKS = 8
LRU_BW = MIX // LRU_BLOCKS
LRU_C = 8.0
N_HEADS = 8
HEAD_DIM = MIX // N_HEADS
N_KV = 2
HPG = N_HEADS // N_KV
CMP_LEN = 32
CMP_STRIDE = 16
SLC_LEN = 64
SLC_TOPN = 8
MAX_FORCED = 3
WIN = 512
NSA_Q = N_HEADS * HEAD_DIM
NSA_KV = N_KV * HEAD_DIM
POOL_WINDOWS = (2, 4, 8, 16)
POOL_GW = MIX // len(POOL_WINDOWS)
N_BRANCH = 4
N_GROUPS = 4
EXPERTS_PER_GROUP = 4
N_EXPERTS = N_GROUPS * EXPERTS_PER_GROUP
D_EXPERT = D_MODEL // 2
EPS = 1e-6
NEG_INF = -1e30
FORCE_SCORE = 1e6
IN_SPLITS = (MIX, MIX, MIX, MIX, NSA_Q, 6 * NSA_KV, 3 * N_HEADS, MIX, N_BRANCH * D_MODEL)

N_CHUNK = SEQ // CMP_STRIDE
N_CMP = N_CHUNK - CMP_LEN // CMP_STRIDE + 1
N_SLC = SEQ // SLC_LEN

LANES = 128
SUBLANES = 8
GATE_PAD = LANES
GATE_ROWS = 16
ROUTER_PAD = LANES
VMEM_LIMIT = 56 * 1024 * 1024

PROJ_WIDTHS = (MIX, MIX, MIX, MIX, NSA_Q, 6 * NSA_KV, MIX, GATE_PAD)
W_IN_HALF = N_BRANCH * D_MODEL
PROJ_TM = 512
PREP_ROWS = 256
GMLP_TM = 512
LRU_TS = 512
POOL_TS = 512
NSA_TQ = 256
SLC_KT = 256
WIN_KEYS = WIN + NSA_TQ
MERGE_TM = 256
ROUTE_TM = 512
MOE_TM = 256
N_PAIRS = EXPERTS_PER_GROUP * (EXPERTS_PER_GROUP - 1) // 2
N_CLASSES = N_GROUPS * N_PAIRS
MOE_ROWS = TOKENS + N_CLASSES * MOE_TM
MOE_NT = MOE_ROWS // MOE_TM
ROW_CHUNK = 256


def _cparams(sem):
    return pltpu.CompilerParams(dimension_semantics=sem, vmem_limit_bytes=VMEM_LIMIT)


def _rms(x, g):
    return x * lax.rsqrt(jnp.mean(x * x, axis=-1, keepdims=True) + EPS) * g


def _proj_kernel(x_ref, g_ref, w_ref, *out_refs):
    nb = _rms(x_ref[...], g_ref[...]).astype(BF16)
    off = 0
    for ref in out_refs:
        w = ref.shape[-1]
        ref[...] = jnp.dot(nb, w_ref[:, off:off + w], preferred_element_type=F32).astype(ref.dtype)
        off += w


def _proj(x, g, w, layer):
    out_shape = [jax.ShapeDtypeStruct((TOKENS, wd), BF16) for wd in PROJ_WIDTHS[:-1]]
    out_shape.append(jax.ShapeDtypeStruct((TOKENS, GATE_PAD), F32))
    return pl.pallas_call(
        _proj_kernel,
        grid=(TOKENS // PROJ_TM,),
        in_specs=[pl.BlockSpec((PROJ_TM, D_MODEL), lambda i: (i, 0)),
                  pl.BlockSpec((1, D_MODEL), lambda i: (0, 0)),
                  pl.BlockSpec((None, D_MODEL, W_IN_HALF), lambda i: (layer, 0, 1))],
        out_specs=[pl.BlockSpec((PROJ_TM, wd), lambda i: (i, 0)) for wd in PROJ_WIDTHS],
        out_shape=out_shape,
        compiler_params=_cparams(("arbitrary",)),
        name="proj",
    )(x, g, w)


def _gmlp_kernel(u_ref, v_ref, g_ref, ws_ref, bs_ref, o_ref):
    u = jax.nn.gelu(u_ref[...].astype(F32))
    v = _rms(jax.nn.gelu(v_ref[...].astype(F32)), g_ref[...]).astype(BF16)
    row = lax.broadcasted_iota(jnp.int32, (GM_CHUNK, GM_CHUNK), 0)
    col = lax.broadcasted_iota(jnp.int32, (GM_CHUNK, GM_CHUNK), 1)
    causal = row >= col
    for gi in range(GM_GROUPS):
        w = jnp.where(causal, ws_ref[gi], 0.0).astype(BF16)
        cs = slice(gi * GM_GW, (gi + 1) * GM_GW)
        for c in range(GMLP_TM // GM_CHUNK):
            rs = slice(c * GM_CHUNK, (c + 1) * GM_CHUNK)
            mixed = jnp.dot(w, v[rs, cs], preferred_element_type=F32) + bs_ref[gi]
            o_ref[rs, cs] = (u[rs, cs] * mixed).astype(o_ref.dtype)


def _gmlp(u, v, g, ws, bs):
    tok = lambda i: (i, 0)
    return pl.pallas_call(
        _gmlp_kernel,
        grid=(TOKENS // GMLP_TM,),
        in_specs=[pl.BlockSpec((GMLP_TM, MIX), tok),
                  pl.BlockSpec((GMLP_TM, MIX), tok),
                  pl.BlockSpec((1, MIX), lambda i: (0, 0)),
                  pl.BlockSpec((GM_GROUPS, GM_CHUNK, GM_CHUNK), lambda i: (0, 0, 0)),
                  pl.BlockSpec((GM_GROUPS, GM_CHUNK, GM_GW), lambda i: (0, 0, 0))],
        out_specs=pl.BlockSpec((GMLP_TM, MIX), tok),
        out_shape=jax.ShapeDtypeStruct((TOKENS, MIX), BF16),
        compiler_params=_cparams(("arbitrary",)),
        name="gmlp",
    )(u, v, g, ws, bs)


LRU_TAIL = 8


def _lru_kernel(gb_ref, rb_ref, cw_ref, cb_ref, wa_ref, ba_ref, wx_ref, bx_ref, lam_ref, o_ref,
                tail_ref, h_ref):
    @pl.when(pl.program_id(1) == 0)
    def _():
        tail_ref[...] = jnp.zeros_like(tail_ref)
        h_ref[...] = jnp.zeros_like(h_ref)

    ts = LRU_TS
    x = rb_ref[...].astype(F32)
    ext = jnp.concatenate([tail_ref[...], x], axis=0)
    tail_ref[...] = x[ts - LRU_TAIL:, :]
    xc = cb_ref[...] + x * cw_ref[CONV_WIDTH - 1:CONV_WIDTH, :]
    for d in range(1, CONV_WIDTH):
        xs = pltpu.roll(ext, d, axis=0)[LRU_TAIL:, :]
        xc = xc + xs * cw_ref[CONV_WIDTH - 1 - d:CONV_WIDTH - d, :]
    xcb = xc.astype(BF16)
    r = jax.nn.sigmoid(jnp.dot(xcb, wa_ref[...], preferred_element_type=F32) + ba_ref[...])
    ig = jax.nn.sigmoid(jnp.dot(xcb, wx_ref[...], preferred_element_type=F32) + bx_ref[...])
    z = -lam_ref[...]
    softplus = jnp.maximum(z, 0.0) + jnp.log1p(jnp.exp(-jnp.abs(z)))
    log_a = -LRU_C * r * softplus
    a = jnp.exp(log_a)
    b = jnp.sqrt(1.0 - jnp.exp(2.0 * log_a)) * (ig * xc)
    rows = lax.broadcasted_iota(jnp.int32, (ts, 1), 0)
    d = 1
    while d < ts:
        valid = rows >= d
        a_prev = pltpu.roll(a, d, axis=0)
        b_prev = pltpu.roll(b, d, axis=0)
        b = jnp.where(valid, a * b_prev, 0.0) + b
        a = jnp.where(valid, a * a_prev, a)
        d *= 2
    h = a * h_ref[...] + b
    h_ref[...] = h[ts - 1:ts, :]
    o_ref[...] = (jax.nn.gelu(gb_ref[...].astype(F32)) * h).astype(o_ref.dtype)


def _lru(gb, rb, cw, cb, wa, ba, wx, bx, lam):
    nt = SEQ // LRU_TS
    tok = lambda b, s: (b * nt + s, 0)
    vec = pl.BlockSpec((1, MIX), lambda b, s: (0, 0))
    mat = pl.BlockSpec((MIX, MIX), lambda b, s: (0, 0))
    return pl.pallas_call(
        _lru_kernel,
        grid=(BATCH, nt),
        in_specs=[pl.BlockSpec((LRU_TS, MIX), tok), pl.BlockSpec((LRU_TS, MIX), tok),
                  pl.BlockSpec((CONV_WIDTH, MIX), lambda b, s: (0, 0)), vec,
                  mat, vec, mat, vec, vec],
        out_specs=pl.BlockSpec((LRU_TS, MIX), tok),
        out_shape=jax.ShapeDtypeStruct((TOKENS, MIX), BF16),
        scratch_shapes=[pltpu.VMEM((LRU_TAIL, MIX), F32), pltpu.VMEM((1, MIX), F32)],
        compiler_params=_cparams(("arbitrary", "arbitrary")),
        name="rglru",
    )(gb, rb, cw, cb, wa, ba, wx, bx, lam)


POOL_TAIL = 16


def _pool_kernel(x_ref, w_ref, sc_ref, o_ref, tail_ref):
    s_id = pl.program_id(1)

    @pl.when(s_id == 0)
    def _():
        tail_ref[...] = jnp.zeros_like(tail_ref)

    ts = POOL_TS
    x = x_ref[...].astype(F32)
    ext = jnp.concatenate([tail_ref[...], x], axis=0)
    tail_ref[...] = x[ts - POOL_TAIL:, :]
    pos = s_id * ts + lax.broadcasted_iota(jnp.int32, (ts, 1), 0)
    acc = ext
    width = 1
    for gi, wdw in enumerate(POOL_WINDOWS):
        while width < wdw:
            acc = acc + pltpu.roll(acc, width, axis=0)
            width *= 2
        cs = slice(gi * POOL_GW, (gi + 1) * POOL_GW)
        cnt = jnp.minimum(pos + 1, wdw).astype(F32)
        pooled = acc[POOL_TAIL:, cs] / cnt - x[:, cs]
        mixed = jnp.dot(pooled.astype(BF16), w_ref[gi], preferred_element_type=F32)
        o_ref[:, cs] = (mixed * sc_ref[:, cs]).astype(o_ref.dtype)


def _pool(xd, w, sc):
    nt = SEQ // POOL_TS
    tok = lambda b, s: (b * nt + s, 0)
    return pl.pallas_call(
        _pool_kernel,
        grid=(BATCH, nt),
        in_specs=[pl.BlockSpec((POOL_TS, MIX), tok),
                  pl.BlockSpec((len(POOL_WINDOWS), POOL_GW, POOL_GW), lambda b, s: (0, 0, 0)),
                  pl.BlockSpec((1, MIX), lambda b, s: (0, 0))],
        out_specs=pl.BlockSpec((POOL_TS, MIX), tok),
        out_shape=jax.ShapeDtypeStruct((TOKENS, MIX), BF16),
        scratch_shapes=[pltpu.VMEM((POOL_TAIL, MIX), F32)],
        compiler_params=_cparams(("arbitrary", "arbitrary")),
        name="pool",
    )(xd, w, sc)


def _kvprep_kernel(kc_in, vc_in, vs_in, vw_in, ng_in, pe_ref, w1_ref, w2_ref, kc_ref, vct_ref,
                   vst_ref, vwt_ref, gt_ref, xf_ref):
    row = lax.broadcasted_iota(jnp.int32, (N_CHUNK, 1), 0)
    for c, x_in in enumerate((kc_in, vc_in)):
        xf_ref[...] = x_in[...].astype(F32)
        first = jnp.zeros((N_CHUNK, NSA_KV), F32)
        second = jnp.zeros((N_CHUNK, NSA_KV), F32)
        for l in range(CMP_STRIDE):
            xl = xf_ref[pl.ds(l, N_CHUNK, stride=CMP_STRIDE), :]
            first += jnp.dot((xl + pe_ref[c, l:l + 1, :]).astype(BF16), w1_ref[c, l],
                             preferred_element_type=F32)
            l2 = l + CMP_STRIDE
            second += jnp.dot((xl + pe_ref[c, l2:l2 + 1, :]).astype(BF16), w1_ref[c, l2],
                              preferred_element_type=F32)
        hid = jax.nn.gelu(first + pltpu.roll(second, N_CHUNK - 1, axis=0))
        out = jnp.dot(hid.astype(BF16), w2_ref[c], preferred_element_type=F32)
        out = jnp.where(row < N_CMP, out, 0.0)
        if c == 0:
            kc_ref[0] = out.astype(kc_ref.dtype)
        else:
            vct_ref[0] = out.T.astype(vct_ref.dtype)
    vst_ref[0] = vs_in[...].astype(F32).T.astype(vst_ref.dtype)
    vwt_ref[0] = vw_in[...].astype(F32).T.astype(vwt_ref.dtype)
    gates_t = ng_in[...].T
    gt_ref[...] = jnp.zeros_like(gt_ref)
    per_group = 3 * HPG
    for g in range(N_KV):
        gt_ref[0, g, :per_group, :] = gates_t[g * per_group:(g + 1) * per_group, :]


def _kvprep(kv, ng, pe, w1, w2):
    col = lambda j: pl.BlockSpec((SEQ, NSA_KV), lambda b: (b, j))
    whole = lambda a: pl.BlockSpec(a.shape, lambda b: (0,) * a.ndim)
    return pl.pallas_call(
        _kvprep_kernel,
        grid=(BATCH,),
        in_specs=[col(0), col(1), col(3), col(5), pl.BlockSpec((SEQ, GATE_PAD), lambda b: (b, 0)),
                  whole(pe), whole(w1), whole(w2)],
        out_specs=[pl.BlockSpec((1, N_CHUNK, NSA_KV), lambda b: (b, 0, 0)),
                   pl.BlockSpec((1, NSA_KV, N_CHUNK), lambda b: (b, 0, 0)),
                   pl.BlockSpec((1, NSA_KV, SEQ), lambda b: (b, 0, 0)),
                   pl.BlockSpec((1, NSA_KV, SEQ), lambda b: (b, 0, 0)),
                   pl.BlockSpec((1, N_KV, GATE_ROWS, SEQ), lambda b: (b, 0, 0, 0))],
        out_shape=[jax.ShapeDtypeStruct((BATCH, N_CHUNK, NSA_KV), BF16),
                   jax.ShapeDtypeStruct((BATCH, NSA_KV, N_CHUNK), BF16),
                   jax.ShapeDtypeStruct((BATCH, NSA_KV, SEQ), BF16),
                   jax.ShapeDtypeStruct((BATCH, NSA_KV, SEQ), BF16),
                   jax.ShapeDtypeStruct((BATCH, N_KV, GATE_ROWS, SEQ), F32)],
        scratch_shapes=[pltpu.VMEM((SEQ, NSA_KV), F32)],
        compiler_params=_cparams(("arbitrary",)),
        name="kvprep",
    )(kv, kv, kv, kv, ng, pe, w1, w2)


def _group_diag(w):
    eye = jnp.eye(N_KV, dtype=w.dtype)
    out = jnp.einsum('...gde,gh->...gdhe', w, eye)
    return out.reshape(w.shape[:-3] + (N_KV * w.shape[-2], N_KV * w.shape[-1]))


def _lane_tile(x, n):
    return jnp.concatenate([x] * n, axis=1)


ONES_ROWS = 16


def _with_ones(v_t):
    return jnp.concatenate([v_t, jnp.ones((ONES_ROWS, v_t.shape[1]), BF16)], axis=0)


def _nsa_kernel(q_ref, kc_ref, vct_ref, ks_ref, vst_ref, kw_ref, vwt_ref, gt_ref, ovt_ref, blk_ref,
                y_ref, sa_ref, sb_ref, oslc_ref):
    i = pl.program_id(1)
    tq = NSA_TQ
    nl = HPG * tq
    t_row = i * tq + lax.broadcasted_iota(jnp.int32, (1, tq), 1)
    groups = range(N_KV)
    vrows = [slice(g * HEAD_DIM, (g + 1) * HEAD_DIM) for g in groups]

    def scores(k, q_t):
        return jnp.dot(k, q_t, preferred_element_type=F32)

    def normalized(acc):
        return acc[:HEAD_DIM, :] / acc[HEAD_DIM:HEAD_DIM + 1, :]

    lane = lax.broadcasted_iota(jnp.int32, (tq, LANES), 1)
    scale = HEAD_DIM ** -0.5
    q4 = [jnp.concatenate(
        [(jnp.where((lane >= HEAD_DIM) == (g == 1),
                    q_ref[:, j * LANES:(j + 1) * LANES].astype(F32), 0.0) * scale).T
         for j in range(HPG)], axis=1).astype(BF16) for g in groups]

    n_idx = lax.broadcasted_iota(jnp.int32, (N_CHUNK, 1), 0)
    ok = n_idx * CMP_STRIDE + (CMP_LEN - 1) <= _lane_tile(t_row, HPG)
    blk = lax.broadcasted_iota(jnp.int32, (N_SLC, 1), 0)
    cur = jnp.right_shift(t_row, 6)
    forced = (blk == 0) | (blk == cur) | (blk == cur - 1)
    causal_blk = blk * SLC_LEN <= t_row
    o_cmp, q_aug = [], []
    for g in groups:
        s = jnp.where(ok, scores(kc_ref[0], q4[g]), NEG_INF)
        m = jnp.max(s, axis=0, keepdims=True)
        e = jnp.where(ok, jnp.exp(s - m), 0.0)
        l = jnp.sum(e, axis=0, keepdims=True)
        p = e / jnp.where(l > 0.0, l, 1.0)
        o_cmp.append(jnp.dot(vct_ref[0, vrows[g], :], p.astype(BF16), preferred_element_type=F32))
        psum = p[:, 0:tq]
        for j in range(1, HPG):
            psum = psum + p[:, j * tq:(j + 1) * tq]
        p_hi = psum.astype(BF16)
        p_lo = (psum - p_hi.astype(F32)).astype(BF16)
        imp = (jnp.dot(ovt_ref[...], p_hi, preferred_element_type=F32)
               + jnp.dot(ovt_ref[...], p_lo, preferred_element_type=F32))
        work = jnp.where(forced, -3e38, jnp.where(causal_blk, imp, -1.0))
        sel = jnp.where(forced, 1.0, 0.0)
        for _ in range(SLC_TOPN - MAX_FORCED):
            mx = jnp.max(work, axis=0, keepdims=True)
            idx = jnp.min(jnp.where(work == mx, blk, N_SLC), axis=0, keepdims=True)
            pick = blk == idx
            sel = jnp.where(pick & (mx >= 0.0), 1.0, sel)
            work = jnp.where(pick, -3e38, work)
        sel_bias = jnp.where(sel > 0.0, 0.0, NEG_INF)
        sel_bias = jnp.concatenate([sel_bias, jnp.zeros((LANES - N_SLC, tq), F32)], axis=0)
        sel_bias = _lane_tile(sel_bias.astype(BF16), HPG)
        q_aug.append(jnp.concatenate([q4[g], sel_bias], axis=0))

    start = pl.multiple_of(jnp.maximum(i - WIN // tq, 0) * tq, tq)
    k_win = kw_ref[pl.ds(start, WIN_KEYS), :]
    delta = t_row - (start + lax.broadcasted_iota(jnp.int32, (WIN_KEYS, 1), 0))
    win_bias = _lane_tile(jnp.where((delta >= 0) & (delta < WIN), 0.0, NEG_INF), HPG)
    o_win = []
    for g in groups:
        sT = scores(k_win, q4[g]) + win_bias
        pT = jnp.exp((sT - jnp.max(sT, axis=0, keepdims=True)).astype(BF16))
        vT = _with_ones(vwt_ref[vrows[g], pl.ds(start, WIN_KEYS)])
        o_win.append(normalized(jnp.dot(vT, pT, preferred_element_type=F32)))

    def slc_scores(kt, dst_ref):
        k0 = pl.multiple_of(kt * SLC_KT, SLC_KT)
        k_aug = jnp.concatenate([ks_ref[pl.ds(k0, SLC_KT), :], blk_ref[pl.ds(k0, SLC_KT), :]],
                                axis=1)
        maxima = []
        for g in groups:
            s_t = scores(k_aug, q_aug[g])
            dst_ref[g] = s_t
            maxima.append(jnp.max(s_t, axis=0, keepdims=True))
        return maxima

    def slc_update(kt, state, s_t, tile_max):
        k0 = pl.multiple_of(kt * SLC_KT, SLC_KT)
        out = []
        for g in groups:
            m_i, acc = state[2 * g], state[2 * g + 1]
            m_new = jnp.maximum(m_i, tile_max[g])
            alpha = jnp.exp(m_i - m_new)
            pT = jnp.exp((s_t[g] - m_new).astype(BF16))
            vT = _with_ones(vst_ref[vrows[g], pl.ds(k0, SLC_KT)])
            out += [m_new, alpha * acc + jnp.dot(vT, pT, preferred_element_type=F32)]
        return out

    def from_ref(src_ref):
        return [src_ref[g] for g in groups]

    def slc_pair(p, carry):
        state, max_a = list(carry[:2 * N_KV]), list(carry[2 * N_KV:])
        max_b = slc_scores(2 * p + 1, sb_ref)
        state = slc_update(2 * p, state, from_ref(sa_ref), max_a)
        max_a = slc_scores(2 * p + 2, sa_ref)
        state = slc_update(2 * p + 1, state, from_ref(sb_ref), max_b)
        return tuple(state + max_a)

    def slc_finish(state, src_ref):
        kpos = last_kt * SLC_KT + lax.broadcasted_iota(jnp.int32, (SLC_KT, 1), 0)
        causal_bias = _lane_tile(jnp.where(kpos <= t_row, 0.0, NEG_INF), HPG)
        s_last = [s + causal_bias for s in from_ref(src_ref)]
        state = slc_update(last_kt, state, s_last,
                           [jnp.max(s, axis=0, keepdims=True) for s in s_last])
        for g in groups:
            oslc_ref[g] = normalized(state[2 * g + 1])

    last_kt = (i * tq) // SLC_KT
    init = ((jnp.full((1, nl), -3e38, F32), jnp.zeros((HEAD_DIM + ONES_ROWS, nl), F32)) * N_KV
            + tuple(slc_scores(0, sa_ref)))
    carry = lax.fori_loop(0, last_kt // 2, slc_pair, init)
    state, max_a = list(carry[:2 * N_KV]), list(carry[2 * N_KV:])
    odd = lax.rem(last_kt, 2) == 1

    @pl.when(odd)
    def _():
        slc_scores(last_kt, sb_ref)
        slc_finish(slc_update(last_kt - 1, state, from_ref(sa_ref), max_a), sb_ref)

    @pl.when(jnp.logical_not(odd))
    def _():
        slc_finish(state, sa_ref)

    o_slc = [oslc_ref[g] for g in groups]

    for g in groups:
        gates = jax.nn.sigmoid(gt_ref[0, g])

        def gate_row(br):
            return jnp.concatenate([gates[3 * j + br:3 * j + br + 1, :] for j in range(HPG)],
                                   axis=1)

        o = gate_row(0) * o_cmp[g] + gate_row(1) * o_slc[g] + gate_row(2) * o_win[g]
        for j in range(HPG):
            h0 = (g * HPG + j) * HEAD_DIM
            y_ref[0, h0:h0 + HEAD_DIM, :] = o[:, j * tq:(j + 1) * tq].astype(y_ref.dtype)


def _block_onehot():
    oh = (np.arange(SEQ)[:, None] // SLC_LEN) == np.arange(LANES)[None, :]
    return jnp.asarray(oh.astype(np.float32), dtype=BF16)


def _nsa(q, kc, vct, kv, vst, vwt, gt, ovt):
    nq = SEQ // NSA_TQ
    return pl.pallas_call(
        _nsa_kernel,
        grid=(BATCH, nq),
        in_specs=[pl.BlockSpec((NSA_TQ, NSA_Q), lambda b, i: (b * nq + i, 0)),
                  pl.BlockSpec((1, N_CHUNK, NSA_KV), lambda b, i: (b, 0, 0)),
                  pl.BlockSpec((1, NSA_KV, N_CHUNK), lambda b, i: (b, 0, 0)),
                  pl.BlockSpec((SEQ, NSA_KV), lambda b, i: (b, 2)),
                  pl.BlockSpec((None, NSA_KV, SEQ), lambda b, i: (b, 0, 0)),
                  pl.BlockSpec((SEQ, NSA_KV), lambda b, i: (b, 4)),
                  pl.BlockSpec((None, NSA_KV, SEQ), lambda b, i: (b, 0, 0)),
                  pl.BlockSpec((1, N_KV, GATE_ROWS, NSA_TQ), lambda b, i: (b, 0, 0, i)),
                  pl.BlockSpec((N_SLC, N_CHUNK), lambda b, i: (0, 0)),
                  pl.BlockSpec((SEQ, LANES), lambda b, i: (0, 0))],
        out_specs=pl.BlockSpec((1, NSA_Q, NSA_TQ), lambda b, i: (b, 0, i)),
        out_shape=jax.ShapeDtypeStruct((BATCH, NSA_Q, SEQ), BF16),
        scratch_shapes=[pltpu.VMEM((N_KV, SLC_KT, HPG * NSA_TQ), F32),
                        pltpu.VMEM((N_KV, SLC_KT, HPG * NSA_TQ), F32),
                        pltpu.VMEM((N_KV, HEAD_DIM, HPG * NSA_TQ), F32)],
        compiler_params=_cparams(("arbitrary", "arbitrary")),
        name="nsa",
    )(q, kc, vct, kv, vst, kv, vwt, gt, ovt, _block_onehot())


def _overlap_t():
    c_start = np.arange(N_CHUNK) * CMP_STRIDE
    s_start = np.arange(N_SLC) * SLC_LEN
    ov = ((c_start[None, :] <= s_start[:, None] + SLC_LEN - 1)
          & (c_start[None, :] + CMP_LEN - 1 >= s_start[:, None])
          & (np.arange(N_CHUNK)[None, :] < N_CMP))
    return jnp.asarray(ov.astype(np.float32), dtype=BF16)


def _merge_kernel(x_ref, g_ref, wmg_ref, ya_ref, yb_ref, yct_ref, yd_ref, wb_ref, wo_ref, o_ref):
    x = x_ref[...]
    nb = _rms(x, g_ref[...]).astype(BF16)
    yc = yct_ref[0].astype(F32).T.astype(BF16)
    ys = (ya_ref[...], yb_ref[...], yc, yd_ref[...])
    merged = jnp.zeros((MERGE_TM, D_MODEL), F32)
    for bi, y in enumerate(ys):
        gate = jax.nn.sigmoid(jnp.dot(nb, wmg_ref[:, bi * D_MODEL:(bi + 1) * D_MODEL],
                                      preferred_element_type=F32))
        merged = merged + gate * jnp.dot(y, wb_ref[bi], preferred_element_type=F32)
    o_ref[...] = x + jnp.dot(merged.astype(BF16), wo_ref[...], preferred_element_type=F32)


def _merge(x, g, wmg, ya, yb, yct, yd, wb, wo, layer):
    nt = SEQ // MERGE_TM
    tok = lambda i: (i, 0)
    const2 = lambda i: (0, 0)
    return pl.pallas_call(
        _merge_kernel,
        grid=(TOKENS // MERGE_TM,),
        in_specs=[pl.BlockSpec((MERGE_TM, D_MODEL), tok),
                  pl.BlockSpec((1, D_MODEL), const2),
                  pl.BlockSpec((None, D_MODEL, N_BRANCH * D_MODEL), lambda i: (layer, 0, 0)),
                  pl.BlockSpec((MERGE_TM, MIX), tok),
                  pl.BlockSpec((MERGE_TM, MIX), tok),
                  pl.BlockSpec((1, MIX, MERGE_TM), lambda i: (i // nt, 0, i % nt)),
                  pl.BlockSpec((MERGE_TM, MIX), tok),
                  pl.BlockSpec((None, N_BRANCH, MIX, D_MODEL), lambda i: (layer, 0, 0, 0)),
                  pl.BlockSpec((None, D_MODEL, D_MODEL), lambda i: (layer, 0, 0))],
        out_specs=pl.BlockSpec((MERGE_TM, D_MODEL), tok),
        out_shape=jax.ShapeDtypeStruct((TOKENS, D_MODEL), F32),
        compiler_params=_cparams(("arbitrary",)),
        name="merge",
    )(x, g, wmg, ya, yb, yct, yd, wb, wo)


def _router_logits(t, wr_ref, br_ref):
    w = wr_ref[...]
    t_hi, w_hi = t.astype(BF16), w.astype(BF16)
    t_lo = (t - t_hi.astype(F32)).astype(BF16)
    w_lo = (w - w_hi.astype(F32)).astype(BF16)
    dot = functools.partial(jnp.dot, preferred_element_type=F32)
    return dot(t_hi, w_hi) + dot(t_hi, w_lo) + dot(t_lo, w_hi) + br_ref[...]


def _top_group(logits):
    lane = lax.broadcasted_iota(jnp.int32, logits.shape, 1)
    is_grp = lane < N_GROUPS
    lg = jnp.where(is_grp, logits, NEG_INF)
    gmax = jnp.max(lg, axis=1, keepdims=True)
    grp_idx = jnp.min(jnp.where(is_grp & (lg == gmax), lane, ROUTER_PAD), axis=1, keepdims=True)
    return lg, gmax, grp_idx


def _group_weight(logits, grp_idx):
    lane = lax.broadcasted_iota(jnp.int32, logits.shape, 1)
    lg, gmax, _ = _top_group(logits)
    ge = jnp.where(lane < N_GROUPS, jnp.exp(lg - gmax), 0.0)
    return (jnp.sum(jnp.where(lane == grp_idx, ge, 0.0), axis=1, keepdims=True)
            / jnp.sum(ge, axis=1, keepdims=True))


ROUTE_ROWS = 32


def _route_kernel(x_ref, g_ref, wr_ref, br_ref, tri_ref, meta_ref, cnt_ref, run_ref):
    @pl.when(pl.program_id(0) == 0)
    def _():
        run_ref[...] = jnp.zeros_like(run_ref)

    logits = _router_logits(_rms(x_ref[...], g_ref[...]), wr_ref, br_ref)
    lt = logits.T[:ROUTE_ROWS, :]
    row = lax.broadcasted_iota(jnp.int32, (ROUTE_ROWS, 1), 0)

    def top(mask, vals):
        v = jnp.where(mask, vals, NEG_INF)
        best = jnp.max(v, axis=0, keepdims=True)
        return v, jnp.min(jnp.where(mask & (v == best), row, ROUTE_ROWS), axis=0, keepdims=True)

    _, grp_idx = top(row < N_GROUPS, lt)
    first = N_GROUPS + grp_idx * EXPERTS_PER_GROUP
    in_grp = (row >= first) & (row < first + EXPERTS_PER_GROUP)
    le, i1 = top(in_grp, lt)
    _, i2 = top(in_grp & (row != i1), le)
    lo = jnp.minimum(i1, i2) - first
    hi = jnp.maximum(i1, i2) - first
    pair = jnp.right_shift(lo * (2 * EXPERTS_PER_GROUP - 1 - lo), 1) + hi - lo - 1
    cls = grp_idx * N_PAIRS + pair
    onehot = jnp.where(row == cls, 1.0, 0.0)
    before = jnp.dot(onehot.astype(BF16), tri_ref[...], preferred_element_type=F32) + run_ref[...]
    rank = jnp.sum(onehot * before, axis=0, keepdims=True).astype(jnp.int32)
    run_ref[...] += jnp.sum(onehot, axis=1, keepdims=True)
    sub = lax.broadcasted_iota(jnp.int32, (SUBLANES, 1), 0)
    meta_ref[...] = jnp.where(sub == 0, cls, jnp.where(sub == 1, rank, 0))
    cnt_ref[...] = jnp.broadcast_to(run_ref[...], cnt_ref.shape)


def _route(x, g, wr, br):
    tok = lambda i: (i, 0)
    const2 = lambda i: (0, 0)
    tri = jnp.asarray(np.triu(np.ones((ROUTE_TM, ROUTE_TM), np.float32), 1), dtype=BF16)
    return pl.pallas_call(
        _route_kernel,
        grid=(TOKENS // ROUTE_TM,),
        in_specs=[pl.BlockSpec((ROUTE_TM, D_MODEL), tok),
                  pl.BlockSpec((1, D_MODEL), const2),
                  pl.BlockSpec((D_MODEL, ROUTER_PAD), const2),
                  pl.BlockSpec((1, ROUTER_PAD), const2),
                  pl.BlockSpec((ROUTE_TM, ROUTE_TM), const2)],
        out_specs=[pl.BlockSpec((SUBLANES, ROUTE_TM), lambda i: (0, i)),
                   pl.BlockSpec((ROUTE_ROWS, LANES), const2)],
        out_shape=[jax.ShapeDtypeStruct((SUBLANES, TOKENS), jnp.int32),
                   jax.ShapeDtypeStruct((ROUTE_ROWS, LANES), F32)],
        scratch_shapes=[pltpu.VMEM((ROUTE_ROWS, 1), F32)],
        compiler_params=_cparams(("arbitrary",)),
        name="route",
    )(x, g, wr, br, tri)


MOVE_SLOTS = 4


def _move_rows_kernel(idx_ref, seg_ref, len_ref, src_hbm, dst_hbm, buf, zbuf, in_sem, out_sem,
                      pad_sem, *, scatter):
    c = pl.program_id(0)
    n = pl.num_programs(0)
    rc = ROW_CHUNK
    chunk = lambda k: pl.ds(k * rc, rc)

    if scatter:
        def read(k, sl):
            return [pltpu.make_async_copy(src_hbm.at[chunk(k)], buf.at[sl], in_sem.at[sl])]

        def write(k, sl):
            return [pltpu.make_async_copy(buf.at[sl, pl.ds(r, 1)],
                                          dst_hbm.at[pl.ds(idx_ref[k * rc + r], 1)], out_sem.at[sl])
                    for r in range(rc)]

        def write_done(sl):
            return pltpu.make_async_copy(buf.at[sl], dst_hbm.at[chunk(0)], out_sem.at[sl])

        def read_done(sl):
            return read(0, sl)[0]
    else:
        def read(k, sl):
            return [pltpu.make_async_copy(src_hbm.at[pl.ds(idx_ref[k * rc + r], 1)],
                                          buf.at[sl, pl.ds(r, 1)], in_sem.at[sl])
                    for r in range(rc)]

        def write(k, sl):
            return [pltpu.make_async_copy(buf.at[sl], dst_hbm.at[chunk(k)], out_sem.at[sl])]

        def write_done(sl):
            return write(0, sl)[0]

        def read_done(sl):
            return pltpu.make_async_copy(src_hbm.at[chunk(0)], buf.at[sl], in_sem.at[sl])

    def pad_fills(fn):
        for sg in range(N_CLASSES):
            start, length = seg_ref[sg], len_ref[sg]
            head = jnp.bitwise_and(-start, SUBLANES - 1)
            for r in range(SUBLANES - 1):
                @pl.when(r < head)
                def _():
                    fn(pltpu.make_async_copy(zbuf.at[pl.ds(0, 1)], dst_hbm.at[pl.ds(start + r, 1)],
                                             pad_sem))
            body = length - head
            bit = rc // 2
            while bit >= SUBLANES:
                done = body - jnp.bitwise_and(body, 2 * bit - 1)
                first = pl.multiple_of(start + head + done, SUBLANES)

                @pl.when(jnp.bitwise_and(body, bit) != 0)
                def _():
                    fn(pltpu.make_async_copy(zbuf.at[pl.ds(0, bit)], dst_hbm.at[pl.ds(first, bit)],
                                             pad_sem))
                bit //= 2
        tail = pl.multiple_of(seg_ref[N_CLASSES], rc)

        def tail_tile(k, carry):
            fn(pltpu.make_async_copy(zbuf, dst_hbm.at[pl.ds(tail + k * rc, rc)], pad_sem))
            return carry
        lax.fori_loop(0, len_ref[N_CLASSES] // rc, tail_tile, 0)

    per_step = MOVE_SLOTS // 2

    @pl.when(c == 0)
    def _():
        for sl in range(per_step):
            for cp in read(sl, sl):
                cp.start()
        if scatter:
            zbuf[...] = jnp.zeros_like(zbuf)
            pad_fills(lambda cp: cp.start())

    def step(mine, other):
        for i, sl in enumerate(mine):
            read_done(sl).wait()
            for cp in write(per_step * c + i, sl):
                cp.start()

        @pl.when(c + 1 < n)
        def _():
            for i, sl in enumerate(other):
                @pl.when(c >= 1)
                def _():
                    write_done(sl).wait()
                for cp in read(per_step * (c + 1) + i, sl):
                    cp.start()

        @pl.when(c == n - 1)
        def _():
            for sl in other + mine:
                write_done(sl).wait()
            if scatter:
                pad_fills(lambda cp: cp.wait())

    halves = (tuple(range(per_step)), tuple(range(per_step, MOVE_SLOTS)))
    even = lax.rem(c, 2) == 0
    pl.when(even)(lambda: step(*halves))
    pl.when(jnp.logical_not(even))(lambda: step(*halves[::-1]))


def _move_rows(idx, seg_start, seg_len, src, n_out, scatter):
    any_spec = pl.BlockSpec(memory_space=pl.ANY)
    grid_spec = pltpu.PrefetchScalarGridSpec(
        num_scalar_prefetch=3, grid=(TOKENS // (MOVE_SLOTS // 2 * ROW_CHUNK),), in_specs=[any_spec],
        out_specs=any_spec,
        scratch_shapes=[pltpu.VMEM((MOVE_SLOTS, ROW_CHUNK, D_MODEL), F32),
                        pltpu.VMEM((ROW_CHUNK, D_MODEL), F32),
                        pltpu.SemaphoreType.DMA((MOVE_SLOTS,)),
                        pltpu.SemaphoreType.DMA((MOVE_SLOTS,)),
                        pltpu.SemaphoreType.DMA(())])
    return pl.pallas_call(
        functools.partial(_move_rows_kernel, scatter=scatter),
        grid_spec=grid_spec,
        out_shape=jax.ShapeDtypeStruct((n_out, D_MODEL), F32),
        compiler_params=_cparams(("arbitrary",)),
        name="dispatch" if scatter else "collect",
    )(idx, seg_start, seg_len, src)


def _experts_kernel(ea_ref, eb_ref, tv_ref, tb_ref, x_ref, g_ref, wr_ref, br_ref, wga_ref, wua_ref,
                    wda_ref, wgb_ref, wub_ref, wdb_ref, fg_ref, o_ref, *, layer, final_norm):
    j = pl.program_id(0)
    nv = tv_ref[j]

    @pl.when(nv > 0)
    def _():
        x = x_ref[...]
        t = _rms(x, g_ref[...])
        logits = _router_logits(t, wr_ref, br_ref)
        lane = lax.broadcasted_iota(jnp.int32, logits.shape, 1)
        ea = ea_ref[j] - layer * N_EXPERTS
        eb = eb_ref[j] - layer * N_EXPERTS
        la = jnp.sum(jnp.where(lane == N_GROUPS + ea, logits, 0.0), axis=1, keepdims=True)
        lb = jnp.sum(jnp.where(lane == N_GROUPS + eb, logits, 0.0), axis=1, keepdims=True)
        top = jnp.maximum(la, lb)
        pa, pb = jnp.exp(la - top), jnp.exp(lb - top)
        scale = _group_weight(logits, jnp.right_shift(ea, 2)) / (pa + pb)
        tb = t.astype(BF16)
        acc = jnp.zeros((MOE_TM, D_MODEL), F32)
        for wg, wu, wd, w in ((wga_ref, wua_ref, wda_ref, pa * scale),
                              (wgb_ref, wub_ref, wdb_ref, pb * scale)):
            hid = (jax.nn.silu(jnp.dot(tb, wg[0], preferred_element_type=F32))
                   * jnp.dot(tb, wu[0], preferred_element_type=F32)) * w
            acc = acc + jnp.dot(hid.astype(BF16), wd[0], preferred_element_type=F32)
        h = x + acc
        if final_norm:
            h = _rms(h, fg_ref[...])
        o_ref[...] = h

    @pl.when(nv == 0)
    def _():
        o_ref[...] = jnp.zeros_like(o_ref)


def _experts(tile_ea, tile_eb, tile_valid, tile_block, hs, g, wr, br, wg, wu, wd, fg, layer,
             final_norm):
    const2 = lambda j, *_: (0, 0)
    of_a = lambda j, ea, eb, tv, tb: (ea[j], 0, 0)
    of_b = lambda j, ea, eb, tv, tb: (eb[j], 0, 0)
    up = lambda idx: pl.BlockSpec((1, D_MODEL, D_EXPERT), idx)
    down = lambda idx: pl.BlockSpec((1, D_EXPERT, D_MODEL), idx)
    grid_spec = pltpu.PrefetchScalarGridSpec(
        num_scalar_prefetch=4,
        grid=(MOE_NT,),
        in_specs=[pl.BlockSpec((MOE_TM, D_MODEL), lambda j, ea, eb, tv, tb: (tb[j], 0)),
                  pl.BlockSpec((1, D_MODEL), const2),
                  pl.BlockSpec((D_MODEL, ROUTER_PAD), const2),
                  pl.BlockSpec((1, ROUTER_PAD), const2),
                  up(of_a), up(of_a), down(of_a), up(of_b), up(of_b), down(of_b),
                  pl.BlockSpec((1, D_MODEL), const2)],
        out_specs=pl.BlockSpec((MOE_TM, D_MODEL), lambda j, *_: (j, 0)))
    return pl.pallas_call(
        functools.partial(_experts_kernel, layer=layer, final_norm=final_norm),
        grid_spec=grid_spec,
        out_shape=jax.ShapeDtypeStruct((MOE_ROWS, D_MODEL), F32),
        compiler_params=_cparams(("arbitrary",)),
        name="experts",
    )(tile_ea, tile_eb, tile_valid, tile_block, hs, g, wr, br, wg, wu, wd, wg, wu, wd, fg)


_PAIR_LO = np.array([a for a in range(EXPERTS_PER_GROUP) for b in range(a + 1, EXPERTS_PER_GROUP)])
_PAIR_HI = np.array([b for a in range(EXPERTS_PER_GROUP) for b in range(a + 1, EXPERTS_PER_GROUP)])


def _moe(x, g, wr, br, wg, wu, wd, fg, layer, final_norm):
    meta, cnt = _route(x, g, wr, br)
    cls, rank = meta[0], meta[1]
    counts = cnt[:N_CLASSES, 0].astype(jnp.int32)
    padded = (counts + MOE_TM - 1) // MOE_TM * MOE_TM
    ends = jnp.cumsum(padded)
    starts = ends - padded
    pos = starts[cls] + rank
    tile_start = jnp.arange(MOE_NT, dtype=jnp.int32) * MOE_TM
    tile_cls = jnp.minimum(jnp.sum(tile_start[:, None] >= ends[None, :], axis=1), N_CLASSES - 1)
    tile_valid = jnp.clip(starts[tile_cls] + counts[tile_cls] - tile_start, 0, MOE_TM)
    tile_valid = tile_valid.astype(jnp.int32)
    tile_block = jnp.where(tile_valid > 0, jnp.arange(MOE_NT, dtype=jnp.int32), 0)
    first = layer * N_EXPERTS + (tile_cls // N_PAIRS) * EXPERTS_PER_GROUP
    tile_ea = (first + jnp.asarray(_PAIR_LO)[tile_cls % N_PAIRS]).astype(jnp.int32)
    tile_eb = (first + jnp.asarray(_PAIR_HI)[tile_cls % N_PAIRS]).astype(jnp.int32)
    seg_start = jnp.concatenate([starts + counts, ends[-1:]])
    seg_len = jnp.concatenate([padded - counts, MOE_ROWS - ends[-1:]])
    hs = _move_rows(pos, seg_start, seg_len, x, MOE_ROWS, scatter=True)
    ys = _experts(tile_ea, tile_eb, tile_valid, tile_block, hs, g, wr, br, wg, wu, wd, fg, layer,
                  final_norm)
    return _move_rows(pos, seg_start, seg_len, ys, TOKENS, scatter=False)


def _block_diag(w):
    eye = jnp.eye(LRU_BLOCKS, dtype=w.dtype)
    return jnp.einsum('hij,hk->hikj', w, eye).reshape(MIX, MIX)


def _w_in_column_ranges():
    cuts = [int(c) for c in np.cumsum((0,) + IN_SPLITS)]
    q_parts = [(cuts[4] + (g * HPG + j) * HEAD_DIM, cuts[4] + (g * HPG + j + 1) * HEAD_DIM)
               for j in range(HPG) for g in range(N_KV)]
    return ([(cuts[8], cuts[9]), (cuts[0], cuts[4])] + q_parts
            + [(cuts[5], cuts[6]), (cuts[7], cuts[8]), (cuts[6], cuts[7])])


def _prep_w_in_kernel(w_ref, o_ref):
    off = 0
    for a, b in _w_in_column_ranges():
        o_ref[:, off:off + b - a] = w_ref[:, a:b].astype(o_ref.dtype)
        off += b - a
    o_ref[:, off:] = jnp.zeros((o_ref.shape[0], o_ref.shape[1] - off), o_ref.dtype)


def _prep_w_in(w_in):
    n_in = w_in.shape[-1]
    return pl.pallas_call(
        _prep_w_in_kernel,
        grid=(DEPTH, D_MODEL // PREP_ROWS),
        in_specs=[pl.BlockSpec((None, PREP_ROWS, n_in), lambda l, r: (l, r, 0))],
        out_specs=pl.BlockSpec((None, PREP_ROWS, 2 * W_IN_HALF), lambda l, r: (l, r, 0)),
        out_shape=jax.ShapeDtypeStruct((DEPTH, D_MODEL, 2 * W_IN_HALF), BF16),
        compiler_params=_cparams(("arbitrary", "arbitrary")),
        name="prep_w_in",
    )(w_in)


def _nsa_mixer(q, kv, ng, p):
    w1 = p['cmp_w1'].reshape(2, N_KV, CMP_LEN, HEAD_DIM, HEAD_DIM).transpose(0, 2, 1, 3, 4)
    pe = jnp.concatenate([p['cmp_pe']] * N_KV, axis=-1)
    kc, vct, vst, vwt, gt = _kvprep(kv, ng, pe, _group_diag(w1).astype(BF16),
                                    _group_diag(p['cmp_w2']).astype(BF16))
    return _nsa(q, kc, vct, kv, vst, vwt, gt, _overlap_t())


def _layer(h, p, big, layer, final_g, final_norm):
    row = lambda a: a.reshape(1, -1)

    u, v, gb, rb, q, kv, xd, ng = _proj(h, row(p['norm1_g']), big['w_in'], layer)

    bs = jnp.broadcast_to(p['gm_b'][:, :, None], (GM_GROUPS, GM_CHUNK, GM_GW))
    y_a = _gmlp(u, v, row(p['gm_norm_g']), p['gm_ws'], bs)

    y_b = _lru(gb, rb, p['conv_w'], row(p['conv_b']), _block_diag(p['lru_wa']).astype(BF16),
               row(p['lru_ba']), _block_diag(p['lru_wx']).astype(BF16), row(p['lru_bx']),
               row(p['lru_lambda']))

    y_d = _pool(xd, p['pool_w'].astype(BF16), row(p['pool_scale']))

    y_ct = _nsa_mixer(q, kv, ng, p)

    h = _merge(h, row(p['norm1_g']), big['w_in'], y_a, y_b, y_ct, y_d, big['w_branch'],
               big['w_out'], layer)

    wr = jnp.concatenate([p['router_w_group'], p['router_w_expert']], axis=1)
    wr = jnp.pad(wr, ((0, 0), (0, ROUTER_PAD - wr.shape[1])))
    br = jnp.concatenate([p['router_b_group'], p['router_b_expert']])
    br = jnp.pad(br, (0, ROUTER_PAD - br.shape[0])).reshape(1, ROUTER_PAD)
    return _moe(h, row(p['norm2_g']), wr, br, big['moe_w_gate'], big['moe_w_up'],
                big['moe_w_down'], row(final_g), layer, final_norm)


_LAYER_PARAMS = ('norm1_g', 'gm_norm_g', 'gm_ws', 'gm_b', 'conv_w', 'conv_b', 'lru_wa',
                 'lru_ba', 'lru_wx', 'lru_bx', 'lru_lambda', 'cmp_pe', 'cmp_w1', 'cmp_w2', 'pool_w',
                 'pool_scale', 'norm2_g', 'router_w_group', 'router_b_group',
                 'router_w_expert', 'router_b_expert')


def kernel(x, norm1_g, w_in, gm_norm_g, gm_ws, gm_b, conv_w, conv_b, lru_wa, lru_ba, lru_wx,
           lru_bx, lru_lambda, cmp_pe, cmp_w1, cmp_w2, pool_w, pool_scale, w_branch, w_out,
           norm2_g, router_w_group, router_b_group, router_w_expert, router_b_expert,
           moe_w_gate, moe_w_up, moe_w_down, final_norm_g):
    stacked = dict(zip(_LAYER_PARAMS, (
        norm1_g, gm_norm_g, gm_ws, gm_b, conv_w, conv_b, lru_wa, lru_ba, lru_wx, lru_bx,
        lru_lambda, cmp_pe, cmp_w1, cmp_w2, pool_w, pool_scale, norm2_g,
        router_w_group, router_b_group, router_w_expert, router_b_expert)))
    w_all = _prep_w_in(w_in)
    experts = lambda w: w.astype(BF16).reshape((DEPTH * N_EXPERTS,) + w.shape[2:])
    big = dict(w_in=w_all, w_branch=w_branch.astype(BF16), w_out=w_out.astype(BF16),
               moe_w_gate=experts(moe_w_gate), moe_w_up=experts(moe_w_up),
               moe_w_down=experts(moe_w_down))
    h = x.reshape(TOKENS, D_MODEL)
    for layer in range(DEPTH):
        p = {k: a[layer] for k, a in stacked.items()}
        h = _layer(h, p, big, layer, final_norm_g, final_norm=(layer == DEPTH - 1))
    return h.reshape(BATCH, SEQ, D_MODEL)
```

```python
import functools

import numpy as np
import jax
import jax.numpy as jnp
from jax import lax
from jax.experimental import pallas as pl
from jax.experimental.pallas import tpu as pltpu

F32 = jnp.float32
BF16 = jnp.bfloat16

D_MODEL = 1024
BATCH = 4
SEQ = 4096
TOKENS = BATCH * SEQ
DEPTH = 2
MIX = D_MODEL // 2
GM_CHUNK = 128
GM_GROUPS = 4
GM_GW = MIX // GM_GROUPS
CONV_WIDTH = 4
LRU_BLOCKS = 8
LRU_BW = MIX // LRU_BLOCKS
LRU_C = 8.0
N_HEADS = 8
HEAD_DIM = MIX // N_HEADS
N_KV = 2
HPG = N_HEADS // N_KV
CMP_LEN = 32
CMP_STRIDE = 16
SLC_LEN = 64
SLC_TOPN = 8
MAX_FORCED = 3
WIN = 512
NSA_Q = N_HEADS * HEAD_DIM
NSA_KV = N_KV * HEAD_DIM
POOL_WINDOWS = (2, 4, 8, 16)
POOL_GW = MIX // len(POOL_WINDOWS)
N_BRANCH = 4
N_GROUPS = 4
EXPERTS_PER_GROUP = 4
N_EXPERTS = N_GROUPS * EXPERTS_PER_GROUP
D_EXPERT = D_MODEL // 2
EPS = 1e-6
NEG_INF = -1e30
FORCE_SCORE = 1e6
IN_SPLITS = (MIX, MIX, MIX, MIX, NSA_Q, 6 * NSA_KV, 3 * N_HEADS, MIX, N_BRANCH * D_MODEL)

N_CHUNK = SEQ // CMP_STRIDE
N_CMP = N_CHUNK - CMP_LEN // CMP_STRIDE + 1
N_SLC = SEQ // SLC_LEN

LANES = 128
SUBLANES = 8
GATE_PAD = LANES
GATE_ROWS = 16
ROUTER_PAD = LANES
VMEM_LIMIT = 56 * 1024 * 1024

PROJ_WIDTHS = (MIX, MIX, MIX, MIX, NSA_Q, 6 * NSA_KV, MIX, GATE_PAD)
W_IN_HALF = N_BRANCH * D_MODEL
PROJ_TM = 512
PREP_ROWS = 256
GMLP_TM = 512
LRU_TS = 512
POOL_TS = 512
NSA_TQ = 256
SLC_KT = 256
WIN_KEYS = WIN + NSA_TQ
MERGE_TM = 256
ROUTE_TM = 512
MOE_TM = 256
N_PAIRS = EXPERTS_PER_GROUP * (EXPERTS_PER_GROUP - 1) // 2
N_CLASSES = N_GROUPS * N_PAIRS
MOE_ROWS = TOKENS + N_CLASSES * MOE_TM
MOE_NT = MOE_ROWS // MOE_TM
ROW_CHUNK = 256


def _cparams(sem):
    return pltpu.CompilerParams(dimension_semantics=sem, vmem_limit_bytes=VMEM_LIMIT)


def _rms(x, g):
    return x * lax.rsqrt(jnp.mean(x * x, axis=-1, keepdims=True) + EPS) * g


def _proj_kernel(x_ref, g_ref, w_ref, *out_refs):
    nb = _rms(x_ref[...], g_ref[...]).astype(BF16)
    off = 0
    for ref in out_refs:
        w = ref.shape[-1]
        ref[...] = jnp.dot(nb, w_ref[:, off:off + w], preferred_element_type=F32).astype(ref.dtype)
        off += w


def _proj(x, g, w, layer):
    out_shape = [jax.ShapeDtypeStruct((TOKENS, wd), BF16) for wd in PROJ_WIDTHS[:-1]]
    out_shape.append(jax.ShapeDtypeStruct((TOKENS, GATE_PAD), F32))
    return pl.pallas_call(
        _proj_kernel,
        grid=(TOKENS // PROJ_TM,),
        in_specs=[pl.BlockSpec((PROJ_TM, D_MODEL), lambda i: (i, 0)),
                  pl.BlockSpec((1, D_MODEL), lambda i: (0, 0)),
                  pl.BlockSpec((None, D_MODEL, W_IN_HALF), lambda i: (layer, 0, 1))],
        out_specs=[pl.BlockSpec((PROJ_TM, wd), lambda i: (i, 0)) for wd in PROJ_WIDTHS],
        out_shape=out_shape,
        compiler_params=_cparams(("arbitrary",)),
        name="proj",
    )(x, g, w)


def _gmlp_kernel(u_ref, v_ref, g_ref, ws_ref, bs_ref, o_ref):
    u = jax.nn.gelu(u_ref[...].astype(F32))
    v = _rms(jax.nn.gelu(v_ref[...].astype(F32)), g_ref[...]).astype(BF16)
    row = lax.broadcasted_iota(jnp.int32, (GM_CHUNK, GM_CHUNK), 0)
    col = lax.broadcasted_iota(jnp.int32, (GM_CHUNK, GM_CHUNK), 1)
    causal = row >= col
    for gi in range(GM_GROUPS):
        w = jnp.where(causal, ws_ref[gi], 0.0).astype(BF16)
        cs = slice(gi * GM_GW, (gi + 1) * GM_GW)
        for c in range(GMLP_TM // GM_CHUNK):
            rs = slice(c * GM_CHUNK, (c + 1) * GM_CHUNK)
            mixed = jnp.dot(w, v[rs, cs], preferred_element_type=F32) + bs_ref[gi]
            o_ref[rs, cs] = (u[rs, cs] * mixed).astype(o_ref.dtype)


def _gmlp(u, v, g, ws, bs):
    tok = lambda i: (i, 0)
    return pl.pallas_call(
        _gmlp_kernel,
        grid=(TOKENS // GMLP_TM,),
        in_specs=[pl.BlockSpec((GMLP_TM, MIX), tok),
                  pl.BlockSpec((GMLP_TM, MIX), tok),
                  pl.BlockSpec((1, MIX), lambda i: (0, 0)),
                  pl.BlockSpec((GM_GROUPS, GM_CHUNK, GM_CHUNK), lambda i: (0, 0, 0)),
                  pl.BlockSpec((GM_GROUPS, GM_CHUNK, GM_GW), lambda i: (0, 0, 0))],
        out_specs=pl.BlockSpec((GMLP_TM, MIX), tok),
        out_shape=jax.ShapeDtypeStruct((TOKENS, MIX), BF16),
        compiler_params=_cparams(("arbitrary",)),
        name="gmlp",
    )(u, v, g, ws, bs)


LRU_TAIL = 8


def _lru_kernel(gb_ref, rb_ref, cw_ref, cb_ref, wa_ref, ba_ref, wx_ref, bx_ref, lam_ref, o_ref,
                tail_ref, h_ref):
    @pl.when(pl.program_id(1) == 0)
    def _():
        tail_ref[...] = jnp.zeros_like(tail_ref)
        h_ref[...] = jnp.zeros_like(h_ref)

    ts = LRU_TS
    x = rb_ref[...].astype(F32)
    ext = jnp.concatenate([tail_ref[...], x], axis=0)
    tail_ref[...] = x[ts - LRU_TAIL:, :]
    xc = cb_ref[...] + x * cw_ref[CONV_WIDTH - 1:CONV_WIDTH, :]
    for d in range(1, CONV_WIDTH):
        xs = pltpu.roll(ext, d, axis=0)[LRU_TAIL:, :]
        xc = xc + xs * cw_ref[CONV_WIDTH - 1 - d:CONV_WIDTH - d, :]
    xcb = xc.astype(BF16)
    r = jax.nn.sigmoid(jnp.dot(xcb, wa_ref[...], preferred_element_type=F32) + ba_ref[...])
    ig = jax.nn.sigmoid(jnp.dot(xcb, wx_ref[...], preferred_element_type=F32) + bx_ref[...])
    z = -lam_ref[...]
    softplus = jnp.maximum(z, 0.0) + jnp.log1p(jnp.exp(-jnp.abs(z)))
    log_a = -LRU_C * r * softplus
    a = jnp.exp(log_a)
    b = jnp.sqrt(1.0 - jnp.exp(2.0 * log_a)) * (ig * xc)
    rows = lax.broadcasted_iota(jnp.int32, (ts, 1), 0)
    d = 1
    while d < ts:
        valid = rows >= d
        a_prev = pltpu.roll(a, d, axis=0)
        b_prev = pltpu.roll(b, d, axis=0)
        b = jnp.where(valid, a * b_prev, 0.0) + b
        a = jnp.where(valid, a * a_prev, a)
        d *= 2
    h = a * h_ref[...] + b
    h_ref[...] = h[ts - 1:ts, :]
    o_ref[...] = (jax.nn.gelu(gb_ref[...].astype(F32)) * h).astype(o_ref.dtype)


def _lru(gb, rb, cw, cb, wa, ba, wx, bx, lam):
    nt = SEQ // LRU_TS
    tok = lambda b, s: (b * nt + s, 0)
    vec = pl.BlockSpec((1, MIX), lambda b, s: (0, 0))
    mat = pl.BlockSpec((MIX, MIX), lambda b, s: (0, 0))
    return pl.pallas_call(
        _lru_kernel,
        grid=(BATCH, nt),
        in_specs=[pl.BlockSpec((LRU_TS, MIX), tok), pl.BlockSpec((LRU_TS, MIX), tok),
                  pl.BlockSpec((CONV_WIDTH, MIX), lambda b, s: (0, 0)), vec,
                  mat, vec, mat, vec, vec],
        out_specs=pl.BlockSpec((LRU_TS, MIX), tok),
        out_shape=jax.ShapeDtypeStruct((TOKENS, MIX), BF16),
        scratch_shapes=[pltpu.VMEM((LRU_TAIL, MIX), F32), pltpu.VMEM((1, MIX), F32)],
        compiler_params=_cparams(("arbitrary", "arbitrary")),
        name="rglru",
    )(gb, rb, cw, cb, wa, ba, wx, bx, lam)


POOL_TAIL = 16


def _pool_kernel(x_ref, w_ref, sc_ref, o_ref, tail_ref):
    s_id = pl.program_id(1)

    @pl.when(s_id == 0)
    def _():
        tail_ref[...] = jnp.zeros_like(tail_ref)

    ts = POOL_TS
    x = x_ref[...].astype(F32)
    ext = jnp.concatenate([tail_ref[...], x], axis=0)
    tail_ref[...] = x[ts - POOL_TAIL:, :]
    pos = s_id * ts + lax.broadcasted_iota(jnp.int32, (ts, 1), 0)
    acc = ext
    width = 1
    for gi, wdw in enumerate(POOL_WINDOWS):
        while width < wdw:
            acc = acc + pltpu.roll(acc, width, axis=0)
            width *= 2
        cs = slice(gi * POOL_GW, (gi + 1) * POOL_GW)
        cnt = jnp.minimum(pos + 1, wdw).astype(F32)
        pooled = acc[POOL_TAIL:, cs] / cnt - x[:, cs]
        mixed = jnp.dot(pooled.astype(BF16), w_ref[gi], preferred_element_type=F32)
        o_ref[:, cs] = (mixed * sc_ref[:, cs]).astype(o_ref.dtype)


def _pool(xd, w, sc):
    nt = SEQ // POOL_TS
    tok = lambda b, s: (b * nt + s, 0)
    return pl.pallas_call(
        _pool_kernel,
        grid=(BATCH, nt),
        in_specs=[pl.BlockSpec((POOL_TS, MIX), tok),
                  pl.BlockSpec((len(POOL_WINDOWS), POOL_GW, POOL_GW), lambda b, s: (0, 0, 0)),
                  pl.BlockSpec((1, MIX), lambda b, s: (0, 0))],
        out_specs=pl.BlockSpec((POOL_TS, MIX), tok),
        out_shape=jax.ShapeDtypeStruct((TOKENS, MIX), BF16),
        scratch_shapes=[pltpu.VMEM((POOL_TAIL, MIX), F32)],
        compiler_params=_cparams(("arbitrary", "arbitrary")),
        name="pool",
    )(xd, w, sc)


def _kvprep_kernel(kc_in, vc_in, vs_in, vw_in, ng_in, pe_ref, w1_ref, w2_ref, kc_ref, vct_ref,
                   vst_ref, vwt_ref, gt_ref, xf_ref):
    row = lax.broadcasted_iota(jnp.int32, (N_CHUNK, 1), 0)
    for c, x_in in enumerate((kc_in, vc_in)):
        xf_ref[...] = x_in[...].astype(F32)
        first = jnp.zeros((N_CHUNK, NSA_KV), F32)
        second = jnp.zeros((N_CHUNK, NSA_KV), F32)
        for l in range(CMP_STRIDE):
            xl = xf_ref[pl.ds(l, N_CHUNK, stride=CMP_STRIDE), :]
            first += jnp.dot((xl + pe_ref[c, l:l + 1, :]).astype(BF16), w1_ref[c, l],
                             preferred_element_type=F32)
            l2 = l + CMP_STRIDE
            second += jnp.dot((xl + pe_ref[c, l2:l2 + 1, :]).astype(BF16), w1_ref[c, l2],
                              preferred_element_type=F32)
        hid = jax.nn.gelu(first + pltpu.roll(second, N_CHUNK - 1, axis=0))
        out = jnp.dot(hid.astype(BF16), w2_ref[c], preferred_element_type=F32)
        out = jnp.where(row < N_CMP, out, 0.0)
        if c == 0:
            kc_ref[0] = out.astype(kc_ref.dtype)
        else:
            vct_ref[0] = out.T.astype(vct_ref.dtype)
    vst_ref[0] = vs_in[...].astype(F32).T.astype(vst_ref.dtype)
    vwt_ref[0] = vw_in[...].astype(F32).T.astype(vwt_ref.dtype)
    gates_t = ng_in[...].T
    gt_ref[...] = jnp.zeros_like(gt_ref)
    per_group = 3 * HPG
    for g in range(N_KV):
        gt_ref[0, g, :per_group, :] = gates_t[g * per_group:(g + 1) * per_group, :]


def _kvprep(kv, ng, pe, w1, w2):
    col = lambda j: pl.BlockSpec((SEQ, NSA_KV), lambda b: (b, j))
    whole = lambda a: pl.BlockSpec(a.shape, lambda b: (0,) * a.ndim)
    return pl.pallas_call(
        _kvprep_kernel,
        grid=(BATCH,),
        in_specs=[col(0), col(1), col(3), col(5), pl.BlockSpec((SEQ, GATE_PAD), lambda b: (b, 0)),
                  whole(pe), whole(w1), whole(w2)],
        out_specs=[pl.BlockSpec((1, N_CHUNK, NSA_KV), lambda b: (b, 0, 0)),
                   pl.BlockSpec((1, NSA_KV, N_CHUNK), lambda b: (b, 0, 0)),
                   pl.BlockSpec((1, NSA_KV, SEQ), lambda b: (b, 0, 0)),
                   pl.BlockSpec((1, NSA_KV, SEQ), lambda b: (b, 0, 0)),
                   pl.BlockSpec((1, N_KV, GATE_ROWS, SEQ), lambda b: (b, 0, 0, 0))],
        out_shape=[jax.ShapeDtypeStruct((BATCH, N_CHUNK, NSA_KV), BF16),
                   jax.ShapeDtypeStruct((BATCH, NSA_KV, N_CHUNK), BF16),
                   jax.ShapeDtypeStruct((BATCH, NSA_KV, SEQ), BF16),
                   jax.ShapeDtypeStruct((BATCH, NSA_KV, SEQ), BF16),
                   jax.ShapeDtypeStruct((BATCH, N_KV, GATE_ROWS, SEQ), F32)],
        scratch_shapes=[pltpu.VMEM((SEQ, NSA_KV), F32)],
        compiler_params=_cparams(("arbitrary",)),
        name="kvprep",
    )(kv, kv, kv, kv, ng, pe, w1, w2)


def _group_diag(w):
    eye = jnp.eye(N_KV, dtype=w.dtype)
    out = jnp.einsum('...gde,gh->...gdhe', w, eye)
    return out.reshape(w.shape[:-3] + (N_KV * w.shape[-2], N_KV * w.shape[-1]))


def _lane_tile(x, n):
    return jnp.concatenate([x] * n, axis=1)


ONES_ROWS = 16


def _with_ones(v_t):
    return jnp.concatenate([v_t, jnp.ones((ONES_ROWS, v_t.shape[1]), BF16)], axis=0)


def _nsa_kernel(q_ref, kc_ref, vct_ref, ks_ref, vst_ref, kw_ref, vwt_ref, gt_ref, ovt_ref, blk_ref,
                y_ref, sa_ref, sb_ref, oslc_ref):
    i = pl.program_id(1)
    tq = NSA_TQ
    nl = HPG * tq
    t_row = i * tq + lax.broadcasted_iota(jnp.int32, (1, tq), 1)
    groups = range(N_KV)
    vrows = [slice(g * HEAD_DIM, (g + 1) * HEAD_DIM) for g in groups]

    def scores(k, q_t):
        return jnp.dot(k, q_t, preferred_element_type=F32)

    def normalized(acc):
        return acc[:HEAD_DIM, :] / acc[HEAD_DIM:HEAD_DIM + 1, :]

    lane = lax.broadcasted_iota(jnp.int32, (tq, LANES), 1)
    scale = HEAD_DIM ** -0.5
    q4 = [jnp.concatenate(
        [(jnp.where((lane >= HEAD_DIM) == (g == 1),
                    q_ref[:, j * LANES:(j + 1) * LANES].astype(F32), 0.0) * scale).T
         for j in range(HPG)], axis=1).astype(BF16) for g in groups]

    n_idx = lax.broadcasted_iota(jnp.int32, (N_CHUNK, 1), 0)
    ok = n_idx * CMP_STRIDE + (CMP_LEN - 1) <= _lane_tile(t_row, HPG)
    blk = lax.broadcasted_iota(jnp.int32, (N_SLC, 1), 0)
    cur = jnp.right_shift(t_row, 6)
    forced = (blk == 0) | (blk == cur) | (blk == cur - 1)
    causal_blk = blk * SLC_LEN <= t_row
    o_cmp, q_aug = [], []
    for g in groups:
        s = jnp.where(ok, scores(kc_ref[0], q4[g]), NEG_INF)
        m = jnp.max(s, axis=0, keepdims=True)
        e = jnp.where(ok, jnp.exp(s - m), 0.0)
        l = jnp.sum(e, axis=0, keepdims=True)
        p = e / jnp.where(l > 0.0, l, 1.0)
        o_cmp.append(jnp.dot(vct_ref[0, vrows[g], :], p.astype(BF16), preferred_element_type=F32))
        psum = p[:, 0:tq]
        for j in range(1, HPG):
            psum = psum + p[:, j * tq:(j + 1) * tq]
        p_hi = psum.astype(BF16)
        p_lo = (psum - p_hi.astype(F32)).astype(BF16)
        imp = (jnp.dot(ovt_ref[...], p_hi, preferred_element_type=F32)
               + jnp.dot(ovt_ref[...], p_lo, preferred_element_type=F32))
        work = jnp.where(forced, -3e38, jnp.where(causal_blk, imp, -1.0))
        sel = jnp.where(forced, 1.0, 0.0)
        for _ in range(SLC_TOPN - MAX_FORCED):
            mx = jnp.max(work, axis=0, keepdims=True)
            idx = jnp.min(jnp.where(work == mx, blk, N_SLC), axis=0, keepdims=True)
            pick = blk == idx
            sel = jnp.where(pick & (mx >= 0.0), 1.0, sel)
            work = jnp.where(pick, -3e38, work)
        sel_bias = jnp.where(sel > 0.0, 0.0, NEG_INF)
        sel_bias = jnp.concatenate([sel_bias, jnp.zeros((LANES - N_SLC, tq), F32)], axis=0)
        sel_bias = _lane_tile(sel_bias.astype(BF16), HPG)
        q_aug.append(jnp.concatenate([q4[g], sel_bias], axis=0))

    start = pl.multiple_of(jnp.maximum(i - WIN // tq, 0) * tq, tq)
    k_win = kw_ref[pl.ds(start, WIN_KEYS), :]
    delta = t_row - (start + lax.broadcasted_iota(jnp.int32, (WIN_KEYS, 1), 0))
    win_bias = _lane_tile(jnp.where((delta >= 0) & (delta < WIN), 0.0, NEG_INF), HPG)
    o_win = []
    for g in groups:
        sT = scores(k_win, q4[g]) + win_bias
        pT = jnp.exp((sT - jnp.max(sT, axis=0, keepdims=True)).astype(BF16))
        vT = _with_ones(vwt_ref[vrows[g], pl.ds(start, WIN_KEYS)])
        o_win.append(normalized(jnp.dot(vT, pT, preferred_element_type=F32)))

    def slc_scores(kt, dst_ref):
        k0 = pl.multiple_of(kt * SLC_KT, SLC_KT)
        k_aug = jnp.concatenate([ks_ref[pl.ds(k0, SLC_KT), :], blk_ref[pl.ds(k0, SLC_KT), :]],
                                axis=1)
        maxima = []
        for g in groups:
            s_t = scores(k_aug, q_aug[g])
            dst_ref[g] = s_t
            maxima.append(jnp.max(s_t, axis=0, keepdims=True))
        return maxima

    def slc_update(kt, state, s_t, tile_max):
        k0 = pl.multiple_of(kt * SLC_KT, SLC_KT)
        out = []
        for g in groups:
            m_i, acc = state[2 * g], state[2 * g + 1]
            m_new = jnp.maximum(m_i, tile_max[g])
            alpha = jnp.exp(m_i - m_new)
            pT = jnp.exp((s_t[g] - m_new).astype(BF16))
            vT = _with_ones(vst_ref[vrows[g], pl.ds(k0, SLC_KT)])
            out += [m_new, alpha * acc + jnp.dot(vT, pT, preferred_element_type=F32)]
        return out

    def from_ref(src_ref):
        return [src_ref[g] for g in groups]

    def slc_pair(p, carry):
        state, max_a = list(carry[:2 * N_KV]), list(carry[2 * N_KV:])
        max_b = slc_scores(2 * p + 1, sb_ref)
        state = slc_update(2 * p, state, from_ref(sa_ref), max_a)
        max_a = slc_scores(2 * p + 2, sa_ref)
        state = slc_update(2 * p + 1, state, from_ref(sb_ref), max_b)
        return tuple(state + max_a)

    def slc_finish(state, src_ref):
        kpos = last_kt * SLC_KT + lax.broadcasted_iota(jnp.int32, (SLC_KT, 1), 0)
        causal_bias = _lane_tile(jnp.where(kpos <= t_row, 0.0, NEG_INF), HPG)
        s_last = [s + causal_bias for s in from_ref(src_ref)]
        state = slc_update(last_kt, state, s_last,
                           [jnp.max(s, axis=0, keepdims=True) for s in s_last])
        for g in groups:
            oslc_ref[g] = normalized(state[2 * g + 1])

    last_kt = (i * tq) // SLC_KT
    init = ((jnp.full((1, nl), -3e38, F32), jnp.zeros((HEAD_DIM + ONES_ROWS, nl), F32)) * N_KV
            + tuple(slc_scores(0, sa_ref)))
    carry = lax.fori_loop(0, last_kt // 2, slc_pair, init)
    state, max_a = list(carry[:2 * N_KV]), list(carry[2 * N_KV:])
    odd = lax.rem(last_kt, 2) == 1

    @pl.when(odd)
    def _():
        slc_scores(last_kt, sb_ref)
        slc_finish(slc_update(last_kt - 1, state, from_ref(sa_ref), max_a), sb_ref)

    @pl.when(jnp.logical_not(odd))
    def _():
        slc_finish(state, sa_ref)

    o_slc = [oslc_ref[g] for g in groups]

    for g in groups:
        gates = jax.nn.sigmoid(gt_ref[0, g])

        def gate_row(br):
            return jnp.concatenate([gates[3 * j + br:3 * j + br + 1, :] for j in range(HPG)],
                                   axis=1)

        o = gate_row(0) * o_cmp[g] + gate_row(1) * o_slc[g] + gate_row(2) * o_win[g]
        for j in range(HPG):
            h0 = (g * HPG + j) * HEAD_DIM
            y_ref[0, h0:h0 + HEAD_DIM, :] = o[:, j * tq:(j + 1) * tq].astype(y_ref.dtype)


def _block_onehot():
    oh = (np.arange(SEQ)[:, None] // SLC_LEN) == np.arange(LANES)[None, :]
    return jnp.asarray(oh.astype(np.float32), dtype=BF16)


def _nsa(q, kc, vct, kv, vst, vwt, gt, ovt):
    nq = SEQ // NSA_TQ
    return pl.pallas_call(
        _nsa_kernel,
        grid=(BATCH, nq),
        in_specs=[pl.BlockSpec((NSA_TQ, NSA_Q), lambda b, i: (b * nq + i, 0)),
                  pl.BlockSpec((1, N_CHUNK, NSA_KV), lambda b, i: (b, 0, 0)),
                  pl.BlockSpec((1, NSA_KV, N_CHUNK), lambda b, i: (b, 0, 0)),
                  pl.BlockSpec((SEQ, NSA_KV), lambda b, i: (b, 2)),
                  pl.BlockSpec((None, NSA_KV, SEQ), lambda b, i: (b, 0, 0)),
                  pl.BlockSpec((SEQ, NSA_KV), lambda b, i: (b, 4)),
                  pl.BlockSpec((None, NSA_KV, SEQ), lambda b, i: (b, 0, 0)),
                  pl.BlockSpec((1, N_KV, GATE_ROWS, NSA_TQ), lambda b, i: (b, 0, 0, i)),
                  pl.BlockSpec((N_SLC, N_CHUNK), lambda b, i: (0, 0)),
                  pl.BlockSpec((SEQ, LANES), lambda b, i: (0, 0))],
        out_specs=pl.BlockSpec((1, NSA_Q, NSA_TQ), lambda b, i: (b, 0, i)),
        out_shape=jax.ShapeDtypeStruct((BATCH, NSA_Q, SEQ), BF16),
        scratch_shapes=[pltpu.VMEM((N_KV, SLC_KT, HPG * NSA_TQ), F32),
                        pltpu.VMEM((N_KV, SLC_KT, HPG * NSA_TQ), F32),
                        pltpu.VMEM((N_KV, HEAD_DIM, HPG * NSA_TQ), F32)],
        compiler_params=_cparams(("arbitrary", "arbitrary")),
        name="nsa",
    )(q, kc, vct, kv, vst, kv, vwt, gt, ovt, _block_onehot())


def _overlap_t():
    c_start = np.arange(N_CHUNK) * CMP_STRIDE
    s_start = np.arange(N_SLC) * SLC_LEN
    ov = ((c_start[None, :] <= s_start[:, None] + SLC_LEN - 1)
          & (c_start[None, :] + CMP_LEN - 1 >= s_start[:, None])
          & (np.arange(N_CHUNK)[None, :] < N_CMP))
    return jnp.asarray(ov.astype(np.float32), dtype=BF16)


def _merge_kernel(x_ref, g_ref, wmg_ref, ya_ref, yb_ref, yct_ref, yd_ref, wb_ref, wo_ref, o_ref):
    x = x_ref[...]
    nb = _rms(x, g_ref[...]).astype(BF16)
    yc = yct_ref[0].astype(F32).T.astype(BF16)
    ys = (ya_ref[...], yb_ref[...], yc, yd_ref[...])
    merged = jnp.zeros((MERGE_TM, D_MODEL), F32)
    for bi, y in enumerate(ys):
        gate = jax.nn.sigmoid(jnp.dot(nb, wmg_ref[:, bi * D_MODEL:(bi + 1) * D_MODEL],
                                      preferred_element_type=F32))
        merged = merged + gate * jnp.dot(y, wb_ref[bi], preferred_element_type=F32)
    o_ref[...] = x + jnp.dot(merged.astype(BF16), wo_ref[...], preferred_element_type=F32)


def _merge(x, g, wmg, ya, yb, yct, yd, wb, wo, layer):
    nt = SEQ // MERGE_TM
    tok = lambda i: (i, 0)
    const2 = lambda i: (0, 0)
    return pl.pallas_call(
        _merge_kernel,
        grid=(TOKENS // MERGE_TM,),
        in_specs=[pl.BlockSpec((MERGE_TM, D_MODEL), tok),
                  pl.BlockSpec((1, D_MODEL), const2),
                  pl.BlockSpec((None, D_MODEL, N_BRANCH * D_MODEL), lambda i: (layer, 0, 0)),
                  pl.BlockSpec((MERGE_TM, MIX), tok),
                  pl.BlockSpec((MERGE_TM, MIX), tok),
                  pl.BlockSpec((1, MIX, MERGE_TM), lambda i: (i // nt, 0, i % nt)),
                  pl.BlockSpec((MERGE_TM, MIX), tok),
                  pl.BlockSpec((None, N_BRANCH, MIX, D_MODEL), lambda i: (layer, 0, 0, 0)),
                  pl.BlockSpec((None, D_MODEL, D_MODEL), lambda i: (layer, 0, 0))],
        out_specs=pl.BlockSpec((MERGE_TM, D_MODEL), tok),
        out_shape=jax.ShapeDtypeStruct((TOKENS, D_MODEL), F32),
        compiler_params=_cparams(("arbitrary",)),
        name="merge",
    )(x, g, wmg, ya, yb, yct, yd, wb, wo)


def _router_logits(t, wr_ref, br_ref):
    w = wr_ref[...]
    t_hi, w_hi = t.astype(BF16), w.astype(BF16)
    t_lo = (t - t_hi.astype(F32)).astype(BF16)
    w_lo = (w - w_hi.astype(F32)).astype(BF16)
    dot = functools.partial(jnp.dot, preferred_element_type=F32)
    return dot(t_hi, w_hi) + dot(t_hi, w_lo) + dot(t_lo, w_hi) + br_ref[...]


def _top_group(logits):
    lane = lax.broadcasted_iota(jnp.int32, logits.shape, 1)
    is_grp = lane < N_GROUPS
    lg = jnp.where(is_grp, logits, NEG_INF)
    gmax = jnp.max(lg, axis=1, keepdims=True)
    grp_idx = jnp.min(jnp.where(is_grp & (lg == gmax), lane, ROUTER_PAD), axis=1, keepdims=True)
    return lg, gmax, grp_idx


def _group_weight(logits, grp_idx):
    lane = lax.broadcasted_iota(jnp.int32, logits.shape, 1)
    lg, gmax, _ = _top_group(logits)
    ge = jnp.where(lane < N_GROUPS, jnp.exp(lg - gmax), 0.0)
    return (jnp.sum(jnp.where(lane == grp_idx, ge, 0.0), axis=1, keepdims=True)
            / jnp.sum(ge, axis=1, keepdims=True))


ROUTE_ROWS = 32


def _route_kernel(x_ref, g_ref, wr_ref, br_ref, tri_ref, meta_ref, cnt_ref, run_ref):
    @pl.when(pl.program_id(0) == 0)
    def _():
        run_ref[...] = jnp.zeros_like(run_ref)

    logits = _router_logits(_rms(x_ref[...], g_ref[...]), wr_ref, br_ref)
    lt = logits.T[:ROUTE_ROWS, :]
    row = lax.broadcasted_iota(jnp.int32, (ROUTE_ROWS, 1), 0)

    def top(mask, vals):
        v = jnp.where(mask, vals, NEG_INF)
        best = jnp.max(v, axis=0, keepdims=True)
        return v, jnp.min(jnp.where(mask & (v == best), row, ROUTE_ROWS), axis=0, keepdims=True)

    _, grp_idx = top(row < N_GROUPS, lt)
    first = N_GROUPS + grp_idx * EXPERTS_PER_GROUP
    in_grp = (row >= first) & (row < first + EXPERTS_PER_GROUP)
    le, i1 = top(in_grp, lt)
    _, i2 = top(in_grp & (row != i1), le)
    lo = jnp.minimum(i1, i2) - first
    hi = jnp.maximum(i1, i2) - first
    pair = jnp.right_shift(lo * (2 * EXPERTS_PER_GROUP - 1 - lo), 1) + hi - lo - 1
    cls = grp_idx * N_PAIRS + pair
    onehot = jnp.where(row == cls, 1.0, 0.0)
    before = jnp.dot(onehot.astype(BF16), tri_ref[...], preferred_element_type=F32) + run_ref[...]
    rank = jnp.sum(onehot * before, axis=0, keepdims=True).astype(jnp.int32)
    run_ref[...] += jnp.sum(onehot, axis=1, keepdims=True)
    sub = lax.broadcasted_iota(jnp.int32, (SUBLANES, 1), 0)
    meta_ref[...] = jnp.where(sub == 0, cls, jnp.where(sub == 1, rank, 0))
    cnt_ref[...] = jnp.broadcast_to(run_ref[...], cnt_ref.shape)


def _route(x, g, wr, br):
    tok = lambda i: (i, 0)
    const2 = lambda i: (0, 0)
    tri = jnp.asarray(np.triu(np.ones((ROUTE_TM, ROUTE_TM), np.float32), 1), dtype=BF16)
    return pl.pallas_call(
        _route_kernel,
        grid=(TOKENS // ROUTE_TM,),
        in_specs=[pl.BlockSpec((ROUTE_TM, D_MODEL), tok),
                  pl.BlockSpec((1, D_MODEL), const2),
                  pl.BlockSpec((D_MODEL, ROUTER_PAD), const2),
                  pl.BlockSpec((1, ROUTER_PAD), const2),
                  pl.BlockSpec((ROUTE_TM, ROUTE_TM), const2)],
        out_specs=[pl.BlockSpec((SUBLANES, ROUTE_TM), lambda i: (0, i)),
                   pl.BlockSpec((ROUTE_ROWS, LANES), const2)],
        out_shape=[jax.ShapeDtypeStruct((SUBLANES, TOKENS), jnp.int32),
                   jax.ShapeDtypeStruct((ROUTE_ROWS, LANES), F32)],
        scratch_shapes=[pltpu.VMEM((ROUTE_ROWS, 1), F32)],
        compiler_params=_cparams(("arbitrary",)),
        name="route",
    )(x, g, wr, br, tri)


MOVE_SLOTS = 3


def _move_rows_kernel(idx_ref, seg_ref, len_ref, src_hbm, dst_hbm, buf, zbuf, in_sem, out_sem,
                      pad_sem, *, scatter):
    c = pl.program_id(0)
    n = pl.num_programs(0)
    slot = lax.rem(c, MOVE_SLOTS)
    nxt = lax.rem(c + 1, MOVE_SLOTS)
    rc = ROW_CHUNK
    chunk = lambda k: pl.ds(k * rc, rc)

    if scatter:
        def read(k, sl):
            return [pltpu.make_async_copy(src_hbm.at[chunk(k)], buf.at[sl], in_sem.at[sl])]

        def write(k, sl):
            return [pltpu.make_async_copy(buf.at[sl, pl.ds(r, 1)],
                                          dst_hbm.at[pl.ds(idx_ref[k * rc + r], 1)], out_sem.at[sl])
                    for r in range(rc)]

        def write_done(sl):
            return pltpu.make_async_copy(buf.at[sl], dst_hbm.at[chunk(0)], out_sem.at[sl])

        def read_done(sl):
            return read(0, sl)[0]
    else:
        def read(k, sl):
            return [pltpu.make_async_copy(src_hbm.at[pl.ds(idx_ref[k * rc + r], 1)],
                                          buf.at[sl, pl.ds(r, 1)], in_sem.at[sl])
                    for r in range(rc)]

        def write(k, sl):
            return [pltpu.make_async_copy(buf.at[sl], dst_hbm.at[chunk(k)], out_sem.at[sl])]

        def write_done(sl):
            return write(0, sl)[0]

        def read_done(sl):
            return pltpu.make_async_copy(src_hbm.at[chunk(0)], buf.at[sl], in_sem.at[sl])

    def pad_fills(fn):
        for sg in range(N_CLASSES):
            start, length = seg_ref[sg], len_ref[sg]
            head = jnp.bitwise_and(-start, SUBLANES - 1)
            for r in range(SUBLANES - 1):
                @pl.when(r < head)
                def _():
                    fn(pltpu.make_async_copy(zbuf.at[pl.ds(0, 1)], dst_hbm.at[pl.ds(start + r, 1)],
                                             pad_sem))
            body = length - head
            bit = rc // 2
            while bit >= SUBLANES:
                done = body - jnp.bitwise_and(body, 2 * bit - 1)
                first = pl.multiple_of(start + head + done, SUBLANES)

                @pl.when(jnp.bitwise_and(body, bit) != 0)
                def _():
                    fn(pltpu.make_async_copy(zbuf.at[pl.ds(0, bit)], dst_hbm.at[pl.ds(first, bit)],
                                             pad_sem))
                bit //= 2
        tail = pl.multiple_of(seg_ref[N_CLASSES], rc)

        def tail_tile(k, carry):
            fn(pltpu.make_async_copy(zbuf, dst_hbm.at[pl.ds(tail + k * rc, rc)], pad_sem))
            return carry
        lax.fori_loop(0, len_ref[N_CLASSES] // rc, tail_tile, 0)

    def start_all(copies):
        for r, cp in enumerate(copies):
            cp.start(priority=r % 2)

    @pl.when(c == 0)
    def _():
        start_all(read(0, 0))
        if scatter:
            zbuf[...] = jnp.zeros_like(zbuf)
            pad_fills(lambda cp: cp.start())

    @pl.when(c + 1 < n)
    def _():
        @pl.when(c >= 2)
        def _():
            write_done(nxt).wait()
        start_all(read(c + 1, nxt))

    read_done(slot).wait()
    start_all(write(c, slot))

    @pl.when(c == n - 1)
    def _():
        write_done(nxt).wait()
        write_done(lax.rem(c + 2, MOVE_SLOTS)).wait()
        write_done(slot).wait()
        if scatter:
            pad_fills(lambda cp: cp.wait())


def _move_rows(idx, seg_start, seg_len, src, n_out, scatter):
    any_spec = pl.BlockSpec(memory_space=pl.ANY)
    grid_spec = pltpu.PrefetchScalarGridSpec(
        num_scalar_prefetch=3, grid=(TOKENS // ROW_CHUNK,), in_specs=[any_spec], out_specs=any_spec,
        scratch_shapes=[pltpu.VMEM((MOVE_SLOTS, ROW_CHUNK, D_MODEL), F32),
                        pltpu.VMEM((ROW_CHUNK, D_MODEL), F32),
                        pltpu.SemaphoreType.DMA((MOVE_SLOTS,)),
                        pltpu.SemaphoreType.DMA((MOVE_SLOTS,)),
                        pltpu.SemaphoreType.DMA(())])
    return pl.pallas_call(
        functools.partial(_move_rows_kernel, scatter=scatter),
        grid_spec=grid_spec,
        out_shape=jax.ShapeDtypeStruct((n_out, D_MODEL), F32),
        compiler_params=_cparams(("arbitrary",)),
        name="dispatch" if scatter else "collect",
    )(idx, seg_start, seg_len, src)


def _experts_kernel(ea_ref, eb_ref, tv_ref, tb_ref, x_ref, g_ref, wr_ref, br_ref, wga_ref, wua_ref,
                    wda_ref, wgb_ref, wub_ref, wdb_ref, fg_ref, o_ref, *, layer, final_norm):
    j = pl.program_id(0)
    nv = tv_ref[j]

    @pl.when(nv > 0)
    def _():
        x = x_ref[...]
        t = _rms(x, g_ref[...])
        logits = _router_logits(t, wr_ref, br_ref)
        lane = lax.broadcasted_iota(jnp.int32, logits.shape, 1)
        ea = ea_ref[j] - layer * N_EXPERTS
        eb = eb_ref[j] - layer * N_EXPERTS
        la = jnp.sum(jnp.where(lane == N_GROUPS + ea, logits, 0.0), axis=1, keepdims=True)
        lb = jnp.sum(jnp.where(lane == N_GROUPS + eb, logits, 0.0), axis=1, keepdims=True)
        top = jnp.maximum(la, lb)
        pa, pb = jnp.exp(la - top), jnp.exp(lb - top)
        scale = _group_weight(logits, jnp.right_shift(ea, 2)) / (pa + pb)
        tb = t.astype(BF16)
        acc = jnp.zeros((MOE_TM, D_MODEL), F32)
        for wg, wu, wd, w in ((wga_ref, wua_ref, wda_ref, pa * scale),
                              (wgb_ref, wub_ref, wdb_ref, pb * scale)):
            hid = (jax.nn.silu(jnp.dot(tb, wg[0], preferred_element_type=F32))
                   * jnp.dot(tb, wu[0], preferred_element_type=F32)) * w
            acc = acc + jnp.dot(hid.astype(BF16), wd[0], preferred_element_type=F32)
        h = x + acc
        if final_norm:
            h = _rms(h, fg_ref[...])
        o_ref[...] = h

    @pl.when(nv == 0)
    def _():
        o_ref[...] = jnp.zeros_like(o_ref)


def _experts(tile_ea, tile_eb, tile_valid, tile_block, hs, g, wr, br, wg, wu, wd, fg, layer,
             final_norm):
    const2 = lambda j, *_: (0, 0)
    of_a = lambda j, ea, eb, tv, tb: (ea[j], 0, 0)
    of_b = lambda j, ea, eb, tv, tb: (eb[j], 0, 0)
    up = lambda idx: pl.BlockSpec((1, D_MODEL, D_EXPERT), idx)
    down = lambda idx: pl.BlockSpec((1, D_EXPERT, D_MODEL), idx)
    grid_spec = pltpu.PrefetchScalarGridSpec(
        num_scalar_prefetch=4,
        grid=(MOE_NT,),
        in_specs=[pl.BlockSpec((MOE_TM, D_MODEL), lambda j, ea, eb, tv, tb: (tb[j], 0)),
                  pl.BlockSpec((1, D_MODEL), const2),
                  pl.BlockSpec((D_MODEL, ROUTER_PAD), const2),
                  pl.BlockSpec((1, ROUTER_PAD), const2),
                  up(of_a), up(of_a), down(of_a), up(of_b), up(of_b), down(of_b),
                  pl.BlockSpec((1, D_MODEL), const2)],
        out_specs=pl.BlockSpec((MOE_TM, D_MODEL), lambda j, *_: (j, 0)))
    return pl.pallas_call(
        functools.partial(_experts_kernel, layer=layer, final_norm=final_norm),
        grid_spec=grid_spec,
        out_shape=jax.ShapeDtypeStruct((MOE_ROWS, D_MODEL), F32),
        compiler_params=_cparams(("arbitrary",)),
        name="experts",
    )(tile_ea, tile_eb, tile_valid, tile_block, hs, g, wr, br, wg, wu, wd, wg, wu, wd, fg)


_PAIR_LO = np.array([a for a in range(EXPERTS_PER_GROUP) for b in range(a + 1, EXPERTS_PER_GROUP)])
_PAIR_HI = np.array([b for a in range(EXPERTS_PER_GROUP) for b in range(a + 1, EXPERTS_PER_GROUP)])


def _moe(x, g, wr, br, wg, wu, wd, fg, layer, final_norm):
    meta, cnt = _route(x, g, wr, br)
    cls, rank = meta[0], meta[1]
    counts = cnt[:N_CLASSES, 0].astype(jnp.int32)
    padded = (counts + MOE_TM - 1) // MOE_TM * MOE_TM
    ends = jnp.cumsum(padded)
    starts = ends - padded
    pos = starts[cls] + rank
    tile_start = jnp.arange(MOE_NT, dtype=jnp.int32) * MOE_TM
    tile_cls = jnp.minimum(jnp.sum(tile_start[:, None] >= ends[None, :], axis=1), N_CLASSES - 1)
    tile_valid = jnp.clip(starts[tile_cls] + counts[tile_cls] - tile_start, 0, MOE_TM)
    tile_valid = tile_valid.astype(jnp.int32)
    tile_block = jnp.where(tile_valid > 0, jnp.arange(MOE_NT, dtype=jnp.int32), 0)
    first = layer * N_EXPERTS + (tile_cls // N_PAIRS) * EXPERTS_PER_GROUP
    tile_ea = (first + jnp.asarray(_PAIR_LO)[tile_cls % N_PAIRS]).astype(jnp.int32)
    tile_eb = (first + jnp.asarray(_PAIR_HI)[tile_cls % N_PAIRS]).astype(jnp.int32)
    seg_start = jnp.concatenate([starts + counts, ends[-1:]])
    seg_len = jnp.concatenate([padded - counts, MOE_ROWS - ends[-1:]])
    hs = _move_rows(pos, seg_start, seg_len, x, MOE_ROWS, scatter=True)
    ys = _experts(tile_ea, tile_eb, tile_valid, tile_block, hs, g, wr, br, wg, wu, wd, fg, layer,
                  final_norm)
    return _move_rows(pos, seg_start, seg_len, ys, TOKENS, scatter=False)


def _block_diag(w):
    eye = jnp.eye(LRU_BLOCKS, dtype=w.dtype)
    return jnp.einsum('hij,hk->hikj', w, eye).reshape(MIX, MIX)


def _w_in_column_ranges():
    cuts = [int(c) for c in np.cumsum((0,) + IN_SPLITS)]
    q_parts = [(cuts[4] + (g * HPG + j) * HEAD_DIM, cuts[4] + (g * HPG + j + 1) * HEAD_DIM)
               for j in range(HPG) for g in range(N_KV)]
    return ([(cuts[8], cuts[9]), (cuts[0], cuts[4])] + q_parts
            + [(cuts[5], cuts[6]), (cuts[7], cuts[8]), (cuts[6], cuts[7])])


def _prep_w_in_kernel(w_ref, o_ref):
    off = 0
    for a, b in _w_in_column_ranges():
        o_ref[:, off:off + b - a] = w_ref[:, a:b].astype(o_ref.dtype)
        off += b - a
    o_ref[:, off:] = jnp.zeros((o_ref.shape[0], o_ref.shape[1] - off), o_ref.dtype)


def _prep_w_in(w_in):
    n_in = w_in.shape[-1]
    return pl.pallas_call(
        _prep_w_in_kernel,
        grid=(DEPTH, D_MODEL // PREP_ROWS),
        in_specs=[pl.BlockSpec((None, PREP_ROWS, n_in), lambda l, r: (l, r, 0))],
        out_specs=pl.BlockSpec((None, PREP_ROWS, 2 * W_IN_HALF), lambda l, r: (l, r, 0)),
        out_shape=jax.ShapeDtypeStruct((DEPTH, D_MODEL, 2 * W_IN_HALF), BF16),
        compiler_params=_cparams(("arbitrary", "arbitrary")),
        name="prep_w_in",
    )(w_in)


def _nsa_mixer(q, kv, ng, p):
    w1 = p['cmp_w1'].reshape(2, N_KV, CMP_LEN, HEAD_DIM, HEAD_DIM).transpose(0, 2, 1, 3, 4)
    pe = jnp.concatenate([p['cmp_pe']] * N_KV, axis=-1)
    kc, vct, vst, vwt, gt = _kvprep(kv, ng, pe, _group_diag(w1).astype(BF16),
                                    _group_diag(p['cmp_w2']).astype(BF16))
    return _nsa(q, kc, vct, kv, vst, vwt, gt, _overlap_t())


def _layer(h, p, big, layer, final_g, final_norm):
    row = lambda a: a.reshape(1, -1)

    u, v, gb, rb, q, kv, xd, ng = _proj(h, row(p['norm1_g']), big['w_in'], layer)

    bs = jnp.broadcast_to(p['gm_b'][:, :, None], (GM_GROUPS, GM_CHUNK, GM_GW))
    y_a = _gmlp(u, v, row(p['gm_norm_g']), p['gm_ws'], bs)

    y_b = _lru(gb, rb, p['conv_w'], row(p['conv_b']), _block_diag(p['lru_wa']).astype(BF16),
               row(p['lru_ba']), _block_diag(p['lru_wx']).astype(BF16), row(p['lru_bx']),
               row(p['lru_lambda']))

    y_d = _pool(xd, p['pool_w'].astype(BF16), row(p['pool_scale']))

    y_ct = _nsa_mixer(q, kv, ng, p)

    h = _merge(h, row(p['norm1_g']), big['w_in'], y_a, y_b, y_ct, y_d, big['w_branch'],
               big['w_out'], layer)

    wr = jnp.concatenate([p['router_w_group'], p['router_w_expert']], axis=1)
    wr = jnp.pad(wr, ((0, 0), (0, ROUTER_PAD - wr.shape[1])))
    br = jnp.concatenate([p['router_b_group'], p['router_b_expert']])
    br = jnp.pad(br, (0, ROUTER_PAD - br.shape[0])).reshape(1, ROUTER_PAD)
    return _moe(h, row(p['norm2_g']), wr, br, big['moe_w_gate'], big['moe_w_up'],
                big['moe_w_down'], row(final_g), layer, final_norm)


_LAYER_PARAMS = ('norm1_g', 'gm_norm_g', 'gm_ws', 'gm_b', 'conv_w', 'conv_b', 'lru_wa',
                 'lru_ba', 'lru_wx', 'lru_bx', 'lru_lambda', 'cmp_pe', 'cmp_w1', 'cmp_w2', 'pool_w',
                 'pool_scale', 'norm2_g', 'router_w_group', 'router_b_group',
                 'router_w_expert', 'router_b_expert')


def kernel(x, norm1_g, w_in, gm_norm_g, gm_ws, gm_b, conv_w, conv_b, lru_wa, lru_ba, lru_wx,
           lru_bx, lru_lambda, cmp_pe, cmp_w1, cmp_w2, pool_w, pool_scale, w_branch, w_out,
           norm2_g, router_w_group, router_b_group, router_w_expert, router_b_expert,
           moe_w_gate, moe_w_up, moe_w_down, final_norm_g):
    stacked = dict(zip(_LAYER_PARAMS, (
        norm1_g, gm_norm_g, gm_ws, gm_b, conv_w, conv_b, lru_wa, lru_ba, lru_wx, lru_bx,
        lru_lambda, cmp_pe, cmp_w1, cmp_w2, pool_w, pool_scale, norm2_g,
        router_w_group, router_b_group, router_w_expert, router_b_expert)))
    w_all = _prep_w_in(w_in)
    experts = lambda w: w.astype(BF16).reshape((DEPTH * N_EXPERTS,) + w.shape[2:])
    big = dict(w_in=w_all, w_branch=w_branch.astype(BF16), w_out=w_out.astype(BF16),
               moe_w_gate=experts(moe_w_gate), moe_w_up=experts(moe_w_up),
               moe_w_down=experts(moe_w_down))
    h = x.reshape(TOKENS, D_MODEL)
    for layer in range(DEPTH):
        p = {k: a[layer] for k, a in stacked.items()}
        h = _layer(h, p, big, layer, final_norm_g, final_norm=(layer == DEPTH - 1))
    return h.reshape(BATCH, SEQ, D_MODEL)
```

```python
import functools

import numpy as np
import jax
import jax.numpy as jnp
from jax import lax
from jax.experimental import pallas as pl
from jax.experimental.pallas import tpu as pltpu

F32 = jnp.float32
BF16 = jnp.bfloat16

D_MODEL = 1024
BATCH = 4
SEQ = 4096
TOKENS = BATCH * SEQ
DEPTH = 2
MIX = D_MODEL // 2
GM_CHUNK = 128
GM_GROUPS = 4
GM_GW = MIX // GM_GROUPS
CONV_WIDTH = 4
LRU_BLOCKS = 8
LRU_BW = MIX // LRU_BLOCKS
LRU_C = 8.0
N_HEADS = 8
HEAD_DIM = MIX // N_HEADS
N_KV = 2
HPG = N_HEADS // N_KV
CMP_LEN = 32
CMP_STRIDE = 16
SLC_LEN = 64
SLC_TOPN = 8
MAX_FORCED = 3
WIN = 512
NSA_Q = N_HEADS * HEAD_DIM
NSA_KV = N_KV * HEAD_DIM
POOL_WINDOWS = (2, 4, 8, 16)
POOL_GW = MIX // len(POOL_WINDOWS)
N_BRANCH = 4
N_GROUPS = 4
EXPERTS_PER_GROUP = 4
N_EXPERTS = N_GROUPS * EXPERTS_PER_GROUP
D_EXPERT = D_MODEL // 2
EPS = 1e-6
NEG_INF = -1e30
FORCE_SCORE = 1e6
IN_SPLITS = (MIX, MIX, MIX, MIX, NSA_Q, 6 * NSA_KV, 3 * N_HEADS, MIX, N_BRANCH * D_MODEL)

N_CHUNK = SEQ // CMP_STRIDE
N_CMP = N_CHUNK - CMP_LEN // CMP_STRIDE + 1
N_SLC = SEQ // SLC_LEN

LANES = 128
SUBLANES = 8
GATE_PAD = LANES
GATE_ROWS = 16
ROUTER_PAD = LANES
VMEM_LIMIT = 56 * 1024 * 1024

PROJ_WIDTHS = (MIX, MIX, MIX, MIX, NSA_Q, 6 * NSA_KV, MIX, GATE_PAD)
W_IN_HALF = N_BRANCH * D_MODEL
PROJ_TM = 512
GMLP_TM = 512
LRU_TS = 512
POOL_TS = 512
NSA_TQ = 256
SLC_KT = 256
WIN_KEYS = WIN + NSA_TQ
MERGE_TM = 256
ROUTE_TM = 512
MOE_TM = 256
N_PAIRS = EXPERTS_PER_GROUP * (EXPERTS_PER_GROUP - 1) // 2
N_CLASSES = N_GROUPS * N_PAIRS
MOE_ROWS = TOKENS + N_CLASSES * MOE_TM
MOE_NT = MOE_ROWS // MOE_TM
ROW_CHUNK = 256


def _cparams(sem):
    return pltpu.CompilerParams(dimension_semantics=sem, vmem_limit_bytes=VMEM_LIMIT)


def _rms(x, g):
    return x * lax.rsqrt(jnp.mean(x * x, axis=-1, keepdims=True) + EPS) * g


def _proj_kernel(x_ref, g_ref, w_ref, *out_refs):
    nb = _rms(x_ref[...], g_ref[...]).astype(BF16)
    off = 0
    for ref in out_refs:
        w = ref.shape[-1]
        ref[...] = jnp.dot(nb, w_ref[:, off:off + w], preferred_element_type=F32).astype(ref.dtype)
        off += w


def _proj(x, g, w, layer):
    out_shape = [jax.ShapeDtypeStruct((TOKENS, wd), BF16) for wd in PROJ_WIDTHS[:-1]]
    out_shape.append(jax.ShapeDtypeStruct((TOKENS, GATE_PAD), F32))
    return pl.pallas_call(
        _proj_kernel,
        grid=(TOKENS // PROJ_TM,),
        in_specs=[pl.BlockSpec((PROJ_TM, D_MODEL), lambda i: (i, 0)),
                  pl.BlockSpec((1, D_MODEL), lambda i: (0, 0)),
                  pl.BlockSpec((None, D_MODEL, W_IN_HALF), lambda i: (layer, 0, 1))],
        out_specs=[pl.BlockSpec((PROJ_TM, wd), lambda i: (i, 0)) for wd in PROJ_WIDTHS],
        out_shape=out_shape,
        compiler_params=_cparams(("arbitrary",)),
        name="proj",
    )(x, g, w)


def _gmlp_kernel(u_ref, v_ref, g_ref, ws_ref, bs_ref, o_ref):
    u = jax.nn.gelu(u_ref[...].astype(F32))
    v = _rms(jax.nn.gelu(v_ref[...].astype(F32)), g_ref[...]).astype(BF16)
    row = lax.broadcasted_iota(jnp.int32, (GM_CHUNK, GM_CHUNK), 0)
    col = lax.broadcasted_iota(jnp.int32, (GM_CHUNK, GM_CHUNK), 1)
    causal = row >= col
    for gi in range(GM_GROUPS):
        w = jnp.where(causal, ws_ref[gi], 0.0).astype(BF16)
        cs = slice(gi * GM_GW, (gi + 1) * GM_GW)
        for c in range(GMLP_TM // GM_CHUNK):
            rs = slice(c * GM_CHUNK, (c + 1) * GM_CHUNK)
            mixed = jnp.dot(w, v[rs, cs], preferred_element_type=F32) + bs_ref[gi]
            o_ref[rs, cs] = (u[rs, cs] * mixed).astype(o_ref.dtype)


def _gmlp(u, v, g, ws, bs):
    tok = lambda i: (i, 0)
    return pl.pallas_call(
        _gmlp_kernel,
        grid=(TOKENS // GMLP_TM,),
        in_specs=[pl.BlockSpec((GMLP_TM, MIX), tok),
                  pl.BlockSpec((GMLP_TM, MIX), tok),
                  pl.BlockSpec((1, MIX), lambda i: (0, 0)),
                  pl.BlockSpec((GM_GROUPS, GM_CHUNK, GM_CHUNK), lambda i: (0, 0, 0)),
                  pl.BlockSpec((GM_GROUPS, GM_CHUNK, GM_GW), lambda i: (0, 0, 0))],
        out_specs=pl.BlockSpec((GMLP_TM, MIX), tok),
        out_shape=jax.ShapeDtypeStruct((TOKENS, MIX), BF16),
        compiler_params=_cparams(("arbitrary",)),
        name="gmlp",
    )(u, v, g, ws, bs)


LRU_TAIL = 8


def _lru_kernel(gb_ref, rb_ref, cw_ref, cb_ref, wa_ref, ba_ref, wx_ref, bx_ref, lam_ref, o_ref,
                tail_ref, h_ref):
    @pl.when(pl.program_id(1) == 0)
    def _():
        tail_ref[...] = jnp.zeros_like(tail_ref)
        h_ref[...] = jnp.zeros_like(h_ref)

    ts = LRU_TS
    x = rb_ref[...].astype(F32)
    ext = jnp.concatenate([tail_ref[...], x], axis=0)
    tail_ref[...] = x[ts - LRU_TAIL:, :]
    xc = cb_ref[...] + x * cw_ref[CONV_WIDTH - 1:CONV_WIDTH, :]
    for d in range(1, CONV_WIDTH):
        xs = pltpu.roll(ext, d, axis=0)[LRU_TAIL:, :]
        xc = xc + xs * cw_ref[CONV_WIDTH - 1 - d:CONV_WIDTH - d, :]
    xcb = xc.astype(BF16)
    r = jax.nn.sigmoid(jnp.dot(xcb, wa_ref[...], preferred_element_type=F32) + ba_ref[...])
    ig = jax.nn.sigmoid(jnp.dot(xcb, wx_ref[...], preferred_element_type=F32) + bx_ref[...])
    z = -lam_ref[...]
    softplus = jnp.maximum(z, 0.0) + jnp.log1p(jnp.exp(-jnp.abs(z)))
    log_a = -LRU_C * r * softplus
    a = jnp.exp(log_a)
    b = jnp.sqrt(1.0 - jnp.exp(2.0 * log_a)) * (ig * xc)
    rows = lax.broadcasted_iota(jnp.int32, (ts, 1), 0)
    d = 1
    while d < ts:
        valid = rows >= d
        a_prev = pltpu.roll(a, d, axis=0)
        b_prev = pltpu.roll(b, d, axis=0)
        b = jnp.where(valid, a * b_prev, 0.0) + b
        a = jnp.where(valid, a * a_prev, a)
        d *= 2
    h = a * h_ref[...] + b
    h_ref[...] = h[ts - 1:ts, :]
    o_ref[...] = (jax.nn.gelu(gb_ref[...].astype(F32)) * h).astype(o_ref.dtype)


def _lru(gb, rb, cw, cb, wa, ba, wx, bx, lam):
    nt = SEQ // LRU_TS
    tok = lambda b, s: (b * nt + s, 0)
    vec = pl.BlockSpec((1, MIX), lambda b, s: (0, 0))
    mat = pl.BlockSpec((MIX, MIX), lambda b, s: (0, 0))
    return pl.pallas_call(
        _lru_kernel,
        grid=(BATCH, nt),
        in_specs=[pl.BlockSpec((LRU_TS, MIX), tok), pl.BlockSpec((LRU_TS, MIX), tok),
                  pl.BlockSpec((CONV_WIDTH, MIX), lambda b, s: (0, 0)), vec,
                  mat, vec, mat, vec, vec],
        out_specs=pl.BlockSpec((LRU_TS, MIX), tok),
        out_shape=jax.ShapeDtypeStruct((TOKENS, MIX), BF16),
        scratch_shapes=[pltpu.VMEM((LRU_TAIL, MIX), F32), pltpu.VMEM((1, MIX), F32)],
        compiler_params=_cparams(("arbitrary", "arbitrary")),
        name="rglru",
    )(gb, rb, cw, cb, wa, ba, wx, bx, lam)


POOL_TAIL = 16


def _pool_kernel(x_ref, w_ref, sc_ref, o_ref, tail_ref):
    s_id = pl.program_id(1)

    @pl.when(s_id == 0)
    def _():
        tail_ref[...] = jnp.zeros_like(tail_ref)

    ts = POOL_TS
    x = x_ref[...].astype(F32)
    ext = jnp.concatenate([tail_ref[...], x], axis=0)
    tail_ref[...] = x[ts - POOL_TAIL:, :]
    pos = s_id * ts + lax.broadcasted_iota(jnp.int32, (ts, 1), 0)
    acc = ext
    width = 1
    for gi, wdw in enumerate(POOL_WINDOWS):
        while width < wdw:
            acc = acc + pltpu.roll(acc, width, axis=0)
            width *= 2
        cs = slice(gi * POOL_GW, (gi + 1) * POOL_GW)
        cnt = jnp.minimum(pos + 1, wdw).astype(F32)
        pooled = acc[POOL_TAIL:, cs] / cnt - x[:, cs]
        mixed = jnp.dot(pooled.astype(BF16), w_ref[gi], preferred_element_type=F32)
        o_ref[:, cs] = (mixed * sc_ref[:, cs]).astype(o_ref.dtype)


def _pool(xd, w, sc):
    nt = SEQ // POOL_TS
    tok = lambda b, s: (b * nt + s, 0)
    return pl.pallas_call(
        _pool_kernel,
        grid=(BATCH, nt),
        in_specs=[pl.BlockSpec((POOL_TS, MIX), tok),
                  pl.BlockSpec((len(POOL_WINDOWS), POOL_GW, POOL_GW), lambda b, s: (0, 0, 0)),
                  pl.BlockSpec((1, MIX), lambda b, s: (0, 0))],
        out_specs=pl.BlockSpec((POOL_TS, MIX), tok),
        out_shape=jax.ShapeDtypeStruct((TOKENS, MIX), BF16),
        scratch_shapes=[pltpu.VMEM((POOL_TAIL, MIX), F32)],
        compiler_params=_cparams(("arbitrary", "arbitrary")),
        name="pool",
    )(xd, w, sc)


def _kvprep_kernel(kc_in, vc_in, vs_in, vw_in, ng_in, pe_ref, w1_ref, w2_ref, kc_ref, vct_ref,
                   vst_ref, vwt_ref, gt_ref, xf_ref):
    row = lax.broadcasted_iota(jnp.int32, (N_CHUNK, 1), 0)
    for c, x_in in enumerate((kc_in, vc_in)):
        xf_ref[...] = x_in[...].astype(F32)
        first = jnp.zeros((N_CHUNK, NSA_KV), F32)
        second = jnp.zeros((N_CHUNK, NSA_KV), F32)
        for l in range(CMP_STRIDE):
            xl = xf_ref[pl.ds(l, N_CHUNK, stride=CMP_STRIDE), :]
            first += jnp.dot((xl + pe_ref[c, l:l + 1, :]).astype(BF16), w1_ref[c, l],
                             preferred_element_type=F32)
            l2 = l + CMP_STRIDE
            second += jnp.dot((xl + pe_ref[c, l2:l2 + 1, :]).astype(BF16), w1_ref[c, l2],
                              preferred_element_type=F32)
        hid = jax.nn.gelu(first + pltpu.roll(second, N_CHUNK - 1, axis=0))
        out = jnp.dot(hid.astype(BF16), w2_ref[c], preferred_element_type=F32)
        out = jnp.where(row < N_CMP, out, 0.0)
        if c == 0:
            kc_ref[0] = out.astype(kc_ref.dtype)
        else:
            vct_ref[0] = out.T.astype(vct_ref.dtype)
    vst_ref[0] = vs_in[...].astype(F32).T.astype(vst_ref.dtype)
    vwt_ref[0] = vw_in[...].astype(F32).T.astype(vwt_ref.dtype)
    gates_t = ng_in[...].T
    gt_ref[...] = jnp.zeros_like(gt_ref)
    per_group = 3 * HPG
    for g in range(N_KV):
        gt_ref[0, g, :per_group, :] = gates_t[g * per_group:(g + 1) * per_group, :]


def _kvprep(kv, ng, pe, w1, w2):
    col = lambda j: pl.BlockSpec((SEQ, NSA_KV), lambda b: (b, j))
    whole = lambda a: pl.BlockSpec(a.shape, lambda b: (0,) * a.ndim)
    return pl.pallas_call(
        _kvprep_kernel,
        grid=(BATCH,),
        in_specs=[col(0), col(1), col(3), col(5), pl.BlockSpec((SEQ, GATE_PAD), lambda b: (b, 0)),
                  whole(pe), whole(w1), whole(w2)],
        out_specs=[pl.BlockSpec((1, N_CHUNK, NSA_KV), lambda b: (b, 0, 0)),
                   pl.BlockSpec((1, NSA_KV, N_CHUNK), lambda b: (b, 0, 0)),
                   pl.BlockSpec((1, NSA_KV, SEQ), lambda b: (b, 0, 0)),
                   pl.BlockSpec((1, NSA_KV, SEQ), lambda b: (b, 0, 0)),
                   pl.BlockSpec((1, N_KV, GATE_ROWS, SEQ), lambda b: (b, 0, 0, 0))],
        out_shape=[jax.ShapeDtypeStruct((BATCH, N_CHUNK, NSA_KV), BF16),
                   jax.ShapeDtypeStruct((BATCH, NSA_KV, N_CHUNK), BF16),
                   jax.ShapeDtypeStruct((BATCH, NSA_KV, SEQ), BF16),
                   jax.ShapeDtypeStruct((BATCH, NSA_KV, SEQ), BF16),
                   jax.ShapeDtypeStruct((BATCH, N_KV, GATE_ROWS, SEQ), F32)],
        scratch_shapes=[pltpu.VMEM((SEQ, NSA_KV), F32)],
        compiler_params=_cparams(("arbitrary",)),
        name="kvprep",
    )(kv, kv, kv, kv, ng, pe, w1, w2)


def _group_diag(w):
    eye = jnp.eye(N_KV, dtype=w.dtype)
    out = jnp.einsum('...gde,gh->...gdhe', w, eye)
    return out.reshape(w.shape[:-3] + (N_KV * w.shape[-2], N_KV * w.shape[-1]))


def _lane_tile(x, n):
    return jnp.concatenate([x] * n, axis=1)


ONES_ROWS = 16


def _with_ones(v_t):
    return jnp.concatenate([v_t, jnp.ones((ONES_ROWS, v_t.shape[1]), BF16)], axis=0)


def _nsa_kernel(q_ref, kc_ref, vct_ref, ks_ref, vst_ref, kw_ref, vwt_ref, gt_ref, ovt_ref, blk_ref,
                y_ref, sa_ref, sb_ref, oslc_ref):
    i = pl.program_id(1)
    tq = NSA_TQ
    nl = HPG * tq
    t_row = i * tq + lax.broadcasted_iota(jnp.int32, (1, tq), 1)
    groups = range(N_KV)
    vrows = [slice(g * HEAD_DIM, (g + 1) * HEAD_DIM) for g in groups]

    def scores(k, q_t):
        return jnp.dot(k, q_t, preferred_element_type=F32)

    def normalized(acc):
        return acc[:HEAD_DIM, :] / acc[HEAD_DIM:HEAD_DIM + 1, :]

    lane = lax.broadcasted_iota(jnp.int32, (tq, LANES), 1)
    scale = HEAD_DIM ** -0.5
    q4 = [jnp.concatenate(
        [(jnp.where((lane >= HEAD_DIM) == (g == 1),
                    q_ref[:, j * LANES:(j + 1) * LANES].astype(F32), 0.0) * scale).T
         for j in range(HPG)], axis=1).astype(BF16) for g in groups]

    n_idx = lax.broadcasted_iota(jnp.int32, (N_CHUNK, 1), 0)
    ok = n_idx * CMP_STRIDE + (CMP_LEN - 1) <= _lane_tile(t_row, HPG)
    blk = lax.broadcasted_iota(jnp.int32, (N_SLC, 1), 0)
    cur = jnp.right_shift(t_row, 6)
    forced = (blk == 0) | (blk == cur) | (blk == cur - 1)
    causal_blk = blk * SLC_LEN <= t_row
    o_cmp, q_aug = [], []
    for g in groups:
        s = jnp.where(ok, scores(kc_ref[0], q4[g]), NEG_INF)
        m = jnp.max(s, axis=0, keepdims=True)
        e = jnp.where(ok, jnp.exp(s - m), 0.0)
        l = jnp.sum(e, axis=0, keepdims=True)
        p = e / jnp.where(l > 0.0, l, 1.0)
        o_cmp.append(jnp.dot(vct_ref[0, vrows[g], :], p.astype(BF16), preferred_element_type=F32))
        psum = p[:, 0:tq]
        for j in range(1, HPG):
            psum = psum + p[:, j * tq:(j + 1) * tq]
        p_hi = psum.astype(BF16)
        p_lo = (psum - p_hi.astype(F32)).astype(BF16)
        imp = (jnp.dot(ovt_ref[...], p_hi, preferred_element_type=F32)
               + jnp.dot(ovt_ref[...], p_lo, preferred_element_type=F32))
        work = jnp.where(forced, -3e38, jnp.where(causal_blk, imp, -1.0))
        sel = jnp.where(forced, 1.0, 0.0)
        for _ in range(SLC_TOPN - MAX_FORCED):
            mx = jnp.max(work, axis=0, keepdims=True)
            idx = jnp.min(jnp.where(work == mx, blk, N_SLC), axis=0, keepdims=True)
            pick = blk == idx
            sel = jnp.where(pick & (mx >= 0.0), 1.0, sel)
            work = jnp.where(pick, -3e38, work)
        sel_bias = jnp.where(sel > 0.0, 0.0, NEG_INF)
        sel_bias = jnp.concatenate([sel_bias, jnp.zeros((LANES - N_SLC, tq), F32)], axis=0)
        sel_bias = _lane_tile(sel_bias.astype(BF16), HPG)
        q_aug.append(jnp.concatenate([q4[g], sel_bias], axis=0))

    start = pl.multiple_of(jnp.maximum(i - WIN // tq, 0) * tq, tq)
    k_win = kw_ref[pl.ds(start, WIN_KEYS), :]
    delta = t_row - (start + lax.broadcasted_iota(jnp.int32, (WIN_KEYS, 1), 0))
    win_bias = _lane_tile(jnp.where((delta >= 0) & (delta < WIN), 0.0, NEG_INF), HPG)
    o_win = []
    for g in groups:
        sT = scores(k_win, q4[g]) + win_bias
        pT = jnp.exp((sT - jnp.max(sT, axis=0, keepdims=True)).astype(BF16))
        vT = _with_ones(vwt_ref[vrows[g], pl.ds(start, WIN_KEYS)])
        o_win.append(normalized(jnp.dot(vT, pT, preferred_element_type=F32)))

    def slc_scores(kt, dst_ref):
        k0 = pl.multiple_of(kt * SLC_KT, SLC_KT)
        k_aug = jnp.concatenate([ks_ref[pl.ds(k0, SLC_KT), :], blk_ref[pl.ds(k0, SLC_KT), :]],
                                axis=1)
        maxima = []
        for g in groups:
            s_t = scores(k_aug, q_aug[g])
            dst_ref[g] = s_t
            maxima.append(jnp.max(s_t, axis=0, keepdims=True))
        return maxima

    def slc_update(kt, state, s_t, tile_max):
        k0 = pl.multiple_of(kt * SLC_KT, SLC_KT)
        out = []
        for g in groups:
            m_i, acc = state[2 * g], state[2 * g + 1]
            m_new = jnp.maximum(m_i, tile_max[g])
            alpha = jnp.exp(m_i - m_new)
            pT = jnp.exp((s_t[g] - m_new).astype(BF16))
            vT = _with_ones(vst_ref[vrows[g], pl.ds(k0, SLC_KT)])
            out += [m_new, alpha * acc + jnp.dot(vT, pT, preferred_element_type=F32)]
        return out

    def from_ref(src_ref):
        return [src_ref[g] for g in groups]

    def slc_pair(p, carry):
        state, max_a = list(carry[:2 * N_KV]), list(carry[2 * N_KV:])
        max_b = slc_scores(2 * p + 1, sb_ref)
        state = slc_update(2 * p, state, from_ref(sa_ref), max_a)
        max_a = slc_scores(2 * p + 2, sa_ref)
        state = slc_update(2 * p + 1, state, from_ref(sb_ref), max_b)
        return tuple(state + max_a)

    def slc_finish(state, src_ref):
        kpos = last_kt * SLC_KT + lax.broadcasted_iota(jnp.int32, (SLC_KT, 1), 0)
        causal_bias = _lane_tile(jnp.where(kpos <= t_row, 0.0, NEG_INF), HPG)
        s_last = [s + causal_bias for s in from_ref(src_ref)]
        state = slc_update(last_kt, state, s_last,
                           [jnp.max(s, axis=0, keepdims=True) for s in s_last])
        for g in groups:
            oslc_ref[g] = normalized(state[2 * g + 1])

    last_kt = (i * tq) // SLC_KT
    init = ((jnp.full((1, nl), -3e38, F32), jnp.zeros((HEAD_DIM + ONES_ROWS, nl), F32)) * N_KV
            + tuple(slc_scores(0, sa_ref)))
    carry = lax.fori_loop(0, last_kt // 2, slc_pair, init)
    state, max_a = list(carry[:2 * N_KV]), list(carry[2 * N_KV:])
    odd = lax.rem(last_kt, 2) == 1

    @pl.when(odd)
    def _():
        slc_scores(last_kt, sb_ref)
        slc_finish(slc_update(last_kt - 1, state, from_ref(sa_ref), max_a), sb_ref)

    @pl.when(jnp.logical_not(odd))
    def _():
        slc_finish(state, sa_ref)

    o_slc = [oslc_ref[g] for g in groups]

    for g in groups:
        gates = jax.nn.sigmoid(gt_ref[0, g])

        def gate_row(br):
            return jnp.concatenate([gates[3 * j + br:3 * j + br + 1, :] for j in range(HPG)],
                                   axis=1)

        o = gate_row(0) * o_cmp[g] + gate_row(1) * o_slc[g] + gate_row(2) * o_win[g]
        for j in range(HPG):
            h0 = (g * HPG + j) * HEAD_DIM
            y_ref[0, h0:h0 + HEAD_DIM, :] = o[:, j * tq:(j + 1) * tq].astype(y_ref.dtype)


def _block_onehot():
    oh = (np.arange(SEQ)[:, None] // SLC_LEN) == np.arange(LANES)[None, :]
    return jnp.asarray(oh.astype(np.float32), dtype=BF16)


def _nsa(q, kc, vct, kv, vst, vwt, gt, ovt):
    nq = SEQ // NSA_TQ
    return pl.pallas_call(
        _nsa_kernel,
        grid=(BATCH, nq),
        in_specs=[pl.BlockSpec((NSA_TQ, NSA_Q), lambda b, i: (b * nq + i, 0)),
                  pl.BlockSpec((1, N_CHUNK, NSA_KV), lambda b, i: (b, 0, 0)),
                  pl.BlockSpec((1, NSA_KV, N_CHUNK), lambda b, i: (b, 0, 0)),
                  pl.BlockSpec((SEQ, NSA_KV), lambda b, i: (b, 2)),
                  pl.BlockSpec((None, NSA_KV, SEQ), lambda b, i: (b, 0, 0)),
                  pl.BlockSpec((SEQ, NSA_KV), lambda b, i: (b, 4)),
                  pl.BlockSpec((None, NSA_KV, SEQ), lambda b, i: (b, 0, 0)),
                  pl.BlockSpec((1, N_KV, GATE_ROWS, NSA_TQ), lambda b, i: (b, 0, 0, i)),
                  pl.BlockSpec((N_SLC, N_CHUNK), lambda b, i: (0, 0)),
                  pl.BlockSpec((SEQ, LANES), lambda b, i: (0, 0))],
        out_specs=pl.BlockSpec((1, NSA_Q, NSA_TQ), lambda b, i: (b, 0, i)),
        out_shape=jax.ShapeDtypeStruct((BATCH, NSA_Q, SEQ), BF16),
        scratch_shapes=[pltpu.VMEM((N_KV, SLC_KT, HPG * NSA_TQ), F32),
                        pltpu.VMEM((N_KV, SLC_KT, HPG * NSA_TQ), F32),
                        pltpu.VMEM((N_KV, HEAD_DIM, HPG * NSA_TQ), F32)],
        compiler_params=_cparams(("arbitrary", "arbitrary")),
        name="nsa",
    )(q, kc, vct, kv, vst, kv, vwt, gt, ovt, _block_onehot())


def _overlap_t():
    c_start = np.arange(N_CHUNK) * CMP_STRIDE
    s_start = np.arange(N_SLC) * SLC_LEN
    ov = ((c_start[None, :] <= s_start[:, None] + SLC_LEN - 1)
          & (c_start[None, :] + CMP_LEN - 1 >= s_start[:, None])
          & (np.arange(N_CHUNK)[None, :] < N_CMP))
    return jnp.asarray(ov.astype(np.float32), dtype=BF16)


def _merge_kernel(x_ref, g_ref, wmg_ref, ya_ref, yb_ref, yct_ref, yd_ref, wb_ref, wo_ref, o_ref):
    x = x_ref[...]
    nb = _rms(x, g_ref[...]).astype(BF16)
    yc = yct_ref[0].astype(F32).T.astype(BF16)
    ys = (ya_ref[...], yb_ref[...], yc, yd_ref[...])
    merged = jnp.zeros((MERGE_TM, D_MODEL), F32)
    for bi, y in enumerate(ys):
        gate = jax.nn.sigmoid(jnp.dot(nb, wmg_ref[:, bi * D_MODEL:(bi + 1) * D_MODEL],
                                      preferred_element_type=F32))
        merged = merged + gate * jnp.dot(y, wb_ref[bi], preferred_element_type=F32)
    o_ref[...] = x + jnp.dot(merged.astype(BF16), wo_ref[...], preferred_element_type=F32)


def _merge(x, g, wmg, ya, yb, yct, yd, wb, wo, layer):
    nt = SEQ // MERGE_TM
    tok = lambda i: (i, 0)
    const2 = lambda i: (0, 0)
    return pl.pallas_call(
        _merge_kernel,
        grid=(TOKENS // MERGE_TM,),
        in_specs=[pl.BlockSpec((MERGE_TM, D_MODEL), tok),
                  pl.BlockSpec((1, D_MODEL), const2),
                  pl.BlockSpec((None, D_MODEL, N_BRANCH * D_MODEL), lambda i: (layer, 0, 0)),
                  pl.BlockSpec((MERGE_TM, MIX), tok),
                  pl.BlockSpec((MERGE_TM, MIX), tok),
                  pl.BlockSpec((1, MIX, MERGE_TM), lambda i: (i // nt, 0, i % nt)),
                  pl.BlockSpec((MERGE_TM, MIX), tok),
                  pl.BlockSpec((None, N_BRANCH, MIX, D_MODEL), lambda i: (layer, 0, 0, 0)),
                  pl.BlockSpec((None, D_MODEL, D_MODEL), lambda i: (layer, 0, 0))],
        out_specs=pl.BlockSpec((MERGE_TM, D_MODEL), tok),
        out_shape=jax.ShapeDtypeStruct((TOKENS, D_MODEL), F32),
        compiler_params=_cparams(("arbitrary",)),
        name="merge",
    )(x, g, wmg, ya, yb, yct, yd, wb, wo)


def _router_logits(t, wr_ref, br_ref):
    w = wr_ref[...]
    t_hi, w_hi = t.astype(BF16), w.astype(BF16)
    t_lo = (t - t_hi.astype(F32)).astype(BF16)
    w_lo = (w - w_hi.astype(F32)).astype(BF16)
    dot = functools.partial(jnp.dot, preferred_element_type=F32)
    return dot(t_hi, w_hi) + dot(t_hi, w_lo) + dot(t_lo, w_hi) + br_ref[...]


def _top_group(logits):
    lane = lax.broadcasted_iota(jnp.int32, logits.shape, 1)
    is_grp = lane < N_GROUPS
    lg = jnp.where(is_grp, logits, NEG_INF)
    gmax = jnp.max(lg, axis=1, keepdims=True)
    grp_idx = jnp.min(jnp.where(is_grp & (lg == gmax), lane, ROUTER_PAD), axis=1, keepdims=True)
    return lg, gmax, grp_idx


def _group_weight(logits, grp_idx):
    lane = lax.broadcasted_iota(jnp.int32, logits.shape, 1)
    lg, gmax, _ = _top_group(logits)
    ge = jnp.where(lane < N_GROUPS, jnp.exp(lg - gmax), 0.0)
    return (jnp.sum(jnp.where(lane == grp_idx, ge, 0.0), axis=1, keepdims=True)
            / jnp.sum(ge, axis=1, keepdims=True))


ROUTE_ROWS = 32


def _route_kernel(x_ref, g_ref, wr_ref, br_ref, tri_ref, meta_ref, cnt_ref, run_ref):
    @pl.when(pl.program_id(0) == 0)
    def _():
        run_ref[...] = jnp.zeros_like(run_ref)

    logits = _router_logits(_rms(x_ref[...], g_ref[...]), wr_ref, br_ref)
    lt = logits.T[:ROUTE_ROWS, :]
    row = lax.broadcasted_iota(jnp.int32, (ROUTE_ROWS, 1), 0)

    def top(mask, vals):
        v = jnp.where(mask, vals, NEG_INF)
        best = jnp.max(v, axis=0, keepdims=True)
        return v, jnp.min(jnp.where(mask & (v == best), row, ROUTE_ROWS), axis=0, keepdims=True)

    _, grp_idx = top(row < N_GROUPS, lt)
    first = N_GROUPS + grp_idx * EXPERTS_PER_GROUP
    in_grp = (row >= first) & (row < first + EXPERTS_PER_GROUP)
    le, i1 = top(in_grp, lt)
    _, i2 = top(in_grp & (row != i1), le)
    lo = jnp.minimum(i1, i2) - first
    hi = jnp.maximum(i1, i2) - first
    pair = jnp.right_shift(lo * (2 * EXPERTS_PER_GROUP - 1 - lo), 1) + hi - lo - 1
    cls = grp_idx * N_PAIRS + pair
    onehot = jnp.where(row == cls, 1.0, 0.0)
    before = jnp.dot(onehot.astype(BF16), tri_ref[...], preferred_element_type=F32) + run_ref[...]
    rank = jnp.sum(onehot * before, axis=0, keepdims=True).astype(jnp.int32)
    run_ref[...] += jnp.sum(onehot, axis=1, keepdims=True)
    sub = lax.broadcasted_iota(jnp.int32, (SUBLANES, 1), 0)
    meta_ref[...] = jnp.where(sub == 0, cls, jnp.where(sub == 1, rank, 0))
    cnt_ref[...] = jnp.broadcast_to(run_ref[...], cnt_ref.shape)


def _route(x, g, wr, br):
    tok = lambda i: (i, 0)
    const2 = lambda i: (0, 0)
    tri = jnp.asarray(np.triu(np.ones((ROUTE_TM, ROUTE_TM), np.float32), 1), dtype=BF16)
    return pl.pallas_call(
        _route_kernel,
        grid=(TOKENS // ROUTE_TM,),
        in_specs=[pl.BlockSpec((ROUTE_TM, D_MODEL), tok),
                  pl.BlockSpec((1, D_MODEL), const2),
                  pl.BlockSpec((D_MODEL, ROUTER_PAD), const2),
                  pl.BlockSpec((1, ROUTER_PAD), const2),
                  pl.BlockSpec((ROUTE_TM, ROUTE_TM), const2)],
        out_specs=[pl.BlockSpec((SUBLANES, ROUTE_TM), lambda i: (0, i)),
                   pl.BlockSpec((ROUTE_ROWS, LANES), const2)],
        out_shape=[jax.ShapeDtypeStruct((SUBLANES, TOKENS), jnp.int32),
                   jax.ShapeDtypeStruct((ROUTE_ROWS, LANES), F32)],
        scratch_shapes=[pltpu.VMEM((ROUTE_ROWS, 1), F32)],
        compiler_params=_cparams(("arbitrary",)),
        name="route",
    )(x, g, wr, br, tri)


MOVE_SLOTS = 3


def _move_rows_kernel(idx_ref, seg_ref, len_ref, src_hbm, dst_hbm, buf, zbuf, in_sem, out_sem,
                      pad_sem, *, scatter):
    c = pl.program_id(0)
    n = pl.num_programs(0)
    slot = lax.rem(c, MOVE_SLOTS)
    nxt = lax.rem(c + 1, MOVE_SLOTS)
    rc = ROW_CHUNK
    chunk = lambda k: pl.ds(k * rc, rc)

    if scatter:
        def read(k, sl):
            return [pltpu.make_async_copy(src_hbm.at[chunk(k)], buf.at[sl], in_sem.at[sl])]

        def write(k, sl):
            return [pltpu.make_async_copy(buf.at[sl, pl.ds(r, 1)],
                                          dst_hbm.at[pl.ds(idx_ref[k * rc + r], 1)], out_sem.at[sl])
                    for r in range(rc)]

        def write_done(sl):
            return pltpu.make_async_copy(buf.at[sl], dst_hbm.at[chunk(0)], out_sem.at[sl])

        def read_done(sl):
            return read(0, sl)[0]
    else:
        def read(k, sl):
            return [pltpu.make_async_copy(src_hbm.at[pl.ds(idx_ref[k * rc + r], 1)],
                                          buf.at[sl, pl.ds(r, 1)], in_sem.at[sl])
                    for r in range(rc)]

        def write(k, sl):
            return [pltpu.make_async_copy(buf.at[sl], dst_hbm.at[chunk(k)], out_sem.at[sl])]

        def write_done(sl):
            return write(0, sl)[0]

        def read_done(sl):
            return pltpu.make_async_copy(src_hbm.at[chunk(0)], buf.at[sl], in_sem.at[sl])

    def pad_fills(fn):
        for sg in range(N_CLASSES):
            start, length = seg_ref[sg], len_ref[sg]
            head = jnp.bitwise_and(-start, SUBLANES - 1)
            for r in range(SUBLANES - 1):
                @pl.when(r < head)
                def _():
                    fn(pltpu.make_async_copy(zbuf.at[pl.ds(0, 1)], dst_hbm.at[pl.ds(start + r, 1)],
                                             pad_sem))
            body = length - head
            bit = rc // 2
            while bit >= SUBLANES:
                done = body - jnp.bitwise_and(body, 2 * bit - 1)
                first = pl.multiple_of(start + head + done, SUBLANES)

                @pl.when(jnp.bitwise_and(body, bit) != 0)
                def _():
                    fn(pltpu.make_async_copy(zbuf.at[pl.ds(0, bit)], dst_hbm.at[pl.ds(first, bit)],
                                             pad_sem))
                bit //= 2
        tail = pl.multiple_of(seg_ref[N_CLASSES], rc)

        def tail_tile(k, carry):
            fn(pltpu.make_async_copy(zbuf, dst_hbm.at[pl.ds(tail + k * rc, rc)], pad_sem))
            return carry
        lax.fori_loop(0, len_ref[N_CLASSES] // rc, tail_tile, 0)

    @pl.when(c == 0)
    def _():
        for cp in read(0, 0):
            cp.start()
        if scatter:
            zbuf[...] = jnp.zeros_like(zbuf)
            pad_fills(lambda cp: cp.start())

    @pl.when(c + 1 < n)
    def _():
        @pl.when(c >= 2)
        def _():
            write_done(nxt).wait()
        for cp in read(c + 1, nxt):
            cp.start()

    read_done(slot).wait()
    for cp in write(c, slot):
        cp.start()

    @pl.when(c == n - 1)
    def _():
        write_done(nxt).wait()
        write_done(lax.rem(c + 2, MOVE_SLOTS)).wait()
        write_done(slot).wait()
        if scatter:
            pad_fills(lambda cp: cp.wait())


def _move_rows(idx, seg_start, seg_len, src, n_out, scatter):
    any_spec = pl.BlockSpec(memory_space=pl.ANY)
    grid_spec = pltpu.PrefetchScalarGridSpec(
        num_scalar_prefetch=3, grid=(TOKENS // ROW_CHUNK,), in_specs=[any_spec], out_specs=any_spec,
        scratch_shapes=[pltpu.VMEM((MOVE_SLOTS, ROW_CHUNK, D_MODEL), F32),
                        pltpu.VMEM((ROW_CHUNK, D_MODEL), F32),
                        pltpu.SemaphoreType.DMA((MOVE_SLOTS,)),
                        pltpu.SemaphoreType.DMA((MOVE_SLOTS,)),
                        pltpu.SemaphoreType.DMA(())])
    return pl.pallas_call(
        functools.partial(_move_rows_kernel, scatter=scatter),
        grid_spec=grid_spec,
        out_shape=jax.ShapeDtypeStruct((n_out, D_MODEL), F32),
        compiler_params=_cparams(("arbitrary",)),
        name="dispatch" if scatter else "collect",
    )(idx, seg_start, seg_len, src)


def _experts_kernel(ea_ref, eb_ref, tv_ref, tb_ref, x_ref, g_ref, wr_ref, br_ref, wga_ref, wua_ref,
                    wda_ref, wgb_ref, wub_ref, wdb_ref, fg_ref, o_ref, *, layer, final_norm):
    j = pl.program_id(0)
    nv = tv_ref[j]

    @pl.when(nv > 0)
    def _():
        x = x_ref[...]
        t = _rms(x, g_ref[...])
        logits = _router_logits(t, wr_ref, br_ref)
        lane = lax.broadcasted_iota(jnp.int32, logits.shape, 1)
        ea = ea_ref[j] - layer * N_EXPERTS
        eb = eb_ref[j] - layer * N_EXPERTS
        la = jnp.sum(jnp.where(lane == N_GROUPS + ea, logits, 0.0), axis=1, keepdims=True)
        lb = jnp.sum(jnp.where(lane == N_GROUPS + eb, logits, 0.0), axis=1, keepdims=True)
        top = jnp.maximum(la, lb)
        pa, pb = jnp.exp(la - top), jnp.exp(lb - top)
        scale = _group_weight(logits, jnp.right_shift(ea, 2)) / (pa + pb)
        tb = t.astype(BF16)
        acc = jnp.zeros((MOE_TM, D_MODEL), F32)
        for wg, wu, wd, w in ((wga_ref, wua_ref, wda_ref, pa * scale),
                              (wgb_ref, wub_ref, wdb_ref, pb * scale)):
            hid = (jax.nn.silu(jnp.dot(tb, wg[0], preferred_element_type=F32))
                   * jnp.dot(tb, wu[0], preferred_element_type=F32)) * w
            acc = acc + jnp.dot(hid.astype(BF16), wd[0], preferred_element_type=F32)
        h = x + acc
        if final_norm:
            h = _rms(h, fg_ref[...])
        o_ref[...] = h

    @pl.when(nv == 0)
    def _():
        o_ref[...] = jnp.zeros_like(o_ref)


def _experts(tile_ea, tile_eb, tile_valid, tile_block, hs, g, wr, br, wg, wu, wd, fg, layer,
             final_norm):
    const2 = lambda j, *_: (0, 0)
    of_a = lambda j, ea, eb, tv, tb: (ea[j], 0, 0)
    of_b = lambda j, ea, eb, tv, tb: (eb[j], 0, 0)
    up = lambda idx: pl.BlockSpec((1, D_MODEL, D_EXPERT), idx)
    down = lambda idx: pl.BlockSpec((1, D_EXPERT, D_MODEL), idx)
    grid_spec = pltpu.PrefetchScalarGridSpec(
        num_scalar_prefetch=4,
        grid=(MOE_NT,),
        in_specs=[pl.BlockSpec((MOE_TM, D_MODEL), lambda j, ea, eb, tv, tb: (tb[j], 0)),
                  pl.BlockSpec((1, D_MODEL), const2),
                  pl.BlockSpec((D_MODEL, ROUTER_PAD), const2),
                  pl.BlockSpec((1, ROUTER_PAD), const2),
                  up(of_a), up(of_a), down(of_a), up(of_b), up(of_b), down(of_b),
                  pl.BlockSpec((1, D_MODEL), const2)],
        out_specs=pl.BlockSpec((MOE_TM, D_MODEL), lambda j, *_: (j, 0)))
    return pl.pallas_call(
        functools.partial(_experts_kernel, layer=layer, final_norm=final_norm),
        grid_spec=grid_spec,
        out_shape=jax.ShapeDtypeStruct((MOE_ROWS, D_MODEL), F32),
        compiler_params=_cparams(("arbitrary",)),
        name="experts",
    )(tile_ea, tile_eb, tile_valid, tile_block, hs, g, wr, br, wg, wu, wd, wg, wu, wd, fg)


_PAIR_LO = np.array([a for a in range(EXPERTS_PER_GROUP) for b in range(a + 1, EXPERTS_PER_GROUP)])
_PAIR_HI = np.array([b for a in range(EXPERTS_PER_GROUP) for b in range(a + 1, EXPERTS_PER_GROUP)])


def _moe(x, g, wr, br, wg, wu, wd, fg, layer, final_norm):
    meta, cnt = _route(x, g, wr, br)
    cls, rank = meta[0], meta[1]
    counts = cnt[:N_CLASSES, 0].astype(jnp.int32)
    padded = (counts + MOE_TM - 1) // MOE_TM * MOE_TM
    ends = jnp.cumsum(padded)
    starts = ends - padded
    pos = starts[cls] + rank
    tile_start = jnp.arange(MOE_NT, dtype=jnp.int32) * MOE_TM
    tile_cls = jnp.minimum(jnp.sum(tile_start[:, None] >= ends[None, :], axis=1), N_CLASSES - 1)
    tile_valid = jnp.clip(starts[tile_cls] + counts[tile_cls] - tile_start, 0, MOE_TM)
    tile_valid = tile_valid.astype(jnp.int32)
    tile_block = jnp.where(tile_valid > 0, jnp.arange(MOE_NT, dtype=jnp.int32), 0)
    first = layer * N_EXPERTS + (tile_cls // N_PAIRS) * EXPERTS_PER_GROUP
    tile_ea = (first + jnp.asarray(_PAIR_LO)[tile_cls % N_PAIRS]).astype(jnp.int32)
    tile_eb = (first + jnp.asarray(_PAIR_HI)[tile_cls % N_PAIRS]).astype(jnp.int32)
    seg_start = jnp.concatenate([starts + counts, ends[-1:]])
    seg_len = jnp.concatenate([padded - counts, MOE_ROWS - ends[-1:]])
    hs = _move_rows(pos, seg_start, seg_len, x, MOE_ROWS, scatter=True)
    ys = _experts(tile_ea, tile_eb, tile_valid, tile_block, hs, g, wr, br, wg, wu, wd, fg, layer,
                  final_norm)
    return _move_rows(pos, seg_start, seg_len, ys, TOKENS, scatter=False)


def _block_diag(w):
    eye = jnp.eye(LRU_BLOCKS, dtype=w.dtype)
    return jnp.einsum('hij,hk->hikj', w, eye).reshape(MIX, MIX)


def _w_in_column_ranges():
    cuts = [int(c) for c in np.cumsum((0,) + IN_SPLITS)]
    q_parts = [(cuts[4] + (g * HPG + j) * HEAD_DIM, cuts[4] + (g * HPG + j + 1) * HEAD_DIM)
               for j in range(HPG) for g in range(N_KV)]
    return ([(cuts[8], cuts[9]), (cuts[0], cuts[4])] + q_parts
            + [(cuts[5], cuts[6]), (cuts[7], cuts[8]), (cuts[6], cuts[7])])


PREP_PIECE = 64
PREP_BLOCK = 512


def _prep_piece_starts():
    starts = []
    for a, b in _w_in_column_ranges():
        starts += list(range(a, b, PREP_PIECE))
    starts += [0] * (2 * W_IN_HALF // PREP_PIECE - len(starts))
    return np.asarray(starts, np.int32)


def _prep_w_in_kernel(src_ref, wt_hbm, o_ref, buf, sem):
    layer, j = pl.program_id(0), pl.program_id(1)
    nj = pl.num_programs(1)
    per_block = PREP_BLOCK // PREP_PIECE
    slot = lax.rem(j, 2)

    def pieces(blk, sl):
        return [pltpu.make_async_copy(
            wt_hbm.at[layer, pl.ds(pl.multiple_of(src_ref[blk * per_block + k], SUBLANES), PREP_PIECE)],
            buf.at[sl, pl.ds(k * PREP_PIECE, PREP_PIECE)], sem.at[sl]) for k in range(per_block)]

    @pl.when(j == 0)
    def _():
        for cp in pieces(0, 0):
            cp.start()

    @pl.when(j + 1 < nj)
    def _():
        for cp in pieces(j + 1, 1 - slot):
            cp.start()

    for cp in pieces(j, slot):
        cp.wait()
    o_ref[...] = buf[slot].T.astype(o_ref.dtype)


def _prep_w_in(w_in):
    grid_spec = pltpu.PrefetchScalarGridSpec(
        num_scalar_prefetch=1,
        grid=(DEPTH, 2 * W_IN_HALF // PREP_BLOCK),
        in_specs=[pl.BlockSpec(memory_space=pl.ANY)],
        out_specs=pl.BlockSpec((None, D_MODEL, PREP_BLOCK), lambda l, j, src: (l, 0, j)),
        scratch_shapes=[pltpu.VMEM((2, PREP_BLOCK, D_MODEL), F32), pltpu.SemaphoreType.DMA((2,))])
    return pl.pallas_call(
        _prep_w_in_kernel,
        grid_spec=grid_spec,
        out_shape=jax.ShapeDtypeStruct((DEPTH, D_MODEL, 2 * W_IN_HALF), BF16),
        compiler_params=_cparams(("arbitrary", "arbitrary")),
        name="prep_w_in",
    )(jnp.asarray(_prep_piece_starts()), w_in.transpose(0, 2, 1))


def _nsa_mixer(q, kv, ng, p):
    w1 = p['cmp_w1'].reshape(2, N_KV, CMP_LEN, HEAD_DIM, HEAD_DIM).transpose(0, 2, 1, 3, 4)
    pe = jnp.concatenate([p['cmp_pe']] * N_KV, axis=-1)
    kc, vct, vst, vwt, gt = _kvprep(kv, ng, pe, _group_diag(w1).astype(BF16),
                                    _group_diag(p['cmp_w2']).astype(BF16))
    return _nsa(q, kc, vct, kv, vst, vwt, gt, _overlap_t())


def _layer(h, p, big, layer, final_g, final_norm):
    row = lambda a: a.reshape(1, -1)

    u, v, gb, rb, q, kv, xd, ng = _proj(h, row(p['norm1_g']), big['w_in'], layer)

    bs = jnp.broadcast_to(p['gm_b'][:, :, None], (GM_GROUPS, GM_CHUNK, GM_GW))
    y_a = _gmlp(u, v, row(p['gm_norm_g']), p['gm_ws'], bs)

    y_b = _lru(gb, rb, p['conv_w'], row(p['conv_b']), _block_diag(p['lru_wa']).astype(BF16),
               row(p['lru_ba']), _block_diag(p['lru_wx']).astype(BF16), row(p['lru_bx']),
               row(p['lru_lambda']))

    y_d = _pool(xd, p['pool_w'].astype(BF16), row(p['pool_scale']))

    y_ct = _nsa_mixer(q, kv, ng, p)

    h = _merge(h, row(p['norm1_g']), big['w_in'], y_a, y_b, y_ct, y_d, big['w_branch'],
               big['w_out'], layer)

    wr = jnp.concatenate([p['router_w_group'], p['router_w_expert']], axis=1)
    wr = jnp.pad(wr, ((0, 0), (0, ROUTER_PAD - wr.shape[1])))
    br = jnp.concatenate([p['router_b_group'], p['router_b_expert']])
    br = jnp.pad(br, (0, ROUTER_PAD - br.shape[0])).reshape(1, ROUTER_PAD)
    return _moe(h, row(p['norm2_g']), wr, br, big['moe_w_gate'], big['moe_w_up'],
                big['moe_w_down'], row(final_g), layer, final_norm)


_LAYER_PARAMS = ('norm1_g', 'gm_norm_g', 'gm_ws', 'gm_b', 'conv_w', 'conv_b', 'lru_wa',
                 'lru_ba', 'lru_wx', 'lru_bx', 'lru_lambda', 'cmp_pe', 'cmp_w1', 'cmp_w2', 'pool_w',
                 'pool_scale', 'norm2_g', 'router_w_group', 'router_b_group',
                 'router_w_expert', 'router_b_expert')


def kernel(x, norm1_g, w_in, gm_norm_g, gm_ws, gm_b, conv_w, conv_b, lru_wa, lru_ba, lru_wx,
           lru_bx, lru_lambda, cmp_pe, cmp_w1, cmp_w2, pool_w, pool_scale, w_branch, w_out,
           norm2_g, router_w_group, router_b_group, router_w_expert, router_b_expert,
           moe_w_gate, moe_w_up, moe_w_down, final_norm_g):
    stacked = dict(zip(_LAYER_PARAMS, (
        norm1_g, gm_norm_g, gm_ws, gm_b, conv_w, conv_b, lru_wa, lru_ba, lru_wx, lru_bx,
        lru_lambda, cmp_pe, cmp_w1, cmp_w2, pool_w, pool_scale, norm2_g,
        router_w_group, router_b_group, router_w_expert, router_b_expert)))
    w_all = _prep_w_in(w_in)
    experts = lambda w: w.astype(BF16).reshape((DEPTH * N_EXPERTS,) + w.shape[2:])
    big = dict(w_in=w_all, w_branch=w_branch.astype(BF16), w_out=w_out.astype(BF16),
               moe_w_gate=experts(moe_w_gate), moe_w_up=experts(moe_w_up),
               moe_w_down=experts(moe_w_down))
    h = x.reshape(TOKENS, D_MODEL)
    for layer in range(DEPTH):
        p = {k: a[layer] for k, a in stacked.items()}
        h = _layer(h, p, big, layer, final_norm_g, final_norm=(layer == DEPTH - 1))
    return h.reshape(BATCH, SEQ, D_MODEL)
```

```python
import functools

import numpy as np
import jax
import jax.numpy as jnp
from jax import lax
from jax.experimental import pallas as pl
from jax.experimental.pallas import tpu as pltpu

F32 = jnp.float32
BF16 = jnp.bfloat16

D_MODEL = 1024
BATCH = 4
SEQ = 4096
TOKENS = BATCH * SEQ
DEPTH = 2
MIX = D_MODEL // 2
GM_CHUNK = 128
GM_GROUPS = 4
GM_GW = MIX // GM_GROUPS
CONV_WIDTH = 4
LRU_BLOCKS = 8
LRU_BW = MIX // LRU_BLOCKS
LRU_C = 8.0
N_HEADS = 8
HEAD_DIM = MIX // N_HEADS
N_KV = 2
HPG = N_HEADS // N_KV
CMP_LEN = 32
CMP_STRIDE = 16
SLC_LEN = 64
SLC_TOPN = 8
MAX_FORCED = 3
WIN = 512
NSA_Q = N_HEADS * HEAD_DIM
NSA_KV = N_KV * HEAD_DIM
POOL_WINDOWS = (2, 4, 8, 16)
POOL_GW = MIX // len(POOL_WINDOWS)
N_BRANCH = 4
N_GROUPS = 4
EXPERTS_PER_GROUP = 4
N_EXPERTS = N_GROUPS * EXPERTS_PER_GROUP
D_EXPERT = D_MODEL // 2
EPS = 1e-6
NEG_INF = -1e30
IN_SPLITS = (MIX, MIX, MIX, MIX, NSA_Q, 6 * NSA_KV, 3 * N_HEADS, MIX, N_BRANCH * D_MODEL)

N_CHUNK = SEQ // CMP_STRIDE
N_CMP = N_CHUNK - CMP_LEN // CMP_STRIDE + 1
N_SLC = SEQ // SLC_LEN

LANES = 128
SUBLANES = 8
GATE_PAD = LANES
GATE_ROWS = 16
ROUTER_PAD = LANES
VMEM_LIMIT = 56 * 1024 * 1024

PROJ_WIDTHS = (MIX, MIX, MIX, MIX, NSA_Q, 6 * NSA_KV, MIX, GATE_PAD)
W_IN_HALF = N_BRANCH * D_MODEL
PROJ_TM = 512
GMLP_TM = 512
LRU_TS = 512
POOL_TS = 512
NSA_TQ = 256
SLC_KT = 256
WIN_KEYS = WIN + NSA_TQ
MERGE_TM = 256
ROUTE_TM = 512
MOE_TM = 256
N_PAIRS = EXPERTS_PER_GROUP * (EXPERTS_PER_GROUP - 1) // 2
N_CLASSES = N_GROUPS * N_PAIRS
MOE_ROWS = TOKENS + N_CLASSES * MOE_TM
MOE_NT = MOE_ROWS // MOE_TM
ROW_CHUNK = 256


def _cparams(sem):
    return pltpu.CompilerParams(dimension_semantics=sem, vmem_limit_bytes=VMEM_LIMIT)


def _rms(x, g):
    return x * lax.rsqrt(jnp.mean(x * x, axis=-1, keepdims=True) + EPS) * g


def _proj_kernel(x_ref, g_ref, w_ref, *out_refs):
    nb = _rms(x_ref[...], g_ref[...]).astype(BF16)
    off = 0
    for ref in out_refs:
        w = ref.shape[-1]
        ref[...] = jnp.dot(nb, w_ref[:, off:off + w], preferred_element_type=F32).astype(ref.dtype)
        off += w


def _proj(x, g, w, layer):
    out_shape = [jax.ShapeDtypeStruct((TOKENS, wd), BF16) for wd in PROJ_WIDTHS[:-1]]
    out_shape.append(jax.ShapeDtypeStruct((TOKENS, GATE_PAD), F32))
    return pl.pallas_call(
        _proj_kernel,
        grid=(TOKENS // PROJ_TM,),
        in_specs=[pl.BlockSpec((PROJ_TM, D_MODEL), lambda i: (i, 0)),
                  pl.BlockSpec((1, D_MODEL), lambda i: (0, 0)),
                  pl.BlockSpec((None, D_MODEL, W_IN_HALF), lambda i: (layer, 0, 1))],
        out_specs=[pl.BlockSpec((PROJ_TM, wd), lambda i: (i, 0)) for wd in PROJ_WIDTHS],
        out_shape=out_shape,
        compiler_params=_cparams(("arbitrary",)),
        name="proj",
    )(x, g, w)


def _gmlp_kernel(u_ref, v_ref, g_ref, ws_ref, bs_ref, o_ref):
    u = jax.nn.gelu(u_ref[...].astype(F32))
    v = _rms(jax.nn.gelu(v_ref[...].astype(F32)), g_ref[...]).astype(BF16)
    row = lax.broadcasted_iota(jnp.int32, (GM_CHUNK, GM_CHUNK), 0)
    col = lax.broadcasted_iota(jnp.int32, (GM_CHUNK, GM_CHUNK), 1)
    causal = row >= col
    for gi in range(GM_GROUPS):
        w = jnp.where(causal, ws_ref[gi], 0.0).astype(BF16)
        cs = slice(gi * GM_GW, (gi + 1) * GM_GW)
        for c in range(GMLP_TM // GM_CHUNK):
            rs = slice(c * GM_CHUNK, (c + 1) * GM_CHUNK)
            mixed = jnp.dot(w, v[rs, cs], preferred_element_type=F32) + bs_ref[gi]
            o_ref[rs, cs] = (u[rs, cs] * mixed).astype(o_ref.dtype)


LRU_TAIL = 8


def _lru_kernel(gb_ref, rb_ref, cw_ref, cb_ref, wa_ref, ba_ref, wx_ref, bx_ref, lam_ref, o_ref,
                tail_ref, h_ref):
    @pl.when(pl.program_id(1) == 0)
    def _():
        tail_ref[...] = jnp.zeros_like(tail_ref)
        h_ref[...] = jnp.zeros_like(h_ref)

    ts = LRU_TS
    x = rb_ref[...].astype(F32)
    ext = jnp.concatenate([tail_ref[...], x], axis=0)
    tail_ref[...] = x[ts - LRU_TAIL:, :]
    xc = cb_ref[...] + x * cw_ref[CONV_WIDTH - 1:CONV_WIDTH, :]
    for d in range(1, CONV_WIDTH):
        xs = pltpu.roll(ext, d, axis=0)[LRU_TAIL:, :]
        xc = xc + xs * cw_ref[CONV_WIDTH - 1 - d:CONV_WIDTH - d, :]
    xcb = xc.astype(BF16)
    r = jax.nn.sigmoid(jnp.dot(xcb, wa_ref[...], preferred_element_type=F32) + ba_ref[...])
    ig = jax.nn.sigmoid(jnp.dot(xcb, wx_ref[...], preferred_element_type=F32) + bx_ref[...])
    z = -lam_ref[...]
    softplus = jnp.maximum(z, 0.0) + jnp.log1p(jnp.exp(-jnp.abs(z)))
    log_a = -LRU_C * r * softplus
    a = jnp.exp(log_a)
    b = jnp.sqrt(1.0 - jnp.exp(2.0 * log_a)) * (ig * xc)
    rows = lax.broadcasted_iota(jnp.int32, (ts, 1), 0)
    d = 1
    while d < ts:
        valid = rows >= d
        a_prev = pltpu.roll(a, d, axis=0)
        b_prev = pltpu.roll(b, d, axis=0)
        b = jnp.where(valid, a * b_prev, 0.0) + b
        a = jnp.where(valid, a * a_prev, a)
        d *= 2
    h = a * h_ref[...] + b
    h_ref[...] = h[ts - 1:ts, :]
    o_ref[...] = (jax.nn.gelu(gb_ref[...].astype(F32)) * h).astype(o_ref.dtype)


POOL_TAIL = 16


def _pool_kernel(x_ref, w_ref, sc_ref, o_ref, tail_ref):
    s_id = pl.program_id(1)

    @pl.when(s_id == 0)
    def _():
        tail_ref[...] = jnp.zeros_like(tail_ref)

    ts = POOL_TS
    x = x_ref[...].astype(F32)
    ext = jnp.concatenate([tail_ref[...], x], axis=0)
    tail_ref[...] = x[ts - POOL_TAIL:, :]
    pos = s_id * ts + lax.broadcasted_iota(jnp.int32, (ts, 1), 0)
    acc = ext
    width = 1
    for gi, wdw in enumerate(POOL_WINDOWS):
        while width < wdw:
            acc = acc + pltpu.roll(acc, width, axis=0)
            width *= 2
        cs = slice(gi * POOL_GW, (gi + 1) * POOL_GW)
        cnt = jnp.minimum(pos + 1, wdw).astype(F32)
        pooled = acc[POOL_TAIL:, cs] / cnt - x[:, cs]
        mixed = jnp.dot(pooled.astype(BF16), w_ref[gi], preferred_element_type=F32)
        o_ref[:, cs] = (mixed * sc_ref[:, cs]).astype(o_ref.dtype)


def _mixers_kernel(u_ref, v_ref, gb_ref, rb_ref, xd_ref, gm_g_ref, gm_ws_ref, gm_bs_ref, cw_ref,
                   cb_ref, wa_ref, ba_ref, wx_ref, bx_ref, lam_ref, pw_ref, psc_ref,
                   ya_ref, yb_ref, yd_ref, lru_tail_ref, lru_h_ref, pool_tail_ref):
    _gmlp_kernel(u_ref, v_ref, gm_g_ref, gm_ws_ref, gm_bs_ref, ya_ref)
    _lru_kernel(gb_ref, rb_ref, cw_ref, cb_ref, wa_ref, ba_ref, wx_ref, bx_ref, lam_ref, yb_ref,
                lru_tail_ref, lru_h_ref)
    _pool_kernel(xd_ref, pw_ref, psc_ref, yd_ref, pool_tail_ref)


def _mixers(u, v, gb, rb, xd, gm_g, gm_ws, gm_bs, cw, cb, wa, ba, wx, bx, lam, pw, psc):
    assert GMLP_TM == LRU_TS == POOL_TS
    nt = SEQ // LRU_TS
    tok = pl.BlockSpec((LRU_TS, MIX), lambda b, s: (b * nt + s, 0))
    whole = lambda a: pl.BlockSpec(a.shape, lambda b, s: (0,) * a.ndim)
    small = (gm_g, gm_ws, gm_bs, cw, cb, wa, ba, wx, bx, lam, pw, psc)
    return pl.pallas_call(
        _mixers_kernel,
        grid=(BATCH, nt),
        in_specs=[tok] * 5 + [whole(a) for a in small],
        out_specs=[tok] * 3,
        out_shape=[jax.ShapeDtypeStruct((TOKENS, MIX), BF16)] * 3,
        scratch_shapes=[pltpu.VMEM((LRU_TAIL, MIX), F32), pltpu.VMEM((1, MIX), F32),
                        pltpu.VMEM((POOL_TAIL, MIX), F32)],
        compiler_params=_cparams(("arbitrary", "arbitrary")),
        name="mixers",
    )(u, v, gb, rb, xd, *small)


def _kvprep_kernel(kc_in, vc_in, vs_in, vw_in, ng_in, pe_ref, w1_ref, w2_ref, kc_ref, vct_ref,
                   vst_ref, vwt_ref, gt_ref, xf_ref):
    row = lax.broadcasted_iota(jnp.int32, (N_CHUNK, 1), 0)
    for c, x_in in enumerate((kc_in, vc_in)):
        xf_ref[...] = x_in[...].astype(F32)
        first = jnp.zeros((N_CHUNK, NSA_KV), F32)
        second = jnp.zeros((N_CHUNK, NSA_KV), F32)
        for l in range(CMP_STRIDE):
            xl = xf_ref[pl.ds(l, N_CHUNK, stride=CMP_STRIDE), :]
            first += jnp.dot((xl + pe_ref[c, l:l + 1, :]).astype(BF16), w1_ref[c, l],
                             preferred_element_type=F32)
            l2 = l + CMP_STRIDE
            second += jnp.dot((xl + pe_ref[c, l2:l2 + 1, :]).astype(BF16), w1_ref[c, l2],
                              preferred_element_type=F32)
        hid = jax.nn.gelu(first + pltpu.roll(second, N_CHUNK - 1, axis=0))
        out = jnp.dot(hid.astype(BF16), w2_ref[c], preferred_element_type=F32)
        out = jnp.where(row < N_CMP, out, 0.0)
        if c == 0:
            kc_ref[0] = out.astype(kc_ref.dtype)
        else:
            vct_ref[0] = out.T.astype(vct_ref.dtype)
    vst_ref[0] = vs_in[...].astype(F32).T.astype(vst_ref.dtype)
    vwt_ref[0] = vw_in[...].astype(F32).T.astype(vwt_ref.dtype)
    gates_t = ng_in[...].T
    gt_ref[...] = jnp.zeros_like(gt_ref)
    per_group = 3 * HPG
    for g in range(N_KV):
        gt_ref[0, g, :per_group, :] = gates_t[g * per_group:(g + 1) * per_group, :]


def _kvprep(kv, ng, pe, w1, w2):
    col = lambda j: pl.BlockSpec((SEQ, NSA_KV), lambda b: (b, j))
    whole = lambda a: pl.BlockSpec(a.shape, lambda b: (0,) * a.ndim)
    return pl.pallas_call(
        _kvprep_kernel,
        grid=(BATCH,),
        in_specs=[col(0), col(1), col(3), col(5), pl.BlockSpec((SEQ, GATE_PAD), lambda b: (b, 0)),
                  whole(pe), whole(w1), whole(w2)],
        out_specs=[pl.BlockSpec((1, N_CHUNK, NSA_KV), lambda b: (b, 0, 0)),
                   pl.BlockSpec((1, NSA_KV, N_CHUNK), lambda b: (b, 0, 0)),
                   pl.BlockSpec((1, NSA_KV, SEQ), lambda b: (b, 0, 0)),
                   pl.BlockSpec((1, NSA_KV, SEQ), lambda b: (b, 0, 0)),
                   pl.BlockSpec((1, N_KV, GATE_ROWS, SEQ), lambda b: (b, 0, 0, 0))],
        out_shape=[jax.ShapeDtypeStruct((BATCH, N_CHUNK, NSA_KV), BF16),
                   jax.ShapeDtypeStruct((BATCH, NSA_KV, N_CHUNK), BF16),
                   jax.ShapeDtypeStruct((BATCH, NSA_KV, SEQ), BF16),
                   jax.ShapeDtypeStruct((BATCH, NSA_KV, SEQ), BF16),
                   jax.ShapeDtypeStruct((BATCH, N_KV, GATE_ROWS, SEQ), F32)],
        scratch_shapes=[pltpu.VMEM((SEQ, NSA_KV), F32)],
        compiler_params=_cparams(("arbitrary",)),
        name="kvprep",
    )(kv, kv, kv, kv, ng, pe, w1, w2)


def _group_diag(w):
    eye = jnp.eye(N_KV, dtype=w.dtype)
    out = jnp.einsum('...gde,gh->...gdhe', w, eye)
    return out.reshape(w.shape[:-3] + (N_KV * w.shape[-2], N_KV * w.shape[-1]))


def _lane_tile(x, n):
    return jnp.concatenate([x] * n, axis=1)


ONES_ROWS = 16


def _with_ones(v_t):
    return jnp.concatenate([v_t, jnp.ones((ONES_ROWS, v_t.shape[1]), BF16)], axis=0)


def _nsa_kernel(q_ref, kc_ref, vct_ref, ks_ref, vst_ref, kw_ref, vwt_ref, gt_ref, ovt_ref, blk_ref,
                y_ref, sa_ref, sb_ref, oslc_ref):
    i = pl.program_id(1)
    tq = NSA_TQ
    nl = HPG * tq
    t_row = i * tq + lax.broadcasted_iota(jnp.int32, (1, tq), 1)
    groups = range(N_KV)
    vrows = [slice(g * HEAD_DIM, (g + 1) * HEAD_DIM) for g in groups]

    def scores(k, q_t):
        return jnp.dot(k, q_t, preferred_element_type=F32)

    def normalized(acc):
        return acc[:HEAD_DIM, :] / acc[HEAD_DIM:HEAD_DIM + 1, :]

    lane = lax.broadcasted_iota(jnp.int32, (tq, LANES), 1)
    scale = HEAD_DIM ** -0.5
    q4 = [jnp.concatenate(
        [(jnp.where((lane >= HEAD_DIM) == (g == 1),
                    q_ref[:, j * LANES:(j + 1) * LANES].astype(F32), 0.0) * scale).T
         for j in range(HPG)], axis=1).astype(BF16) for g in groups]

    n_idx = lax.broadcasted_iota(jnp.int32, (N_CHUNK, 1), 0)
    ok = n_idx * CMP_STRIDE + (CMP_LEN - 1) <= _lane_tile(t_row, HPG)
    blk = lax.broadcasted_iota(jnp.int32, (N_SLC, 1), 0)
    cur = jnp.right_shift(t_row, 6)
    forced = (blk == 0) | (blk == cur) | (blk == cur - 1)
    causal_blk = blk * SLC_LEN <= t_row
    o_cmp, q_aug = [], []
    for g in groups:
        s = jnp.where(ok, scores(kc_ref[0], q4[g]), NEG_INF)
        m = jnp.max(s, axis=0, keepdims=True)
        e = jnp.where(ok, jnp.exp(s - m), 0.0)
        l = jnp.sum(e, axis=0, keepdims=True)
        p = e / jnp.where(l > 0.0, l, 1.0)
        o_cmp.append(jnp.dot(vct_ref[0, vrows[g], :], p.astype(BF16), preferred_element_type=F32))
        psum = p[:, 0:tq]
        for j in range(1, HPG):
            psum = psum + p[:, j * tq:(j + 1) * tq]
        p_hi = psum.astype(BF16)
        p_lo = (psum - p_hi.astype(F32)).astype(BF16)
        imp = (jnp.dot(ovt_ref[...], p_hi, preferred_element_type=F32)
               + jnp.dot(ovt_ref[...], p_lo, preferred_element_type=F32))
        work = jnp.where(forced, -3e38, jnp.where(causal_blk, imp, -1.0))
        sel = jnp.where(forced, 1.0, 0.0)
        for _ in range(SLC_TOPN - MAX_FORCED):
            mx = jnp.max(work, axis=0, keepdims=True)
            idx = jnp.min(jnp.where(work == mx, blk, N_SLC), axis=0, keepdims=True)
            pick = blk == idx
            sel = jnp.where(pick & (mx >= 0.0), 1.0, sel)
            work = jnp.where(pick, -3e38, work)
        sel_bias = jnp.where(sel > 0.0, 0.0, NEG_INF)
        sel_bias = jnp.concatenate([sel_bias, jnp.zeros((LANES - N_SLC, tq), F32)], axis=0)
        sel_bias = _lane_tile(sel_bias.astype(BF16), HPG)
        q_aug.append(jnp.concatenate([q4[g], sel_bias], axis=0))

    start = pl.multiple_of(jnp.maximum(i - WIN // tq, 0) * tq, tq)
    k_win = kw_ref[pl.ds(start, WIN_KEYS), :]
    delta = t_row - (start + lax.broadcasted_iota(jnp.int32, (WIN_KEYS, 1), 0))
    win_bias = _lane_tile(jnp.where((delta >= 0) & (delta < WIN), 0.0, NEG_INF), HPG)
    o_win = []
    for g in groups:
        sT = scores(k_win, q4[g]) + win_bias
        pT = jnp.exp((sT - jnp.max(sT, axis=0, keepdims=True)).astype(BF16))
        vT = _with_ones(vwt_ref[vrows[g], pl.ds(start, WIN_KEYS)])
        o_win.append(normalized(jnp.dot(vT, pT, preferred_element_type=F32)))

    def slc_scores(kt, dst_ref):
        k0 = pl.multiple_of(kt * SLC_KT, SLC_KT)
        k_aug = jnp.concatenate([ks_ref[pl.ds(k0, SLC_KT), :], blk_ref[pl.ds(k0, SLC_KT), :]],
                                axis=1)
        maxima = []
        for g in groups:
            s_t = scores(k_aug, q_aug[g])
            dst_ref[g] = s_t
            maxima.append(jnp.max(s_t, axis=0, keepdims=True))
        return maxima

    def slc_update(kt, state, s_t, tile_max):
        k0 = pl.multiple_of(kt * SLC_KT, SLC_KT)
        out = []
        for g in groups:
            m_i, acc = state[2 * g], state[2 * g + 1]
            m_new = jnp.maximum(m_i, tile_max[g])
            alpha = jnp.exp(m_i - m_new)
            pT = jnp.exp((s_t[g] - m_new).astype(BF16))
            vT = _with_ones(vst_ref[vrows[g], pl.ds(k0, SLC_KT)])
            out += [m_new, alpha * acc + jnp.dot(vT, pT, preferred_element_type=F32)]
        return out

    def from_ref(src_ref):
        return [src_ref[g] for g in groups]

    def slc_pair(p, carry):
        state, max_a = list(carry[:2 * N_KV]), list(carry[2 * N_KV:])
        max_b = slc_scores(2 * p + 1, sb_ref)
        state = slc_update(2 * p, state, from_ref(sa_ref), max_a)
        max_a = slc_scores(2 * p + 2, sa_ref)
        state = slc_update(2 * p + 1, state, from_ref(sb_ref), max_b)
        return tuple(state + max_a)

    def slc_finish(state, src_ref):
        kpos = last_kt * SLC_KT + lax.broadcasted_iota(jnp.int32, (SLC_KT, 1), 0)
        causal_bias = _lane_tile(jnp.where(kpos <= t_row, 0.0, NEG_INF), HPG)
        s_last = [s + causal_bias for s in from_ref(src_ref)]
        state = slc_update(last_kt, state, s_last,
                           [jnp.max(s, axis=0, keepdims=True) for s in s_last])
        for g in groups:
            oslc_ref[g] = normalized(state[2 * g + 1])

    last_kt = (i * tq) // SLC_KT
    init = ((jnp.full((1, nl), -3e38, F32), jnp.zeros((HEAD_DIM + ONES_ROWS, nl), F32)) * N_KV
            + tuple(slc_scores(0, sa_ref)))
    carry = lax.fori_loop(0, last_kt // 2, slc_pair, init)
    state, max_a = list(carry[:2 * N_KV]), list(carry[2 * N_KV:])
    odd = lax.rem(last_kt, 2) == 1

    @pl.when(odd)
    def _():
        slc_scores(last_kt, sb_ref)
        slc_finish(slc_update(last_kt - 1, state, from_ref(sa_ref), max_a), sb_ref)

    @pl.when(jnp.logical_not(odd))
    def _():
        slc_finish(state, sa_ref)

    o_slc = [oslc_ref[g] for g in groups]

    for g in groups:
        gates = jax.nn.sigmoid(gt_ref[0, g])

        def gate_row(br):
            return jnp.concatenate([gates[3 * j + br:3 * j + br + 1, :] for j in range(HPG)],
                                   axis=1)

        o = gate_row(0) * o_cmp[g] + gate_row(1) * o_slc[g] + gate_row(2) * o_win[g]
        for j in range(HPG):
            h0 = (g * HPG + j) * HEAD_DIM
            y_ref[0, h0:h0 + HEAD_DIM, :] = o[:, j * tq:(j + 1) * tq].astype(y_ref.dtype)


def _block_onehot():
    oh = (np.arange(SEQ)[:, None] // SLC_LEN) == np.arange(LANES)[None, :]
    return jnp.asarray(oh.astype(np.float32), dtype=BF16)


def _nsa(q, kc, vct, kv, vst, vwt, gt, ovt):
    nq = SEQ // NSA_TQ
    return pl.pallas_call(
        _nsa_kernel,
        grid=(BATCH, nq),
        in_specs=[pl.BlockSpec((NSA_TQ, NSA_Q), lambda b, i: (b * nq + i, 0)),
                  pl.BlockSpec((1, N_CHUNK, NSA_KV), lambda b, i: (b, 0, 0)),
                  pl.BlockSpec((1, NSA_KV, N_CHUNK), lambda b, i: (b, 0, 0)),
                  pl.BlockSpec((SEQ, NSA_KV), lambda b, i: (b, 2)),
                  pl.BlockSpec((None, NSA_KV, SEQ), lambda b, i: (b, 0, 0)),
                  pl.BlockSpec((SEQ, NSA_KV), lambda b, i: (b, 4)),
                  pl.BlockSpec((None, NSA_KV, SEQ), lambda b, i: (b, 0, 0)),
                  pl.BlockSpec((1, N_KV, GATE_ROWS, NSA_TQ), lambda b, i: (b, 0, 0, i)),
                  pl.BlockSpec((N_SLC, N_CHUNK), lambda b, i: (0, 0)),
                  pl.BlockSpec((SEQ, LANES), lambda b, i: (0, 0))],
        out_specs=pl.BlockSpec((1, NSA_Q, NSA_TQ), lambda b, i: (b, 0, i)),
        out_shape=jax.ShapeDtypeStruct((BATCH, NSA_Q, SEQ), BF16),
        scratch_shapes=[pltpu.VMEM((N_KV, SLC_KT, HPG * NSA_TQ), F32),
                        pltpu.VMEM((N_KV, SLC_KT, HPG * NSA_TQ), F32),
                        pltpu.VMEM((N_KV, HEAD_DIM, HPG * NSA_TQ), F32)],
        compiler_params=_cparams(("arbitrary", "arbitrary")),
        name="nsa",
    )(q, kc, vct, kv, vst, kv, vwt, gt, ovt, _block_onehot())


def _overlap_t():
    c_start = np.arange(N_CHUNK) * CMP_STRIDE
    s_start = np.arange(N_SLC) * SLC_LEN
    ov = ((c_start[None, :] <= s_start[:, None] + SLC_LEN - 1)
          & (c_start[None, :] + CMP_LEN - 1 >= s_start[:, None])
          & (np.arange(N_CHUNK)[None, :] < N_CMP))
    return jnp.asarray(ov.astype(np.float32), dtype=BF16)


def _merge_kernel(x_ref, g_ref, wmg_ref, ya_ref, yb_ref, yct_ref, yd_ref, wb_ref, wo_ref, o_ref):
    x = x_ref[...]
    nb = _rms(x, g_ref[...]).astype(BF16)
    yc = yct_ref[0].astype(F32).T.astype(BF16)
    ys = (ya_ref[...], yb_ref[...], yc, yd_ref[...])
    merged = jnp.zeros((MERGE_TM, D_MODEL), F32)
    for bi, y in enumerate(ys):
        gate = jax.nn.sigmoid(jnp.dot(nb, wmg_ref[:, bi * D_MODEL:(bi + 1) * D_MODEL],
                                      preferred_element_type=F32))
        merged = merged + gate * jnp.dot(y, wb_ref[bi], preferred_element_type=F32)
    o_ref[...] = x + jnp.dot(merged.astype(BF16), wo_ref[...], preferred_element_type=F32)


def _merge(x, g, wmg, ya, yb, yct, yd, wb, wo, layer):
    nt = SEQ // MERGE_TM
    tok = lambda i: (i, 0)
    const2 = lambda i: (0, 0)
    return pl.pallas_call(
        _merge_kernel,
        grid=(TOKENS // MERGE_TM,),
        in_specs=[pl.BlockSpec((MERGE_TM, D_MODEL), tok),
                  pl.BlockSpec((1, D_MODEL), const2),
                  pl.BlockSpec((None, D_MODEL, N_BRANCH * D_MODEL), lambda i: (layer, 0, 0)),
                  pl.BlockSpec((MERGE_TM, MIX), tok),
                  pl.BlockSpec((MERGE_TM, MIX), tok),
                  pl.BlockSpec((1, MIX, MERGE_TM), lambda i: (i // nt, 0, i % nt)),
                  pl.BlockSpec((MERGE_TM, MIX), tok),
                  pl.BlockSpec((None, N_BRANCH, MIX, D_MODEL), lambda i: (layer, 0, 0, 0)),
                  pl.BlockSpec((None, D_MODEL, D_MODEL), lambda i: (layer, 0, 0))],
        out_specs=pl.BlockSpec((MERGE_TM, D_MODEL), tok),
        out_shape=jax.ShapeDtypeStruct((TOKENS, D_MODEL), F32),
        compiler_params=_cparams(("arbitrary",)),
        name="merge",
    )(x, g, wmg, ya, yb, yct, yd, wb, wo)


def _router_logits(t, wr_ref, br_ref):
    w = wr_ref[...]
    t_hi, w_hi = t.astype(BF16), w.astype(BF16)
    t_lo = (t - t_hi.astype(F32)).astype(BF16)
    w_lo = (w - w_hi.astype(F32)).astype(BF16)
    dot = functools.partial(jnp.dot, preferred_element_type=F32)
    return dot(t_hi, w_hi) + dot(t_hi, w_lo) + dot(t_lo, w_hi) + br_ref[...]


def _top_group(logits):
    lane = lax.broadcasted_iota(jnp.int32, logits.shape, 1)
    is_grp = lane < N_GROUPS
    lg = jnp.where(is_grp, logits, NEG_INF)
    gmax = jnp.max(lg, axis=1, keepdims=True)
    grp_idx = jnp.min(jnp.where(is_grp & (lg == gmax), lane, ROUTER_PAD), axis=1, keepdims=True)
    return lg, gmax, grp_idx


def _group_weight(logits, grp_idx):
    lane = lax.broadcasted_iota(jnp.int32, logits.shape, 1)
    lg, gmax, _ = _top_group(logits)
    ge = jnp.where(lane < N_GROUPS, jnp.exp(lg - gmax), 0.0)
    return (jnp.sum(jnp.where(lane == grp_idx, ge, 0.0), axis=1, keepdims=True)
            / jnp.sum(ge, axis=1, keepdims=True))


ROUTE_ROWS = 32


def _route_kernel(x_ref, g_ref, wr_ref, br_ref, tri_ref, meta_ref, cnt_ref, run_ref):
    @pl.when(pl.program_id(0) == 0)
    def _():
        run_ref[...] = jnp.zeros_like(run_ref)

    logits = _router_logits(_rms(x_ref[...], g_ref[...]), wr_ref, br_ref)
    lt = logits.T[:ROUTE_ROWS, :]
    row = lax.broadcasted_iota(jnp.int32, (ROUTE_ROWS, 1), 0)

    def top(mask, vals):
        v = jnp.where(mask, vals, NEG_INF)
        best = jnp.max(v, axis=0, keepdims=True)
        return v, jnp.min(jnp.where(mask & (v == best), row, ROUTE_ROWS), axis=0, keepdims=True)

    _, grp_idx = top(row < N_GROUPS, lt)
    first = N_GROUPS + grp_idx * EXPERTS_PER_GROUP
    in_grp = (row >= first) & (row < first + EXPERTS_PER_GROUP)
    le, i1 = top(in_grp, lt)
    _, i2 = top(in_grp & (row != i1), le)
    lo = jnp.minimum(i1, i2) - first
    hi = jnp.maximum(i1, i2) - first
    pair = jnp.right_shift(lo * (2 * EXPERTS_PER_GROUP - 1 - lo), 1) + hi - lo - 1
    cls = grp_idx * N_PAIRS + pair
    onehot = jnp.where(row == cls, 1.0, 0.0)
    before = jnp.dot(onehot.astype(BF16), tri_ref[...], preferred_element_type=F32) + run_ref[...]
    rank = jnp.sum(onehot * before, axis=0, keepdims=True).astype(jnp.int32)
    run_ref[...] += jnp.sum(onehot, axis=1, keepdims=True)
    sub = lax.broadcasted_iota(jnp.int32, (SUBLANES, 1), 0)
    meta_ref[...] = jnp.where(sub == 0, cls, jnp.where(sub == 1, rank, 0))
    cnt_ref[...] = jnp.broadcast_to(run_ref[...], cnt_ref.shape)


def _route(x, g, wr, br):
    tok = lambda i: (i, 0)
    const2 = lambda i: (0, 0)
    tri = jnp.asarray(np.triu(np.ones((ROUTE_TM, ROUTE_TM), np.float32), 1), dtype=BF16)
    return pl.pallas_call(
        _route_kernel,
        grid=(TOKENS // ROUTE_TM,),
        in_specs=[pl.BlockSpec((ROUTE_TM, D_MODEL), tok),
                  pl.BlockSpec((1, D_MODEL), const2),
                  pl.BlockSpec((D_MODEL, ROUTER_PAD), const2),
                  pl.BlockSpec((1, ROUTER_PAD), const2),
                  pl.BlockSpec((ROUTE_TM, ROUTE_TM), const2)],
        out_specs=[pl.BlockSpec((SUBLANES, ROUTE_TM), lambda i: (0, i)),
                   pl.BlockSpec((ROUTE_ROWS, LANES), const2)],
        out_shape=[jax.ShapeDtypeStruct((SUBLANES, TOKENS), jnp.int32),
                   jax.ShapeDtypeStruct((ROUTE_ROWS, LANES), F32)],
        scratch_shapes=[pltpu.VMEM((ROUTE_ROWS, 1), F32)],
        compiler_params=_cparams(("arbitrary",)),
        name="route",
    )(x, g, wr, br, tri)


MOVE_SLOTS = 3


def _move_rows_kernel(idx_ref, seg_ref, len_ref, src_hbm, dst_hbm, buf, zbuf, in_sem, out_sem,
                      pad_sem, *, scatter):
    c = pl.program_id(0)
    n = pl.num_programs(0)
    slot = lax.rem(c, MOVE_SLOTS)
    nxt = lax.rem(c + 1, MOVE_SLOTS)
    rc = ROW_CHUNK
    chunk = lambda k: pl.ds(k * rc, rc)

    if scatter:
        def read(k, sl):
            return [pltpu.make_async_copy(src_hbm.at[chunk(k)], buf.at[sl], in_sem.at[sl])]

        def write(k, sl):
            return [pltpu.make_async_copy(buf.at[sl, pl.ds(r, 1)],
                                          dst_hbm.at[pl.ds(idx_ref[k * rc + r], 1)], out_sem.at[sl])
                    for r in range(rc)]

        def write_done(sl):
            return pltpu.make_async_copy(buf.at[sl], dst_hbm.at[chunk(0)], out_sem.at[sl])

        def read_done(sl):
            return read(0, sl)[0]
    else:
        def read(k, sl):
            return [pltpu.make_async_copy(src_hbm.at[pl.ds(idx_ref[k * rc + r], 1)],
                                          buf.at[sl, pl.ds(r, 1)], in_sem.at[sl])
                    for r in range(rc)]

        def write(k, sl):
            return [pltpu.make_async_copy(buf.at[sl], dst_hbm.at[chunk(k)], out_sem.at[sl])]

        def write_done(sl):
            return write(0, sl)[0]

        def read_done(sl):
            return pltpu.make_async_copy(src_hbm.at[chunk(0)], buf.at[sl], in_sem.at[sl])

    def pad_fills(fn):
        for sg in range(N_CLASSES):
            start, length = seg_ref[sg], len_ref[sg]
            head = jnp.bitwise_and(-start, SUBLANES - 1)
            for r in range(SUBLANES - 1):
                @pl.when(r < head)
                def _():
                    fn(pltpu.make_async_copy(zbuf.at[pl.ds(0, 1)], dst_hbm.at[pl.ds(start + r, 1)],
                                             pad_sem))
            body = length - head
            bit = rc // 2
            while bit >= SUBLANES:
                done = body - jnp.bitwise_and(body, 2 * bit - 1)
                first = pl.multiple_of(start + head + done, SUBLANES)

                @pl.when(jnp.bitwise_and(body, bit) != 0)
                def _():
                    fn(pltpu.make_async_copy(zbuf.at[pl.ds(0, bit)], dst_hbm.at[pl.ds(first, bit)],
                                             pad_sem))
                bit //= 2
        tail = pl.multiple_of(seg_ref[N_CLASSES], rc)

        def tail_tile(k, carry):
            fn(pltpu.make_async_copy(zbuf, dst_hbm.at[pl.ds(tail + k * rc, rc)], pad_sem))
            return carry
        lax.fori_loop(0, len_ref[N_CLASSES] // rc, tail_tile, 0)

    @pl.when(c == 0)
    def _():
        for cp in read(0, 0):
            cp.start()
        if scatter:
            zbuf[...] = jnp.zeros_like(zbuf)
            pad_fills(lambda cp: cp.start())

    @pl.when(c + 1 < n)
    def _():
        @pl.when(c >= 2)
        def _():
            write_done(nxt).wait()
        for cp in read(c + 1, nxt):
            cp.start()

    read_done(slot).wait()
    for cp in write(c, slot):
        cp.start()

    @pl.when(c == n - 1)
    def _():
        write_done(nxt).wait()
        write_done(lax.rem(c + 2, MOVE_SLOTS)).wait()
        write_done(slot).wait()
        if scatter:
            pad_fills(lambda cp: cp.wait())


def _move_rows(idx, seg_start, seg_len, src, n_out, scatter):
    any_spec = pl.BlockSpec(memory_space=pl.ANY)
    grid_spec = pltpu.PrefetchScalarGridSpec(
        num_scalar_prefetch=3, grid=(TOKENS // ROW_CHUNK,), in_specs=[any_spec], out_specs=any_spec,
        scratch_shapes=[pltpu.VMEM((MOVE_SLOTS, ROW_CHUNK, D_MODEL), F32),
                        pltpu.VMEM((ROW_CHUNK, D_MODEL), F32),
                        pltpu.SemaphoreType.DMA((MOVE_SLOTS,)),
                        pltpu.SemaphoreType.DMA((MOVE_SLOTS,)),
                        pltpu.SemaphoreType.DMA(())])
    return pl.pallas_call(
        functools.partial(_move_rows_kernel, scatter=scatter),
        grid_spec=grid_spec,
        out_shape=jax.ShapeDtypeStruct((n_out, D_MODEL), F32),
        compiler_params=_cparams(("arbitrary",)),
        name="dispatch" if scatter else "collect",
    )(idx, seg_start, seg_len, src)


def _experts_kernel(ea_ref, eb_ref, tv_ref, tb_ref, x_ref, g_ref, wr_ref, br_ref, wga_ref, wua_ref,
                    wda_ref, wgb_ref, wub_ref, wdb_ref, fg_ref, o_ref, *, layer, final_norm):
    j = pl.program_id(0)
    nv = tv_ref[j]

    @pl.when(nv > 0)
    def _():
        x = x_ref[...]
        t = _rms(x, g_ref[...])
        logits = _router_logits(t, wr_ref, br_ref)
        lane = lax.broadcasted_iota(jnp.int32, logits.shape, 1)
        ea = ea_ref[j] - layer * N_EXPERTS
        eb = eb_ref[j] - layer * N_EXPERTS
        la = jnp.sum(jnp.where(lane == N_GROUPS + ea, logits, 0.0), axis=1, keepdims=True)
        lb = jnp.sum(jnp.where(lane == N_GROUPS + eb, logits, 0.0), axis=1, keepdims=True)
        top = jnp.maximum(la, lb)
        pa, pb = jnp.exp(la - top), jnp.exp(lb - top)
        scale = _group_weight(logits, jnp.right_shift(ea, 2)) / (pa + pb)
        tb = t.astype(BF16)
        acc = jnp.zeros((MOE_TM, D_MODEL), F32)
        for wg, wu, wd, w in ((wga_ref, wua_ref, wda_ref, pa * scale),
                              (wgb_ref, wub_ref, wdb_ref, pb * scale)):
            hid = (jax.nn.silu(jnp.dot(tb, wg[0], preferred_element_type=F32))
                   * jnp.dot(tb, wu[0], preferred_element_type=F32)) * w
            acc = acc + jnp.dot(hid.astype(BF16), wd[0], preferred_element_type=F32)
        h = x + acc
        if final_norm:
            h = _rms(h, fg_ref[...])
        o_ref[...] = h

    @pl.when(nv == 0)
    def _():
        o_ref[...] = jnp.zeros_like(o_ref)


def _experts(tile_ea, tile_eb, tile_valid, tile_block, hs, g, wr, br, wg, wu, wd, fg, layer,
             final_norm):
    const2 = lambda j, *_: (0, 0)
    of_a = lambda j, ea, eb, tv, tb: (ea[j], 0, 0)
    of_b = lambda j, ea, eb, tv, tb: (eb[j], 0, 0)
    up = lambda idx: pl.BlockSpec((1, D_MODEL, D_EXPERT), idx)
    down = lambda idx: pl.BlockSpec((1, D_EXPERT, D_MODEL), idx)
    grid_spec = pltpu.PrefetchScalarGridSpec(
        num_scalar_prefetch=4,
        grid=(MOE_NT,),
        in_specs=[pl.BlockSpec((MOE_TM, D_MODEL), lambda j, ea, eb, tv, tb: (tb[j], 0)),
                  pl.BlockSpec((1, D_MODEL), const2),
                  pl.BlockSpec((D_MODEL, ROUTER_PAD), const2),
                  pl.BlockSpec((1, ROUTER_PAD), const2),
                  up(of_a), up(of_a), down(of_a), up(of_b), up(of_b), down(of_b),
                  pl.BlockSpec((1, D_MODEL), const2)],
        out_specs=pl.BlockSpec((MOE_TM, D_MODEL), lambda j, *_: (j, 0)))
    return pl.pallas_call(
        functools.partial(_experts_kernel, layer=layer, final_norm=final_norm),
        grid_spec=grid_spec,
        out_shape=jax.ShapeDtypeStruct((MOE_ROWS, D_MODEL), F32),
        compiler_params=_cparams(("arbitrary",)),
        name="experts",
    )(tile_ea, tile_eb, tile_valid, tile_block, hs, g, wr, br, wg, wu, wd, wg, wu, wd, fg)


_PAIR_LO = np.array([a for a in range(EXPERTS_PER_GROUP) for b in range(a + 1, EXPERTS_PER_GROUP)])
_PAIR_HI = np.array([b for a in range(EXPERTS_PER_GROUP) for b in range(a + 1, EXPERTS_PER_GROUP)])


def _moe(x, g, wr, br, wg, wu, wd, fg, layer, final_norm):
    meta, cnt = _route(x, g, wr, br)
    cls, rank = meta[0], meta[1]
    counts = cnt[:N_CLASSES, 0].astype(jnp.int32)
    padded = (counts + MOE_TM - 1) // MOE_TM * MOE_TM
    ends = jnp.cumsum(padded)
    starts = ends - padded
    pos = starts[cls] + rank
    tile_start = jnp.arange(MOE_NT, dtype=jnp.int32) * MOE_TM
    tile_cls = jnp.minimum(jnp.sum(tile_start[:, None] >= ends[None, :], axis=1), N_CLASSES - 1)
    tile_valid = jnp.clip(starts[tile_cls] + counts[tile_cls] - tile_start, 0, MOE_TM)
    tile_valid = tile_valid.astype(jnp.int32)
    tile_block = jnp.where(tile_valid > 0, jnp.arange(MOE_NT, dtype=jnp.int32), 0)
    first = layer * N_EXPERTS + (tile_cls // N_PAIRS) * EXPERTS_PER_GROUP
    tile_ea = (first + jnp.asarray(_PAIR_LO)[tile_cls % N_PAIRS]).astype(jnp.int32)
    tile_eb = (first + jnp.asarray(_PAIR_HI)[tile_cls % N_PAIRS]).astype(jnp.int32)
    seg_start = jnp.concatenate([starts + counts, ends[-1:]])
    seg_len = jnp.concatenate([padded - counts, MOE_ROWS - ends[-1:]])
    hs = _move_rows(pos, seg_start, seg_len, x, MOE_ROWS, scatter=True)
    ys = _experts(tile_ea, tile_eb, tile_valid, tile_block, hs, g, wr, br, wg, wu, wd, fg, layer,
                  final_norm)
    return _move_rows(pos, seg_start, seg_len, ys, TOKENS, scatter=False)


def _block_diag(w):
    eye = jnp.eye(LRU_BLOCKS, dtype=w.dtype)
    return jnp.einsum('hij,hk->hikj', w, eye).reshape(MIX, MIX)


def _w_in_column_ranges():
    cuts = [int(c) for c in np.cumsum((0,) + IN_SPLITS)]
    q_parts = [(cuts[4] + (g * HPG + j) * HEAD_DIM, cuts[4] + (g * HPG + j + 1) * HEAD_DIM)
               for j in range(HPG) for g in range(N_KV)]
    return ([(cuts[8], cuts[9]), (cuts[0], cuts[4])] + q_parts
            + [(cuts[5], cuts[6]), (cuts[7], cuts[8]), (cuts[6], cuts[7])])


PREP_PIECE = 64
PREP_BLOCK = 512


def _prep_piece_starts():
    starts = []
    for a, b in _w_in_column_ranges():
        starts += list(range(a, b, PREP_PIECE))
    starts += [0] * (2 * W_IN_HALF // PREP_PIECE - len(starts))
    return np.asarray(starts, np.int32)


def _prep_w_in_kernel(src_ref, wt_hbm, o_ref, buf, sem):
    layer, j = pl.program_id(0), pl.program_id(1)
    nj = pl.num_programs(1)
    per_block = PREP_BLOCK // PREP_PIECE
    slot = lax.rem(j, 2)

    def pieces(blk, sl):
        return [pltpu.make_async_copy(
            wt_hbm.at[layer, pl.ds(pl.multiple_of(src_ref[blk * per_block + k], SUBLANES), PREP_PIECE)],
            buf.at[sl, pl.ds(k * PREP_PIECE, PREP_PIECE)], sem.at[sl]) for k in range(per_block)]

    @pl.when(j == 0)
    def _():
        for cp in pieces(0, 0):
            cp.start()

    @pl.when(j + 1 < nj)
    def _():
        for cp in pieces(j + 1, 1 - slot):
            cp.start()

    for cp in pieces(j, slot):
        cp.wait()
    o_ref[...] = buf[slot].T.astype(o_ref.dtype)


def _prep_w_in(w_in):
    grid_spec = pltpu.PrefetchScalarGridSpec(
        num_scalar_prefetch=1,
        grid=(DEPTH, 2 * W_IN_HALF // PREP_BLOCK),
        in_specs=[pl.BlockSpec(memory_space=pl.ANY)],
        out_specs=pl.BlockSpec((None, D_MODEL, PREP_BLOCK), lambda l, j, src: (l, 0, j)),
        scratch_shapes=[pltpu.VMEM((2, PREP_BLOCK, D_MODEL), F32), pltpu.SemaphoreType.DMA((2,))])
    return pl.pallas_call(
        _prep_w_in_kernel,
        grid_spec=grid_spec,
        out_shape=jax.ShapeDtypeStruct((DEPTH, D_MODEL, 2 * W_IN_HALF), BF16),
        compiler_params=_cparams(("arbitrary", "arbitrary")),
        name="prep_w_in",
    )(jnp.asarray(_prep_piece_starts()), w_in.transpose(0, 2, 1))


def _nsa_mixer(q, kv, ng, p):
    w1 = p['cmp_w1'].reshape(2, N_KV, CMP_LEN, HEAD_DIM, HEAD_DIM).transpose(0, 2, 1, 3, 4)
    pe = jnp.concatenate([p['cmp_pe']] * N_KV, axis=-1)
    kc, vct, vst, vwt, gt = _kvprep(kv, ng, pe, _group_diag(w1).astype(BF16),
                                    _group_diag(p['cmp_w2']).astype(BF16))
    return _nsa(q, kc, vct, kv, vst, vwt, gt, _overlap_t())


def _layer(h, p, big, layer, final_g, final_norm):
    row = lambda a: a.reshape(1, -1)

    u, v, gb, rb, q, kv, xd, ng = _proj(h, row(p['norm1_g']), big['w_in'], layer)

    bs = jnp.broadcast_to(p['gm_b'][:, :, None], (GM_GROUPS, GM_CHUNK, GM_GW))
    y_a, y_b, y_d = _mixers(
        u, v, gb, rb, xd, row(p['gm_norm_g']), p['gm_ws'], bs,
        p['conv_w'], row(p['conv_b']), _block_diag(p['lru_wa']).astype(BF16), row(p['lru_ba']),
        _block_diag(p['lru_wx']).astype(BF16), row(p['lru_bx']), row(p['lru_lambda']),
        p['pool_w'].astype(BF16), row(p['pool_scale']))

    y_ct = _nsa_mixer(q, kv, ng, p)

    h = _merge(h, row(p['norm1_g']), big['w_in'], y_a, y_b, y_ct, y_d, big['w_branch'],
               big['w_out'], layer)

    wr = jnp.concatenate([p['router_w_group'], p['router_w_expert']], axis=1)
    wr = jnp.pad(wr, ((0, 0), (0, ROUTER_PAD - wr.shape[1])))
    br = jnp.concatenate([p['router_b_group'], p['router_b_expert']])
    br = jnp.pad(br, (0, ROUTER_PAD - br.shape[0])).reshape(1, ROUTER_PAD)
    return _moe(h, row(p['norm2_g']), wr, br, big['moe_w_gate'], big['moe_w_up'],
                big['moe_w_down'], row(final_g), layer, final_norm)


_LAYER_PARAMS = ('norm1_g', 'gm_norm_g', 'gm_ws', 'gm_b', 'conv_w', 'conv_b', 'lru_wa',
                 'lru_ba', 'lru_wx', 'lru_bx', 'lru_lambda', 'cmp_pe', 'cmp_w1', 'cmp_w2', 'pool_w',
                 'pool_scale', 'norm2_g', 'router_w_group', 'router_b_group',
                 'router_w_expert', 'router_b_expert')


def kernel(x, norm1_g, w_in, gm_norm_g, gm_ws, gm_b, conv_w, conv_b, lru_wa, lru_ba, lru_wx,
           lru_bx, lru_lambda, cmp_pe, cmp_w1, cmp_w2, pool_w, pool_scale, w_branch, w_out,
           norm2_g, router_w_group, router_b_group, router_w_expert, router_b_expert,
           moe_w_gate, moe_w_up, moe_w_down, final_norm_g):
    stacked = dict(zip(_LAYER_PARAMS, (
        norm1_g, gm_norm_g, gm_ws, gm_b, conv_w, conv_b, lru_wa, lru_ba, lru_wx, lru_bx,
        lru_lambda, cmp_pe, cmp_w1, cmp_w2, pool_w, pool_scale, norm2_g,
        router_w_group, router_b_group, router_w_expert, router_b_expert)))
    w_all = _prep_w_in(w_in)
    experts = lambda w: w.astype(BF16).reshape((DEPTH * N_EXPERTS,) + w.shape[2:])
    big = dict(w_in=w_all, w_branch=w_branch.astype(BF16), w_out=w_out.astype(BF16),
               moe_w_gate=experts(moe_w_gate), moe_w_up=experts(moe_w_up),
               moe_w_down=experts(moe_w_down))
    h = x.reshape(TOKENS, D_MODEL)
    for layer in range(DEPTH):
        p = {k: a[layer] for k, a in stacked.items()}
        h = _layer(h, p, big, layer, final_norm_g, final_norm=(layer == DEPTH - 1))
    return h.reshape(BATCH, SEQ, D_MODEL)
```

```python
import functools

import numpy as np
import jax
import jax.numpy as jnp
from jax import lax
from jax.experimental import pallas as pl
from jax.experimental.pallas import tpu as pltpu

F32 = jnp.float32
BF16 = jnp.bfloat16

D_MODEL = 1024
BATCH = 4
SEQ = 4096
TOKENS = BATCH * SEQ
DEPTH = 2
MIX = D_MODEL // 2
GM_CHUNK = 128
GM_GROUPS = 4
GM_GW = MIX // GM_GROUPS
CONV_WIDTH = 4
LRU_BLOCKS = 8
LRU_BW = MIX // LRU_BLOCKS
LRU_C = 8.0
N_HEADS = 8
HEAD_DIM = MIX // N_HEADS
N_KV = 2
HPG = N_HEADS // N_KV
CMP_LEN = 32
CMP_STRIDE = 16
SLC_LEN = 64
SLC_TOPN = 8
MAX_FORCED = 3
WIN = 512
NSA_Q = N_HEADS * HEAD_DIM
NSA_KV = N_KV * HEAD_DIM
POOL_WINDOWS = (2, 4, 8, 16)
POOL_GW = MIX // len(POOL_WINDOWS)
N_BRANCH = 4
N_GROUPS = 4
EXPERTS_PER_GROUP = 4
N_EXPERTS = N_GROUPS * EXPERTS_PER_GROUP
D_EXPERT = D_MODEL // 2
EPS = 1e-6
NEG_INF = -1e30
IN_SPLITS = (MIX, MIX, MIX, MIX, NSA_Q, 6 * NSA_KV, 3 * N_HEADS, MIX, N_BRANCH * D_MODEL)

N_CHUNK = SEQ // CMP_STRIDE
N_CMP = N_CHUNK - CMP_LEN // CMP_STRIDE + 1
N_SLC = SEQ // SLC_LEN

LANES = 128
SUBLANES = 8
GATE_PAD = LANES
GATE_ROWS = 16
ROUTER_PAD = LANES
VMEM_LIMIT = 56 * 1024 * 1024

PROJ_WIDTHS = (MIX, MIX, MIX, MIX, NSA_Q, 6 * NSA_KV, MIX, GATE_PAD)
W_IN_HALF = N_BRANCH * D_MODEL
PROJ_TM = 1024
GMLP_TM = 512
LRU_TS = 512
POOL_TS = 512
NSA_TQ = 256
SLC_KT = 256
WIN_KEYS = WIN + NSA_TQ
MERGE_TM = 512
ROUTE_TM = 512
MOE_TM = 256
N_PAIRS = EXPERTS_PER_GROUP * (EXPERTS_PER_GROUP - 1) // 2
N_CLASSES = N_GROUPS * N_PAIRS
MOE_ROWS = TOKENS + N_CLASSES * MOE_TM
MOE_NT = MOE_ROWS // MOE_TM
ROW_CHUNK = 256


def _cparams(sem):
    return pltpu.CompilerParams(dimension_semantics=sem, vmem_limit_bytes=VMEM_LIMIT)


def _rms(x, g):
    return x * lax.rsqrt(jnp.mean(x * x, axis=-1, keepdims=True) + EPS) * g


def _proj_kernel(x_ref, g_ref, w_ref, *out_refs):
    nb = _rms(x_ref[...], g_ref[...]).astype(BF16)
    off = 0
    for ref in out_refs:
        w = ref.shape[-1]
        ref[...] = jnp.dot(nb, w_ref[:, off:off + w], preferred_element_type=F32).astype(ref.dtype)
        off += w


def _proj(x, g, w, layer):
    out_shape = [jax.ShapeDtypeStruct((TOKENS, wd), BF16) for wd in PROJ_WIDTHS[:-1]]
    out_shape.append(jax.ShapeDtypeStruct((TOKENS, GATE_PAD), F32))
    return pl.pallas_call(
        _proj_kernel,
        grid=(TOKENS // PROJ_TM,),
        in_specs=[pl.BlockSpec((PROJ_TM, D_MODEL), lambda i: (i, 0)),
                  pl.BlockSpec((1, D_MODEL), lambda i: (0, 0)),
                  pl.BlockSpec((None, D_MODEL, W_IN_HALF), lambda i: (layer, 0, 1))],
        out_specs=[pl.BlockSpec((PROJ_TM, wd), lambda i: (i, 0)) for wd in PROJ_WIDTHS],
        out_shape=out_shape,
        compiler_params=_cparams(("arbitrary",)),
        name="proj",
    )(x, g, w)


def _gmlp_kernel(u_ref, v_ref, g_ref, ws_ref, bs_ref, o_ref):
    u = jax.nn.gelu(u_ref[...].astype(F32))
    v = _rms(jax.nn.gelu(v_ref[...].astype(F32)), g_ref[...]).astype(BF16)
    row = lax.broadcasted_iota(jnp.int32, (GM_CHUNK, GM_CHUNK), 0)
    col = lax.broadcasted_iota(jnp.int32, (GM_CHUNK, GM_CHUNK), 1)
    causal = row >= col
    for gi in range(GM_GROUPS):
        w = jnp.where(causal, ws_ref[gi], 0.0).astype(BF16)
        cs = slice(gi * GM_GW, (gi + 1) * GM_GW)
        for c in range(GMLP_TM // GM_CHUNK):
            rs = slice(c * GM_CHUNK, (c + 1) * GM_CHUNK)
            mixed = jnp.dot(w, v[rs, cs], preferred_element_type=F32) + bs_ref[gi]
            o_ref[rs, cs] = (u[rs, cs] * mixed).astype(o_ref.dtype)


LRU_TAIL = 8


def _lru_kernel(gb_ref, rb_ref, cw_ref, cb_ref, wa_ref, ba_ref, wx_ref, bx_ref, lam_ref, o_ref,
                tail_ref, h_ref):
    @pl.when(pl.program_id(1) == 0)
    def _():
        tail_ref[...] = jnp.zeros_like(tail_ref)
        h_ref[...] = jnp.zeros_like(h_ref)

    ts = LRU_TS
    x = rb_ref[...].astype(F32)
    ext = jnp.concatenate([tail_ref[...], x], axis=0)
    tail_ref[...] = x[ts - LRU_TAIL:, :]
    xc = cb_ref[...] + x * cw_ref[CONV_WIDTH - 1:CONV_WIDTH, :]
    for d in range(1, CONV_WIDTH):
        xs = pltpu.roll(ext, d, axis=0)[LRU_TAIL:, :]
        xc = xc + xs * cw_ref[CONV_WIDTH - 1 - d:CONV_WIDTH - d, :]
    xcb = xc.astype(BF16)
    r = jax.nn.sigmoid(jnp.dot(xcb, wa_ref[...], preferred_element_type=F32) + ba_ref[...])
    ig = jax.nn.sigmoid(jnp.dot(xcb, wx_ref[...], preferred_element_type=F32) + bx_ref[...])
    z = -lam_ref[...]
    softplus = jnp.maximum(z, 0.0) + jnp.log1p(jnp.exp(-jnp.abs(z)))
    log_a = -LRU_C * r * softplus
    a = jnp.exp(log_a)
    b = jnp.sqrt(1.0 - jnp.exp(2.0 * log_a)) * (ig * xc)
    rows = lax.broadcasted_iota(jnp.int32, (ts, 1), 0)
    d = 1
    while d < ts:
        valid = rows >= d
        a_prev = pltpu.roll(a, d, axis=0)
        b_prev = pltpu.roll(b, d, axis=0)
        b = jnp.where(valid, a * b_prev, 0.0) + b
        a = jnp.where(valid, a * a_prev, a)
        d *= 2
    h = a * h_ref[...] + b
    h_ref[...] = h[ts - 1:ts, :]
    o_ref[...] = (jax.nn.gelu(gb_ref[...].astype(F32)) * h).astype(o_ref.dtype)


POOL_TAIL = 16


def _pool_kernel(x_ref, w_ref, sc_ref, o_ref, tail_ref):
    s_id = pl.program_id(1)

    @pl.when(s_id == 0)
    def _():
        tail_ref[...] = jnp.zeros_like(tail_ref)

    ts = POOL_TS
    x = x_ref[...].astype(F32)
    ext = jnp.concatenate([tail_ref[...], x], axis=0)
    tail_ref[...] = x[ts - POOL_TAIL:, :]
    pos = s_id * ts + lax.broadcasted_iota(jnp.int32, (ts, 1), 0)
    acc = ext
    width = 1
    for gi, wdw in enumerate(POOL_WINDOWS):
        while width < wdw:
            acc = acc + pltpu.roll(acc, width, axis=0)
            width *= 2
        cs = slice(gi * POOL_GW, (gi + 1) * POOL_GW)
        cnt = jnp.minimum(pos + 1, wdw).astype(F32)
        pooled = acc[POOL_TAIL:, cs] / cnt - x[:, cs]
        mixed = jnp.dot(pooled.astype(BF16), w_ref[gi], preferred_element_type=F32)
        o_ref[:, cs] = (mixed * sc_ref[:, cs]).astype(o_ref.dtype)


def _mixers_kernel(u_ref, v_ref, gb_ref, rb_ref, xd_ref, gm_g_ref, gm_ws_ref, gm_bs_ref, cw_ref,
                   cb_ref, wa_ref, ba_ref, wx_ref, bx_ref, lam_ref, pw_ref, psc_ref,
                   ya_ref, yb_ref, yd_ref, lru_tail_ref, lru_h_ref, pool_tail_ref):
    _gmlp_kernel(u_ref, v_ref, gm_g_ref, gm_ws_ref, gm_bs_ref, ya_ref)
    _lru_kernel(gb_ref, rb_ref, cw_ref, cb_ref, wa_ref, ba_ref, wx_ref, bx_ref, lam_ref, yb_ref,
                lru_tail_ref, lru_h_ref)
    _pool_kernel(xd_ref, pw_ref, psc_ref, yd_ref, pool_tail_ref)


def _mixers(u, v, gb, rb, xd, gm_g, gm_ws, gm_bs, cw, cb, wa, ba, wx, bx, lam, pw, psc):
    assert GMLP_TM == LRU_TS == POOL_TS
    nt = SEQ // LRU_TS
    tok = pl.BlockSpec((LRU_TS, MIX), lambda b, s: (b * nt + s, 0))
    whole = lambda a: pl.BlockSpec(a.shape, lambda b, s: (0,) * a.ndim)
    small = (gm_g, gm_ws, gm_bs, cw, cb, wa, ba, wx, bx, lam, pw, psc)
    return pl.pallas_call(
        _mixers_kernel,
        grid=(BATCH, nt),
        in_specs=[tok] * 5 + [whole(a) for a in small],
        out_specs=[tok] * 3,
        out_shape=[jax.ShapeDtypeStruct((TOKENS, MIX), BF16)] * 3,
        scratch_shapes=[pltpu.VMEM((LRU_TAIL, MIX), F32), pltpu.VMEM((1, MIX), F32),
                        pltpu.VMEM((POOL_TAIL, MIX), F32)],
        compiler_params=_cparams(("arbitrary", "arbitrary")),
        name="mixers",
    )(u, v, gb, rb, xd, *small)


def _kvprep_kernel(kc_in, vc_in, vs_in, vw_in, ng_in, pe_ref, w1_ref, w2_ref, kc_ref, vct_ref,
                   vst_ref, vwt_ref, gt_ref, xf_ref):
    row = lax.broadcasted_iota(jnp.int32, (N_CHUNK, 1), 0)
    for c, x_in in enumerate((kc_in, vc_in)):
        xf_ref[...] = x_in[...].astype(F32)
        first = jnp.zeros((N_CHUNK, NSA_KV), F32)
        second = jnp.zeros((N_CHUNK, NSA_KV), F32)
        for l in range(CMP_STRIDE):
            xl = xf_ref[pl.ds(l, N_CHUNK, stride=CMP_STRIDE), :]
            first += jnp.dot((xl + pe_ref[c, l:l + 1, :]).astype(BF16), w1_ref[c, l],
                             preferred_element_type=F32)
            l2 = l + CMP_STRIDE
            second += jnp.dot((xl + pe_ref[c, l2:l2 + 1, :]).astype(BF16), w1_ref[c, l2],
                              preferred_element_type=F32)
        hid = jax.nn.gelu(first + pltpu.roll(second, N_CHUNK - 1, axis=0))
        out = jnp.dot(hid.astype(BF16), w2_ref[c], preferred_element_type=F32)
        out = jnp.where(row < N_CMP, out, 0.0)
        if c == 0:
            kc_ref[0] = out.astype(kc_ref.dtype)
        else:
            vct_ref[0] = out.T.astype(vct_ref.dtype)
    vst_ref[0] = vs_in[...].astype(F32).T.astype(vst_ref.dtype)
    vwt_ref[0] = vw_in[...].astype(F32).T.astype(vwt_ref.dtype)
    gates_t = ng_in[...].T
    gt_ref[...] = jnp.zeros_like(gt_ref)
    per_group = 3 * HPG
    for g in range(N_KV):
        gt_ref[0, g, :per_group, :] = gates_t[g * per_group:(g + 1) * per_group, :]


def _kvprep(kv, ng, pe, w1, w2):
    col = lambda j: pl.BlockSpec((SEQ, NSA_KV), lambda b: (b, j))
    whole = lambda a: pl.BlockSpec(a.shape, lambda b: (0,) * a.ndim)
    return pl.pallas_call(
        _kvprep_kernel,
        grid=(BATCH,),
        in_specs=[col(0), col(1), col(3), col(5), pl.BlockSpec((SEQ, GATE_PAD), lambda b: (b, 0)),
                  whole(pe), whole(w1), whole(w2)],
        out_specs=[pl.BlockSpec((1, N_CHUNK, NSA_KV), lambda b: (b, 0, 0)),
                   pl.BlockSpec((1, NSA_KV, N_CHUNK), lambda b: (b, 0, 0)),
                   pl.BlockSpec((1, NSA_KV, SEQ), lambda b: (b, 0, 0)),
                   pl.BlockSpec((1, NSA_KV, SEQ), lambda b: (b, 0, 0)),
                   pl.BlockSpec((1, N_KV, GATE_ROWS, SEQ), lambda b: (b, 0, 0, 0))],
        out_shape=[jax.ShapeDtypeStruct((BATCH, N_CHUNK, NSA_KV), BF16),
                   jax.ShapeDtypeStruct((BATCH, NSA_KV, N_CHUNK), BF16),
                   jax.ShapeDtypeStruct((BATCH, NSA_KV, SEQ), BF16),
                   jax.ShapeDtypeStruct((BATCH, NSA_KV, SEQ), BF16),
                   jax.ShapeDtypeStruct((BATCH, N_KV, GATE_ROWS, SEQ), F32)],
        scratch_shapes=[pltpu.VMEM((SEQ, NSA_KV), F32)],
        compiler_params=_cparams(("arbitrary",)),
        name="kvprep",
    )(kv, kv, kv, kv, ng, pe, w1, w2)


def _group_diag(w):
    eye = jnp.eye(N_KV, dtype=w.dtype)
    out = jnp.einsum('...gde,gh->...gdhe', w, eye)
    return out.reshape(w.shape[:-3] + (N_KV * w.shape[-2], N_KV * w.shape[-1]))


def _lane_tile(x, n):
    return jnp.concatenate([x] * n, axis=1)


ONES_ROWS = 16


def _with_ones(v_t):
    return jnp.concatenate([v_t, jnp.ones((ONES_ROWS, v_t.shape[1]), BF16)], axis=0)


def _nsa_kernel(q_ref, kc_ref, vct_ref, ks_ref, vst_ref, kw_ref, vwt_ref, gt_ref, ovt_ref, blk_ref,
                y_ref, sa_ref, sb_ref, oslc_ref):
    i = pl.program_id(1)
    tq = NSA_TQ
    nl = HPG * tq
    t_row = i * tq + lax.broadcasted_iota(jnp.int32, (1, tq), 1)
    groups = range(N_KV)
    vrows = [slice(g * HEAD_DIM, (g + 1) * HEAD_DIM) for g in groups]

    def scores(k, q_t):
        return jnp.dot(k, q_t, preferred_element_type=F32)

    def normalized(acc):
        return acc[:HEAD_DIM, :] / acc[HEAD_DIM:HEAD_DIM + 1, :]

    lane = lax.broadcasted_iota(jnp.int32, (tq, LANES), 1)
    scale = HEAD_DIM ** -0.5
    q4 = [jnp.concatenate(
        [(jnp.where((lane >= HEAD_DIM) == (g == 1),
                    q_ref[:, j * LANES:(j + 1) * LANES].astype(F32), 0.0) * scale).T
         for j in range(HPG)], axis=1).astype(BF16) for g in groups]

    n_idx = lax.broadcasted_iota(jnp.int32, (N_CHUNK, 1), 0)
    ok = n_idx * CMP_STRIDE + (CMP_LEN - 1) <= _lane_tile(t_row, HPG)
    blk = lax.broadcasted_iota(jnp.int32, (N_SLC, 1), 0)
    cur = jnp.right_shift(t_row, 6)
    forced = (blk == 0) | (blk == cur) | (blk == cur - 1)
    causal_blk = blk * SLC_LEN <= t_row
    o_cmp, q_aug = [], []
    for g in groups:
        s = jnp.where(ok, scores(kc_ref[0], q4[g]), NEG_INF)
        m = jnp.max(s, axis=0, keepdims=True)
        e = jnp.where(ok, jnp.exp(s - m), 0.0)
        l = jnp.sum(e, axis=0, keepdims=True)
        p = e / jnp.where(l > 0.0, l, 1.0)
        o_cmp.append(jnp.dot(vct_ref[0, vrows[g], :], p.astype(BF16), preferred_element_type=F32))
        psum = p[:, 0:tq]
        for j in range(1, HPG):
            psum = psum + p[:, j * tq:(j + 1) * tq]
        p_hi = psum.astype(BF16)
        p_lo = (psum - p_hi.astype(F32)).astype(BF16)
        imp = (jnp.dot(ovt_ref[...], p_hi, preferred_element_type=F32)
               + jnp.dot(ovt_ref[...], p_lo, preferred_element_type=F32))
        work = jnp.where(forced, -3e38, jnp.where(causal_blk, imp, -1.0))
        sel = jnp.where(forced, 1.0, 0.0)
        for _ in range(SLC_TOPN - MAX_FORCED):
            mx = jnp.max(work, axis=0, keepdims=True)
            idx = jnp.min(jnp.where(work == mx, blk, N_SLC), axis=0, keepdims=True)
            pick = blk == idx
            sel = jnp.where(pick & (mx >= 0.0), 1.0, sel)
            work = jnp.where(pick, -3e38, work)
        sel_bias = jnp.where(sel > 0.0, 0.0, NEG_INF)
        sel_bias = jnp.concatenate([sel_bias, jnp.zeros((LANES - N_SLC, tq), F32)], axis=0)
        sel_bias = _lane_tile(sel_bias.astype(BF16), HPG)
        q_aug.append(jnp.concatenate([q4[g], sel_bias], axis=0))

    start = pl.multiple_of(jnp.maximum(i - WIN // tq, 0) * tq, tq)
    k_win = kw_ref[pl.ds(start, WIN_KEYS), :]
    delta = t_row - (start + lax.broadcasted_iota(jnp.int32, (WIN_KEYS, 1), 0))
    win_bias = _lane_tile(jnp.where((delta >= 0) & (delta < WIN), 0.0, NEG_INF), HPG)
    o_win = []
    for g in groups:
        sT = scores(k_win, q4[g]) + win_bias
        pT = jnp.exp((sT - jnp.max(sT, axis=0, keepdims=True)).astype(BF16))
        vT = _with_ones(vwt_ref[vrows[g], pl.ds(start, WIN_KEYS)])
        o_win.append(normalized(jnp.dot(vT, pT, preferred_element_type=F32)))

    def slc_scores(kt, dst_ref):
        k0 = pl.multiple_of(kt * SLC_KT, SLC_KT)
        k_aug = jnp.concatenate([ks_ref[pl.ds(k0, SLC_KT), :], blk_ref[pl.ds(k0, SLC_KT), :]],
                                axis=1)
        maxima = []
        for g in groups:
            s_t = scores(k_aug, q_aug[g])
            dst_ref[g] = s_t
            maxima.append(jnp.max(s_t, axis=0, keepdims=True))
        return maxima

    def slc_update(kt, state, s_t, tile_max):
        k0 = pl.multiple_of(kt * SLC_KT, SLC_KT)
        out = []
        for g in groups:
            m_i, acc = state[2 * g], state[2 * g + 1]
            m_new = jnp.maximum(m_i, tile_max[g])
            alpha = jnp.exp(m_i - m_new)
            pT = jnp.exp((s_t[g] - m_new).astype(BF16))
            vT = _with_ones(vst_ref[vrows[g], pl.ds(k0, SLC_KT)])
            out += [m_new, alpha * acc + jnp.dot(vT, pT, preferred_element_type=F32)]
        return out

    def from_ref(src_ref):
        return [src_ref[g] for g in groups]

    def slc_pair(p, carry):
        state, max_a = list(carry[:2 * N_KV]), list(carry[2 * N_KV:])
        max_b = slc_scores(2 * p + 1, sb_ref)
        state = slc_update(2 * p, state, from_ref(sa_ref), max_a)
        max_a = slc_scores(2 * p + 2, sa_ref)
        state = slc_update(2 * p + 1, state, from_ref(sb_ref), max_b)
        return tuple(state + max_a)

    def slc_finish(state, src_ref):
        kpos = last_kt * SLC_KT + lax.broadcasted_iota(jnp.int32, (SLC_KT, 1), 0)
        causal_bias = _lane_tile(jnp.where(kpos <= t_row, 0.0, NEG_INF), HPG)
        s_last = [s + causal_bias for s in from_ref(src_ref)]
        state = slc_update(last_kt, state, s_last,
                           [jnp.max(s, axis=0, keepdims=True) for s in s_last])
        for g in groups:
            oslc_ref[g] = normalized(state[2 * g + 1])

    last_kt = (i * tq) // SLC_KT
    init = ((jnp.full((1, nl), -3e38, F32), jnp.zeros((HEAD_DIM + ONES_ROWS, nl), F32)) * N_KV
            + tuple(slc_scores(0, sa_ref)))
    carry = lax.fori_loop(0, last_kt // 2, slc_pair, init)
    state, max_a = list(carry[:2 * N_KV]), list(carry[2 * N_KV:])
    odd = lax.rem(last_kt, 2) == 1

    @pl.when(odd)
    def _():
        slc_scores(last_kt, sb_ref)
        slc_finish(slc_update(last_kt - 1, state, from_ref(sa_ref), max_a), sb_ref)

    @pl.when(jnp.logical_not(odd))
    def _():
        slc_finish(state, sa_ref)

    o_slc = [oslc_ref[g] for g in groups]

    for g in groups:
        gates = jax.nn.sigmoid(gt_ref[0, g])

        def gate_row(br):
            return jnp.concatenate([gates[3 * j + br:3 * j + br + 1, :] for j in range(HPG)],
                                   axis=1)

        o = gate_row(0) * o_cmp[g] + gate_row(1) * o_slc[g] + gate_row(2) * o_win[g]
        for j in range(HPG):
            h0 = (g * HPG + j) * HEAD_DIM
            y_ref[0, h0:h0 + HEAD_DIM, :] = o[:, j * tq:(j + 1) * tq].astype(y_ref.dtype)


def _block_onehot():
    oh = (np.arange(SEQ)[:, None] // SLC_LEN) == np.arange(LANES)[None, :]
    return jnp.asarray(oh.astype(np.float32), dtype=BF16)


def _nsa(q, kc, vct, kv, vst, vwt, gt, ovt):
    nq = SEQ // NSA_TQ
    return pl.pallas_call(
        _nsa_kernel,
        grid=(BATCH, nq),
        in_specs=[pl.BlockSpec((NSA_TQ, NSA_Q), lambda b, i: (b * nq + i, 0)),
                  pl.BlockSpec((1, N_CHUNK, NSA_KV), lambda b, i: (b, 0, 0)),
                  pl.BlockSpec((1, NSA_KV, N_CHUNK), lambda b, i: (b, 0, 0)),
                  pl.BlockSpec((SEQ, NSA_KV), lambda b, i: (b, 2)),
                  pl.BlockSpec((None, NSA_KV, SEQ), lambda b, i: (b, 0, 0)),
                  pl.BlockSpec((SEQ, NSA_KV), lambda b, i: (b, 4)),
                  pl.BlockSpec((None, NSA_KV, SEQ), lambda b, i: (b, 0, 0)),
                  pl.BlockSpec((1, N_KV, GATE_ROWS, NSA_TQ), lambda b, i: (b, 0, 0, i)),
                  pl.BlockSpec((N_SLC, N_CHUNK), lambda b, i: (0, 0)),
                  pl.BlockSpec((SEQ, LANES), lambda b, i: (0, 0))],
        out_specs=pl.BlockSpec((1, NSA_Q, NSA_TQ), lambda b, i: (b, 0, i)),
        out_shape=jax.ShapeDtypeStruct((BATCH, NSA_Q, SEQ), BF16),
        scratch_shapes=[pltpu.VMEM((N_KV, SLC_KT, HPG * NSA_TQ), F32),
                        pltpu.VMEM((N_KV, SLC_KT, HPG * NSA_TQ), F32),
                        pltpu.VMEM((N_KV, HEAD_DIM, HPG * NSA_TQ), F32)],
        compiler_params=_cparams(("arbitrary", "arbitrary")),
        name="nsa",
    )(q, kc, vct, kv, vst, kv, vwt, gt, ovt, _block_onehot())


def _overlap_t():
    c_start = np.arange(N_CHUNK) * CMP_STRIDE
    s_start = np.arange(N_SLC) * SLC_LEN
    ov = ((c_start[None, :] <= s_start[:, None] + SLC_LEN - 1)
          & (c_start[None, :] + CMP_LEN - 1 >= s_start[:, None])
          & (np.arange(N_CHUNK)[None, :] < N_CMP))
    return jnp.asarray(ov.astype(np.float32), dtype=BF16)


def _merge_kernel(x_ref, g_ref, wmg_ref, ya_ref, yb_ref, yct_ref, yd_ref, wb_ref, wo_ref, o_ref):
    x = x_ref[...]
    nb = _rms(x, g_ref[...]).astype(BF16)
    yc = yct_ref[0].astype(F32).T.astype(BF16)
    ys = (ya_ref[...], yb_ref[...], yc, yd_ref[...])
    merged = jnp.zeros((MERGE_TM, D_MODEL), F32)
    for bi, y in enumerate(ys):
        gate = jax.nn.sigmoid(jnp.dot(nb, wmg_ref[:, bi * D_MODEL:(bi + 1) * D_MODEL],
                                      preferred_element_type=F32))
        merged = merged + gate * jnp.dot(y, wb_ref[bi], preferred_element_type=F32)
    o_ref[...] = x + jnp.dot(merged.astype(BF16), wo_ref[...], preferred_element_type=F32)


def _merge(x, g, wmg, ya, yb, yct, yd, wb, wo, layer):
    nt = SEQ // MERGE_TM
    tok = lambda i: (i, 0)
    const2 = lambda i: (0, 0)
    return pl.pallas_call(
        _merge_kernel,
        grid=(TOKENS // MERGE_TM,),
        in_specs=[pl.BlockSpec((MERGE_TM, D_MODEL), tok),
                  pl.BlockSpec((1, D_MODEL), const2),
                  pl.BlockSpec((None, D_MODEL, N_BRANCH * D_MODEL), lambda i: (layer, 0, 0)),
                  pl.BlockSpec((MERGE_TM, MIX), tok),
                  pl.BlockSpec((MERGE_TM, MIX), tok),
                  pl.BlockSpec((1, MIX, MERGE_TM), lambda i: (i // nt, 0, i % nt)),
                  pl.BlockSpec((MERGE_TM, MIX), tok),
                  pl.BlockSpec((None, N_BRANCH, MIX, D_MODEL), lambda i: (layer, 0, 0, 0)),
                  pl.BlockSpec((None, D_MODEL, D_MODEL), lambda i: (layer, 0, 0))],
        out_specs=pl.BlockSpec((MERGE_TM, D_MODEL), tok),
        out_shape=jax.ShapeDtypeStruct((TOKENS, D_MODEL), F32),
        compiler_params=_cparams(("arbitrary",)),
        name="merge",
    )(x, g, wmg, ya, yb, yct, yd, wb, wo)


def _router_logits(t, wr_ref, br_ref):
    w = wr_ref[...]
    t_hi, w_hi = t.astype(BF16), w.astype(BF16)
    t_lo = (t - t_hi.astype(F32)).astype(BF16)
    w_lo = (w - w_hi.astype(F32)).astype(BF16)
    dot = functools.partial(jnp.dot, preferred_element_type=F32)
    return dot(t_hi, w_hi) + dot(t_hi, w_lo) + dot(t_lo, w_hi) + br_ref[...]


def _top_group(logits):
    lane = lax.broadcasted_iota(jnp.int32, logits.shape, 1)
    is_grp = lane < N_GROUPS
    lg = jnp.where(is_grp, logits, NEG_INF)
    gmax = jnp.max(lg, axis=1, keepdims=True)
    grp_idx = jnp.min(jnp.where(is_grp & (lg == gmax), lane, ROUTER_PAD), axis=1, keepdims=True)
    return lg, gmax, grp_idx


def _group_weight(logits, grp_idx):
    lane = lax.broadcasted_iota(jnp.int32, logits.shape, 1)
    lg, gmax, _ = _top_group(logits)
    ge = jnp.where(lane < N_GROUPS, jnp.exp(lg - gmax), 0.0)
    return (jnp.sum(jnp.where(lane == grp_idx, ge, 0.0), axis=1, keepdims=True)
            / jnp.sum(ge, axis=1, keepdims=True))


ROUTE_ROWS = 32


def _route_kernel(x_ref, g_ref, wr_ref, br_ref, tri_ref, meta_ref, cnt_ref, run_ref):
    @pl.when(pl.program_id(0) == 0)
    def _():
        run_ref[...] = jnp.zeros_like(run_ref)

    logits = _router_logits(_rms(x_ref[...], g_ref[...]), wr_ref, br_ref)
    lt = logits.T[:ROUTE_ROWS, :]
    row = lax.broadcasted_iota(jnp.int32, (ROUTE_ROWS, 1), 0)

    def top(mask, vals):
        v = jnp.where(mask, vals, NEG_INF)
        best = jnp.max(v, axis=0, keepdims=True)
        return v, jnp.min(jnp.where(mask & (v == best), row, ROUTE_ROWS), axis=0, keepdims=True)

    _, grp_idx = top(row < N_GROUPS, lt)
    first = N_GROUPS + grp_idx * EXPERTS_PER_GROUP
    in_grp = (row >= first) & (row < first + EXPERTS_PER_GROUP)
    le, i1 = top(in_grp, lt)
    _, i2 = top(in_grp & (row != i1), le)
    lo = jnp.minimum(i1, i2) - first
    hi = jnp.maximum(i1, i2) - first
    pair = jnp.right_shift(lo * (2 * EXPERTS_PER_GROUP - 1 - lo), 1) + hi - lo - 1
    cls = grp_idx * N_PAIRS + pair
    onehot = jnp.where(row == cls, 1.0, 0.0)
    before = jnp.dot(onehot.astype(BF16), tri_ref[...], preferred_element_type=F32) + run_ref[...]
    rank = jnp.sum(onehot * before, axis=0, keepdims=True).astype(jnp.int32)
    run_ref[...] += jnp.sum(onehot, axis=1, keepdims=True)
    sub = lax.broadcasted_iota(jnp.int32, (SUBLANES, 1), 0)
    meta_ref[...] = jnp.where(sub == 0, cls, jnp.where(sub == 1, rank, 0))
    cnt_ref[...] = jnp.broadcast_to(run_ref[...], cnt_ref.shape)


def _route(x, g, wr, br):
    tok = lambda i: (i, 0)
    const2 = lambda i: (0, 0)
    tri = jnp.asarray(np.triu(np.ones((ROUTE_TM, ROUTE_TM), np.float32), 1), dtype=BF16)
    return pl.pallas_call(
        _route_kernel,
        grid=(TOKENS // ROUTE_TM,),
        in_specs=[pl.BlockSpec((ROUTE_TM, D_MODEL), tok),
                  pl.BlockSpec((1, D_MODEL), const2),
                  pl.BlockSpec((D_MODEL, ROUTER_PAD), const2),
                  pl.BlockSpec((1, ROUTER_PAD), const2),
                  pl.BlockSpec((ROUTE_TM, ROUTE_TM), const2)],
        out_specs=[pl.BlockSpec((SUBLANES, ROUTE_TM), lambda i: (0, i)),
                   pl.BlockSpec((ROUTE_ROWS, LANES), const2)],
        out_shape=[jax.ShapeDtypeStruct((SUBLANES, TOKENS), jnp.int32),
                   jax.ShapeDtypeStruct((ROUTE_ROWS, LANES), F32)],
        scratch_shapes=[pltpu.VMEM((ROUTE_ROWS, 1), F32)],
        compiler_params=_cparams(("arbitrary",)),
        name="route",
    )(x, g, wr, br, tri)


MOVE_SLOTS = 3


def _move_rows_kernel(idx_ref, seg_ref, len_ref, src_hbm, dst_hbm, buf, zbuf, in_sem, out_sem,
                      pad_sem, *, scatter):
    c = pl.program_id(0)
    n = pl.num_programs(0)
    slot = lax.rem(c, MOVE_SLOTS)
    nxt = lax.rem(c + 1, MOVE_SLOTS)
    rc = ROW_CHUNK
    chunk = lambda k: pl.ds(k * rc, rc)

    if scatter:
        def read(k, sl):
            return [pltpu.make_async_copy(src_hbm.at[chunk(k)], buf.at[sl], in_sem.at[sl])]

        def write(k, sl):
            return [pltpu.make_async_copy(buf.at[sl, pl.ds(r, 1)],
                                          dst_hbm.at[pl.ds(idx_ref[k * rc + r], 1)], out_sem.at[sl])
                    for r in range(rc)]

        def write_done(sl):
            return pltpu.make_async_copy(buf.at[sl], dst_hbm.at[chunk(0)], out_sem.at[sl])

        def read_done(sl):
            return read(0, sl)[0]
    else:
        def read(k, sl):
            return [pltpu.make_async_copy(src_hbm.at[pl.ds(idx_ref[k * rc + r], 1)],
                                          buf.at[sl, pl.ds(r, 1)], in_sem.at[sl])
                    for r in range(rc)]

        def write(k, sl):
            return [pltpu.make_async_copy(buf.at[sl], dst_hbm.at[chunk(k)], out_sem.at[sl])]

        def write_done(sl):
            return write(0, sl)[0]

        def read_done(sl):
            return pltpu.make_async_copy(src_hbm.at[chunk(0)], buf.at[sl], in_sem.at[sl])

    def pad_fills(fn):
        for sg in range(N_CLASSES):
            start, length = seg_ref[sg], len_ref[sg]
            head = jnp.bitwise_and(-start, SUBLANES - 1)
            for r in range(SUBLANES - 1):
                @pl.when(r < head)
                def _():
                    fn(pltpu.make_async_copy(zbuf.at[pl.ds(0, 1)], dst_hbm.at[pl.ds(start + r, 1)],
                                             pad_sem))
            body = length - head
            bit = rc // 2
            while bit >= SUBLANES:
                done = body - jnp.bitwise_and(body, 2 * bit - 1)
                first = pl.multiple_of(start + head + done, SUBLANES)

                @pl.when(jnp.bitwise_and(body, bit) != 0)
                def _():
                    fn(pltpu.make_async_copy(zbuf.at[pl.ds(0, bit)], dst_hbm.at[pl.ds(first, bit)],
                                             pad_sem))
                bit //= 2
        tail = pl.multiple_of(seg_ref[N_CLASSES], rc)

        def tail_tile(k, carry):
            fn(pltpu.make_async_copy(zbuf, dst_hbm.at[pl.ds(tail + k * rc, rc)], pad_sem))
            return carry
        lax.fori_loop(0, len_ref[N_CLASSES] // rc, tail_tile, 0)

    @pl.when(c == 0)
    def _():
        for cp in read(0, 0):
            cp.start()
        if scatter:
            zbuf[...] = jnp.zeros_like(zbuf)
            pad_fills(lambda cp: cp.start())

    @pl.when(c + 1 < n)
    def _():
        @pl.when(c >= 2)
        def _():
            write_done(nxt).wait()
        for cp in read(c + 1, nxt):
            cp.start()

    read_done(slot).wait()
    for cp in write(c, slot):
        cp.start()

    @pl.when(c == n - 1)
    def _():
        write_done(nxt).wait()
        write_done(lax.rem(c + 2, MOVE_SLOTS)).wait()
        write_done(slot).wait()
        if scatter:
            pad_fills(lambda cp: cp.wait())


def _move_rows(idx, seg_start, seg_len, src, n_out, scatter):
    any_spec = pl.BlockSpec(memory_space=pl.ANY)
    grid_spec = pltpu.PrefetchScalarGridSpec(
        num_scalar_prefetch=3, grid=(TOKENS // ROW_CHUNK,), in_specs=[any_spec], out_specs=any_spec,
        scratch_shapes=[pltpu.VMEM((MOVE_SLOTS, ROW_CHUNK, D_MODEL), F32),
                        pltpu.VMEM((ROW_CHUNK, D_MODEL), F32),
                        pltpu.SemaphoreType.DMA((MOVE_SLOTS,)),
                        pltpu.SemaphoreType.DMA((MOVE_SLOTS,)),
                        pltpu.SemaphoreType.DMA(())])
    return pl.pallas_call(
        functools.partial(_move_rows_kernel, scatter=scatter),
        grid_spec=grid_spec,
        out_shape=jax.ShapeDtypeStruct((n_out, D_MODEL), F32),
        compiler_params=_cparams(("arbitrary",)),
        name="dispatch" if scatter else "collect",
    )(idx, seg_start, seg_len, src)


def _experts_kernel(ea_ref, eb_ref, tv_ref, tb_ref, x_ref, g_ref, wr_ref, br_ref, wga_ref, wua_ref,
                    wda_ref, wgb_ref, wub_ref, wdb_ref, fg_ref, o_ref, *, layer, final_norm):
    j = pl.program_id(0)
    nv = tv_ref[j]

    @pl.when(nv > 0)
    def _():
        x = x_ref[...]
        t = _rms(x, g_ref[...])
        logits = _router_logits(t, wr_ref, br_ref)
        lane = lax.broadcasted_iota(jnp.int32, logits.shape, 1)
        ea = ea_ref[j] - layer * N_EXPERTS
        eb = eb_ref[j] - layer * N_EXPERTS
        la = jnp.sum(jnp.where(lane == N_GROUPS + ea, logits, 0.0), axis=1, keepdims=True)
        lb = jnp.sum(jnp.where(lane == N_GROUPS + eb, logits, 0.0), axis=1, keepdims=True)
        top = jnp.maximum(la, lb)
        pa, pb = jnp.exp(la - top), jnp.exp(lb - top)
        scale = _group_weight(logits, jnp.right_shift(ea, 2)) / (pa + pb)
        tb = t.astype(BF16)
        acc = jnp.zeros((MOE_TM, D_MODEL), F32)
        for wg, wu, wd, w in ((wga_ref, wua_ref, wda_ref, pa * scale),
                              (wgb_ref, wub_ref, wdb_ref, pb * scale)):
            hid = (jax.nn.silu(jnp.dot(tb, wg[0], preferred_element_type=F32))
                   * jnp.dot(tb, wu[0], preferred_element_type=F32)) * w
            acc = acc + jnp.dot(hid.astype(BF16), wd[0], preferred_element_type=F32)
        h = x + acc
        if final_norm:
            h = _rms(h, fg_ref[...])
        o_ref[...] = h

    @pl.when(nv == 0)
    def _():
        o_ref[...] = jnp.zeros_like(o_ref)


def _experts(tile_ea, tile_eb, tile_valid, tile_block, hs, g, wr, br, wg, wu, wd, fg, layer,
             final_norm):
    const2 = lambda j, *_: (0, 0)
    of_a = lambda j, ea, eb, tv, tb: (ea[j], 0, 0)
    of_b = lambda j, ea, eb, tv, tb: (eb[j], 0, 0)
    up = lambda idx: pl.BlockSpec((1, D_MODEL, D_EXPERT), idx)
    down = lambda idx: pl.BlockSpec((1, D_EXPERT, D_MODEL), idx)
    grid_spec = pltpu.PrefetchScalarGridSpec(
        num_scalar_prefetch=4,
        grid=(MOE_NT,),
        in_specs=[pl.BlockSpec((MOE_TM, D_MODEL), lambda j, ea, eb, tv, tb: (tb[j], 0)),
                  pl.BlockSpec((1, D_MODEL), const2),
                  pl.BlockSpec((D_MODEL, ROUTER_PAD), const2),
                  pl.BlockSpec((1, ROUTER_PAD), const2),
                  up(of_a), up(of_a), down(of_a), up(of_b), up(of_b), down(of_b),
                  pl.BlockSpec((1, D_MODEL), const2)],
        out_specs=pl.BlockSpec((MOE_TM, D_MODEL), lambda j, *_: (j, 0)))
    return pl.pallas_call(
        functools.partial(_experts_kernel, layer=layer, final_norm=final_norm),
        grid_spec=grid_spec,
        out_shape=jax.ShapeDtypeStruct((MOE_ROWS, D_MODEL), F32),
        compiler_params=_cparams(("arbitrary",)),
        name="experts",
    )(tile_ea, tile_eb, tile_valid, tile_block, hs, g, wr, br, wg, wu, wd, wg, wu, wd, fg)


_PAIR_LO = np.array([a for a in range(EXPERTS_PER_GROUP) for b in range(a + 1, EXPERTS_PER_GROUP)])
_PAIR_HI = np.array([b for a in range(EXPERTS_PER_GROUP) for b in range(a + 1, EXPERTS_PER_GROUP)])


def _moe(x, g, wr, br, wg, wu, wd, fg, layer, final_norm):
    meta, cnt = _route(x, g, wr, br)
    cls, rank = meta[0], meta[1]
    counts = cnt[:N_CLASSES, 0].astype(jnp.int32)
    padded = (counts + MOE_TM - 1) // MOE_TM * MOE_TM
    ends = jnp.cumsum(padded)
    starts = ends - padded
    pos = starts[cls] + rank
    tile_start = jnp.arange(MOE_NT, dtype=jnp.int32) * MOE_TM
    tile_cls = jnp.minimum(jnp.sum(tile_start[:, None] >= ends[None, :], axis=1), N_CLASSES - 1)
    tile_valid = jnp.clip(starts[tile_cls] + counts[tile_cls] - tile_start, 0, MOE_TM)
    tile_valid = tile_valid.astype(jnp.int32)
    tile_block = jnp.where(tile_valid > 0, jnp.arange(MOE_NT, dtype=jnp.int32), 0)
    first = layer * N_EXPERTS + (tile_cls // N_PAIRS) * EXPERTS_PER_GROUP
    tile_ea = (first + jnp.asarray(_PAIR_LO)[tile_cls % N_PAIRS]).astype(jnp.int32)
    tile_eb = (first + jnp.asarray(_PAIR_HI)[tile_cls % N_PAIRS]).astype(jnp.int32)
    seg_start = jnp.concatenate([starts + counts, ends[-1:]])
    seg_len = jnp.concatenate([padded - counts, MOE_ROWS - ends[-1:]])
    hs = _move_rows(pos, seg_start, seg_len, x, MOE_ROWS, scatter=True)
    ys = _experts(tile_ea, tile_eb, tile_valid, tile_block, hs, g, wr, br, wg, wu, wd, fg, layer,
                  final_norm)
    return _move_rows(pos, seg_start, seg_len, ys, TOKENS, scatter=False)


def _block_diag(w):
    eye = jnp.eye(LRU_BLOCKS, dtype=w.dtype)
    return jnp.einsum('hij,hk->hikj', w, eye).reshape(MIX, MIX)


def _w_in_column_ranges():
    cuts = [int(c) for c in np.cumsum((0,) + IN_SPLITS)]
    q_parts = [(cuts[4] + (g * HPG + j) * HEAD_DIM, cuts[4] + (g * HPG + j + 1) * HEAD_DIM)
               for j in range(HPG) for g in range(N_KV)]
    return ([(cuts[8], cuts[9]), (cuts[0], cuts[4])] + q_parts
            + [(cuts[5], cuts[6]), (cuts[7], cuts[8]), (cuts[6], cuts[7])])


PREP_PIECE = 64
PREP_BLOCK = 512


def _prep_piece_starts():
    starts = []
    for a, b in _w_in_column_ranges():
        starts += list(range(a, b, PREP_PIECE))
    starts += [0] * (2 * W_IN_HALF // PREP_PIECE - len(starts))
    return np.asarray(starts, np.int32)


def _prep_w_in_kernel(src_ref, wt_hbm, o_ref, buf, sem):
    layer, j = pl.program_id(0), pl.program_id(1)
    nj = pl.num_programs(1)
    per_block = PREP_BLOCK // PREP_PIECE
    slot = lax.rem(j, 2)

    def pieces(blk, sl):
        return [pltpu.make_async_copy(
            wt_hbm.at[layer, pl.ds(pl.multiple_of(src_ref[blk * per_block + k], SUBLANES), PREP_PIECE)],
            buf.at[sl, pl.ds(k * PREP_PIECE, PREP_PIECE)], sem.at[sl]) for k in range(per_block)]

    @pl.when(j == 0)
    def _():
        for cp in pieces(0, 0):
            cp.start()

    @pl.when(j + 1 < nj)
    def _():
        for cp in pieces(j + 1, 1 - slot):
            cp.start()

    for cp in pieces(j, slot):
        cp.wait()
    o_ref[...] = buf[slot].T.astype(o_ref.dtype)


def _prep_w_in(w_in):
    grid_spec = pltpu.PrefetchScalarGridSpec(
        num_scalar_prefetch=1,
        grid=(DEPTH, 2 * W_IN_HALF // PREP_BLOCK),
        in_specs=[pl.BlockSpec(memory_space=pl.ANY)],
        out_specs=pl.BlockSpec((None, D_MODEL, PREP_BLOCK), lambda l, j, src: (l, 0, j)),
        scratch_shapes=[pltpu.VMEM((2, PREP_BLOCK, D_MODEL), F32), pltpu.SemaphoreType.DMA((2,))])
    return pl.pallas_call(
        _prep_w_in_kernel,
        grid_spec=grid_spec,
        out_shape=jax.ShapeDtypeStruct((DEPTH, D_MODEL, 2 * W_IN_HALF), BF16),
        compiler_params=_cparams(("arbitrary", "arbitrary")),
        name="prep_w_in",
    )(jnp.asarray(_prep_piece_starts()), w_in.transpose(0, 2, 1))


def _nsa_mixer(q, kv, ng, p):
    w1 = p['cmp_w1'].reshape(2, N_KV, CMP_LEN, HEAD_DIM, HEAD_DIM).transpose(0, 2, 1, 3, 4)
    pe = jnp.concatenate([p['cmp_pe']] * N_KV, axis=-1)
    kc, vct, vst, vwt, gt = _kvprep(kv, ng, pe, _group_diag(w1).astype(BF16),
                                    _group_diag(p['cmp_w2']).astype(BF16))
    return _nsa(q, kc, vct, kv, vst, vwt, gt, _overlap_t())


def _layer(h, p, big, layer, final_g, final_norm):
    row = lambda a: a.reshape(1, -1)

    u, v, gb, rb, q, kv, xd, ng = _proj(h, row(p['norm1_g']), big['w_in'], layer)

    bs = jnp.broadcast_to(p['gm_b'][:, :, None], (GM_GROUPS, GM_CHUNK, GM_GW))
    y_a, y_b, y_d = _mixers(
        u, v, gb, rb, xd, row(p['gm_norm_g']), p['gm_ws'], bs,
        p['conv_w'], row(p['conv_b']), _block_diag(p['lru_wa']).astype(BF16), row(p['lru_ba']),
        _block_diag(p['lru_wx']).astype(BF16), row(p['lru_bx']), row(p['lru_lambda']),
        p['pool_w'].astype(BF16), row(p['pool_scale']))

    y_ct = _nsa_mixer(q, kv, ng, p)

    h = _merge(h, row(p['norm1_g']), big['w_in'], y_a, y_b, y_ct, y_d, big['w_branch'],
               big['w_out'], layer)

    wr = jnp.concatenate([p['router_w_group'], p['router_w_expert']], axis=1)
    wr = jnp.pad(wr, ((0, 0), (0, ROUTER_PAD - wr.shape[1])))
    br = jnp.concatenate([p['router_b_group'], p['router_b_expert']])
    br = jnp.pad(br, (0, ROUTER_PAD - br.shape[0])).reshape(1, ROUTER_PAD)
    return _moe(h, row(p['norm2_g']), wr, br, big['moe_w_gate'], big['moe_w_up'],
                big['moe_w_down'], row(final_g), layer, final_norm)


_LAYER_PARAMS = ('norm1_g', 'gm_norm_g', 'gm_ws', 'gm_b', 'conv_w', 'conv_b', 'lru_wa',
                 'lru_ba', 'lru_wx', 'lru_bx', 'lru_lambda', 'cmp_pe', 'cmp_w1', 'cmp_w2', 'pool_w',
                 'pool_scale', 'norm2_g', 'router_w_group', 'router_b_group',
                 'router_w_expert', 'router_b_expert')


def kernel(x, norm1_g, w_in, gm_norm_g, gm_ws, gm_b, conv_w, conv_b, lru_wa, lru_ba, lru_wx,
           lru_bx, lru_lambda, cmp_pe, cmp_w1, cmp_w2, pool_w, pool_scale, w_branch, w_out,
           norm2_g, router_w_group, router_b_group, router_w_expert, router_b_expert,
           moe_w_gate, moe_w_up, moe_w_down, final_norm_g):
    stacked = dict(zip(_LAYER_PARAMS, (
        norm1_g, gm_norm_g, gm_ws, gm_b, conv_w, conv_b, lru_wa, lru_ba, lru_wx, lru_bx,
        lru_lambda, cmp_pe, cmp_w1, cmp_w2, pool_w, pool_scale, norm2_g,
        router_w_group, router_b_group, router_w_expert, router_b_expert)))
    w_all = _prep_w_in(w_in)
    experts = lambda w: w.astype(BF16).reshape((DEPTH * N_EXPERTS,) + w.shape[2:])
    big = dict(w_in=w_all, w_branch=w_branch.astype(BF16), w_out=w_out.astype(BF16),
               moe_w_gate=experts(moe_w_gate), moe_w_up=experts(moe_w_up),
               moe_w_down=experts(moe_w_down))
    h = x.reshape(TOKENS, D_MODEL)
    for layer in range(DEPTH):
        p = {k: a[layer] for k, a in stacked.items()}
        h = _layer(h, p, big, layer, final_norm_g, final_norm=(layer == DEPTH - 1))
    return h.reshape(BATCH, SEQ, D_MODEL)
```

```python
import functools

import numpy as np
import jax
import jax.numpy as jnp
from jax import lax
from jax.experimental import pallas as pl
from jax.experimental.pallas import tpu as pltpu

F32 = jnp.float32
BF16 = jnp.bfloat16

D_MODEL = 1024
BATCH = 4
SEQ = 4096
TOKENS = BATCH * SEQ
DEPTH = 2
MIX = D_MODEL // 2
GM_CHUNK = 128
GM_GROUPS = 4
GM_GW = MIX // GM_GROUPS
CONV_WIDTH = 4
LRU_BLOCKS = 8
LRU_BW = MIX // LRU_BLOCKS
LRU_C = 8.0
N_HEADS = 8
HEAD_DIM = MIX // N_HEADS
N_KV = 2
HPG = N_HEADS // N_KV
CMP_LEN = 32
CMP_STRIDE = 16
SLC_LEN = 64
SLC_TOPN = 8
MAX_FORCED = 3
WIN = 512
NSA_Q = N_HEADS * HEAD_DIM
NSA_KV = N_KV * HEAD_DIM
POOL_WINDOWS = (2, 4, 8, 16)
POOL_GW = MIX // len(POOL_WINDOWS)
N_BRANCH = 4
N_GROUPS = 4
EXPERTS_PER_GROUP = 4
N_EXPERTS = N_GROUPS * EXPERTS_PER_GROUP
D_EXPERT = D_MODEL // 2
EPS = 1e-6
NEG_INF = -1e30
IN_SPLITS = (MIX, MIX, MIX, MIX, NSA_Q, 6 * NSA_KV, 3 * N_HEADS, MIX, N_BRANCH * D_MODEL)

N_CHUNK = SEQ // CMP_STRIDE
N_CMP = N_CHUNK - CMP_LEN // CMP_STRIDE + 1
N_SLC = SEQ // SLC_LEN

LANES = 128
SUBLANES = 8
GATE_PAD = LANES
GATE_ROWS = 16
ROUTER_PAD = LANES
VMEM_LIMIT = 56 * 1024 * 1024

PROJ_WIDTHS = (MIX, MIX, MIX, MIX, NSA_Q, 6 * NSA_KV, MIX, GATE_PAD)
W_IN_HALF = N_BRANCH * D_MODEL
PROJ_TM = 1024
GMLP_TM = 512
LRU_TS = 512
POOL_TS = 512
NSA_TQ = 256
SLC_KT = 256
WIN_KEYS = WIN + NSA_TQ
MERGE_TM = 512
ROUTE_TM = 512
MOE_TM = 256
N_PAIRS = EXPERTS_PER_GROUP * (EXPERTS_PER_GROUP - 1) // 2
N_CLASSES = N_GROUPS * N_PAIRS
MOE_ROWS = TOKENS + N_CLASSES * MOE_TM
MOE_NT = MOE_ROWS // MOE_TM
ROW_CHUNK = 256


def _cparams(sem):
    return pltpu.CompilerParams(dimension_semantics=sem, vmem_limit_bytes=VMEM_LIMIT)


def _rms(x, g):
    return x * lax.rsqrt(jnp.mean(x * x, axis=-1, keepdims=True) + EPS) * g


def _proj_kernel(x_ref, g_ref, w_ref, *out_refs):
    nb = _rms(x_ref[...], g_ref[...]).astype(BF16)
    off = 0
    for ref in out_refs:
        w = ref.shape[-1]
        ref[...] = jnp.dot(nb, w_ref[:, off:off + w], preferred_element_type=F32).astype(ref.dtype)
        off += w


def _proj(x, g, w, layer):
    out_shape = [jax.ShapeDtypeStruct((TOKENS, wd), BF16) for wd in PROJ_WIDTHS[:-1]]
    out_shape.append(jax.ShapeDtypeStruct((TOKENS, GATE_PAD), F32))
    return pl.pallas_call(
        _proj_kernel,
        grid=(TOKENS // PROJ_TM,),
        in_specs=[pl.BlockSpec((PROJ_TM, D_MODEL), lambda i: (i, 0)),
                  pl.BlockSpec((1, D_MODEL), lambda i: (0, 0)),
                  pl.BlockSpec((None, D_MODEL, W_IN_HALF), lambda i: (layer, 0, 1))],
        out_specs=[pl.BlockSpec((PROJ_TM, wd), lambda i: (i, 0)) for wd in PROJ_WIDTHS],
        out_shape=out_shape,
        compiler_params=_cparams(("arbitrary",)),
        name="proj",
    )(x, g, w)


def _gmlp_kernel(u_ref, v_ref, g_ref, ws_ref, bs_ref, o_ref):
    u = jax.nn.gelu(u_ref[...].astype(F32))
    v = _rms(jax.nn.gelu(v_ref[...].astype(F32)), g_ref[...]).astype(BF16)
    row = lax.broadcasted_iota(jnp.int32, (GM_CHUNK, GM_CHUNK), 0)
    col = lax.broadcasted_iota(jnp.int32, (GM_CHUNK, GM_CHUNK), 1)
    causal = row >= col
    for gi in range(GM_GROUPS):
        w = jnp.where(causal, ws_ref[gi], 0.0).astype(BF16)
        cs = slice(gi * GM_GW, (gi + 1) * GM_GW)
        for c in range(GMLP_TM // GM_CHUNK):
            rs = slice(c * GM_CHUNK, (c + 1) * GM_CHUNK)
            mixed = jnp.dot(w, v[rs, cs], preferred_element_type=F32) + bs_ref[gi]
            o_ref[rs, cs] = (u[rs, cs] * mixed).astype(o_ref.dtype)


LRU_TAIL = 8


def _lru_kernel(gb_ref, rb_ref, cw_ref, cb_ref, wa_ref, ba_ref, wx_ref, bx_ref, lam_ref, o_ref,
                tail_ref, h_ref):
    @pl.when(pl.program_id(1) == 0)
    def _():
        tail_ref[...] = jnp.zeros_like(tail_ref)
        h_ref[...] = jnp.zeros_like(h_ref)

    ts = LRU_TS
    x = rb_ref[...].astype(F32)
    ext = jnp.concatenate([tail_ref[...], x], axis=0)
    tail_ref[...] = x[ts - LRU_TAIL:, :]
    xc = cb_ref[...] + x * cw_ref[CONV_WIDTH - 1:CONV_WIDTH, :]
    for d in range(1, CONV_WIDTH):
        xs = pltpu.roll(ext, d, axis=0)[LRU_TAIL:, :]
        xc = xc + xs * cw_ref[CONV_WIDTH - 1 - d:CONV_WIDTH - d, :]
    xcb = xc.astype(BF16)
    r = jax.nn.sigmoid(jnp.dot(xcb, wa_ref[...], preferred_element_type=F32) + ba_ref[...])
    ig = jax.nn.sigmoid(jnp.dot(xcb, wx_ref[...], preferred_element_type=F32) + bx_ref[...])
    z = -lam_ref[...]
    softplus = jnp.maximum(z, 0.0) + jnp.log1p(jnp.exp(-jnp.abs(z)))
    log_a = -LRU_C * r * softplus
    a = jnp.exp(log_a)
    b = jnp.sqrt(1.0 - jnp.exp(2.0 * log_a)) * (ig * xc)
    rows = lax.broadcasted_iota(jnp.int32, (ts, 1), 0)
    d = 1
    while d < ts:
        valid = rows >= d
        a_prev = pltpu.roll(a, d, axis=0)
        b_prev = pltpu.roll(b, d, axis=0)
        b = jnp.where(valid, a * b_prev, 0.0) + b
        a = jnp.where(valid, a * a_prev, a)
        d *= 2
    h = a * h_ref[...] + b
    h_ref[...] = h[ts - 1:ts, :]
    o_ref[...] = (jax.nn.gelu(gb_ref[...].astype(F32)) * h).astype(o_ref.dtype)


POOL_TAIL = 16


def _pool_kernel(x_ref, w_ref, sc_ref, o_ref, tail_ref):
    s_id = pl.program_id(1)

    @pl.when(s_id == 0)
    def _():
        tail_ref[...] = jnp.zeros_like(tail_ref)

    ts = POOL_TS
    x = x_ref[...].astype(F32)
    ext = jnp.concatenate([tail_ref[...], x], axis=0)
    tail_ref[...] = x[ts - POOL_TAIL:, :]
    pos = s_id * ts + lax.broadcasted_iota(jnp.int32, (ts, 1), 0)
    acc = ext
    width = 1
    for gi, wdw in enumerate(POOL_WINDOWS):
        while width < wdw:
            acc = acc + pltpu.roll(acc, width, axis=0)
            width *= 2
        cs = slice(gi * POOL_GW, (gi + 1) * POOL_GW)
        cnt = jnp.minimum(pos + 1, wdw).astype(F32)
        pooled = acc[POOL_TAIL:, cs] / cnt - x[:, cs]
        mixed = jnp.dot(pooled.astype(BF16), w_ref[gi], preferred_element_type=F32)
        o_ref[:, cs] = (mixed * sc_ref[:, cs]).astype(o_ref.dtype)


def _mixers_kernel(u_ref, v_ref, gb_ref, rb_ref, xd_ref, gm_g_ref, gm_ws_ref, gm_bs_ref, cw_ref,
                   cb_ref, wa_ref, ba_ref, wx_ref, bx_ref, lam_ref, pw_ref, psc_ref,
                   ya_ref, yb_ref, yd_ref, lru_tail_ref, lru_h_ref, pool_tail_ref):
    _gmlp_kernel(u_ref, v_ref, gm_g_ref, gm_ws_ref, gm_bs_ref, ya_ref)
    _lru_kernel(gb_ref, rb_ref, cw_ref, cb_ref, wa_ref, ba_ref, wx_ref, bx_ref, lam_ref, yb_ref,
                lru_tail_ref, lru_h_ref)
    _pool_kernel(xd_ref, pw_ref, psc_ref, yd_ref, pool_tail_ref)


def _mixers(u, v, gb, rb, xd, gm_g, gm_ws, gm_bs, cw, cb, wa, ba, wx, bx, lam, pw, psc):
    assert GMLP_TM == LRU_TS == POOL_TS
    nt = SEQ // LRU_TS
    tok = pl.BlockSpec((LRU_TS, MIX), lambda b, s: (b * nt + s, 0))
    whole = lambda a: pl.BlockSpec(a.shape, lambda b, s: (0,) * a.ndim)
    small = (gm_g, gm_ws, gm_bs, cw, cb, wa, ba, wx, bx, lam, pw, psc)
    return pl.pallas_call(
        _mixers_kernel,
        grid=(BATCH, nt),
        in_specs=[tok] * 5 + [whole(a) for a in small],
        out_specs=[tok] * 3,
        out_shape=[jax.ShapeDtypeStruct((TOKENS, MIX), BF16)] * 3,
        scratch_shapes=[pltpu.VMEM((LRU_TAIL, MIX), F32), pltpu.VMEM((1, MIX), F32),
                        pltpu.VMEM((POOL_TAIL, MIX), F32)],
        compiler_params=_cparams(("arbitrary", "arbitrary")),
        name="mixers",
    )(u, v, gb, rb, xd, *small)


def _kvprep_kernel(kc_in, vc_in, vs_in, vw_in, ng_in, pe_ref, w1_ref, w2_ref, kc_ref, vct_ref,
                   vst_ref, vwt_ref, gt_ref, xf_ref):
    row = lax.broadcasted_iota(jnp.int32, (N_CHUNK, 1), 0)
    for c, x_in in enumerate((kc_in, vc_in)):
        xf_ref[...] = x_in[...].astype(F32)
        first = jnp.zeros((N_CHUNK, NSA_KV), F32)
        second = jnp.zeros((N_CHUNK, NSA_KV), F32)
        for l in range(CMP_STRIDE):
            xl = xf_ref[pl.ds(l, N_CHUNK, stride=CMP_STRIDE), :]
            first += jnp.dot((xl + pe_ref[c, l:l + 1, :]).astype(BF16), w1_ref[c, l],
                             preferred_element_type=F32)
            l2 = l + CMP_STRIDE
            second += jnp.dot((xl + pe_ref[c, l2:l2 + 1, :]).astype(BF16), w1_ref[c, l2],
                              preferred_element_type=F32)
        hid = jax.nn.gelu(first + pltpu.roll(second, N_CHUNK - 1, axis=0))
        out = jnp.dot(hid.astype(BF16), w2_ref[c], preferred_element_type=F32)
        out = jnp.where(row < N_CMP, out, 0.0)
        if c == 0:
            kc_ref[0] = out.astype(kc_ref.dtype)
        else:
            vct_ref[0] = out.T.astype(vct_ref.dtype)
    vst_ref[0] = vs_in[...].astype(F32).T.astype(vst_ref.dtype)
    vwt_ref[0] = vw_in[...].astype(F32).T.astype(vwt_ref.dtype)
    gates_t = ng_in[...].T
    gt_ref[...] = jnp.zeros_like(gt_ref)
    per_group = 3 * HPG
    for g in range(N_KV):
        gt_ref[0, g, :per_group, :] = gates_t[g * per_group:(g + 1) * per_group, :]


def _kvprep(kv, ng, pe, w1, w2):
    col = lambda j: pl.BlockSpec((SEQ, NSA_KV), lambda b: (b, j))
    whole = lambda a: pl.BlockSpec(a.shape, lambda b: (0,) * a.ndim)
    return pl.pallas_call(
        _kvprep_kernel,
        grid=(BATCH,),
        in_specs=[col(0), col(1), col(3), col(5), pl.BlockSpec((SEQ, GATE_PAD), lambda b: (b, 0)),
                  whole(pe), whole(w1), whole(w2)],
        out_specs=[pl.BlockSpec((1, N_CHUNK, NSA_KV), lambda b: (b, 0, 0)),
                   pl.BlockSpec((1, NSA_KV, N_CHUNK), lambda b: (b, 0, 0)),
                   pl.BlockSpec((1, NSA_KV, SEQ), lambda b: (b, 0, 0)),
                   pl.BlockSpec((1, NSA_KV, SEQ), lambda b: (b, 0, 0)),
                   pl.BlockSpec((1, N_KV, GATE_ROWS, SEQ), lambda b: (b, 0, 0, 0))],
        out_shape=[jax.ShapeDtypeStruct((BATCH, N_CHUNK, NSA_KV), BF16),
                   jax.ShapeDtypeStruct((BATCH, NSA_KV, N_CHUNK), BF16),
                   jax.ShapeDtypeStruct((BATCH, NSA_KV, SEQ), BF16),
                   jax.ShapeDtypeStruct((BATCH, NSA_KV, SEQ), BF16),
                   jax.ShapeDtypeStruct((BATCH, N_KV, GATE_ROWS, SEQ), F32)],
        scratch_shapes=[pltpu.VMEM((SEQ, NSA_KV), F32)],
        compiler_params=_cparams(("arbitrary",)),
        name="kvprep",
    )(kv, kv, kv, kv, ng, pe, w1, w2)


def _group_diag(w):
    eye = jnp.eye(N_KV, dtype=w.dtype)
    out = jnp.einsum('...gde,gh->...gdhe', w, eye)
    return out.reshape(w.shape[:-3] + (N_KV * w.shape[-2], N_KV * w.shape[-1]))


def _lane_tile(x, n):
    return jnp.concatenate([x] * n, axis=1)


ONES_ROWS = 16


def _with_ones(v_t):
    return jnp.concatenate([v_t, jnp.ones((ONES_ROWS, v_t.shape[1]), BF16)], axis=0)


def _nsa_kernel(q_ref, kc_ref, vct_ref, ks_ref, vst_ref, kw_ref, vwt_ref, gt_ref, ovt_ref, blk_ref,
                y_ref, sa_ref, sb_ref, oslc_ref):
    i = pl.program_id(1)
    tq = NSA_TQ
    nl = HPG * tq
    t_row = i * tq + lax.broadcasted_iota(jnp.int32, (1, tq), 1)
    groups = range(N_KV)
    vrows = [slice(g * HEAD_DIM, (g + 1) * HEAD_DIM) for g in groups]

    def scores(k, q_t):
        return jnp.dot(k, q_t, preferred_element_type=F32)

    def normalized(acc):
        return acc[:HEAD_DIM, :] / acc[HEAD_DIM:HEAD_DIM + 1, :]

    lane = lax.broadcasted_iota(jnp.int32, (tq, LANES), 1)
    scale = HEAD_DIM ** -0.5
    q4 = [jnp.concatenate(
        [(jnp.where((lane >= HEAD_DIM) == (g == 1),
                    q_ref[:, j * LANES:(j + 1) * LANES].astype(F32), 0.0) * scale).T
         for j in range(HPG)], axis=1).astype(BF16) for g in groups]

    n_idx = lax.broadcasted_iota(jnp.int32, (N_CHUNK, 1), 0)
    ok = n_idx * CMP_STRIDE + (CMP_LEN - 1) <= _lane_tile(t_row, HPG)
    blk = lax.broadcasted_iota(jnp.int32, (N_SLC, 1), 0)
    cur = jnp.right_shift(t_row, 6)
    forced = (blk == 0) | (blk == cur) | (blk == cur - 1)
    causal_blk = blk * SLC_LEN <= t_row
    o_cmp, q_aug = [], []
    for g in groups:
        s = jnp.where(ok, scores(kc_ref[0], q4[g]), NEG_INF)
        m = jnp.max(s, axis=0, keepdims=True)
        e = jnp.where(ok, jnp.exp(s - m), 0.0)
        l = jnp.sum(e, axis=0, keepdims=True)
        p = e / jnp.where(l > 0.0, l, 1.0)
        o_cmp.append(jnp.dot(vct_ref[0, vrows[g], :], p.astype(BF16), preferred_element_type=F32))
        psum = p[:, 0:tq]
        for j in range(1, HPG):
            psum = psum + p[:, j * tq:(j + 1) * tq]
        p_hi = psum.astype(BF16)
        p_lo = (psum - p_hi.astype(F32)).astype(BF16)
        imp = (jnp.dot(ovt_ref[...], p_hi, preferred_element_type=F32)
               + jnp.dot(ovt_ref[...], p_lo, preferred_element_type=F32))
        work = jnp.where(forced, -3e38, jnp.where(causal_blk, imp, -1.0))
        sel = jnp.where(forced, 1.0, 0.0)
        for _ in range(SLC_TOPN - MAX_FORCED):
            mx = jnp.max(work, axis=0, keepdims=True)
            idx = jnp.min(jnp.where(work == mx, blk, N_SLC), axis=0, keepdims=True)
            pick = blk == idx
            sel = jnp.where(pick & (mx >= 0.0), 1.0, sel)
            work = jnp.where(pick, -3e38, work)
        sel_bias = jnp.where(sel > 0.0, 0.0, NEG_INF)
        sel_bias = jnp.concatenate([sel_bias, jnp.zeros((LANES - N_SLC, tq), F32)], axis=0)
        sel_bias = _lane_tile(sel_bias.astype(BF16), HPG)
        q_aug.append(jnp.concatenate([q4[g], sel_bias], axis=0))

    start = pl.multiple_of(jnp.maximum(i - WIN // tq, 0) * tq, tq)
    k_win = kw_ref[pl.ds(start, WIN_KEYS), :]
    delta = t_row - (start + lax.broadcasted_iota(jnp.int32, (WIN_KEYS, 1), 0))
    win_bias = _lane_tile(jnp.where((delta >= 0) & (delta < WIN), 0.0, NEG_INF), HPG)
    o_win = []
    for g in groups:
        sT = scores(k_win, q4[g]) + win_bias
        pT = jnp.exp((sT - jnp.max(sT, axis=0, keepdims=True)).astype(BF16))
        vT = _with_ones(vwt_ref[vrows[g], pl.ds(start, WIN_KEYS)])
        o_win.append(normalized(jnp.dot(vT, pT, preferred_element_type=F32)))

    def slc_scores(kt, dst_ref):
        k0 = pl.multiple_of(kt * SLC_KT, SLC_KT)
        k_aug = jnp.concatenate([ks_ref[pl.ds(k0, SLC_KT), :], blk_ref[pl.ds(k0, SLC_KT), :]],
                                axis=1)
        maxima = []
        for g in groups:
            s_t = scores(k_aug, q_aug[g])
            dst_ref[g] = s_t
            maxima.append(jnp.max(s_t, axis=0, keepdims=True))
        return maxima

    def slc_update(kt, state, s_t, tile_max):
        k0 = pl.multiple_of(kt * SLC_KT, SLC_KT)
        out = []
        for g in groups:
            m_i, acc = state[2 * g], state[2 * g + 1]
            m_new = jnp.maximum(m_i, tile_max[g])
            alpha = jnp.exp(m_i - m_new)
            pT = jnp.exp((s_t[g] - m_new).astype(BF16))
            vT = _with_ones(vst_ref[vrows[g], pl.ds(k0, SLC_KT)])
            out += [m_new, alpha * acc + jnp.dot(vT, pT, preferred_element_type=F32)]
        return out

    def from_ref(src_ref):
        return [src_ref[g] for g in groups]

    def slc_pair(p, carry):
        state, max_a = list(carry[:2 * N_KV]), list(carry[2 * N_KV:])
        max_b = slc_scores(2 * p + 1, sb_ref)
        state = slc_update(2 * p, state, from_ref(sa_ref), max_a)
        max_a = slc_scores(2 * p + 2, sa_ref)
        state = slc_update(2 * p + 1, state, from_ref(sb_ref), max_b)
        return tuple(state + max_a)

    def slc_finish(state, src_ref):
        kpos = last_kt * SLC_KT + lax.broadcasted_iota(jnp.int32, (SLC_KT, 1), 0)
        causal_bias = _lane_tile(jnp.where(kpos <= t_row, 0.0, NEG_INF), HPG)
        s_last = [s + causal_bias for s in from_ref(src_ref)]
        state = slc_update(last_kt, state, s_last,
                           [jnp.max(s, axis=0, keepdims=True) for s in s_last])
        for g in groups:
            oslc_ref[g] = normalized(state[2 * g + 1])

    last_kt = (i * tq) // SLC_KT
    init = ((jnp.full((1, nl), -3e38, F32), jnp.zeros((HEAD_DIM + ONES_ROWS, nl), F32)) * N_KV
            + tuple(slc_scores(0, sa_ref)))
    carry = lax.fori_loop(0, last_kt // 2, slc_pair, init)
    state, max_a = list(carry[:2 * N_KV]), list(carry[2 * N_KV:])
    odd = lax.rem(last_kt, 2) == 1

    @pl.when(odd)
    def _():
        slc_scores(last_kt, sb_ref)
        slc_finish(slc_update(last_kt - 1, state, from_ref(sa_ref), max_a), sb_ref)

    @pl.when(jnp.logical_not(odd))
    def _():
        slc_finish(state, sa_ref)

    o_slc = [oslc_ref[g] for g in groups]

    for g in groups:
        gates = jax.nn.sigmoid(gt_ref[0, g])

        def gate_row(br):
            return jnp.concatenate([gates[3 * j + br:3 * j + br + 1, :] for j in range(HPG)],
                                   axis=1)

        o = gate_row(0) * o_cmp[g] + gate_row(1) * o_slc[g] + gate_row(2) * o_win[g]
        for j in range(HPG):
            h0 = (g * HPG + j) * HEAD_DIM
            y_ref[0, h0:h0 + HEAD_DIM, :] = o[:, j * tq:(j + 1) * tq].astype(y_ref.dtype)


def _block_onehot():
    oh = (np.arange(SEQ)[:, None] // SLC_LEN) == np.arange(LANES)[None, :]
    return jnp.asarray(oh.astype(np.float32), dtype=BF16)


def _nsa(q, kc, vct, kv, vst, vwt, gt, ovt):
    nq = SEQ // NSA_TQ
    return pl.pallas_call(
        _nsa_kernel,
        grid=(BATCH, nq),
        in_specs=[pl.BlockSpec((NSA_TQ, NSA_Q), lambda b, i: (b * nq + i, 0)),
                  pl.BlockSpec((1, N_CHUNK, NSA_KV), lambda b, i: (b, 0, 0)),
                  pl.BlockSpec((1, NSA_KV, N_CHUNK), lambda b, i: (b, 0, 0)),
                  pl.BlockSpec((SEQ, NSA_KV), lambda b, i: (b, 2)),
                  pl.BlockSpec((None, NSA_KV, SEQ), lambda b, i: (b, 0, 0)),
                  pl.BlockSpec((SEQ, NSA_KV), lambda b, i: (b, 4)),
                  pl.BlockSpec((None, NSA_KV, SEQ), lambda b, i: (b, 0, 0)),
                  pl.BlockSpec((1, N_KV, GATE_ROWS, NSA_TQ), lambda b, i: (b, 0, 0, i)),
                  pl.BlockSpec((N_SLC, N_CHUNK), lambda b, i: (0, 0)),
                  pl.BlockSpec((SEQ, LANES), lambda b, i: (0, 0))],
        out_specs=pl.BlockSpec((1, NSA_Q, NSA_TQ), lambda b, i: (b, 0, i)),
        out_shape=jax.ShapeDtypeStruct((BATCH, NSA_Q, SEQ), BF16),
        scratch_shapes=[pltpu.VMEM((N_KV, SLC_KT, HPG * NSA_TQ), F32),
                        pltpu.VMEM((N_KV, SLC_KT, HPG * NSA_TQ), F32),
                        pltpu.VMEM((N_KV, HEAD_DIM, HPG * NSA_TQ), F32)],
        compiler_params=_cparams(("arbitrary", "arbitrary")),
        name="nsa",
    )(q, kc, vct, kv, vst, kv, vwt, gt, ovt, _block_onehot())


def _overlap_t():
    c_start = np.arange(N_CHUNK) * CMP_STRIDE
    s_start = np.arange(N_SLC) * SLC_LEN
    ov = ((c_start[None, :] <= s_start[:, None] + SLC_LEN - 1)
          & (c_start[None, :] + CMP_LEN - 1 >= s_start[:, None])
          & (np.arange(N_CHUNK)[None, :] < N_CMP))
    return jnp.asarray(ov.astype(np.float32), dtype=BF16)


def _merge_kernel(x_ref, g_ref, wmg_ref, ya_ref, yb_ref, yct_ref, yd_ref, wb_ref, wo_ref, o_ref):
    x = x_ref[...]
    nb = _rms(x, g_ref[...]).astype(BF16)
    yc = yct_ref[0].astype(F32).T.astype(BF16)
    ys = (ya_ref[...], yb_ref[...], yc, yd_ref[...])
    merged = jnp.zeros((MERGE_TM, D_MODEL), F32)
    for bi, y in enumerate(ys):
        gate = jax.nn.sigmoid(jnp.dot(nb, wmg_ref[:, bi * D_MODEL:(bi + 1) * D_MODEL],
                                      preferred_element_type=F32))
        merged = merged + gate * jnp.dot(y, wb_ref[bi], preferred_element_type=F32)
    o_ref[...] = x + jnp.dot(merged.astype(BF16), wo_ref[...], preferred_element_type=F32)


def _merge(x, g, wmg, ya, yb, yct, yd, wb, wo, layer):
    nt = SEQ // MERGE_TM
    tok = lambda i: (i, 0)
    const2 = lambda i: (0, 0)
    return pl.pallas_call(
        _merge_kernel,
        grid=(TOKENS // MERGE_TM,),
        in_specs=[pl.BlockSpec((MERGE_TM, D_MODEL), tok),
                  pl.BlockSpec((1, D_MODEL), const2),
                  pl.BlockSpec((None, D_MODEL, N_BRANCH * D_MODEL), lambda i: (layer, 0, 0)),
                  pl.BlockSpec((MERGE_TM, MIX), tok),
                  pl.BlockSpec((MERGE_TM, MIX), tok),
                  pl.BlockSpec((1, MIX, MERGE_TM), lambda i: (i // nt, 0, i % nt)),
                  pl.BlockSpec((MERGE_TM, MIX), tok),
                  pl.BlockSpec((None, N_BRANCH, MIX, D_MODEL), lambda i: (layer, 0, 0, 0)),
                  pl.BlockSpec((None, D_MODEL, D_MODEL), lambda i: (layer, 0, 0))],
        out_specs=pl.BlockSpec((MERGE_TM, D_MODEL), tok),
        out_shape=jax.ShapeDtypeStruct((TOKENS, D_MODEL), F32),
        compiler_params=_cparams(("arbitrary",)),
        name="merge",
    )(x, g, wmg, ya, yb, yct, yd, wb, wo)


def _router_logits(t, wr_ref, br_ref, discrete):
    w = wr_ref[...]
    t_hi, w_hi = t.astype(BF16), w.astype(BF16)
    dot = functools.partial(jnp.dot, preferred_element_type=F32)
    logits = dot(t_hi, w_hi) + br_ref[...]
    if discrete:
        t_lo = (t - t_hi.astype(F32)).astype(BF16)
        w_lo = (w - w_hi.astype(F32)).astype(BF16)
        logits = logits + dot(t_hi, w_lo) + dot(t_lo, w_hi)
    return logits


def _top_group(logits):
    lane = lax.broadcasted_iota(jnp.int32, logits.shape, 1)
    is_grp = lane < N_GROUPS
    lg = jnp.where(is_grp, logits, NEG_INF)
    gmax = jnp.max(lg, axis=1, keepdims=True)
    grp_idx = jnp.min(jnp.where(is_grp & (lg == gmax), lane, ROUTER_PAD), axis=1, keepdims=True)
    return lg, gmax, grp_idx


def _group_weight(logits, grp_idx):
    lane = lax.broadcasted_iota(jnp.int32, logits.shape, 1)
    lg, gmax, _ = _top_group(logits)
    ge = jnp.where(lane < N_GROUPS, jnp.exp(lg - gmax), 0.0)
    return (jnp.sum(jnp.where(lane == grp_idx, ge, 0.0), axis=1, keepdims=True)
            / jnp.sum(ge, axis=1, keepdims=True))


ROUTE_ROWS = 32


def _route_kernel(x_ref, g_ref, wr_ref, br_ref, tri_ref, meta_ref, cnt_ref, run_ref):
    @pl.when(pl.program_id(0) == 0)
    def _():
        run_ref[...] = jnp.zeros_like(run_ref)

    logits = _router_logits(_rms(x_ref[...], g_ref[...]), wr_ref, br_ref, discrete=True)
    lt = logits.T[:ROUTE_ROWS, :]
    row = lax.broadcasted_iota(jnp.int32, (ROUTE_ROWS, 1), 0)

    def top(mask, vals):
        v = jnp.where(mask, vals, NEG_INF)
        best = jnp.max(v, axis=0, keepdims=True)
        return v, jnp.min(jnp.where(mask & (v == best), row, ROUTE_ROWS), axis=0, keepdims=True)

    _, grp_idx = top(row < N_GROUPS, lt)
    first = N_GROUPS + grp_idx * EXPERTS_PER_GROUP
    in_grp = (row >= first) & (row < first + EXPERTS_PER_GROUP)
    le, i1 = top(in_grp, lt)
    _, i2 = top(in_grp & (row != i1), le)
    lo = jnp.minimum(i1, i2) - first
    hi = jnp.maximum(i1, i2) - first
    pair = jnp.right_shift(lo * (2 * EXPERTS_PER_GROUP - 1 - lo), 1) + hi - lo - 1
    cls = grp_idx * N_PAIRS + pair
    onehot = jnp.where(row == cls, 1.0, 0.0)
    before = jnp.dot(onehot.astype(BF16), tri_ref[...], preferred_element_type=F32) + run_ref[...]
    rank = jnp.sum(onehot * before, axis=0, keepdims=True).astype(jnp.int32)
    run_ref[...] += jnp.sum(onehot, axis=1, keepdims=True)
    sub = lax.broadcasted_iota(jnp.int32, (SUBLANES, 1), 0)
    meta_ref[...] = jnp.where(sub == 0, cls, jnp.where(sub == 1, rank, 0))
    cnt_ref[...] = jnp.broadcast_to(run_ref[...], cnt_ref.shape)


def _route(x, g, wr, br):
    tok = lambda i: (i, 0)
    const2 = lambda i: (0, 0)
    tri = jnp.asarray(np.triu(np.ones((ROUTE_TM, ROUTE_TM), np.float32), 1), dtype=BF16)
    return pl.pallas_call(
        _route_kernel,
        grid=(TOKENS // ROUTE_TM,),
        in_specs=[pl.BlockSpec((ROUTE_TM, D_MODEL), tok),
                  pl.BlockSpec((1, D_MODEL), const2),
                  pl.BlockSpec((D_MODEL, ROUTER_PAD), const2),
                  pl.BlockSpec((1, ROUTER_PAD), const2),
                  pl.BlockSpec((ROUTE_TM, ROUTE_TM), const2)],
        out_specs=[pl.BlockSpec((SUBLANES, ROUTE_TM), lambda i: (0, i)),
                   pl.BlockSpec((ROUTE_ROWS, LANES), const2)],
        out_shape=[jax.ShapeDtypeStruct((SUBLANES, TOKENS), jnp.int32),
                   jax.ShapeDtypeStruct((ROUTE_ROWS, LANES), F32)],
        scratch_shapes=[pltpu.VMEM((ROUTE_ROWS, 1), F32)],
        compiler_params=_cparams(("arbitrary",)),
        name="route",
    )(x, g, wr, br, tri)


MOVE_SLOTS = 3


def _move_rows_kernel(idx_ref, seg_ref, len_ref, src_hbm, dst_hbm, buf, zbuf, in_sem, out_sem,
                      pad_sem, *, scatter):
    c = pl.program_id(0)
    n = pl.num_programs(0)
    slot = lax.rem(c, MOVE_SLOTS)
    nxt = lax.rem(c + 1, MOVE_SLOTS)
    rc = ROW_CHUNK
    chunk = lambda k: pl.ds(k * rc, rc)

    if scatter:
        def read(k, sl):
            return [pltpu.make_async_copy(src_hbm.at[chunk(k)], buf.at[sl], in_sem.at[sl])]

        def write(k, sl):
            return [pltpu.make_async_copy(buf.at[sl, pl.ds(r, 1)],
                                          dst_hbm.at[pl.ds(idx_ref[k * rc + r], 1)], out_sem.at[sl])
                    for r in range(rc)]

        def write_done(sl):
            return pltpu.make_async_copy(buf.at[sl], dst_hbm.at[chunk(0)], out_sem.at[sl])

        def read_done(sl):
            return read(0, sl)[0]
    else:
        def read(k, sl):
            return [pltpu.make_async_copy(src_hbm.at[pl.ds(idx_ref[k * rc + r], 1)],
                                          buf.at[sl, pl.ds(r, 1)], in_sem.at[sl])
                    for r in range(rc)]

        def write(k, sl):
            return [pltpu.make_async_copy(buf.at[sl], dst_hbm.at[chunk(k)], out_sem.at[sl])]

        def write_done(sl):
            return write(0, sl)[0]

        def read_done(sl):
            return pltpu.make_async_copy(src_hbm.at[chunk(0)], buf.at[sl], in_sem.at[sl])

    def pad_fills(fn):
        for sg in range(N_CLASSES):
            start, length = seg_ref[sg], len_ref[sg]
            head = jnp.bitwise_and(-start, SUBLANES - 1)
            for r in range(SUBLANES - 1):
                @pl.when(r < head)
                def _():
                    fn(pltpu.make_async_copy(zbuf.at[pl.ds(0, 1)], dst_hbm.at[pl.ds(start + r, 1)],
                                             pad_sem))
            body = length - head
            bit = rc // 2
            while bit >= SUBLANES:
                done = body - jnp.bitwise_and(body, 2 * bit - 1)
                first = pl.multiple_of(start + head + done, SUBLANES)

                @pl.when(jnp.bitwise_and(body, bit) != 0)
                def _():
                    fn(pltpu.make_async_copy(zbuf.at[pl.ds(0, bit)], dst_hbm.at[pl.ds(first, bit)],
                                             pad_sem))
                bit //= 2
        tail = pl.multiple_of(seg_ref[N_CLASSES], rc)

        def tail_tile(k, carry):
            fn(pltpu.make_async_copy(zbuf, dst_hbm.at[pl.ds(tail + k * rc, rc)], pad_sem))
            return carry
        lax.fori_loop(0, len_ref[N_CLASSES] // rc, tail_tile, 0)

    @pl.when(c == 0)
    def _():
        for cp in read(0, 0):
            cp.start()
        if scatter:
            zbuf[...] = jnp.zeros_like(zbuf)
            pad_fills(lambda cp: cp.start())

    @pl.when(c + 1 < n)
    def _():
        @pl.when(c >= 2)
        def _():
            write_done(nxt).wait()
        for cp in read(c + 1, nxt):
            cp.start()

    read_done(slot).wait()
    for cp in write(c, slot):
        cp.start()

    @pl.when(c == n - 1)
    def _():
        write_done(nxt).wait()
        write_done(lax.rem(c + 2, MOVE_SLOTS)).wait()
        write_done(slot).wait()
        if scatter:
            pad_fills(lambda cp: cp.wait())


def _move_rows(idx, seg_start, seg_len, src, n_out, scatter):
    any_spec = pl.BlockSpec(memory_space=pl.ANY)
    grid_spec = pltpu.PrefetchScalarGridSpec(
        num_scalar_prefetch=3, grid=(TOKENS // ROW_CHUNK,), in_specs=[any_spec], out_specs=any_spec,
        scratch_shapes=[pltpu.VMEM((MOVE_SLOTS, ROW_CHUNK, D_MODEL), F32),
                        pltpu.VMEM((ROW_CHUNK, D_MODEL), F32),
                        pltpu.SemaphoreType.DMA((MOVE_SLOTS,)),
                        pltpu.SemaphoreType.DMA((MOVE_SLOTS,)),
                        pltpu.SemaphoreType.DMA(())])
    return pl.pallas_call(
        functools.partial(_move_rows_kernel, scatter=scatter),
        grid_spec=grid_spec,
        out_shape=jax.ShapeDtypeStruct((n_out, D_MODEL), F32),
        compiler_params=_cparams(("arbitrary",)),
        name="dispatch" if scatter else "collect",
    )(idx, seg_start, seg_len, src)


def _experts_kernel(ea_ref, eb_ref, tv_ref, tb_ref, x_ref, g_ref, wr_ref, br_ref, wga_ref, wua_ref,
                    wda_ref, wgb_ref, wub_ref, wdb_ref, fg_ref, o_ref, *, layer, final_norm):
    j = pl.program_id(0)
    nv = tv_ref[j]

    @pl.when(nv > 0)
    def _():
        x = x_ref[...]
        t = _rms(x, g_ref[...])
        logits = _router_logits(t, wr_ref, br_ref, discrete=False)
        lane = lax.broadcasted_iota(jnp.int32, logits.shape, 1)
        ea = ea_ref[j] - layer * N_EXPERTS
        eb = eb_ref[j] - layer * N_EXPERTS
        la = jnp.sum(jnp.where(lane == N_GROUPS + ea, logits, 0.0), axis=1, keepdims=True)
        lb = jnp.sum(jnp.where(lane == N_GROUPS + eb, logits, 0.0), axis=1, keepdims=True)
        top = jnp.maximum(la, lb)
        pa, pb = jnp.exp(la - top), jnp.exp(lb - top)
        scale = _group_weight(logits, jnp.right_shift(ea, 2)) / (pa + pb)
        tb = t.astype(BF16)
        acc = jnp.zeros((MOE_TM, D_MODEL), F32)
        for wg, wu, wd, w in ((wga_ref, wua_ref, wda_ref, pa * scale),
                              (wgb_ref, wub_ref, wdb_ref, pb * scale)):
            hid = (jax.nn.silu(jnp.dot(tb, wg[0], preferred_element_type=F32))
                   * jnp.dot(tb, wu[0], preferred_element_type=F32)) * w
            acc = acc + jnp.dot(hid.astype(BF16), wd[0], preferred_element_type=F32)
        h = x + acc
        if final_norm:
            h = _rms(h, fg_ref[...])
        o_ref[...] = h

    @pl.when(nv == 0)
    def _():
        o_ref[...] = jnp.zeros_like(o_ref)


def _experts(tile_ea, tile_eb, tile_valid, tile_block, hs, g, wr, br, wg, wu, wd, fg, layer,
             final_norm):
    const2 = lambda j, *_: (0, 0)
    of_a = lambda j, ea, eb, tv, tb: (ea[j], 0, 0)
    of_b = lambda j, ea, eb, tv, tb: (eb[j], 0, 0)
    up = lambda idx: pl.BlockSpec((1, D_MODEL, D_EXPERT), idx)
    down = lambda idx: pl.BlockSpec((1, D_EXPERT, D_MODEL), idx)
    grid_spec = pltpu.PrefetchScalarGridSpec(
        num_scalar_prefetch=4,
        grid=(MOE_NT,),
        in_specs=[pl.BlockSpec((MOE_TM, D_MODEL), lambda j, ea, eb, tv, tb: (tb[j], 0)),
                  pl.BlockSpec((1, D_MODEL), const2),
                  pl.BlockSpec((D_MODEL, ROUTER_PAD), const2),
                  pl.BlockSpec((1, ROUTER_PAD), const2),
                  up(of_a), up(of_a), down(of_a), up(of_b), up(of_b), down(of_b),
                  pl.BlockSpec((1, D_MODEL), const2)],
        out_specs=pl.BlockSpec((MOE_TM, D_MODEL), lambda j, *_: (j, 0)))
    return pl.pallas_call(
        functools.partial(_experts_kernel, layer=layer, final_norm=final_norm),
        grid_spec=grid_spec,
        out_shape=jax.ShapeDtypeStruct((MOE_ROWS, D_MODEL), F32),
        compiler_params=_cparams(("arbitrary",)),
        name="experts",
    )(tile_ea, tile_eb, tile_valid, tile_block, hs, g, wr, br, wg, wu, wd, wg, wu, wd, fg)


_PAIR_LO = np.array([a for a in range(EXPERTS_PER_GROUP) for b in range(a + 1, EXPERTS_PER_GROUP)])
_PAIR_HI = np.array([b for a in range(EXPERTS_PER_GROUP) for b in range(a + 1, EXPERTS_PER_GROUP)])


def _moe(x, g, wr, br, wg, wu, wd, fg, layer, final_norm):
    meta, cnt = _route(x, g, wr, br)
    cls, rank = meta[0], meta[1]
    counts = cnt[:N_CLASSES, 0].astype(jnp.int32)
    padded = (counts + MOE_TM - 1) // MOE_TM * MOE_TM
    ends = jnp.cumsum(padded)
    starts = ends - padded
    pos = starts[cls] + rank
    tile_start = jnp.arange(MOE_NT, dtype=jnp.int32) * MOE_TM
    tile_cls = jnp.minimum(jnp.sum(tile_start[:, None] >= ends[None, :], axis=1), N_CLASSES - 1)
    tile_valid = jnp.clip(starts[tile_cls] + counts[tile_cls] - tile_start, 0, MOE_TM)
    tile_valid = tile_valid.astype(jnp.int32)
    tile_block = jnp.where(tile_valid > 0, jnp.arange(MOE_NT, dtype=jnp.int32), 0)
    first = layer * N_EXPERTS + (tile_cls // N_PAIRS) * EXPERTS_PER_GROUP
    tile_ea = (first + jnp.asarray(_PAIR_LO)[tile_cls % N_PAIRS]).astype(jnp.int32)
    tile_eb = (first + jnp.asarray(_PAIR_HI)[tile_cls % N_PAIRS]).astype(jnp.int32)
    seg_start = jnp.concatenate([starts + counts, ends[-1:]])
    seg_len = jnp.concatenate([padded - counts, MOE_ROWS - ends[-1:]])
    hs = _move_rows(pos, seg_start, seg_len, x, MOE_ROWS, scatter=True)
    ys = _experts(tile_ea, tile_eb, tile_valid, tile_block, hs, g, wr, br, wg, wu, wd, fg, layer,
                  final_norm)
    return _move_rows(pos, seg_start, seg_len, ys, TOKENS, scatter=False)


def _block_diag(w):
    eye = jnp.eye(LRU_BLOCKS, dtype=w.dtype)
    return jnp.einsum('hij,hk->hikj', w, eye).reshape(MIX, MIX)


def _w_in_column_ranges():
    cuts = [int(c) for c in np.cumsum((0,) + IN_SPLITS)]
    q_parts = [(cuts[4] + (g * HPG + j) * HEAD_DIM, cuts[4] + (g * HPG + j + 1) * HEAD_DIM)
               for j in range(HPG) for g in range(N_KV)]
    return ([(cuts[8], cuts[9]), (cuts[0], cuts[4])] + q_parts
            + [(cuts[5], cuts[6]), (cuts[7], cuts[8]), (cuts[6], cuts[7])])


PREP_PIECE = 64
PREP_BLOCK = 512


def _prep_piece_starts():
    starts = []
    for a, b in _w_in_column_ranges():
        starts += list(range(a, b, PREP_PIECE))
    starts += [0] * (2 * W_IN_HALF // PREP_PIECE - len(starts))
    return np.asarray(starts, np.int32)


def _prep_w_in_kernel(src_ref, wt_hbm, o_ref, buf, sem):
    layer, j = pl.program_id(0), pl.program_id(1)
    nj = pl.num_programs(1)
    per_block = PREP_BLOCK // PREP_PIECE
    slot = lax.rem(j, 2)

    def pieces(blk, sl):
        return [pltpu.make_async_copy(
            wt_hbm.at[layer, pl.ds(pl.multiple_of(src_ref[blk * per_block + k], SUBLANES), PREP_PIECE)],
            buf.at[sl, pl.ds(k * PREP_PIECE, PREP_PIECE)], sem.at[sl]) for k in range(per_block)]

    @pl.when(j == 0)
    def _():
        for cp in pieces(0, 0):
            cp.start()

    @pl.when(j + 1 < nj)
    def _():
        for cp in pieces(j + 1, 1 - slot):
            cp.start()

    for cp in pieces(j, slot):
        cp.wait()
    o_ref[...] = buf[slot].T.astype(o_ref.dtype)


def _prep_w_in(w_in):
    grid_spec = pltpu.PrefetchScalarGridSpec(
        num_scalar_prefetch=1,
        grid=(DEPTH, 2 * W_IN_HALF // PREP_BLOCK),
        in_specs=[pl.BlockSpec(memory_space=pl.ANY)],
        out_specs=pl.BlockSpec((None, D_MODEL, PREP_BLOCK), lambda l, j, src: (l, 0, j)),
        scratch_shapes=[pltpu.VMEM((2, PREP_BLOCK, D_MODEL), F32), pltpu.SemaphoreType.DMA((2,))])
    return pl.pallas_call(
        _prep_w_in_kernel,
        grid_spec=grid_spec,
        out_shape=jax.ShapeDtypeStruct((DEPTH, D_MODEL, 2 * W_IN_HALF), BF16),
        compiler_params=_cparams(("arbitrary", "arbitrary")),
        name="prep_w_in",
    )(jnp.asarray(_prep_piece_starts()), w_in.transpose(0, 2, 1))


def _nsa_mixer(q, kv, ng, p):
    w1 = p['cmp_w1'].reshape(2, N_KV, CMP_LEN, HEAD_DIM, HEAD_DIM).transpose(0, 2, 1, 3, 4)
    pe = jnp.concatenate([p['cmp_pe']] * N_KV, axis=-1)
    kc, vct, vst, vwt, gt = _kvprep(kv, ng, pe, _group_diag(w1).astype(BF16),
                                    _group_diag(p['cmp_w2']).astype(BF16))
    return _nsa(q, kc, vct, kv, vst, vwt, gt, _overlap_t())


def _layer(h, p, big, layer, final_g, final_norm):
    row = lambda a: a.reshape(1, -1)

    u, v, gb, rb, q, kv, xd, ng = _proj(h, row(p['norm1_g']), big['w_in'], layer)

    bs = jnp.broadcast_to(p['gm_b'][:, :, None], (GM_GROUPS, GM_CHUNK, GM_GW))
    y_a, y_b, y_d = _mixers(
        u, v, gb, rb, xd, row(p['gm_norm_g']), p['gm_ws'], bs,
        p['conv_w'], row(p['conv_b']), _block_diag(p['lru_wa']).astype(BF16), row(p['lru_ba']),
        _block_diag(p['lru_wx']).astype(BF16), row(p['lru_bx']), row(p['lru_lambda']),
        p['pool_w'].astype(BF16), row(p['pool_scale']))

    y_ct = _nsa_mixer(q, kv, ng, p)

    h = _merge(h, row(p['norm1_g']), big['w_in'], y_a, y_b, y_ct, y_d, big['w_branch'],
               big['w_out'], layer)

    wr = jnp.concatenate([p['router_w_group'], p['router_w_expert']], axis=1)
    wr = jnp.pad(wr, ((0, 0), (0, ROUTER_PAD - wr.shape[1])))
    br = jnp.concatenate([p['router_b_group'], p['router_b_expert']])
    br = jnp.pad(br, (0, ROUTER_PAD - br.shape[0])).reshape(1, ROUTER_PAD)
    return _moe(h, row(p['norm2_g']), wr, br, big['moe_w_gate'], big['moe_w_up'],
                big['moe_w_down'], row(final_g), layer, final_norm)


_LAYER_PARAMS = ('norm1_g', 'gm_norm_g', 'gm_ws', 'gm_b', 'conv_w', 'conv_b', 'lru_wa',
                 'lru_ba', 'lru_wx', 'lru_bx', 'lru_lambda', 'cmp_pe', 'cmp_w1', 'cmp_w2', 'pool_w',
                 'pool_scale', 'norm2_g', 'router_w_group', 'router_b_group',
                 'router_w_expert', 'router_b_expert')


def kernel(x, norm1_g, w_in, gm_norm_g, gm_ws, gm_b, conv_w, conv_b, lru_wa, lru_ba, lru_wx,
           lru_bx, lru_lambda, cmp_pe, cmp_w1, cmp_w2, pool_w, pool_scale, w_branch, w_out,
           norm2_g, router_w_group, router_b_group, router_w_expert, router_b_expert,
           moe_w_gate, moe_w_up, moe_w_down, final_norm_g):
    stacked = dict(zip(_LAYER_PARAMS, (
        norm1_g, gm_norm_g, gm_ws, gm_b, conv_w, conv_b, lru_wa, lru_ba, lru_wx, lru_bx,
        lru_lambda, cmp_pe, cmp_w1, cmp_w2, pool_w, pool_scale, norm2_g,
        router_w_group, router_b_group, router_w_expert, router_b_expert)))
    w_all = _prep_w_in(w_in)
    experts = lambda w: w.astype(BF16).reshape((DEPTH * N_EXPERTS,) + w.shape[2:])
    big = dict(w_in=w_all, w_branch=w_branch.astype(BF16), w_out=w_out.astype(BF16),
               moe_w_gate=experts(moe_w_gate), moe_w_up=experts(moe_w_up),
               moe_w_down=experts(moe_w_down))
    h = x.reshape(TOKENS, D_MODEL)
    for layer in range(DEPTH):
        p = {k: a[layer] for k, a in stacked.items()}
        h = _layer(h, p, big, layer, final_norm_g, final_norm=(layer == DEPTH - 1))
    return h.reshape(BATCH, SEQ, D_MODEL)
```

```python
import functools

import numpy as np
import jax
import jax.numpy as jnp
from jax import lax
from jax.experimental import pallas as pl
from jax.experimental.pallas import tpu as pltpu

F32 = jnp.float32
BF16 = jnp.bfloat16

D_MODEL = 1024
BATCH = 4
SEQ = 4096
TOKENS = BATCH * SEQ
DEPTH = 2
MIX = D_MODEL // 2
GM_CHUNK = 128
GM_GROUPS = 4
GM_GW = MIX // GM_GROUPS
CONV_WIDTH = 4
LRU_BLOCKS = 8
LRU_BW = MIX // LRU_BLOCKS
LRU_C = 8.0
N_HEADS = 8
HEAD_DIM = MIX // N_HEADS
N_KV = 2
HPG = N_HEADS // N_KV
CMP_LEN = 32
CMP_STRIDE = 16
SLC_LEN = 64
SLC_TOPN = 8
MAX_FORCED = 3
WIN = 512
NSA_Q = N_HEADS * HEAD_DIM
NSA_KV = N_KV * HEAD_DIM
POOL_WINDOWS = (2, 4, 8, 16)
POOL_GW = MIX // len(POOL_WINDOWS)
N_BRANCH = 4
N_GROUPS = 4
EXPERTS_PER_GROUP = 4
N_EXPERTS = N_GROUPS * EXPERTS_PER_GROUP
D_EXPERT = D_MODEL // 2
EPS = 1e-6
NEG_INF = -1e30
IN_SPLITS = (MIX, MIX, MIX, MIX, NSA_Q, 6 * NSA_KV, 3 * N_HEADS, MIX, N_BRANCH * D_MODEL)

N_CHUNK = SEQ // CMP_STRIDE
N_CMP = N_CHUNK - CMP_LEN // CMP_STRIDE + 1
N_SLC = SEQ // SLC_LEN

LANES = 128
SUBLANES = 8
GATE_PAD = LANES
GATE_ROWS = 16
ROUTER_PAD = LANES
VMEM_LIMIT = 56 * 1024 * 1024

PROJ_WIDTHS = (MIX, MIX, MIX, MIX, NSA_Q, 6 * NSA_KV, MIX, GATE_PAD)
W_IN_HALF = N_BRANCH * D_MODEL
PROJ_TM = 1024
GMLP_TM = 512
LRU_TS = 512
POOL_TS = 512
NSA_TQ = 256
SLC_KT = 256
WIN_KEYS = WIN + NSA_TQ
MERGE_TM = 512
ROUTE_TM = 512
MOE_TM = 256
N_PAIRS = EXPERTS_PER_GROUP * (EXPERTS_PER_GROUP - 1) // 2
N_CLASSES = N_GROUPS * N_PAIRS
MOE_ROWS = TOKENS + N_CLASSES * MOE_TM
MOE_NT = MOE_ROWS // MOE_TM
ROW_CHUNK = 256


def _cparams(sem):
    return pltpu.CompilerParams(dimension_semantics=sem, vmem_limit_bytes=VMEM_LIMIT)


def _rms(x, g):
    return x * lax.rsqrt(jnp.mean(x * x, axis=-1, keepdims=True) + EPS) * g


def _proj_kernel(x_ref, g_ref, w_ref, *out_refs):
    nb = _rms(x_ref[...], g_ref[...]).astype(BF16)
    off = 0
    for ref in out_refs:
        w = ref.shape[-1]
        ref[...] = jnp.dot(nb, w_ref[:, off:off + w], preferred_element_type=F32).astype(ref.dtype)
        off += w


def _proj(x, g, w, layer):
    out_shape = [jax.ShapeDtypeStruct((TOKENS, wd), BF16) for wd in PROJ_WIDTHS[:-1]]
    out_shape.append(jax.ShapeDtypeStruct((TOKENS, GATE_PAD), F32))
    return pl.pallas_call(
        _proj_kernel,
        grid=(TOKENS // PROJ_TM,),
        in_specs=[pl.BlockSpec((PROJ_TM, D_MODEL), lambda i: (i, 0)),
                  pl.BlockSpec((1, D_MODEL), lambda i: (0, 0)),
                  pl.BlockSpec((None, D_MODEL, W_IN_HALF), lambda i: (layer, 0, 1))],
        out_specs=[pl.BlockSpec((PROJ_TM, wd), lambda i: (i, 0)) for wd in PROJ_WIDTHS],
        out_shape=out_shape,
        compiler_params=_cparams(("arbitrary",)),
        name="proj",
    )(x, g, w)


def _gmlp_kernel(u_ref, v_ref, g_ref, ws_ref, bs_ref, o_ref):
    u = jax.nn.gelu(u_ref[...].astype(F32))
    v = _rms(jax.nn.gelu(v_ref[...].astype(F32)), g_ref[...]).astype(BF16)
    row = lax.broadcasted_iota(jnp.int32, (GM_CHUNK, GM_CHUNK), 0)
    col = lax.broadcasted_iota(jnp.int32, (GM_CHUNK, GM_CHUNK), 1)
    causal = row >= col
    for gi in range(GM_GROUPS):
        w = jnp.where(causal, ws_ref[gi], 0.0).astype(BF16)
        cs = slice(gi * GM_GW, (gi + 1) * GM_GW)
        for c in range(GMLP_TM // GM_CHUNK):
            rs = slice(c * GM_CHUNK, (c + 1) * GM_CHUNK)
            mixed = jnp.dot(w, v[rs, cs], preferred_element_type=F32) + bs_ref[gi]
            o_ref[rs, cs] = (u[rs, cs] * mixed).astype(o_ref.dtype)


LRU_TAIL = 8


def _lru_kernel(gb_ref, rb_ref, cw_ref, cb_ref, wa_ref, ba_ref, wx_ref, bx_ref, lam_ref, o_ref,
                tail_ref, h_ref):
    @pl.when(pl.program_id(1) == 0)
    def _():
        tail_ref[...] = jnp.zeros_like(tail_ref)
        h_ref[...] = jnp.zeros_like(h_ref)

    ts = LRU_TS
    x = rb_ref[...].astype(F32)
    ext = jnp.concatenate([tail_ref[...], x], axis=0)
    tail_ref[...] = x[ts - LRU_TAIL:, :]
    xc = cb_ref[...] + x * cw_ref[CONV_WIDTH - 1:CONV_WIDTH, :]
    for d in range(1, CONV_WIDTH):
        xs = pltpu.roll(ext, d, axis=0)[LRU_TAIL:, :]
        xc = xc + xs * cw_ref[CONV_WIDTH - 1 - d:CONV_WIDTH - d, :]
    xcb = xc.astype(BF16)
    r = jax.nn.sigmoid(jnp.dot(xcb, wa_ref[...], preferred_element_type=F32) + ba_ref[...])
    ig = jax.nn.sigmoid(jnp.dot(xcb, wx_ref[...], preferred_element_type=F32) + bx_ref[...])
    z = -lam_ref[...]
    softplus = jnp.maximum(z, 0.0) + jnp.log1p(jnp.exp(-jnp.abs(z)))
    log_a = -LRU_C * r * softplus
    a = jnp.exp(log_a)
    b = jnp.sqrt(1.0 - jnp.exp(2.0 * log_a)) * (ig * xc)
    rows = lax.broadcasted_iota(jnp.int32, (ts, 1), 0)
    d = 1
    while d < ts:
        valid = rows >= d
        a_prev = pltpu.roll(a, d, axis=0)
        b_prev = pltpu.roll(b, d, axis=0)
        b = jnp.where(valid, a * b_prev, 0.0) + b
        a = jnp.where(valid, a * a_prev, a)
        d *= 2
    h = a * h_ref[...] + b
    h_ref[...] = h[ts - 1:ts, :]
    o_ref[...] = (jax.nn.gelu(gb_ref[...].astype(F32)) * h).astype(o_ref.dtype)


POOL_TAIL = 16


def _pool_kernel(x_ref, w_ref, sc_ref, o_ref, tail_ref):
    s_id = pl.program_id(1)

    @pl.when(s_id == 0)
    def _():
        tail_ref[...] = jnp.zeros_like(tail_ref)

    ts = POOL_TS
    x = x_ref[...].astype(F32)
    ext = jnp.concatenate([tail_ref[...], x], axis=0)
    tail_ref[...] = x[ts - POOL_TAIL:, :]
    pos = s_id * ts + lax.broadcasted_iota(jnp.int32, (ts, 1), 0)
    acc = ext
    width = 1
    for gi, wdw in enumerate(POOL_WINDOWS):
        while width < wdw:
            acc = acc + pltpu.roll(acc, width, axis=0)
            width *= 2
        cs = slice(gi * POOL_GW, (gi + 1) * POOL_GW)
        cnt = jnp.minimum(pos + 1, wdw).astype(F32)
        pooled = acc[POOL_TAIL:, cs] / cnt - x[:, cs]
        mixed = jnp.dot(pooled.astype(BF16), w_ref[gi], preferred_element_type=F32)
        o_ref[:, cs] = (mixed * sc_ref[:, cs]).astype(o_ref.dtype)


def _mixers_kernel(u_ref, v_ref, gb_ref, rb_ref, xd_ref, gm_g_ref, gm_ws_ref, gm_bs_ref, cw_ref,
                   cb_ref, wa_ref, ba_ref, wx_ref, bx_ref, lam_ref, pw_ref, psc_ref,
                   ya_ref, yb_ref, yd_ref, lru_tail_ref, lru_h_ref, pool_tail_ref):
    _gmlp_kernel(u_ref, v_ref, gm_g_ref, gm_ws_ref, gm_bs_ref, ya_ref)
    _lru_kernel(gb_ref, rb_ref, cw_ref, cb_ref, wa_ref, ba_ref, wx_ref, bx_ref, lam_ref, yb_ref,
                lru_tail_ref, lru_h_ref)
    _pool_kernel(xd_ref, pw_ref, psc_ref, yd_ref, pool_tail_ref)


def _mixers(u, v, gb, rb, xd, gm_g, gm_ws, gm_bs, cw, cb, wa, ba, wx, bx, lam, pw, psc):
    assert GMLP_TM == LRU_TS == POOL_TS
    nt = SEQ // LRU_TS
    tok = pl.BlockSpec((LRU_TS, MIX), lambda b, s: (b * nt + s, 0))
    whole = lambda a: pl.BlockSpec(a.shape, lambda b, s: (0,) * a.ndim)
    small = (gm_g, gm_ws, gm_bs, cw, cb, wa, ba, wx, bx, lam, pw, psc)
    return pl.pallas_call(
        _mixers_kernel,
        grid=(BATCH, nt),
        in_specs=[tok] * 5 + [whole(a) for a in small],
        out_specs=[tok] * 3,
        out_shape=[jax.ShapeDtypeStruct((TOKENS, MIX), BF16)] * 3,
        scratch_shapes=[pltpu.VMEM((LRU_TAIL, MIX), F32), pltpu.VMEM((1, MIX), F32),
                        pltpu.VMEM((POOL_TAIL, MIX), F32)],
        compiler_params=_cparams(("arbitrary", "arbitrary")),
        name="mixers",
    )(u, v, gb, rb, xd, *small)


def _kvprep_kernel(kc_in, vc_in, vs_in, vw_in, ng_in, pe_ref, w1_ref, w2_ref, kc_ref, vct_ref,
                   vst_ref, vwt_ref, gt_ref, xf_ref):
    row = lax.broadcasted_iota(jnp.int32, (N_CHUNK, 1), 0)
    for c, x_in in enumerate((kc_in, vc_in)):
        xf_ref[...] = x_in[...].astype(F32)
        first = jnp.zeros((N_CHUNK, NSA_KV), F32)
        second = jnp.zeros((N_CHUNK, NSA_KV), F32)
        for l in range(CMP_STRIDE):
            xl = xf_ref[pl.ds(l, N_CHUNK, stride=CMP_STRIDE), :]
            first += jnp.dot((xl + pe_ref[c, l:l + 1, :]).astype(BF16), w1_ref[c, l],
                             preferred_element_type=F32)
            l2 = l + CMP_STRIDE
            second += jnp.dot((xl + pe_ref[c, l2:l2 + 1, :]).astype(BF16), w1_ref[c, l2],
                              preferred_element_type=F32)
        hid = jax.nn.gelu(first + pltpu.roll(second, N_CHUNK - 1, axis=0))
        out = jnp.dot(hid.astype(BF16), w2_ref[c], preferred_element_type=F32)
        out = jnp.where(row < N_CMP, out, 0.0)
        if c == 0:
            kc_ref[0] = out.astype(kc_ref.dtype)
        else:
            vct_ref[0] = out.T.astype(vct_ref.dtype)
    vst_ref[0] = vs_in[...].astype(F32).T.astype(vst_ref.dtype)
    vwt_ref[0] = vw_in[...].astype(F32).T.astype(vwt_ref.dtype)
    gates_t = ng_in[...].T
    gt_ref[...] = jnp.zeros_like(gt_ref)
    per_group = 3 * HPG
    for g in range(N_KV):
        gt_ref[0, g, :per_group, :] = gates_t[g * per_group:(g + 1) * per_group, :]


def _kvprep(kv, ng, pe, w1, w2):
    col = lambda j: pl.BlockSpec((SEQ, NSA_KV), lambda b: (b, j))
    whole = lambda a: pl.BlockSpec(a.shape, lambda b: (0,) * a.ndim)
    return pl.pallas_call(
        _kvprep_kernel,
        grid=(BATCH,),
        in_specs=[col(0), col(1), col(3), col(5), pl.BlockSpec((SEQ, GATE_PAD), lambda b: (b, 0)),
                  whole(pe), whole(w1), whole(w2)],
        out_specs=[pl.BlockSpec((1, N_CHUNK, NSA_KV), lambda b: (b, 0, 0)),
                   pl.BlockSpec((1, NSA_KV, N_CHUNK), lambda b: (b, 0, 0)),
                   pl.BlockSpec((1, NSA_KV, SEQ), lambda b: (b, 0, 0)),
                   pl.BlockSpec((1, NSA_KV, SEQ), lambda b: (b, 0, 0)),
                   pl.BlockSpec((1, N_KV, GATE_ROWS, SEQ), lambda b: (b, 0, 0, 0))],
        out_shape=[jax.ShapeDtypeStruct((BATCH, N_CHUNK, NSA_KV), BF16),
                   jax.ShapeDtypeStruct((BATCH, NSA_KV, N_CHUNK), BF16),
                   jax.ShapeDtypeStruct((BATCH, NSA_KV, SEQ), BF16),
                   jax.ShapeDtypeStruct((BATCH, NSA_KV, SEQ), BF16),
                   jax.ShapeDtypeStruct((BATCH, N_KV, GATE_ROWS, SEQ), F32)],
        scratch_shapes=[pltpu.VMEM((SEQ, NSA_KV), F32)],
        compiler_params=_cparams(("arbitrary",)),
        name="kvprep",
    )(kv, kv, kv, kv, ng, pe, w1, w2)


def _group_diag(w):
    eye = jnp.eye(N_KV, dtype=w.dtype)
    out = jnp.einsum('...gde,gh->...gdhe', w, eye)
    return out.reshape(w.shape[:-3] + (N_KV * w.shape[-2], N_KV * w.shape[-1]))


def _lane_tile(x, n):
    return jnp.concatenate([x] * n, axis=1)


ONES_ROWS = 16


def _with_ones(v_t):
    return jnp.concatenate([v_t, jnp.ones((ONES_ROWS, v_t.shape[1]), BF16)], axis=0)


def _nsa_kernel(q_ref, kc_ref, vct_ref, ks_ref, vst_ref, kw_ref, vwt_ref, gt_ref, ovt_ref, blk_ref,
                y_ref, sa_ref, sb_ref, oslc_ref):
    i = pl.program_id(1)
    tq = NSA_TQ
    nl = HPG * tq
    t_row = i * tq + lax.broadcasted_iota(jnp.int32, (1, tq), 1)
    groups = range(N_KV)
    vrows = [slice(g * HEAD_DIM, (g + 1) * HEAD_DIM) for g in groups]

    def scores(k, q_t):
        return jnp.dot(k, q_t, preferred_element_type=F32)

    def normalized(acc):
        return acc[:HEAD_DIM, :] / acc[HEAD_DIM:HEAD_DIM + 1, :]

    lane = lax.broadcasted_iota(jnp.int32, (tq, LANES), 1)
    scale = HEAD_DIM ** -0.5
    q4 = [jnp.concatenate(
        [(jnp.where((lane >= HEAD_DIM) == (g == 1),
                    q_ref[:, j * LANES:(j + 1) * LANES].astype(F32), 0.0) * scale).T
         for j in range(HPG)], axis=1).astype(BF16) for g in groups]

    n_idx = lax.broadcasted_iota(jnp.int32, (N_CHUNK, 1), 0)
    ok = n_idx * CMP_STRIDE + (CMP_LEN - 1) <= _lane_tile(t_row, HPG)
    blk = lax.broadcasted_iota(jnp.int32, (N_SLC, 1), 0)
    cur = jnp.right_shift(t_row, 6)
    forced = (blk == 0) | (blk == cur) | (blk == cur - 1)
    causal_blk = blk * SLC_LEN <= t_row
    any_block = _lane_tile(t_row, HPG) >= CMP_LEN - 1
    o_cmp, q_aug = [], []
    for g in groups:
        s = jnp.where(ok, scores(kc_ref[0], q4[g]), NEG_INF)
        e = jnp.exp((s - jnp.max(s, axis=0, keepdims=True)).astype(BF16))
        acc_c = jnp.dot(_with_ones(vct_ref[0, vrows[g], :]), e, preferred_element_type=F32)
        inv_l = jnp.where(any_block, 1.0 / acc_c[HEAD_DIM:HEAD_DIM + 1, :], 0.0)
        o_cmp.append(acc_c[:HEAD_DIM, :] * inv_l)
        imp = jnp.zeros((N_SLC, tq), F32)
        for j in range(HPG):
            hs = slice(j * tq, (j + 1) * tq)
            imp = imp + jnp.dot(ovt_ref[...], e[:, hs], preferred_element_type=F32) * inv_l[:, hs]
        work = jnp.where(forced, -3e38, jnp.where(causal_blk, imp, -1.0))
        sel = jnp.where(forced, 1.0, 0.0)
        for _ in range(SLC_TOPN - MAX_FORCED):
            mx = jnp.max(work, axis=0, keepdims=True)
            idx = jnp.min(jnp.where(work == mx, blk, N_SLC), axis=0, keepdims=True)
            pick = blk == idx
            sel = jnp.where(pick & (mx >= 0.0), 1.0, sel)
            work = jnp.where(pick, -3e38, work)
        sel_bias = jnp.where(sel > 0.0, 0.0, NEG_INF)
        sel_bias = jnp.concatenate([sel_bias, jnp.zeros((LANES - N_SLC, tq), F32)], axis=0)
        sel_bias = _lane_tile(sel_bias.astype(BF16), HPG)
        q_aug.append(jnp.concatenate([q4[g], sel_bias], axis=0))

    start = pl.multiple_of(jnp.maximum(i - WIN // tq, 0) * tq, tq)
    k_win = kw_ref[pl.ds(start, WIN_KEYS), :]
    delta = t_row - (start + lax.broadcasted_iota(jnp.int32, (WIN_KEYS, 1), 0))
    win_bias = _lane_tile(jnp.where((delta >= 0) & (delta < WIN), 0.0, NEG_INF), HPG)
    o_win = []
    for g in groups:
        sT = scores(k_win, q4[g]) + win_bias
        pT = jnp.exp((sT - jnp.max(sT, axis=0, keepdims=True)).astype(BF16))
        vT = _with_ones(vwt_ref[vrows[g], pl.ds(start, WIN_KEYS)])
        o_win.append(normalized(jnp.dot(vT, pT, preferred_element_type=F32)))

    def slc_scores(kt, dst_ref):
        k0 = pl.multiple_of(kt * SLC_KT, SLC_KT)
        k_aug = jnp.concatenate([ks_ref[pl.ds(k0, SLC_KT), :], blk_ref[pl.ds(k0, SLC_KT), :]],
                                axis=1)
        maxima = []
        for g in groups:
            s_t = scores(k_aug, q_aug[g])
            dst_ref[g] = s_t
            maxima.append(jnp.max(s_t, axis=0, keepdims=True))
        return maxima

    def slc_update(kt, state, s_t, tile_max):
        k0 = pl.multiple_of(kt * SLC_KT, SLC_KT)
        out = []
        for g in groups:
            m_i, acc = state[2 * g], state[2 * g + 1]
            m_new = jnp.maximum(m_i, tile_max[g])
            alpha = jnp.exp(m_i - m_new)
            pT = jnp.exp((s_t[g] - m_new).astype(BF16))
            vT = _with_ones(vst_ref[vrows[g], pl.ds(k0, SLC_KT)])
            out += [m_new, alpha * acc + jnp.dot(vT, pT, preferred_element_type=F32)]
        return out

    def from_ref(src_ref):
        return [src_ref[g] for g in groups]

    def slc_pair(p, carry):
        state, max_a = list(carry[:2 * N_KV]), list(carry[2 * N_KV:])
        max_b = slc_scores(2 * p + 1, sb_ref)
        state = slc_update(2 * p, state, from_ref(sa_ref), max_a)
        max_a = slc_scores(2 * p + 2, sa_ref)
        state = slc_update(2 * p + 1, state, from_ref(sb_ref), max_b)
        return tuple(state + max_a)

    def slc_finish(state, src_ref):
        kpos = last_kt * SLC_KT + lax.broadcasted_iota(jnp.int32, (SLC_KT, 1), 0)
        causal_bias = _lane_tile(jnp.where(kpos <= t_row, 0.0, NEG_INF), HPG)
        s_last = [s + causal_bias for s in from_ref(src_ref)]
        state = slc_update(last_kt, state, s_last,
                           [jnp.max(s, axis=0, keepdims=True) for s in s_last])
        for g in groups:
            oslc_ref[g] = normalized(state[2 * g + 1])

    last_kt = (i * tq) // SLC_KT
    init = ((jnp.full((1, nl), -3e38, F32), jnp.zeros((HEAD_DIM + ONES_ROWS, nl), F32)) * N_KV
            + tuple(slc_scores(0, sa_ref)))
    carry = lax.fori_loop(0, last_kt // 2, slc_pair, init)
    state, max_a = list(carry[:2 * N_KV]), list(carry[2 * N_KV:])
    odd = lax.rem(last_kt, 2) == 1

    @pl.when(odd)
    def _():
        slc_scores(last_kt, sb_ref)
        slc_finish(slc_update(last_kt - 1, state, from_ref(sa_ref), max_a), sb_ref)

    @pl.when(jnp.logical_not(odd))
    def _():
        slc_finish(state, sa_ref)

    o_slc = [oslc_ref[g] for g in groups]

    for g in groups:
        gates = jax.nn.sigmoid(gt_ref[0, g])

        def gate_row(br):
            return jnp.concatenate([gates[3 * j + br:3 * j + br + 1, :] for j in range(HPG)],
                                   axis=1)

        o = gate_row(0) * o_cmp[g] + gate_row(1) * o_slc[g] + gate_row(2) * o_win[g]
        for j in range(HPG):
            h0 = (g * HPG + j) * HEAD_DIM
            y_ref[0, h0:h0 + HEAD_DIM, :] = o[:, j * tq:(j + 1) * tq].astype(y_ref.dtype)


def _block_onehot():
    oh = (np.arange(SEQ)[:, None] // SLC_LEN) == np.arange(LANES)[None, :]
    return jnp.asarray(oh.astype(np.float32), dtype=BF16)


def _nsa(q, kc, vct, kv, vst, vwt, gt, ovt):
    nq = SEQ // NSA_TQ
    return pl.pallas_call(
        _nsa_kernel,
        grid=(BATCH, nq),
        in_specs=[pl.BlockSpec((NSA_TQ, NSA_Q), lambda b, i: (b * nq + i, 0)),
                  pl.BlockSpec((1, N_CHUNK, NSA_KV), lambda b, i: (b, 0, 0)),
                  pl.BlockSpec((1, NSA_KV, N_CHUNK), lambda b, i: (b, 0, 0)),
                  pl.BlockSpec((SEQ, NSA_KV), lambda b, i: (b, 2)),
                  pl.BlockSpec((None, NSA_KV, SEQ), lambda b, i: (b, 0, 0)),
                  pl.BlockSpec((SEQ, NSA_KV), lambda b, i: (b, 4)),
                  pl.BlockSpec((None, NSA_KV, SEQ), lambda b, i: (b, 0, 0)),
                  pl.BlockSpec((1, N_KV, GATE_ROWS, NSA_TQ), lambda b, i: (b, 0, 0, i)),
                  pl.BlockSpec((N_SLC, N_CHUNK), lambda b, i: (0, 0)),
                  pl.BlockSpec((SEQ, LANES), lambda b, i: (0, 0))],
        out_specs=pl.BlockSpec((1, NSA_Q, NSA_TQ), lambda b, i: (b, 0, i)),
        out_shape=jax.ShapeDtypeStruct((BATCH, NSA_Q, SEQ), BF16),
        scratch_shapes=[pltpu.VMEM((N_KV, SLC_KT, HPG * NSA_TQ), F32),
                        pltpu.VMEM((N_KV, SLC_KT, HPG * NSA_TQ), F32),
                        pltpu.VMEM((N_KV, HEAD_DIM, HPG * NSA_TQ), F32)],
        compiler_params=_cparams(("arbitrary", "arbitrary")),
        name="nsa",
    )(q, kc, vct, kv, vst, kv, vwt, gt, ovt, _block_onehot())


def _overlap_t():
    c_start = np.arange(N_CHUNK) * CMP_STRIDE
    s_start = np.arange(N_SLC) * SLC_LEN
    ov = ((c_start[None, :] <= s_start[:, None] + SLC_LEN - 1)
          & (c_start[None, :] + CMP_LEN - 1 >= s_start[:, None])
          & (np.arange(N_CHUNK)[None, :] < N_CMP))
    return jnp.asarray(ov.astype(np.float32), dtype=BF16)


def _merge_kernel(x_ref, g_ref, wmg_ref, ya_ref, yb_ref, yct_ref, yd_ref, wb_ref, wo_ref, o_ref):
    x = x_ref[...]
    nb = _rms(x, g_ref[...]).astype(BF16)
    yc = yct_ref[0].astype(F32).T.astype(BF16)
    ys = (ya_ref[...], yb_ref[...], yc, yd_ref[...])
    merged = jnp.zeros((MERGE_TM, D_MODEL), F32)
    for bi, y in enumerate(ys):
        gate = jax.nn.sigmoid(jnp.dot(nb, wmg_ref[:, bi * D_MODEL:(bi + 1) * D_MODEL],
                                      preferred_element_type=F32))
        merged = merged + gate * jnp.dot(y, wb_ref[bi], preferred_element_type=F32)
    o_ref[...] = x + jnp.dot(merged.astype(BF16), wo_ref[...], preferred_element_type=F32)


def _merge(x, g, wmg, ya, yb, yct, yd, wb, wo, layer):
    nt = SEQ // MERGE_TM
    tok = lambda i: (i, 0)
    const2 = lambda i: (0, 0)
    return pl.pallas_call(
        _merge_kernel,
        grid=(TOKENS // MERGE_TM,),
        in_specs=[pl.BlockSpec((MERGE_TM, D_MODEL), tok),
                  pl.BlockSpec((1, D_MODEL), const2),
                  pl.BlockSpec((None, D_MODEL, N_BRANCH * D_MODEL), lambda i: (layer, 0, 0)),
                  pl.BlockSpec((MERGE_TM, MIX), tok),
                  pl.BlockSpec((MERGE_TM, MIX), tok),
                  pl.BlockSpec((1, MIX, MERGE_TM), lambda i: (i // nt, 0, i % nt)),
                  pl.BlockSpec((MERGE_TM, MIX), tok),
                  pl.BlockSpec((None, N_BRANCH, MIX, D_MODEL), lambda i: (layer, 0, 0, 0)),
                  pl.BlockSpec((None, D_MODEL, D_MODEL), lambda i: (layer, 0, 0))],
        out_specs=pl.BlockSpec((MERGE_TM, D_MODEL), tok),
        out_shape=jax.ShapeDtypeStruct((TOKENS, D_MODEL), F32),
        compiler_params=_cparams(("arbitrary",)),
        name="merge",
    )(x, g, wmg, ya, yb, yct, yd, wb, wo)


def _router_logits(t, wr_ref, br_ref, discrete):
    w = wr_ref[...]
    t_hi, w_hi = t.astype(BF16), w.astype(BF16)
    dot = functools.partial(jnp.dot, preferred_element_type=F32)
    logits = dot(t_hi, w_hi) + br_ref[...]
    if discrete:
        t_lo = (t - t_hi.astype(F32)).astype(BF16)
        w_lo = (w - w_hi.astype(F32)).astype(BF16)
        logits = logits + dot(t_hi, w_lo) + dot(t_lo, w_hi)
    return logits


def _top_group(logits):
    lane = lax.broadcasted_iota(jnp.int32, logits.shape, 1)
    is_grp = lane < N_GROUPS
    lg = jnp.where(is_grp, logits, NEG_INF)
    gmax = jnp.max(lg, axis=1, keepdims=True)
    grp_idx = jnp.min(jnp.where(is_grp & (lg == gmax), lane, ROUTER_PAD), axis=1, keepdims=True)
    return lg, gmax, grp_idx


def _group_weight(logits, grp_idx):
    lane = lax.broadcasted_iota(jnp.int32, logits.shape, 1)
    lg, gmax, _ = _top_group(logits)
    ge = jnp.where(lane < N_GROUPS, jnp.exp(lg - gmax), 0.0)
    return (jnp.sum(jnp.where(lane == grp_idx, ge, 0.0), axis=1, keepdims=True)
            / jnp.sum(ge, axis=1, keepdims=True))


ROUTE_ROWS = 32


def _route_kernel(x_ref, g_ref, wr_ref, br_ref, tri_ref, meta_ref, cnt_ref, run_ref):
    @pl.when(pl.program_id(0) == 0)
    def _():
        run_ref[...] = jnp.zeros_like(run_ref)

    logits = _router_logits(_rms(x_ref[...], g_ref[...]), wr_ref, br_ref, discrete=True)
    lt = logits.T[:ROUTE_ROWS, :]
    row = lax.broadcasted_iota(jnp.int32, (ROUTE_ROWS, 1), 0)

    def top(mask, vals):
        v = jnp.where(mask, vals, NEG_INF)
        best = jnp.max(v, axis=0, keepdims=True)
        return v, jnp.min(jnp.where(mask & (v == best), row, ROUTE_ROWS), axis=0, keepdims=True)

    _, grp_idx = top(row < N_GROUPS, lt)
    first = N_GROUPS + grp_idx * EXPERTS_PER_GROUP
    in_grp = (row >= first) & (row < first + EXPERTS_PER_GROUP)
    le, i1 = top(in_grp, lt)
    _, i2 = top(in_grp & (row != i1), le)
    lo = jnp.minimum(i1, i2) - first
    hi = jnp.maximum(i1, i2) - first
    pair = jnp.right_shift(lo * (2 * EXPERTS_PER_GROUP - 1 - lo), 1) + hi - lo - 1
    cls = grp_idx * N_PAIRS + pair
    onehot = jnp.where(row == cls, 1.0, 0.0)
    before = jnp.dot(onehot.astype(BF16), tri_ref[...], preferred_element_type=F32) + run_ref[...]
    rank = jnp.sum(onehot * before, axis=0, keepdims=True).astype(jnp.int32)
    run_ref[...] += jnp.sum(onehot, axis=1, keepdims=True)
    sub = lax.broadcasted_iota(jnp.int32, (SUBLANES, 1), 0)
    meta_ref[...] = jnp.where(sub == 0, cls, jnp.where(sub == 1, rank, 0))
    cnt_ref[...] = jnp.broadcast_to(run_ref[...], cnt_ref.shape)


def _route(x, g, wr, br):
    tok = lambda i: (i, 0)
    const2 = lambda i: (0, 0)
    tri = jnp.asarray(np.triu(np.ones((ROUTE_TM, ROUTE_TM), np.float32), 1), dtype=BF16)
    return pl.pallas_call(
        _route_kernel,
        grid=(TOKENS // ROUTE_TM,),
        in_specs=[pl.BlockSpec((ROUTE_TM, D_MODEL), tok),
                  pl.BlockSpec((1, D_MODEL), const2),
                  pl.BlockSpec((D_MODEL, ROUTER_PAD), const2),
                  pl.BlockSpec((1, ROUTER_PAD), const2),
                  pl.BlockSpec((ROUTE_TM, ROUTE_TM), const2)],
        out_specs=[pl.BlockSpec((SUBLANES, ROUTE_TM), lambda i: (0, i)),
                   pl.BlockSpec((ROUTE_ROWS, LANES), const2)],
        out_shape=[jax.ShapeDtypeStruct((SUBLANES, TOKENS), jnp.int32),
                   jax.ShapeDtypeStruct((ROUTE_ROWS, LANES), F32)],
        scratch_shapes=[pltpu.VMEM((ROUTE_ROWS, 1), F32)],
        compiler_params=_cparams(("arbitrary",)),
        name="route",
    )(x, g, wr, br, tri)


MOVE_SLOTS = 3


def _move_rows_kernel(idx_ref, seg_ref, len_ref, src_hbm, dst_hbm, buf, zbuf, in_sem, out_sem,
                      pad_sem, *, scatter):
    c = pl.program_id(0)
    n = pl.num_programs(0)
    slot = lax.rem(c, MOVE_SLOTS)
    nxt = lax.rem(c + 1, MOVE_SLOTS)
    rc = ROW_CHUNK
    chunk = lambda k: pl.ds(k * rc, rc)

    if scatter:
        def read(k, sl):
            return [pltpu.make_async_copy(src_hbm.at[chunk(k)], buf.at[sl], in_sem.at[sl])]

        def write(k, sl):
            return [pltpu.make_async_copy(buf.at[sl, pl.ds(r, 1)],
                                          dst_hbm.at[pl.ds(idx_ref[k * rc + r], 1)], out_sem.at[sl])
                    for r in range(rc)]

        def write_done(sl):
            return pltpu.make_async_copy(buf.at[sl], dst_hbm.at[chunk(0)], out_sem.at[sl])

        def read_done(sl):
            return read(0, sl)[0]
    else:
        def read(k, sl):
            return [pltpu.make_async_copy(src_hbm.at[pl.ds(idx_ref[k * rc + r], 1)],
                                          buf.at[sl, pl.ds(r, 1)], in_sem.at[sl])
                    for r in range(rc)]

        def write(k, sl):
            return [pltpu.make_async_copy(buf.at[sl], dst_hbm.at[chunk(k)], out_sem.at[sl])]

        def write_done(sl):
            return write(0, sl)[0]

        def read_done(sl):
            return pltpu.make_async_copy(src_hbm.at[chunk(0)], buf.at[sl], in_sem.at[sl])

    def pad_fills(fn):
        for sg in range(N_CLASSES):
            start, length = seg_ref[sg], len_ref[sg]
            head = jnp.bitwise_and(-start, SUBLANES - 1)
            for r in range(SUBLANES - 1):
                @pl.when(r < head)
                def _():
                    fn(pltpu.make_async_copy(zbuf.at[pl.ds(0, 1)], dst_hbm.at[pl.ds(start + r, 1)],
                                             pad_sem))
            body = length - head
            bit = rc // 2
            while bit >= SUBLANES:
                done = body - jnp.bitwise_and(body, 2 * bit - 1)
                first = pl.multiple_of(start + head + done, SUBLANES)

                @pl.when(jnp.bitwise_and(body, bit) != 0)
                def _():
                    fn(pltpu.make_async_copy(zbuf.at[pl.ds(0, bit)], dst_hbm.at[pl.ds(first, bit)],
                                             pad_sem))
                bit //= 2
        tail = pl.multiple_of(seg_ref[N_CLASSES], rc)

        def tail_tile(k, carry):
            fn(pltpu.make_async_copy(zbuf, dst_hbm.at[pl.ds(tail + k * rc, rc)], pad_sem))
            return carry
        lax.fori_loop(0, len_ref[N_CLASSES] // rc, tail_tile, 0)

    @pl.when(c == 0)
    def _():
        for cp in read(0, 0):
            cp.start()
        if scatter:
            zbuf[...] = jnp.zeros_like(zbuf)
            pad_fills(lambda cp: cp.start())

    @pl.when(c + 1 < n)
    def _():
        @pl.when(c >= 2)
        def _():
            write_done(nxt).wait()
        for cp in read(c + 1, nxt):
            cp.start()

    read_done(slot).wait()
    for cp in write(c, slot):
        cp.start()

    @pl.when(c == n - 1)
    def _():
        write_done(nxt).wait()
        write_done(lax.rem(c + 2, MOVE_SLOTS)).wait()
        write_done(slot).wait()
        if scatter:
            pad_fills(lambda cp: cp.wait())


def _move_rows(idx, seg_start, seg_len, src, n_out, scatter):
    any_spec = pl.BlockSpec(memory_space=pl.ANY)
    grid_spec = pltpu.PrefetchScalarGridSpec(
        num_scalar_prefetch=3, grid=(TOKENS // ROW_CHUNK,), in_specs=[any_spec], out_specs=any_spec,
        scratch_shapes=[pltpu.VMEM((MOVE_SLOTS, ROW_CHUNK, D_MODEL), F32),
                        pltpu.VMEM((ROW_CHUNK, D_MODEL), F32),
                        pltpu.SemaphoreType.DMA((MOVE_SLOTS,)),
                        pltpu.SemaphoreType.DMA((MOVE_SLOTS,)),
                        pltpu.SemaphoreType.DMA(())])
    return pl.pallas_call(
        functools.partial(_move_rows_kernel, scatter=scatter),
        grid_spec=grid_spec,
        out_shape=jax.ShapeDtypeStruct((n_out, D_MODEL), F32),
        compiler_params=_cparams(("arbitrary",)),
        name="dispatch" if scatter else "collect",
    )(idx, seg_start, seg_len, src)


def _experts_kernel(ea_ref, eb_ref, tv_ref, tb_ref, x_ref, g_ref, wr_ref, br_ref, wga_ref, wua_ref,
                    wda_ref, wgb_ref, wub_ref, wdb_ref, fg_ref, o_ref, *, layer, final_norm):
    j = pl.program_id(0)
    nv = tv_ref[j]

    @pl.when(nv > 0)
    def _():
        x = x_ref[...]
        t = _rms(x, g_ref[...])
        logits = _router_logits(t, wr_ref, br_ref, discrete=False)
        lane = lax.broadcasted_iota(jnp.int32, logits.shape, 1)
        ea = ea_ref[j] - layer * N_EXPERTS
        eb = eb_ref[j] - layer * N_EXPERTS
        la = jnp.sum(jnp.where(lane == N_GROUPS + ea, logits, 0.0), axis=1, keepdims=True)
        lb = jnp.sum(jnp.where(lane == N_GROUPS + eb, logits, 0.0), axis=1, keepdims=True)
        top = jnp.maximum(la, lb)
        pa, pb = jnp.exp(la - top), jnp.exp(lb - top)
        scale = _group_weight(logits, jnp.right_shift(ea, 2)) / (pa + pb)
        tb = t.astype(BF16)
        acc = jnp.zeros((MOE_TM, D_MODEL), F32)
        for wg, wu, wd, w in ((wga_ref, wua_ref, wda_ref, pa * scale),
                              (wgb_ref, wub_ref, wdb_ref, pb * scale)):
            hid = (jax.nn.silu(jnp.dot(tb, wg[0], preferred_element_type=F32))
                   * jnp.dot(tb, wu[0], preferred_element_type=F32)) * w
            acc = acc + jnp.dot(hid.astype(BF16), wd[0], preferred_element_type=F32)
        h = x + acc
        if final_norm:
            h = _rms(h, fg_ref[...])
        o_ref[...] = h

    @pl.when(nv == 0)
    def _():
        o_ref[...] = jnp.zeros_like(o_ref)


def _experts(tile_ea, tile_eb, tile_valid, tile_block, hs, g, wr, br, wg, wu, wd, fg, layer,
             final_norm):
    const2 = lambda j, *_: (0, 0)
    of_a = lambda j, ea, eb, tv, tb: (ea[j], 0, 0)
    of_b = lambda j, ea, eb, tv, tb: (eb[j], 0, 0)
    up = lambda idx: pl.BlockSpec((1, D_MODEL, D_EXPERT), idx)
    down = lambda idx: pl.BlockSpec((1, D_EXPERT, D_MODEL), idx)
    grid_spec = pltpu.PrefetchScalarGridSpec(
        num_scalar_prefetch=4,
        grid=(MOE_NT,),
        in_specs=[pl.BlockSpec((MOE_TM, D_MODEL), lambda j, ea, eb, tv, tb: (tb[j], 0)),
                  pl.BlockSpec((1, D_MODEL), const2),
                  pl.BlockSpec((D_MODEL, ROUTER_PAD), const2),
                  pl.BlockSpec((1, ROUTER_PAD), const2),
                  up(of_a), up(of_a), down(of_a), up(of_b), up(of_b), down(of_b),
                  pl.BlockSpec((1, D_MODEL), const2)],
        out_specs=pl.BlockSpec((MOE_TM, D_MODEL), lambda j, *_: (j, 0)))
    return pl.pallas_call(
        functools.partial(_experts_kernel, layer=layer, final_norm=final_norm),
        grid_spec=grid_spec,
        out_shape=jax.ShapeDtypeStruct((MOE_ROWS, D_MODEL), F32),
        compiler_params=_cparams(("arbitrary",)),
        name="experts",
    )(tile_ea, tile_eb, tile_valid, tile_block, hs, g, wr, br, wg, wu, wd, wg, wu, wd, fg)


_PAIR_LO = np.array([a for a in range(EXPERTS_PER_GROUP) for b in range(a + 1, EXPERTS_PER_GROUP)])
_PAIR_HI = np.array([b for a in range(EXPERTS_PER_GROUP) for b in range(a + 1, EXPERTS_PER_GROUP)])


def _moe(x, g, wr, br, wg, wu, wd, fg, layer, final_norm):
    meta, cnt = _route(x, g, wr, br)
    cls, rank = meta[0], meta[1]
    counts = cnt[:N_CLASSES, 0].astype(jnp.int32)
    padded = (counts + MOE_TM - 1) // MOE_TM * MOE_TM
    ends = jnp.cumsum(padded)
    starts = ends - padded
    pos = starts[cls] + rank
    tile_start = jnp.arange(MOE_NT, dtype=jnp.int32) * MOE_TM
    tile_cls = jnp.minimum(jnp.sum(tile_start[:, None] >= ends[None, :], axis=1), N_CLASSES - 1)
    tile_valid = jnp.clip(starts[tile_cls] + counts[tile_cls] - tile_start, 0, MOE_TM)
    tile_valid = tile_valid.astype(jnp.int32)
    tile_block = jnp.where(tile_valid > 0, jnp.arange(MOE_NT, dtype=jnp.int32), 0)
    first = layer * N_EXPERTS + (tile_cls // N_PAIRS) * EXPERTS_PER_GROUP
    tile_ea = (first + jnp.asarray(_PAIR_LO)[tile_cls % N_PAIRS]).astype(jnp.int32)
    tile_eb = (first + jnp.asarray(_PAIR_HI)[tile_cls % N_PAIRS]).astype(jnp.int32)
    seg_start = jnp.concatenate([starts + counts, ends[-1:]])
    seg_len = jnp.concatenate([padded - counts, MOE_ROWS - ends[-1:]])
    hs = _move_rows(pos, seg_start, seg_len, x, MOE_ROWS, scatter=True)
    ys = _experts(tile_ea, tile_eb, tile_valid, tile_block, hs, g, wr, br, wg, wu, wd, fg, layer,
                  final_norm)
    return _move_rows(pos, seg_start, seg_len, ys, TOKENS, scatter=False)


def _block_diag(w):
    eye = jnp.eye(LRU_BLOCKS, dtype=w.dtype)
    return jnp.einsum('hij,hk->hikj', w, eye).reshape(MIX, MIX)


def _w_in_column_ranges():
    cuts = [int(c) for c in np.cumsum((0,) + IN_SPLITS)]
    q_parts = [(cuts[4] + (g * HPG + j) * HEAD_DIM, cuts[4] + (g * HPG + j + 1) * HEAD_DIM)
               for j in range(HPG) for g in range(N_KV)]
    return ([(cuts[8], cuts[9]), (cuts[0], cuts[4])] + q_parts
            + [(cuts[5], cuts[6]), (cuts[7], cuts[8]), (cuts[6], cuts[7])])


PREP_PIECE = 64
PREP_BLOCK = 512


def _prep_piece_starts():
    starts = []
    for a, b in _w_in_column_ranges():
        starts += list(range(a, b, PREP_PIECE))
    starts += [0] * (2 * W_IN_HALF // PREP_PIECE - len(starts))
    return np.asarray(starts, np.int32)


def _prep_w_in_kernel(src_ref, wt_hbm, o_ref, buf, sem):
    layer, j = pl.program_id(0), pl.program_id(1)
    nj = pl.num_programs(1)
    per_block = PREP_BLOCK // PREP_PIECE
    slot = lax.rem(j, 2)

    def pieces(blk, sl):
        return [pltpu.make_async_copy(
            wt_hbm.at[layer, pl.ds(pl.multiple_of(src_ref[blk * per_block + k], SUBLANES), PREP_PIECE)],
            buf.at[sl, pl.ds(k * PREP_PIECE, PREP_PIECE)], sem.at[sl]) for k in range(per_block)]

    @pl.when(j == 0)
    def _():
        for cp in pieces(0, 0):
            cp.start()

    @pl.when(j + 1 < nj)
    def _():
        for cp in pieces(j + 1, 1 - slot):
            cp.start()

    for cp in pieces(j, slot):
        cp.wait()
    o_ref[...] = buf[slot].T.astype(o_ref.dtype)


def _prep_w_in(w_in):
    grid_spec = pltpu.PrefetchScalarGridSpec(
        num_scalar_prefetch=1,
        grid=(DEPTH, 2 * W_IN_HALF // PREP_BLOCK),
        in_specs=[pl.BlockSpec(memory_space=pl.ANY)],
        out_specs=pl.BlockSpec((None, D_MODEL, PREP_BLOCK), lambda l, j, src: (l, 0, j)),
        scratch_shapes=[pltpu.VMEM((2, PREP_BLOCK, D_MODEL), F32), pltpu.SemaphoreType.DMA((2,))])
    return pl.pallas_call(
        _prep_w_in_kernel,
        grid_spec=grid_spec,
        out_shape=jax.ShapeDtypeStruct((DEPTH, D_MODEL, 2 * W_IN_HALF), BF16),
        compiler_params=_cparams(("arbitrary", "arbitrary")),
        name="prep_w_in",
    )(jnp.asarray(_prep_piece_starts()), w_in.transpose(0, 2, 1))


def _nsa_mixer(q, kv, ng, p):
    w1 = p['cmp_w1'].reshape(2, N_KV, CMP_LEN, HEAD_DIM, HEAD_DIM).transpose(0, 2, 1, 3, 4)
    pe = jnp.concatenate([p['cmp_pe']] * N_KV, axis=-1)
    kc, vct, vst, vwt, gt = _kvprep(kv, ng, pe, _group_diag(w1).astype(BF16),
                                    _group_diag(p['cmp_w2']).astype(BF16))
    return _nsa(q, kc, vct, kv, vst, vwt, gt, _overlap_t())


def _layer(h, p, big, layer, final_g, final_norm):
    row = lambda a: a.reshape(1, -1)

    u, v, gb, rb, q, kv, xd, ng = _proj(h, row(p['norm1_g']), big['w_in'], layer)

    bs = jnp.broadcast_to(p['gm_b'][:, :, None], (GM_GROUPS, GM_CHUNK, GM_GW))
    y_a, y_b, y_d = _mixers(
        u, v, gb, rb, xd, row(p['gm_norm_g']), p['gm_ws'], bs,
        p['conv_w'], row(p['conv_b']), _block_diag(p['lru_wa']).astype(BF16), row(p['lru_ba']),
        _block_diag(p['lru_wx']).astype(BF16), row(p['lru_bx']), row(p['lru_lambda']),
        p['pool_w'].astype(BF16), row(p['pool_scale']))

    y_ct = _nsa_mixer(q, kv, ng, p)

    h = _merge(h, row(p['norm1_g']), big['w_in'], y_a, y_b, y_ct, y_d, big['w_branch'],
               big['w_out'], layer)

    wr = jnp.concatenate([p['router_w_group'], p['router_w_expert']], axis=1)
    wr = jnp.pad(wr, ((0, 0), (0, ROUTER_PAD - wr.shape[1])))
    br = jnp.concatenate([p['router_b_group'], p['router_b_expert']])
    br = jnp.pad(br, (0, ROUTER_PAD - br.shape[0])).reshape(1, ROUTER_PAD)
    return _moe(h, row(p['norm2_g']), wr, br, big['moe_w_gate'], big['moe_w_up'],
                big['moe_w_down'], row(final_g), layer, final_norm)


_LAYER_PARAMS = ('norm1_g', 'gm_norm_g', 'gm_ws', 'gm_b', 'conv_w', 'conv_b', 'lru_wa',
                 'lru_ba', 'lru_wx', 'lru_bx', 'lru_lambda', 'cmp_pe', 'cmp_w1', 'cmp_w2', 'pool_w',
                 'pool_scale', 'norm2_g', 'router_w_group', 'router_b_group',
                 'router_w_expert', 'router_b_expert')


def kernel(x, norm1_g, w_in, gm_norm_g, gm_ws, gm_b, conv_w, conv_b, lru_wa, lru_ba, lru_wx,
           lru_bx, lru_lambda, cmp_pe, cmp_w1, cmp_w2, pool_w, pool_scale, w_branch, w_out,
           norm2_g, router_w_group, router_b_group, router_w_expert, router_b_expert,
           moe_w_gate, moe_w_up, moe_w_down, final_norm_g):
    stacked = dict(zip(_LAYER_PARAMS, (
        norm1_g, gm_norm_g, gm_ws, gm_b, conv_w, conv_b, lru_wa, lru_ba, lru_wx, lru_bx,
        lru_lambda, cmp_pe, cmp_w1, cmp_w2, pool_w, pool_scale, norm2_g,
        router_w_group, router_b_group, router_w_expert, router_b_expert)))
    w_all = _prep_w_in(w_in)
    experts = lambda w: w.astype(BF16).reshape((DEPTH * N_EXPERTS,) + w.shape[2:])
    big = dict(w_in=w_all, w_branch=w_branch.astype(BF16), w_out=w_out.astype(BF16),
               moe_w_gate=experts(moe_w_gate), moe_w_up=experts(moe_w_up),
               moe_w_down=experts(moe_w_down))
    h = x.reshape(TOKENS, D_MODEL)
    for layer in range(DEPTH):
        p = {k: a[layer] for k, a in stacked.items()}
        h = _layer(h, p, big, layer, final_norm_g, final_norm=(layer == DEPTH - 1))
    return h.reshape(BATCH, SEQ, D_MODEL)
```

```python
import functools

import numpy as np
import jax
import jax.numpy as jnp
from jax import lax
from jax.experimental import pallas as pl
from jax.experimental.pallas import tpu as pltpu

F32 = jnp.float32
BF16 = jnp.bfloat16

D_MODEL = 1024
BATCH = 4
SEQ = 4096
TOKENS = BATCH * SEQ
DEPTH = 2
MIX = D_MODEL // 2
GM_CHUNK = 128
GM_GROUPS = 4
GM_GW = MIX // GM_GROUPS
CONV_WIDTH = 4
LRU_BLOCKS = 8
LRU_BW = MIX // LRU_BLOCKS
LRU_C = 8.0
N_HEADS = 8
HEAD_DIM = MIX // N_HEADS
N_KV = 2
HPG = N_HEADS // N_KV
CMP_LEN = 32
CMP_STRIDE = 16
SLC_LEN = 64
SLC_TOPN = 8
MAX_FORCED = 3
WIN = 512
NSA_Q = N_HEADS * HEAD_DIM
NSA_KV = N_KV * HEAD_DIM
POOL_WINDOWS = (2, 4, 8, 16)
POOL_GW = MIX // len(POOL_WINDOWS)
N_BRANCH = 4
N_GROUPS = 4
EXPERTS_PER_GROUP = 4
N_EXPERTS = N_GROUPS * EXPERTS_PER_GROUP
D_EXPERT = D_MODEL // 2
EPS = 1e-6
NEG_INF = -1e30
IN_SPLITS = (MIX, MIX, MIX, MIX, NSA_Q, 6 * NSA_KV, 3 * N_HEADS, MIX, N_BRANCH * D_MODEL)

N_CHUNK = SEQ // CMP_STRIDE
N_CMP = N_CHUNK - CMP_LEN // CMP_STRIDE + 1
N_SLC = SEQ // SLC_LEN

LANES = 128
SUBLANES = 8
GATE_PAD = LANES
GATE_ROWS = 16
ROUTER_PAD = LANES
VMEM_LIMIT = 56 * 1024 * 1024

PROJ_WIDTHS = (MIX, MIX, MIX, MIX, NSA_Q, 6 * NSA_KV, MIX, GATE_PAD)
W_IN_HALF = N_BRANCH * D_MODEL
PROJ_TM = 1024
GMLP_TM = 512
LRU_TS = 512
POOL_TS = 512
NSA_TQ = 256
SLC_KT = 256
WIN_KEYS = WIN + NSA_TQ
MERGE_TM = 1024
ROUTE_TM = 512
MOE_TM = 256
N_PAIRS = EXPERTS_PER_GROUP * (EXPERTS_PER_GROUP - 1) // 2
N_CLASSES = N_GROUPS * N_PAIRS
MOE_ROWS = TOKENS + N_CLASSES * MOE_TM
MOE_NT = MOE_ROWS // MOE_TM
ROW_CHUNK = 256


def _cparams(sem):
    return pltpu.CompilerParams(dimension_semantics=sem, vmem_limit_bytes=VMEM_LIMIT)


def _rms(x, g):
    return x * lax.rsqrt(jnp.mean(x * x, axis=-1, keepdims=True) + EPS) * g


def _proj_kernel(x_ref, g_ref, w_ref, *out_refs):
    nb = _rms(x_ref[...], g_ref[...]).astype(BF16)
    off = 0
    for ref in out_refs:
        w = ref.shape[-1]
        ref[...] = jnp.dot(nb, w_ref[:, off:off + w], preferred_element_type=F32).astype(ref.dtype)
        off += w


def _proj(x, g, w, layer):
    out_shape = [jax.ShapeDtypeStruct((TOKENS, wd), BF16) for wd in PROJ_WIDTHS[:-1]]
    out_shape.append(jax.ShapeDtypeStruct((TOKENS, GATE_PAD), F32))
    return pl.pallas_call(
        _proj_kernel,
        grid=(TOKENS // PROJ_TM,),
        in_specs=[pl.BlockSpec((PROJ_TM, D_MODEL), lambda i: (i, 0)),
                  pl.BlockSpec((1, D_MODEL), lambda i: (0, 0)),
                  pl.BlockSpec((None, D_MODEL, W_IN_HALF), lambda i: (layer, 0, 1))],
        out_specs=[pl.BlockSpec((PROJ_TM, wd), lambda i: (i, 0)) for wd in PROJ_WIDTHS],
        out_shape=out_shape,
        compiler_params=_cparams(("arbitrary",)),
        name="proj",
    )(x, g, w)


def _gmlp_kernel(u_ref, v_ref, g_ref, ws_ref, bs_ref, o_ref):
    u = jax.nn.gelu(u_ref[...].astype(F32))
    v = _rms(jax.nn.gelu(v_ref[...].astype(F32)), g_ref[...]).astype(BF16)
    row = lax.broadcasted_iota(jnp.int32, (GM_CHUNK, GM_CHUNK), 0)
    col = lax.broadcasted_iota(jnp.int32, (GM_CHUNK, GM_CHUNK), 1)
    causal = row >= col
    for gi in range(GM_GROUPS):
        w = jnp.where(causal, ws_ref[gi], 0.0).astype(BF16)
        cs = slice(gi * GM_GW, (gi + 1) * GM_GW)
        for c in range(GMLP_TM // GM_CHUNK):
            rs = slice(c * GM_CHUNK, (c + 1) * GM_CHUNK)
            mixed = jnp.dot(w, v[rs, cs], preferred_element_type=F32) + bs_ref[gi]
            o_ref[rs, cs] = (u[rs, cs] * mixed).astype(o_ref.dtype)


LRU_TAIL = 8


def _lru_kernel(gb_ref, rb_ref, cw_ref, cb_ref, wa_ref, ba_ref, wx_ref, bx_ref, lam_ref, o_ref,
                tail_ref, h_ref):
    @pl.when(pl.program_id(1) == 0)
    def _():
        tail_ref[...] = jnp.zeros_like(tail_ref)
        h_ref[...] = jnp.zeros_like(h_ref)

    ts = LRU_TS
    x = rb_ref[...].astype(F32)
    ext = jnp.concatenate([tail_ref[...], x], axis=0)
    tail_ref[...] = x[ts - LRU_TAIL:, :]
    xc = cb_ref[...] + x * cw_ref[CONV_WIDTH - 1:CONV_WIDTH, :]
    for d in range(1, CONV_WIDTH):
        xs = pltpu.roll(ext, d, axis=0)[LRU_TAIL:, :]
        xc = xc + xs * cw_ref[CONV_WIDTH - 1 - d:CONV_WIDTH - d, :]
    xcb = xc.astype(BF16)
    r = jax.nn.sigmoid(jnp.dot(xcb, wa_ref[...], preferred_element_type=F32) + ba_ref[...])
    ig = jax.nn.sigmoid(jnp.dot(xcb, wx_ref[...], preferred_element_type=F32) + bx_ref[...])
    z = -lam_ref[...]
    softplus = jnp.maximum(z, 0.0) + jnp.log1p(jnp.exp(-jnp.abs(z)))
    log_a = -LRU_C * r * softplus
    a = jnp.exp(log_a)
    b = jnp.sqrt(1.0 - jnp.exp(2.0 * log_a)) * (ig * xc)
    rows = lax.broadcasted_iota(jnp.int32, (ts, 1), 0)
    d = 1
    while d < ts:
        valid = rows >= d
        a_prev = pltpu.roll(a, d, axis=0)
        b_prev = pltpu.roll(b, d, axis=0)
        b = jnp.where(valid, a * b_prev, 0.0) + b
        a = jnp.where(valid, a * a_prev, a)
        d *= 2
    h = a * h_ref[...] + b
    h_ref[...] = h[ts - 1:ts, :]
    o_ref[...] = (jax.nn.gelu(gb_ref[...].astype(F32)) * h).astype(o_ref.dtype)


POOL_TAIL = 16


def _pool_kernel(x_ref, w_ref, sc_ref, o_ref, tail_ref):
    s_id = pl.program_id(1)

    @pl.when(s_id == 0)
    def _():
        tail_ref[...] = jnp.zeros_like(tail_ref)

    ts = POOL_TS
    x = x_ref[...].astype(F32)
    ext = jnp.concatenate([tail_ref[...], x], axis=0)
    tail_ref[...] = x[ts - POOL_TAIL:, :]
    pos = s_id * ts + lax.broadcasted_iota(jnp.int32, (ts, 1), 0)
    acc = ext
    width = 1
    for gi, wdw in enumerate(POOL_WINDOWS):
        while width < wdw:
            acc = acc + pltpu.roll(acc, width, axis=0)
            width *= 2
        cs = slice(gi * POOL_GW, (gi + 1) * POOL_GW)
        cnt = jnp.minimum(pos + 1, wdw).astype(F32)
        pooled = acc[POOL_TAIL:, cs] / cnt - x[:, cs]
        mixed = jnp.dot(pooled.astype(BF16), w_ref[gi], preferred_element_type=F32)
        o_ref[:, cs] = (mixed * sc_ref[:, cs]).astype(o_ref.dtype)


def _mixers_kernel(u_ref, v_ref, gb_ref, rb_ref, xd_ref, gm_g_ref, gm_ws_ref, gm_bs_ref, cw_ref,
                   cb_ref, wa_ref, ba_ref, wx_ref, bx_ref, lam_ref, pw_ref, psc_ref,
                   ya_ref, yb_ref, yd_ref, lru_tail_ref, lru_h_ref, pool_tail_ref):
    _gmlp_kernel(u_ref, v_ref, gm_g_ref, gm_ws_ref, gm_bs_ref, ya_ref)
    _lru_kernel(gb_ref, rb_ref, cw_ref, cb_ref, wa_ref, ba_ref, wx_ref, bx_ref, lam_ref, yb_ref,
                lru_tail_ref, lru_h_ref)
    _pool_kernel(xd_ref, pw_ref, psc_ref, yd_ref, pool_tail_ref)


def _mixers(u, v, gb, rb, xd, gm_g, gm_ws, gm_bs, cw, cb, wa, ba, wx, bx, lam, pw, psc):
    assert GMLP_TM == LRU_TS == POOL_TS
    nt = SEQ // LRU_TS
    tok = pl.BlockSpec((LRU_TS, MIX), lambda b, s: (b * nt + s, 0))
    whole = lambda a: pl.BlockSpec(a.shape, lambda b, s: (0,) * a.ndim)
    small = (gm_g, gm_ws, gm_bs, cw, cb, wa, ba, wx, bx, lam, pw, psc)
    return pl.pallas_call(
        _mixers_kernel,
        grid=(BATCH, nt),
        in_specs=[tok] * 5 + [whole(a) for a in small],
        out_specs=[tok] * 3,
        out_shape=[jax.ShapeDtypeStruct((TOKENS, MIX), BF16)] * 3,
        scratch_shapes=[pltpu.VMEM((LRU_TAIL, MIX), F32), pltpu.VMEM((1, MIX), F32),
                        pltpu.VMEM((POOL_TAIL, MIX), F32)],
        compiler_params=_cparams(("arbitrary", "arbitrary")),
        name="mixers",
    )(u, v, gb, rb, xd, *small)


def _kvprep_kernel(kc_in, vc_in, vs_in, vw_in, ng_in, pe_ref, w1_ref, w2_ref, kc_ref, vct_ref,
                   vst_ref, vwt_ref, gt_ref, xf_ref):
    row = lax.broadcasted_iota(jnp.int32, (N_CHUNK, 1), 0)
    for c, x_in in enumerate((kc_in, vc_in)):
        xf_ref[...] = x_in[...].astype(F32)
        first = jnp.zeros((N_CHUNK, NSA_KV), F32)
        second = jnp.zeros((N_CHUNK, NSA_KV), F32)
        for l in range(CMP_STRIDE):
            xl = xf_ref[pl.ds(l, N_CHUNK, stride=CMP_STRIDE), :]
            first += jnp.dot((xl + pe_ref[c, l:l + 1, :]).astype(BF16), w1_ref[c, l],
                             preferred_element_type=F32)
            l2 = l + CMP_STRIDE
            second += jnp.dot((xl + pe_ref[c, l2:l2 + 1, :]).astype(BF16), w1_ref[c, l2],
                              preferred_element_type=F32)
        hid = jax.nn.gelu(first + pltpu.roll(second, N_CHUNK - 1, axis=0))
        out = jnp.dot(hid.astype(BF16), w2_ref[c], preferred_element_type=F32)
        out = jnp.where(row < N_CMP, out, 0.0)
        if c == 0:
            kc_ref[0] = out.astype(kc_ref.dtype)
        else:
            vct_ref[0] = out.T.astype(vct_ref.dtype)
    vst_ref[0] = vs_in[...].astype(F32).T.astype(vst_ref.dtype)
    vwt_ref[0] = vw_in[...].astype(F32).T.astype(vwt_ref.dtype)
    gates_t = ng_in[...].T
    gt_ref[...] = jnp.zeros_like(gt_ref)
    per_group = 3 * HPG
    for g in range(N_KV):
        gt_ref[0, g, :per_group, :] = gates_t[g * per_group:(g + 1) * per_group, :]


def _kvprep(kv, ng, pe, w1, w2):
    col = lambda j: pl.BlockSpec((SEQ, NSA_KV), lambda b: (b, j))
    whole = lambda a: pl.BlockSpec(a.shape, lambda b: (0,) * a.ndim)
    return pl.pallas_call(
        _kvprep_kernel,
        grid=(BATCH,),
        in_specs=[col(0), col(1), col(3), col(5), pl.BlockSpec((SEQ, GATE_PAD), lambda b: (b, 0)),
                  whole(pe), whole(w1), whole(w2)],
        out_specs=[pl.BlockSpec((1, N_CHUNK, NSA_KV), lambda b: (b, 0, 0)),
                   pl.BlockSpec((1, NSA_KV, N_CHUNK), lambda b: (b, 0, 0)),
                   pl.BlockSpec((1, NSA_KV, SEQ), lambda b: (b, 0, 0)),
                   pl.BlockSpec((1, NSA_KV, SEQ), lambda b: (b, 0, 0)),
                   pl.BlockSpec((1, N_KV, GATE_ROWS, SEQ), lambda b: (b, 0, 0, 0))],
        out_shape=[jax.ShapeDtypeStruct((BATCH, N_CHUNK, NSA_KV), BF16),
                   jax.ShapeDtypeStruct((BATCH, NSA_KV, N_CHUNK), BF16),
                   jax.ShapeDtypeStruct((BATCH, NSA_KV, SEQ), BF16),
                   jax.ShapeDtypeStruct((BATCH, NSA_KV, SEQ), BF16),
                   jax.ShapeDtypeStruct((BATCH, N_KV, GATE_ROWS, SEQ), F32)],
        scratch_shapes=[pltpu.VMEM((SEQ, NSA_KV), F32)],
        compiler_params=_cparams(("arbitrary",)),
        name="kvprep",
    )(kv, kv, kv, kv, ng, pe, w1, w2)


def _group_diag(w):
    eye = jnp.eye(N_KV, dtype=w.dtype)
    out = jnp.einsum('...gde,gh->...gdhe', w, eye)
    return out.reshape(w.shape[:-3] + (N_KV * w.shape[-2], N_KV * w.shape[-1]))


def _lane_tile(x, n):
    return jnp.concatenate([x] * n, axis=1)


ONES_ROWS = 16


def _with_ones(v_t):
    return jnp.concatenate([v_t, jnp.ones((ONES_ROWS, v_t.shape[1]), BF16)], axis=0)


def _nsa_kernel(q_ref, kc_ref, vct_ref, ks_ref, vst_ref, kw_ref, vwt_ref, gt_ref, ovt_ref, blk_ref,
                y_ref, sa_ref, sb_ref, oslc_ref):
    i = pl.program_id(1)
    tq = NSA_TQ
    nl = HPG * tq
    t_row = i * tq + lax.broadcasted_iota(jnp.int32, (1, tq), 1)
    groups = range(N_KV)
    vrows = [slice(g * HEAD_DIM, (g + 1) * HEAD_DIM) for g in groups]

    def scores(k, q_t):
        return jnp.dot(k, q_t, preferred_element_type=F32)

    def normalized(acc):
        return acc[:HEAD_DIM, :] / acc[HEAD_DIM:HEAD_DIM + 1, :]

    lane = lax.broadcasted_iota(jnp.int32, (tq, LANES), 1)
    scale = HEAD_DIM ** -0.5
    q4 = [jnp.concatenate(
        [(jnp.where((lane >= HEAD_DIM) == (g == 1),
                    q_ref[:, j * LANES:(j + 1) * LANES].astype(F32), 0.0) * scale).T
         for j in range(HPG)], axis=1).astype(BF16) for g in groups]

    n_idx = lax.broadcasted_iota(jnp.int32, (N_CHUNK, 1), 0)
    ok = n_idx * CMP_STRIDE + (CMP_LEN - 1) <= _lane_tile(t_row, HPG)
    blk = lax.broadcasted_iota(jnp.int32, (N_SLC, 1), 0)
    cur = jnp.right_shift(t_row, 6)
    forced = (blk == 0) | (blk == cur) | (blk == cur - 1)
    causal_blk = blk * SLC_LEN <= t_row
    any_block = _lane_tile(t_row, HPG) >= CMP_LEN - 1
    o_cmp, q_aug = [], []
    for g in groups:
        s = jnp.where(ok, scores(kc_ref[0], q4[g]), NEG_INF)
        e = jnp.exp((s - jnp.max(s, axis=0, keepdims=True)).astype(BF16))
        acc_c = jnp.dot(_with_ones(vct_ref[0, vrows[g], :]), e, preferred_element_type=F32)
        inv_l = jnp.where(any_block, 1.0 / acc_c[HEAD_DIM:HEAD_DIM + 1, :], 0.0)
        o_cmp.append(acc_c[:HEAD_DIM, :] * inv_l)
        imp = jnp.zeros((N_SLC, tq), F32)
        for j in range(HPG):
            hs = slice(j * tq, (j + 1) * tq)
            imp = imp + jnp.dot(ovt_ref[...], e[:, hs], preferred_element_type=F32) * inv_l[:, hs]
        work = jnp.where(forced, -3e38, jnp.where(causal_blk, imp, -1.0))
        sel = jnp.where(forced, 1.0, 0.0)
        for _ in range(SLC_TOPN - MAX_FORCED):
            mx = jnp.max(work, axis=0, keepdims=True)
            idx = jnp.min(jnp.where(work == mx, blk, N_SLC), axis=0, keepdims=True)
            pick = blk == idx
            sel = jnp.where(pick & (mx >= 0.0), 1.0, sel)
            work = jnp.where(pick, -3e38, work)
        sel_bias = jnp.where(sel > 0.0, 0.0, NEG_INF)
        sel_bias = jnp.concatenate([sel_bias, jnp.zeros((LANES - N_SLC, tq), F32)], axis=0)
        sel_bias = _lane_tile(sel_bias.astype(BF16), HPG)
        q_aug.append(jnp.concatenate([q4[g], sel_bias], axis=0))

    start = pl.multiple_of(jnp.maximum(i - WIN // tq, 0) * tq, tq)
    k_win = kw_ref[pl.ds(start, WIN_KEYS), :]
    delta = t_row - (start + lax.broadcasted_iota(jnp.int32, (WIN_KEYS, 1), 0))
    win_bias = _lane_tile(jnp.where((delta >= 0) & (delta < WIN), 0.0, NEG_INF), HPG)
    o_win = []
    for g in groups:
        sT = scores(k_win, q4[g]) + win_bias
        pT = jnp.exp((sT - jnp.max(sT, axis=0, keepdims=True)).astype(BF16))
        vT = _with_ones(vwt_ref[vrows[g], pl.ds(start, WIN_KEYS)])
        o_win.append(normalized(jnp.dot(vT, pT, preferred_element_type=F32)))

    def slc_scores(kt, dst_ref):
        k0 = pl.multiple_of(kt * SLC_KT, SLC_KT)
        k_aug = jnp.concatenate([ks_ref[pl.ds(k0, SLC_KT), :], blk_ref[pl.ds(k0, SLC_KT), :]],
                                axis=1)
        maxima = []
        for g in groups:
            s_t = scores(k_aug, q_aug[g])
            dst_ref[g] = s_t
            maxima.append(jnp.max(s_t, axis=0, keepdims=True))
        return maxima

    def slc_update(kt, state, s_t, tile_max):
        k0 = pl.multiple_of(kt * SLC_KT, SLC_KT)
        out = []
        for g in groups:
            m_i, acc = state[2 * g], state[2 * g + 1]
            m_new = jnp.maximum(m_i, tile_max[g])
            alpha = jnp.exp(m_i - m_new)
            pT = jnp.exp((s_t[g] - m_new).astype(BF16))
            vT = _with_ones(vst_ref[vrows[g], pl.ds(k0, SLC_KT)])
            out += [m_new, alpha * acc + jnp.dot(vT, pT, preferred_element_type=F32)]
        return out

    def from_ref(src_ref):
        return [src_ref[g] for g in groups]

    def slc_pair(p, carry):
        state, max_a = list(carry[:2 * N_KV]), list(carry[2 * N_KV:])
        max_b = slc_scores(2 * p + 1, sb_ref)
        state = slc_update(2 * p, state, from_ref(sa_ref), max_a)
        max_a = slc_scores(2 * p + 2, sa_ref)
        state = slc_update(2 * p + 1, state, from_ref(sb_ref), max_b)
        return tuple(state + max_a)

    def slc_finish(state, src_ref):
        kpos = last_kt * SLC_KT + lax.broadcasted_iota(jnp.int32, (SLC_KT, 1), 0)
        causal_bias = _lane_tile(jnp.where(kpos <= t_row, 0.0, NEG_INF), HPG)
        s_last = [s + causal_bias for s in from_ref(src_ref)]
        state = slc_update(last_kt, state, s_last,
                           [jnp.max(s, axis=0, keepdims=True) for s in s_last])
        for g in groups:
            oslc_ref[g] = normalized(state[2 * g + 1])

    last_kt = (i * tq) // SLC_KT
    init = ((jnp.full((1, nl), -3e38, F32), jnp.zeros((HEAD_DIM + ONES_ROWS, nl), F32)) * N_KV
            + tuple(slc_scores(0, sa_ref)))
    carry = lax.fori_loop(0, last_kt // 2, slc_pair, init)
    state, max_a = list(carry[:2 * N_KV]), list(carry[2 * N_KV:])
    odd = lax.rem(last_kt, 2) == 1

    @pl.when(odd)
    def _():
        slc_scores(last_kt, sb_ref)
        slc_finish(slc_update(last_kt - 1, state, from_ref(sa_ref), max_a), sb_ref)

    @pl.when(jnp.logical_not(odd))
    def _():
        slc_finish(state, sa_ref)

    o_slc = [oslc_ref[g] for g in groups]

    for g in groups:
        gates = jax.nn.sigmoid(gt_ref[0, g])

        def gate_row(br):
            return jnp.concatenate([gates[3 * j + br:3 * j + br + 1, :] for j in range(HPG)],
                                   axis=1)

        o = gate_row(0) * o_cmp[g] + gate_row(1) * o_slc[g] + gate_row(2) * o_win[g]
        for j in range(HPG):
            h0 = (g * HPG + j) * HEAD_DIM
            y_ref[0, h0:h0 + HEAD_DIM, :] = o[:, j * tq:(j + 1) * tq].astype(y_ref.dtype)


def _block_onehot():
    oh = (np.arange(SEQ)[:, None] // SLC_LEN) == np.arange(LANES)[None, :]
    return jnp.asarray(oh.astype(np.float32), dtype=BF16)


def _nsa(q, kc, vct, kv, vst, vwt, gt, ovt):
    nq = SEQ // NSA_TQ
    return pl.pallas_call(
        _nsa_kernel,
        grid=(BATCH, nq),
        in_specs=[pl.BlockSpec((NSA_TQ, NSA_Q), lambda b, i: (b * nq + i, 0)),
                  pl.BlockSpec((1, N_CHUNK, NSA_KV), lambda b, i: (b, 0, 0)),
                  pl.BlockSpec((1, NSA_KV, N_CHUNK), lambda b, i: (b, 0, 0)),
                  pl.BlockSpec((SEQ, NSA_KV), lambda b, i: (b, 2)),
                  pl.BlockSpec((None, NSA_KV, SEQ), lambda b, i: (b, 0, 0)),
                  pl.BlockSpec((SEQ, NSA_KV), lambda b, i: (b, 4)),
                  pl.BlockSpec((None, NSA_KV, SEQ), lambda b, i: (b, 0, 0)),
                  pl.BlockSpec((1, N_KV, GATE_ROWS, NSA_TQ), lambda b, i: (b, 0, 0, i)),
                  pl.BlockSpec((N_SLC, N_CHUNK), lambda b, i: (0, 0)),
                  pl.BlockSpec((SEQ, LANES), lambda b, i: (0, 0))],
        out_specs=pl.BlockSpec((1, NSA_Q, NSA_TQ), lambda b, i: (b, 0, i)),
        out_shape=jax.ShapeDtypeStruct((BATCH, NSA_Q, SEQ), BF16),
        scratch_shapes=[pltpu.VMEM((N_KV, SLC_KT, HPG * NSA_TQ), F32),
                        pltpu.VMEM((N_KV, SLC_KT, HPG * NSA_TQ), F32),
                        pltpu.VMEM((N_KV, HEAD_DIM, HPG * NSA_TQ), F32)],
        compiler_params=_cparams(("arbitrary", "arbitrary")),
        name="nsa",
    )(q, kc, vct, kv, vst, kv, vwt, gt, ovt, _block_onehot())


def _overlap_t():
    c_start = np.arange(N_CHUNK) * CMP_STRIDE
    s_start = np.arange(N_SLC) * SLC_LEN
    ov = ((c_start[None, :] <= s_start[:, None] + SLC_LEN - 1)
          & (c_start[None, :] + CMP_LEN - 1 >= s_start[:, None])
          & (np.arange(N_CHUNK)[None, :] < N_CMP))
    return jnp.asarray(ov.astype(np.float32), dtype=BF16)


def _merge_kernel(x_ref, g_ref, wmg_ref, ya_ref, yb_ref, yct_ref, yd_ref, wb_ref, wo_ref, o_ref):
    x = x_ref[...]
    nb = _rms(x, g_ref[...]).astype(BF16)
    yc = yct_ref[0].astype(F32).T.astype(BF16)
    ys = (ya_ref[...], yb_ref[...], yc, yd_ref[...])
    merged = jnp.zeros((MERGE_TM, D_MODEL), F32)
    for bi, y in enumerate(ys):
        gate = jax.nn.sigmoid(jnp.dot(nb, wmg_ref[:, bi * D_MODEL:(bi + 1) * D_MODEL],
                                      preferred_element_type=F32))
        merged = merged + gate * jnp.dot(y, wb_ref[bi], preferred_element_type=F32)
    o_ref[...] = x + jnp.dot(merged.astype(BF16), wo_ref[...], preferred_element_type=F32)


def _merge(x, g, wmg, ya, yb, yct, yd, wb, wo, layer):
    nt = SEQ // MERGE_TM
    tok = lambda i: (i, 0)
    const2 = lambda i: (0, 0)
    return pl.pallas_call(
        _merge_kernel,
        grid=(TOKENS // MERGE_TM,),
        in_specs=[pl.BlockSpec((MERGE_TM, D_MODEL), tok),
                  pl.BlockSpec((1, D_MODEL), const2),
                  pl.BlockSpec((None, D_MODEL, N_BRANCH * D_MODEL), lambda i: (layer, 0, 0),
                               pipeline_mode=pl.Buffered(1)),
                  pl.BlockSpec((MERGE_TM, MIX), tok),
                  pl.BlockSpec((MERGE_TM, MIX), tok),
                  pl.BlockSpec((1, MIX, MERGE_TM), lambda i: (i // nt, 0, i % nt)),
                  pl.BlockSpec((MERGE_TM, MIX), tok),
                  pl.BlockSpec((None, N_BRANCH, MIX, D_MODEL), lambda i: (layer, 0, 0, 0),
                               pipeline_mode=pl.Buffered(1)),
                  pl.BlockSpec((None, D_MODEL, D_MODEL), lambda i: (layer, 0, 0),
                               pipeline_mode=pl.Buffered(1))],
        out_specs=pl.BlockSpec((MERGE_TM, D_MODEL), tok),
        out_shape=jax.ShapeDtypeStruct((TOKENS, D_MODEL), F32),
        compiler_params=_cparams(("arbitrary",)),
        name="merge",
    )(x, g, wmg, ya, yb, yct, yd, wb, wo)


def _router_logits(t, wr_ref, br_ref, discrete):
    w = wr_ref[...]
    t_hi, w_hi = t.astype(BF16), w.astype(BF16)
    dot = functools.partial(jnp.dot, preferred_element_type=F32)
    logits = dot(t_hi, w_hi) + br_ref[...]
    if discrete:
        t_lo = (t - t_hi.astype(F32)).astype(BF16)
        w_lo = (w - w_hi.astype(F32)).astype(BF16)
        logits = logits + dot(t_hi, w_lo) + dot(t_lo, w_hi)
    return logits


def _top_group(logits):
    lane = lax.broadcasted_iota(jnp.int32, logits.shape, 1)
    is_grp = lane < N_GROUPS
    lg = jnp.where(is_grp, logits, NEG_INF)
    gmax = jnp.max(lg, axis=1, keepdims=True)
    grp_idx = jnp.min(jnp.where(is_grp & (lg == gmax), lane, ROUTER_PAD), axis=1, keepdims=True)
    return lg, gmax, grp_idx


def _group_weight(logits, grp_idx):
    lane = lax.broadcasted_iota(jnp.int32, logits.shape, 1)
    lg, gmax, _ = _top_group(logits)
    ge = jnp.where(lane < N_GROUPS, jnp.exp(lg - gmax), 0.0)
    return (jnp.sum(jnp.where(lane == grp_idx, ge, 0.0), axis=1, keepdims=True)
            / jnp.sum(ge, axis=1, keepdims=True))


ROUTE_ROWS = 32


def _route_kernel(x_ref, g_ref, wr_ref, br_ref, tri_ref, meta_ref, cnt_ref, run_ref):
    @pl.when(pl.program_id(0) == 0)
    def _():
        run_ref[...] = jnp.zeros_like(run_ref)

    logits = _router_logits(_rms(x_ref[...], g_ref[...]), wr_ref, br_ref, discrete=True)
    lt = logits.T[:ROUTE_ROWS, :]
    row = lax.broadcasted_iota(jnp.int32, (ROUTE_ROWS, 1), 0)

    def top(mask, vals):
        v = jnp.where(mask, vals, NEG_INF)
        best = jnp.max(v, axis=0, keepdims=True)
        return v, jnp.min(jnp.where(mask & (v == best), row, ROUTE_ROWS), axis=0, keepdims=True)

    _, grp_idx = top(row < N_GROUPS, lt)
    first = N_GROUPS + grp_idx * EXPERTS_PER_GROUP
    in_grp = (row >= first) & (row < first + EXPERTS_PER_GROUP)
    le, i1 = top(in_grp, lt)
    _, i2 = top(in_grp & (row != i1), le)
    lo = jnp.minimum(i1, i2) - first
    hi = jnp.maximum(i1, i2) - first
    pair = jnp.right_shift(lo * (2 * EXPERTS_PER_GROUP - 1 - lo), 1) + hi - lo - 1
    cls = grp_idx * N_PAIRS + pair
    onehot = jnp.where(row == cls, 1.0, 0.0)
    before = jnp.dot(onehot.astype(BF16), tri_ref[...], preferred_element_type=F32) + run_ref[...]
    rank = jnp.sum(onehot * before, axis=0, keepdims=True).astype(jnp.int32)
    run_ref[...] += jnp.sum(onehot, axis=1, keepdims=True)
    sub = lax.broadcasted_iota(jnp.int32, (SUBLANES, 1), 0)
    meta_ref[...] = jnp.where(sub == 0, cls, jnp.where(sub == 1, rank, 0))
    cnt_ref[...] = jnp.broadcast_to(run_ref[...], cnt_ref.shape)


def _route(x, g, wr, br):
    tok = lambda i: (i, 0)
    const2 = lambda i: (0, 0)
    tri = jnp.asarray(np.triu(np.ones((ROUTE_TM, ROUTE_TM), np.float32), 1), dtype=BF16)
    return pl.pallas_call(
        _route_kernel,
        grid=(TOKENS // ROUTE_TM,),
        in_specs=[pl.BlockSpec((ROUTE_TM, D_MODEL), tok),
                  pl.BlockSpec((1, D_MODEL), const2),
                  pl.BlockSpec((D_MODEL, ROUTER_PAD), const2),
                  pl.BlockSpec((1, ROUTER_PAD), const2),
                  pl.BlockSpec((ROUTE_TM, ROUTE_TM), const2)],
        out_specs=[pl.BlockSpec((SUBLANES, ROUTE_TM), lambda i: (0, i)),
                   pl.BlockSpec((ROUTE_ROWS, LANES), const2)],
        out_shape=[jax.ShapeDtypeStruct((SUBLANES, TOKENS), jnp.int32),
                   jax.ShapeDtypeStruct((ROUTE_ROWS, LANES), F32)],
        scratch_shapes=[pltpu.VMEM((ROUTE_ROWS, 1), F32)],
        compiler_params=_cparams(("arbitrary",)),
        name="route",
    )(x, g, wr, br, tri)


MOVE_SLOTS = 3


def _move_rows_kernel(idx_ref, seg_ref, len_ref, src_hbm, dst_hbm, buf, zbuf, in_sem, out_sem,
                      pad_sem, *, scatter):
    c = pl.program_id(0)
    n = pl.num_programs(0)
    slot = lax.rem(c, MOVE_SLOTS)
    nxt = lax.rem(c + 1, MOVE_SLOTS)
    rc = ROW_CHUNK
    chunk = lambda k: pl.ds(k * rc, rc)

    if scatter:
        def read(k, sl):
            return [pltpu.make_async_copy(src_hbm.at[chunk(k)], buf.at[sl], in_sem.at[sl])]

        def write(k, sl):
            return [pltpu.make_async_copy(buf.at[sl, pl.ds(r, 1)],
                                          dst_hbm.at[pl.ds(idx_ref[k * rc + r], 1)], out_sem.at[sl])
                    for r in range(rc)]

        def write_done(sl):
            return pltpu.make_async_copy(buf.at[sl], dst_hbm.at[chunk(0)], out_sem.at[sl])

        def read_done(sl):
            return read(0, sl)[0]
    else:
        def read(k, sl):
            return [pltpu.make_async_copy(src_hbm.at[pl.ds(idx_ref[k * rc + r], 1)],
                                          buf.at[sl, pl.ds(r, 1)], in_sem.at[sl])
                    for r in range(rc)]

        def write(k, sl):
            return [pltpu.make_async_copy(buf.at[sl], dst_hbm.at[chunk(k)], out_sem.at[sl])]

        def write_done(sl):
            return write(0, sl)[0]

        def read_done(sl):
            return pltpu.make_async_copy(src_hbm.at[chunk(0)], buf.at[sl], in_sem.at[sl])

    def pad_fills(fn):
        for sg in range(N_CLASSES):
            start, length = seg_ref[sg], len_ref[sg]
            head = jnp.bitwise_and(-start, SUBLANES - 1)
            for r in range(SUBLANES - 1):
                @pl.when(r < head)
                def _():
                    fn(pltpu.make_async_copy(zbuf.at[pl.ds(0, 1)], dst_hbm.at[pl.ds(start + r, 1)],
                                             pad_sem))
            body = length - head
            bit = rc // 2
            while bit >= SUBLANES:
                done = body - jnp.bitwise_and(body, 2 * bit - 1)
                first = pl.multiple_of(start + head + done, SUBLANES)

                @pl.when(jnp.bitwise_and(body, bit) != 0)
                def _():
                    fn(pltpu.make_async_copy(zbuf.at[pl.ds(0, bit)], dst_hbm.at[pl.ds(first, bit)],
                                             pad_sem))
                bit //= 2
        tail = pl.multiple_of(seg_ref[N_CLASSES], rc)

        def tail_tile(k, carry):
            fn(pltpu.make_async_copy(zbuf, dst_hbm.at[pl.ds(tail + k * rc, rc)], pad_sem))
            return carry
        lax.fori_loop(0, len_ref[N_CLASSES] // rc, tail_tile, 0)

    @pl.when(c == 0)
    def _():
        for cp in read(0, 0):
            cp.start()
        if scatter:
            zbuf[...] = jnp.zeros_like(zbuf)
            pad_fills(lambda cp: cp.start())

    @pl.when(c + 1 < n)
    def _():
        @pl.when(c >= 2)
        def _():
            write_done(nxt).wait()
        for cp in read(c + 1, nxt):
            cp.start()

    read_done(slot).wait()
    for cp in write(c, slot):
        cp.start()

    @pl.when(c == n - 1)
    def _():
        write_done(nxt).wait()
        write_done(lax.rem(c + 2, MOVE_SLOTS)).wait()
        write_done(slot).wait()
        if scatter:
            pad_fills(lambda cp: cp.wait())


def _move_rows(idx, seg_start, seg_len, src, n_out, scatter):
    any_spec = pl.BlockSpec(memory_space=pl.ANY)
    grid_spec = pltpu.PrefetchScalarGridSpec(
        num_scalar_prefetch=3, grid=(TOKENS // ROW_CHUNK,), in_specs=[any_spec], out_specs=any_spec,
        scratch_shapes=[pltpu.VMEM((MOVE_SLOTS, ROW_CHUNK, D_MODEL), F32),
                        pltpu.VMEM((ROW_CHUNK, D_MODEL), F32),
                        pltpu.SemaphoreType.DMA((MOVE_SLOTS,)),
                        pltpu.SemaphoreType.DMA((MOVE_SLOTS,)),
                        pltpu.SemaphoreType.DMA(())])
    return pl.pallas_call(
        functools.partial(_move_rows_kernel, scatter=scatter),
        grid_spec=grid_spec,
        out_shape=jax.ShapeDtypeStruct((n_out, D_MODEL), F32),
        compiler_params=_cparams(("arbitrary",)),
        name="dispatch" if scatter else "collect",
    )(idx, seg_start, seg_len, src)


def _experts_kernel(ea_ref, eb_ref, tv_ref, tb_ref, x_ref, g_ref, wr_ref, br_ref, wga_ref, wua_ref,
                    wda_ref, wgb_ref, wub_ref, wdb_ref, fg_ref, o_ref, *, layer, final_norm):
    j = pl.program_id(0)
    nv = tv_ref[j]

    @pl.when(nv > 0)
    def _():
        x = x_ref[...]
        t = _rms(x, g_ref[...])
        logits = _router_logits(t, wr_ref, br_ref, discrete=False)
        lane = lax.broadcasted_iota(jnp.int32, logits.shape, 1)
        ea = ea_ref[j] - layer * N_EXPERTS
        eb = eb_ref[j] - layer * N_EXPERTS
        la = jnp.sum(jnp.where(lane == N_GROUPS + ea, logits, 0.0), axis=1, keepdims=True)
        lb = jnp.sum(jnp.where(lane == N_GROUPS + eb, logits, 0.0), axis=1, keepdims=True)
        top = jnp.maximum(la, lb)
        pa, pb = jnp.exp(la - top), jnp.exp(lb - top)
        scale = _group_weight(logits, jnp.right_shift(ea, 2)) / (pa + pb)
        tb = t.astype(BF16)
        acc = jnp.zeros((MOE_TM, D_MODEL), F32)
        for wg, wu, wd, w in ((wga_ref, wua_ref, wda_ref, pa * scale),
                              (wgb_ref, wub_ref, wdb_ref, pb * scale)):
            hid = (jax.nn.silu(jnp.dot(tb, wg[0], preferred_element_type=F32))
                   * jnp.dot(tb, wu[0], preferred_element_type=F32)) * w
            acc = acc + jnp.dot(hid.astype(BF16), wd[0], preferred_element_type=F32)
        h = x + acc
        if final_norm:
            h = _rms(h, fg_ref[...])
        o_ref[...] = h

    @pl.when(nv == 0)
    def _():
        o_ref[...] = jnp.zeros_like(o_ref)


def _experts(tile_ea, tile_eb, tile_valid, tile_block, hs, g, wr, br, wg, wu, wd, fg, layer,
             final_norm):
    const2 = lambda j, *_: (0, 0)
    of_a = lambda j, ea, eb, tv, tb: (ea[j], 0, 0)
    of_b = lambda j, ea, eb, tv, tb: (eb[j], 0, 0)
    up = lambda idx: pl.BlockSpec((1, D_MODEL, D_EXPERT), idx)
    down = lambda idx: pl.BlockSpec((1, D_EXPERT, D_MODEL), idx)
    grid_spec = pltpu.PrefetchScalarGridSpec(
        num_scalar_prefetch=4,
        grid=(MOE_NT,),
        in_specs=[pl.BlockSpec((MOE_TM, D_MODEL), lambda j, ea, eb, tv, tb: (tb[j], 0)),
                  pl.BlockSpec((1, D_MODEL), const2),
                  pl.BlockSpec((D_MODEL, ROUTER_PAD), const2),
                  pl.BlockSpec((1, ROUTER_PAD), const2),
                  up(of_a), up(of_a), down(of_a), up(of_b), up(of_b), down(of_b),
                  pl.BlockSpec((1, D_MODEL), const2)],
        out_specs=pl.BlockSpec((MOE_TM, D_MODEL), lambda j, *_: (j, 0)))
    return pl.pallas_call(
        functools.partial(_experts_kernel, layer=layer, final_norm=final_norm),
        grid_spec=grid_spec,
        out_shape=jax.ShapeDtypeStruct((MOE_ROWS, D_MODEL), F32),
        compiler_params=_cparams(("arbitrary",)),
        name="experts",
    )(tile_ea, tile_eb, tile_valid, tile_block, hs, g, wr, br, wg, wu, wd, wg, wu, wd, fg)


_PAIR_LO = np.array([a for a in range(EXPERTS_PER_GROUP) for b in range(a + 1, EXPERTS_PER_GROUP)])
_PAIR_HI = np.array([b for a in range(EXPERTS_PER_GROUP) for b in range(a + 1, EXPERTS_PER_GROUP)])


def _moe(x, g, wr, br, wg, wu, wd, fg, layer, final_norm):
    meta, cnt = _route(x, g, wr, br)
    cls, rank = meta[0], meta[1]
    counts = cnt[:N_CLASSES, 0].astype(jnp.int32)
    padded = (counts + MOE_TM - 1) // MOE_TM * MOE_TM
    ends = jnp.cumsum(padded)
    starts = ends - padded
    pos = starts[cls] + rank
    tile_start = jnp.arange(MOE_NT, dtype=jnp.int32) * MOE_TM
    tile_cls = jnp.minimum(jnp.sum(tile_start[:, None] >= ends[None, :], axis=1), N_CLASSES - 1)
    tile_valid = jnp.clip(starts[tile_cls] + counts[tile_cls] - tile_start, 0, MOE_TM)
    tile_valid = tile_valid.astype(jnp.int32)
    tile_block = jnp.where(tile_valid > 0, jnp.arange(MOE_NT, dtype=jnp.int32), 0)
    first = layer * N_EXPERTS + (tile_cls // N_PAIRS) * EXPERTS_PER_GROUP
    tile_ea = (first + jnp.asarray(_PAIR_LO)[tile_cls % N_PAIRS]).astype(jnp.int32)
    tile_eb = (first + jnp.asarray(_PAIR_HI)[tile_cls % N_PAIRS]).astype(jnp.int32)
    seg_start = jnp.concatenate([starts + counts, ends[-1:]])
    seg_len = jnp.concatenate([padded - counts, MOE_ROWS - ends[-1:]])
    hs = _move_rows(pos, seg_start, seg_len, x, MOE_ROWS, scatter=True)
    ys = _experts(tile_ea, tile_eb, tile_valid, tile_block, hs, g, wr, br, wg, wu, wd, fg, layer,
                  final_norm)
    return _move_rows(pos, seg_start, seg_len, ys, TOKENS, scatter=False)


def _block_diag(w):
    eye = jnp.eye(LRU_BLOCKS, dtype=w.dtype)
    return jnp.einsum('hij,hk->hikj', w, eye).reshape(MIX, MIX)


def _w_in_column_ranges():
    cuts = [int(c) for c in np.cumsum((0,) + IN_SPLITS)]
    q_parts = [(cuts[4] + (g * HPG + j) * HEAD_DIM, cuts[4] + (g * HPG + j + 1) * HEAD_DIM)
               for j in range(HPG) for g in range(N_KV)]
    return ([(cuts[8], cuts[9]), (cuts[0], cuts[4])] + q_parts
            + [(cuts[5], cuts[6]), (cuts[7], cuts[8]), (cuts[6], cuts[7])])


PREP_PIECE = 64
PREP_BLOCK = 512


def _prep_piece_starts():
    starts = []
    for a, b in _w_in_column_ranges():
        starts += list(range(a, b, PREP_PIECE))
    starts += [0] * (2 * W_IN_HALF // PREP_PIECE - len(starts))
    return np.asarray(starts, np.int32)


def _prep_w_in_kernel(src_ref, wt_hbm, o_ref, buf, sem):
    layer, j = pl.program_id(0), pl.program_id(1)
    nj = pl.num_programs(1)
    per_block = PREP_BLOCK // PREP_PIECE
    slot = lax.rem(j, 2)

    def pieces(blk, sl):
        return [pltpu.make_async_copy(
            wt_hbm.at[layer, pl.ds(pl.multiple_of(src_ref[blk * per_block + k], SUBLANES), PREP_PIECE)],
            buf.at[sl, pl.ds(k * PREP_PIECE, PREP_PIECE)], sem.at[sl]) for k in range(per_block)]

    @pl.when(j == 0)
    def _():
        for cp in pieces(0, 0):
            cp.start()

    @pl.when(j + 1 < nj)
    def _():
        for cp in pieces(j + 1, 1 - slot):
            cp.start()

    for cp in pieces(j, slot):
        cp.wait()
    o_ref[...] = buf[slot].T.astype(o_ref.dtype)


def _prep_w_in(w_in):
    grid_spec = pltpu.PrefetchScalarGridSpec(
        num_scalar_prefetch=1,
        grid=(DEPTH, 2 * W_IN_HALF // PREP_BLOCK),
        in_specs=[pl.BlockSpec(memory_space=pl.ANY)],
        out_specs=pl.BlockSpec((None, D_MODEL, PREP_BLOCK), lambda l, j, src: (l, 0, j)),
        scratch_shapes=[pltpu.VMEM((2, PREP_BLOCK, D_MODEL), F32), pltpu.SemaphoreType.DMA((2,))])
    return pl.pallas_call(
        _prep_w_in_kernel,
        grid_spec=grid_spec,
        out_shape=jax.ShapeDtypeStruct((DEPTH, D_MODEL, 2 * W_IN_HALF), BF16),
        compiler_params=_cparams(("arbitrary", "arbitrary")),
        name="prep_w_in",
    )(jnp.asarray(_prep_piece_starts()), w_in.transpose(0, 2, 1))


def _nsa_mixer(q, kv, ng, p):
    w1 = p['cmp_w1'].reshape(2, N_KV, CMP_LEN, HEAD_DIM, HEAD_DIM).transpose(0, 2, 1, 3, 4)
    pe = jnp.concatenate([p['cmp_pe']] * N_KV, axis=-1)
    kc, vct, vst, vwt, gt = _kvprep(kv, ng, pe, _group_diag(w1).astype(BF16),
                                    _group_diag(p['cmp_w2']).astype(BF16))
    return _nsa(q, kc, vct, kv, vst, vwt, gt, _overlap_t())


def _layer(h, p, big, layer, final_g, final_norm):
    row = lambda a: a.reshape(1, -1)

    u, v, gb, rb, q, kv, xd, ng = _proj(h, row(p['norm1_g']), big['w_in'], layer)

    bs = jnp.broadcast_to(p['gm_b'][:, :, None], (GM_GROUPS, GM_CHUNK, GM_GW))
    y_a, y_b, y_d = _mixers(
        u, v, gb, rb, xd, row(p['gm_norm_g']), p['gm_ws'], bs,
        p['conv_w'], row(p['conv_b']), _block_diag(p['lru_wa']).astype(BF16), row(p['lru_ba']),
        _block_diag(p['lru_wx']).astype(BF16), row(p['lru_bx']), row(p['lru_lambda']),
        p['pool_w'].astype(BF16), row(p['pool_scale']))

    y_ct = _nsa_mixer(q, kv, ng, p)

    h = _merge(h, row(p['norm1_g']), big['w_in'], y_a, y_b, y_ct, y_d, big['w_branch'],
               big['w_out'], layer)

    wr = jnp.concatenate([p['router_w_group'], p['router_w_expert']], axis=1)
    wr = jnp.pad(wr, ((0, 0), (0, ROUTER_PAD - wr.shape[1])))
    br = jnp.concatenate([p['router_b_group'], p['router_b_expert']])
    br = jnp.pad(br, (0, ROUTER_PAD - br.shape[0])).reshape(1, ROUTER_PAD)
    return _moe(h, row(p['norm2_g']), wr, br, big['moe_w_gate'], big['moe_w_up'],
                big['moe_w_down'], row(final_g), layer, final_norm)


_LAYER_PARAMS = ('norm1_g', 'gm_norm_g', 'gm_ws', 'gm_b', 'conv_w', 'conv_b', 'lru_wa',
                 'lru_ba', 'lru_wx', 'lru_bx', 'lru_lambda', 'cmp_pe', 'cmp_w1', 'cmp_w2', 'pool_w',
                 'pool_scale', 'norm2_g', 'router_w_group', 'router_b_group',
                 'router_w_expert', 'router_b_expert')


def kernel(x, norm1_g, w_in, gm_norm_g, gm_ws, gm_b, conv_w, conv_b, lru_wa, lru_ba, lru_wx,
           lru_bx, lru_lambda, cmp_pe, cmp_w1, cmp_w2, pool_w, pool_scale, w_branch, w_out,
           norm2_g, router_w_group, router_b_group, router_w_expert, router_b_expert,
           moe_w_gate, moe_w_up, moe_w_down, final_norm_g):
    stacked = dict(zip(_LAYER_PARAMS, (
        norm1_g, gm_norm_g, gm_ws, gm_b, conv_w, conv_b, lru_wa, lru_ba, lru_wx, lru_bx,
        lru_lambda, cmp_pe, cmp_w1, cmp_w2, pool_w, pool_scale, norm2_g,
        router_w_group, router_b_group, router_w_expert, router_b_expert)))
    w_all = _prep_w_in(w_in)
    experts = lambda w: w.astype(BF16).reshape((DEPTH * N_EXPERTS,) + w.shape[2:])
    big = dict(w_in=w_all, w_branch=w_branch.astype(BF16), w_out=w_out.astype(BF16),
               moe_w_gate=experts(moe_w_gate), moe_w_up=experts(moe_w_up),
               moe_w_down=experts(moe_w_down))
    h = x.reshape(TOKENS, D_MODEL)
    for layer in range(DEPTH):
        p = {k: a[layer] for k, a in stacked.items()}
        h = _layer(h, p, big, layer, final_norm_g, final_norm=(layer == DEPTH - 1))
    return h.reshape(BATCH, SEQ, D_MODEL)
```
